```python
import math
import jax
import jax.numpy as jnp
from jax import lax
import numpy as np

D_MODEL = 1024
BATCH = 16
SEQ = 2048
DEPTH = 2

MIX_DIM = D_MODEL
HEAD_DIM = 64
RWKV_DIM = MIX_DIM // 2
RWKV_HEADS = RWKV_DIM // HEAD_DIM
RWKV_DECAY_LORA = 64
RWKV_A_LORA = 64
RWKV_GATE_LORA = 128
RWKV_GN_EPS = 64e-5
RWKV_COLS = 3 * RWKV_DIM + RWKV_DECAY_LORA + RWKV_A_LORA + RWKV_GATE_LORA
SSM_DIM = MIX_DIM // 2
SSM_HEAD_DIM = 64
SSM_HEADS = SSM_DIM // SSM_HEAD_DIM
SSM_GROUPS = 2
SSM_STATE = 128
SSM_CONV = 4
SSM_CHUNK = 128
SSM_XBC = SSM_DIM + 2 * SSM_GROUPS * SSM_STATE
SSM_COLS = SSM_DIM + SSM_XBC + SSM_HEADS
L0_COLS = RWKV_COLS + SSM_COLS
SB_DIM = MIX_DIM // 2
SB_HEADS = SB_DIM // HEAD_DIM
MLA_NOPE = 64
MLA_ROPE = 32
MLA_V = 64
MLA_HEADS = (MIX_DIM // 2) // MLA_V
MLA_Q_LORA = 256
MLA_KV_LORA = 128
ROPE_THETA = 10000.0
L1_COLS = 3 * SB_DIM + MLA_Q_LORA + MLA_KV_LORA + MLA_ROPE
Q_BLOCK = 128
D_FF = 2816
FFN_CONV = 3
ALPHA = (2 * DEPTH) ** 0.25
BETA = (8 * DEPTH) ** -0.25

kernel_name = 'hybrid_rwkv7_ssd_stickbreak_mla_convffn'


def _layer_norm(x, g, b, eps=1e-5):
    xf = x.astype(jnp.float32)
    mu = jnp.mean(xf, axis=-1, keepdims=True)
    var = jnp.mean(jnp.square(xf - mu), axis=-1, keepdims=True)
    return ((xf - mu) * lax.rsqrt(var + eps) * g + b).astype(x.dtype)


def _rms_norm(x, g, eps=1e-6):
    xf = x.astype(jnp.float32)
    return (xf * lax.rsqrt(jnp.mean(xf * xf, axis=-1, keepdims=True) + eps) * g).astype(x.dtype)


def _causal_dwconv(u, w, b):
    K = w.shape[0]
    T = u.shape[1]
    up = jnp.pad(u, ((0, 0), (K - 1, 0), (0, 0)))
    y = b + up[:, 0:T] * w[0]
    for i in range(1, K):
        y = y + up[:, i:i + T] * w[i]
    return y


def _to_heads(t, n):
    return t.reshape(t.shape[0], t.shape[1], n, -1)


def _rwkv7_scan(r, w, k, v, a, b):
    bsz, T, H, N = r.shape

    def step(S, inp):
        r_t, w_t, k_t, v_t, a_t, b_t = inp
        sa = jnp.einsum('bhij,bhj->bhi', S, a_t)
        S = S * w_t[:, :, None, :] + sa[..., None] * b_t[:, :, None, :] + v_t[..., None] * k_t[:, :, None, :]
        return S, jnp.einsum('bhij,bhj->bhi', S, r_t)

    xs = tuple(jnp.swapaxes(t, 0, 1) for t in (r, w, k, v, a, b))
    _, y = lax.scan(step, jnp.zeros((bsz, H, N, N), jnp.float32), xs)
    return jnp.swapaxes(y, 0, 1)


def _rwkv7_group(p, mix, w0, w2, a0, a2, g2, k_k, k_a, r_k, ln_g, ln_b):
    bsz, T, _ = p.shape
    prev = jnp.pad(p, ((0, 0), (1, 0), (0, 0)))[:, :-1]
    p = p + (prev - p) * mix
    cuts = [RWKV_DIM, 2 * RWKV_DIM, 3 * RWKV_DIM, 3 * RWKV_DIM + RWKV_DECAY_LORA,
            3 * RWKV_DIM + RWKV_DECAY_LORA + RWKV_A_LORA]
    r, k, v, w_lo, a_lo, g_lo = jnp.split(p, cuts, axis=-1)
    log_w = -jax.nn.softplus(-(w0 + jnp.tanh(w_lo) @ w2)) - 0.5
    decay = jnp.exp(-jnp.exp(log_w.astype(jnp.float32)))
    a = jax.nn.sigmoid(a0 + a_lo @ a2)
    g = jax.nn.sigmoid(g_lo) @ g2
    kk = _to_heads(k * k_k, RWKV_HEADS).astype(jnp.float32)
    kk = kk / jnp.maximum(jnp.sqrt(jnp.sum(kk * kk, axis=-1, keepdims=True)), 1e-12)
    k = k * (1 + (a - 1) * k_a)
    r_h, k_h, v_h, a_h, w_h = [_to_heads(t, RWKV_HEADS).astype(jnp.float32) for t in (r, k, v, a, decay)]
    y = _rwkv7_scan(r_h, w_h, k_h, v_h, -kk, kk * a_h)
    mu = jnp.mean(y, axis=-1, keepdims=True)
    var = jnp.mean(jnp.square(y - mu), axis=-1, keepdims=True)
    y = ((y - mu) * lax.rsqrt(var + RWKV_GN_EPS)).reshape(bsz, T, RWKV_DIM) * ln_g + ln_b
    bonus = jnp.sum(r_h * k_h * r_k, axis=-1, keepdims=True) * v_h
    return ((y + bonus.reshape(bsz, T, RWKV_DIM)) * g).astype(p.dtype)


def _segsum(a):
    L = a.shape[-1]
    ar = jnp.broadcast_to(a[..., :, None], a.shape + (L,))
    strict = jnp.tril(jnp.ones((L, L), bool), -1)
    s = jnp.cumsum(jnp.where(strict, ar, 0.0), axis=-2)
    return jnp.where(jnp.tril(jnp.ones((L, L), bool)), s, -jnp.inf)


def _ssd_chunked(xs, dt, A, Bm, Cm):
    bsz, T, H, P = xs.shape
    G, N = Bm.shape[2], Bm.shape[3]
    J = H // G
    c, l = T // SSM_CHUNK, SSM_CHUNK
    X = (xs * dt[..., None]).reshape(bsz, c, l, G, J, P)
    a_dt = (dt * A).reshape(bsz, c, l, G, J).transpose(0, 3, 4, 1, 2)
    Bc = Bm.reshape(bsz, c, l, G, N)
    Cc = Cm.reshape(bsz, c, l, G, N)
    a_cum = jnp.cumsum(a_dt, axis=-1)
    decay_in = jnp.exp(_segsum(a_dt))
    cb = jnp.einsum('bclgn,bcsgn->bgcls', Cc, Bc)
    y_diag = jnp.einsum('bgjcls,bcsgjp->bclgjp', cb[:, :, None] * decay_in, X)
    decay_to_end = jnp.exp(a_cum[..., -1:] - a_cum)
    states = jnp.einsum('bclgn,bgjcl,bclgjp->bcgjpn', Bc, decay_to_end, X)
    states = jnp.concatenate([jnp.zeros_like(states[:, :1]), states], axis=1)
    chunk_decay = jnp.exp(_segsum(jnp.pad(a_cum[..., -1], ((0, 0), (0, 0), (0, 0), (1, 0)))))
    states = jnp.einsum('bgjzc,bcgjpn->bzgjpn', chunk_decay, states)[:, :-1]
    y_off = jnp.einsum('bclgn,bcgjpn,bgjcl->bclgjp', Cc, states, jnp.exp(a_cum))
    return (y_diag + y_off).reshape(bsz, T, H, P)


def _mamba2_group(p, conv_w, conv_b, dt_bias, a_log, d_skip, norm_g):
    bsz, T, _ = p.shape
    z, xbc, dt_raw = jnp.split(p, [SSM_DIM, SSM_DIM + SSM_XBC], axis=-1)
    xbc = jax.nn.silu(_causal_dwconv(xbc, conv_w, conv_b))
    xs, Bm, Cm = jnp.split(xbc, [SSM_DIM, SSM_DIM + SSM_GROUPS * SSM_STATE], axis=-1)
    xs = xs.reshape(bsz, T, SSM_HEADS, SSM_HEAD_DIM).astype(jnp.float32)
    Bm = Bm.reshape(bsz, T, SSM_GROUPS, SSM_STATE).astype(jnp.float32)
    Cm = Cm.reshape(bsz, T, SSM_GROUPS, SSM_STATE).astype(jnp.float32)
    dt = jax.nn.softplus((dt_raw + dt_bias).astype(jnp.float32))
    A = -jnp.exp(a_log.astype(jnp.float32))
    y = _ssd_chunked(xs, dt, A, Bm, Cm) + xs * d_skip[:, None]
    u = (y.reshape(bsz, T, SSM_DIM) * jax.nn.silu(z.astype(jnp.float32))).reshape(bsz, T, SSM_GROUPS, -1)
    u = u * lax.rsqrt(jnp.mean(u * u, axis=-1, keepdims=True) + 1e-5)
    return (u.reshape(bsz, T, SSM_DIM) * norm_g).astype(p.dtype)


def _mixer_rwkv_ssd(h, w_in, mix, w0, w2, a0, a2, g2, k_k, k_a, r_k, ln_g, ln_b,
                    conv_w, conv_b, dt_bias, a_log, d_skip, norm_g, w_out):
    proj = h @ w_in
    y_a = _rwkv7_group(proj[..., :RWKV_COLS], mix, w0, w2, a0, a2, g2, k_k, k_a, r_k, ln_g, ln_b)
    y_b = _mamba2_group(proj[..., RWKV_COLS:], conv_w, conv_b, dt_bias, a_log, d_skip, norm_g)
    return jnp.concatenate([y_a, y_b], axis=-1) @ w_out


def _stick_breaking(q, k, v):
    T = q.shape[2]
    scale = q.shape[-1] ** -0.5
    outs = []
    for start in range(0, T, Q_BLOCK):
        end = start + Q_BLOCK
        z = jnp.einsum('bhqd,bhkd->bhqk', q[:, :, start:end], k[:, :, :end]).astype(jnp.float32) * scale
        strict = jnp.arange(end)[None, :] < jnp.arange(start, end)[:, None]
        log_keep = jnp.where(strict, jax.nn.log_sigmoid(-z), 0.0)
        log_att = jax.nn.log_sigmoid(z) + lax.cumsum(log_keep, axis=3, reverse=True) - log_keep
        att = jnp.where(strict, jnp.exp(log_att), 0.0)
        outs.append(jnp.einsum('bhqk,bhkd->bhqd', att.astype(v.dtype), v[:, :, :end]))
    return jnp.concatenate(outs, axis=2)


def _rope_tables(positions):
    inv_freq = 1.0 / (ROPE_THETA ** (jnp.arange(0, MLA_ROPE, 2, dtype=jnp.float32) / MLA_ROPE))
    ang = positions.astype(jnp.float32)[..., None] * inv_freq
    return jnp.cos(ang), jnp.sin(ang)


def _apply_rope(x, cos, sin):
    half = x.shape[-1] // 2
    x1, x2 = x[..., :half], x[..., half:]
    return jnp.concatenate([x1 * cos - x2 * sin, x2 * cos + x1 * sin], axis=-1)


def _mla_attention(q_nope, q_pe, k_nope, k_pe, v):
    T = q_nope.shape[2]
    scale = (MLA_NOPE + MLA_ROPE) ** -0.5
    outs = []
    for start in range(0, T, Q_BLOCK):
        end = start + Q_BLOCK
        s = (jnp.einsum('bhqd,bhkd->bhqk', q_nope[:, :, start:end], k_nope[:, :, :end])
             + jnp.einsum('bhqd,bkd->bhqk', q_pe[:, :, start:end], k_pe[:, :end])).astype(jnp.float32) * scale
        causal = jnp.arange(end)[None, :] <= jnp.arange(start, end)[:, None]
        prob = jax.nn.softmax(jnp.where(causal, s, -jnp.inf), axis=-1)
        outs.append(jnp.einsum('bhqk,bhkd->bhqd', prob.astype(v.dtype), v[:, :, :end]))
    return jnp.concatenate(outs, axis=2)


def _mixer_sb_mla(h, positions, w_in, q_norm_g, w_uq, kv_norm_g, w_ukv, w_out):
    bsz, T, _ = h.shape
    proj = h @ w_in
    cuts = [SB_DIM, 2 * SB_DIM, 3 * SB_DIM, 3 * SB_DIM + MLA_Q_LORA, 3 * SB_DIM + MLA_Q_LORA + MLA_KV_LORA]
    q_sb, k_sb, v_sb, c_q, c_kv, k_pe = jnp.split(proj, cuts, axis=-1)
    tr = lambda t: t.transpose(0, 2, 1, 3)
    y_c = _stick_breaking(tr(_to_heads(q_sb, SB_HEADS)), tr(_to_heads(k_sb, SB_HEADS)), tr(_to_heads(v_sb, SB_HEADS)))
    y_c = tr(y_c).reshape(bsz, T, SB_DIM)
    q = _to_heads(_rms_norm(c_q, q_norm_g) @ w_uq, MLA_HEADS)
    kv = _to_heads(_rms_norm(c_kv, kv_norm_g) @ w_ukv, MLA_HEADS)
    cos, sin = _rope_tables(positions)
    q_pe = _apply_rope(q[..., MLA_NOPE:], cos[:, :, None], sin[:, :, None])
    k_pe = _apply_rope(k_pe, cos, sin)
    y_d = _mla_attention(tr(q[..., :MLA_NOPE]), tr(q_pe), tr(kv[..., :MLA_NOPE]), k_pe, tr(kv[..., MLA_NOPE:]))
    y_d = tr(y_d).reshape(bsz, T, MLA_HEADS * MLA_V)
    return jnp.concatenate([y_c, y_d.astype(y_c.dtype)], axis=-1) @ w_out


def _conv_ffn(h, w_up, conv_w, conv_b, w_down):
    gate, up = jnp.split(h @ w_up, [D_FF], axis=-1)
    gate = _causal_dwconv(gate, conv_w, conv_b)
    return (jax.nn.silu(gate) * up) @ w_down


def _fwd_setup_inputs(seed: int = 0) -> dict:
    key = jax.random.key(seed)
    ks = iter(jax.random.split(key, 64))

    def nrm(shape, scale):
        return jax.random.normal(next(ks), shape, jnp.float32) * scale

    def uni(shape, lo, hi):
        return jax.random.uniform(next(ks), shape, jnp.float32, lo, hi)

    def gain(n):
        return 1.0 + nrm((n,), 0.02)

    inp = {}
    inp['x'] = nrm((BATCH, SEQ, D_MODEL), 1.0)
    inp['positions'] = (jax.random.randint(next(ks), (BATCH, 1), 0, 4096, dtype=jnp.int32)
                        + jnp.arange(SEQ, dtype=jnp.int32)[None, :])
    inp['l0_w_in'] = nrm((D_MODEL, L0_COLS), D_MODEL ** -0.5)
    inp['rwkv_mix'] = uni((RWKV_COLS,), 0.0, 1.0)
    inp['rwkv_w0'] = uni((RWKV_DIM,), -6.0, -1.0)
    inp['rwkv_w2'] = nrm((RWKV_DECAY_LORA, RWKV_DIM), 0.1)
    inp['rwkv_a0'] = nrm((RWKV_DIM,), 0.1)
    inp['rwkv_a2'] = nrm((RWKV_A_LORA, RWKV_DIM), 0.1)
    inp['rwkv_g2'] = nrm((RWKV_GATE_LORA, RWKV_DIM), RWKV_GATE_LORA ** -0.5)
    inp['rwkv_k_k'] = 0.85 + nrm((RWKV_DIM,), 0.05)
    inp['rwkv_k_a'] = 1.0 + nrm((RWKV_DIM,), 0.05)
    inp['rwkv_r_k'] = nrm((RWKV_HEADS, HEAD_DIM), 0.1)
    inp['rwkv_ln_g'] = gain(RWKV_DIM)
    inp['rwkv_ln_b'] = nrm((RWKV_DIM,), 0.02)
    inp['ssm_conv_w'] = nrm((SSM_CONV, SSM_XBC), 0.5)
    inp['ssm_conv_b'] = nrm((SSM_XBC,), 0.02)
    dt0 = jnp.exp(uni((SSM_HEADS,), math.log(1e-3), math.log(1e-1)))
    inp['ssm_dt_bias'] = dt0 + jnp.log(-jnp.expm1(-dt0))
    inp['ssm_a_log'] = jnp.log(uni((SSM_HEADS,), 1.0, 16.0))
    inp['ssm_d'] = 1.0 + nrm((SSM_HEADS,), 0.1)
    inp['ssm_norm_g'] = gain(SSM_DIM)
    inp['l0_w_out'] = nrm((MIX_DIM, D_MODEL), MIX_DIM ** -0.5 * BETA)
    inp['l0_ln1_g'] = gain(D_MODEL)
    inp['l0_ln1_b'] = nrm((D_MODEL,), 0.02)
    inp['ffn0_w_up'] = nrm((D_MODEL, 2 * D_FF), D_MODEL ** -0.5)
    inp['ffn0_conv_w'] = nrm((FFN_CONV, D_FF), FFN_CONV ** -0.5)
    inp['ffn0_conv_b'] = nrm((D_FF,), 0.02)
    inp['ffn0_w_down'] = nrm((D_FF, D_MODEL), D_FF ** -0.5 * BETA)
    inp['l0_ln2_g'] = gain(D_MODEL)
    inp['l0_ln2_b'] = nrm((D_MODEL,), 0.02)
    inp['l1_w_in'] = nrm((D_MODEL, L1_COLS), D_MODEL ** -0.5)
    inp['mla_q_norm_g'] = gain(MLA_Q_LORA)
    inp['mla_w_uq'] = nrm((MLA_Q_LORA, MLA_HEADS * (MLA_NOPE + MLA_ROPE)), MLA_Q_LORA ** -0.5)
    inp['mla_kv_norm_g'] = gain(MLA_KV_LORA)
    inp['mla_w_ukv'] = nrm((MLA_KV_LORA, MLA_HEADS * (MLA_NOPE + MLA_V)), MLA_KV_LORA ** -0.5)
    inp['l1_w_out'] = nrm((MIX_DIM, D_MODEL), MIX_DIM ** -0.5 * BETA)
    inp['l1_ln1_g'] = gain(D_MODEL)
    inp['l1_ln1_b'] = nrm((D_MODEL,), 0.02)
    inp['ffn1_w_up'] = nrm((D_MODEL, 2 * D_FF), D_MODEL ** -0.5)
    inp['ffn1_conv_w'] = nrm((FFN_CONV, D_FF), FFN_CONV ** -0.5)
    inp['ffn1_conv_b'] = nrm((D_FF,), 0.02)
    inp['ffn1_w_down'] = nrm((D_FF, D_MODEL), D_FF ** -0.5 * BETA)
    inp['l1_ln2_g'] = gain(D_MODEL)
    inp['l1_ln2_b'] = nrm((D_MODEL,), 0.02)
    return inp


def _fwd_reference(x, positions, l0_w_in, rwkv_mix, rwkv_w0, rwkv_w2, rwkv_a0, rwkv_a2, rwkv_g2,
              rwkv_k_k, rwkv_k_a, rwkv_r_k, rwkv_ln_g, rwkv_ln_b, ssm_conv_w, ssm_conv_b,
              ssm_dt_bias, ssm_a_log, ssm_d, ssm_norm_g, l0_w_out, l0_ln1_g, l0_ln1_b,
              ffn0_w_up, ffn0_conv_w, ffn0_conv_b, ffn0_w_down, l0_ln2_g, l0_ln2_b,
              l1_w_in, mla_q_norm_g, mla_w_uq, mla_kv_norm_g, mla_w_ukv, l1_w_out,
              l1_ln1_g, l1_ln1_b, ffn1_w_up, ffn1_conv_w, ffn1_conv_b, ffn1_w_down,
              l1_ln2_g, l1_ln2_b):
    mixers = (_mixer_rwkv_ssd, _mixer_sb_mla)
    mixer_args = (
        (l0_w_in, rwkv_mix, rwkv_w0, rwkv_w2, rwkv_a0, rwkv_a2, rwkv_g2, rwkv_k_k, rwkv_k_a,
         rwkv_r_k, rwkv_ln_g, rwkv_ln_b, ssm_conv_w, ssm_conv_b, ssm_dt_bias, ssm_a_log,
         ssm_d, ssm_norm_g, l0_w_out),
        (positions, l1_w_in, mla_q_norm_g, mla_w_uq, mla_kv_norm_g, mla_w_ukv, l1_w_out),
    )
    ffn_args = ((ffn0_w_up, ffn0_conv_w, ffn0_conv_b, ffn0_w_down),
                (ffn1_w_up, ffn1_conv_w, ffn1_conv_b, ffn1_w_down))
    ln_mix = ((l0_ln1_g, l0_ln1_b), (l1_ln1_g, l1_ln1_b))
    ln_ffn = ((l0_ln2_g, l0_ln2_b), (l1_ln2_g, l1_ln2_b))
    h = x
    for layer in range(DEPTH):
        mixed = mixers[layer % 2](h, *mixer_args[layer])
        h = _layer_norm(ALPHA * h + mixed, *ln_mix[layer])
        h = _layer_norm(ALPHA * h + _conv_ffn(h, *ffn_args[layer]), *ln_ffn[layer])
    return h.astype(x.dtype)


import jax as _jax
import jax.numpy as _jnp

TWIN_FORMAT = 'train_step'
FWD_PARAMS = ['x', 'positions', 'l0_w_in', 'rwkv_mix', 'rwkv_w0', 'rwkv_w2', 'rwkv_a0', 'rwkv_a2', 'rwkv_g2', 'rwkv_k_k', 'rwkv_k_a', 'rwkv_r_k', 'rwkv_ln_g', 'rwkv_ln_b', 'ssm_conv_w', 'ssm_conv_b', 'ssm_dt_bias', 'ssm_a_log', 'ssm_d', 'ssm_norm_g', 'l0_w_out', 'l0_ln1_g', 'l0_ln1_b', 'ffn0_w_up', 'ffn0_conv_w', 'ffn0_conv_b', 'ffn0_w_down', 'l0_ln2_g', 'l0_ln2_b', 'l1_w_in', 'mla_q_norm_g', 'mla_w_uq', 'mla_kv_norm_g', 'mla_w_ukv', 'l1_w_out', 'l1_ln1_g', 'l1_ln1_b', 'ffn1_w_up', 'ffn1_conv_w', 'ffn1_conv_b', 'ffn1_w_down', 'l1_ln2_g', 'l1_ln2_b']
TWIN_WEIGHTS = ['l0_w_in', 'rwkv_mix', 'rwkv_w0', 'rwkv_w2', 'rwkv_a0', 'rwkv_a2', 'rwkv_g2', 'rwkv_k_k', 'rwkv_k_a', 'rwkv_r_k', 'rwkv_ln_g', 'rwkv_ln_b', 'ssm_conv_w', 'ssm_conv_b', 'ssm_dt_bias', 'ssm_a_log', 'ssm_d', 'ssm_norm_g', 'l0_w_out', 'l0_ln1_g', 'l0_ln1_b', 'ffn0_w_up', 'ffn0_conv_w', 'ffn0_conv_b', 'ffn0_w_down', 'l0_ln2_g', 'l0_ln2_b', 'l1_w_in', 'mla_q_norm_g', 'mla_w_uq', 'mla_kv_norm_g', 'mla_w_ukv', 'l1_w_out', 'l1_ln1_g', 'l1_ln1_b', 'ffn1_w_up', 'ffn1_conv_w', 'ffn1_conv_b', 'ffn1_w_down', 'l1_ln2_g', 'l1_ln2_b']
TWIN_DIFF_INPUT = 'x'
TWIN_INPUTS = ['x', 'positions', 'l0_w_in', 'rwkv_mix', 'rwkv_w0', 'rwkv_w2', 'rwkv_a0', 'rwkv_a2', 'rwkv_g2', 'rwkv_k_k', 'rwkv_k_a', 'rwkv_r_k', 'rwkv_ln_g', 'rwkv_ln_b', 'ssm_conv_w', 'ssm_conv_b', 'ssm_dt_bias', 'ssm_a_log', 'ssm_d', 'ssm_norm_g', 'l0_w_out', 'l0_ln1_g', 'l0_ln1_b', 'ffn0_w_up', 'ffn0_conv_w', 'ffn0_conv_b', 'ffn0_w_down', 'l0_ln2_g', 'l0_ln2_b', 'l1_w_in', 'mla_q_norm_g', 'mla_w_uq', 'mla_kv_norm_g', 'mla_w_ukv', 'l1_w_out', 'l1_ln1_g', 'l1_ln1_b', 'ffn1_w_up', 'ffn1_conv_w', 'ffn1_conv_b', 'ffn1_w_down', 'l1_ln2_g', 'l1_ln2_b', 'loss_target', 'm_l0_w_in', 'm_rwkv_mix', 'm_rwkv_w0', 'm_rwkv_w2', 'm_rwkv_a0', 'm_rwkv_a2', 'm_rwkv_g2', 'm_rwkv_k_k', 'm_rwkv_k_a', 'm_rwkv_r_k', 'm_rwkv_ln_g', 'm_rwkv_ln_b', 'm_ssm_conv_w', 'm_ssm_conv_b', 'm_ssm_dt_bias', 'm_ssm_a_log', 'm_ssm_d', 'm_ssm_norm_g', 'm_l0_w_out', 'm_l0_ln1_g', 'm_l0_ln1_b', 'm_ffn0_w_up', 'm_ffn0_conv_w', 'm_ffn0_conv_b', 'm_ffn0_w_down', 'm_l0_ln2_g', 'm_l0_ln2_b', 'm_l1_w_in', 'm_mla_q_norm_g', 'm_mla_w_uq', 'm_mla_kv_norm_g', 'm_mla_w_ukv', 'm_l1_w_out', 'm_l1_ln1_g', 'm_l1_ln1_b', 'm_ffn1_w_up', 'm_ffn1_conv_w', 'm_ffn1_conv_b', 'm_ffn1_w_down', 'm_l1_ln2_g', 'm_l1_ln2_b', 'v_l0_w_in', 'v_rwkv_mix', 'v_rwkv_w0', 'v_rwkv_w2', 'v_rwkv_a0', 'v_rwkv_a2', 'v_rwkv_g2', 'v_rwkv_k_k', 'v_rwkv_k_a', 'v_rwkv_r_k', 'v_rwkv_ln_g', 'v_rwkv_ln_b', 'v_ssm_conv_w', 'v_ssm_conv_b', 'v_ssm_dt_bias', 'v_ssm_a_log', 'v_ssm_d', 'v_ssm_norm_g', 'v_l0_w_out', 'v_l0_ln1_g', 'v_l0_ln1_b', 'v_ffn0_w_up', 'v_ffn0_conv_w', 'v_ffn0_conv_b', 'v_ffn0_w_down', 'v_l0_ln2_g', 'v_l0_ln2_b', 'v_l1_w_in', 'v_mla_q_norm_g', 'v_mla_w_uq', 'v_mla_kv_norm_g', 'v_mla_w_ukv', 'v_l1_w_out', 'v_l1_ln1_g', 'v_l1_ln1_b', 'v_ffn1_w_up', 'v_ffn1_conv_w', 'v_ffn1_conv_b', 'v_ffn1_w_down', 'v_l1_ln2_g', 'v_l1_ln2_b']
TWIN_OUTPUTS = ['loss', 'grad_x', 'grad_l0_w_in', 'grad_rwkv_mix', 'grad_rwkv_w0', 'grad_rwkv_w2', 'grad_rwkv_a0', 'grad_rwkv_a2', 'grad_rwkv_g2', 'grad_rwkv_k_k', 'grad_rwkv_k_a', 'grad_rwkv_r_k', 'grad_rwkv_ln_g', 'grad_rwkv_ln_b', 'grad_ssm_conv_w', 'grad_ssm_conv_b', 'grad_ssm_dt_bias', 'grad_ssm_a_log', 'grad_ssm_d', 'grad_ssm_norm_g', 'grad_l0_w_out', 'grad_l0_ln1_g', 'grad_l0_ln1_b', 'grad_ffn0_w_up', 'grad_ffn0_conv_w', 'grad_ffn0_conv_b', 'grad_ffn0_w_down', 'grad_l0_ln2_g', 'grad_l0_ln2_b', 'grad_l1_w_in', 'grad_mla_q_norm_g', 'grad_mla_w_uq', 'grad_mla_kv_norm_g', 'grad_mla_w_ukv', 'grad_l1_w_out', 'grad_l1_ln1_g', 'grad_l1_ln1_b', 'grad_ffn1_w_up', 'grad_ffn1_conv_w', 'grad_ffn1_conv_b', 'grad_ffn1_w_down', 'grad_l1_ln2_g', 'grad_l1_ln2_b', 'delta_l0_w_in', 'delta_rwkv_mix', 'delta_rwkv_w0', 'delta_rwkv_w2', 'delta_rwkv_a0', 'delta_rwkv_a2', 'delta_rwkv_g2', 'delta_rwkv_k_k', 'delta_rwkv_k_a', 'delta_rwkv_r_k', 'delta_rwkv_ln_g', 'delta_rwkv_ln_b', 'delta_ssm_conv_w', 'delta_ssm_conv_b', 'delta_ssm_dt_bias', 'delta_ssm_a_log', 'delta_ssm_d', 'delta_ssm_norm_g', 'delta_l0_w_out', 'delta_l0_ln1_g', 'delta_l0_ln1_b', 'delta_ffn0_w_up', 'delta_ffn0_conv_w', 'delta_ffn0_conv_b', 'delta_ffn0_w_down', 'delta_l0_ln2_g', 'delta_l0_ln2_b', 'delta_l1_w_in', 'delta_mla_q_norm_g', 'delta_mla_w_uq', 'delta_mla_kv_norm_g', 'delta_mla_w_ukv', 'delta_l1_w_out', 'delta_l1_ln1_g', 'delta_l1_ln1_b', 'delta_ffn1_w_up', 'delta_ffn1_conv_w', 'delta_ffn1_conv_b', 'delta_ffn1_w_down', 'delta_l1_ln2_g', 'delta_l1_ln2_b', 'new_m_l0_w_in', 'new_m_rwkv_mix', 'new_m_rwkv_w0', 'new_m_rwkv_w2', 'new_m_rwkv_a0', 'new_m_rwkv_a2', 'new_m_rwkv_g2', 'new_m_rwkv_k_k', 'new_m_rwkv_k_a', 'new_m_rwkv_r_k', 'new_m_rwkv_ln_g', 'new_m_rwkv_ln_b', 'new_m_ssm_conv_w', 'new_m_ssm_conv_b', 'new_m_ssm_dt_bias', 'new_m_ssm_a_log', 'new_m_ssm_d', 'new_m_ssm_norm_g', 'new_m_l0_w_out', 'new_m_l0_ln1_g', 'new_m_l0_ln1_b', 'new_m_ffn0_w_up', 'new_m_ffn0_conv_w', 'new_m_ffn0_conv_b', 'new_m_ffn0_w_down', 'new_m_l0_ln2_g', 'new_m_l0_ln2_b', 'new_m_l1_w_in', 'new_m_mla_q_norm_g', 'new_m_mla_w_uq', 'new_m_mla_kv_norm_g', 'new_m_mla_w_ukv', 'new_m_l1_w_out', 'new_m_l1_ln1_g', 'new_m_l1_ln1_b', 'new_m_ffn1_w_up', 'new_m_ffn1_conv_w', 'new_m_ffn1_conv_b', 'new_m_ffn1_w_down', 'new_m_l1_ln2_g', 'new_m_l1_ln2_b', 'new_v_l0_w_in', 'new_v_rwkv_mix', 'new_v_rwkv_w0', 'new_v_rwkv_w2', 'new_v_rwkv_a0', 'new_v_rwkv_a2', 'new_v_rwkv_g2', 'new_v_rwkv_k_k', 'new_v_rwkv_k_a', 'new_v_rwkv_r_k', 'new_v_rwkv_ln_g', 'new_v_rwkv_ln_b', 'new_v_ssm_conv_w', 'new_v_ssm_conv_b', 'new_v_ssm_dt_bias', 'new_v_ssm_a_log', 'new_v_ssm_d', 'new_v_ssm_norm_g', 'new_v_l0_w_out', 'new_v_l0_ln1_g', 'new_v_l0_ln1_b', 'new_v_ffn0_w_up', 'new_v_ffn0_conv_w', 'new_v_ffn0_conv_b', 'new_v_ffn0_w_down', 'new_v_l0_ln2_g', 'new_v_l0_ln2_b', 'new_v_l1_w_in', 'new_v_mla_q_norm_g', 'new_v_mla_w_uq', 'new_v_mla_kv_norm_g', 'new_v_mla_w_ukv', 'new_v_l1_w_out', 'new_v_l1_ln1_g', 'new_v_l1_ln1_b', 'new_v_ffn1_w_up', 'new_v_ffn1_conv_w', 'new_v_ffn1_conv_b', 'new_v_ffn1_w_down', 'new_v_l1_ln2_g', 'new_v_l1_ln2_b']
TWIN_LEAF_KINDS = {'loss': 'loss', 'grad_x': 'grad_x', 'grad_l0_w_in': 'grad_w', 'grad_rwkv_mix': 'grad_w', 'grad_rwkv_w0': 'grad_w', 'grad_rwkv_w2': 'grad_w', 'grad_rwkv_a0': 'grad_w', 'grad_rwkv_a2': 'grad_w', 'grad_rwkv_g2': 'grad_w', 'grad_rwkv_k_k': 'grad_w', 'grad_rwkv_k_a': 'grad_w', 'grad_rwkv_r_k': 'grad_w', 'grad_rwkv_ln_g': 'grad_w', 'grad_rwkv_ln_b': 'grad_w', 'grad_ssm_conv_w': 'grad_w', 'grad_ssm_conv_b': 'grad_w', 'grad_ssm_dt_bias': 'grad_w', 'grad_ssm_a_log': 'grad_w', 'grad_ssm_d': 'grad_w', 'grad_ssm_norm_g': 'grad_w', 'grad_l0_w_out': 'grad_w', 'grad_l0_ln1_g': 'grad_w', 'grad_l0_ln1_b': 'grad_w', 'grad_ffn0_w_up': 'grad_w', 'grad_ffn0_conv_w': 'grad_w', 'grad_ffn0_conv_b': 'grad_w', 'grad_ffn0_w_down': 'grad_w', 'grad_l0_ln2_g': 'grad_w', 'grad_l0_ln2_b': 'grad_w', 'grad_l1_w_in': 'grad_w', 'grad_mla_q_norm_g': 'grad_w', 'grad_mla_w_uq': 'grad_w', 'grad_mla_kv_norm_g': 'grad_w', 'grad_mla_w_ukv': 'grad_w', 'grad_l1_w_out': 'grad_w', 'grad_l1_ln1_g': 'grad_w', 'grad_l1_ln1_b': 'grad_w', 'grad_ffn1_w_up': 'grad_w', 'grad_ffn1_conv_w': 'grad_w', 'grad_ffn1_conv_b': 'grad_w', 'grad_ffn1_w_down': 'grad_w', 'grad_l1_ln2_g': 'grad_w', 'grad_l1_ln2_b': 'grad_w', 'delta_l0_w_in': 'delta_w', 'delta_rwkv_mix': 'delta_w', 'delta_rwkv_w0': 'delta_w', 'delta_rwkv_w2': 'delta_w', 'delta_rwkv_a0': 'delta_w', 'delta_rwkv_a2': 'delta_w', 'delta_rwkv_g2': 'delta_w', 'delta_rwkv_k_k': 'delta_w', 'delta_rwkv_k_a': 'delta_w', 'delta_rwkv_r_k': 'delta_w', 'delta_rwkv_ln_g': 'delta_w', 'delta_rwkv_ln_b': 'delta_w', 'delta_ssm_conv_w': 'delta_w', 'delta_ssm_conv_b': 'delta_w', 'delta_ssm_dt_bias': 'delta_w', 'delta_ssm_a_log': 'delta_w', 'delta_ssm_d': 'delta_w', 'delta_ssm_norm_g': 'delta_w', 'delta_l0_w_out': 'delta_w', 'delta_l0_ln1_g': 'delta_w', 'delta_l0_ln1_b': 'delta_w', 'delta_ffn0_w_up': 'delta_w', 'delta_ffn0_conv_w': 'delta_w', 'delta_ffn0_conv_b': 'delta_w', 'delta_ffn0_w_down': 'delta_w', 'delta_l0_ln2_g': 'delta_w', 'delta_l0_ln2_b': 'delta_w', 'delta_l1_w_in': 'delta_w', 'delta_mla_q_norm_g': 'delta_w', 'delta_mla_w_uq': 'delta_w', 'delta_mla_kv_norm_g': 'delta_w', 'delta_mla_w_ukv': 'delta_w', 'delta_l1_w_out': 'delta_w', 'delta_l1_ln1_g': 'delta_w', 'delta_l1_ln1_b': 'delta_w', 'delta_ffn1_w_up': 'delta_w', 'delta_ffn1_conv_w': 'delta_w', 'delta_ffn1_conv_b': 'delta_w', 'delta_ffn1_w_down': 'delta_w', 'delta_l1_ln2_g': 'delta_w', 'delta_l1_ln2_b': 'delta_w', 'new_m_l0_w_in': 'new_m', 'new_m_rwkv_mix': 'new_m', 'new_m_rwkv_w0': 'new_m', 'new_m_rwkv_w2': 'new_m', 'new_m_rwkv_a0': 'new_m', 'new_m_rwkv_a2': 'new_m', 'new_m_rwkv_g2': 'new_m', 'new_m_rwkv_k_k': 'new_m', 'new_m_rwkv_k_a': 'new_m', 'new_m_rwkv_r_k': 'new_m', 'new_m_rwkv_ln_g': 'new_m', 'new_m_rwkv_ln_b': 'new_m', 'new_m_ssm_conv_w': 'new_m', 'new_m_ssm_conv_b': 'new_m', 'new_m_ssm_dt_bias': 'new_m', 'new_m_ssm_a_log': 'new_m', 'new_m_ssm_d': 'new_m', 'new_m_ssm_norm_g': 'new_m', 'new_m_l0_w_out': 'new_m', 'new_m_l0_ln1_g': 'new_m', 'new_m_l0_ln1_b': 'new_m', 'new_m_ffn0_w_up': 'new_m', 'new_m_ffn0_conv_w': 'new_m', 'new_m_ffn0_conv_b': 'new_m', 'new_m_ffn0_w_down': 'new_m', 'new_m_l0_ln2_g': 'new_m', 'new_m_l0_ln2_b': 'new_m', 'new_m_l1_w_in': 'new_m', 'new_m_mla_q_norm_g': 'new_m', 'new_m_mla_w_uq': 'new_m', 'new_m_mla_kv_norm_g': 'new_m', 'new_m_mla_w_ukv': 'new_m', 'new_m_l1_w_out': 'new_m', 'new_m_l1_ln1_g': 'new_m', 'new_m_l1_ln1_b': 'new_m', 'new_m_ffn1_w_up': 'new_m', 'new_m_ffn1_conv_w': 'new_m', 'new_m_ffn1_conv_b': 'new_m', 'new_m_ffn1_w_down': 'new_m', 'new_m_l1_ln2_g': 'new_m', 'new_m_l1_ln2_b': 'new_m', 'new_v_l0_w_in': 'new_v', 'new_v_rwkv_mix': 'new_v', 'new_v_rwkv_w0': 'new_v', 'new_v_rwkv_w2': 'new_v', 'new_v_rwkv_a0': 'new_v', 'new_v_rwkv_a2': 'new_v', 'new_v_rwkv_g2': 'new_v', 'new_v_rwkv_k_k': 'new_v', 'new_v_rwkv_k_a': 'new_v', 'new_v_rwkv_r_k': 'new_v', 'new_v_rwkv_ln_g': 'new_v', 'new_v_rwkv_ln_b': 'new_v', 'new_v_ssm_conv_w': 'new_v', 'new_v_ssm_conv_b': 'new_v', 'new_v_ssm_dt_bias': 'new_v', 'new_v_ssm_a_log': 'new_v', 'new_v_ssm_d': 'new_v', 'new_v_ssm_norm_g': 'new_v', 'new_v_l0_w_out': 'new_v', 'new_v_l0_ln1_g': 'new_v', 'new_v_l0_ln1_b': 'new_v', 'new_v_ffn0_w_up': 'new_v', 'new_v_ffn0_conv_w': 'new_v', 'new_v_ffn0_conv_b': 'new_v', 'new_v_ffn0_w_down': 'new_v', 'new_v_l0_ln2_g': 'new_v', 'new_v_l0_ln2_b': 'new_v', 'new_v_l1_w_in': 'new_v', 'new_v_mla_q_norm_g': 'new_v', 'new_v_mla_w_uq': 'new_v', 'new_v_mla_kv_norm_g': 'new_v', 'new_v_mla_w_ukv': 'new_v', 'new_v_l1_w_out': 'new_v', 'new_v_l1_ln1_g': 'new_v', 'new_v_l1_ln1_b': 'new_v', 'new_v_ffn1_w_up': 'new_v', 'new_v_ffn1_conv_w': 'new_v', 'new_v_ffn1_conv_b': 'new_v', 'new_v_ffn1_w_down': 'new_v', 'new_v_l1_ln2_g': 'new_v', 'new_v_l1_ln2_b': 'new_v'}


def _forward(args):
    return _fwd_reference(*[args[k] for k in FWD_PARAMS])


def _output_shape():
    out = _jax.eval_shape(lambda: _forward(_fwd_setup_inputs(0)))
    return out.shape, out.dtype

N_MICROBATCH = 1
ADAM_LR = 0.001
ADAM_B1 = 0.9
ADAM_B2 = 0.999
ADAM_EPS = 1e-08
ADAM_WD = 0.01
ADAM_STEP = 10
PER_EXAMPLE_BATCH_AXIS = {'x': 0, 'positions': 0, 'loss_target': 0}
SHARED_INPUTS = []
_WEIGHT_DTYPES = {'l0_w_in': _jnp.float32, 'rwkv_mix': _jnp.float32, 'rwkv_w0': _jnp.float32, 'rwkv_w2': _jnp.float32, 'rwkv_a0': _jnp.float32, 'rwkv_a2': _jnp.float32, 'rwkv_g2': _jnp.float32, 'rwkv_k_k': _jnp.float32, 'rwkv_k_a': _jnp.float32, 'rwkv_r_k': _jnp.float32, 'rwkv_ln_g': _jnp.float32, 'rwkv_ln_b': _jnp.float32, 'ssm_conv_w': _jnp.float32, 'ssm_conv_b': _jnp.float32, 'ssm_dt_bias': _jnp.float32, 'ssm_a_log': _jnp.float32, 'ssm_d': _jnp.float32, 'ssm_norm_g': _jnp.float32, 'l0_w_out': _jnp.float32, 'l0_ln1_g': _jnp.float32, 'l0_ln1_b': _jnp.float32, 'ffn0_w_up': _jnp.float32, 'ffn0_conv_w': _jnp.float32, 'ffn0_conv_b': _jnp.float32, 'ffn0_w_down': _jnp.float32, 'l0_ln2_g': _jnp.float32, 'l0_ln2_b': _jnp.float32, 'l1_w_in': _jnp.float32, 'mla_q_norm_g': _jnp.float32, 'mla_w_uq': _jnp.float32, 'mla_kv_norm_g': _jnp.float32, 'mla_w_ukv': _jnp.float32, 'l1_w_out': _jnp.float32, 'l1_ln1_g': _jnp.float32, 'l1_ln1_b': _jnp.float32, 'ffn1_w_up': _jnp.float32, 'ffn1_conv_w': _jnp.float32, 'ffn1_conv_b': _jnp.float32, 'ffn1_w_down': _jnp.float32, 'l1_ln2_g': _jnp.float32, 'l1_ln2_b': _jnp.float32}
MOMENT_SCALE = {'l0_w_in': 4.523575e-02, 'rwkv_mix': 6.010474e-02, 'rwkv_w0': 1.531939e-02, 'rwkv_w2': 1.690044e-03, 'rwkv_a0': 1.451514e-02, 'rwkv_a2': 1.298113e-02, 'rwkv_g2': 3.436378e-02, 'rwkv_k_k': 5.614035e-02, 'rwkv_k_a': 3.705775e-02, 'rwkv_r_k': 7.648597e-02, 'rwkv_ln_g': 3.727840e-02, 'rwkv_ln_b': 8.494815e-02, 'ssm_conv_w': 4.794944e-02, 'ssm_conv_b': 6.254287e-02, 'ssm_dt_bias': 9.600628e-02, 'ssm_a_log': 1.114825e-01, 'ssm_d': 5.286325e-01, 'ssm_norm_g': 6.647160e-02, 'l0_w_out': 1.057141e-01, 'l0_ln1_g': 9.091782e-01, 'l0_ln1_b': 4.357966e-01, 'ffn0_w_up': 2.391175e-02, 'ffn0_conv_w': 2.420510e-02, 'ffn0_conv_b': 2.340580e-02, 'ffn0_w_down': 7.801777e-02, 'l0_ln2_g': 9.688892e-01, 'l0_ln2_b': 4.394867e-01, 'l1_w_in': 2.711051e-02, 'mla_q_norm_g': 1.724364e-02, 'mla_w_uq': 9.590025e-03, 'mla_kv_norm_g': 4.164793e-02, 'mla_w_ukv': 1.247876e-02, 'l1_w_out': 6.123996e-02, 'l1_ln1_g': 1.007614e+00, 'l1_ln1_b': 4.236351e-01, 'ffn1_w_up': 2.270470e-02, 'ffn1_conv_w': 2.329686e-02, 'ffn1_conv_b': 2.217944e-02, 'ffn1_w_down': 7.433745e-02, 'l1_ln2_g': 3.205926e+01, 'l1_ln2_b': 1.432899e+00}


def _to_microbatches(a, axis):
    t = _jnp.moveaxis(a, axis, 0)
    t = t.reshape((N_MICROBATCH, t.shape[0] // N_MICROBATCH) + t.shape[1:])
    return _jnp.moveaxis(t, 1, axis + 1)


def setup_inputs(seed: int = 0) -> dict:
    inp = _fwd_setup_inputs(seed)
    key = _jax.random.fold_in(_jax.random.key(seed), 7919)
    shape, _ = _output_shape()
    out = dict(inp)
    out["loss_target"] = _jax.random.normal(_jax.random.fold_in(key, 0), shape, _jnp.float32)
    for i, name in enumerate(TWIN_WEIGHTS):
        w = inp[name].astype(_jnp.float32)
        if MOMENT_SCALE is None:
            s = _jnp.sqrt(_jnp.mean(_jnp.square(w)) + 1e-30)
        else:
            s = MOMENT_SCALE[name]
        km, kv = _jax.random.split(_jax.random.fold_in(key, i + 1))
        out[name] = w
        out["m_" + name] = s * _jax.random.normal(km, w.shape, _jnp.float32)
        out["v_" + name] = (s * s) * _jax.random.uniform(kv, w.shape, _jnp.float32, 0.5, 1.5)
    if N_MICROBATCH > 1:
        for name, axis in PER_EXAMPLE_BATCH_AXIS.items():
            out[name] = _to_microbatches(out[name], axis)
    return {'x': out['x'], 'positions': out['positions'], 'l0_w_in': out['l0_w_in'], 'rwkv_mix': out['rwkv_mix'], 'rwkv_w0': out['rwkv_w0'], 'rwkv_w2': out['rwkv_w2'], 'rwkv_a0': out['rwkv_a0'], 'rwkv_a2': out['rwkv_a2'], 'rwkv_g2': out['rwkv_g2'], 'rwkv_k_k': out['rwkv_k_k'], 'rwkv_k_a': out['rwkv_k_a'], 'rwkv_r_k': out['rwkv_r_k'], 'rwkv_ln_g': out['rwkv_ln_g'], 'rwkv_ln_b': out['rwkv_ln_b'], 'ssm_conv_w': out['ssm_conv_w'], 'ssm_conv_b': out['ssm_conv_b'], 'ssm_dt_bias': out['ssm_dt_bias'], 'ssm_a_log': out['ssm_a_log'], 'ssm_d': out['ssm_d'], 'ssm_norm_g': out['ssm_norm_g'], 'l0_w_out': out['l0_w_out'], 'l0_ln1_g': out['l0_ln1_g'], 'l0_ln1_b': out['l0_ln1_b'], 'ffn0_w_up': out['ffn0_w_up'], 'ffn0_conv_w': out['ffn0_conv_w'], 'ffn0_conv_b': out['ffn0_conv_b'], 'ffn0_w_down': out['ffn0_w_down'], 'l0_ln2_g': out['l0_ln2_g'], 'l0_ln2_b': out['l0_ln2_b'], 'l1_w_in': out['l1_w_in'], 'mla_q_norm_g': out['mla_q_norm_g'], 'mla_w_uq': out['mla_w_uq'], 'mla_kv_norm_g': out['mla_kv_norm_g'], 'mla_w_ukv': out['mla_w_ukv'], 'l1_w_out': out['l1_w_out'], 'l1_ln1_g': out['l1_ln1_g'], 'l1_ln1_b': out['l1_ln1_b'], 'ffn1_w_up': out['ffn1_w_up'], 'ffn1_conv_w': out['ffn1_conv_w'], 'ffn1_conv_b': out['ffn1_conv_b'], 'ffn1_w_down': out['ffn1_w_down'], 'l1_ln2_g': out['l1_ln2_g'], 'l1_ln2_b': out['l1_ln2_b'], 'loss_target': out['loss_target'], 'm_l0_w_in': out['m_l0_w_in'], 'm_rwkv_mix': out['m_rwkv_mix'], 'm_rwkv_w0': out['m_rwkv_w0'], 'm_rwkv_w2': out['m_rwkv_w2'], 'm_rwkv_a0': out['m_rwkv_a0'], 'm_rwkv_a2': out['m_rwkv_a2'], 'm_rwkv_g2': out['m_rwkv_g2'], 'm_rwkv_k_k': out['m_rwkv_k_k'], 'm_rwkv_k_a': out['m_rwkv_k_a'], 'm_rwkv_r_k': out['m_rwkv_r_k'], 'm_rwkv_ln_g': out['m_rwkv_ln_g'], 'm_rwkv_ln_b': out['m_rwkv_ln_b'], 'm_ssm_conv_w': out['m_ssm_conv_w'], 'm_ssm_conv_b': out['m_ssm_conv_b'], 'm_ssm_dt_bias': out['m_ssm_dt_bias'], 'm_ssm_a_log': out['m_ssm_a_log'], 'm_ssm_d': out['m_ssm_d'], 'm_ssm_norm_g': out['m_ssm_norm_g'], 'm_l0_w_out': out['m_l0_w_out'], 'm_l0_ln1_g': out['m_l0_ln1_g'], 'm_l0_ln1_b': out['m_l0_ln1_b'], 'm_ffn0_w_up': out['m_ffn0_w_up'], 'm_ffn0_conv_w': out['m_ffn0_conv_w'], 'm_ffn0_conv_b': out['m_ffn0_conv_b'], 'm_ffn0_w_down': out['m_ffn0_w_down'], 'm_l0_ln2_g': out['m_l0_ln2_g'], 'm_l0_ln2_b': out['m_l0_ln2_b'], 'm_l1_w_in': out['m_l1_w_in'], 'm_mla_q_norm_g': out['m_mla_q_norm_g'], 'm_mla_w_uq': out['m_mla_w_uq'], 'm_mla_kv_norm_g': out['m_mla_kv_norm_g'], 'm_mla_w_ukv': out['m_mla_w_ukv'], 'm_l1_w_out': out['m_l1_w_out'], 'm_l1_ln1_g': out['m_l1_ln1_g'], 'm_l1_ln1_b': out['m_l1_ln1_b'], 'm_ffn1_w_up': out['m_ffn1_w_up'], 'm_ffn1_conv_w': out['m_ffn1_conv_w'], 'm_ffn1_conv_b': out['m_ffn1_conv_b'], 'm_ffn1_w_down': out['m_ffn1_w_down'], 'm_l1_ln2_g': out['m_l1_ln2_g'], 'm_l1_ln2_b': out['m_l1_ln2_b'], 'v_l0_w_in': out['v_l0_w_in'], 'v_rwkv_mix': out['v_rwkv_mix'], 'v_rwkv_w0': out['v_rwkv_w0'], 'v_rwkv_w2': out['v_rwkv_w2'], 'v_rwkv_a0': out['v_rwkv_a0'], 'v_rwkv_a2': out['v_rwkv_a2'], 'v_rwkv_g2': out['v_rwkv_g2'], 'v_rwkv_k_k': out['v_rwkv_k_k'], 'v_rwkv_k_a': out['v_rwkv_k_a'], 'v_rwkv_r_k': out['v_rwkv_r_k'], 'v_rwkv_ln_g': out['v_rwkv_ln_g'], 'v_rwkv_ln_b': out['v_rwkv_ln_b'], 'v_ssm_conv_w': out['v_ssm_conv_w'], 'v_ssm_conv_b': out['v_ssm_conv_b'], 'v_ssm_dt_bias': out['v_ssm_dt_bias'], 'v_ssm_a_log': out['v_ssm_a_log'], 'v_ssm_d': out['v_ssm_d'], 'v_ssm_norm_g': out['v_ssm_norm_g'], 'v_l0_w_out': out['v_l0_w_out'], 'v_l0_ln1_g': out['v_l0_ln1_g'], 'v_l0_ln1_b': out['v_l0_ln1_b'], 'v_ffn0_w_up': out['v_ffn0_w_up'], 'v_ffn0_conv_w': out['v_ffn0_conv_w'], 'v_ffn0_conv_b': out['v_ffn0_conv_b'], 'v_ffn0_w_down': out['v_ffn0_w_down'], 'v_l0_ln2_g': out['v_l0_ln2_g'], 'v_l0_ln2_b': out['v_l0_ln2_b'], 'v_l1_w_in': out['v_l1_w_in'], 'v_mla_q_norm_g': out['v_mla_q_norm_g'], 'v_mla_w_uq': out['v_mla_w_uq'], 'v_mla_kv_norm_g': out['v_mla_kv_norm_g'], 'v_mla_w_ukv': out['v_mla_w_ukv'], 'v_l1_w_out': out['v_l1_w_out'], 'v_l1_ln1_g': out['v_l1_ln1_g'], 'v_l1_ln1_b': out['v_l1_ln1_b'], 'v_ffn1_w_up': out['v_ffn1_w_up'], 'v_ffn1_conv_w': out['v_ffn1_conv_w'], 'v_ffn1_conv_b': out['v_ffn1_conv_b'], 'v_ffn1_w_down': out['v_ffn1_w_down'], 'v_l1_ln2_g': out['v_l1_ln2_g'], 'v_l1_ln2_b': out['v_l1_ln2_b']}


def _loss(weights, diff, rest, loss_target):
    with _jax.named_scope("forward"):
        args = {**rest, TWIN_DIFF_INPUT: diff, **{k: w.astype(_WEIGHT_DTYPES[k]) for k, w in weights.items()}}
        y = _forward(args)
    with _jax.named_scope("loss_head"):
        err = _jnp.square(y.astype(_jnp.float32) - loss_target)
        return 0.5 * _jnp.sum(_jnp.mean(err, axis=-1)) if err.ndim else 0.5 * err


def _adamw(w, g, m, v):
    m = ADAM_B1 * m + (1.0 - ADAM_B1) * g
    v = ADAM_B2 * v + (1.0 - ADAM_B2) * _jnp.square(g)
    m_hat = m / (1.0 - ADAM_B1 ** ADAM_STEP)
    v_hat = v / (1.0 - ADAM_B2 ** ADAM_STEP)
    delta = -ADAM_LR * (m_hat / (_jnp.sqrt(v_hat) + ADAM_EPS) + ADAM_WD * w)
    return delta, m, v


def reference(x, positions, l0_w_in, rwkv_mix, rwkv_w0, rwkv_w2, rwkv_a0, rwkv_a2, rwkv_g2, rwkv_k_k, rwkv_k_a, rwkv_r_k, rwkv_ln_g, rwkv_ln_b, ssm_conv_w, ssm_conv_b, ssm_dt_bias, ssm_a_log, ssm_d, ssm_norm_g, l0_w_out, l0_ln1_g, l0_ln1_b, ffn0_w_up, ffn0_conv_w, ffn0_conv_b, ffn0_w_down, l0_ln2_g, l0_ln2_b, l1_w_in, mla_q_norm_g, mla_w_uq, mla_kv_norm_g, mla_w_ukv, l1_w_out, l1_ln1_g, l1_ln1_b, ffn1_w_up, ffn1_conv_w, ffn1_conv_b, ffn1_w_down, l1_ln2_g, l1_ln2_b, loss_target, m_l0_w_in, m_rwkv_mix, m_rwkv_w0, m_rwkv_w2, m_rwkv_a0, m_rwkv_a2, m_rwkv_g2, m_rwkv_k_k, m_rwkv_k_a, m_rwkv_r_k, m_rwkv_ln_g, m_rwkv_ln_b, m_ssm_conv_w, m_ssm_conv_b, m_ssm_dt_bias, m_ssm_a_log, m_ssm_d, m_ssm_norm_g, m_l0_w_out, m_l0_ln1_g, m_l0_ln1_b, m_ffn0_w_up, m_ffn0_conv_w, m_ffn0_conv_b, m_ffn0_w_down, m_l0_ln2_g, m_l0_ln2_b, m_l1_w_in, m_mla_q_norm_g, m_mla_w_uq, m_mla_kv_norm_g, m_mla_w_ukv, m_l1_w_out, m_l1_ln1_g, m_l1_ln1_b, m_ffn1_w_up, m_ffn1_conv_w, m_ffn1_conv_b, m_ffn1_w_down, m_l1_ln2_g, m_l1_ln2_b, v_l0_w_in, v_rwkv_mix, v_rwkv_w0, v_rwkv_w2, v_rwkv_a0, v_rwkv_a2, v_rwkv_g2, v_rwkv_k_k, v_rwkv_k_a, v_rwkv_r_k, v_rwkv_ln_g, v_rwkv_ln_b, v_ssm_conv_w, v_ssm_conv_b, v_ssm_dt_bias, v_ssm_a_log, v_ssm_d, v_ssm_norm_g, v_l0_w_out, v_l0_ln1_g, v_l0_ln1_b, v_ffn0_w_up, v_ffn0_conv_w, v_ffn0_conv_b, v_ffn0_w_down, v_l0_ln2_g, v_l0_ln2_b, v_l1_w_in, v_mla_q_norm_g, v_mla_w_uq, v_mla_kv_norm_g, v_mla_w_ukv, v_l1_w_out, v_l1_ln1_g, v_l1_ln1_b, v_ffn1_w_up, v_ffn1_conv_w, v_ffn1_conv_b, v_ffn1_w_down, v_l1_ln2_g, v_l1_ln2_b):
    given = dict(x=x, positions=positions, l0_w_in=l0_w_in, rwkv_mix=rwkv_mix, rwkv_w0=rwkv_w0, rwkv_w2=rwkv_w2, rwkv_a0=rwkv_a0, rwkv_a2=rwkv_a2, rwkv_g2=rwkv_g2, rwkv_k_k=rwkv_k_k, rwkv_k_a=rwkv_k_a, rwkv_r_k=rwkv_r_k, rwkv_ln_g=rwkv_ln_g, rwkv_ln_b=rwkv_ln_b, ssm_conv_w=ssm_conv_w, ssm_conv_b=ssm_conv_b, ssm_dt_bias=ssm_dt_bias, ssm_a_log=ssm_a_log, ssm_d=ssm_d, ssm_norm_g=ssm_norm_g, l0_w_out=l0_w_out, l0_ln1_g=l0_ln1_g, l0_ln1_b=l0_ln1_b, ffn0_w_up=ffn0_w_up, ffn0_conv_w=ffn0_conv_w, ffn0_conv_b=ffn0_conv_b, ffn0_w_down=ffn0_w_down, l0_ln2_g=l0_ln2_g, l0_ln2_b=l0_ln2_b, l1_w_in=l1_w_in, mla_q_norm_g=mla_q_norm_g, mla_w_uq=mla_w_uq, mla_kv_norm_g=mla_kv_norm_g, mla_w_ukv=mla_w_ukv, l1_w_out=l1_w_out, l1_ln1_g=l1_ln1_g, l1_ln1_b=l1_ln1_b, ffn1_w_up=ffn1_w_up, ffn1_conv_w=ffn1_conv_w, ffn1_conv_b=ffn1_conv_b, ffn1_w_down=ffn1_w_down, l1_ln2_g=l1_ln2_g, l1_ln2_b=l1_ln2_b, loss_target=loss_target, m_l0_w_in=m_l0_w_in, m_rwkv_mix=m_rwkv_mix, m_rwkv_w0=m_rwkv_w0, m_rwkv_w2=m_rwkv_w2, m_rwkv_a0=m_rwkv_a0, m_rwkv_a2=m_rwkv_a2, m_rwkv_g2=m_rwkv_g2, m_rwkv_k_k=m_rwkv_k_k, m_rwkv_k_a=m_rwkv_k_a, m_rwkv_r_k=m_rwkv_r_k, m_rwkv_ln_g=m_rwkv_ln_g, m_rwkv_ln_b=m_rwkv_ln_b, m_ssm_conv_w=m_ssm_conv_w, m_ssm_conv_b=m_ssm_conv_b, m_ssm_dt_bias=m_ssm_dt_bias, m_ssm_a_log=m_ssm_a_log, m_ssm_d=m_ssm_d, m_ssm_norm_g=m_ssm_norm_g, m_l0_w_out=m_l0_w_out, m_l0_ln1_g=m_l0_ln1_g, m_l0_ln1_b=m_l0_ln1_b, m_ffn0_w_up=m_ffn0_w_up, m_ffn0_conv_w=m_ffn0_conv_w, m_ffn0_conv_b=m_ffn0_conv_b, m_ffn0_w_down=m_ffn0_w_down, m_l0_ln2_g=m_l0_ln2_g, m_l0_ln2_b=m_l0_ln2_b, m_l1_w_in=m_l1_w_in, m_mla_q_norm_g=m_mla_q_norm_g, m_mla_w_uq=m_mla_w_uq, m_mla_kv_norm_g=m_mla_kv_norm_g, m_mla_w_ukv=m_mla_w_ukv, m_l1_w_out=m_l1_w_out, m_l1_ln1_g=m_l1_ln1_g, m_l1_ln1_b=m_l1_ln1_b, m_ffn1_w_up=m_ffn1_w_up, m_ffn1_conv_w=m_ffn1_conv_w, m_ffn1_conv_b=m_ffn1_conv_b, m_ffn1_w_down=m_ffn1_w_down, m_l1_ln2_g=m_l1_ln2_g, m_l1_ln2_b=m_l1_ln2_b, v_l0_w_in=v_l0_w_in, v_rwkv_mix=v_rwkv_mix, v_rwkv_w0=v_rwkv_w0, v_rwkv_w2=v_rwkv_w2, v_rwkv_a0=v_rwkv_a0, v_rwkv_a2=v_rwkv_a2, v_rwkv_g2=v_rwkv_g2, v_rwkv_k_k=v_rwkv_k_k, v_rwkv_k_a=v_rwkv_k_a, v_rwkv_r_k=v_rwkv_r_k, v_rwkv_ln_g=v_rwkv_ln_g, v_rwkv_ln_b=v_rwkv_ln_b, v_ssm_conv_w=v_ssm_conv_w, v_ssm_conv_b=v_ssm_conv_b, v_ssm_dt_bias=v_ssm_dt_bias, v_ssm_a_log=v_ssm_a_log, v_ssm_d=v_ssm_d, v_ssm_norm_g=v_ssm_norm_g, v_l0_w_out=v_l0_w_out, v_l0_ln1_g=v_l0_ln1_g, v_l0_ln1_b=v_l0_ln1_b, v_ffn0_w_up=v_ffn0_w_up, v_ffn0_conv_w=v_ffn0_conv_w, v_ffn0_conv_b=v_ffn0_conv_b, v_ffn0_w_down=v_ffn0_w_down, v_l0_ln2_g=v_l0_ln2_g, v_l0_ln2_b=v_l0_ln2_b, v_l1_w_in=v_l1_w_in, v_mla_q_norm_g=v_mla_q_norm_g, v_mla_w_uq=v_mla_w_uq, v_mla_kv_norm_g=v_mla_kv_norm_g, v_mla_w_ukv=v_mla_w_ukv, v_l1_w_out=v_l1_w_out, v_l1_ln1_g=v_l1_ln1_g, v_l1_ln1_b=v_l1_ln1_b, v_ffn1_w_up=v_ffn1_w_up, v_ffn1_conv_w=v_ffn1_conv_w, v_ffn1_conv_b=v_ffn1_conv_b, v_ffn1_w_down=v_ffn1_w_down, v_l1_ln2_g=v_l1_ln2_g, v_l1_ln2_b=v_l1_ln2_b)
    weights = {n: given[n] for n in TWIN_WEIGHTS}
    shared = {n: given[n] for n in SHARED_INPUTS}
    per_example = {n: given[n] for n in ['x', 'positions']}
    grad_fn = _jax.value_and_grad(_loss, argnums=(0, 1))

    def one_microbatch(ex, loss_target):
        ex = dict(ex)
        diff = ex.pop(TWIN_DIFF_INPUT)
        return grad_fn(weights, diff, {**shared, **ex}, loss_target)

    if N_MICROBATCH == 1:
        loss, (grad_w, grad_x) = one_microbatch(per_example, given["loss_target"])
    else:
        def body(carry, xs):
            loss_sum, grad_sum = carry
            l_k, (gw_k, gx_k) = one_microbatch(xs[0], xs[1])
            with _jax.named_scope("update"):
                return (loss_sum + l_k, _jax.tree.map(_jnp.add, grad_sum, gw_k)), gx_k

        init = (_jnp.zeros((), _jnp.float32), _jax.tree.map(_jnp.zeros_like, weights))
        (loss, grad_w), grad_x = _jax.lax.scan(body, init, (per_example, given["loss_target"]))
    with _jax.named_scope("update"):
        delta_w, new_m, new_v = {}, {}, {}
        for n in TWIN_WEIGHTS:
            delta_w[n], new_m[n], new_v[n] = _adamw(weights[n], grad_w[n], given["m_" + n], given["v_" + n])
    return (loss, grad_x, *[grad_w[n] for n in TWIN_WEIGHTS], *[delta_w[n] for n in TWIN_WEIGHTS],
            *[new_m[n] for n in TWIN_WEIGHTS], *[new_v[n] for n in TWIN_WEIGHTS])
```

```python
import functools
import math

import jax
import jax.numpy as jnp
from jax import lax
from jax.experimental import pallas as pl
from jax.experimental.pallas import tpu as pltpu

F32 = jnp.float32
BF16 = jnp.bfloat16
HI = lax.Precision.HIGHEST

V7X_VMEM_BYTES = 64 * 1024 * 1024
VMEM_LIMIT = V7X_VMEM_BYTES - 8 * 1024 * 1024
LANES = 128
SUBLANES = 8
N_DEV = 8

D_MODEL = 1024
HEAD_DIM = 64
RWKV_DIM = 512
RWKV_HEADS = 8
RWKV_GN_EPS = 64e-5
RWKV_CHUNK = 64
SSM_DIM = 512
SSM_HEADS = 8
SSM_CHUNK = 128
SSM_STATE = 128
Q_BLOCK = 128
MLA_NOPE = 64
MLA_ROPE = 32
ROPE_THETA = 10000.0
D_FF = 2816
DEPTH = 2
ALPHA = (2 * DEPTH) ** 0.25
L0_PAD = 3456
L1_PAD = 2048

ADAM_LR = 0.001
ADAM_B1 = 0.9
ADAM_B2 = 0.999
ADAM_EPS = 1e-08
ADAM_WD = 0.01
ADAM_STEP = 10

NEG_BIG = -1e30


def _params(sem=None):
    return pltpu.CompilerParams(dimension_semantics=sem, vmem_limit_bytes=VMEM_LIMIT)


def _dg_raw(a, b, ca, cb, fast):
    dims = (((ca,), (cb,)), ((), ()))
    if fast:
        return lax.dot_general(a.astype(BF16), b.astype(BF16), dims, preferred_element_type=F32)
    return lax.dot_general(a, b, dims, precision=HI, preferred_element_type=F32)


@functools.partial(jax.custom_vjp, nondiff_argnums=(2, 3, 4))
def dg(a, b, ca, cb, fast):
    return _dg_raw(a, b, ca, cb, fast)


def _dg_fwd(a, b, ca, cb, fast):
    return _dg_raw(a, b, ca, cb, fast), (a, b)


def _dg_bwd(ca, cb, fast, res, ct):
    a, b = res
    fa, fb = 1 - ca, 1 - cb
    da = _dg_raw(ct, b, 1, fb, fast) if ca == 1 else _dg_raw(b, ct, fb, 1, fast)
    db = _dg_raw(a, ct, fa, 0, fast) if cb == 0 else _dg_raw(ct, a, 0, fa, fast)
    return da.astype(a.dtype), db.astype(b.dtype)


dg.defvjp(_dg_fwd, _dg_bwd)


def mmb(a, b):
    return dg(a, b, 1, 0, True)


def mmf(a, b):
    return dg(a, b, 1, 0, False)


def mmf_nt(a, b):
    return dg(a, b, 1, 1, False)


def mmf_tn(a, b):
    return dg(a, b, 0, 0, False)


def _iota(shape, dim):
    return lax.broadcasted_iota(jnp.int32, shape, dim)


def _softplus(x):
    return jnp.maximum(x, 0.0) + jnp.log1p(jnp.exp(-jnp.abs(x)))


def _silu(x):
    return x * jax.nn.sigmoid(x)


def _largest_tile(n, cap, mult):
    best = None
    t = mult
    while t <= min(n, cap):
        if n % t == 0:
            best = t
        t += mult
    return n if best is None else best


def mm(a, b, mode, name, add=None):
    if mode == "nn":
        (M, K), N = a.shape, b.shape[1]
    elif mode == "nt":
        (M, K), N = a.shape, b.shape[0]
    else:
        (K, M), N = a.shape, b.shape[1]
    tm = _largest_tile(M, 1024, LANES)
    tn = _largest_tile(N, 512, LANES)
    tk = _largest_tile(K, 512, LANES)
    nk = K // tk
    if mode == "nn":
        a_spec = pl.BlockSpec((tm, tk), lambda i, j, k: (i, k))
        b_spec = pl.BlockSpec((tk, tn), lambda i, j, k: (k, j))
        dims = (((1,), (0,)), ((), ()))
    elif mode == "nt":
        a_spec = pl.BlockSpec((tm, tk), lambda i, j, k: (i, k))
        b_spec = pl.BlockSpec((tn, tk), lambda i, j, k: (j, k))
        dims = (((1,), (1,)), ((), ()))
    else:
        a_spec = pl.BlockSpec((tk, tm), lambda i, j, k: (k, i))
        b_spec = pl.BlockSpec((tk, tn), lambda i, j, k: (k, j))
        dims = (((0,), (0,)), ((), ()))
    o_spec = pl.BlockSpec((tm, tn), lambda i, j, k: (i, j))
    has_add = add is not None

    def body(a_ref, b_ref, *rest):
        o_ref = rest[-1]
        k = pl.program_id(2)
        part = lax.dot_general(a_ref[...].astype(BF16), b_ref[...].astype(BF16), dims,
                               preferred_element_type=F32)

        @pl.when(k == 0)
        def _():
            o_ref[...] = part + rest[0][...] if has_add else part

        @pl.when(k > 0)
        def _():
            o_ref[...] += part

    ins = [a, b] + ([add] if has_add else [])
    specs = [a_spec, b_spec] + ([o_spec] if has_add else [])
    return pl.pallas_call(
        body, name=name, grid=(M // tm, N // tn, nk), in_specs=specs, out_specs=o_spec,
        out_shape=jax.ShapeDtypeStruct((M, N), F32),
        compiler_params=_params(("parallel", "parallel", "arbitrary")),
    )(*ins)


def _x_specs(xs, tm):
    return [pl.BlockSpec((tm, w), functools.partial(lambda i, cb: (i, cb), cb=cb)) for _, w, cb in xs]


def _p_specs(ps):
    return [pl.BlockSpec(p.shape, lambda i: (0, 0)) for p in ps]


def tok_fwd(name, f, xs, ps, out_widths, tm):
    n = xs[0][0].shape[0]
    nx, npar = len(xs), len(ps)

    def body(*refs):
        xv = [r[...] for r in refs[:nx]]
        pv = [r[...].astype(F32) for r in refs[nx:nx + npar]]
        outs = f(*xv, *pv)
        for o, r in zip(outs, refs[nx + npar:]):
            r[...] = o

    return pl.pallas_call(
        body, name=name, grid=(n // tm,),
        in_specs=_x_specs(xs, tm) + _p_specs(ps),
        out_specs=[pl.BlockSpec((tm, w), lambda i: (i, 0)) for w in out_widths],
        out_shape=[jax.ShapeDtypeStruct((n, w), F32) for w in out_widths],
        compiler_params=_params(("parallel",)),
    )(*[x[0] for x in xs], *ps)


def tok_bwd(name, f, xs, ps, cts, tm):
    n = xs[0][0].shape[0]
    nx, npar = len(xs), len(ps)
    ct_flat = [c for group in cts for c in group]
    nct = len(ct_flat)

    def body(*refs):
        xv = [r[...] for r in refs[:nx]]
        pv = [r[...].astype(F32) for r in refs[nx:nx + npar]]
        ct_refs = refs[nx + npar:nx + npar + nct]
        dx_refs = refs[nx + npar + nct:nx + npar + nct + nx]
        dp_refs = refs[nx + npar + nct + nx:]
        cv, pos = [], 0
        for group in cts:
            acc = ct_refs[pos][...]
            for r in ct_refs[pos + 1:pos + len(group)]:
                acc = acc + r[...]
            cv.append(acc)
            pos += len(group)
        _, vjp = jax.vjp(f, *xv, *pv)
        grads = vjp(tuple(cv))
        for g, r in zip(grads[:nx], dx_refs):
            r[...] = g

        @pl.when(pl.program_id(0) == 0)
        def _():
            for r in dp_refs:
                r[...] = jnp.zeros_like(r)

        for g, r in zip(grads[nx:], dp_refs):
            r[...] += g

    outs = pl.pallas_call(
        body, name=name, grid=(n // tm,),
        in_specs=(_x_specs(xs, tm) + _p_specs(ps)
                  + [pl.BlockSpec((tm, c.shape[1]), lambda i: (i, 0)) for c in ct_flat]),
        out_specs=([pl.BlockSpec((tm, w), lambda i: (i, 0)) for _, w, _ in xs] + _p_specs(ps)),
        out_shape=([jax.ShapeDtypeStruct((n, w), F32) for _, w, _ in xs]
                   + [jax.ShapeDtypeStruct(p.shape, F32) for p in ps]),
        compiler_params=_params(("arbitrary",)),
    )(*[x[0] for x in xs], *ps, *ct_flat)
    return outs[:nx], outs[nx:]


def f_ln(h, y, g, b):
    pre = ALPHA * h + y
    mu = jnp.mean(pre, axis=-1, keepdims=True)
    xc = pre - mu
    var = jnp.mean(xc * xc, axis=-1, keepdims=True)
    return (xc * lax.rsqrt(var + 1e-5) * g + b,)


def _head_sel(width, nheads_pad, per):
    return jnp.where(_iota((width, nheads_pad), 0) // per == _iota((width, nheads_pad), 1), 1.0, 0.0).astype(F32)


def _head_sel_t(nheads_pad, width, per):
    return jnp.where(_iota((nheads_pad, width), 1) // per == _iota((nheads_pad, width), 0), 1.0, 0.0).astype(F32)


def f_rwkv_pre(r, k, v, lora, glo, w0, w2p, a0, a2p, g2, k_k, k_a):
    lane = _iota(lora.shape, 1)
    tw = jnp.where(lane < 64, jnp.tanh(lora), 0.0)
    ta = jnp.where(lane >= 64, lora, 0.0)
    log_w = -_softplus(-(w0 + mmb(tw, w2p))) - 0.5
    lw = -jnp.exp(log_w)
    a = jax.nn.sigmoid(a0 + mmb(ta, a2p))
    g = mmb(jax.nn.sigmoid(glo), g2)
    sel = _head_sel(RWKV_DIM, LANES, HEAD_DIM)
    sel_t = _head_sel_t(LANES, RWKV_DIM, HEAD_DIM)
    kk = k * k_k
    nrm = jnp.sqrt(jnp.maximum(mmf(kk * kk, sel), 1e-24))
    kkn = kk * mmf(1.0 / nrm, sel_t)
    kmod = k * (1.0 + (a - 1.0) * k_a)
    return r, v, lw, kmod, -kkn, kkn * a, g


def f_rwkv_post(y, r, kmod, v, g, ln_g, ln_b, r_k):
    sel = _head_sel(RWKV_DIM, LANES, HEAD_DIM)
    sel_t = _head_sel_t(LANES, RWKV_DIM, HEAD_DIM)
    inv = 1.0 / HEAD_DIM
    mu = mmf(mmf(y, sel) * inv, sel_t)
    yc = y - mu
    var = mmf(yc * yc, sel) * inv
    rstd = mmf(lax.rsqrt(var + RWKV_GN_EPS), sel_t)
    yn = yc * rstd * ln_g + ln_b
    bonus = mmf(mmf(r * kmod * r_k, sel), sel_t) * v
    return ((yn + bonus) * g,)


def f_ssd_post(y, xs, z, d_skip, norm_g):
    sel_t = _head_sel_t(LANES, SSM_DIM, HEAD_DIM)
    d_e = jnp.sum(mmf(jnp.broadcast_to(d_skip, (SUBLANES, LANES)), sel_t), axis=0, keepdims=True) * (1.0 / SUBLANES)
    u = (y + xs * d_e) * _silu(z)
    first = _iota(u.shape, 1) < (SSM_DIM // 2)
    uu = u * u
    inv = 2.0 / SSM_DIM
    ms0 = jnp.sum(jnp.where(first, uu, 0.0), axis=-1, keepdims=True) * inv
    ms1 = jnp.sum(jnp.where(first, 0.0, uu), axis=-1, keepdims=True) * inv
    ms = jnp.where(first, ms0, ms1)
    return (u * lax.rsqrt(ms + 1e-5) * norm_g,)


def f_mla_pre(cq, ckv, qg, w_uq, kvg, w_ukv):
    def rms(x, g):
        return x * lax.rsqrt(jnp.mean(x * x, axis=-1, keepdims=True) + 1e-6) * g
    return mmb(rms(cq, qg), w_uq), mmb(rms(ckv, kvg), w_ukv)


def _shift_down(x, s, row):
    return x if s == 0 else jnp.where(row >= s, pltpu.roll(x, s, 0), 0.0)


def _shift_up(x, s, row, t):
    return x if s == 0 else jnp.where(row < t - s, pltpu.roll(x, t - s, 0), 0.0)


def dwconv_fwd(name, u, colmap, w, b, taps, silu, upmap=None):
    bsz, t, _ = u.shape
    c = w.shape[1]
    tc = LANES
    has_up = upmap is not None

    def body(*refs):
        u_ref, w_ref, b_ref = refs[:3]
        o_ref = refs[-1]
        uv = u_ref[...]
        wv = w_ref[...]
        row = _iota(uv.shape, 0)
        acc = jnp.broadcast_to(b_ref[...], uv.shape)
        for i in range(taps):
            acc = acc + wv[i:i + 1, :] * _shift_down(uv, taps - 1 - i, row)
        if silu:
            acc = _silu(acc)
        if has_up:
            acc = acc * refs[3][...]
        o_ref[...] = acc

    specs = [pl.BlockSpec((None, t, tc), lambda bb, j: (bb, 0, colmap(j))),
             pl.BlockSpec((taps, tc), lambda bb, j: (0, j)),
             pl.BlockSpec((1, tc), lambda bb, j: (0, j))]
    ins = [u, w, b]
    if has_up:
        specs.append(pl.BlockSpec((None, t, tc), lambda bb, j: (bb, 0, upmap(j))))
        ins.append(u)
    return pl.pallas_call(
        body, name=name, grid=(bsz, c // tc), in_specs=specs,
        out_specs=pl.BlockSpec((None, t, tc), lambda bb, j: (bb, 0, j)),
        out_shape=jax.ShapeDtypeStruct((bsz, t, c), F32),
        compiler_params=_params(("parallel", "parallel")),
    )(*ins)


def dwconv_bwd(name, u, colmap, w, b, taps, silu, dout, upmap=None):
    bsz, t, _ = u.shape
    c = w.shape[1]
    tc = LANES
    has_up = upmap is not None

    def body(*refs):
        u_ref, w_ref, b_ref, d_ref = refs[:4]
        nin = 5 if has_up else 4
        du_ref, dw_ref, db_ref = refs[nin:nin + 3]
        uv = u_ref[...]
        wv = w_ref[...]
        dv = d_ref[...]
        row = _iota(uv.shape, 0)
        shifted = [_shift_down(uv, taps - 1 - i, row) for i in range(taps)]
        cg = jnp.broadcast_to(b_ref[...], uv.shape)
        for i in range(taps):
            cg = cg + wv[i:i + 1, :] * shifted[i]
        if silu:
            sg = jax.nn.sigmoid(cg)
            act = cg * sg
            dact_dcg = sg * (1.0 + cg * (1.0 - sg))
        else:
            act = cg
            dact_dcg = None
        if has_up:
            refs[nin + 3][...] = dv * act
            dv = dv * refs[4][...]
        dcg = dv * dact_dcg if silu else dv
        du = jnp.zeros_like(uv)
        for i in range(taps):
            du = du + wv[i:i + 1, :] * _shift_up(dcg, taps - 1 - i, row, t)
        du_ref[...] = du

        @pl.when(pl.program_id(1) == 0)
        def _():
            dw_ref[...] = jnp.zeros_like(dw_ref)
            db_ref[...] = jnp.zeros_like(db_ref)

        for i in range(taps):
            dw_ref[i:i + 1, :] += jnp.sum(dcg * shifted[i], axis=0, keepdims=True)
        db_ref[...] += jnp.sum(dcg, axis=0, keepdims=True)

    specs = [pl.BlockSpec((None, t, tc), lambda j, bb: (bb, 0, colmap(j))),
             pl.BlockSpec((taps, tc), lambda j, bb: (0, j)),
             pl.BlockSpec((1, tc), lambda j, bb: (0, j)),
             pl.BlockSpec((None, t, tc), lambda j, bb: (bb, 0, j))]
    ins = [u, w, b, dout]
    if has_up:
        specs.append(pl.BlockSpec((None, t, tc), lambda j, bb: (bb, 0, upmap(j))))
        ins.append(u)
    big = pl.BlockSpec((None, t, tc), lambda j, bb: (bb, 0, j))
    out_specs = [big, pl.BlockSpec((taps, tc), lambda j, bb: (0, j)), pl.BlockSpec((1, tc), lambda j, bb: (0, j))]
    out_shape = [jax.ShapeDtypeStruct((bsz, t, c), F32), jax.ShapeDtypeStruct((taps, c), F32),
                 jax.ShapeDtypeStruct((1, c), F32)]
    if has_up:
        out_specs.append(big)
        out_shape.append(jax.ShapeDtypeStruct((bsz, t, c), F32))
    return pl.pallas_call(
        body, name=name, grid=(c // tc, bsz), in_specs=specs, out_specs=out_specs, out_shape=out_shape,
        compiler_params=_params(("parallel", "arbitrary")),
    )(*ins)


def rwkv_chunk(s0, r, lw, k, v, al, be):
    c = r.shape[0]
    ii, jj = _iota((c, c), 0), _iota((c, c), 1)
    incl, strict = ii >= jj, ii > jj
    cum = mmf(jnp.where(incl, 1.0, 0.0), lw)
    gam = jnp.exp(cum)
    gam_inv = jnp.exp(-cum)
    at = al * jnp.exp(cum - lw)
    rt = r * gam
    bt = be * gam_inv
    kt = k * gam_inv
    a_b = jnp.where(strict, mmf_nt(at, bt), 0.0)
    a_k = jnp.where(strict, mmf_nt(at, kt), 0.0)
    rhs = mmf_nt(at, s0) + mmf(a_k, v)
    p = jnp.where(ii == jj, 1.0, 0.0) + a_b
    m = a_b
    for _ in range(int(math.log2(c)) - 1):
        m = mmf(m, m)
        p = p + mmf(p, m)
    u = mmf(p, rhs)
    r_b = jnp.where(incl, mmf_nt(rt, bt), 0.0)
    r_k = jnp.where(incl, mmf_nt(rt, kt), 0.0)
    y = mmf_nt(rt, s0) + mmf(r_b, u) + mmf(r_k, v)
    gam_end = jnp.exp(jnp.sum(lw, axis=0, keepdims=True))
    s1 = (s0 + mmf_tn(u, bt) + mmf_tn(v, kt)) * gam_end
    return y, s1


def rwkv_scan_fwd(r, lw, k, v, al, be):
    bsz, h, t, d = r.shape
    c = RWKV_CHUNK
    nc = t // c

    def body(r_ref, lw_ref, k_ref, v_ref, al_ref, be_ref, y_ref, st_ref, s_scr):
        @pl.when(pl.program_id(1) == 0)
        def _():
            s_scr[...] = jnp.zeros_like(s_scr)

        for hh in range(h):
            s0 = s_scr[hh]
            st_ref[hh] = s0
            y, s1 = rwkv_chunk(s0, r_ref[hh], lw_ref[hh], k_ref[hh], v_ref[hh], al_ref[hh], be_ref[hh])
            y_ref[hh] = y
            s_scr[hh] = s1

    seq = pl.BlockSpec((None, h, c, d), lambda b, i: (b, 0, i, 0))
    return pl.pallas_call(
        body, name="rwkv_scan_fwd", grid=(bsz, nc), in_specs=[seq] * 6,
        out_specs=[seq, pl.BlockSpec((None, h, None, d, d), lambda b, i: (b, 0, i, 0, 0))],
        out_shape=[jax.ShapeDtypeStruct((bsz, h, t, d), F32), jax.ShapeDtypeStruct((bsz, h, nc, d, d), F32)],
        scratch_shapes=[pltpu.VMEM((h, d, d), F32)],
        compiler_params=_params(("parallel", "arbitrary")),
    )(r, lw, k, v, al, be)


def rwkv_scan_bwd(r, lw, k, v, al, be, states, dy):
    bsz, h, t, d = r.shape
    c = RWKV_CHUNK
    nc = t // c

    def body(r_ref, lw_ref, k_ref, v_ref, al_ref, be_ref, st_ref, dy_ref,
             dr_ref, dlw_ref, dk_ref, dv_ref, dal_ref, dbe_ref, ds_scr):
        @pl.when(pl.program_id(1) == 0)
        def _():
            ds_scr[...] = jnp.zeros_like(ds_scr)

        for hh in range(h):
            args = (st_ref[hh], r_ref[hh], lw_ref[hh], k_ref[hh], v_ref[hh], al_ref[hh], be_ref[hh])
            _, vjp = jax.vjp(rwkv_chunk, *args)
            ds0, dr, dlw, dk, dv, dal, dbe = vjp((dy_ref[hh], ds_scr[hh]))
            ds_scr[hh] = ds0
            dr_ref[hh] = dr
            dlw_ref[hh] = dlw
            dk_ref[hh] = dk
            dv_ref[hh] = dv
            dal_ref[hh] = dal
            dbe_ref[hh] = dbe

    seq = pl.BlockSpec((None, h, c, d), lambda b, i: (b, 0, nc - 1 - i, 0))
    st = pl.BlockSpec((None, h, None, d, d), lambda b, i: (b, 0, nc - 1 - i, 0, 0))
    return pl.pallas_call(
        body, name="rwkv_scan_bwd", grid=(bsz, nc), in_specs=[seq] * 6 + [st, seq],
        out_specs=[seq] * 6, out_shape=[jax.ShapeDtypeStruct((bsz, h, t, d), F32)] * 6,
        scratch_shapes=[pltpu.VMEM((h, d, d), F32)],
        compiler_params=_params(("parallel", "arbitrary")),
    )(r, lw, k, v, al, be, states, dy)


def ssd_chunk(st, xs, bm, cm, dtr, dt_bias, a_log):
    n = SSM_CHUNK
    ii, jj = _iota((n, n), 0), _iota((n, n), 1)
    incl = ii >= jj
    lane = _iota((n, LANES), 1)
    dt = _softplus(dtr + dt_bias)
    a = dt * (-jnp.exp(a_log))
    acum = mmf(jnp.where(incl, 1.0, 0.0), a)
    last_row = jnp.where(jj == n - 1, 1.0, 0.0)
    cb = [mmf_nt(cm[g], bm[g]) for g in range(2)]
    ys, st_new = [], []
    for m in range(4):
        g = m // 2
        e_m = jnp.where(_iota((LANES, LANES), 0) == 2 * m + _iota((LANES, LANES), 1) // HEAD_DIM, 1.0, 0.0)
        dt_m = mmf(dt, e_m)
        ac_m = mmf(acum, e_m)
        x = xs[m] * dt_m
        last_m = mmf(last_row, ac_m)
        y = jnp.zeros((n, LANES), F32)
        for hh in range(2):
            e_h = jnp.where(_iota((LANES, n), 0) == 2 * m + hh, 1.0, 0.0)
            colb = mmf(acum, e_h)
            seg = jnp.where(incl, colb - colb.T, NEG_BIG)
            yh = mmf(cb[g] * jnp.exp(seg), x)
            y = y + jnp.where(lane // HEAD_DIM == hh, yh, 0.0)
        y = y + jnp.exp(ac_m) * mmf(cm[g], st[m])
        ys.append(y)
        st_new.append(jnp.exp(last_m) * st[m] + mmf_tn(bm[g], x * jnp.exp(last_m - ac_m)))
    return tuple(ys), tuple(st_new)


def _ssd_load(xbc_ref, dtr_ref):
    xs = tuple(xbc_ref[:, m * LANES:(m + 1) * LANES] for m in range(4))
    bm = tuple(xbc_ref[:, SSM_DIM + g * LANES:SSM_DIM + (g + 1) * LANES] for g in range(2))
    cm = tuple(xbc_ref[:, SSM_DIM + 2 * LANES + g * LANES:SSM_DIM + 2 * LANES + (g + 1) * LANES] for g in range(2))
    return xs, bm, cm, dtr_ref[...]


def ssd_fwd(xbc, proj, dt_col, dt_bias, a_log):
    bsz, t, _ = xbc.shape
    n = SSM_CHUNK
    nc = t // n

    def body(xbc_ref, dtr_ref, dtb_ref, al_ref, y_ref, st_ref, s_scr):
        @pl.when(pl.program_id(1) == 0)
        def _():
            s_scr[...] = jnp.zeros_like(s_scr)

        st = tuple(s_scr[m] for m in range(4))
        for m in range(4):
            st_ref[m] = st[m]
        xs, bm, cm, dtr = _ssd_load(xbc_ref, dtr_ref)
        ys, st_new = ssd_chunk(st, xs, bm, cm, dtr, dtb_ref[...], al_ref[...])
        for m in range(4):
            y_ref[:, m * LANES:(m + 1) * LANES] = ys[m]
            s_scr[m] = st_new[m]

    vec = pl.BlockSpec((1, LANES), lambda b, i: (0, 0))
    return pl.pallas_call(
        body, name="ssd_fwd", grid=(bsz, nc),
        in_specs=[pl.BlockSpec((None, n, 2 * SSM_DIM), lambda b, i: (b, i, 0)),
                  pl.BlockSpec((None, n, LANES), lambda b, i: (b, i, dt_col)), vec, vec],
        out_specs=[pl.BlockSpec((None, n, SSM_DIM), lambda b, i: (b, i, 0)),
                   pl.BlockSpec((None, None, 4, SSM_STATE, LANES), lambda b, i: (b, i, 0, 0, 0))],
        out_shape=[jax.ShapeDtypeStruct((bsz, t, SSM_DIM), F32),
                   jax.ShapeDtypeStruct((bsz, nc, 4, SSM_STATE, LANES), F32)],
        scratch_shapes=[pltpu.VMEM((4, SSM_STATE, LANES), F32)],
        compiler_params=_params(("parallel", "arbitrary")),
    )(xbc, proj, dt_bias, a_log)


def ssd_bwd(xbc, proj, dt_col, dt_bias, a_log, states, dy, dxs_extra):
    bsz, t, _ = xbc.shape
    n = SSM_CHUNK
    nc = t // n

    def body(xbc_ref, dtr_ref, dtb_ref, al_ref, st_ref, dy_ref, ex_ref,
             dxbc_ref, ddtr_ref, ddtb_ref, dal_ref, ds_scr):
        first = jnp.logical_and(pl.program_id(0) == 0, pl.program_id(1) == 0)

        @pl.when(pl.program_id(1) == 0)
        def _():
            ds_scr[...] = jnp.zeros_like(ds_scr)

        @pl.when(first)
        def _():
            ddtb_ref[...] = jnp.zeros_like(ddtb_ref)
            dal_ref[...] = jnp.zeros_like(dal_ref)

        st = tuple(st_ref[m] for m in range(4))
        xs, bm, cm, dtr = _ssd_load(xbc_ref, dtr_ref)
        _, vjp = jax.vjp(ssd_chunk, st, xs, bm, cm, dtr, dtb_ref[...], al_ref[...])
        dys = tuple(dy_ref[:, m * LANES:(m + 1) * LANES] for m in range(4))
        dst_in = tuple(ds_scr[m] for m in range(4))
        dst, dxs, dbm, dcm, ddtr, ddtb, dal = vjp((dys, dst_in))
        for m in range(4):
            ds_scr[m] = dst[m]
            sl = slice(m * LANES, (m + 1) * LANES)
            dxbc_ref[:, sl] = dxs[m] + ex_ref[:, sl]
        for g in range(2):
            dxbc_ref[:, SSM_DIM + g * LANES:SSM_DIM + (g + 1) * LANES] = dbm[g]
            dxbc_ref[:, SSM_DIM + 2 * LANES + g * LANES:SSM_DIM + 2 * LANES + (g + 1) * LANES] = dcm[g]
        ddtr_ref[...] = ddtr
        ddtb_ref[...] += ddtb
        dal_ref[...] += dal

    vec = pl.BlockSpec((1, LANES), lambda b, i: (0, 0))
    rev = lambda b, i: (b, nc - 1 - i, 0)
    return pl.pallas_call(
        body, name="ssd_bwd", grid=(bsz, nc),
        in_specs=[pl.BlockSpec((None, n, 2 * SSM_DIM), rev),
                  pl.BlockSpec((None, n, LANES), lambda b, i: (b, nc - 1 - i, dt_col)), vec, vec,
                  pl.BlockSpec((None, None, 4, SSM_STATE, LANES), lambda b, i: (b, nc - 1 - i, 0, 0, 0)),
                  pl.BlockSpec((None, n, SSM_DIM), rev), pl.BlockSpec((None, n, SSM_DIM), rev)],
        out_specs=[pl.BlockSpec((None, n, 2 * SSM_DIM), rev), pl.BlockSpec((None, n, LANES), rev), vec, vec],
        out_shape=[jax.ShapeDtypeStruct((bsz, t, 2 * SSM_DIM), F32), jax.ShapeDtypeStruct((bsz, t, LANES), F32),
                   jax.ShapeDtypeStruct((1, LANES), F32), jax.ShapeDtypeStruct((1, LANES), F32)],
        scratch_shapes=[pltpu.VMEM((4, SSM_STATE, LANES), F32)],
        compiler_params=_params(("arbitrary", "arbitrary")),
    )(xbc, proj, dt_bias, a_log, states, dy, dxs_extra)


def sb_block(q, kj, vj, carry, maskf):
    mask = maskf > 0.5
    z = dg(q, kj, 1, 1, True) * (HEAD_DIM ** -0.5)
    ls = -_softplus(-z)
    lk = jnp.where(mask, ls - z, 0.0)
    n = Q_BLOCK
    tri = jnp.where(_iota((n, n), 0) >= _iota((n, n), 1), 1.0, 0.0)
    log_att = ls + carry + mmf(lk, tri) - lk
    att = jnp.where(mask, jnp.exp(log_att), 0.0)
    return mmb(att, vj), carry + jnp.sum(lk, axis=1, keepdims=True)


def _sb_mask(qi, j):
    n = Q_BLOCK
    return jnp.where(j * n + _iota((n, n), 1) < qi * n + _iota((n, n), 0), 1.0, 0.0)


def sb_fwd(q, k, v):
    bsz, h, t, d = q.shape
    n = Q_BLOCK

    def body(q_ref, k_ref, v_ref, o_ref):
        qi = pl.program_id(2)
        qv = q_ref[...]

        def step(i, state):
            acc, carry = state
            j = qi - i
            rows = pl.ds(pl.multiple_of(j * n, n), n)
            o, carry = sb_block(qv, k_ref[rows, :], v_ref[rows, :], carry, _sb_mask(qi, j))
            return acc + o, carry

        acc, _ = lax.fori_loop(0, qi + 1, step, (jnp.zeros((n, d), F32), jnp.zeros((n, 1), F32)))
        o_ref[...] = acc

    blk = pl.BlockSpec((None, None, n, d), lambda b, hh, i: (b, hh, i, 0))
    full = pl.BlockSpec((None, None, t, d), lambda b, hh, i: (b, hh, 0, 0))
    return pl.pallas_call(
        body, name="sb_fwd", grid=(bsz, h, t // n), in_specs=[blk, full, full], out_specs=blk,
        out_shape=jax.ShapeDtypeStruct((bsz, h, t, d), F32),
        compiler_params=_params(("parallel", "parallel", "arbitrary")),
    )(q, k, v)


def sb_bwd(q, k, v, do):
    bsz, h, t, d = q.shape
    n = Q_BLOCK

    def body(q_ref, k_ref, v_ref, do_ref, dq_ref, dk_ref, dv_ref, carries):
        qi = pl.program_id(2)
        qv = q_ref[...]
        dov = do_ref[...]

        @pl.when(qi == 0)
        def _():
            dk_ref[...] = jnp.zeros_like(dk_ref)
            dv_ref[...] = jnp.zeros_like(dv_ref)

        def fwd_step(i, carry):
            j = qi - i
            carries[j] = carry
            rows = pl.ds(pl.multiple_of(j * n, n), n)
            _, carry = sb_block(qv, k_ref[rows, :], v_ref[rows, :], carry, _sb_mask(qi, j))
            return carry

        lax.fori_loop(0, qi + 1, fwd_step, jnp.zeros((n, 1), F32))

        def bwd_step(j, state):
            dq, dcarry = state
            rows = pl.ds(pl.multiple_of(j * n, n), n)
            _, vjp = jax.vjp(sb_block, qv, k_ref[rows, :], v_ref[rows, :], carries[j], _sb_mask(qi, j))
            dqj, dkj, dvj, dc, _ = vjp((dov, dcarry))
            dk_ref[rows, :] += dkj
            dv_ref[rows, :] += dvj
            return dq + dqj, dc

        dq, _ = lax.fori_loop(0, qi + 1, bwd_step, (jnp.zeros((n, d), F32), jnp.zeros((n, 1), F32)))
        dq_ref[...] = dq

    blk = pl.BlockSpec((None, None, n, d), lambda b, hh, i: (b, hh, i, 0))
    full = pl.BlockSpec((None, None, t, d), lambda b, hh, i: (b, hh, 0, 0))
    shp = jax.ShapeDtypeStruct((bsz, h, t, d), F32)
    return pl.pallas_call(
        body, name="sb_bwd", grid=(bsz, h, t // n), in_specs=[blk, full, full, blk],
        out_specs=[blk, full, full], out_shape=[shp, shp, shp],
        scratch_shapes=[pltpu.VMEM((t // n, n, 1), F32)],
        compiler_params=_params(("parallel", "parallel", "arbitrary")),
    )(q, k, v, do)


def mla_block(qn, qp, kn, kp, v, maskf):
    scale = (MLA_NOPE + MLA_ROPE) ** -0.5
    s = (dg(qn, kn, 1, 1, True) + dg(qp, kp, 1, 1, True)) * scale
    s = jnp.where(maskf > 0.5, s, NEG_BIG)
    p = jnp.exp(s - jnp.max(s, axis=-1, keepdims=True))
    prob = p / jnp.sum(p, axis=-1, keepdims=True)
    return mmb(prob, v)


def _mla_mask(qi, t):
    n = Q_BLOCK
    return jnp.where(_iota((n, t), 1) <= qi * n + _iota((n, t), 0), 1.0, 0.0)


def _mla_specs(t):
    n = Q_BLOCK
    return (pl.BlockSpec((None, None, n, MLA_NOPE), lambda b, hh, i: (b, hh, i, 0)),
            pl.BlockSpec((None, None, n, MLA_ROPE), lambda b, hh, i: (b, hh, i, 0)),
            pl.BlockSpec((None, None, t, MLA_NOPE), lambda b, hh, i: (b, hh, 0, 0)),
            pl.BlockSpec((None, None, t, MLA_ROPE), lambda b, hh, i: (b, 0, 0, 0)))


def mla_fwd(qn, qp, kn, kp, v):
    bsz, h, t, _ = qn.shape

    def body(qn_ref, qp_ref, kn_ref, kp_ref, v_ref, o_ref):
        o_ref[...] = mla_block(qn_ref[...], qp_ref[...], kn_ref[...], kp_ref[...], v_ref[...],
                               _mla_mask(pl.program_id(2), t))

    qn_s, qp_s, kn_s, kp_s = _mla_specs(t)
    return pl.pallas_call(
        body, name="mla_fwd", grid=(bsz, h, t // Q_BLOCK), in_specs=[qn_s, qp_s, kn_s, kp_s, kn_s],
        out_specs=qn_s, out_shape=jax.ShapeDtypeStruct(qn.shape, F32),
        compiler_params=_params(("parallel", "parallel", "arbitrary")),
    )(qn, qp, kn, kp, v)


def mla_bwd(qn, qp, kn, kp, v, do):
    bsz, h, t, _ = qn.shape

    def body(qn_ref, qp_ref, kn_ref, kp_ref, v_ref, do_ref, dqn_ref, dqp_ref, dkn_ref, dkp_ref, dv_ref):
        hh, qi = pl.program_id(1), pl.program_id(2)

        @pl.when(qi == 0)
        def _():
            dkn_ref[...] = jnp.zeros_like(dkn_ref)
            dv_ref[...] = jnp.zeros_like(dv_ref)

        @pl.when(jnp.logical_and(qi == 0, hh == 0))
        def _():
            dkp_ref[...] = jnp.zeros_like(dkp_ref)

        _, vjp = jax.vjp(mla_block, qn_ref[...], qp_ref[...], kn_ref[...], kp_ref[...], v_ref[...],
                         _mla_mask(qi, t))
        dqn, dqp, dkn, dkp, dv, _ = vjp(do_ref[...])
        dqn_ref[...] = dqn
        dqp_ref[...] = dqp
        dkn_ref[...] += dkn
        dkp_ref[...] += dkp
        dv_ref[...] += dv

    qn_s, qp_s, kn_s, kp_s = _mla_specs(t)
    return pl.pallas_call(
        body, name="mla_bwd", grid=(bsz, h, t // Q_BLOCK), in_specs=[qn_s, qp_s, kn_s, kp_s, kn_s, qn_s],
        out_specs=[qn_s, qp_s, kn_s, kp_s, kn_s],
        out_shape=[jax.ShapeDtypeStruct(qn.shape, F32), jax.ShapeDtypeStruct(qp.shape, F32),
                   jax.ShapeDtypeStruct(kn.shape, F32), jax.ShapeDtypeStruct(kp.shape, F32),
                   jax.ShapeDtypeStruct(v.shape, F32)],
        compiler_params=_params(("parallel", "arbitrary", "arbitrary")),
    )(qn, qp, kn, kp, v, do)


def rope(name, x, pos, inv_freq, sign):
    bsz, hx, t, d = x.shape
    half = d // 2

    def body(x_ref, pos_ref, f_ref, o_ref):
        xv = x_ref[...]
        ang = pos_ref[...].astype(F32) * f_ref[...]
        ri, ci = _iota((d, d), 0), _iota((d, d), 1)
        rot = jnp.where(ri == ci + half, -1.0, 0.0) + jnp.where(ri + half == ci, 1.0, 0.0)
        o_ref[...] = xv * jnp.cos(ang) + sign * (mmf(xv, rot) * jnp.sin(ang))

    return pl.pallas_call(
        body, name=name, grid=(bsz, hx),
        in_specs=[pl.BlockSpec((None, None, t, d), lambda b, hh: (b, hh, 0, 0)),
                  pl.BlockSpec((None, t, 1), lambda b, hh: (b, 0, 0)),
                  pl.BlockSpec((1, d), lambda b, hh: (0, 0))],
        out_specs=pl.BlockSpec((None, None, t, d), lambda b, hh: (b, hh, 0, 0)),
        out_shape=jax.ShapeDtypeStruct(x.shape, F32),
        compiler_params=_params(("parallel", "parallel")),
    )(x, pos, inv_freq)


def loss_head(h, target, tm):
    n, d = h.shape

    def body(h_ref, t_ref, dh_ref, l_ref):
        @pl.when(pl.program_id(0) == 0)
        def _():
            l_ref[...] = jnp.zeros_like(l_ref)

        e = h_ref[...] - t_ref[...]
        dh_ref[...] = e * (1.0 / d)
        l_ref[...] += jnp.sum(e * e, axis=(0, 1), keepdims=True) * (0.5 / d)

    row = pl.BlockSpec((tm, d), lambda i: (i, 0))
    dh, l = pl.pallas_call(
        body, name="loss_head", grid=(n // tm,), in_specs=[row, row],
        out_specs=[row, pl.BlockSpec((SUBLANES, LANES), lambda i: (0, 0))],
        out_shape=[jax.ShapeDtypeStruct((n, d), F32), jax.ShapeDtypeStruct((SUBLANES, LANES), F32)],
        compiler_params=_params(("arbitrary",)),
    )(h, target)
    return dh, l[0, 0]


def _peer_exchange(name, src, out_rows, gather):
    r = src.shape[-2]

    def body(src_ref, out_ref, send_sems, recv_sems, local_sem):
        x, y, c = lax.axis_index("x"), lax.axis_index("y"), lax.axis_index("c")
        me = 4 * x + 2 * y + c
        local = pltpu.make_async_copy(src_ref if gather else src_ref.at[me], out_ref.at[me], local_sem)
        local.start()
        copies = []
        for m in range(1, N_DEV):
            px, py, pc = x ^ (m >> 2), y ^ ((m >> 1) & 1), c ^ (m & 1)
            peer = 4 * px + 2 * py + pc
            copies.append(pltpu.make_async_remote_copy(
                src_ref=src_ref if gather else src_ref.at[peer], dst_ref=out_ref.at[me],
                send_sem=send_sems.at[m], recv_sem=recv_sems.at[m],
                device_id=(px, py, pc), device_id_type=pl.DeviceIdType.MESH))
        for cp in copies:
            cp.start()
        for cp in copies:
            cp.wait_recv()
        for cp in copies:
            cp.wait_send()
        local.wait()

    return pl.pallas_call(
        body, name=name,
        in_specs=[pl.BlockSpec(memory_space=pl.ANY)], out_specs=pl.BlockSpec(memory_space=pl.ANY),
        out_shape=jax.ShapeDtypeStruct((N_DEV, r, LANES), src.dtype),
        scratch_shapes=[pltpu.SemaphoreType.DMA((N_DEV,)), pltpu.SemaphoreType.DMA((N_DEV,)),
                        pltpu.SemaphoreType.DMA(())],
        compiler_params=pltpu.CompilerParams(has_side_effects=True),
    )(src)


def all_gather_rows(name, shard):
    return _peer_exchange(name, shard, shard.shape[0], True)


def all_to_all_rows(name, blocks):
    return _peer_exchange(name, blocks, blocks.shape[1], False)


def adamw_sum(parts, w, m, v):
    r = w.shape[0]
    tr = _largest_tile(r, 512, SUBLANES)

    def body(p_ref, w_ref, m_ref, v_ref, g_ref, d_ref, nm_ref, nv_ref):
        g = p_ref[0]
        for j in range(1, N_DEV):
            g = g + p_ref[j]
        mm_ = ADAM_B1 * m_ref[...] + (1.0 - ADAM_B1) * g
        vv = ADAM_B2 * v_ref[...] + (1.0 - ADAM_B2) * (g * g)
        m_hat = mm_ / (1.0 - ADAM_B1 ** ADAM_STEP)
        v_hat = vv / (1.0 - ADAM_B2 ** ADAM_STEP)
        g_ref[...] = g
        d_ref[...] = -ADAM_LR * (m_hat / (jnp.sqrt(v_hat) + ADAM_EPS) + ADAM_WD * w_ref[...])
        nm_ref[...] = mm_
        nv_ref[...] = vv

    row = pl.BlockSpec((tr, LANES), lambda i: (i, 0))
    shp = jax.ShapeDtypeStruct((r, LANES), F32)
    return pl.pallas_call(
        body, name="adamw_sum", grid=(r // tr,),
        in_specs=[pl.BlockSpec((N_DEV, tr, LANES), lambda i: (0, i, 0)), row, row, row],
        out_specs=[row] * 4, out_shape=[shp] * 4,
        compiler_params=_params(("parallel",)),
    )(parts, w, m, v)


WEIGHTS = ['l0_w_in', 'rwkv_mix', 'rwkv_w0', 'rwkv_w2', 'rwkv_a0', 'rwkv_a2', 'rwkv_g2', 'rwkv_k_k', 'rwkv_k_a',
           'rwkv_r_k', 'rwkv_ln_g', 'rwkv_ln_b', 'ssm_conv_w', 'ssm_conv_b', 'ssm_dt_bias', 'ssm_a_log', 'ssm_d',
           'ssm_norm_g', 'l0_w_out', 'l0_ln1_g', 'l0_ln1_b', 'ffn0_w_up', 'ffn0_conv_w', 'ffn0_conv_b',
           'ffn0_w_down', 'l0_ln2_g', 'l0_ln2_b', 'l1_w_in', 'mla_q_norm_g', 'mla_w_uq', 'mla_kv_norm_g',
           'mla_w_ukv', 'l1_w_out', 'l1_ln1_g', 'l1_ln1_b', 'ffn1_w_up', 'ffn1_conv_w', 'ffn1_conv_b',
           'ffn1_w_down', 'l1_ln2_g', 'l1_ln2_b']
SHARD_AXIS = {'l0_w_in': 1, 'rwkv_w2': 1, 'rwkv_a2': 1, 'rwkv_g2': 1, 'ssm_conv_w': 1, 'l0_w_out': 0,
              'ffn0_w_up': 1, 'ffn0_conv_w': 1, 'ffn0_w_down': 0, 'l1_w_in': 1, 'mla_w_uq': 1, 'mla_w_ukv': 1,
              'l1_w_out': 0, 'ffn1_w_up': 1, 'ffn1_conv_w': 1, 'ffn1_w_down': 0}
MATMUL_W = ['l0_w_in', 'rwkv_w2', 'rwkv_a2', 'rwkv_g2', 'l0_w_out', 'ffn0_w_up', 'ffn0_w_down', 'l1_w_in',
            'mla_w_uq', 'mla_w_ukv', 'l1_w_out', 'ffn1_w_up', 'ffn1_w_down']
CONV_W = ['ssm_conv_w', 'ffn0_conv_w', 'ffn1_conv_w']
TOK_TILE = 256


def _flat_rows(parts, row_mult):
    flat = jnp.concatenate(parts, axis=-1)
    size = flat.shape[-1]
    unit = row_mult * LANES
    padded = -(-size // unit) * unit
    if padded != size:
        flat = jnp.pad(flat, [(0, 0)] * (flat.ndim - 1) + [(0, padded - size)])
    return flat.reshape(flat.shape[:-1] + (padded // LANES, LANES))


def _gather_weights(a, names, dtype, tag):
    row_mult = 16 if dtype == BF16 else SUBLANES
    shard = _flat_rows([a[nm].astype(dtype).ravel() for nm in names], row_mult)
    got = all_gather_rows(tag, shard).reshape(N_DEV, -1)
    out, off = {}, 0
    for nm in names:
        shp = a[nm].shape
        size = math.prod(shp)
        blk = got[:, off:off + size].reshape((N_DEV,) + shp)
        off += size
        out[nm] = jnp.concatenate([blk[k] for k in range(N_DEV)], axis=SHARD_AXIS[nm])
    return out


def _to_heads(t2, bsz, h):
    n, w = t2.shape
    return t2.reshape(bsz, n // bsz, h, w // h).transpose(0, 2, 1, 3)


def _from_heads(t4):
    b, h, t, d = t4.shape
    return t4.transpose(0, 2, 1, 3).reshape(b * t, h * d)


def _row(v):
    return v.reshape(1, -1)


def _pad_lanes(v):
    return jnp.pad(v.reshape(1, -1), ((0, 0), (0, LANES - v.size)))


def _local_step(a, w):
    x = a['x']
    bsz, t, d = x.shape
    n = bsz * t
    tm = TOK_TILE
    pos = a['positions'].reshape(bsz, t, 1)
    inv_freq = 1.0 / (ROPE_THETA ** (jnp.arange(0, MLA_ROPE, 2, dtype=F32) / MLA_ROPE))
    inv_freq = jnp.concatenate([inv_freq, inv_freq]).reshape(1, MLA_ROPE)
    target = a['loss_target'].reshape(n, d)

    wi0 = w['l0_w_in']
    win0 = jnp.concatenate([wi0[:, 0:1536], wi0[:, 1792:3328], wi0[:, 1536:1792], wi0[:, 3328:3336],
                            jnp.zeros((d, L0_PAD - 3336), wi0.dtype)], axis=1)
    wi1 = w['l1_w_in']
    win1 = jnp.concatenate([wi1, jnp.zeros((d, L1_PAD - 1952), wi1.dtype)], axis=1)
    w2p = jnp.concatenate([w['rwkv_w2'], jnp.zeros_like(w['rwkv_w2'])], axis=0)
    a2p = jnp.concatenate([jnp.zeros_like(w['rwkv_a2']), w['rwkv_a2']], axis=0)
    mix = a['rwkv_mix']
    taps = jnp.stack([mix, 1.0 - mix])
    zero_b = jnp.zeros((1, mix.size), F32)
    rw_map = lambda j: j + jnp.where(j >= 12, 12, 0)
    ssm_map = lambda j: j + 16
    gate_map = lambda j: j
    up_map = lambda j: j + D_FF // LANES
    dt_col = 3328 // LANES
    dtb, alog, dsk = _pad_lanes(a['ssm_dt_bias']), _pad_lanes(a['ssm_a_log']), _pad_lanes(a['ssm_d'])
    pre_p = [_row(a['rwkv_w0']), w2p, _row(a['rwkv_a0']), a2p, w['rwkv_g2'], _row(a['rwkv_k_k']), _row(a['rwkv_k_a'])]
    post_p = [_row(a['rwkv_ln_g']), _row(a['rwkv_ln_b']), _row(a['rwkv_r_k'])]
    sp_p = [dsk, _row(a['ssm_norm_g'])]
    mla_p = [_row(a['mla_q_norm_g']), w['mla_w_uq'], _row(a['mla_kv_norm_g']), w['mla_w_ukv']]

    def ln(name, h, y, layer, which):
        ps = [_row(a[f'l{layer}_ln{which}_g']), _row(a[f'l{layer}_ln{which}_b'])]
        return tok_fwd(name, f_ln, [(h, d, 0), (y, d, 0)], ps, [d], tm)[0]

    def ffn_fwd(layer, h):
        up = mm(h, w[f'ffn{layer}_w_up'], 'nn', f'ffn{layer}_up')
        act = dwconv_fwd(f'ffn{layer}_conv', up.reshape(bsz, t, 2 * D_FF), gate_map, w[f'ffn{layer}_conv_w'],
                         _row(a[f'ffn{layer}_conv_b']), 3, True, upmap=up_map)
        act = act.reshape(n, D_FF)
        return up, act, mm(act, w[f'ffn{layer}_w_down'], 'nn', f'ffn{layer}_down')

    x2 = x.reshape(n, d)
    proj0 = mm(x2, win0, 'nn', 'l0_in')
    p0 = proj0.reshape(bsz, t, L0_PAD)
    xs_r = dwconv_fwd('rwkv_shift', p0, rw_map, taps, zero_b, 2, False).reshape(n, 1792)
    pre_x = [(xs_r, 512, 0), (xs_r, 512, 1), (xs_r, 512, 2), (xs_r, LANES, 12), (xs_r, LANES, 13)]
    r_, v_, lw, kmod, al, be, gt = tok_fwd('rwkv_pre', f_rwkv_pre, pre_x, pre_p, [RWKV_DIM] * 7, tm)
    scan_in = [_to_heads(u, bsz, RWKV_HEADS) for u in (r_, lw, kmod, v_, al, be)]
    y_h, rstates = rwkv_scan_fwd(*scan_in)
    y_r = _from_heads(y_h)
    post_x = [(y_r, 512, 0), (r_, 512, 0), (kmod, 512, 0), (v_, 512, 0), (gt, 512, 0)]
    y_a = tok_fwd('rwkv_post', f_rwkv_post, post_x, post_p, [RWKV_DIM], tm)[0]
    xbc = dwconv_fwd('ssm_conv', p0, ssm_map, w['ssm_conv_w'], _row(a['ssm_conv_b']), 4, True)
    ys, sstates = ssd_fwd(xbc, p0, dt_col, dtb, alog)
    xbc2 = xbc.reshape(n, 2 * SSM_DIM)
    sp_x = [(ys.reshape(n, SSM_DIM), 512, 0), (xbc2, 512, 0), (proj0, 512, 3)]
    y_b = tok_fwd('ssd_post', f_ssd_post, sp_x, sp_p, [SSM_DIM], tm)[0]
    wo0 = w['l0_w_out']
    mixed0 = mm(y_b, wo0[512:], 'nn', 'l0_out_b', add=mm(y_a, wo0[:512], 'nn', 'l0_out_a'))
    h1 = ln('l0_ln1', x2, mixed0, 0, 1)
    up0, act0, f0 = ffn_fwd(0, h1)
    h2 = ln('l0_ln2', h1, f0, 0, 2)

    proj1 = mm(h2, win1, 'nn', 'l1_in')
    q_sb, k_sb, v_sb = (_to_heads(proj1[:, i * 512:(i + 1) * 512], bsz, 8) for i in range(3))
    y_c = _from_heads(sb_fwd(q_sb, k_sb, v_sb))
    mla_x = [(proj1, 256, 6), (proj1, LANES, 14)]
    q_all, kv_all = tok_fwd('mla_pre', f_mla_pre, mla_x, mla_p, [768, 1024], tm)
    q4 = _to_heads(q_all, bsz, 8)
    kv4 = _to_heads(kv_all, bsz, 8)
    qn, qp_raw = q4[..., :MLA_NOPE], q4[..., MLA_NOPE:]
    kn, vv = kv4[..., :MLA_NOPE], kv4[..., MLA_NOPE:]
    kp_raw = proj1[:, 1920:1920 + MLA_ROPE].reshape(bsz, 1, t, MLA_ROPE)
    qp = rope('rope_q', qp_raw, pos, inv_freq, 1.0)
    kp = rope('rope_k', kp_raw, pos, inv_freq, 1.0)
    y_d = _from_heads(mla_fwd(qn, qp, kn, kp, vv))
    wo1 = w['l1_w_out']
    mixed1 = mm(y_d, wo1[512:], 'nn', 'l1_out_b', add=mm(y_c, wo1[:512], 'nn', 'l1_out_a'))
    h3 = ln('l1_ln1', h2, mixed1, 1, 1)
    up1, act1, f1 = ffn_fwd(1, h3)
    h4 = ln('l1_ln2', h3, f1, 1, 2)
    dh4, loss = loss_head(h4, target, tm)

    g = {}

    def ln_bwd(name, h, y, layer, which, dout):
        ps = [_row(a[f'l{layer}_ln{which}_g']), _row(a[f'l{layer}_ln{which}_b'])]
        (dh, dy), (dg, db) = tok_bwd(name, f_ln, [(h, d, 0), (y, d, 0)], ps, [[dout]], tm)
        g[f'l{layer}_ln{which}_g'], g[f'l{layer}_ln{which}_b'] = dg.reshape(-1), db.reshape(-1)
        return dh, dy

    def ffn_bwd(layer, h, up, act, df, dh_res):
        wup, wdown = w[f'ffn{layer}_w_up'], w[f'ffn{layer}_w_down']
        g[f'ffn{layer}_w_down'] = mm(act, df, 'tn', f'ffn{layer}_dwdown')
        dact = mm(df, wdown, 'nt', f'ffn{layer}_dact').reshape(bsz, t, D_FF)
        dgate, dcw, dcb, dup = dwconv_bwd(f'ffn{layer}_conv_bwd', up.reshape(bsz, t, 2 * D_FF), gate_map,
                                          w[f'ffn{layer}_conv_w'], _row(a[f'ffn{layer}_conv_b']), 3, True, dact,
                                          upmap=up_map)
        dgate, dup = dgate.reshape(n, D_FF), dup.reshape(n, D_FF)
        g[f'ffn{layer}_conv_w'], g[f'ffn{layer}_conv_b'] = dcw, dcb.reshape(-1)
        g[f'ffn{layer}_w_up'] = jnp.concatenate([mm(h, dgate, 'tn', f'ffn{layer}_dwgate'),
                                                 mm(h, dup, 'tn', f'ffn{layer}_dwup')], axis=1)
        dh = mm(dgate, wup[:, :D_FF], 'nt', f'ffn{layer}_dh_gate', add=dh_res)
        return mm(dup, wup[:, D_FF:], 'nt', f'ffn{layer}_dh_up', add=dh)

    dh3_res, df1 = ln_bwd('l1_ln2_bwd', h3, f1, 1, 2, dh4)
    dh3 = ffn_bwd(1, h3, up1, act1, df1, dh3_res)
    dh2_res, dmixed1 = ln_bwd('l1_ln1_bwd', h2, mixed1, 1, 1, dh3)
    g['l1_w_out'] = jnp.concatenate([mm(y_c, dmixed1, 'tn', 'l1_dwout_a'), mm(y_d, dmixed1, 'tn', 'l1_dwout_b')],
                                    axis=0)
    dy_c = mm(dmixed1, wo1[:512], 'nt', 'l1_dy_c')
    dy_d = mm(dmixed1, wo1[512:], 'nt', 'l1_dy_d')
    dq_sb, dk_sb, dv_sb = sb_bwd(q_sb, k_sb, v_sb, _to_heads(dy_c, bsz, 8))
    dqn, dqp, dkn, dkp, dvv = mla_bwd(qn, qp, kn, kp, vv, _to_heads(dy_d, bsz, 8))
    dqp_raw = rope('rope_q_bwd', dqp, pos, inv_freq, -1.0)
    dkp_raw = rope('rope_k_bwd', dkp, pos, inv_freq, -1.0).reshape(n, MLA_ROPE)
    dq_all = _from_heads(jnp.concatenate([dqn, dqp_raw], axis=-1))
    dkv_all = _from_heads(jnp.concatenate([dkn, dvv], axis=-1))
    (dcq, dckv), (dqg, dwuq, dkvg, dwukv) = tok_bwd('mla_pre_bwd', f_mla_pre, mla_x, mla_p,
                                                    [[dq_all], [dkv_all]], tm)
    g['mla_q_norm_g'], g['mla_w_uq'] = dqg.reshape(-1), dwuq
    g['mla_kv_norm_g'], g['mla_w_ukv'] = dkvg.reshape(-1), dwukv
    dproj1 = jnp.concatenate([_from_heads(dq_sb), _from_heads(dk_sb), _from_heads(dv_sb), dcq, dckv,
                              jnp.pad(dkp_raw, ((0, 0), (0, LANES - MLA_ROPE)))], axis=1)
    g['l1_w_in'] = mm(h2, dproj1, 'tn', 'l1_dwin')[:, :1952]
    dh2 = mm(dproj1, win1, 'nt', 'l1_dh', add=dh2_res)

    dh1_res, df0 = ln_bwd('l0_ln2_bwd', h1, f0, 0, 2, dh2)
    dh1 = ffn_bwd(0, h1, up0, act0, df0, dh1_res)
    dx_res, dmixed0 = ln_bwd('l0_ln1_bwd', x2, mixed0, 0, 1, dh1)
    g['l0_w_out'] = jnp.concatenate([mm(y_a, dmixed0, 'tn', 'l0_dwout_a'), mm(y_b, dmixed0, 'tn', 'l0_dwout_b')],
                                    axis=0)
    dy_a = mm(dmixed0, wo0[:512], 'nt', 'l0_dy_a')
    dy_b = mm(dmixed0, wo0[512:], 'nt', 'l0_dy_b')
    (dy_r, dr1, dkm1, dv1, dgt), (dlng, dlnb, drk) = tok_bwd('rwkv_post_bwd', f_rwkv_post, post_x, post_p,
                                                            [[dy_a]], tm)
    g['rwkv_ln_g'], g['rwkv_ln_b'] = dlng.reshape(-1), dlnb.reshape(-1)
    g['rwkv_r_k'] = drk.reshape(RWKV_HEADS, HEAD_DIM)
    dscan = rwkv_scan_bwd(*scan_in, rstates, _to_heads(dy_r, bsz, RWKV_HEADS))
    dr2, dlw, dk2, dv2, dal, dbe = (_from_heads(u) for u in dscan)
    pre_ct = [[dr1, dr2], [dv1, dv2], [dlw], [dkm1, dk2], [dal], [dbe], [dgt]]
    dpre_x, dpre_p = tok_bwd('rwkv_pre_bwd', f_rwkv_pre, pre_x, pre_p, pre_ct, tm)
    g['rwkv_w0'], g['rwkv_a0'] = dpre_p[0].reshape(-1), dpre_p[2].reshape(-1)
    g['rwkv_w2'], g['rwkv_a2'], g['rwkv_g2'] = dpre_p[1][:64], dpre_p[3][64:], dpre_p[4]
    g['rwkv_k_k'], g['rwkv_k_a'] = dpre_p[5].reshape(-1), dpre_p[6].reshape(-1)
    dxs_r = jnp.concatenate(dpre_x, axis=1).reshape(bsz, t, 1792)
    d_rw, dtaps, _ = dwconv_bwd('rwkv_shift_bwd', p0, rw_map, taps, zero_b, 2, False, dxs_r)
    d_rw = d_rw.reshape(n, 1792)
    g['rwkv_mix'] = dtaps[0] - dtaps[1]
    (dys, dxs_skip, dz), (ddsk, dng) = tok_bwd('ssd_post_bwd', f_ssd_post, sp_x, sp_p, [[dy_b]], tm)
    g['ssm_d'], g['ssm_norm_g'] = ddsk[0, :SSM_HEADS], dng.reshape(-1)
    dxbc_act, ddtr, ddtb, dalog = ssd_bwd(xbc, p0, dt_col, dtb, alog, sstates, dys.reshape(bsz, t, SSM_DIM),
                                          dxs_skip.reshape(bsz, t, SSM_DIM))
    g['ssm_dt_bias'], g['ssm_a_log'] = ddtb[0, :SSM_HEADS], dalog[0, :SSM_HEADS]
    dxbc, dscw, dscb = dwconv_bwd('ssm_conv_bwd', p0, ssm_map, w['ssm_conv_w'], _row(a['ssm_conv_b']), 4, True,
                                  dxbc_act)
    g['ssm_conv_w'], g['ssm_conv_b'] = dscw, dscb.reshape(-1)
    dproj0 = jnp.concatenate([d_rw[:, :1536], dz, dxbc.reshape(n, 2 * SSM_DIM), d_rw[:, 1536:],
                              ddtr.reshape(n, LANES)], axis=1)
    dwin0 = mm(x2, dproj0, 'tn', 'l0_dwin')
    g['l0_w_in'] = jnp.concatenate([dwin0[:, 0:1536], dwin0[:, 3072:3328], dwin0[:, 1536:3072],
                                    dwin0[:, 3328:3336]], axis=1)
    dx = mm(dproj0, win0, 'nt', 'l0_dx', add=dx_res)
    return loss, dx.reshape(bsz, t, d), g


def _step(a):
    w = _gather_weights(a, MATMUL_W, BF16, 'gather_matmul_weights')
    w.update(_gather_weights(a, CONV_W, F32, 'gather_conv_weights'))
    loss, dx, g = _local_step(a, w)
    loss = lax.psum(loss, ('x', 'y', 'c'))

    def shards(nm, full):
        if nm not in SHARD_AXIS:
            return jnp.broadcast_to(full.reshape(1, -1), (N_DEV, full.size))
        if SHARD_AXIS[nm] == 0:
            return full.reshape(N_DEV, -1)
        rows, cols = full.shape
        return full.reshape(rows, N_DEV, cols // N_DEV).transpose(1, 0, 2).reshape(N_DEV, -1)

    send = _flat_rows([shards(nm, g[nm].astype(F32)) for nm in WEIGHTS], SUBLANES)
    parts = all_to_all_rows('grad_exchange', send)
    flat = lambda prefix: _flat_rows([a[prefix + nm].ravel() for nm in WEIGHTS], SUBLANES)
    outs = adamw_sum(parts, flat(''), flat('m_'), flat('v_'))
    res = []
    for o in outs:
        o = o.reshape(-1)
        off = 0
        for nm in WEIGHTS:
            shp = a[nm].shape
            res.append(o[off:off + math.prod(shp)].reshape(shp))
            off += math.prod(shp)
    return (loss, dx, *res)


def kernel(x, positions, l0_w_in, rwkv_mix, rwkv_w0, rwkv_w2, rwkv_a0, rwkv_a2, rwkv_g2, rwkv_k_k, rwkv_k_a, rwkv_r_k, rwkv_ln_g, rwkv_ln_b, ssm_conv_w, ssm_conv_b, ssm_dt_bias, ssm_a_log, ssm_d, ssm_norm_g, l0_w_out, l0_ln1_g, l0_ln1_b, ffn0_w_up, ffn0_conv_w, ffn0_conv_b, ffn0_w_down, l0_ln2_g, l0_ln2_b, l1_w_in, mla_q_norm_g, mla_w_uq, mla_kv_norm_g, mla_w_ukv, l1_w_out, l1_ln1_g, l1_ln1_b, ffn1_w_up, ffn1_conv_w, ffn1_conv_b, ffn1_w_down, l1_ln2_g, l1_ln2_b, loss_target, m_l0_w_in, m_rwkv_mix, m_rwkv_w0, m_rwkv_w2, m_rwkv_a0, m_rwkv_a2, m_rwkv_g2, m_rwkv_k_k, m_rwkv_k_a, m_rwkv_r_k, m_rwkv_ln_g, m_rwkv_ln_b, m_ssm_conv_w, m_ssm_conv_b, m_ssm_dt_bias, m_ssm_a_log, m_ssm_d, m_ssm_norm_g, m_l0_w_out, m_l0_ln1_g, m_l0_ln1_b, m_ffn0_w_up, m_ffn0_conv_w, m_ffn0_conv_b, m_ffn0_w_down, m_l0_ln2_g, m_l0_ln2_b, m_l1_w_in, m_mla_q_norm_g, m_mla_w_uq, m_mla_kv_norm_g, m_mla_w_ukv, m_l1_w_out, m_l1_ln1_g, m_l1_ln1_b, m_ffn1_w_up, m_ffn1_conv_w, m_ffn1_conv_b, m_ffn1_w_down, m_l1_ln2_g, m_l1_ln2_b, v_l0_w_in, v_rwkv_mix, v_rwkv_w0, v_rwkv_w2, v_rwkv_a0, v_rwkv_a2, v_rwkv_g2, v_rwkv_k_k, v_rwkv_k_a, v_rwkv_r_k, v_rwkv_ln_g, v_rwkv_ln_b, v_ssm_conv_w, v_ssm_conv_b, v_ssm_dt_bias, v_ssm_a_log, v_ssm_d, v_ssm_norm_g, v_l0_w_out, v_l0_ln1_g, v_l0_ln1_b, v_ffn0_w_up, v_ffn0_conv_w, v_ffn0_conv_b, v_ffn0_w_down, v_l0_ln2_g, v_l0_ln2_b, v_l1_w_in, v_mla_q_norm_g, v_mla_w_uq, v_mla_kv_norm_g, v_mla_w_ukv, v_l1_w_out, v_l1_ln1_g, v_l1_ln1_b, v_ffn1_w_up, v_ffn1_conv_w, v_ffn1_conv_b, v_ffn1_w_down, v_l1_ln2_g, v_l1_ln2_b):
    return _step(dict(locals()))
```

```python
import functools
import math

import jax
import jax.numpy as jnp
from jax import lax
from jax.experimental import pallas as pl
from jax.experimental.pallas import tpu as pltpu

F32 = jnp.float32
BF16 = jnp.bfloat16
HI = lax.Precision.HIGHEST

V7X_VMEM_BYTES = 64 * 1024 * 1024
VMEM_LIMIT = V7X_VMEM_BYTES - 8 * 1024 * 1024
LANES = 128
SUBLANES = 8
N_DEV = 8

D_MODEL = 1024
HEAD_DIM = 64
RWKV_DIM = 512
RWKV_HEADS = 8
RWKV_GN_EPS = 64e-5
RWKV_CHUNK = 64
SSM_DIM = 512
SSM_HEADS = 8
SSM_CHUNK = 128
SSM_STATE = 128
Q_BLOCK = 128
SB_HEADS_PER_STEP = 4
MLA_NOPE = 64
MLA_ROPE = 32
ROPE_THETA = 10000.0
D_FF = 2816
DEPTH = 2
ALPHA = (2 * DEPTH) ** 0.25
L0_PAD = 3456
L1_PAD = 2048

ADAM_LR = 0.001
ADAM_B1 = 0.9
ADAM_B2 = 0.999
ADAM_EPS = 1e-08
ADAM_WD = 0.01
ADAM_STEP = 10

NEG_BIG = -1e30


def _params(sem=None):
    return pltpu.CompilerParams(dimension_semantics=sem, vmem_limit_bytes=VMEM_LIMIT)


P_F32, P_BF16, P_BF16X3 = 0, 1, 2


def _dg_raw(a, b, ca, cb, fast):
    dims = (((ca,), (cb,)), ((), ()))
    if fast == P_BF16:
        return lax.dot_general(a.astype(BF16), b.astype(BF16), dims, preferred_element_type=F32)
    prec = HI if fast == P_F32 else lax.Precision.HIGH
    return lax.dot_general(a, b, dims, precision=prec, preferred_element_type=F32)


@functools.partial(jax.custom_vjp, nondiff_argnums=(2, 3, 4))
def dg(a, b, ca, cb, fast):
    return _dg_raw(a, b, ca, cb, fast)


def _dg_fwd(a, b, ca, cb, fast):
    return _dg_raw(a, b, ca, cb, fast), (a, b)


def _dg_bwd(ca, cb, fast, res, ct):
    a, b = res
    fa, fb = 1 - ca, 1 - cb
    da = _dg_raw(ct, b, 1, fb, fast) if ca == 1 else _dg_raw(b, ct, fb, 1, fast)
    db = _dg_raw(a, ct, fa, 0, fast) if cb == 0 else _dg_raw(ct, a, 0, fa, fast)
    return da.astype(a.dtype), db.astype(b.dtype)


dg.defvjp(_dg_fwd, _dg_bwd)


def mmb(a, b):
    return dg(a, b, 1, 0, P_BF16)


def mmf(a, b):
    return dg(a, b, 1, 0, P_F32)


def mmf_nt(a, b):
    return dg(a, b, 1, 1, P_F32)


def mmf_tn(a, b):
    return dg(a, b, 0, 0, P_F32)


def mm3(a, b):
    return dg(a, b, 1, 0, P_BF16X3)


def mm3_nt(a, b):
    return dg(a, b, 1, 1, P_BF16X3)


def mm3_tn(a, b):
    return dg(a, b, 0, 0, P_BF16X3)


def _split3_dot(x, m01, cb):
    hi = x.astype(BF16)
    r1 = x - hi.astype(F32)
    mid = r1.astype(BF16)
    lo = (r1 - mid.astype(F32)).astype(BF16)
    rows = x.shape[0]
    out = lax.dot_general(jnp.concatenate([hi, mid, lo], axis=0), m01.astype(BF16), (((1,), (cb,)), ((), ())),
                          preferred_element_type=F32)
    return out[:rows] + out[rows:2 * rows] + out[2 * rows:]


def _lower_ones(n):
    return jnp.where(_iota((n, n), 0) >= _iota((n, n), 1), 1.0, 0.0)


@jax.custom_vjp
def suffix_sum(x):
    return _split3_dot(x, _lower_ones(x.shape[1]), 0)


def _suffix_sum_fwd(x):
    return suffix_sum(x), None


def _suffix_sum_bwd(_, ct):
    return (_split3_dot(ct, _lower_ones(ct.shape[1]), 1),)


suffix_sum.defvjp(_suffix_sum_fwd, _suffix_sum_bwd)


def _iota(shape, dim):
    return lax.broadcasted_iota(jnp.int32, shape, dim)


def _softplus(x):
    return jnp.maximum(x, 0.0) + jnp.log1p(jnp.exp(-jnp.abs(x)))


def _silu(x):
    return x * jax.nn.sigmoid(x)


def _largest_tile(n, cap, mult):
    best = None
    t = mult
    while t <= min(n, cap):
        if n % t == 0:
            best = t
        t += mult
    return n if best is None else best


def mm(a, b, mode, name, add=None):
    if mode == "nn":
        (M, K), N = a.shape, b.shape[1]
    elif mode == "nt":
        (M, K), N = a.shape, b.shape[0]
    else:
        (K, M), N = a.shape, b.shape[1]
    tm = _largest_tile(M, 1024, LANES)
    tn = _largest_tile(N, 512, LANES)
    tk = _largest_tile(K, 512, LANES)
    nk = K // tk
    if mode == "nn":
        a_spec = pl.BlockSpec((tm, tk), lambda i, j, k: (i, k))
        b_spec = pl.BlockSpec((tk, tn), lambda i, j, k: (k, j))
        dims = (((1,), (0,)), ((), ()))
    elif mode == "nt":
        a_spec = pl.BlockSpec((tm, tk), lambda i, j, k: (i, k))
        b_spec = pl.BlockSpec((tn, tk), lambda i, j, k: (j, k))
        dims = (((1,), (1,)), ((), ()))
    else:
        a_spec = pl.BlockSpec((tk, tm), lambda i, j, k: (k, i))
        b_spec = pl.BlockSpec((tk, tn), lambda i, j, k: (k, j))
        dims = (((0,), (0,)), ((), ()))
    o_spec = pl.BlockSpec((tm, tn), lambda i, j, k: (i, j))
    has_add = add is not None

    def body(a_ref, b_ref, *rest):
        o_ref = rest[-1]
        k = pl.program_id(2)
        part = lax.dot_general(a_ref[...].astype(BF16), b_ref[...].astype(BF16), dims,
                               preferred_element_type=F32)

        @pl.when(k == 0)
        def _():
            o_ref[...] = part + rest[0][...] if has_add else part

        @pl.when(k > 0)
        def _():
            o_ref[...] += part

    ins = [a, b] + ([add] if has_add else [])
    specs = [a_spec, b_spec] + ([o_spec] if has_add else [])
    return pl.pallas_call(
        body, name=name, grid=(M // tm, N // tn, nk), in_specs=specs, out_specs=o_spec,
        out_shape=jax.ShapeDtypeStruct((M, N), F32),
        compiler_params=_params(("parallel", "parallel", "arbitrary")),
    )(*ins)


def _x_specs(xs, tm):
    return [pl.BlockSpec((tm, w), functools.partial(lambda i, cb: (i, cb), cb=cb)) for _, w, cb in xs]


def _p_specs(ps):
    return [pl.BlockSpec(p.shape, lambda i: (0, 0)) for p in ps]


def tok_fwd(name, f, xs, ps, out_widths, tm):
    n = xs[0][0].shape[0]
    nx, npar = len(xs), len(ps)

    def body(*refs):
        xv = [r[...] for r in refs[:nx]]
        pv = [r[...].astype(F32) for r in refs[nx:nx + npar]]
        outs = f(*xv, *pv)
        for o, r in zip(outs, refs[nx + npar:]):
            r[...] = o

    return pl.pallas_call(
        body, name=name, grid=(n // tm,),
        in_specs=_x_specs(xs, tm) + _p_specs(ps),
        out_specs=[pl.BlockSpec((tm, w), lambda i: (i, 0)) for w in out_widths],
        out_shape=[jax.ShapeDtypeStruct((n, w), F32) for w in out_widths],
        compiler_params=_params(("parallel",)),
    )(*[x[0] for x in xs], *ps)


def tok_bwd(name, f, xs, ps, cts, tm):
    n = xs[0][0].shape[0]
    nx, npar = len(xs), len(ps)
    ct_flat = [c for group in cts for c in group]
    nct = len(ct_flat)

    def body(*refs):
        xv = [r[...] for r in refs[:nx]]
        pv = [r[...].astype(F32) for r in refs[nx:nx + npar]]
        ct_refs = refs[nx + npar:nx + npar + nct]
        dx_refs = refs[nx + npar + nct:nx + npar + nct + nx]
        dp_refs = refs[nx + npar + nct + nx:]
        cv, pos = [], 0
        for group in cts:
            acc = ct_refs[pos][...]
            for r in ct_refs[pos + 1:pos + len(group)]:
                acc = acc + r[...]
            cv.append(acc)
            pos += len(group)
        _, vjp = jax.vjp(f, *xv, *pv)
        grads = vjp(tuple(cv))
        for g, r in zip(grads[:nx], dx_refs):
            r[...] = g

        @pl.when(pl.program_id(0) == 0)
        def _():
            for r in dp_refs:
                r[...] = jnp.zeros_like(r)

        for g, r in zip(grads[nx:], dp_refs):
            r[...] += g

    outs = pl.pallas_call(
        body, name=name, grid=(n // tm,),
        in_specs=(_x_specs(xs, tm) + _p_specs(ps)
                  + [pl.BlockSpec((tm, c.shape[1]), lambda i: (i, 0)) for c in ct_flat]),
        out_specs=([pl.BlockSpec((tm, w), lambda i: (i, 0)) for _, w, _ in xs] + _p_specs(ps)),
        out_shape=([jax.ShapeDtypeStruct((n, w), F32) for _, w, _ in xs]
                   + [jax.ShapeDtypeStruct(p.shape, F32) for p in ps]),
        compiler_params=_params(("arbitrary",)),
    )(*[x[0] for x in xs], *ps, *ct_flat)
    return outs[:nx], outs[nx:]


def f_ln(h, y, g, b):
    pre = ALPHA * h + y
    mu = jnp.mean(pre, axis=-1, keepdims=True)
    xc = pre - mu
    var = jnp.mean(xc * xc, axis=-1, keepdims=True)
    return (xc * lax.rsqrt(var + 1e-5) * g + b,)


def _head_sel(width, nheads_pad, per):
    return jnp.where(_iota((width, nheads_pad), 0) // per == _iota((width, nheads_pad), 1), 1.0, 0.0).astype(F32)


def _head_sel_t(nheads_pad, width, per):
    return jnp.where(_iota((nheads_pad, width), 1) // per == _iota((nheads_pad, width), 0), 1.0, 0.0).astype(F32)


def f_rwkv_pre(r, k, v, lora, glo, w0, w2p, a0, a2p, g2, k_k, k_a):
    lane = _iota(lora.shape, 1)
    tw = jnp.where(lane < 64, jnp.tanh(lora), 0.0)
    ta = jnp.where(lane >= 64, lora, 0.0)
    log_w = -_softplus(-(w0 + mmb(tw, w2p))) - 0.5
    lw = -jnp.exp(log_w)
    a = jax.nn.sigmoid(a0 + mmb(ta, a2p))
    g = mmb(jax.nn.sigmoid(glo), g2)
    sel = _head_sel(RWKV_DIM, LANES, HEAD_DIM)
    sel_t = _head_sel_t(LANES, RWKV_DIM, HEAD_DIM)
    kk = k * k_k
    nrm = jnp.sqrt(jnp.maximum(mmf(kk * kk, sel), 1e-24))
    kkn = kk * mmf(1.0 / nrm, sel_t)
    kmod = k * (1.0 + (a - 1.0) * k_a)
    return r, v, lw, kmod, -kkn, kkn * a, g


def f_rwkv_post(y, r, kmod, v, g, ln_g, ln_b, r_k):
    sel = _head_sel(RWKV_DIM, LANES, HEAD_DIM)
    sel_t = _head_sel_t(LANES, RWKV_DIM, HEAD_DIM)
    inv = 1.0 / HEAD_DIM
    mu = mmf(mmf(y, sel) * inv, sel_t)
    yc = y - mu
    var = mmf(yc * yc, sel) * inv
    rstd = mmf(lax.rsqrt(var + RWKV_GN_EPS), sel_t)
    yn = yc * rstd * ln_g + ln_b
    bonus = mmf(mmf(r * kmod * r_k, sel), sel_t) * v
    return ((yn + bonus) * g,)


def f_ssd_post(y, xs, z, d_skip, norm_g):
    sel_t = _head_sel_t(LANES, SSM_DIM, HEAD_DIM)
    d_e = jnp.sum(mmf(jnp.broadcast_to(d_skip, (SUBLANES, LANES)), sel_t), axis=0, keepdims=True) * (1.0 / SUBLANES)
    u = (y + xs * d_e) * _silu(z)
    first = _iota(u.shape, 1) < (SSM_DIM // 2)
    uu = u * u
    inv = 2.0 / SSM_DIM
    ms0 = jnp.sum(jnp.where(first, uu, 0.0), axis=-1, keepdims=True) * inv
    ms1 = jnp.sum(jnp.where(first, 0.0, uu), axis=-1, keepdims=True) * inv
    ms = jnp.where(first, ms0, ms1)
    return (u * lax.rsqrt(ms + 1e-5) * norm_g,)


def f_mla_pre(cq, ckv, qg, w_uq, kvg, w_ukv):
    def rms(x, g):
        return x * lax.rsqrt(jnp.mean(x * x, axis=-1, keepdims=True) + 1e-6) * g
    return mmb(rms(cq, qg), w_uq), mmb(rms(ckv, kvg), w_ukv)


def _shift_down(x, s, row):
    return x if s == 0 else jnp.where(row >= s, pltpu.roll(x, s, 0), 0.0)


def _shift_up(x, s, row, t):
    return x if s == 0 else jnp.where(row < t - s, pltpu.roll(x, t - s, 0), 0.0)


def dwconv_fwd(name, u, colmap, w, b, taps, silu, upmap=None):
    bsz, t, _ = u.shape
    c = w.shape[1]
    tc = LANES
    has_up = upmap is not None

    def body(*refs):
        u_ref, w_ref, b_ref = refs[:3]
        o_ref = refs[-1]
        uv = u_ref[...]
        wv = w_ref[...]
        row = _iota(uv.shape, 0)
        acc = jnp.broadcast_to(b_ref[...], uv.shape)
        for i in range(taps):
            acc = acc + wv[i:i + 1, :] * _shift_down(uv, taps - 1 - i, row)
        if silu:
            acc = _silu(acc)
        if has_up:
            acc = acc * refs[3][...]
        o_ref[...] = acc

    specs = [pl.BlockSpec((None, t, tc), lambda bb, j: (bb, 0, colmap(j))),
             pl.BlockSpec((taps, tc), lambda bb, j: (0, j)),
             pl.BlockSpec((1, tc), lambda bb, j: (0, j))]
    ins = [u, w, b]
    if has_up:
        specs.append(pl.BlockSpec((None, t, tc), lambda bb, j: (bb, 0, upmap(j))))
        ins.append(u)
    return pl.pallas_call(
        body, name=name, grid=(bsz, c // tc), in_specs=specs,
        out_specs=pl.BlockSpec((None, t, tc), lambda bb, j: (bb, 0, j)),
        out_shape=jax.ShapeDtypeStruct((bsz, t, c), F32),
        compiler_params=_params(("parallel", "parallel")),
    )(*ins)


def dwconv_bwd(name, u, colmap, w, b, taps, silu, dout, upmap=None):
    bsz, t, _ = u.shape
    c = w.shape[1]
    tc = LANES
    has_up = upmap is not None

    def body(*refs):
        u_ref, w_ref, b_ref, d_ref = refs[:4]
        nin = 5 if has_up else 4
        du_ref, dw_ref, db_ref = refs[nin:nin + 3]
        uv = u_ref[...]
        wv = w_ref[...]
        dv = d_ref[...]
        row = _iota(uv.shape, 0)
        shifted = [_shift_down(uv, taps - 1 - i, row) for i in range(taps)]
        cg = jnp.broadcast_to(b_ref[...], uv.shape)
        for i in range(taps):
            cg = cg + wv[i:i + 1, :] * shifted[i]
        if silu:
            sg = jax.nn.sigmoid(cg)
            act = cg * sg
            dact_dcg = sg * (1.0 + cg * (1.0 - sg))
        else:
            act = cg
            dact_dcg = None
        if has_up:
            refs[nin + 3][...] = dv * act
            dv = dv * refs[4][...]
        dcg = dv * dact_dcg if silu else dv
        du = jnp.zeros_like(uv)
        for i in range(taps):
            du = du + wv[i:i + 1, :] * _shift_up(dcg, taps - 1 - i, row, t)
        du_ref[...] = du

        @pl.when(pl.program_id(1) == 0)
        def _():
            dw_ref[...] = jnp.zeros_like(dw_ref)
            db_ref[...] = jnp.zeros_like(db_ref)

        for i in range(taps):
            dw_ref[i:i + 1, :] += jnp.sum(dcg * shifted[i], axis=0, keepdims=True)
        db_ref[...] += jnp.sum(dcg, axis=0, keepdims=True)

    specs = [pl.BlockSpec((None, t, tc), lambda j, bb: (bb, 0, colmap(j))),
             pl.BlockSpec((taps, tc), lambda j, bb: (0, j)),
             pl.BlockSpec((1, tc), lambda j, bb: (0, j)),
             pl.BlockSpec((None, t, tc), lambda j, bb: (bb, 0, j))]
    ins = [u, w, b, dout]
    if has_up:
        specs.append(pl.BlockSpec((None, t, tc), lambda j, bb: (bb, 0, upmap(j))))
        ins.append(u)
    big = pl.BlockSpec((None, t, tc), lambda j, bb: (bb, 0, j))
    out_specs = [big, pl.BlockSpec((taps, tc), lambda j, bb: (0, j)), pl.BlockSpec((1, tc), lambda j, bb: (0, j))]
    out_shape = [jax.ShapeDtypeStruct((bsz, t, c), F32), jax.ShapeDtypeStruct((taps, c), F32),
                 jax.ShapeDtypeStruct((1, c), F32)]
    if has_up:
        out_specs.append(big)
        out_shape.append(jax.ShapeDtypeStruct((bsz, t, c), F32))
    return pl.pallas_call(
        body, name=name, grid=(c // tc, bsz), in_specs=specs, out_specs=out_specs, out_shape=out_shape,
        compiler_params=_params(("parallel", "arbitrary")),
    )(*ins)


def _each(f, *lists):
    return [f(*xs) for xs in zip(*lists)]


def rwkv_chunk(s0, r, lw, k, v, al, be):
    c = r[0].shape[0]
    ii, jj = _iota((c, c), 0), _iota((c, c), 1)
    incl, strict = ii >= jj, ii > jj
    ones_incl = jnp.where(incl, 1.0, 0.0)
    eye = jnp.where(ii == jj, 1.0, 0.0)
    cum = _each(lambda x: mmf(ones_incl, x), lw)
    gam_inv = _each(lambda x: jnp.exp(-x), cum)
    at = _each(lambda a_, c_, l_: a_ * jnp.exp(c_ - l_), al, cum, lw)
    rt = _each(lambda r_, c_: r_ * jnp.exp(c_), r, cum)
    bt = _each(lambda b_, g_: b_ * g_, be, gam_inv)
    kt = _each(lambda k_, g_: k_ * g_, k, gam_inv)
    a_b = _each(lambda x, y_: jnp.where(strict, mm3_nt(x, y_), 0.0), at, bt)
    a_k = _each(lambda x, y_: jnp.where(strict, mm3_nt(x, y_), 0.0), at, kt)
    rhs0 = _each(mm3_nt, at, s0)
    rhs = _each(lambda x, a_, v_: x + mm3(a_, v_), rhs0, a_k, v)
    p = _each(lambda x: eye + x, a_b)
    m = a_b
    for _ in range(int(math.log2(c)) - 1):
        m = _each(mm3, m, m)
        p = _each(lambda p_, m_: p_ + mm3(p_, m_), p, m)
    u = _each(mm3, p, rhs)
    r_b = _each(lambda x, y_: jnp.where(incl, mm3_nt(x, y_), 0.0), rt, bt)
    r_k = _each(lambda x, y_: jnp.where(incl, mm3_nt(x, y_), 0.0), rt, kt)
    y0 = _each(mm3_nt, rt, s0)
    y1 = _each(lambda y_, b_, u_: y_ + mm3(b_, u_), y0, r_b, u)
    y = _each(lambda y_, k_, v_: y_ + mm3(k_, v_), y1, r_k, v)
    su = _each(mm3_tn, u, bt)
    sv = _each(mm3_tn, v, kt)
    s1 = _each(lambda s_, a_, b_, l_: (s_ + a_ + b_) * jnp.exp(jnp.sum(l_, axis=0, keepdims=True)), s0, su, sv, lw)
    return y, s1


def rwkv_scan_fwd(r, lw, k, v, al, be):
    bsz, h, t, d = r.shape
    c = RWKV_CHUNK
    nc = t // c

    def body(r_ref, lw_ref, k_ref, v_ref, al_ref, be_ref, y_ref, st_ref, s_scr):
        @pl.when(pl.program_id(1) == 0)
        def _():
            s_scr[...] = jnp.zeros_like(s_scr)

        heads = lambda ref: [ref[hh] for hh in range(h)]
        s0 = heads(s_scr)
        y, s1 = rwkv_chunk(s0, heads(r_ref), heads(lw_ref), heads(k_ref), heads(v_ref), heads(al_ref),
                           heads(be_ref))
        for hh in range(h):
            st_ref[hh] = s0[hh]
            y_ref[hh] = y[hh]
            s_scr[hh] = s1[hh]

    seq = pl.BlockSpec((None, h, c, d), lambda b, i: (b, 0, i, 0))
    return pl.pallas_call(
        body, name="rwkv_scan_fwd", grid=(bsz, nc), in_specs=[seq] * 6,
        out_specs=[seq, pl.BlockSpec((None, h, None, d, d), lambda b, i: (b, 0, i, 0, 0))],
        out_shape=[jax.ShapeDtypeStruct((bsz, h, t, d), F32), jax.ShapeDtypeStruct((bsz, h, nc, d, d), F32)],
        scratch_shapes=[pltpu.VMEM((h, d, d), F32)],
        compiler_params=_params(("parallel", "arbitrary")),
    )(r, lw, k, v, al, be)


def rwkv_scan_bwd(r, lw, k, v, al, be, states, dy):
    bsz, h, t, d = r.shape
    c = RWKV_CHUNK
    nc = t // c

    def body(r_ref, lw_ref, k_ref, v_ref, al_ref, be_ref, st_ref, dy_ref,
             dr_ref, dlw_ref, dk_ref, dv_ref, dal_ref, dbe_ref, ds_scr):
        @pl.when(pl.program_id(1) == 0)
        def _():
            ds_scr[...] = jnp.zeros_like(ds_scr)

        heads = lambda ref: [ref[hh] for hh in range(h)]
        _, vjp = jax.vjp(rwkv_chunk, heads(st_ref), heads(r_ref), heads(lw_ref), heads(k_ref), heads(v_ref),
                         heads(al_ref), heads(be_ref))
        grads = vjp((heads(dy_ref), heads(ds_scr)))
        for ref, gl in zip((ds_scr, dr_ref, dlw_ref, dk_ref, dv_ref, dal_ref, dbe_ref), grads):
            for hh in range(h):
                ref[hh] = gl[hh]

    seq = pl.BlockSpec((None, h, c, d), lambda b, i: (b, 0, nc - 1 - i, 0))
    st = pl.BlockSpec((None, h, None, d, d), lambda b, i: (b, 0, nc - 1 - i, 0, 0))
    return pl.pallas_call(
        body, name="rwkv_scan_bwd", grid=(bsz, nc), in_specs=[seq] * 6 + [st, seq],
        out_specs=[seq] * 6, out_shape=[jax.ShapeDtypeStruct((bsz, h, t, d), F32)] * 6,
        scratch_shapes=[pltpu.VMEM((h, d, d), F32)],
        compiler_params=_params(("parallel", "arbitrary")),
    )(r, lw, k, v, al, be, states, dy)


def ssd_chunk(st, xs, bm, cm, dtr, dt_bias, a_log):
    n = SSM_CHUNK
    ii, jj = _iota((n, n), 0), _iota((n, n), 1)
    incl = ii >= jj
    lane = _iota((n, LANES), 1)
    dt = _softplus(dtr + dt_bias)
    a = dt * (-jnp.exp(a_log))
    acum = mmf(jnp.where(incl, 1.0, 0.0), a)
    last_row = jnp.where(jj == n - 1, 1.0, 0.0)
    cb = [mmf_nt(cm[g], bm[g]) for g in range(2)]
    pairs, heads = range(4), range(SSM_HEADS)
    e_m = [jnp.where(_iota((LANES, LANES), 0) == 2 * m + _iota((LANES, LANES), 1) // HEAD_DIM, 1.0, 0.0)
           for m in pairs]
    dt_m = [mmf(dt, e_m[m]) for m in pairs]
    ac_m = [mmf(acum, e_m[m]) for m in pairs]
    x = [xs[m] * dt_m[m] for m in pairs]
    last_m = [mmf(last_row, ac_m[m]) for m in pairs]
    colb = [mmf(acum, jnp.where(_iota((LANES, n), 0) == h, 1.0, 0.0)) for h in heads]
    decay = [jnp.exp(jnp.where(incl, colb[h] - colb[h].T, NEG_BIG)) for h in heads]
    yh = [mmf(cb[h // 4] * decay[h], x[h // 2]) for h in heads]
    y_off = [mmf(cm[m // 2], st[m]) for m in pairs]
    ys = [jnp.where(lane // HEAD_DIM == 0, yh[2 * m], yh[2 * m + 1]) + jnp.exp(ac_m[m]) * y_off[m] for m in pairs]
    st_in = [mmf_tn(bm[m // 2], x[m] * jnp.exp(last_m[m] - ac_m[m])) for m in pairs]
    st_new = [jnp.exp(last_m[m]) * st[m] + st_in[m] for m in pairs]
    return tuple(ys), tuple(st_new)


def _ssd_load(xbc_ref, dtr_ref):
    xs = tuple(xbc_ref[:, m * LANES:(m + 1) * LANES] for m in range(4))
    bm = tuple(xbc_ref[:, SSM_DIM + g * LANES:SSM_DIM + (g + 1) * LANES] for g in range(2))
    cm = tuple(xbc_ref[:, SSM_DIM + 2 * LANES + g * LANES:SSM_DIM + 2 * LANES + (g + 1) * LANES] for g in range(2))
    return xs, bm, cm, dtr_ref[...]


def ssd_fwd(xbc, proj, dt_col, dt_bias, a_log):
    bsz, t, _ = xbc.shape
    n = SSM_CHUNK
    nc = t // n

    def body(xbc_ref, dtr_ref, dtb_ref, al_ref, y_ref, st_ref, s_scr):
        @pl.when(pl.program_id(1) == 0)
        def _():
            s_scr[...] = jnp.zeros_like(s_scr)

        st = tuple(s_scr[m] for m in range(4))
        for m in range(4):
            st_ref[m] = st[m]
        xs, bm, cm, dtr = _ssd_load(xbc_ref, dtr_ref)
        ys, st_new = ssd_chunk(st, xs, bm, cm, dtr, dtb_ref[...], al_ref[...])
        for m in range(4):
            y_ref[:, m * LANES:(m + 1) * LANES] = ys[m]
            s_scr[m] = st_new[m]

    vec = pl.BlockSpec((1, LANES), lambda b, i: (0, 0))
    return pl.pallas_call(
        body, name="ssd_fwd", grid=(bsz, nc),
        in_specs=[pl.BlockSpec((None, n, 2 * SSM_DIM), lambda b, i: (b, i, 0)),
                  pl.BlockSpec((None, n, LANES), lambda b, i: (b, i, dt_col)), vec, vec],
        out_specs=[pl.BlockSpec((None, n, SSM_DIM), lambda b, i: (b, i, 0)),
                   pl.BlockSpec((None, None, 4, SSM_STATE, LANES), lambda b, i: (b, i, 0, 0, 0))],
        out_shape=[jax.ShapeDtypeStruct((bsz, t, SSM_DIM), F32),
                   jax.ShapeDtypeStruct((bsz, nc, 4, SSM_STATE, LANES), F32)],
        scratch_shapes=[pltpu.VMEM((4, SSM_STATE, LANES), F32)],
        compiler_params=_params(("parallel", "arbitrary")),
    )(xbc, proj, dt_bias, a_log)


def ssd_bwd(xbc, proj, dt_col, dt_bias, a_log, states, dy, dxs_extra):
    bsz, t, _ = xbc.shape
    n = SSM_CHUNK
    nc = t // n

    def body(xbc_ref, dtr_ref, dtb_ref, al_ref, st_ref, dy_ref, ex_ref,
             dxbc_ref, ddtr_ref, ddtb_ref, dal_ref, ds_scr):
        first = jnp.logical_and(pl.program_id(0) == 0, pl.program_id(1) == 0)

        @pl.when(pl.program_id(1) == 0)
        def _():
            ds_scr[...] = jnp.zeros_like(ds_scr)

        @pl.when(first)
        def _():
            ddtb_ref[...] = jnp.zeros_like(ddtb_ref)
            dal_ref[...] = jnp.zeros_like(dal_ref)

        st = tuple(st_ref[m] for m in range(4))
        xs, bm, cm, dtr = _ssd_load(xbc_ref, dtr_ref)
        _, vjp = jax.vjp(ssd_chunk, st, xs, bm, cm, dtr, dtb_ref[...], al_ref[...])
        dys = tuple(dy_ref[:, m * LANES:(m + 1) * LANES] for m in range(4))
        dst_in = tuple(ds_scr[m] for m in range(4))
        dst, dxs, dbm, dcm, ddtr, ddtb, dal = vjp((dys, dst_in))
        for m in range(4):
            ds_scr[m] = dst[m]
            sl = slice(m * LANES, (m + 1) * LANES)
            dxbc_ref[:, sl] = dxs[m] + ex_ref[:, sl]
        for g in range(2):
            dxbc_ref[:, SSM_DIM + g * LANES:SSM_DIM + (g + 1) * LANES] = dbm[g]
            dxbc_ref[:, SSM_DIM + 2 * LANES + g * LANES:SSM_DIM + 2 * LANES + (g + 1) * LANES] = dcm[g]
        ddtr_ref[...] = ddtr
        ddtb_ref[...] += ddtb
        dal_ref[...] += dal

    vec = pl.BlockSpec((1, LANES), lambda b, i: (0, 0))
    rev = lambda b, i: (b, nc - 1 - i, 0)
    return pl.pallas_call(
        body, name="ssd_bwd", grid=(bsz, nc),
        in_specs=[pl.BlockSpec((None, n, 2 * SSM_DIM), rev),
                  pl.BlockSpec((None, n, LANES), lambda b, i: (b, nc - 1 - i, dt_col)), vec, vec,
                  pl.BlockSpec((None, None, 4, SSM_STATE, LANES), lambda b, i: (b, nc - 1 - i, 0, 0, 0)),
                  pl.BlockSpec((None, n, SSM_DIM), rev), pl.BlockSpec((None, n, SSM_DIM), rev)],
        out_specs=[pl.BlockSpec((None, n, 2 * SSM_DIM), rev), pl.BlockSpec((None, n, LANES), rev), vec, vec],
        out_shape=[jax.ShapeDtypeStruct((bsz, t, 2 * SSM_DIM), F32), jax.ShapeDtypeStruct((bsz, t, LANES), F32),
                   jax.ShapeDtypeStruct((1, LANES), F32), jax.ShapeDtypeStruct((1, LANES), F32)],
        scratch_shapes=[pltpu.VMEM((4, SSM_STATE, LANES), F32)],
        compiler_params=_params(("arbitrary", "arbitrary")),
    )(xbc, proj, dt_bias, a_log, states, dy, dxs_extra)


def sb_block(q, kj, vj, carry, maskf):
    mask = maskf > 0.5
    z = _each(lambda q_, k_: dg(q_, k_, 1, 1, P_BF16) * (HEAD_DIM ** -0.5), q, kj)
    ls = _each(lambda z_: -_softplus(-z_), z)
    lk = _each(lambda l_, z_: jnp.where(mask, l_ - z_, 0.0), ls, z)
    sfx = _each(suffix_sum, lk)
    att = _each(lambda l_, c_, s_, k_: jnp.where(mask, jnp.exp(l_ + c_ + s_ - k_), 0.0), ls, carry, sfx, lk)
    out = _each(mmb, att, vj)
    return out, _each(lambda c_, k_: c_ + jnp.sum(k_, axis=1, keepdims=True), carry, lk)


def _sb_mask(qi, j):
    n = Q_BLOCK
    return jnp.where(j * n + _iota((n, n), 1) < qi * n + _iota((n, n), 0), 1.0, 0.0)


def sb_fwd(q, k, v):
    bsz, h, t, d = q.shape
    n = Q_BLOCK

    hp = SB_HEADS_PER_STEP

    def body(q_ref, k_ref, v_ref, o_ref):
        qi = pl.program_id(2)

        def step(i, state):
            acc, carry = state
            j = qi - i
            rows = pl.ds(pl.multiple_of(j * n, n), n)
            o, carry = sb_block([q_ref[hh] for hh in range(hp)], [k_ref[hh, rows, :] for hh in range(hp)],
                                [v_ref[hh, rows, :] for hh in range(hp)], carry, _sb_mask(qi, j))
            return [a_ + o_ for a_, o_ in zip(acc, o)], carry

        init = ([jnp.zeros((n, d), F32) for _ in range(hp)], [jnp.zeros((n, 1), F32) for _ in range(hp)])
        acc, _ = lax.fori_loop(0, qi + 1, step, init)
        for hh in range(hp):
            o_ref[hh] = acc[hh]

    blk = pl.BlockSpec((None, hp, n, d), lambda b, hg, i: (b, hg, i, 0))
    full = pl.BlockSpec((None, hp, t, d), lambda b, hg, i: (b, hg, 0, 0))
    return pl.pallas_call(
        body, name="sb_fwd", grid=(bsz, h // hp, t // n), in_specs=[blk, full, full], out_specs=blk,
        out_shape=jax.ShapeDtypeStruct((bsz, h, t, d), F32),
        compiler_params=_params(("parallel", "parallel", "arbitrary")),
    )(q, k, v)


def sb_bwd(q, k, v, do):
    bsz, h, t, d = q.shape
    n = Q_BLOCK

    hp = SB_HEADS_PER_STEP

    def body(q_ref, k_ref, v_ref, do_ref, dq_ref, dk_ref, dv_ref, carries):
        qi = pl.program_id(2)

        @pl.when(qi == 0)
        def _():
            dk_ref[...] = jnp.zeros_like(dk_ref)
            dv_ref[...] = jnp.zeros_like(dv_ref)

        heads = range(hp)
        qv = [q_ref[hh] for hh in heads]

        def fwd_step(i, carry):
            j = qi - i
            rows = pl.ds(pl.multiple_of(j * n, n), n)
            for hh in heads:
                carries[hh, j] = carry[hh]
            return sb_block(qv, [k_ref[hh, rows, :] for hh in heads], [v_ref[hh, rows, :] for hh in heads],
                            carry, _sb_mask(qi, j))[1]

        lax.fori_loop(0, qi + 1, fwd_step, [jnp.zeros((n, 1), F32) for _ in heads])

        def bwd_step(j, state):
            dq, dcarry = state
            rows = pl.ds(pl.multiple_of(j * n, n), n)
            _, vjp = jax.vjp(sb_block, qv, [k_ref[hh, rows, :] for hh in heads],
                             [v_ref[hh, rows, :] for hh in heads], [carries[hh, j] for hh in heads], _sb_mask(qi, j))
            dqj, dkj, dvj, dc, _ = vjp(([do_ref[hh] for hh in heads], dcarry))
            for hh in heads:
                dk_ref[hh, rows, :] += dkj[hh]
                dv_ref[hh, rows, :] += dvj[hh]
            return [a_ + b_ for a_, b_ in zip(dq, dqj)], dc

        init = ([jnp.zeros((n, d), F32) for _ in heads], [jnp.zeros((n, 1), F32) for _ in heads])
        dq, _ = lax.fori_loop(0, qi + 1, bwd_step, init)
        for hh in heads:
            dq_ref[hh] = dq[hh]

    blk = pl.BlockSpec((None, hp, n, d), lambda b, hg, i: (b, hg, i, 0))
    full = pl.BlockSpec((None, hp, t, d), lambda b, hg, i: (b, hg, 0, 0))
    shp = jax.ShapeDtypeStruct((bsz, h, t, d), F32)
    return pl.pallas_call(
        body, name="sb_bwd", grid=(bsz, h // hp, t // n), in_specs=[blk, full, full, blk],
        out_specs=[blk, full, full], out_shape=[shp, shp, shp],
        scratch_shapes=[pltpu.VMEM((hp, t // n, n, 1), F32)],
        compiler_params=_params(("parallel", "parallel", "arbitrary")),
    )(q, k, v, do)


def mla_block(qn, qp, kn, kp, v, maskf):
    scale = (MLA_NOPE + MLA_ROPE) ** -0.5
    s = (dg(qn, kn, 1, 1, P_BF16) + dg(qp, kp, 1, 1, P_BF16)) * scale
    s = jnp.where(maskf > 0.5, s, NEG_BIG)
    p = jnp.exp(s - jnp.max(s, axis=-1, keepdims=True))
    prob = p / jnp.sum(p, axis=-1, keepdims=True)
    return mmb(prob, v)


def _mla_mask(qi, t):
    n = Q_BLOCK
    return jnp.where(_iota((n, t), 1) <= qi * n + _iota((n, t), 0), 1.0, 0.0)


def _mla_specs(t):
    n = Q_BLOCK
    return (pl.BlockSpec((None, None, n, MLA_NOPE), lambda b, hh, i: (b, hh, i, 0)),
            pl.BlockSpec((None, None, n, MLA_ROPE), lambda b, hh, i: (b, hh, i, 0)),
            pl.BlockSpec((None, None, t, MLA_NOPE), lambda b, hh, i: (b, hh, 0, 0)),
            pl.BlockSpec((None, None, t, MLA_ROPE), lambda b, hh, i: (b, 0, 0, 0)))


def mla_fwd(qn, qp, kn, kp, v):
    bsz, h, t, _ = qn.shape

    def body(qn_ref, qp_ref, kn_ref, kp_ref, v_ref, o_ref):
        o_ref[...] = mla_block(qn_ref[...], qp_ref[...], kn_ref[...], kp_ref[...], v_ref[...],
                               _mla_mask(pl.program_id(2), t))

    qn_s, qp_s, kn_s, kp_s = _mla_specs(t)
    return pl.pallas_call(
        body, name="mla_fwd", grid=(bsz, h, t // Q_BLOCK), in_specs=[qn_s, qp_s, kn_s, kp_s, kn_s],
        out_specs=qn_s, out_shape=jax.ShapeDtypeStruct(qn.shape, F32),
        compiler_params=_params(("parallel", "parallel", "arbitrary")),
    )(qn, qp, kn, kp, v)


def mla_bwd(qn, qp, kn, kp, v, do):
    bsz, h, t, _ = qn.shape

    def body(qn_ref, qp_ref, kn_ref, kp_ref, v_ref, do_ref, dqn_ref, dqp_ref, dkn_ref, dkp_ref, dv_ref):
        hh, qi = pl.program_id(1), pl.program_id(2)

        @pl.when(qi == 0)
        def _():
            dkn_ref[...] = jnp.zeros_like(dkn_ref)
            dv_ref[...] = jnp.zeros_like(dv_ref)

        @pl.when(jnp.logical_and(qi == 0, hh == 0))
        def _():
            dkp_ref[...] = jnp.zeros_like(dkp_ref)

        _, vjp = jax.vjp(mla_block, qn_ref[...], qp_ref[...], kn_ref[...], kp_ref[...], v_ref[...],
                         _mla_mask(qi, t))
        dqn, dqp, dkn, dkp, dv, _ = vjp(do_ref[...])
        dqn_ref[...] = dqn
        dqp_ref[...] = dqp
        dkn_ref[...] += dkn
        dkp_ref[...] += dkp
        dv_ref[...] += dv

    qn_s, qp_s, kn_s, kp_s = _mla_specs(t)
    return pl.pallas_call(
        body, name="mla_bwd", grid=(bsz, h, t // Q_BLOCK), in_specs=[qn_s, qp_s, kn_s, kp_s, kn_s, qn_s],
        out_specs=[qn_s, qp_s, kn_s, kp_s, kn_s],
        out_shape=[jax.ShapeDtypeStruct(qn.shape, F32), jax.ShapeDtypeStruct(qp.shape, F32),
                   jax.ShapeDtypeStruct(kn.shape, F32), jax.ShapeDtypeStruct(kp.shape, F32),
                   jax.ShapeDtypeStruct(v.shape, F32)],
        compiler_params=_params(("parallel", "arbitrary", "arbitrary")),
    )(qn, qp, kn, kp, v, do)


def rope(name, x, pos, inv_freq, sign):
    bsz, hx, t, d = x.shape
    half = d // 2

    def body(x_ref, pos_ref, f_ref, o_ref):
        xv = x_ref[...]
        ang = pos_ref[...].astype(F32) * f_ref[...]
        ri, ci = _iota((d, d), 0), _iota((d, d), 1)
        rot = jnp.where(ri == ci + half, -1.0, 0.0) + jnp.where(ri + half == ci, 1.0, 0.0)
        o_ref[...] = xv * jnp.cos(ang) + sign * (mmf(xv, rot) * jnp.sin(ang))

    return pl.pallas_call(
        body, name=name, grid=(bsz, hx),
        in_specs=[pl.BlockSpec((None, None, t, d), lambda b, hh: (b, hh, 0, 0)),
                  pl.BlockSpec((None, t, 1), lambda b, hh: (b, 0, 0)),
                  pl.BlockSpec((1, d), lambda b, hh: (0, 0))],
        out_specs=pl.BlockSpec((None, None, t, d), lambda b, hh: (b, hh, 0, 0)),
        out_shape=jax.ShapeDtypeStruct(x.shape, F32),
        compiler_params=_params(("parallel", "parallel")),
    )(x, pos, inv_freq)


def loss_head(h, target, tm):
    n, d = h.shape

    def body(h_ref, t_ref, dh_ref, l_ref):
        @pl.when(pl.program_id(0) == 0)
        def _():
            l_ref[...] = jnp.zeros_like(l_ref)

        e = h_ref[...] - t_ref[...]
        dh_ref[...] = e * (1.0 / d)
        l_ref[...] += jnp.sum(e * e, axis=(0, 1), keepdims=True) * (0.5 / d)

    row = pl.BlockSpec((tm, d), lambda i: (i, 0))
    dh, l = pl.pallas_call(
        body, name="loss_head", grid=(n // tm,), in_specs=[row, row],
        out_specs=[row, pl.BlockSpec((SUBLANES, LANES), lambda i: (0, 0))],
        out_shape=[jax.ShapeDtypeStruct((n, d), F32), jax.ShapeDtypeStruct((SUBLANES, LANES), F32)],
        compiler_params=_params(("arbitrary",)),
    )(h, target)
    return dh, l[0, 0]


def _peer_exchange(name, src, out_rows, gather):
    r = src.shape[-2]

    def body(src_ref, out_ref, send_sems, recv_sems, local_sem):
        x, y, c = lax.axis_index("x"), lax.axis_index("y"), lax.axis_index("c")
        me = 4 * x + 2 * y + c
        local = pltpu.make_async_copy(src_ref if gather else src_ref.at[me], out_ref.at[me], local_sem)
        local.start()
        copies = []
        for m in range(1, N_DEV):
            px, py, pc = x ^ (m >> 2), y ^ ((m >> 1) & 1), c ^ (m & 1)
            peer = 4 * px + 2 * py + pc
            copies.append(pltpu.make_async_remote_copy(
                src_ref=src_ref if gather else src_ref.at[peer], dst_ref=out_ref.at[me],
                send_sem=send_sems.at[m], recv_sem=recv_sems.at[m],
                device_id=(px, py, pc), device_id_type=pl.DeviceIdType.MESH))
        for cp in copies:
            cp.start()
        for cp in copies:
            cp.wait_recv()
        for cp in copies:
            cp.wait_send()
        local.wait()

    return pl.pallas_call(
        body, name=name,
        in_specs=[pl.BlockSpec(memory_space=pl.ANY)], out_specs=pl.BlockSpec(memory_space=pl.ANY),
        out_shape=jax.ShapeDtypeStruct((N_DEV, r, LANES), src.dtype),
        scratch_shapes=[pltpu.SemaphoreType.DMA((N_DEV,)), pltpu.SemaphoreType.DMA((N_DEV,)),
                        pltpu.SemaphoreType.DMA(())],
        compiler_params=pltpu.CompilerParams(has_side_effects=True),
    )(src)


def all_gather_rows(name, shard):
    return _peer_exchange(name, shard, shard.shape[0], True)


def all_to_all_rows(name, blocks):
    return _peer_exchange(name, blocks, blocks.shape[1], False)


def adamw_sum(parts, w, m, v):
    r = w.shape[0]
    tr = ADAM_ROWS
    assert r % tr == 0

    def body(p_ref, w_ref, m_ref, v_ref, g_ref, d_ref, nm_ref, nv_ref):
        g = p_ref[0]
        for j in range(1, N_DEV):
            g = g + p_ref[j]
        mm_ = ADAM_B1 * m_ref[...] + (1.0 - ADAM_B1) * g
        vv = ADAM_B2 * v_ref[...] + (1.0 - ADAM_B2) * (g * g)
        m_hat = mm_ / (1.0 - ADAM_B1 ** ADAM_STEP)
        v_hat = vv / (1.0 - ADAM_B2 ** ADAM_STEP)
        g_ref[...] = g
        d_ref[...] = -ADAM_LR * (m_hat / (jnp.sqrt(v_hat) + ADAM_EPS) + ADAM_WD * w_ref[...])
        nm_ref[...] = mm_
        nv_ref[...] = vv

    row = pl.BlockSpec((tr, LANES), lambda i: (i, 0))
    shp = jax.ShapeDtypeStruct((r, LANES), F32)
    return pl.pallas_call(
        body, name="adamw_sum", grid=(r // tr,),
        in_specs=[pl.BlockSpec((N_DEV, tr, LANES), lambda i: (0, i, 0)), row, row, row],
        out_specs=[row] * 4, out_shape=[shp] * 4,
        compiler_params=_params(("parallel",)),
    )(parts, w, m, v)


WEIGHTS = ['l0_w_in', 'rwkv_mix', 'rwkv_w0', 'rwkv_w2', 'rwkv_a0', 'rwkv_a2', 'rwkv_g2', 'rwkv_k_k', 'rwkv_k_a',
           'rwkv_r_k', 'rwkv_ln_g', 'rwkv_ln_b', 'ssm_conv_w', 'ssm_conv_b', 'ssm_dt_bias', 'ssm_a_log', 'ssm_d',
           'ssm_norm_g', 'l0_w_out', 'l0_ln1_g', 'l0_ln1_b', 'ffn0_w_up', 'ffn0_conv_w', 'ffn0_conv_b',
           'ffn0_w_down', 'l0_ln2_g', 'l0_ln2_b', 'l1_w_in', 'mla_q_norm_g', 'mla_w_uq', 'mla_kv_norm_g',
           'mla_w_ukv', 'l1_w_out', 'l1_ln1_g', 'l1_ln1_b', 'ffn1_w_up', 'ffn1_conv_w', 'ffn1_conv_b',
           'ffn1_w_down', 'l1_ln2_g', 'l1_ln2_b']
SHARD_AXIS = {'l0_w_in': 1, 'rwkv_w2': 1, 'rwkv_a2': 1, 'rwkv_g2': 1, 'ssm_conv_w': 1, 'l0_w_out': 0,
              'ffn0_w_up': 1, 'ffn0_conv_w': 1, 'ffn0_w_down': 0, 'l1_w_in': 1, 'mla_w_uq': 1, 'mla_w_ukv': 1,
              'l1_w_out': 0, 'ffn1_w_up': 1, 'ffn1_conv_w': 1, 'ffn1_w_down': 0}
MATMUL_W = ['l0_w_in', 'rwkv_w2', 'rwkv_a2', 'rwkv_g2', 'l0_w_out', 'ffn0_w_up', 'ffn0_w_down', 'l1_w_in',
            'mla_w_uq', 'mla_w_ukv', 'l1_w_out', 'ffn1_w_up', 'ffn1_w_down']
CONV_W = ['ssm_conv_w', 'ffn0_conv_w', 'ffn1_conv_w']
TOK_TILE = 256
ADAM_ROWS = 512


def _ceil_to(size, unit):
    return -(-size // unit) * unit


def _flat_rows(pieces, seg_rows, total_rows):
    unit = seg_rows * LANES
    out, total = [], 0
    for p in pieces:
        p = p.reshape(-1)
        pad = _ceil_to(p.size, unit) - p.size
        out.append(jnp.pad(p, (0, pad)) if pad else p)
        total += p.size + pad
    tail = _ceil_to(total, total_rows * LANES) - total
    if tail:
        out.append(jnp.zeros((tail,), out[0].dtype))
    return jnp.concatenate(out).reshape(-1, LANES)


def _unflatten(flat2d, shapes, seg_rows):
    flat = flat2d.reshape(-1)
    out, off = [], 0
    for shp in shapes:
        size = math.prod(shp)
        out.append(flat[off:off + size].reshape(shp))
        off += _ceil_to(size, seg_rows * LANES)
    return out


def _gather_weights(a, names, dtype, tag):
    seg_rows = 16 if dtype == BF16 else SUBLANES
    shard = _flat_rows([a[nm].astype(dtype) for nm in names], seg_rows, seg_rows)
    got = all_gather_rows(tag, shard)
    blocks = [_unflatten(got[k], [a[nm].shape for nm in names], seg_rows) for k in range(N_DEV)]
    return {nm: jnp.concatenate([blocks[k][i] for k in range(N_DEV)], axis=SHARD_AXIS[nm])
            for i, nm in enumerate(names)}


def _to_heads(t2, bsz, h):
    n, w = t2.shape
    return t2.reshape(bsz, n // bsz, h, w // h).transpose(0, 2, 1, 3)


def _from_heads(t4):
    b, h, t, d = t4.shape
    return t4.transpose(0, 2, 1, 3).reshape(b * t, h * d)


def _row(v):
    return v.reshape(1, -1)


def _pad_lanes(v):
    return jnp.pad(v.reshape(1, -1), ((0, 0), (0, LANES - v.size)))


def _local_step(a, w):
    x = a['x']
    bsz, t, d = x.shape
    n = bsz * t
    tm = TOK_TILE
    pos = a['positions'].reshape(bsz, t, 1)
    inv_freq = 1.0 / (ROPE_THETA ** (jnp.arange(0, MLA_ROPE, 2, dtype=F32) / MLA_ROPE))
    inv_freq = jnp.concatenate([inv_freq, inv_freq]).reshape(1, MLA_ROPE)
    target = a['loss_target'].reshape(n, d)

    wi0 = w['l0_w_in']
    win0 = jnp.concatenate([wi0[:, 0:1536], wi0[:, 1792:3328], wi0[:, 1536:1792], wi0[:, 3328:3336],
                            jnp.zeros((d, L0_PAD - 3336), wi0.dtype)], axis=1)
    wi1 = w['l1_w_in']
    win1 = jnp.concatenate([wi1, jnp.zeros((d, L1_PAD - 1952), wi1.dtype)], axis=1)
    w2p = jnp.concatenate([w['rwkv_w2'], jnp.zeros_like(w['rwkv_w2'])], axis=0)
    a2p = jnp.concatenate([jnp.zeros_like(w['rwkv_a2']), w['rwkv_a2']], axis=0)
    mix = a['rwkv_mix']
    taps = jnp.stack([mix, 1.0 - mix])
    zero_b = jnp.zeros((1, mix.size), F32)
    rw_map = lambda j: j + jnp.where(j >= 12, 12, 0)
    ssm_map = lambda j: j + 16
    gate_map = lambda j: j
    up_map = lambda j: j + D_FF // LANES
    dt_col = 3328 // LANES
    dtb, alog, dsk = _pad_lanes(a['ssm_dt_bias']), _pad_lanes(a['ssm_a_log']), _pad_lanes(a['ssm_d'])
    pre_p = [_row(a['rwkv_w0']), w2p, _row(a['rwkv_a0']), a2p, w['rwkv_g2'], _row(a['rwkv_k_k']), _row(a['rwkv_k_a'])]
    post_p = [_row(a['rwkv_ln_g']), _row(a['rwkv_ln_b']), _row(a['rwkv_r_k'])]
    sp_p = [dsk, _row(a['ssm_norm_g'])]
    mla_p = [_row(a['mla_q_norm_g']), w['mla_w_uq'], _row(a['mla_kv_norm_g']), w['mla_w_ukv']]

    def ln(name, h, y, layer, which):
        ps = [_row(a[f'l{layer}_ln{which}_g']), _row(a[f'l{layer}_ln{which}_b'])]
        return tok_fwd(name, f_ln, [(h, d, 0), (y, d, 0)], ps, [d], tm)[0]

    def ffn_fwd(layer, h):
        up = mm(h, w[f'ffn{layer}_w_up'], 'nn', f'ffn{layer}_up')
        act = dwconv_fwd(f'ffn{layer}_conv', up.reshape(bsz, t, 2 * D_FF), gate_map, w[f'ffn{layer}_conv_w'],
                         _row(a[f'ffn{layer}_conv_b']), 3, True, upmap=up_map)
        act = act.reshape(n, D_FF)
        return up, act, mm(act, w[f'ffn{layer}_w_down'], 'nn', f'ffn{layer}_down')

    x2 = x.reshape(n, d)
    proj0 = mm(x2, win0, 'nn', 'l0_in')
    p0 = proj0.reshape(bsz, t, L0_PAD)
    xs_r = dwconv_fwd('rwkv_shift', p0, rw_map, taps, zero_b, 2, False).reshape(n, 1792)
    pre_x = [(xs_r, 512, 0), (xs_r, 512, 1), (xs_r, 512, 2), (xs_r, LANES, 12), (xs_r, LANES, 13)]
    r_, v_, lw, kmod, al, be, gt = tok_fwd('rwkv_pre', f_rwkv_pre, pre_x, pre_p, [RWKV_DIM] * 7, tm)
    scan_in = [_to_heads(u, bsz, RWKV_HEADS) for u in (r_, lw, kmod, v_, al, be)]
    y_h, rstates = rwkv_scan_fwd(*scan_in)
    y_r = _from_heads(y_h)
    post_x = [(y_r, 512, 0), (r_, 512, 0), (kmod, 512, 0), (v_, 512, 0), (gt, 512, 0)]
    y_a = tok_fwd('rwkv_post', f_rwkv_post, post_x, post_p, [RWKV_DIM], tm)[0]
    xbc = dwconv_fwd('ssm_conv', p0, ssm_map, w['ssm_conv_w'], _row(a['ssm_conv_b']), 4, True)
    ys, sstates = ssd_fwd(xbc, p0, dt_col, dtb, alog)
    xbc2 = xbc.reshape(n, 2 * SSM_DIM)
    sp_x = [(ys.reshape(n, SSM_DIM), 512, 0), (xbc2, 512, 0), (proj0, 512, 3)]
    y_b = tok_fwd('ssd_post', f_ssd_post, sp_x, sp_p, [SSM_DIM], tm)[0]
    wo0 = w['l0_w_out']
    mixed0 = mm(y_b, wo0[512:], 'nn', 'l0_out_b', add=mm(y_a, wo0[:512], 'nn', 'l0_out_a'))
    h1 = ln('l0_ln1', x2, mixed0, 0, 1)
    up0, act0, f0 = ffn_fwd(0, h1)
    h2 = ln('l0_ln2', h1, f0, 0, 2)

    proj1 = mm(h2, win1, 'nn', 'l1_in')
    q_sb, k_sb, v_sb = (_to_heads(proj1[:, i * 512:(i + 1) * 512], bsz, 8) for i in range(3))
    y_c = _from_heads(sb_fwd(q_sb, k_sb, v_sb))
    mla_x = [(proj1, 256, 6), (proj1, LANES, 14)]
    q_all, kv_all = tok_fwd('mla_pre', f_mla_pre, mla_x, mla_p, [768, 1024], tm)
    q4 = _to_heads(q_all, bsz, 8)
    kv4 = _to_heads(kv_all, bsz, 8)
    qn, qp_raw = q4[..., :MLA_NOPE], q4[..., MLA_NOPE:]
    kn, vv = kv4[..., :MLA_NOPE], kv4[..., MLA_NOPE:]
    kp_raw = proj1[:, 1920:1920 + MLA_ROPE].reshape(bsz, 1, t, MLA_ROPE)
    qp = rope('rope_q', qp_raw, pos, inv_freq, 1.0)
    kp = rope('rope_k', kp_raw, pos, inv_freq, 1.0)
    y_d = _from_heads(mla_fwd(qn, qp, kn, kp, vv))
    wo1 = w['l1_w_out']
    mixed1 = mm(y_d, wo1[512:], 'nn', 'l1_out_b', add=mm(y_c, wo1[:512], 'nn', 'l1_out_a'))
    h3 = ln('l1_ln1', h2, mixed1, 1, 1)
    up1, act1, f1 = ffn_fwd(1, h3)
    h4 = ln('l1_ln2', h3, f1, 1, 2)
    dh4, loss = loss_head(h4, target, tm)

    g = {}

    def ln_bwd(name, h, y, layer, which, dout):
        ps = [_row(a[f'l{layer}_ln{which}_g']), _row(a[f'l{layer}_ln{which}_b'])]
        (dh, dy), (dg, db) = tok_bwd(name, f_ln, [(h, d, 0), (y, d, 0)], ps, [[dout]], tm)
        g[f'l{layer}_ln{which}_g'], g[f'l{layer}_ln{which}_b'] = dg.reshape(-1), db.reshape(-1)
        return dh, dy

    def ffn_bwd(layer, h, up, act, df, dh_res):
        wup, wdown = w[f'ffn{layer}_w_up'], w[f'ffn{layer}_w_down']
        g[f'ffn{layer}_w_down'] = mm(act, df, 'tn', f'ffn{layer}_dwdown')
        dact = mm(df, wdown, 'nt', f'ffn{layer}_dact').reshape(bsz, t, D_FF)
        dgate, dcw, dcb, dup = dwconv_bwd(f'ffn{layer}_conv_bwd', up.reshape(bsz, t, 2 * D_FF), gate_map,
                                          w[f'ffn{layer}_conv_w'], _row(a[f'ffn{layer}_conv_b']), 3, True, dact,
                                          upmap=up_map)
        dgate, dup = dgate.reshape(n, D_FF), dup.reshape(n, D_FF)
        g[f'ffn{layer}_conv_w'], g[f'ffn{layer}_conv_b'] = dcw, dcb.reshape(-1)
        g[f'ffn{layer}_w_up'] = (mm(h, dgate, 'tn', f'ffn{layer}_dwgate'), mm(h, dup, 'tn', f'ffn{layer}_dwup'))
        dh = mm(dgate, wup[:, :D_FF], 'nt', f'ffn{layer}_dh_gate', add=dh_res)
        return mm(dup, wup[:, D_FF:], 'nt', f'ffn{layer}_dh_up', add=dh)

    dh3_res, df1 = ln_bwd('l1_ln2_bwd', h3, f1, 1, 2, dh4)
    dh3 = ffn_bwd(1, h3, up1, act1, df1, dh3_res)
    dh2_res, dmixed1 = ln_bwd('l1_ln1_bwd', h2, mixed1, 1, 1, dh3)
    g['l1_w_out'] = (mm(y_c, dmixed1, 'tn', 'l1_dwout_a'), mm(y_d, dmixed1, 'tn', 'l1_dwout_b'))
    dy_c = mm(dmixed1, wo1[:512], 'nt', 'l1_dy_c')
    dy_d = mm(dmixed1, wo1[512:], 'nt', 'l1_dy_d')
    dq_sb, dk_sb, dv_sb = sb_bwd(q_sb, k_sb, v_sb, _to_heads(dy_c, bsz, 8))
    dqn, dqp, dkn, dkp, dvv = mla_bwd(qn, qp, kn, kp, vv, _to_heads(dy_d, bsz, 8))
    dqp_raw = rope('rope_q_bwd', dqp, pos, inv_freq, -1.0)
    dkp_raw = rope('rope_k_bwd', dkp, pos, inv_freq, -1.0).reshape(n, MLA_ROPE)
    dq_all = _from_heads(jnp.concatenate([dqn, dqp_raw], axis=-1))
    dkv_all = _from_heads(jnp.concatenate([dkn, dvv], axis=-1))
    (dcq, dckv), (dqg, dwuq, dkvg, dwukv) = tok_bwd('mla_pre_bwd', f_mla_pre, mla_x, mla_p,
                                                    [[dq_all], [dkv_all]], tm)
    g['mla_q_norm_g'], g['mla_w_uq'] = dqg.reshape(-1), dwuq
    g['mla_kv_norm_g'], g['mla_w_ukv'] = dkvg.reshape(-1), dwukv
    dproj1 = jnp.concatenate([_from_heads(dq_sb), _from_heads(dk_sb), _from_heads(dv_sb), dcq, dckv,
                              jnp.pad(dkp_raw, ((0, 0), (0, LANES - MLA_ROPE)))], axis=1)
    g['l1_w_in'] = mm(h2, dproj1, 'tn', 'l1_dwin')[:, :1952]
    dh2 = mm(dproj1, win1, 'nt', 'l1_dh', add=dh2_res)

    dh1_res, df0 = ln_bwd('l0_ln2_bwd', h1, f0, 0, 2, dh2)
    dh1 = ffn_bwd(0, h1, up0, act0, df0, dh1_res)
    dx_res, dmixed0 = ln_bwd('l0_ln1_bwd', x2, mixed0, 0, 1, dh1)
    g['l0_w_out'] = (mm(y_a, dmixed0, 'tn', 'l0_dwout_a'), mm(y_b, dmixed0, 'tn', 'l0_dwout_b'))
    dy_a = mm(dmixed0, wo0[:512], 'nt', 'l0_dy_a')
    dy_b = mm(dmixed0, wo0[512:], 'nt', 'l0_dy_b')
    (dy_r, dr1, dkm1, dv1, dgt), (dlng, dlnb, drk) = tok_bwd('rwkv_post_bwd', f_rwkv_post, post_x, post_p,
                                                            [[dy_a]], tm)
    g['rwkv_ln_g'], g['rwkv_ln_b'] = dlng.reshape(-1), dlnb.reshape(-1)
    g['rwkv_r_k'] = drk.reshape(RWKV_HEADS, HEAD_DIM)
    dscan = rwkv_scan_bwd(*scan_in, rstates, _to_heads(dy_r, bsz, RWKV_HEADS))
    dr2, dlw, dk2, dv2, dal, dbe = (_from_heads(u) for u in dscan)
    pre_ct = [[dr1, dr2], [dv1, dv2], [dlw], [dkm1, dk2], [dal], [dbe], [dgt]]
    dpre_x, dpre_p = tok_bwd('rwkv_pre_bwd', f_rwkv_pre, pre_x, pre_p, pre_ct, tm)
    g['rwkv_w0'], g['rwkv_a0'] = dpre_p[0].reshape(-1), dpre_p[2].reshape(-1)
    g['rwkv_w2'], g['rwkv_a2'], g['rwkv_g2'] = dpre_p[1][:64], dpre_p[3][64:], dpre_p[4]
    g['rwkv_k_k'], g['rwkv_k_a'] = dpre_p[5].reshape(-1), dpre_p[6].reshape(-1)
    dxs_r = jnp.concatenate(dpre_x, axis=1).reshape(bsz, t, 1792)
    d_rw, dtaps, _ = dwconv_bwd('rwkv_shift_bwd', p0, rw_map, taps, zero_b, 2, False, dxs_r)
    d_rw = d_rw.reshape(n, 1792)
    g['rwkv_mix'] = dtaps[0] - dtaps[1]
    (dys, dxs_skip, dz), (ddsk, dng) = tok_bwd('ssd_post_bwd', f_ssd_post, sp_x, sp_p, [[dy_b]], tm)
    g['ssm_d'], g['ssm_norm_g'] = ddsk[0, :SSM_HEADS], dng.reshape(-1)
    dxbc_act, ddtr, ddtb, dalog = ssd_bwd(xbc, p0, dt_col, dtb, alog, sstates, dys.reshape(bsz, t, SSM_DIM),
                                          dxs_skip.reshape(bsz, t, SSM_DIM))
    g['ssm_dt_bias'], g['ssm_a_log'] = ddtb[0, :SSM_HEADS], dalog[0, :SSM_HEADS]
    dxbc, dscw, dscb = dwconv_bwd('ssm_conv_bwd', p0, ssm_map, w['ssm_conv_w'], _row(a['ssm_conv_b']), 4, True,
                                  dxbc_act)
    g['ssm_conv_w'], g['ssm_conv_b'] = dscw, dscb.reshape(-1)
    dproj0 = jnp.concatenate([d_rw[:, :1536], dz, dxbc.reshape(n, 2 * SSM_DIM), d_rw[:, 1536:],
                              ddtr.reshape(n, LANES)], axis=1)
    dwin0 = mm(x2, dproj0, 'tn', 'l0_dwin')
    g['l0_w_in'] = jnp.concatenate([dwin0[:, 0:1536], dwin0[:, 3072:3328], dwin0[:, 1536:3072],
                                    dwin0[:, 3328:3336]], axis=1)
    dx = mm(dproj0, win0, 'nt', 'l0_dx', add=dx_res)
    return loss, dx.reshape(bsz, t, d), g


def _step(a):
    w = _gather_weights(a, MATMUL_W, BF16, 'gather_matmul_weights')
    w.update(_gather_weights(a, CONV_W, F32, 'gather_conv_weights'))
    loss, dx, g = _local_step(a, w)
    loss = lax.psum(loss, ('x', 'y', 'c'))

    def shard_of(nm, k):
        gv = g[nm]
        if nm not in SHARD_AXIS:
            return gv
        per = N_DEV
        if isinstance(gv, tuple):
            gv, k, per = gv[k // 4], k % 4, 4
        width = gv.shape[SHARD_AXIS[nm]] // per
        return lax.slice_in_dim(gv, k * width, (k + 1) * width, axis=SHARD_AXIS[nm])

    send = jnp.stack([_flat_rows([shard_of(nm, k) for nm in WEIGHTS], SUBLANES, ADAM_ROWS) for k in range(N_DEV)])
    parts = all_to_all_rows('grad_exchange', send)
    flat = lambda prefix: _flat_rows([a[prefix + nm] for nm in WEIGHTS], SUBLANES, ADAM_ROWS)
    outs = adamw_sum(parts, flat(''), flat('m_'), flat('v_'))
    res = []
    for o in outs:
        res.extend(_unflatten(o, [a[nm].shape for nm in WEIGHTS], SUBLANES))
    return (loss, dx, *res)


def kernel(x, positions, l0_w_in, rwkv_mix, rwkv_w0, rwkv_w2, rwkv_a0, rwkv_a2, rwkv_g2, rwkv_k_k, rwkv_k_a, rwkv_r_k, rwkv_ln_g, rwkv_ln_b, ssm_conv_w, ssm_conv_b, ssm_dt_bias, ssm_a_log, ssm_d, ssm_norm_g, l0_w_out, l0_ln1_g, l0_ln1_b, ffn0_w_up, ffn0_conv_w, ffn0_conv_b, ffn0_w_down, l0_ln2_g, l0_ln2_b, l1_w_in, mla_q_norm_g, mla_w_uq, mla_kv_norm_g, mla_w_ukv, l1_w_out, l1_ln1_g, l1_ln1_b, ffn1_w_up, ffn1_conv_w, ffn1_conv_b, ffn1_w_down, l1_ln2_g, l1_ln2_b, loss_target, m_l0_w_in, m_rwkv_mix, m_rwkv_w0, m_rwkv_w2, m_rwkv_a0, m_rwkv_a2, m_rwkv_g2, m_rwkv_k_k, m_rwkv_k_a, m_rwkv_r_k, m_rwkv_ln_g, m_rwkv_ln_b, m_ssm_conv_w, m_ssm_conv_b, m_ssm_dt_bias, m_ssm_a_log, m_ssm_d, m_ssm_norm_g, m_l0_w_out, m_l0_ln1_g, m_l0_ln1_b, m_ffn0_w_up, m_ffn0_conv_w, m_ffn0_conv_b, m_ffn0_w_down, m_l0_ln2_g, m_l0_ln2_b, m_l1_w_in, m_mla_q_norm_g, m_mla_w_uq, m_mla_kv_norm_g, m_mla_w_ukv, m_l1_w_out, m_l1_ln1_g, m_l1_ln1_b, m_ffn1_w_up, m_ffn1_conv_w, m_ffn1_conv_b, m_ffn1_w_down, m_l1_ln2_g, m_l1_ln2_b, v_l0_w_in, v_rwkv_mix, v_rwkv_w0, v_rwkv_w2, v_rwkv_a0, v_rwkv_a2, v_rwkv_g2, v_rwkv_k_k, v_rwkv_k_a, v_rwkv_r_k, v_rwkv_ln_g, v_rwkv_ln_b, v_ssm_conv_w, v_ssm_conv_b, v_ssm_dt_bias, v_ssm_a_log, v_ssm_d, v_ssm_norm_g, v_l0_w_out, v_l0_ln1_g, v_l0_ln1_b, v_ffn0_w_up, v_ffn0_conv_w, v_ffn0_conv_b, v_ffn0_w_down, v_l0_ln2_g, v_l0_ln2_b, v_l1_w_in, v_mla_q_norm_g, v_mla_w_uq, v_mla_kv_norm_g, v_mla_w_ukv, v_l1_w_out, v_l1_ln1_g, v_l1_ln1_b, v_ffn1_w_up, v_ffn1_conv_w, v_ffn1_conv_b, v_ffn1_w_down, v_l1_ln2_g, v_l1_ln2_b):
    return _step(dict(locals()))
```

```python
import functools
import math

import jax
import jax.numpy as jnp
from jax import lax
from jax.experimental import pallas as pl
from jax.experimental.pallas import tpu as pltpu

F32 = jnp.float32
BF16 = jnp.bfloat16
HI = lax.Precision.HIGHEST

V7X_VMEM_BYTES = 64 * 1024 * 1024
VMEM_LIMIT = V7X_VMEM_BYTES - 8 * 1024 * 1024
LANES = 128
SUBLANES = 8
N_DEV = 8

D_MODEL = 1024
HEAD_DIM = 64
RWKV_DIM = 512
RWKV_HEADS = 8
RWKV_GN_EPS = 64e-5
RWKV_CHUNK = 64
SSM_DIM = 512
SSM_HEADS = 8
SSM_CHUNK = 128
SSM_STATE = 128
Q_BLOCK = 128
SB_HEADS_PER_STEP = 4
MLA_NOPE = 64
MLA_ROPE = 32
ROPE_THETA = 10000.0
D_FF = 2816
DEPTH = 2
ALPHA = (2 * DEPTH) ** 0.25
L0_PAD = 3456
L1_PAD = 2048

ADAM_LR = 0.001
ADAM_B1 = 0.9
ADAM_B2 = 0.999
ADAM_EPS = 1e-08
ADAM_WD = 0.01
ADAM_STEP = 10

NEG_BIG = -1e30


def _params(sem=None):
    return pltpu.CompilerParams(dimension_semantics=sem, vmem_limit_bytes=VMEM_LIMIT)


P_F32, P_BF16, P_BF16X3 = 0, 1, 2


def _dg_raw(a, b, ca, cb, fast):
    dims = (((ca,), (cb,)), ((), ()))
    if fast == P_BF16:
        return lax.dot_general(a.astype(BF16), b.astype(BF16), dims, preferred_element_type=F32)
    prec = HI if fast == P_F32 else lax.Precision.HIGH
    return lax.dot_general(a, b, dims, precision=prec, preferred_element_type=F32)


@functools.partial(jax.custom_vjp, nondiff_argnums=(2, 3, 4))
def dg(a, b, ca, cb, fast):
    return _dg_raw(a, b, ca, cb, fast)


def _dg_fwd(a, b, ca, cb, fast):
    return _dg_raw(a, b, ca, cb, fast), (a, b)


def _dg_bwd(ca, cb, fast, res, ct):
    a, b = res
    fa, fb = 1 - ca, 1 - cb
    da = _dg_raw(ct, b, 1, fb, fast) if ca == 1 else _dg_raw(b, ct, fb, 1, fast)
    db = _dg_raw(a, ct, fa, 0, fast) if cb == 0 else _dg_raw(ct, a, 0, fa, fast)
    return da.astype(a.dtype), db.astype(b.dtype)


dg.defvjp(_dg_fwd, _dg_bwd)


def mmb(a, b):
    return dg(a, b, 1, 0, P_BF16)


def mmf(a, b):
    return dg(a, b, 1, 0, P_F32)


def mmf_nt(a, b):
    return dg(a, b, 1, 1, P_F32)


def mmf_tn(a, b):
    return dg(a, b, 0, 0, P_F32)


def mm3(a, b):
    return dg(a, b, 1, 0, P_BF16X3)


def mm3_nt(a, b):
    return dg(a, b, 1, 1, P_BF16X3)


def mm3_tn(a, b):
    return dg(a, b, 0, 0, P_BF16X3)


def _split3_dot(x, m01, cb):
    hi = x.astype(BF16)
    r1 = x - hi.astype(F32)
    mid = r1.astype(BF16)
    lo = (r1 - mid.astype(F32)).astype(BF16)
    rows = x.shape[0]
    out = lax.dot_general(jnp.concatenate([hi, mid, lo], axis=0), m01.astype(BF16), (((1,), (cb,)), ((), ())),
                          preferred_element_type=F32)
    return out[:rows] + out[rows:2 * rows] + out[2 * rows:]


def _lower_ones(n):
    return jnp.where(_iota((n, n), 0) >= _iota((n, n), 1), 1.0, 0.0)


@jax.custom_vjp
def suffix_sum(x):
    return _split3_dot(x, _lower_ones(x.shape[1]), 0)


def _suffix_sum_fwd(x):
    return suffix_sum(x), None


def _suffix_sum_bwd(_, ct):
    return (_split3_dot(ct, _lower_ones(ct.shape[1]), 1),)


suffix_sum.defvjp(_suffix_sum_fwd, _suffix_sum_bwd)


def _iota(shape, dim):
    return lax.broadcasted_iota(jnp.int32, shape, dim)


def _softplus(x):
    return jnp.maximum(x, 0.0) + jnp.log1p(jnp.exp(-jnp.abs(x)))


def _silu(x):
    return x * jax.nn.sigmoid(x)


def _largest_tile(n, cap, mult):
    best = None
    t = mult
    while t <= min(n, cap):
        if n % t == 0:
            best = t
        t += mult
    return n if best is None else best


MM_VMEM_BUDGET = 40 * 1024 * 1024
V7X_HBM_BYTES_PER_S = 3.2e12
GRID_STEP_S = 0.35e-6


def _mm_tiles(M, N, K, a_bytes, b_bytes, has_add):
    def divs(n):
        return [d for d in range(LANES, n + 1, LANES) if n % d == 0] or [n]

    best = None
    for tm in divs(M):
        for tn in divs(N):
            if tm * tn * 4 > 12 * 1024 * 1024:
                continue
            for tk in divs(K):
                vmem = (2 * (tm * tk * a_bytes + tk * tn * b_bytes) + 2 * tm * tn * 4 * (2 if has_add else 1)
                        + (tm * tk + tk * tn) * 2 + tm * tn * 4)
                if vmem > MM_VMEM_BUDGET:
                    continue
                ni, nj, nk = M // tm, N // tn, K // tk
                a_reads = M * K * a_bytes * (1 if nk == 1 else nj)
                b_reads = K * N * b_bytes * (1 if (nk == 1 and nj == 1) else ni)
                traffic = a_reads + b_reads + M * N * 4 * (2 if has_add else 1)
                cost = traffic / V7X_HBM_BYTES_PER_S + ni * nj * nk * GRID_STEP_S
                if min(tm, tn, tk) < 256 and min(M, N, K) >= 256:
                    cost *= 1.5
                if best is None or cost < best[0]:
                    best = (cost, tm, tn, tk)
    return best[1:]


def mm(a, b, mode, name, add=None):
    if mode == "nn":
        (M, K), N = a.shape, b.shape[1]
    elif mode == "nt":
        (M, K), N = a.shape, b.shape[0]
    else:
        (K, M), N = a.shape, b.shape[1]
    has_add = add is not None
    tm, tn, tk = _mm_tiles(M, N, K, a.dtype.itemsize, b.dtype.itemsize, has_add)
    nk = K // tk
    keep_a = nk == 1 and N // tn > 1 and a.dtype != BF16
    if mode == "nn":
        a_spec = pl.BlockSpec((tm, tk), lambda i, j, k: (i, k))
        b_spec = pl.BlockSpec((tk, tn), lambda i, j, k: (k, j))
        dims = (((1,), (0,)), ((), ()))
    elif mode == "nt":
        a_spec = pl.BlockSpec((tm, tk), lambda i, j, k: (i, k))
        b_spec = pl.BlockSpec((tn, tk), lambda i, j, k: (j, k))
        dims = (((1,), (1,)), ((), ()))
    else:
        a_spec = pl.BlockSpec((tk, tm), lambda i, j, k: (k, i))
        b_spec = pl.BlockSpec((tk, tn), lambda i, j, k: (k, j))
        dims = (((0,), (0,)), ((), ()))
    o_spec = pl.BlockSpec((tm, tn), lambda i, j, k: (i, j))

    def body(a_ref, b_ref, *rest):
        o_ref = rest[1] if has_add else rest[0]
        k = pl.program_id(2)
        if keep_a:
            a_bf = rest[-1]

            @pl.when(pl.program_id(1) == 0)
            def _():
                a_bf[...] = a_ref[...].astype(BF16)

            av = a_bf[...]
        else:
            av = a_ref[...].astype(BF16)
        part = lax.dot_general(av, b_ref[...].astype(BF16), dims, preferred_element_type=F32)

        @pl.when(k == 0)
        def _():
            o_ref[...] = part + rest[0][...] if has_add else part

        @pl.when(k > 0)
        def _():
            o_ref[...] += part

    ins = [a, b] + ([add] if has_add else [])
    specs = [a_spec, b_spec] + ([o_spec] if has_add else [])
    return pl.pallas_call(
        body, name=name, grid=(M // tm, N // tn, nk), in_specs=specs, out_specs=o_spec,
        out_shape=jax.ShapeDtypeStruct((M, N), F32),
        scratch_shapes=[pltpu.VMEM(a_spec.block_shape, BF16)] if keep_a else [],
        compiler_params=_params(("parallel", "arbitrary", "arbitrary")),
    )(*ins)


def _x_specs(xs, tm):
    return [pl.BlockSpec((tm, w), functools.partial(lambda i, cb: (i, cb), cb=cb)) for _, w, cb in xs]


def _p_specs(ps):
    return [pl.BlockSpec(p.shape, lambda i: (0, 0)) for p in ps]


def tok_fwd(name, f, xs, ps, out_widths, tm):
    n = xs[0][0].shape[0]
    nx, npar = len(xs), len(ps)

    def body(*refs):
        xv = [r[...] for r in refs[:nx]]
        pv = [r[...].astype(F32) for r in refs[nx:nx + npar]]
        outs = f(*xv, *pv)
        for o, r in zip(outs, refs[nx + npar:]):
            r[...] = o

    return pl.pallas_call(
        body, name=name, grid=(n // tm,),
        in_specs=_x_specs(xs, tm) + _p_specs(ps),
        out_specs=[pl.BlockSpec((tm, w), lambda i: (i, 0)) for w in out_widths],
        out_shape=[jax.ShapeDtypeStruct((n, w), F32) for w in out_widths],
        compiler_params=_params(("parallel",)),
    )(*[x[0] for x in xs], *ps)


def tok_bwd(name, f, xs, ps, cts, tm):
    n = xs[0][0].shape[0]
    nx, npar = len(xs), len(ps)
    ct_flat = [c for group in cts for c in group]
    nct = len(ct_flat)

    def body(*refs):
        xv = [r[...] for r in refs[:nx]]
        pv = [r[...].astype(F32) for r in refs[nx:nx + npar]]
        ct_refs = refs[nx + npar:nx + npar + nct]
        dx_refs = refs[nx + npar + nct:nx + npar + nct + nx]
        dp_refs = refs[nx + npar + nct + nx:]
        cv, pos = [], 0
        for group in cts:
            acc = ct_refs[pos][...]
            for r in ct_refs[pos + 1:pos + len(group)]:
                acc = acc + r[...]
            cv.append(acc)
            pos += len(group)
        _, vjp = jax.vjp(f, *xv, *pv)
        grads = vjp(tuple(cv))
        for g, r in zip(grads[:nx], dx_refs):
            r[...] = g

        @pl.when(pl.program_id(0) == 0)
        def _():
            for r in dp_refs:
                r[...] = jnp.zeros_like(r)

        for g, r in zip(grads[nx:], dp_refs):
            r[...] += g

    outs = pl.pallas_call(
        body, name=name, grid=(n // tm,),
        in_specs=(_x_specs(xs, tm) + _p_specs(ps)
                  + [pl.BlockSpec((tm, c.shape[1]), lambda i: (i, 0)) for c in ct_flat]),
        out_specs=([pl.BlockSpec((tm, w), lambda i: (i, 0)) for _, w, _ in xs] + _p_specs(ps)),
        out_shape=([jax.ShapeDtypeStruct((n, w), F32) for _, w, _ in xs]
                   + [jax.ShapeDtypeStruct(p.shape, F32) for p in ps]),
        compiler_params=_params(("arbitrary",)),
    )(*[x[0] for x in xs], *ps, *ct_flat)
    return outs[:nx], outs[nx:]


def f_ln(h, y, g, b):
    pre = ALPHA * h + y
    mu = jnp.mean(pre, axis=-1, keepdims=True)
    xc = pre - mu
    var = jnp.mean(xc * xc, axis=-1, keepdims=True)
    return (xc * lax.rsqrt(var + 1e-5) * g + b,)


def _head_sel(width, nheads_pad, per):
    return jnp.where(_iota((width, nheads_pad), 0) // per == _iota((width, nheads_pad), 1), 1.0, 0.0).astype(F32)


def _head_sel_t(nheads_pad, width, per):
    return jnp.where(_iota((nheads_pad, width), 1) // per == _iota((nheads_pad, width), 0), 1.0, 0.0).astype(F32)


def f_rwkv_pre(r, k, v, lora, glo, w0, w2p, a0, a2p, g2, k_k, k_a):
    lane = _iota(lora.shape, 1)
    tw = jnp.where(lane < 64, jnp.tanh(lora), 0.0)
    ta = jnp.where(lane >= 64, lora, 0.0)
    log_w = -_softplus(-(w0 + mmb(tw, w2p))) - 0.5
    lw = -jnp.exp(log_w)
    a = jax.nn.sigmoid(a0 + mmb(ta, a2p))
    g = mmb(jax.nn.sigmoid(glo), g2)
    sel = _head_sel(RWKV_DIM, LANES, HEAD_DIM)
    sel_t = _head_sel_t(LANES, RWKV_DIM, HEAD_DIM)
    kk = k * k_k
    nrm = jnp.sqrt(jnp.maximum(mmf(kk * kk, sel), 1e-24))
    kkn = kk * mmf(1.0 / nrm, sel_t)
    kmod = k * (1.0 + (a - 1.0) * k_a)
    return r, v, lw, kmod, -kkn, kkn * a, g


def f_rwkv_post(y, r, kmod, v, g, ln_g, ln_b, r_k):
    sel = _head_sel(RWKV_DIM, LANES, HEAD_DIM)
    sel_t = _head_sel_t(LANES, RWKV_DIM, HEAD_DIM)
    inv = 1.0 / HEAD_DIM
    mu = mmf(mmf(y, sel) * inv, sel_t)
    yc = y - mu
    var = mmf(yc * yc, sel) * inv
    rstd = mmf(lax.rsqrt(var + RWKV_GN_EPS), sel_t)
    yn = yc * rstd * ln_g + ln_b
    bonus = mmf(mmf(r * kmod * r_k, sel), sel_t) * v
    return ((yn + bonus) * g,)


def f_ssd_post(y, xs, z, d_skip, norm_g):
    sel_t = _head_sel_t(LANES, SSM_DIM, HEAD_DIM)
    d_e = jnp.sum(mmf(jnp.broadcast_to(d_skip, (SUBLANES, LANES)), sel_t), axis=0, keepdims=True) * (1.0 / SUBLANES)
    u = (y + xs * d_e) * _silu(z)
    first = _iota(u.shape, 1) < (SSM_DIM // 2)
    uu = u * u
    inv = 2.0 / SSM_DIM
    ms0 = jnp.sum(jnp.where(first, uu, 0.0), axis=-1, keepdims=True) * inv
    ms1 = jnp.sum(jnp.where(first, 0.0, uu), axis=-1, keepdims=True) * inv
    ms = jnp.where(first, ms0, ms1)
    return (u * lax.rsqrt(ms + 1e-5) * norm_g,)


def f_mla_pre(cq, ckv, qg, w_uq, kvg, w_ukv):
    def rms(x, g):
        return x * lax.rsqrt(jnp.mean(x * x, axis=-1, keepdims=True) + 1e-6) * g
    return mmb(rms(cq, qg), w_uq), mmb(rms(ckv, kvg), w_ukv)


def _shift_down(x, s, row):
    return x if s == 0 else jnp.where(row >= s, pltpu.roll(x, s, 0), 0.0)


def _shift_up(x, s, row, t):
    return x if s == 0 else jnp.where(row < t - s, pltpu.roll(x, t - s, 0), 0.0)


def dwconv_fwd(name, u, colmap, w, b, taps, silu, upmap=None):
    bsz, t, _ = u.shape
    c = w.shape[1]
    tc = LANES
    has_up = upmap is not None

    def body(*refs):
        u_ref, w_ref, b_ref = refs[:3]
        o_ref = refs[-1]
        uv = u_ref[...]
        wv = w_ref[...]
        row = _iota(uv.shape, 0)
        acc = jnp.broadcast_to(b_ref[...], uv.shape)
        for i in range(taps):
            acc = acc + wv[i:i + 1, :] * _shift_down(uv, taps - 1 - i, row)
        if silu:
            acc = _silu(acc)
        if has_up:
            acc = acc * refs[3][...]
        o_ref[...] = acc

    specs = [pl.BlockSpec((None, t, tc), lambda bb, j: (bb, 0, colmap(j))),
             pl.BlockSpec((taps, tc), lambda bb, j: (0, j)),
             pl.BlockSpec((1, tc), lambda bb, j: (0, j))]
    ins = [u, w, b]
    if has_up:
        specs.append(pl.BlockSpec((None, t, tc), lambda bb, j: (bb, 0, upmap(j))))
        ins.append(u)
    return pl.pallas_call(
        body, name=name, grid=(bsz, c // tc), in_specs=specs,
        out_specs=pl.BlockSpec((None, t, tc), lambda bb, j: (bb, 0, j)),
        out_shape=jax.ShapeDtypeStruct((bsz, t, c), F32),
        compiler_params=_params(("parallel", "parallel")),
    )(*ins)


def dwconv_bwd(name, u, colmap, w, b, taps, silu, dout, upmap=None):
    bsz, t, _ = u.shape
    c = w.shape[1]
    tc = LANES
    has_up = upmap is not None

    def body(*refs):
        u_ref, w_ref, b_ref, d_ref = refs[:4]
        nin = 5 if has_up else 4
        du_ref, dw_ref, db_ref = refs[nin:nin + 3]
        uv = u_ref[...]
        wv = w_ref[...]
        dv = d_ref[...]
        row = _iota(uv.shape, 0)
        shifted = [_shift_down(uv, taps - 1 - i, row) for i in range(taps)]
        cg = jnp.broadcast_to(b_ref[...], uv.shape)
        for i in range(taps):
            cg = cg + wv[i:i + 1, :] * shifted[i]
        if silu:
            sg = jax.nn.sigmoid(cg)
            act = cg * sg
            dact_dcg = sg * (1.0 + cg * (1.0 - sg))
        else:
            act = cg
            dact_dcg = None
        if has_up:
            refs[nin + 3][...] = dv * act
            dv = dv * refs[4][...]
        dcg = dv * dact_dcg if silu else dv
        du = jnp.zeros_like(uv)
        for i in range(taps):
            du = du + wv[i:i + 1, :] * _shift_up(dcg, taps - 1 - i, row, t)
        du_ref[...] = du

        @pl.when(pl.program_id(1) == 0)
        def _():
            dw_ref[...] = jnp.zeros_like(dw_ref)
            db_ref[...] = jnp.zeros_like(db_ref)

        for i in range(taps):
            dw_ref[i:i + 1, :] += jnp.sum(dcg * shifted[i], axis=0, keepdims=True)
        db_ref[...] += jnp.sum(dcg, axis=0, keepdims=True)

    specs = [pl.BlockSpec((None, t, tc), lambda j, bb: (bb, 0, colmap(j))),
             pl.BlockSpec((taps, tc), lambda j, bb: (0, j)),
             pl.BlockSpec((1, tc), lambda j, bb: (0, j)),
             pl.BlockSpec((None, t, tc), lambda j, bb: (bb, 0, j))]
    ins = [u, w, b, dout]
    if has_up:
        specs.append(pl.BlockSpec((None, t, tc), lambda j, bb: (bb, 0, upmap(j))))
        ins.append(u)
    big = pl.BlockSpec((None, t, tc), lambda j, bb: (bb, 0, j))
    out_specs = [big, pl.BlockSpec((taps, tc), lambda j, bb: (0, j)), pl.BlockSpec((1, tc), lambda j, bb: (0, j))]
    out_shape = [jax.ShapeDtypeStruct((bsz, t, c), F32), jax.ShapeDtypeStruct((taps, c), F32),
                 jax.ShapeDtypeStruct((1, c), F32)]
    if has_up:
        out_specs.append(big)
        out_shape.append(jax.ShapeDtypeStruct((bsz, t, c), F32))
    return pl.pallas_call(
        body, name=name, grid=(c // tc, bsz), in_specs=specs, out_specs=out_specs, out_shape=out_shape,
        compiler_params=_params(("parallel", "arbitrary")),
    )(*ins)


def _each(f, *lists):
    return [f(*xs) for xs in zip(*lists)]


def rwkv_chunk(s0, r, lw, k, v, al, be):
    c = r[0].shape[0]
    ii, jj = _iota((c, c), 0), _iota((c, c), 1)
    incl, strict = ii >= jj, ii > jj
    ones_incl = jnp.where(incl, 1.0, 0.0)
    eye = jnp.where(ii == jj, 1.0, 0.0)
    cum = _each(lambda x: mmf(ones_incl, x), lw)
    gam_inv = _each(lambda x: jnp.exp(-x), cum)
    at = _each(lambda a_, c_, l_: a_ * jnp.exp(c_ - l_), al, cum, lw)
    rt = _each(lambda r_, c_: r_ * jnp.exp(c_), r, cum)
    bt = _each(lambda b_, g_: b_ * g_, be, gam_inv)
    kt = _each(lambda k_, g_: k_ * g_, k, gam_inv)
    a_b = _each(lambda x, y_: jnp.where(strict, mm3_nt(x, y_), 0.0), at, bt)
    a_k = _each(lambda x, y_: jnp.where(strict, mm3_nt(x, y_), 0.0), at, kt)
    rhs0 = _each(mm3_nt, at, s0)
    rhs = _each(lambda x, a_, v_: x + mm3(a_, v_), rhs0, a_k, v)
    p = _each(lambda x: eye + x, a_b)
    m = a_b
    for _ in range(int(math.log2(c)) - 1):
        m = _each(mm3, m, m)
        p = _each(lambda p_, m_: p_ + mm3(p_, m_), p, m)
    u = _each(mm3, p, rhs)
    r_b = _each(lambda x, y_: jnp.where(incl, mm3_nt(x, y_), 0.0), rt, bt)
    r_k = _each(lambda x, y_: jnp.where(incl, mm3_nt(x, y_), 0.0), rt, kt)
    y0 = _each(mm3_nt, rt, s0)
    y1 = _each(lambda y_, b_, u_: y_ + mm3(b_, u_), y0, r_b, u)
    y = _each(lambda y_, k_, v_: y_ + mm3(k_, v_), y1, r_k, v)
    su = _each(mm3_tn, u, bt)
    sv = _each(mm3_tn, v, kt)
    s1 = _each(lambda s_, a_, b_, l_: (s_ + a_ + b_) * jnp.exp(jnp.sum(l_, axis=0, keepdims=True)), s0, su, sv, lw)
    return y, s1


def rwkv_scan_fwd(r, lw, k, v, al, be, ride=None):
    bsz, h, t, d = r.shape
    c = RWKV_CHUNK
    nc = t // c
    grid = (bsz, nc)
    r_in, r_specs, r_out, r_ospecs, r_scr = _ride_args(ride)

    def body(*refs):
        r_ref, lw_ref, k_ref, v_ref, al_ref, be_ref = refs[:6]
        y_ref, st_ref = refs[6 + len(r_in):8 + len(r_in)]
        s_scr = refs[8 + 2 * len(r_in)]
        if ride is not None:
            first, last = _grid_first_last(grid)
            copies = _ride_start((refs[6], refs[8 + len(r_in)], *refs[-3:]), ride[1], first)

        @pl.when(pl.program_id(1) == 0)
        def _():
            s_scr[...] = jnp.zeros_like(s_scr)

        heads = lambda ref: [ref[hh] for hh in range(h)]
        s0 = heads(s_scr)
        y, s1 = rwkv_chunk(s0, heads(r_ref), heads(lw_ref), heads(k_ref), heads(v_ref), heads(al_ref),
                           heads(be_ref))
        for hh in range(h):
            st_ref[hh] = s0[hh]
            y_ref[hh] = y[hh]
            s_scr[hh] = s1[hh]
        if ride is not None:
            _ride_wait(copies, last)

    seq = pl.BlockSpec((None, h, c, d), lambda b, i: (b, 0, i, 0))
    return pl.pallas_call(
        body, name="rwkv_scan_fwd", grid=grid, in_specs=[seq] * 6 + r_specs,
        out_specs=[seq, pl.BlockSpec((None, h, None, d, d), lambda b, i: (b, 0, i, 0, 0))] + r_ospecs,
        out_shape=[jax.ShapeDtypeStruct((bsz, h, t, d), F32), jax.ShapeDtypeStruct((bsz, h, nc, d, d), F32)] + r_out,
        scratch_shapes=[pltpu.VMEM((h, d, d), F32)] + r_scr,
        compiler_params=_params(("arbitrary", "arbitrary")),
    )(r, lw, k, v, al, be, *r_in)


def rwkv_scan_bwd(r, lw, k, v, al, be, states, dy, ride=None):
    bsz, h, t, d = r.shape
    c = RWKV_CHUNK
    nc = t // c
    grid = (bsz, nc)
    r_in, r_specs, r_out, r_ospecs, r_scr = _ride_args(ride)

    def body(*refs):
        r_ref, lw_ref, k_ref, v_ref, al_ref, be_ref, st_ref, dy_ref = refs[:8]
        nin = 8 + len(r_in)
        dr_ref, dlw_ref, dk_ref, dv_ref, dal_ref, dbe_ref = refs[nin:nin + 6]
        ds_scr = refs[nin + 6 + len(r_in)]
        if ride is not None:
            first, last = _grid_first_last(grid)
            copies = _ride_start((refs[8], refs[nin + 6], *refs[-3:]), ride[1], first)

        @pl.when(pl.program_id(1) == 0)
        def _():
            ds_scr[...] = jnp.zeros_like(ds_scr)

        heads = lambda ref: [ref[hh] for hh in range(h)]
        _, vjp = jax.vjp(rwkv_chunk, heads(st_ref), heads(r_ref), heads(lw_ref), heads(k_ref), heads(v_ref),
                         heads(al_ref), heads(be_ref))
        grads = vjp((heads(dy_ref), heads(ds_scr)))
        for ref, gl in zip((ds_scr, dr_ref, dlw_ref, dk_ref, dv_ref, dal_ref, dbe_ref), grads):
            for hh in range(h):
                ref[hh] = gl[hh]
        if ride is not None:
            _ride_wait(copies, last)

    seq = pl.BlockSpec((None, h, c, d), lambda b, i: (b, 0, nc - 1 - i, 0))
    st = pl.BlockSpec((None, h, None, d, d), lambda b, i: (b, 0, nc - 1 - i, 0, 0))
    return pl.pallas_call(
        body, name="rwkv_scan_bwd", grid=grid, in_specs=[seq] * 6 + [st, seq] + r_specs,
        out_specs=[seq] * 6 + r_ospecs, out_shape=[jax.ShapeDtypeStruct((bsz, h, t, d), F32)] * 6 + r_out,
        scratch_shapes=[pltpu.VMEM((h, d, d), F32)] + r_scr,
        compiler_params=_params(("arbitrary", "arbitrary")),
    )(r, lw, k, v, al, be, states, dy, *r_in)


def ssd_chunk(st, xs, bm, cm, dtr, dt_bias, a_log):
    n = SSM_CHUNK
    ii, jj = _iota((n, n), 0), _iota((n, n), 1)
    incl = ii >= jj
    lane = _iota((n, LANES), 1)
    dt = _softplus(dtr + dt_bias)
    a = dt * (-jnp.exp(a_log))
    acum = mmf(jnp.where(incl, 1.0, 0.0), a)
    last_row = jnp.where(jj == n - 1, 1.0, 0.0)
    cb = [mmf_nt(cm[g], bm[g]) for g in range(2)]
    pairs, heads = range(4), range(SSM_HEADS)
    e_m = [jnp.where(_iota((LANES, LANES), 0) == 2 * m + _iota((LANES, LANES), 1) // HEAD_DIM, 1.0, 0.0)
           for m in pairs]
    dt_m = [mmf(dt, e_m[m]) for m in pairs]
    ac_m = [mmf(acum, e_m[m]) for m in pairs]
    x = [xs[m] * dt_m[m] for m in pairs]
    last_m = [mmf(last_row, ac_m[m]) for m in pairs]
    colb = [mmf(acum, jnp.where(_iota((LANES, n), 0) == h, 1.0, 0.0)) for h in heads]
    decay = [jnp.exp(jnp.where(incl, colb[h] - colb[h].T, NEG_BIG)) for h in heads]
    yh = [mmf(cb[h // 4] * decay[h], x[h // 2]) for h in heads]
    y_off = [mmf(cm[m // 2], st[m]) for m in pairs]
    ys = [jnp.where(lane // HEAD_DIM == 0, yh[2 * m], yh[2 * m + 1]) + jnp.exp(ac_m[m]) * y_off[m] for m in pairs]
    st_in = [mmf_tn(bm[m // 2], x[m] * jnp.exp(last_m[m] - ac_m[m])) for m in pairs]
    st_new = [jnp.exp(last_m[m]) * st[m] + st_in[m] for m in pairs]
    return tuple(ys), tuple(st_new)


def _ssd_load(xbc_ref, dtr_ref):
    xs = tuple(xbc_ref[:, m * LANES:(m + 1) * LANES] for m in range(4))
    bm = tuple(xbc_ref[:, SSM_DIM + g * LANES:SSM_DIM + (g + 1) * LANES] for g in range(2))
    cm = tuple(xbc_ref[:, SSM_DIM + 2 * LANES + g * LANES:SSM_DIM + 2 * LANES + (g + 1) * LANES] for g in range(2))
    return xs, bm, cm, dtr_ref[...]


def ssd_fwd(xbc, proj, dt_col, dt_bias, a_log):
    bsz, t, _ = xbc.shape
    n = SSM_CHUNK
    nc = t // n

    def body(xbc_ref, dtr_ref, dtb_ref, al_ref, y_ref, st_ref, s_scr):
        @pl.when(pl.program_id(1) == 0)
        def _():
            s_scr[...] = jnp.zeros_like(s_scr)

        st = tuple(s_scr[m] for m in range(4))
        for m in range(4):
            st_ref[m] = st[m]
        xs, bm, cm, dtr = _ssd_load(xbc_ref, dtr_ref)
        ys, st_new = ssd_chunk(st, xs, bm, cm, dtr, dtb_ref[...], al_ref[...])
        for m in range(4):
            y_ref[:, m * LANES:(m + 1) * LANES] = ys[m]
            s_scr[m] = st_new[m]

    vec = pl.BlockSpec((1, LANES), lambda b, i: (0, 0))
    return pl.pallas_call(
        body, name="ssd_fwd", grid=(bsz, nc),
        in_specs=[pl.BlockSpec((None, n, 2 * SSM_DIM), lambda b, i: (b, i, 0)),
                  pl.BlockSpec((None, n, LANES), lambda b, i: (b, i, dt_col)), vec, vec],
        out_specs=[pl.BlockSpec((None, n, SSM_DIM), lambda b, i: (b, i, 0)),
                   pl.BlockSpec((None, None, 4, SSM_STATE, LANES), lambda b, i: (b, i, 0, 0, 0))],
        out_shape=[jax.ShapeDtypeStruct((bsz, t, SSM_DIM), F32),
                   jax.ShapeDtypeStruct((bsz, nc, 4, SSM_STATE, LANES), F32)],
        scratch_shapes=[pltpu.VMEM((4, SSM_STATE, LANES), F32)],
        compiler_params=_params(("parallel", "arbitrary")),
    )(xbc, proj, dt_bias, a_log)


def ssd_bwd(xbc, proj, dt_col, dt_bias, a_log, states, dy, dxs_extra):
    bsz, t, _ = xbc.shape
    n = SSM_CHUNK
    nc = t // n

    def body(xbc_ref, dtr_ref, dtb_ref, al_ref, st_ref, dy_ref, ex_ref,
             dxbc_ref, ddtr_ref, ddtb_ref, dal_ref, ds_scr):
        first = jnp.logical_and(pl.program_id(0) == 0, pl.program_id(1) == 0)

        @pl.when(pl.program_id(1) == 0)
        def _():
            ds_scr[...] = jnp.zeros_like(ds_scr)

        @pl.when(first)
        def _():
            ddtb_ref[...] = jnp.zeros_like(ddtb_ref)
            dal_ref[...] = jnp.zeros_like(dal_ref)

        st = tuple(st_ref[m] for m in range(4))
        xs, bm, cm, dtr = _ssd_load(xbc_ref, dtr_ref)
        _, vjp = jax.vjp(ssd_chunk, st, xs, bm, cm, dtr, dtb_ref[...], al_ref[...])
        dys = tuple(dy_ref[:, m * LANES:(m + 1) * LANES] for m in range(4))
        dst_in = tuple(ds_scr[m] for m in range(4))
        dst, dxs, dbm, dcm, ddtr, ddtb, dal = vjp((dys, dst_in))
        for m in range(4):
            ds_scr[m] = dst[m]
            sl = slice(m * LANES, (m + 1) * LANES)
            dxbc_ref[:, sl] = dxs[m] + ex_ref[:, sl]
        for g in range(2):
            dxbc_ref[:, SSM_DIM + g * LANES:SSM_DIM + (g + 1) * LANES] = dbm[g]
            dxbc_ref[:, SSM_DIM + 2 * LANES + g * LANES:SSM_DIM + 2 * LANES + (g + 1) * LANES] = dcm[g]
        ddtr_ref[...] = ddtr
        ddtb_ref[...] += ddtb
        dal_ref[...] += dal

    vec = pl.BlockSpec((1, LANES), lambda b, i: (0, 0))
    rev = lambda b, i: (b, nc - 1 - i, 0)
    return pl.pallas_call(
        body, name="ssd_bwd", grid=(bsz, nc),
        in_specs=[pl.BlockSpec((None, n, 2 * SSM_DIM), rev),
                  pl.BlockSpec((None, n, LANES), lambda b, i: (b, nc - 1 - i, dt_col)), vec, vec,
                  pl.BlockSpec((None, None, 4, SSM_STATE, LANES), lambda b, i: (b, nc - 1 - i, 0, 0, 0)),
                  pl.BlockSpec((None, n, SSM_DIM), rev), pl.BlockSpec((None, n, SSM_DIM), rev)],
        out_specs=[pl.BlockSpec((None, n, 2 * SSM_DIM), rev), pl.BlockSpec((None, n, LANES), rev), vec, vec],
        out_shape=[jax.ShapeDtypeStruct((bsz, t, 2 * SSM_DIM), F32), jax.ShapeDtypeStruct((bsz, t, LANES), F32),
                   jax.ShapeDtypeStruct((1, LANES), F32), jax.ShapeDtypeStruct((1, LANES), F32)],
        scratch_shapes=[pltpu.VMEM((4, SSM_STATE, LANES), F32)],
        compiler_params=_params(("arbitrary", "arbitrary")),
    )(xbc, proj, dt_bias, a_log, states, dy, dxs_extra)


def sb_block(q, kj, vj, carry, maskf):
    mask = maskf > 0.5
    z = _each(lambda q_, k_: dg(q_, k_, 1, 1, P_BF16) * (HEAD_DIM ** -0.5), q, kj)
    ls = _each(lambda z_: -_softplus(-z_), z)
    lk = _each(lambda l_, z_: jnp.where(mask, l_ - z_, 0.0), ls, z)
    sfx = _each(suffix_sum, lk)
    att = _each(lambda l_, c_, s_, k_: jnp.where(mask, jnp.exp(l_ + c_ + s_ - k_), 0.0), ls, carry, sfx, lk)
    out = _each(mmb, att, vj)
    return out, _each(lambda c_, k_: c_ + jnp.sum(k_, axis=1, keepdims=True), carry, lk)


def _sb_mask(qi, j):
    n = Q_BLOCK
    return jnp.where(j * n + _iota((n, n), 1) < qi * n + _iota((n, n), 0), 1.0, 0.0)


def sb_fwd(q, k, v):
    bsz, h, t, d = q.shape
    n = Q_BLOCK

    hp = SB_HEADS_PER_STEP

    def body(q_ref, k_ref, v_ref, o_ref):
        qi = pl.program_id(2)

        def step(i, state):
            acc, carry = state
            j = qi - i
            rows = pl.ds(pl.multiple_of(j * n, n), n)
            o, carry = sb_block([q_ref[hh] for hh in range(hp)], [k_ref[hh, rows, :] for hh in range(hp)],
                                [v_ref[hh, rows, :] for hh in range(hp)], carry, _sb_mask(qi, j))
            return [a_ + o_ for a_, o_ in zip(acc, o)], carry

        init = ([jnp.zeros((n, d), F32) for _ in range(hp)], [jnp.zeros((n, 1), F32) for _ in range(hp)])
        acc, _ = lax.fori_loop(0, qi + 1, step, init)
        for hh in range(hp):
            o_ref[hh] = acc[hh]

    blk = pl.BlockSpec((None, hp, n, d), lambda b, hg, i: (b, hg, i, 0))
    full = pl.BlockSpec((None, hp, t, d), lambda b, hg, i: (b, hg, 0, 0))
    return pl.pallas_call(
        body, name="sb_fwd", grid=(bsz, h // hp, t // n), in_specs=[blk, full, full], out_specs=blk,
        out_shape=jax.ShapeDtypeStruct((bsz, h, t, d), F32),
        compiler_params=_params(("parallel", "parallel", "arbitrary")),
    )(q, k, v)


def sb_bwd(q, k, v, do, ride=None):
    bsz, h, t, d = q.shape
    n = Q_BLOCK
    hp = SB_HEADS_PER_STEP
    grid = (bsz, h // hp, t // n)
    r_in, r_specs, r_out, r_ospecs, r_scr = _ride_args(ride)

    def body(*refs):
        q_ref, k_ref, v_ref, do_ref = refs[:4]
        nin = 4 + len(r_in)
        dq_ref, dk_ref, dv_ref = refs[nin:nin + 3]
        carries = refs[nin + 3 + len(r_in)]
        if ride is not None:
            first, last = _grid_first_last(grid)
            copies = _ride_start((refs[4], refs[nin + 3], *refs[-3:]), ride[1], first)
        qi = pl.program_id(2)

        @pl.when(qi == 0)
        def _():
            dk_ref[...] = jnp.zeros_like(dk_ref)
            dv_ref[...] = jnp.zeros_like(dv_ref)

        heads = range(hp)
        qv = [q_ref[hh] for hh in heads]

        def fwd_step(i, carry):
            j = qi - i
            rows = pl.ds(pl.multiple_of(j * n, n), n)
            for hh in heads:
                carries[hh, j] = carry[hh]
            return sb_block(qv, [k_ref[hh, rows, :] for hh in heads], [v_ref[hh, rows, :] for hh in heads],
                            carry, _sb_mask(qi, j))[1]

        lax.fori_loop(0, qi + 1, fwd_step, [jnp.zeros((n, 1), F32) for _ in heads])

        def bwd_step(j, state):
            dq, dcarry = state
            rows = pl.ds(pl.multiple_of(j * n, n), n)
            _, vjp = jax.vjp(sb_block, qv, [k_ref[hh, rows, :] for hh in heads],
                             [v_ref[hh, rows, :] for hh in heads], [carries[hh, j] for hh in heads], _sb_mask(qi, j))
            dqj, dkj, dvj, dc, _ = vjp(([do_ref[hh] for hh in heads], dcarry))
            for hh in heads:
                dk_ref[hh, rows, :] += dkj[hh]
                dv_ref[hh, rows, :] += dvj[hh]
            return [a_ + b_ for a_, b_ in zip(dq, dqj)], dc

        init = ([jnp.zeros((n, d), F32) for _ in heads], [jnp.zeros((n, 1), F32) for _ in heads])
        dq, _ = lax.fori_loop(0, qi + 1, bwd_step, init)
        for hh in heads:
            dq_ref[hh] = dq[hh]
        if ride is not None:
            _ride_wait(copies, last)

    blk = pl.BlockSpec((None, hp, n, d), lambda b, hg, i: (b, hg, i, 0))
    full = pl.BlockSpec((None, hp, t, d), lambda b, hg, i: (b, hg, 0, 0))
    shp = jax.ShapeDtypeStruct((bsz, h, t, d), F32)
    return pl.pallas_call(
        body, name="sb_bwd", grid=grid, in_specs=[blk, full, full, blk] + r_specs,
        out_specs=[blk, full, full] + r_ospecs, out_shape=[shp, shp, shp] + r_out,
        scratch_shapes=[pltpu.VMEM((hp, t // n, n, 1), F32)] + r_scr,
        compiler_params=_params(("arbitrary", "arbitrary", "arbitrary")),
    )(q, k, v, do, *r_in)


def mla_block(qn, qp, kn, kp, v, maskf):
    scale = (MLA_NOPE + MLA_ROPE) ** -0.5
    s = (dg(qn, kn, 1, 1, P_BF16) + dg(qp, kp, 1, 1, P_BF16)) * scale
    s = jnp.where(maskf > 0.5, s, NEG_BIG)
    p = jnp.exp(s - jnp.max(s, axis=-1, keepdims=True))
    prob = p / jnp.sum(p, axis=-1, keepdims=True)
    return mmb(prob, v)


def _mla_mask(qi, t):
    n = Q_BLOCK
    return jnp.where(_iota((n, t), 1) <= qi * n + _iota((n, t), 0), 1.0, 0.0)


def _mla_specs(t):
    n = Q_BLOCK
    return (pl.BlockSpec((None, None, n, MLA_NOPE), lambda b, hh, i: (b, hh, i, 0)),
            pl.BlockSpec((None, None, n, MLA_ROPE), lambda b, hh, i: (b, hh, i, 0)),
            pl.BlockSpec((None, None, t, MLA_NOPE), lambda b, hh, i: (b, hh, 0, 0)),
            pl.BlockSpec((None, None, t, MLA_ROPE), lambda b, hh, i: (b, 0, 0, 0)))


def mla_fwd(qn, qp, kn, kp, v):
    bsz, h, t, _ = qn.shape

    def body(qn_ref, qp_ref, kn_ref, kp_ref, v_ref, o_ref):
        o_ref[...] = mla_block(qn_ref[...], qp_ref[...], kn_ref[...], kp_ref[...], v_ref[...],
                               _mla_mask(pl.program_id(2), t))

    qn_s, qp_s, kn_s, kp_s = _mla_specs(t)
    return pl.pallas_call(
        body, name="mla_fwd", grid=(bsz, h, t // Q_BLOCK), in_specs=[qn_s, qp_s, kn_s, kp_s, kn_s],
        out_specs=qn_s, out_shape=jax.ShapeDtypeStruct(qn.shape, F32),
        compiler_params=_params(("parallel", "parallel", "arbitrary")),
    )(qn, qp, kn, kp, v)


def mla_bwd(qn, qp, kn, kp, v, do):
    bsz, h, t, _ = qn.shape

    def body(qn_ref, qp_ref, kn_ref, kp_ref, v_ref, do_ref, dqn_ref, dqp_ref, dkn_ref, dkp_ref, dv_ref):
        hh, qi = pl.program_id(1), pl.program_id(2)

        @pl.when(qi == 0)
        def _():
            dkn_ref[...] = jnp.zeros_like(dkn_ref)
            dv_ref[...] = jnp.zeros_like(dv_ref)

        @pl.when(jnp.logical_and(qi == 0, hh == 0))
        def _():
            dkp_ref[...] = jnp.zeros_like(dkp_ref)

        _, vjp = jax.vjp(mla_block, qn_ref[...], qp_ref[...], kn_ref[...], kp_ref[...], v_ref[...],
                         _mla_mask(qi, t))
        dqn, dqp, dkn, dkp, dv, _ = vjp(do_ref[...])
        dqn_ref[...] = dqn
        dqp_ref[...] = dqp
        dkn_ref[...] += dkn
        dkp_ref[...] += dkp
        dv_ref[...] += dv

    qn_s, qp_s, kn_s, kp_s = _mla_specs(t)
    return pl.pallas_call(
        body, name="mla_bwd", grid=(bsz, h, t // Q_BLOCK), in_specs=[qn_s, qp_s, kn_s, kp_s, kn_s, qn_s],
        out_specs=[qn_s, qp_s, kn_s, kp_s, kn_s],
        out_shape=[jax.ShapeDtypeStruct(qn.shape, F32), jax.ShapeDtypeStruct(qp.shape, F32),
                   jax.ShapeDtypeStruct(kn.shape, F32), jax.ShapeDtypeStruct(kp.shape, F32),
                   jax.ShapeDtypeStruct(v.shape, F32)],
        compiler_params=_params(("parallel", "arbitrary", "arbitrary")),
    )(qn, qp, kn, kp, v, do)


def rope(name, x, pos, inv_freq, sign):
    bsz, hx, t, d = x.shape
    half = d // 2

    def body(x_ref, pos_ref, f_ref, o_ref):
        xv = x_ref[...]
        ang = pos_ref[...].astype(F32) * f_ref[...]
        ri, ci = _iota((d, d), 0), _iota((d, d), 1)
        rot = jnp.where(ri == ci + half, -1.0, 0.0) + jnp.where(ri + half == ci, 1.0, 0.0)
        o_ref[...] = xv * jnp.cos(ang) + sign * (mmf(xv, rot) * jnp.sin(ang))

    return pl.pallas_call(
        body, name=name, grid=(bsz, hx),
        in_specs=[pl.BlockSpec((None, None, t, d), lambda b, hh: (b, hh, 0, 0)),
                  pl.BlockSpec((None, t, 1), lambda b, hh: (b, 0, 0)),
                  pl.BlockSpec((1, d), lambda b, hh: (0, 0))],
        out_specs=pl.BlockSpec((None, None, t, d), lambda b, hh: (b, hh, 0, 0)),
        out_shape=jax.ShapeDtypeStruct(x.shape, F32),
        compiler_params=_params(("parallel", "parallel")),
    )(x, pos, inv_freq)


def loss_head(h, target, tm):
    n, d = h.shape

    def body(h_ref, t_ref, dh_ref, l_ref):
        @pl.when(pl.program_id(0) == 0)
        def _():
            l_ref[...] = jnp.zeros_like(l_ref)

        e = h_ref[...] - t_ref[...]
        dh_ref[...] = e * (1.0 / d)
        l_ref[...] += jnp.sum(e * e, axis=(0, 1), keepdims=True) * (0.5 / d)

    row = pl.BlockSpec((tm, d), lambda i: (i, 0))
    dh, l = pl.pallas_call(
        body, name="loss_head", grid=(n // tm,), in_specs=[row, row],
        out_specs=[row, pl.BlockSpec((SUBLANES, LANES), lambda i: (0, 0))],
        out_shape=[jax.ShapeDtypeStruct((n, d), F32), jax.ShapeDtypeStruct((SUBLANES, LANES), F32)],
        compiler_params=_params(("arbitrary",)),
    )(h, target)
    return dh, l[0, 0]


def _exchange_copies(src_ref, out_ref, send_sems, recv_sems, local_sem, gather):
    x, y, c = lax.axis_index("x"), lax.axis_index("y"), lax.axis_index("c")
    me = 4 * x + 2 * y + c
    copies = [pltpu.make_async_copy(src_ref if gather else src_ref.at[me], out_ref.at[me], local_sem)]
    for m in range(1, N_DEV):
        px, py, pc = x ^ (m >> 2), y ^ ((m >> 1) & 1), c ^ (m & 1)
        peer = 4 * px + 2 * py + pc
        copies.append(pltpu.make_async_remote_copy(
            src_ref=src_ref if gather else src_ref.at[peer], dst_ref=out_ref.at[me],
            send_sem=send_sems.at[m], recv_sem=recv_sems.at[m],
            device_id=(px, py, pc), device_id_type=pl.DeviceIdType.MESH))
    return copies


def _exchange_start(copies):
    for cp in copies:
        cp.start()


def _exchange_wait(copies):
    for cp in copies[1:]:
        cp.wait_recv()
    for cp in copies[1:]:
        cp.wait_send()
    copies[0].wait()


EXCHANGE_SCRATCH = [pltpu.SemaphoreType.DMA((N_DEV,)), pltpu.SemaphoreType.DMA((N_DEV,)),
                    pltpu.SemaphoreType.DMA(())]


def _exchange_out(src):
    return jax.ShapeDtypeStruct((N_DEV, src.shape[-2], LANES), src.dtype)


def peer_exchange(name, src, gather):
    def body(src_ref, out_ref, send_sems, recv_sems, local_sem):
        copies = _exchange_copies(src_ref, out_ref, send_sems, recv_sems, local_sem, gather)
        _exchange_start(copies)
        _exchange_wait(copies)

    return pl.pallas_call(
        body, name=name,
        in_specs=[pl.BlockSpec(memory_space=pl.ANY)], out_specs=pl.BlockSpec(memory_space=pl.ANY),
        out_shape=_exchange_out(src), scratch_shapes=list(EXCHANGE_SCRATCH),
    )(src)


def _grid_first_last(grid):
    ids = [pl.program_id(a) for a in range(len(grid))]
    first = functools.reduce(jnp.logical_and, [i == 0 for i in ids])
    last = functools.reduce(jnp.logical_and, [i == g - 1 for i, g in zip(ids, grid)])
    return first, last


def _ride_start(refs, gather, first):
    copies = _exchange_copies(*refs, gather)

    @pl.when(first)
    def _():
        _exchange_start(copies)

    return copies


def _ride_wait(copies, last):
    @pl.when(last)
    def _():
        _exchange_wait(copies)


def _ride_args(ride):
    if ride is None:
        return [], [], [], [], []
    hbm = pl.BlockSpec(memory_space=pl.ANY)
    return [ride[0]], [hbm], [_exchange_out(ride[0])], [hbm], list(EXCHANGE_SCRATCH)


def adamw_sum(name, parts, w, m, v):
    r = w.shape[0]
    tr = ADAM_ROWS
    assert r % tr == 0

    def body(p_ref, w_ref, m_ref, v_ref, g_ref, d_ref, nm_ref, nv_ref):
        g = p_ref[0]
        for j in range(1, N_DEV):
            g = g + p_ref[j]
        mm_ = ADAM_B1 * m_ref[...] + (1.0 - ADAM_B1) * g
        vv = ADAM_B2 * v_ref[...] + (1.0 - ADAM_B2) * (g * g)
        m_hat = mm_ / (1.0 - ADAM_B1 ** ADAM_STEP)
        v_hat = vv / (1.0 - ADAM_B2 ** ADAM_STEP)
        g_ref[...] = g
        d_ref[...] = -ADAM_LR * (m_hat / (jnp.sqrt(v_hat) + ADAM_EPS) + ADAM_WD * w_ref[...])
        nm_ref[...] = mm_
        nv_ref[...] = vv

    row = pl.BlockSpec((tr, LANES), lambda i: (i, 0))
    shp = jax.ShapeDtypeStruct((r, LANES), F32)
    return pl.pallas_call(
        body, name=name, grid=(r // tr,),
        in_specs=[pl.BlockSpec((N_DEV, tr, LANES), lambda i: (0, i, 0)), row, row, row],
        out_specs=[row] * 4, out_shape=[shp] * 4,
        compiler_params=_params(("parallel",)),
    )(parts, w, m, v)


WEIGHTS = ['l0_w_in', 'rwkv_mix', 'rwkv_w0', 'rwkv_w2', 'rwkv_a0', 'rwkv_a2', 'rwkv_g2', 'rwkv_k_k', 'rwkv_k_a',
           'rwkv_r_k', 'rwkv_ln_g', 'rwkv_ln_b', 'ssm_conv_w', 'ssm_conv_b', 'ssm_dt_bias', 'ssm_a_log', 'ssm_d',
           'ssm_norm_g', 'l0_w_out', 'l0_ln1_g', 'l0_ln1_b', 'ffn0_w_up', 'ffn0_conv_w', 'ffn0_conv_b',
           'ffn0_w_down', 'l0_ln2_g', 'l0_ln2_b', 'l1_w_in', 'mla_q_norm_g', 'mla_w_uq', 'mla_kv_norm_g',
           'mla_w_ukv', 'l1_w_out', 'l1_ln1_g', 'l1_ln1_b', 'ffn1_w_up', 'ffn1_conv_w', 'ffn1_conv_b',
           'ffn1_w_down', 'l1_ln2_g', 'l1_ln2_b']
SHARD_AXIS = {'l0_w_in': 1, 'rwkv_w2': 1, 'rwkv_a2': 1, 'rwkv_g2': 1, 'ssm_conv_w': 1, 'l0_w_out': 0,
              'ffn0_w_up': 1, 'ffn0_conv_w': 1, 'ffn0_w_down': 0, 'l1_w_in': 1, 'mla_w_uq': 1, 'mla_w_ukv': 1,
              'l1_w_out': 0, 'ffn1_w_up': 1, 'ffn1_conv_w': 1, 'ffn1_w_down': 0}
MATMUL_W = ['l0_w_in', 'rwkv_w2', 'rwkv_a2', 'rwkv_g2', 'l0_w_out', 'ffn0_w_up', 'ffn0_w_down', 'l1_w_in',
            'mla_w_uq', 'mla_w_ukv', 'l1_w_out', 'ffn1_w_up', 'ffn1_w_down']
CONV_W = ['ssm_conv_w', 'ffn0_conv_w', 'ffn1_conv_w']
TOK_TILE = 256
ADAM_ROWS = 512


def _ceil_to(size, unit):
    return -(-size // unit) * unit


def _flat_rows(pieces, seg_rows, total_rows):
    unit = seg_rows * LANES
    out, total = [], 0
    for p in pieces:
        p = p.reshape(-1)
        pad = _ceil_to(p.size, unit) - p.size
        out.append(jnp.pad(p, (0, pad)) if pad else p)
        total += p.size + pad
    tail = _ceil_to(total, total_rows * LANES) - total
    if tail:
        out.append(jnp.zeros((tail,), out[0].dtype))
    return jnp.concatenate(out).reshape(-1, LANES)


def _unflatten(flat2d, shapes, seg_rows):
    flat = flat2d.reshape(-1)
    out, off = [], 0
    for shp in shapes:
        size = math.prod(shp)
        out.append(flat[off:off + size].reshape(shp))
        off += _ceil_to(size, seg_rows * LANES)
    return out


def _to_heads(t2, bsz, h):
    n, w = t2.shape
    return t2.reshape(bsz, n // bsz, h, w // h).transpose(0, 2, 1, 3)


def _from_heads(t4):
    b, h, t, d = t4.shape
    return t4.transpose(0, 2, 1, 3).reshape(b * t, h * d)


def _row(v):
    return v.reshape(1, -1)


def _pad_lanes(v):
    return jnp.pad(v.reshape(1, -1), ((0, 0), (0, LANES - v.size)))


def _local_step(a, w, comm=None):
    x = a['x']
    bsz, t, d = x.shape
    n = bsz * t
    tm = TOK_TILE
    pos = a['positions'].reshape(bsz, t, 1)
    inv_freq = 1.0 / (ROPE_THETA ** (jnp.arange(0, MLA_ROPE, 2, dtype=F32) / MLA_ROPE))
    inv_freq = jnp.concatenate([inv_freq, inv_freq]).reshape(1, MLA_ROPE)
    target = a['loss_target'].reshape(n, d)

    wi0 = w['l0_w_in']
    win0 = jnp.concatenate([wi0[:, 0:1536], wi0[:, 1792:3328], wi0[:, 1536:1792], wi0[:, 3328:3336],
                            jnp.zeros((d, L0_PAD - 3336), wi0.dtype)], axis=1)
    w2p = jnp.concatenate([w['rwkv_w2'], jnp.zeros_like(w['rwkv_w2'])], axis=0)
    a2p = jnp.concatenate([jnp.zeros_like(w['rwkv_a2']), w['rwkv_a2']], axis=0)
    mix = a['rwkv_mix']
    taps = jnp.stack([mix, 1.0 - mix])
    zero_b = jnp.zeros((1, mix.size), F32)
    rw_map = lambda j: j + jnp.where(j >= 12, 12, 0)
    ssm_map = lambda j: j + 16
    gate_map = lambda j: j
    up_map = lambda j: j + D_FF // LANES
    dt_col = 3328 // LANES
    dtb, alog, dsk = _pad_lanes(a['ssm_dt_bias']), _pad_lanes(a['ssm_a_log']), _pad_lanes(a['ssm_d'])
    pre_p = [_row(a['rwkv_w0']), w2p, _row(a['rwkv_a0']), a2p, w['rwkv_g2'], _row(a['rwkv_k_k']), _row(a['rwkv_k_a'])]
    post_p = [_row(a['rwkv_ln_g']), _row(a['rwkv_ln_b']), _row(a['rwkv_r_k'])]
    sp_p = [dsk, _row(a['ssm_norm_g'])]

    def ln(name, h, y, layer, which):
        ps = [_row(a[f'l{layer}_ln{which}_g']), _row(a[f'l{layer}_ln{which}_b'])]
        return tok_fwd(name, f_ln, [(h, d, 0), (y, d, 0)], ps, [d], tm)[0]

    def ffn_fwd(layer, h):
        up = mm(h, w[f'ffn{layer}_w_up'], 'nn', f'ffn{layer}_up')
        act = dwconv_fwd(f'ffn{layer}_conv', up.reshape(bsz, t, 2 * D_FF), gate_map, w[f'ffn{layer}_conv_w'],
                         _row(a[f'ffn{layer}_conv_b']), 3, True, upmap=up_map)
        act = act.reshape(n, D_FF)
        return up, act, mm(act, w[f'ffn{layer}_w_down'], 'nn', f'ffn{layer}_down')

    x2 = x.reshape(n, d)
    proj0 = mm(x2, win0, 'nn', 'l0_in')
    p0 = proj0.reshape(bsz, t, L0_PAD)
    xs_r = dwconv_fwd('rwkv_shift', p0, rw_map, taps, zero_b, 2, False).reshape(n, 1792)
    pre_x = [(xs_r, 512, 0), (xs_r, 512, 1), (xs_r, 512, 2), (xs_r, LANES, 12), (xs_r, LANES, 13)]
    r_, v_, lw, kmod, al, be, gt = tok_fwd('rwkv_pre', f_rwkv_pre, pre_x, pre_p, [RWKV_DIM] * 7, tm)
    scan_in = [_to_heads(u, bsz, RWKV_HEADS) for u in (r_, lw, kmod, v_, al, be)]
    if comm is None:
        y_h, rstates = rwkv_scan_fwd(*scan_in)
    else:
        y_h, rstates, got = rwkv_scan_fwd(*scan_in, ride=comm.layer1_weights_ride())
        w = {**w, **comm.layer1_weights(got)}
    wi1 = w['l1_w_in']
    win1 = jnp.concatenate([wi1, jnp.zeros((d, L1_PAD - 1952), wi1.dtype)], axis=1)
    mla_p = [_row(a['mla_q_norm_g']), w['mla_w_uq'], _row(a['mla_kv_norm_g']), w['mla_w_ukv']]
    y_r = _from_heads(y_h)
    post_x = [(y_r, 512, 0), (r_, 512, 0), (kmod, 512, 0), (v_, 512, 0), (gt, 512, 0)]
    y_a = tok_fwd('rwkv_post', f_rwkv_post, post_x, post_p, [RWKV_DIM], tm)[0]
    xbc = dwconv_fwd('ssm_conv', p0, ssm_map, w['ssm_conv_w'], _row(a['ssm_conv_b']), 4, True)
    ys, sstates = ssd_fwd(xbc, p0, dt_col, dtb, alog)
    xbc2 = xbc.reshape(n, 2 * SSM_DIM)
    sp_x = [(ys.reshape(n, SSM_DIM), 512, 0), (xbc2, 512, 0), (proj0, 512, 3)]
    y_b = tok_fwd('ssd_post', f_ssd_post, sp_x, sp_p, [SSM_DIM], tm)[0]
    wo0 = w['l0_w_out']
    mixed0 = mm(y_b, wo0[512:], 'nn', 'l0_out_b', add=mm(y_a, wo0[:512], 'nn', 'l0_out_a'))
    h1 = ln('l0_ln1', x2, mixed0, 0, 1)
    up0, act0, f0 = ffn_fwd(0, h1)
    h2 = ln('l0_ln2', h1, f0, 0, 2)

    proj1 = mm(h2, win1, 'nn', 'l1_in')
    q_sb, k_sb, v_sb = (_to_heads(proj1[:, i * 512:(i + 1) * 512], bsz, 8) for i in range(3))
    y_c = _from_heads(sb_fwd(q_sb, k_sb, v_sb))
    mla_x = [(proj1, 256, 6), (proj1, LANES, 14)]
    q_all, kv_all = tok_fwd('mla_pre', f_mla_pre, mla_x, mla_p, [768, 1024], tm)
    q4 = _to_heads(q_all, bsz, 8)
    kv4 = _to_heads(kv_all, bsz, 8)
    qn, qp_raw = q4[..., :MLA_NOPE], q4[..., MLA_NOPE:]
    kn, vv = kv4[..., :MLA_NOPE], kv4[..., MLA_NOPE:]
    kp_raw = proj1[:, 1920:1920 + MLA_ROPE].reshape(bsz, 1, t, MLA_ROPE)
    qp = rope('rope_q', qp_raw, pos, inv_freq, 1.0)
    kp = rope('rope_k', kp_raw, pos, inv_freq, 1.0)
    y_d = _from_heads(mla_fwd(qn, qp, kn, kp, vv))
    wo1 = w['l1_w_out']
    mixed1 = mm(y_d, wo1[512:], 'nn', 'l1_out_b', add=mm(y_c, wo1[:512], 'nn', 'l1_out_a'))
    h3 = ln('l1_ln1', h2, mixed1, 1, 1)
    up1, act1, f1 = ffn_fwd(1, h3)
    h4 = ln('l1_ln2', h3, f1, 1, 2)
    dh4, loss = loss_head(h4, target, tm)

    g = {}

    def ln_bwd(name, h, y, layer, which, dout):
        ps = [_row(a[f'l{layer}_ln{which}_g']), _row(a[f'l{layer}_ln{which}_b'])]
        (dh, dy), (dg, db) = tok_bwd(name, f_ln, [(h, d, 0), (y, d, 0)], ps, [[dout]], tm)
        g[f'l{layer}_ln{which}_g'], g[f'l{layer}_ln{which}_b'] = dg.reshape(-1), db.reshape(-1)
        return dh, dy

    def ffn_bwd(layer, h, up, act, df, dh_res):
        wup, wdown = w[f'ffn{layer}_w_up'], w[f'ffn{layer}_w_down']
        g[f'ffn{layer}_w_down'] = mm(act, df, 'tn', f'ffn{layer}_dwdown')
        dact = mm(df, wdown, 'nt', f'ffn{layer}_dact').reshape(bsz, t, D_FF)
        dgate, dcw, dcb, dup = dwconv_bwd(f'ffn{layer}_conv_bwd', up.reshape(bsz, t, 2 * D_FF), gate_map,
                                          w[f'ffn{layer}_conv_w'], _row(a[f'ffn{layer}_conv_b']), 3, True, dact,
                                          upmap=up_map)
        dgate, dup = dgate.reshape(n, D_FF), dup.reshape(n, D_FF)
        g[f'ffn{layer}_conv_w'], g[f'ffn{layer}_conv_b'] = dcw, dcb.reshape(-1)
        g[f'ffn{layer}_w_up'] = (mm(h, dgate, 'tn', f'ffn{layer}_dwgate'), mm(h, dup, 'tn', f'ffn{layer}_dwup'))
        dh = mm(dgate, wup[:, :D_FF], 'nt', f'ffn{layer}_dh_gate', add=dh_res)
        return mm(dup, wup[:, D_FF:], 'nt', f'ffn{layer}_dh_up', add=dh)

    dh3_res, df1 = ln_bwd('l1_ln2_bwd', h3, f1, 1, 2, dh4)
    dh3 = ffn_bwd(1, h3, up1, act1, df1, dh3_res)
    dh2_res, dmixed1 = ln_bwd('l1_ln1_bwd', h2, mixed1, 1, 1, dh3)
    g['l1_w_out'] = (mm(y_c, dmixed1, 'tn', 'l1_dwout_a'), mm(y_d, dmixed1, 'tn', 'l1_dwout_b'))
    dy_c = mm(dmixed1, wo1[:512], 'nt', 'l1_dy_c')
    dy_d = mm(dmixed1, wo1[512:], 'nt', 'l1_dy_d')
    parts = {}
    if comm is None:
        dq_sb, dk_sb, dv_sb = sb_bwd(q_sb, k_sb, v_sb, _to_heads(dy_c, bsz, 8))
    else:
        dq_sb, dk_sb, dv_sb, parts['a'] = sb_bwd(q_sb, k_sb, v_sb, _to_heads(dy_c, bsz, 8),
                                                 ride=comm.grad_ride('a', g))
    dqn, dqp, dkn, dkp, dvv = mla_bwd(qn, qp, kn, kp, vv, _to_heads(dy_d, bsz, 8))
    dqp_raw = rope('rope_q_bwd', dqp, pos, inv_freq, -1.0)
    dkp_raw = rope('rope_k_bwd', dkp, pos, inv_freq, -1.0).reshape(n, MLA_ROPE)
    dq_all = _from_heads(jnp.concatenate([dqn, dqp_raw], axis=-1))
    dkv_all = _from_heads(jnp.concatenate([dkn, dvv], axis=-1))
    (dcq, dckv), (dqg, dwuq, dkvg, dwukv) = tok_bwd('mla_pre_bwd', f_mla_pre, mla_x, mla_p,
                                                    [[dq_all], [dkv_all]], tm)
    g['mla_q_norm_g'], g['mla_w_uq'] = dqg.reshape(-1), dwuq
    g['mla_kv_norm_g'], g['mla_w_ukv'] = dkvg.reshape(-1), dwukv
    dproj1 = jnp.concatenate([_from_heads(dq_sb), _from_heads(dk_sb), _from_heads(dv_sb), dcq, dckv,
                              jnp.pad(dkp_raw, ((0, 0), (0, LANES - MLA_ROPE)))], axis=1)
    g['l1_w_in'] = mm(h2, dproj1, 'tn', 'l1_dwin')[:, :1952]
    dh2 = mm(dproj1, win1, 'nt', 'l1_dh', add=dh2_res)

    dh1_res, df0 = ln_bwd('l0_ln2_bwd', h1, f0, 0, 2, dh2)
    dh1 = ffn_bwd(0, h1, up0, act0, df0, dh1_res)
    dx_res, dmixed0 = ln_bwd('l0_ln1_bwd', x2, mixed0, 0, 1, dh1)
    g['l0_w_out'] = (mm(y_a, dmixed0, 'tn', 'l0_dwout_a'), mm(y_b, dmixed0, 'tn', 'l0_dwout_b'))
    dy_a = mm(dmixed0, wo0[:512], 'nt', 'l0_dy_a')
    dy_b = mm(dmixed0, wo0[512:], 'nt', 'l0_dy_b')
    (dy_r, dr1, dkm1, dv1, dgt), (dlng, dlnb, drk) = tok_bwd('rwkv_post_bwd', f_rwkv_post, post_x, post_p,
                                                            [[dy_a]], tm)
    g['rwkv_ln_g'], g['rwkv_ln_b'] = dlng.reshape(-1), dlnb.reshape(-1)
    g['rwkv_r_k'] = drk.reshape(RWKV_HEADS, HEAD_DIM)
    (dys, dxs_skip, dz), (ddsk, dng) = tok_bwd('ssd_post_bwd', f_ssd_post, sp_x, sp_p, [[dy_b]], tm)
    g['ssm_d'], g['ssm_norm_g'] = ddsk[0, :SSM_HEADS], dng.reshape(-1)
    dxbc_act, ddtr, ddtb, dalog = ssd_bwd(xbc, p0, dt_col, dtb, alog, sstates, dys.reshape(bsz, t, SSM_DIM),
                                          dxs_skip.reshape(bsz, t, SSM_DIM))
    g['ssm_dt_bias'], g['ssm_a_log'] = ddtb[0, :SSM_HEADS], dalog[0, :SSM_HEADS]
    dxbc, dscw, dscb = dwconv_bwd('ssm_conv_bwd', p0, ssm_map, w['ssm_conv_w'], _row(a['ssm_conv_b']), 4, True,
                                  dxbc_act)
    g['ssm_conv_w'], g['ssm_conv_b'] = dscw, dscb.reshape(-1)
    if comm is None:
        dscan = rwkv_scan_bwd(*scan_in, rstates, _to_heads(dy_r, bsz, RWKV_HEADS))
    else:
        *dscan, parts['b'] = rwkv_scan_bwd(*scan_in, rstates, _to_heads(dy_r, bsz, RWKV_HEADS),
                                           ride=comm.grad_ride('b', g))
    dr2, dlw, dk2, dv2, dal, dbe = (_from_heads(u) for u in dscan)
    pre_ct = [[dr1, dr2], [dv1, dv2], [dlw], [dkm1, dk2], [dal], [dbe], [dgt]]
    dpre_x, dpre_p = tok_bwd('rwkv_pre_bwd', f_rwkv_pre, pre_x, pre_p, pre_ct, tm)
    g['rwkv_w0'], g['rwkv_a0'] = dpre_p[0].reshape(-1), dpre_p[2].reshape(-1)
    g['rwkv_w2'], g['rwkv_a2'], g['rwkv_g2'] = dpre_p[1][:64], dpre_p[3][64:], dpre_p[4]
    g['rwkv_k_k'], g['rwkv_k_a'] = dpre_p[5].reshape(-1), dpre_p[6].reshape(-1)
    dxs_r = jnp.concatenate(dpre_x, axis=1).reshape(bsz, t, 1792)
    d_rw, dtaps, _ = dwconv_bwd('rwkv_shift_bwd', p0, rw_map, taps, zero_b, 2, False, dxs_r)
    d_rw = d_rw.reshape(n, 1792)
    g['rwkv_mix'] = dtaps[0] - dtaps[1]
    dproj0 =jnp.concatenate([d_rw[:, :1536], dz, dxbc.reshape(n, 2 * SSM_DIM), d_rw[:, 1536:],
                              ddtr.reshape(n, LANES)], axis=1)
    dwin0 = mm(x2, dproj0, 'tn', 'l0_dwin')
    g['l0_w_in'] = jnp.concatenate([dwin0[:, 0:1536], dwin0[:, 3072:3328], dwin0[:, 1536:3072],
                                    dwin0[:, 3328:3336]], axis=1)
    dx = mm(dproj0, win0, 'nt', 'l0_dx', add=dx_res)
    if comm is not None:
        parts['c'] = peer_exchange('grad_exchange_c', comm.grad_ride('c', g)[0], False)
    return loss, dx.reshape(bsz, t, d), g, parts


GRAD_GROUPS = {
    'a': ['ffn1_w_up', 'ffn1_conv_w', 'ffn1_conv_b', 'ffn1_w_down', 'l1_ln2_g', 'l1_ln2_b'],
    'c': ['l0_w_in', 'rwkv_mix', 'rwkv_w0', 'rwkv_w2', 'rwkv_a0', 'rwkv_a2', 'rwkv_g2', 'rwkv_k_k', 'rwkv_k_a'],
}
GRAD_GROUPS['b'] = [nm for nm in WEIGHTS if nm not in GRAD_GROUPS['a'] + GRAD_GROUPS['c']]
L0_GATHER = ['l0_w_in', 'rwkv_w2', 'rwkv_a2', 'rwkv_g2', 'l0_w_out', 'ffn0_w_up', 'ffn0_w_down']
L1_GATHER = ['l1_w_in', 'mla_w_uq', 'mla_w_ukv', 'l1_w_out', 'ffn1_w_up', 'ffn1_w_down']
BF16_ROWS = 16


class _Comm:
    def __init__(self, a):
        self.a = a

    def _pieces(self, names):
        return [lax.bitcast_convert_type(self.a[nm], BF16) if nm in CONV_W else self.a[nm].astype(BF16)
                for nm in names]

    def _unpack(self, names, got):
        shapes = [p.shape for p in self._pieces(names)]
        blocks = [_unflatten(got[k], shapes, BF16_ROWS) for k in range(N_DEV)]
        out = {}
        for i, nm in enumerate(names):
            blk = [blocks[k][i] for k in range(N_DEV)]
            if nm in CONV_W:
                blk = [lax.bitcast_convert_type(b, F32) for b in blk]
            out[nm] = jnp.concatenate(blk, axis=SHARD_AXIS[nm])
        return out

    def layer0_weights(self):
        names = L0_GATHER + CONV_W
        shard = _flat_rows(self._pieces(names), BF16_ROWS, BF16_ROWS)
        return self._unpack(names, peer_exchange('gather_layer0_weights', shard, True))

    def layer1_weights_ride(self):
        return _flat_rows(self._pieces(L1_GATHER), BF16_ROWS, BF16_ROWS), True

    def layer1_weights(self, got):
        return self._unpack(L1_GATHER, got)

    def grad_ride(self, group, g):
        def shard_of(nm, k):
            gv = g[nm]
            if nm not in SHARD_AXIS:
                return gv
            per = N_DEV
            if isinstance(gv, tuple):
                gv, k, per = gv[k // 4], k % 4, 4
            width = gv.shape[SHARD_AXIS[nm]] // per
            return lax.slice_in_dim(gv, k * width, (k + 1) * width, axis=SHARD_AXIS[nm])

        names = GRAD_GROUPS[group]
        return jnp.stack([_flat_rows([shard_of(nm, k) for nm in names], SUBLANES, ADAM_ROWS)
                          for k in range(N_DEV)]), False


def _step(a):
    comm = _Comm(a)
    loss, dx, _, parts = _local_step(a, comm.layer0_weights(), comm)
    loss = lax.psum(loss, ('x', 'y', 'c'))
    res = {}
    for group, names in GRAD_GROUPS.items():
        flat = lambda prefix: _flat_rows([a[prefix + nm] for nm in names], SUBLANES, ADAM_ROWS)
        outs = adamw_sum(f'adamw_{group}', parts[group], flat(''), flat('m_'), flat('v_'))
        per_out = [_unflatten(o, [a[nm].shape for nm in names], SUBLANES) for o in outs]
        for i, nm in enumerate(names):
            res[nm] = [per_out[j][i] for j in range(4)]
    return (loss, dx, *[res[nm][j] for j in range(4) for nm in WEIGHTS])


def kernel(x, positions, l0_w_in, rwkv_mix, rwkv_w0, rwkv_w2, rwkv_a0, rwkv_a2, rwkv_g2, rwkv_k_k, rwkv_k_a, rwkv_r_k, rwkv_ln_g, rwkv_ln_b, ssm_conv_w, ssm_conv_b, ssm_dt_bias, ssm_a_log, ssm_d, ssm_norm_g, l0_w_out, l0_ln1_g, l0_ln1_b, ffn0_w_up, ffn0_conv_w, ffn0_conv_b, ffn0_w_down, l0_ln2_g, l0_ln2_b, l1_w_in, mla_q_norm_g, mla_w_uq, mla_kv_norm_g, mla_w_ukv, l1_w_out, l1_ln1_g, l1_ln1_b, ffn1_w_up, ffn1_conv_w, ffn1_conv_b, ffn1_w_down, l1_ln2_g, l1_ln2_b, loss_target, m_l0_w_in, m_rwkv_mix, m_rwkv_w0, m_rwkv_w2, m_rwkv_a0, m_rwkv_a2, m_rwkv_g2, m_rwkv_k_k, m_rwkv_k_a, m_rwkv_r_k, m_rwkv_ln_g, m_rwkv_ln_b, m_ssm_conv_w, m_ssm_conv_b, m_ssm_dt_bias, m_ssm_a_log, m_ssm_d, m_ssm_norm_g, m_l0_w_out, m_l0_ln1_g, m_l0_ln1_b, m_ffn0_w_up, m_ffn0_conv_w, m_ffn0_conv_b, m_ffn0_w_down, m_l0_ln2_g, m_l0_ln2_b, m_l1_w_in, m_mla_q_norm_g, m_mla_w_uq, m_mla_kv_norm_g, m_mla_w_ukv, m_l1_w_out, m_l1_ln1_g, m_l1_ln1_b, m_ffn1_w_up, m_ffn1_conv_w, m_ffn1_conv_b, m_ffn1_w_down, m_l1_ln2_g, m_l1_ln2_b, v_l0_w_in, v_rwkv_mix, v_rwkv_w0, v_rwkv_w2, v_rwkv_a0, v_rwkv_a2, v_rwkv_g2, v_rwkv_k_k, v_rwkv_k_a, v_rwkv_r_k, v_rwkv_ln_g, v_rwkv_ln_b, v_ssm_conv_w, v_ssm_conv_b, v_ssm_dt_bias, v_ssm_a_log, v_ssm_d, v_ssm_norm_g, v_l0_w_out, v_l0_ln1_g, v_l0_ln1_b, v_ffn0_w_up, v_ffn0_conv_w, v_ffn0_conv_b, v_ffn0_w_down, v_l0_ln2_g, v_l0_ln2_b, v_l1_w_in, v_mla_q_norm_g, v_mla_w_uq, v_mla_kv_norm_g, v_mla_w_ukv, v_l1_w_out, v_l1_ln1_g, v_l1_ln1_b, v_ffn1_w_up, v_ffn1_conv_w, v_ffn1_conv_b, v_ffn1_w_down, v_l1_ln2_g, v_l1_ln2_b):
    return _step(dict(locals()))
```

```python
import functools
import math

import jax
import jax.numpy as jnp
from jax import lax
from jax.experimental import pallas as pl
from jax.experimental.pallas import tpu as pltpu

F32 = jnp.float32
BF16 = jnp.bfloat16
HI = lax.Precision.HIGHEST

V7X_VMEM_BYTES = 64 * 1024 * 1024
VMEM_LIMIT = V7X_VMEM_BYTES - 8 * 1024 * 1024
LANES = 128
SUBLANES = 8
N_DEV = 8

D_MODEL = 1024
HEAD_DIM = 64
RWKV_DIM = 512
RWKV_HEADS = 8
RWKV_GN_EPS = 64e-5
RWKV_CHUNK = 64
SSM_DIM = 512
SSM_HEADS = 8
SSM_CHUNK = 128
SSM_STATE = 128
Q_BLOCK = 128
SB_HEADS_PER_STEP = 4
MLA_HEADS_PER_STEP = 4
MLA_NOPE = 64
MLA_ROPE = 32
ROPE_THETA = 10000.0
D_FF = 2816
DEPTH = 2
ALPHA = (2 * DEPTH) ** 0.25
L0_PAD = 3456
L1_PAD = 2048

ADAM_LR = 0.001
ADAM_B1 = 0.9
ADAM_B2 = 0.999
ADAM_EPS = 1e-08
ADAM_WD = 0.01
ADAM_STEP = 10

NEG_BIG = -1e30


def _params(sem=None):
    return pltpu.CompilerParams(dimension_semantics=sem, vmem_limit_bytes=VMEM_LIMIT)


P_F32, P_BF16, P_BF16X3 = 0, 1, 2


def _dg_raw(a, b, ca, cb, fast):
    dims = (((ca,), (cb,)), ((), ()))
    if fast == P_BF16:
        return lax.dot_general(a.astype(BF16), b.astype(BF16), dims, preferred_element_type=F32)
    prec = HI if fast == P_F32 else lax.Precision.HIGH
    return lax.dot_general(a, b, dims, precision=prec, preferred_element_type=F32)


@functools.partial(jax.custom_vjp, nondiff_argnums=(2, 3, 4))
def dg(a, b, ca, cb, fast):
    return _dg_raw(a, b, ca, cb, fast)


def _dg_fwd(a, b, ca, cb, fast):
    return _dg_raw(a, b, ca, cb, fast), (a, b)


def _dg_bwd(ca, cb, fast, res, ct):
    a, b = res
    fa, fb = 1 - ca, 1 - cb
    da = _dg_raw(ct, b, 1, fb, fast) if ca == 1 else _dg_raw(b, ct, fb, 1, fast)
    db = _dg_raw(a, ct, fa, 0, fast) if cb == 0 else _dg_raw(ct, a, 0, fa, fast)
    return da.astype(a.dtype), db.astype(b.dtype)


dg.defvjp(_dg_fwd, _dg_bwd)


def mmb(a, b):
    return dg(a, b, 1, 0, P_BF16)


def mmf(a, b):
    return dg(a, b, 1, 0, P_F32)


def mmf_nt(a, b):
    return dg(a, b, 1, 1, P_F32)


def mmf_tn(a, b):
    return dg(a, b, 0, 0, P_F32)


def mm3(a, b):
    return dg(a, b, 1, 0, P_BF16X3)


def mm3_nt(a, b):
    return dg(a, b, 1, 1, P_BF16X3)


def mm3_tn(a, b):
    return dg(a, b, 0, 0, P_BF16X3)


def _split3_dot(x, m01, cb):
    hi = x.astype(BF16)
    r1 = x - hi.astype(F32)
    mid = r1.astype(BF16)
    lo = (r1 - mid.astype(F32)).astype(BF16)
    rows = x.shape[0]
    out = lax.dot_general(jnp.concatenate([hi, mid, lo], axis=0), m01.astype(BF16), (((1,), (cb,)), ((), ())),
                          preferred_element_type=F32)
    return out[:rows] + out[rows:2 * rows] + out[2 * rows:]


def _lower_ones(n):
    return jnp.where(_iota((n, n), 0) >= _iota((n, n), 1), 1.0, 0.0)


@jax.custom_vjp
def suffix_sum(x):
    return _split3_dot(x, _lower_ones(x.shape[1]), 0)


def _suffix_sum_fwd(x):
    return suffix_sum(x), None


def _suffix_sum_bwd(_, ct):
    return (_split3_dot(ct, _lower_ones(ct.shape[1]), 1),)


suffix_sum.defvjp(_suffix_sum_fwd, _suffix_sum_bwd)


def _iota(shape, dim):
    return lax.broadcasted_iota(jnp.int32, shape, dim)


def _softplus(x):
    return jnp.maximum(x, 0.0) + jnp.log1p(jnp.exp(-jnp.abs(x)))


def _silu(x):
    return x * jax.nn.sigmoid(x)


def _largest_tile(n, cap, mult):
    best = None
    t = mult
    while t <= min(n, cap):
        if n % t == 0:
            best = t
        t += mult
    return n if best is None else best


MM_VMEM_BUDGET = 40 * 1024 * 1024
V7X_HBM_BYTES_PER_S = 3.2e12
GRID_STEP_S = 0.35e-6


def _mm_tiles(M, N, K, a_bytes, b_bytes, has_add):
    def divs(n):
        return [d for d in range(LANES, n + 1, LANES) if n % d == 0] or [n]

    best = None
    for tm in divs(M):
        for tn in divs(N):
            if tm * tn * 4 > 12 * 1024 * 1024:
                continue
            for tk in divs(K):
                vmem = (2 * (tm * tk * a_bytes + tk * tn * b_bytes) + 2 * tm * tn * 4 * (2 if has_add else 1)
                        + (tm * tk + tk * tn) * 2 + tm * tn * 4)
                if vmem > MM_VMEM_BUDGET:
                    continue
                ni, nj, nk = M // tm, N // tn, K // tk
                a_reads = M * K * a_bytes * (1 if nk == 1 else nj)
                b_reads = K * N * b_bytes * (1 if (nk == 1 and nj == 1) else ni)
                traffic = a_reads + b_reads + M * N * 4 * (2 if has_add else 1)
                cost = traffic / V7X_HBM_BYTES_PER_S + ni * nj * nk * GRID_STEP_S
                if min(tm, tn, tk) < 256 and min(M, N, K) >= 256:
                    cost *= 1.5
                if best is None or cost < best[0]:
                    best = (cost, tm, tn, tk)
    return best[1:]


def mm(a, b, mode, name, add=None):
    if mode == "nn":
        (M, K), N = a.shape, b.shape[1]
    elif mode == "nt":
        (M, K), N = a.shape, b.shape[0]
    else:
        (K, M), N = a.shape, b.shape[1]
    has_add = add is not None
    tm, tn, tk = _mm_tiles(M, N, K, a.dtype.itemsize, b.dtype.itemsize, has_add)
    nk = K // tk
    keep_a = nk == 1 and N // tn > 1 and a.dtype != BF16
    if mode == "nn":
        a_spec = pl.BlockSpec((tm, tk), lambda i, j, k: (i, k))
        b_spec = pl.BlockSpec((tk, tn), lambda i, j, k: (k, j))
        dims = (((1,), (0,)), ((), ()))
    elif mode == "nt":
        a_spec = pl.BlockSpec((tm, tk), lambda i, j, k: (i, k))
        b_spec = pl.BlockSpec((tn, tk), lambda i, j, k: (j, k))
        dims = (((1,), (1,)), ((), ()))
    else:
        a_spec = pl.BlockSpec((tk, tm), lambda i, j, k: (k, i))
        b_spec = pl.BlockSpec((tk, tn), lambda i, j, k: (k, j))
        dims = (((0,), (0,)), ((), ()))
    o_spec = pl.BlockSpec((tm, tn), lambda i, j, k: (i, j))

    def body(a_ref, b_ref, *rest):
        o_ref = rest[1] if has_add else rest[0]
        k = pl.program_id(2)
        if keep_a:
            a_bf = rest[-1]

            @pl.when(pl.program_id(1) == 0)
            def _():
                a_bf[...] = a_ref[...].astype(BF16)

            av = a_bf[...]
        else:
            av = a_ref[...].astype(BF16)
        part = lax.dot_general(av, b_ref[...].astype(BF16), dims, preferred_element_type=F32)

        @pl.when(k == 0)
        def _():
            o_ref[...] = part + rest[0][...] if has_add else part

        @pl.when(k > 0)
        def _():
            o_ref[...] += part

    ins = [a, b] + ([add] if has_add else [])
    specs = [a_spec, b_spec] + ([o_spec] if has_add else [])
    return pl.pallas_call(
        body, name=name, grid=(M // tm, N // tn, nk), in_specs=specs, out_specs=o_spec,
        out_shape=jax.ShapeDtypeStruct((M, N), F32),
        scratch_shapes=[pltpu.VMEM(a_spec.block_shape, BF16)] if keep_a else [],
        compiler_params=_params(("parallel", "arbitrary", "arbitrary")),
    )(*ins)


def _x_specs(xs, tm):
    return [pl.BlockSpec((tm, w), functools.partial(lambda i, cb: (i, cb), cb=cb)) for _, w, cb in xs]


def _p_specs(ps):
    return [pl.BlockSpec(p.shape, lambda i: (0, 0)) for p in ps]


def tok_fwd(name, f, xs, ps, out_widths, tm):
    n = xs[0][0].shape[0]
    nx, npar = len(xs), len(ps)

    def body(*refs):
        xv = [r[...] for r in refs[:nx]]
        pv = [r[...].astype(F32) for r in refs[nx:nx + npar]]
        outs = f(*xv, *pv)
        for o, r in zip(outs, refs[nx + npar:]):
            r[...] = o

    return pl.pallas_call(
        body, name=name, grid=(n // tm,),
        in_specs=_x_specs(xs, tm) + _p_specs(ps),
        out_specs=[pl.BlockSpec((tm, w), lambda i: (i, 0)) for w in out_widths],
        out_shape=[jax.ShapeDtypeStruct((n, w), F32) for w in out_widths],
        compiler_params=_params(("parallel",)),
    )(*[x[0] for x in xs], *ps)


def tok_bwd(name, f, xs, ps, cts, tm):
    n = xs[0][0].shape[0]
    nx, npar = len(xs), len(ps)
    ct_flat = [c for group in cts for c in group]
    nct = len(ct_flat)

    def body(*refs):
        xv = [r[...] for r in refs[:nx]]
        pv = [r[...].astype(F32) for r in refs[nx:nx + npar]]
        ct_refs = refs[nx + npar:nx + npar + nct]
        dx_refs = refs[nx + npar + nct:nx + npar + nct + nx]
        dp_refs = refs[nx + npar + nct + nx:]
        cv, pos = [], 0
        for group in cts:
            acc = ct_refs[pos][...]
            for r in ct_refs[pos + 1:pos + len(group)]:
                acc = acc + r[...]
            cv.append(acc)
            pos += len(group)
        _, vjp = jax.vjp(f, *xv, *pv)
        grads = vjp(tuple(cv))
        for g, r in zip(grads[:nx], dx_refs):
            r[...] = g

        @pl.when(pl.program_id(0) == 0)
        def _():
            for r in dp_refs:
                r[...] = jnp.zeros_like(r)

        for g, r in zip(grads[nx:], dp_refs):
            r[...] += g

    outs = pl.pallas_call(
        body, name=name, grid=(n // tm,),
        in_specs=(_x_specs(xs, tm) + _p_specs(ps)
                  + [pl.BlockSpec((tm, c.shape[1]), lambda i: (i, 0)) for c in ct_flat]),
        out_specs=([pl.BlockSpec((tm, w), lambda i: (i, 0)) for _, w, _ in xs] + _p_specs(ps)),
        out_shape=([jax.ShapeDtypeStruct((n, w), F32) for _, w, _ in xs]
                   + [jax.ShapeDtypeStruct(p.shape, F32) for p in ps]),
        compiler_params=_params(("arbitrary",)),
    )(*[x[0] for x in xs], *ps, *ct_flat)
    return outs[:nx], outs[nx:]


def f_ln(h, y, g, b):
    pre = ALPHA * h + y
    mu = jnp.mean(pre, axis=-1, keepdims=True)
    xc = pre - mu
    var = jnp.mean(xc * xc, axis=-1, keepdims=True)
    return (xc * lax.rsqrt(var + 1e-5) * g + b,)


def _head_sel(width, nheads_pad, per):
    return jnp.where(_iota((width, nheads_pad), 0) // per == _iota((width, nheads_pad), 1), 1.0, 0.0).astype(F32)


def _head_sel_t(nheads_pad, width, per):
    return jnp.where(_iota((nheads_pad, width), 1) // per == _iota((nheads_pad, width), 0), 1.0, 0.0).astype(F32)


@jax.custom_vjp
def head_sum(x):
    return _split3_dot(x, _head_sel(RWKV_DIM, LANES, HEAD_DIM), 0)


@jax.custom_vjp
def head_spread(y):
    return _split3_dot(y, _head_sel(RWKV_DIM, LANES, HEAD_DIM), 1)


head_sum.defvjp(lambda x: (head_sum(x), None), lambda _, ct: (head_spread(ct),))
head_spread.defvjp(lambda y: (head_spread(y), None), lambda _, ct: (head_sum(ct),))


def f_rwkv_pre(r, k, v, lora, glo, w0, w2p, a0, a2p, g2, k_k, k_a):
    lane = _iota(lora.shape, 1)
    tw = jnp.where(lane < 64, jnp.tanh(lora), 0.0)
    ta = jnp.where(lane >= 64, lora, 0.0)
    log_w = -_softplus(-(w0 + mmb(tw, w2p))) - 0.5
    lw = -jnp.exp(log_w)
    a = jax.nn.sigmoid(a0 + mmb(ta, a2p))
    g = mmb(jax.nn.sigmoid(glo), g2)
    kk = k * k_k
    nrm = jnp.sqrt(jnp.maximum(head_sum(kk * kk), 1e-24))
    kkn = kk * head_spread(1.0 / nrm)
    kmod = k * (1.0 + (a - 1.0) * k_a)
    return r, v, lw, kmod, -kkn, kkn * a, g


def f_rwkv_post(y, r, kmod, v, g, ln_g, ln_b, r_k):
    inv = 1.0 / HEAD_DIM
    mu = head_spread(head_sum(y) * inv)
    yc = y - mu
    var = head_sum(yc * yc) * inv
    rstd = head_spread(lax.rsqrt(var + RWKV_GN_EPS))
    yn = yc * rstd * ln_g + ln_b
    bonus = head_spread(head_sum(r * kmod * r_k)) * v
    return ((yn + bonus) * g,)


def f_ssd_post(y, xs, z, d_skip, norm_g):
    sel_t = _head_sel_t(LANES, SSM_DIM, HEAD_DIM)
    d_e = jnp.sum(mmf(jnp.broadcast_to(d_skip, (SUBLANES, LANES)), sel_t), axis=0, keepdims=True) * (1.0 / SUBLANES)
    u = (y + xs * d_e) * _silu(z)
    first = _iota(u.shape, 1) < (SSM_DIM // 2)
    uu = u * u
    inv = 2.0 / SSM_DIM
    ms0 = jnp.sum(jnp.where(first, uu, 0.0), axis=-1, keepdims=True) * inv
    ms1 = jnp.sum(jnp.where(first, 0.0, uu), axis=-1, keepdims=True) * inv
    ms = jnp.where(first, ms0, ms1)
    return (u * lax.rsqrt(ms + 1e-5) * norm_g,)


def f_mla_pre(cq, ckv, qg, w_uq, kvg, w_ukv):
    def rms(x, g):
        return x * lax.rsqrt(jnp.mean(x * x, axis=-1, keepdims=True) + 1e-6) * g
    return mmb(rms(cq, qg), w_uq), mmb(rms(ckv, kvg), w_ukv)


def _shift_down(x, s, row):
    return x if s == 0 else jnp.where(row >= s, pltpu.roll(x, s, 0), 0.0)


def _shift_up(x, s, row, t):
    return x if s == 0 else jnp.where(row < t - s, pltpu.roll(x, t - s, 0), 0.0)


def dwconv_fwd(name, u, colmap, w, b, taps, silu, upmap=None):
    bsz, t, _ = u.shape
    c = w.shape[1]
    tc = LANES
    has_up = upmap is not None

    def body(*refs):
        u_ref, w_ref, b_ref = refs[:3]
        o_ref = refs[-1]
        uv = u_ref[...]
        wv = w_ref[...]
        row = _iota(uv.shape, 0)
        acc = jnp.broadcast_to(b_ref[...], uv.shape)
        for i in range(taps):
            acc = acc + wv[i:i + 1, :] * _shift_down(uv, taps - 1 - i, row)
        if silu:
            acc = _silu(acc)
        if has_up:
            acc = acc * refs[3][...]
        o_ref[...] = acc

    specs = [pl.BlockSpec((None, t, tc), lambda bb, j: (bb, 0, colmap(j))),
             pl.BlockSpec((taps, tc), lambda bb, j: (0, j)),
             pl.BlockSpec((1, tc), lambda bb, j: (0, j))]
    ins = [u, w, b]
    if has_up:
        specs.append(pl.BlockSpec((None, t, tc), lambda bb, j: (bb, 0, upmap(j))))
        ins.append(u)
    return pl.pallas_call(
        body, name=name, grid=(bsz, c // tc), in_specs=specs,
        out_specs=pl.BlockSpec((None, t, tc), lambda bb, j: (bb, 0, j)),
        out_shape=jax.ShapeDtypeStruct((bsz, t, c), F32),
        compiler_params=_params(("parallel", "parallel")),
    )(*ins)


def dwconv_bwd(name, u, colmap, w, b, taps, silu, dout, upmap=None):
    bsz, t, _ = u.shape
    c = w.shape[1]
    tc = LANES
    has_up = upmap is not None

    def body(*refs):
        u_ref, w_ref, b_ref, d_ref = refs[:4]
        nin = 5 if has_up else 4
        du_ref, dw_ref, db_ref = refs[nin:nin + 3]
        uv = u_ref[...]
        wv = w_ref[...]
        dv = d_ref[...]
        row = _iota(uv.shape, 0)
        shifted = [_shift_down(uv, taps - 1 - i, row) for i in range(taps)]
        cg = jnp.broadcast_to(b_ref[...], uv.shape)
        for i in range(taps):
            cg = cg + wv[i:i + 1, :] * shifted[i]
        if silu:
            sg = jax.nn.sigmoid(cg)
            act = cg * sg
            dact_dcg = sg * (1.0 + cg * (1.0 - sg))
        else:
            act = cg
            dact_dcg = None
        if has_up:
            refs[nin + 3][...] = dv * act
            dv = dv * refs[4][...]
        dcg = dv * dact_dcg if silu else dv
        du = jnp.zeros_like(uv)
        for i in range(taps):
            du = du + wv[i:i + 1, :] * _shift_up(dcg, taps - 1 - i, row, t)
        du_ref[...] = du

        @pl.when(pl.program_id(1) == 0)
        def _():
            dw_ref[...] = jnp.zeros_like(dw_ref)
            db_ref[...] = jnp.zeros_like(db_ref)

        for i in range(taps):
            dw_ref[i:i + 1, :] += jnp.sum(dcg * shifted[i], axis=0, keepdims=True)
        db_ref[...] += jnp.sum(dcg, axis=0, keepdims=True)

    specs = [pl.BlockSpec((None, t, tc), lambda j, bb: (bb, 0, colmap(j))),
             pl.BlockSpec((taps, tc), lambda j, bb: (0, j)),
             pl.BlockSpec((1, tc), lambda j, bb: (0, j)),
             pl.BlockSpec((None, t, tc), lambda j, bb: (bb, 0, j))]
    ins = [u, w, b, dout]
    if has_up:
        specs.append(pl.BlockSpec((None, t, tc), lambda j, bb: (bb, 0, upmap(j))))
        ins.append(u)
    big = pl.BlockSpec((None, t, tc), lambda j, bb: (bb, 0, j))
    out_specs = [big, pl.BlockSpec((taps, tc), lambda j, bb: (0, j)), pl.BlockSpec((1, tc), lambda j, bb: (0, j))]
    out_shape = [jax.ShapeDtypeStruct((bsz, t, c), F32), jax.ShapeDtypeStruct((taps, c), F32),
                 jax.ShapeDtypeStruct((1, c), F32)]
    if has_up:
        out_specs.append(big)
        out_shape.append(jax.ShapeDtypeStruct((bsz, t, c), F32))
    return pl.pallas_call(
        body, name=name, grid=(c // tc, bsz), in_specs=specs, out_specs=out_specs, out_shape=out_shape,
        compiler_params=_params(("parallel", "arbitrary")),
    )(*ins)


def _each(f, *lists):
    return [f(*xs) for xs in zip(*lists)]


def rwkv_chunk(s0, r, lw, k, v, al, be):
    c = r[0].shape[0]
    ii, jj = _iota((c, c), 0), _iota((c, c), 1)
    incl, strict = ii >= jj, ii > jj
    ones_incl = jnp.where(incl, 1.0, 0.0)
    eye = jnp.where(ii == jj, 1.0, 0.0)
    cum = _each(lambda x: mmf(ones_incl, x), lw)
    gam_inv = _each(lambda x: jnp.exp(-x), cum)
    at = _each(lambda a_, c_, l_: a_ * jnp.exp(c_ - l_), al, cum, lw)
    rt = _each(lambda r_, c_: r_ * jnp.exp(c_), r, cum)
    bt = _each(lambda b_, g_: b_ * g_, be, gam_inv)
    kt = _each(lambda k_, g_: k_ * g_, k, gam_inv)
    a_b = _each(lambda x, y_: jnp.where(strict, mm3_nt(x, y_), 0.0), at, bt)
    a_k = _each(lambda x, y_: jnp.where(strict, mm3_nt(x, y_), 0.0), at, kt)
    rhs0 = _each(mm3_nt, at, s0)
    rhs = _each(lambda x, a_, v_: x + mm3(a_, v_), rhs0, a_k, v)
    p = _each(lambda x: eye + x, a_b)
    m = a_b
    for _ in range(int(math.log2(c)) - 1):
        m = _each(mm3, m, m)
        p = _each(lambda p_, m_: p_ + mm3(p_, m_), p, m)
    u = _each(mm3, p, rhs)
    r_b = _each(lambda x, y_: jnp.where(incl, mm3_nt(x, y_), 0.0), rt, bt)
    r_k = _each(lambda x, y_: jnp.where(incl, mm3_nt(x, y_), 0.0), rt, kt)
    y0 = _each(mm3_nt, rt, s0)
    y1 = _each(lambda y_, b_, u_: y_ + mm3(b_, u_), y0, r_b, u)
    y = _each(lambda y_, k_, v_: y_ + mm3(k_, v_), y1, r_k, v)
    su = _each(mm3_tn, u, bt)
    sv = _each(mm3_tn, v, kt)
    s1 = _each(lambda s_, a_, b_, l_: (s_ + a_ + b_) * jnp.exp(jnp.sum(l_, axis=0, keepdims=True)), s0, su, sv, lw)
    return y, s1


def rwkv_scan_fwd(r, lw, k, v, al, be, ride=None):
    bsz, h, t, d = r.shape
    c = RWKV_CHUNK
    nc = t // c
    grid = (bsz, nc)
    r_in, r_specs, r_out, r_ospecs, r_scr = _ride_args(ride)

    def body(*refs):
        r_ref, lw_ref, k_ref, v_ref, al_ref, be_ref = refs[:6]
        y_ref, st_ref = refs[6 + len(r_in):8 + len(r_in)]
        s_scr = refs[8 + 2 * len(r_in)]
        if ride is not None:
            first, last = _grid_first_last(grid)
            copies = _ride_start((refs[6], refs[8 + len(r_in)], *refs[-3:]), ride[1], first)

        @pl.when(pl.program_id(1) == 0)
        def _():
            s_scr[...] = jnp.zeros_like(s_scr)

        heads = lambda ref: [ref[hh] for hh in range(h)]
        s0 = heads(s_scr)
        y, s1 = rwkv_chunk(s0, heads(r_ref), heads(lw_ref), heads(k_ref), heads(v_ref), heads(al_ref),
                           heads(be_ref))
        for hh in range(h):
            st_ref[hh] = s0[hh]
            y_ref[hh] = y[hh]
            s_scr[hh] = s1[hh]
        if ride is not None:
            _ride_wait(copies, last)

    seq = pl.BlockSpec((None, h, c, d), lambda b, i: (b, 0, i, 0))
    return pl.pallas_call(
        body, name="rwkv_scan_fwd", grid=grid, in_specs=[seq] * 6 + r_specs,
        out_specs=[seq, pl.BlockSpec((None, h, None, d, d), lambda b, i: (b, 0, i, 0, 0))] + r_ospecs,
        out_shape=[jax.ShapeDtypeStruct((bsz, h, t, d), F32), jax.ShapeDtypeStruct((bsz, h, nc, d, d), F32)] + r_out,
        scratch_shapes=[pltpu.VMEM((h, d, d), F32)] + r_scr,
        compiler_params=_params(("arbitrary", "arbitrary")),
    )(r, lw, k, v, al, be, *r_in)


def rwkv_scan_bwd(r, lw, k, v, al, be, states, dy, ride=None):
    bsz, h, t, d = r.shape
    c = RWKV_CHUNK
    nc = t // c
    grid = (bsz, nc)
    r_in, r_specs, r_out, r_ospecs, r_scr = _ride_args(ride)

    def body(*refs):
        r_ref, lw_ref, k_ref, v_ref, al_ref, be_ref, st_ref, dy_ref = refs[:8]
        nin = 8 + len(r_in)
        dr_ref, dlw_ref, dk_ref, dv_ref, dal_ref, dbe_ref = refs[nin:nin + 6]
        ds_scr = refs[nin + 6 + len(r_in)]
        if ride is not None:
            first, last = _grid_first_last(grid)
            copies = _ride_start((refs[8], refs[nin + 6], *refs[-3:]), ride[1], first)

        @pl.when(pl.program_id(1) == 0)
        def _():
            ds_scr[...] = jnp.zeros_like(ds_scr)

        heads = lambda ref: [ref[hh] for hh in range(h)]
        _, vjp = jax.vjp(rwkv_chunk, heads(st_ref), heads(r_ref), heads(lw_ref), heads(k_ref), heads(v_ref),
                         heads(al_ref), heads(be_ref))
        grads = vjp((heads(dy_ref), heads(ds_scr)))
        for ref, gl in zip((ds_scr, dr_ref, dlw_ref, dk_ref, dv_ref, dal_ref, dbe_ref), grads):
            for hh in range(h):
                ref[hh] = gl[hh]
        if ride is not None:
            _ride_wait(copies, last)

    seq = pl.BlockSpec((None, h, c, d), lambda b, i: (b, 0, nc - 1 - i, 0))
    st = pl.BlockSpec((None, h, None, d, d), lambda b, i: (b, 0, nc - 1 - i, 0, 0))
    return pl.pallas_call(
        body, name="rwkv_scan_bwd", grid=grid, in_specs=[seq] * 6 + [st, seq] + r_specs,
        out_specs=[seq] * 6 + r_ospecs, out_shape=[jax.ShapeDtypeStruct((bsz, h, t, d), F32)] * 6 + r_out,
        scratch_shapes=[pltpu.VMEM((h, d, d), F32)] + r_scr,
        compiler_params=_params(("arbitrary", "arbitrary")),
    )(r, lw, k, v, al, be, states, dy, *r_in)


def ssd_chunk(st, xs, bm, cm, dtr, dt_bias, a_log):
    n = SSM_CHUNK
    ii, jj = _iota((n, n), 0), _iota((n, n), 1)
    incl = ii >= jj
    lane = _iota((n, LANES), 1)
    dt = _softplus(dtr + dt_bias)
    a = dt * (-jnp.exp(a_log))
    acum = mmf(jnp.where(incl, 1.0, 0.0), a)
    last_row = jnp.where(jj == n - 1, 1.0, 0.0)
    cb = [mmf_nt(cm[g], bm[g]) for g in range(2)]
    pairs, heads = range(4), range(SSM_HEADS)
    e_m = [jnp.where(_iota((LANES, LANES), 0) == 2 * m + _iota((LANES, LANES), 1) // HEAD_DIM, 1.0, 0.0)
           for m in pairs]
    dt_m = [mmf(dt, e_m[m]) for m in pairs]
    ac_m = [mmf(acum, e_m[m]) for m in pairs]
    x = [xs[m] * dt_m[m] for m in pairs]
    last_m = [mmf(last_row, ac_m[m]) for m in pairs]
    colb = [mmf(acum, jnp.where(_iota((LANES, n), 0) == h, 1.0, 0.0)) for h in heads]
    decay = [jnp.exp(jnp.where(incl, colb[h] - colb[h].T, NEG_BIG)) for h in heads]
    yh = [mmf(cb[h // 4] * decay[h], x[h // 2]) for h in heads]
    y_off = [mmf(cm[m // 2], st[m]) for m in pairs]
    ys = [jnp.where(lane // HEAD_DIM == 0, yh[2 * m], yh[2 * m + 1]) + jnp.exp(ac_m[m]) * y_off[m] for m in pairs]
    st_in = [mmf_tn(bm[m // 2], x[m] * jnp.exp(last_m[m] - ac_m[m])) for m in pairs]
    st_new = [jnp.exp(last_m[m]) * st[m] + st_in[m] for m in pairs]
    return tuple(ys), tuple(st_new)


def _ssd_load(xbc_ref, dtr_ref):
    xs = tuple(xbc_ref[:, m * LANES:(m + 1) * LANES] for m in range(4))
    bm = tuple(xbc_ref[:, SSM_DIM + g * LANES:SSM_DIM + (g + 1) * LANES] for g in range(2))
    cm = tuple(xbc_ref[:, SSM_DIM + 2 * LANES + g * LANES:SSM_DIM + 2 * LANES + (g + 1) * LANES] for g in range(2))
    return xs, bm, cm, dtr_ref[...]


def ssd_fwd(xbc, proj, dt_col, dt_bias, a_log):
    bsz, t, _ = xbc.shape
    n = SSM_CHUNK
    nc = t // n

    def body(xbc_ref, dtr_ref, dtb_ref, al_ref, y_ref, st_ref, s_scr):
        @pl.when(pl.program_id(1) == 0)
        def _():
            s_scr[...] = jnp.zeros_like(s_scr)

        st = tuple(s_scr[m] for m in range(4))
        for m in range(4):
            st_ref[m] = st[m]
        xs, bm, cm, dtr = _ssd_load(xbc_ref, dtr_ref)
        ys, st_new = ssd_chunk(st, xs, bm, cm, dtr, dtb_ref[...], al_ref[...])
        for m in range(4):
            y_ref[:, m * LANES:(m + 1) * LANES] = ys[m]
            s_scr[m] = st_new[m]

    vec = pl.BlockSpec((1, LANES), lambda b, i: (0, 0))
    return pl.pallas_call(
        body, name="ssd_fwd", grid=(bsz, nc),
        in_specs=[pl.BlockSpec((None, n, 2 * SSM_DIM), lambda b, i: (b, i, 0)),
                  pl.BlockSpec((None, n, LANES), lambda b, i: (b, i, dt_col)), vec, vec],
        out_specs=[pl.BlockSpec((None, n, SSM_DIM), lambda b, i: (b, i, 0)),
                   pl.BlockSpec((None, None, 4, SSM_STATE, LANES), lambda b, i: (b, i, 0, 0, 0))],
        out_shape=[jax.ShapeDtypeStruct((bsz, t, SSM_DIM), F32),
                   jax.ShapeDtypeStruct((bsz, nc, 4, SSM_STATE, LANES), F32)],
        scratch_shapes=[pltpu.VMEM((4, SSM_STATE, LANES), F32)],
        compiler_params=_params(("parallel", "arbitrary")),
    )(xbc, proj, dt_bias, a_log)


def ssd_bwd(xbc, proj, dt_col, dt_bias, a_log, states, dy, dxs_extra):
    bsz, t, _ = xbc.shape
    n = SSM_CHUNK
    nc = t // n

    def body(xbc_ref, dtr_ref, dtb_ref, al_ref, st_ref, dy_ref, ex_ref,
             dxbc_ref, ddtr_ref, ddtb_ref, dal_ref, ds_scr):
        first = jnp.logical_and(pl.program_id(0) == 0, pl.program_id(1) == 0)

        @pl.when(pl.program_id(1) == 0)
        def _():
            ds_scr[...] = jnp.zeros_like(ds_scr)

        @pl.when(first)
        def _():
            ddtb_ref[...] = jnp.zeros_like(ddtb_ref)
            dal_ref[...] = jnp.zeros_like(dal_ref)

        st = tuple(st_ref[m] for m in range(4))
        xs, bm, cm, dtr = _ssd_load(xbc_ref, dtr_ref)
        _, vjp = jax.vjp(ssd_chunk, st, xs, bm, cm, dtr, dtb_ref[...], al_ref[...])
        dys = tuple(dy_ref[:, m * LANES:(m + 1) * LANES] for m in range(4))
        dst_in = tuple(ds_scr[m] for m in range(4))
        dst, dxs, dbm, dcm, ddtr, ddtb, dal = vjp((dys, dst_in))
        for m in range(4):
            ds_scr[m] = dst[m]
            sl = slice(m * LANES, (m + 1) * LANES)
            dxbc_ref[:, sl] = dxs[m] + ex_ref[:, sl]
        for g in range(2):
            dxbc_ref[:, SSM_DIM + g * LANES:SSM_DIM + (g + 1) * LANES] = dbm[g]
            dxbc_ref[:, SSM_DIM + 2 * LANES + g * LANES:SSM_DIM + 2 * LANES + (g + 1) * LANES] = dcm[g]
        ddtr_ref[...] = ddtr
        ddtb_ref[...] += ddtb
        dal_ref[...] += dal

    vec = pl.BlockSpec((1, LANES), lambda b, i: (0, 0))
    rev = lambda b, i: (b, nc - 1 - i, 0)
    return pl.pallas_call(
        body, name="ssd_bwd", grid=(bsz, nc),
        in_specs=[pl.BlockSpec((None, n, 2 * SSM_DIM), rev),
                  pl.BlockSpec((None, n, LANES), lambda b, i: (b, nc - 1 - i, dt_col)), vec, vec,
                  pl.BlockSpec((None, None, 4, SSM_STATE, LANES), lambda b, i: (b, nc - 1 - i, 0, 0, 0)),
                  pl.BlockSpec((None, n, SSM_DIM), rev), pl.BlockSpec((None, n, SSM_DIM), rev)],
        out_specs=[pl.BlockSpec((None, n, 2 * SSM_DIM), rev), pl.BlockSpec((None, n, LANES), rev), vec, vec],
        out_shape=[jax.ShapeDtypeStruct((bsz, t, 2 * SSM_DIM), F32), jax.ShapeDtypeStruct((bsz, t, LANES), F32),
                   jax.ShapeDtypeStruct((1, LANES), F32), jax.ShapeDtypeStruct((1, LANES), F32)],
        scratch_shapes=[pltpu.VMEM((4, SSM_STATE, LANES), F32)],
        compiler_params=_params(("arbitrary", "arbitrary")),
    )(xbc, proj, dt_bias, a_log, states, dy, dxs_extra)


def sb_block(q, kj, vj, carry, maskf):
    mask = maskf > 0.5
    z = _each(lambda q_, k_: dg(q_, k_, 1, 1, P_BF16) * (HEAD_DIM ** -0.5), q, kj)
    ls = _each(lambda z_: -_softplus(-z_), z)
    lk = _each(lambda l_, z_: jnp.where(mask, l_ - z_, 0.0), ls, z)
    sfx = _each(suffix_sum, lk)
    att = _each(lambda l_, c_, s_, k_: jnp.where(mask, jnp.exp(l_ + c_ + s_ - k_), 0.0), ls, carry, sfx, lk)
    out = _each(mmb, att, vj)
    return out, _each(lambda c_, k_: c_ + jnp.sum(k_, axis=1, keepdims=True), carry, lk)


def _sb_mask(qi, j):
    n = Q_BLOCK
    return jnp.where(j * n + _iota((n, n), 1) < qi * n + _iota((n, n), 0), 1.0, 0.0)


def sb_fwd(q, k, v, ride=None):
    bsz, h, t, d = q.shape
    n = Q_BLOCK
    hp = SB_HEADS_PER_STEP
    grid = (bsz, h // hp, t // n)
    r_in, r_specs, r_out, r_ospecs, r_scr = _ride_args(ride)

    def body(*refs):
        q_ref, k_ref, v_ref = refs[:3]
        o_ref = refs[3 + len(r_in)]
        if ride is not None:
            first, last = _grid_first_last(grid)
            copies = _ride_start((refs[3], refs[4 + len(r_in)], *refs[-3:]), ride[1], first)
        qi = pl.program_id(2)

        def step(i, state):
            acc, carry = state
            j = qi - i
            rows = pl.ds(pl.multiple_of(j * n, n), n)
            o, carry = sb_block([q_ref[hh] for hh in range(hp)], [k_ref[hh, rows, :] for hh in range(hp)],
                                [v_ref[hh, rows, :] for hh in range(hp)], carry, _sb_mask(qi, j))
            return [a_ + o_ for a_, o_ in zip(acc, o)], carry

        init = ([jnp.zeros((n, d), F32) for _ in range(hp)], [jnp.zeros((n, 1), F32) for _ in range(hp)])
        acc, _ = lax.fori_loop(0, qi + 1, step, init)
        for hh in range(hp):
            o_ref[hh] = acc[hh]
        if ride is not None:
            _ride_wait(copies, last)

    blk = pl.BlockSpec((None, hp, n, d), lambda b, hg, i: (b, hg, i, 0))
    full = pl.BlockSpec((None, hp, t, d), lambda b, hg, i: (b, hg, 0, 0))
    return pl.pallas_call(
        body, name="sb_fwd", grid=grid, in_specs=[blk, full, full] + r_specs, out_specs=[blk] + r_ospecs,
        out_shape=[jax.ShapeDtypeStruct((bsz, h, t, d), F32)] + r_out, scratch_shapes=r_scr,
        compiler_params=_params(("arbitrary", "arbitrary", "arbitrary")),
    )(q, k, v, *r_in)


def sb_bwd(q, k, v, do, ride=None):
    bsz, h, t, d = q.shape
    n = Q_BLOCK
    hp = SB_HEADS_PER_STEP
    grid = (bsz, h // hp, t // n)
    r_in, r_specs, r_out, r_ospecs, r_scr = _ride_args(ride)

    def body(*refs):
        q_ref, k_ref, v_ref, do_ref = refs[:4]
        nin = 4 + len(r_in)
        dq_ref, dk_ref, dv_ref = refs[nin:nin + 3]
        carries = refs[nin + 3 + len(r_in)]
        if ride is not None:
            first, last = _grid_first_last(grid)
            copies = _ride_start((refs[4], refs[nin + 3], *refs[-3:]), ride[1], first)
        qi = pl.program_id(2)

        @pl.when(qi == 0)
        def _():
            dk_ref[...] = jnp.zeros_like(dk_ref)
            dv_ref[...] = jnp.zeros_like(dv_ref)

        heads = range(hp)
        qv = [q_ref[hh] for hh in heads]

        def fwd_step(i, carry):
            j = qi - i
            rows = pl.ds(pl.multiple_of(j * n, n), n)
            for hh in heads:
                carries[hh, j] = carry[hh]
            return sb_block(qv, [k_ref[hh, rows, :] for hh in heads], [v_ref[hh, rows, :] for hh in heads],
                            carry, _sb_mask(qi, j))[1]

        lax.fori_loop(0, qi + 1, fwd_step, [jnp.zeros((n, 1), F32) for _ in heads])

        def bwd_step(j, state):
            dq, dcarry = state
            rows = pl.ds(pl.multiple_of(j * n, n), n)
            _, vjp = jax.vjp(sb_block, qv, [k_ref[hh, rows, :] for hh in heads],
                             [v_ref[hh, rows, :] for hh in heads], [carries[hh, j] for hh in heads], _sb_mask(qi, j))
            dqj, dkj, dvj, dc, _ = vjp(([do_ref[hh] for hh in heads], dcarry))
            for hh in heads:
                dk_ref[hh, rows, :] += dkj[hh]
                dv_ref[hh, rows, :] += dvj[hh]
            return [a_ + b_ for a_, b_ in zip(dq, dqj)], dc

        init = ([jnp.zeros((n, d), F32) for _ in heads], [jnp.zeros((n, 1), F32) for _ in heads])
        dq, _ = lax.fori_loop(0, qi + 1, bwd_step, init)
        for hh in heads:
            dq_ref[hh] = dq[hh]
        if ride is not None:
            _ride_wait(copies, last)

    blk = pl.BlockSpec((None, hp, n, d), lambda b, hg, i: (b, hg, i, 0))
    full = pl.BlockSpec((None, hp, t, d), lambda b, hg, i: (b, hg, 0, 0))
    shp = jax.ShapeDtypeStruct((bsz, h, t, d), F32)
    return pl.pallas_call(
        body, name="sb_bwd", grid=grid, in_specs=[blk, full, full, blk] + r_specs,
        out_specs=[blk, full, full] + r_ospecs, out_shape=[shp, shp, shp] + r_out,
        scratch_shapes=[pltpu.VMEM((hp, t // n, n, 1), F32)] + r_scr,
        compiler_params=_params(("arbitrary", "arbitrary", "arbitrary")),
    )(q, k, v, do, *r_in)


def _bdot(a, b, ca, cb):
    return _dg_raw(a, b, ca, cb, P_BF16)


def _mla_scores(qn, qp, knj, kpj, qi, j):
    n = Q_BLOCK
    mask = j * n + _iota((n, n), 1) <= qi * n + _iota((n, n), 0)
    scale = (MLA_NOPE + MLA_ROPE) ** -0.5
    return _each(lambda a_, b_, k_: jnp.where(mask, (_bdot(a_, k_, 1, 1) + _bdot(b_, kpj, 1, 1)) * scale, NEG_BIG),
                 qn, qp, knj)


def _mla_specs(t, hp):
    n = Q_BLOCK
    return (pl.BlockSpec((None, hp, n, MLA_NOPE), lambda b, hg, i: (b, hg, i, 0)),
            pl.BlockSpec((None, hp, n, MLA_ROPE), lambda b, hg, i: (b, hg, i, 0)),
            pl.BlockSpec((None, hp, t, MLA_NOPE), lambda b, hg, i: (b, hg, 0, 0)),
            pl.BlockSpec((None, None, t, MLA_ROPE), lambda b, hg, i: (b, 0, 0, 0)),
            pl.BlockSpec((None, hp, n, 1), lambda b, hg, i: (b, hg, i, 0)))


def mla_fwd(qn, qp, kn, kp, v):
    bsz, h, t, _ = qn.shape
    n, hp = Q_BLOCK, MLA_HEADS_PER_STEP
    heads = range(hp)

    def body(qn_ref, qp_ref, kn_ref, kp_ref, v_ref, o_ref, lse_ref):
        qi = pl.program_id(2)
        qn_v, qp_v = [qn_ref[hh] for hh in heads], [qp_ref[hh] for hh in heads]

        def step(j, state):
            m, l, acc = state
            rows = pl.ds(pl.multiple_of(j * n, n), n)
            s = _mla_scores(qn_v, qp_v, [kn_ref[hh, rows, :] for hh in heads], kp_ref[rows, :], qi, j)
            m_new = _each(lambda m_, s_: jnp.maximum(m_, jnp.max(s_, axis=1, keepdims=True)), m, s)
            p = _each(lambda s_, m_: jnp.exp(s_ - m_), s, m_new)
            corr = _each(lambda a_, b_: jnp.exp(a_ - b_), m, m_new)
            l = _each(lambda l_, c_, p_: l_ * c_ + jnp.sum(p_, axis=1, keepdims=True), l, corr, p)
            pv = _each(lambda p_, v_: _bdot(p_, v_, 1, 0), p, [v_ref[hh, rows, :] for hh in heads])
            acc = _each(lambda a_, c_, x_: a_ * c_ + x_, acc, corr, pv)
            return m_new, l, acc

        init = ([jnp.full((n, 1), NEG_BIG, F32) for _ in heads], [jnp.zeros((n, 1), F32) for _ in heads],
                [jnp.zeros((n, MLA_NOPE), F32) for _ in heads])
        m, l, acc = lax.fori_loop(0, qi + 1, step, init)
        for hh in heads:
            o_ref[hh] = acc[hh] / l[hh]
            lse_ref[hh] = m[hh] + jnp.log(l[hh])

    qn_s, qp_s, kn_s, kp_s, row_s = _mla_specs(t, hp)
    return pl.pallas_call(
        body, name="mla_fwd", grid=(bsz, h // hp, t // n), in_specs=[qn_s, qp_s, kn_s, kp_s, kn_s],
        out_specs=[qn_s, row_s],
        out_shape=[jax.ShapeDtypeStruct(qn.shape, F32), jax.ShapeDtypeStruct((bsz, h, t, 1), F32)],
        compiler_params=_params(("parallel", "parallel", "arbitrary")),
    )(qn, qp, kn, kp, v)


def mla_bwd(qn, qp, kn, kp, v, o, lse, do):
    bsz, h, t, _ = qn.shape
    n, hp = Q_BLOCK, MLA_HEADS_PER_STEP
    heads = range(hp)
    scale = (MLA_NOPE + MLA_ROPE) ** -0.5

    def body(qn_ref, qp_ref, kn_ref, kp_ref, v_ref, o_ref, lse_ref, do_ref,
             dqn_ref, dqp_ref, dkn_ref, dkp_ref, dv_ref):
        hg, qi = pl.program_id(1), pl.program_id(2)

        @pl.when(qi == 0)
        def _():
            dkn_ref[...] = jnp.zeros_like(dkn_ref)
            dv_ref[...] = jnp.zeros_like(dv_ref)

        @pl.when(jnp.logical_and(qi == 0, hg == 0))
        def _():
            dkp_ref[...] = jnp.zeros_like(dkp_ref)

        qn_v, qp_v = [qn_ref[hh] for hh in heads], [qp_ref[hh] for hh in heads]
        do_v, lse_v = [do_ref[hh] for hh in heads], [lse_ref[hh] for hh in heads]
        dsum = [jnp.sum(do_v[hh] * o_ref[hh], axis=1, keepdims=True) for hh in heads]

        def step(j, state):
            dqn, dqp = state
            rows = pl.ds(pl.multiple_of(j * n, n), n)
            knj, vj, kpj = [kn_ref[hh, rows, :] for hh in heads], [v_ref[hh, rows, :] for hh in heads], kp_ref[rows, :]
            s = _mla_scores(qn_v, qp_v, knj, kpj, qi, j)
            p = _each(lambda s_, l_: jnp.exp(s_ - l_), s, lse_v)
            dp = _each(lambda d_, v_: _bdot(d_, v_, 1, 1), do_v, vj)
            ds = _each(lambda p_, dp_, d_: p_ * (dp_ - d_) * scale, p, dp, dsum)
            dqn = _each(lambda a_, ds_, k_: a_ + _bdot(ds_, k_, 1, 0), dqn, ds, knj)
            dqp = _each(lambda a_, ds_: a_ + _bdot(ds_, kpj, 1, 0), dqp, ds)
            dkn = _each(lambda ds_, q_: _bdot(ds_, q_, 0, 0), ds, qn_v)
            dv = _each(lambda p_, d_: _bdot(p_, d_, 0, 0), p, do_v)
            dkp = _each(lambda ds_, q_: _bdot(ds_, q_, 0, 0), ds, qp_v)
            for hh in heads:
                dkn_ref[hh, rows, :] += dkn[hh]
                dv_ref[hh, rows, :] += dv[hh]
            dkp_ref[rows, :] += functools.reduce(lambda a_, b_: a_ + b_, dkp)
            return dqn, dqp

        init = ([jnp.zeros((n, MLA_NOPE), F32) for _ in heads], [jnp.zeros((n, MLA_ROPE), F32) for _ in heads])
        dqn, dqp = lax.fori_loop(0, qi + 1, step, init)
        for hh in heads:
            dqn_ref[hh] = dqn[hh]
            dqp_ref[hh] = dqp[hh]

    qn_s, qp_s, kn_s, kp_s, row_s = _mla_specs(t, hp)
    return pl.pallas_call(
        body, name="mla_bwd", grid=(bsz, h // hp, t // n),
        in_specs=[qn_s, qp_s, kn_s, kp_s, kn_s, qn_s, row_s, qn_s],
        out_specs=[qn_s, qp_s, kn_s, kp_s, kn_s],
        out_shape=[jax.ShapeDtypeStruct(qn.shape, F32), jax.ShapeDtypeStruct(qp.shape, F32),
                   jax.ShapeDtypeStruct(kn.shape, F32), jax.ShapeDtypeStruct(kp.shape, F32),
                   jax.ShapeDtypeStruct(v.shape, F32)],
        compiler_params=_params(("parallel", "arbitrary", "arbitrary")),
    )(qn, qp, kn, kp, v, o, lse, do)


def rope(name, x, pos, inv_freq, sign):
    bsz, hx, t, d = x.shape
    half = d // 2

    tt = _largest_tile(t, 512, SUBLANES)

    def body(x_ref, pos_ref, f_ref, o_ref):
        ang = pos_ref[...].astype(F32) * f_ref[...]
        cos, sin = jnp.cos(ang), sign * jnp.sin(ang)
        ri, ci = _iota((d, d), 0), _iota((d, d), 1)
        rot = jnp.where(ri == ci + half, -1.0, 0.0) + jnp.where(ri + half == ci, 1.0, 0.0)
        for hh in range(hx):
            xv = x_ref[hh]
            o_ref[hh] = xv * cos + mmf(xv, rot) * sin

    blk = pl.BlockSpec((None, hx, tt, d), lambda b, i: (b, 0, i, 0))
    return pl.pallas_call(
        body, name=name, grid=(bsz, t // tt),
        in_specs=[blk, pl.BlockSpec((None, tt, 1), lambda b, i: (b, i, 0)), pl.BlockSpec((1, d), lambda b, i: (0, 0))],
        out_specs=blk, out_shape=jax.ShapeDtypeStruct(x.shape, F32),
        compiler_params=_params(("parallel", "parallel")),
    )(x, pos, inv_freq)


def loss_head(h, target, tm):
    n, d = h.shape

    def body(h_ref, t_ref, dh_ref, l_ref):
        @pl.when(pl.program_id(0) == 0)
        def _():
            l_ref[...] = jnp.zeros_like(l_ref)

        e = h_ref[...] - t_ref[...]
        dh_ref[...] = e * (1.0 / d)
        l_ref[...] += jnp.sum(e * e, axis=(0, 1), keepdims=True) * (0.5 / d)

    row = pl.BlockSpec((tm, d), lambda i: (i, 0))
    dh, l = pl.pallas_call(
        body, name="loss_head", grid=(n // tm,), in_specs=[row, row],
        out_specs=[row, pl.BlockSpec((SUBLANES, LANES), lambda i: (0, 0))],
        out_shape=[jax.ShapeDtypeStruct((n, d), F32), jax.ShapeDtypeStruct((SUBLANES, LANES), F32)],
        compiler_params=_params(("arbitrary",)),
    )(h, target)
    return dh, l[0, 0]


def _exchange_copies(src_ref, out_ref, send_sems, recv_sems, local_sem, gather):
    x, y, c = lax.axis_index("x"), lax.axis_index("y"), lax.axis_index("c")
    me = 4 * x + 2 * y + c
    copies = [pltpu.make_async_copy(src_ref if gather else src_ref.at[me], out_ref.at[me], local_sem)]
    for m in range(1, N_DEV):
        px, py, pc = x ^ (m >> 2), y ^ ((m >> 1) & 1), c ^ (m & 1)
        peer = 4 * px + 2 * py + pc
        copies.append(pltpu.make_async_remote_copy(
            src_ref=src_ref if gather else src_ref.at[peer], dst_ref=out_ref.at[me],
            send_sem=send_sems.at[m], recv_sem=recv_sems.at[m],
            device_id=(px, py, pc), device_id_type=pl.DeviceIdType.MESH))
    return copies


def _exchange_start(copies):
    for cp in copies:
        cp.start()


def _exchange_wait(copies):
    for cp in copies[1:]:
        cp.wait_recv()
    for cp in copies[1:]:
        cp.wait_send()
    copies[0].wait()


EXCHANGE_SCRATCH = [pltpu.SemaphoreType.DMA((N_DEV,)), pltpu.SemaphoreType.DMA((N_DEV,)),
                    pltpu.SemaphoreType.DMA(())]


def _exchange_out(src):
    return jax.ShapeDtypeStruct((N_DEV, src.shape[-2], LANES), src.dtype)


def peer_exchange(name, src, gather):
    def body(src_ref, out_ref, send_sems, recv_sems, local_sem):
        copies = _exchange_copies(src_ref, out_ref, send_sems, recv_sems, local_sem, gather)
        _exchange_start(copies)
        _exchange_wait(copies)

    return pl.pallas_call(
        body, name=name,
        in_specs=[pl.BlockSpec(memory_space=pl.ANY)], out_specs=pl.BlockSpec(memory_space=pl.ANY),
        out_shape=_exchange_out(src), scratch_shapes=list(EXCHANGE_SCRATCH),
    )(src)


def _grid_first_last(grid):
    ids = [pl.program_id(a) for a in range(len(grid))]
    first = functools.reduce(jnp.logical_and, [i == 0 for i in ids])
    last = functools.reduce(jnp.logical_and, [i == g - 1 for i, g in zip(ids, grid)])
    return first, last


def _ride_start(refs, gather, first):
    copies = _exchange_copies(*refs, gather)

    @pl.when(first)
    def _():
        _exchange_start(copies)

    return copies


def _ride_wait(copies, last):
    @pl.when(last)
    def _():
        _exchange_wait(copies)


def _ride_args(ride):
    if ride is None:
        return [], [], [], [], []
    hbm = pl.BlockSpec(memory_space=pl.ANY)
    return [ride[0]], [hbm], [_exchange_out(ride[0])], [hbm], list(EXCHANGE_SCRATCH)


def adamw_sum(name, parts, w, m, v):
    r = w.shape[0]
    tr = ADAM_ROWS
    assert r % tr == 0

    def body(p_ref, w_ref, m_ref, v_ref, g_ref, d_ref, nm_ref, nv_ref):
        g = p_ref[0]
        for j in range(1, N_DEV):
            g = g + p_ref[j]
        mm_ = ADAM_B1 * m_ref[...] + (1.0 - ADAM_B1) * g
        vv = ADAM_B2 * v_ref[...] + (1.0 - ADAM_B2) * (g * g)
        m_hat = mm_ / (1.0 - ADAM_B1 ** ADAM_STEP)
        v_hat = vv / (1.0 - ADAM_B2 ** ADAM_STEP)
        g_ref[...] = g
        d_ref[...] = -ADAM_LR * (m_hat / (jnp.sqrt(v_hat) + ADAM_EPS) + ADAM_WD * w_ref[...])
        nm_ref[...] = mm_
        nv_ref[...] = vv

    row = pl.BlockSpec((tr, LANES), lambda i: (i, 0))
    shp = jax.ShapeDtypeStruct((r, LANES), F32)
    return pl.pallas_call(
        body, name=name, grid=(r // tr,),
        in_specs=[pl.BlockSpec((N_DEV, tr, LANES), lambda i: (0, i, 0)), row, row, row],
        out_specs=[row] * 4, out_shape=[shp] * 4,
        compiler_params=_params(("parallel",)),
    )(parts, w, m, v)


WEIGHTS = ['l0_w_in', 'rwkv_mix', 'rwkv_w0', 'rwkv_w2', 'rwkv_a0', 'rwkv_a2', 'rwkv_g2', 'rwkv_k_k', 'rwkv_k_a',
           'rwkv_r_k', 'rwkv_ln_g', 'rwkv_ln_b', 'ssm_conv_w', 'ssm_conv_b', 'ssm_dt_bias', 'ssm_a_log', 'ssm_d',
           'ssm_norm_g', 'l0_w_out', 'l0_ln1_g', 'l0_ln1_b', 'ffn0_w_up', 'ffn0_conv_w', 'ffn0_conv_b',
           'ffn0_w_down', 'l0_ln2_g', 'l0_ln2_b', 'l1_w_in', 'mla_q_norm_g', 'mla_w_uq', 'mla_kv_norm_g',
           'mla_w_ukv', 'l1_w_out', 'l1_ln1_g', 'l1_ln1_b', 'ffn1_w_up', 'ffn1_conv_w', 'ffn1_conv_b',
           'ffn1_w_down', 'l1_ln2_g', 'l1_ln2_b']
SHARD_AXIS = {'l0_w_in': 1, 'rwkv_w2': 1, 'rwkv_a2': 1, 'rwkv_g2': 1, 'ssm_conv_w': 1, 'l0_w_out': 0,
              'ffn0_w_up': 1, 'ffn0_conv_w': 1, 'ffn0_w_down': 0, 'l1_w_in': 1, 'mla_w_uq': 1, 'mla_w_ukv': 1,
              'l1_w_out': 0, 'ffn1_w_up': 1, 'ffn1_conv_w': 1, 'ffn1_w_down': 0}
MATMUL_W = ['l0_w_in', 'rwkv_w2', 'rwkv_a2', 'rwkv_g2', 'l0_w_out', 'ffn0_w_up', 'ffn0_w_down', 'l1_w_in',
            'mla_w_uq', 'mla_w_ukv', 'l1_w_out', 'ffn1_w_up', 'ffn1_w_down']
CONV_W = ['ssm_conv_w', 'ffn0_conv_w', 'ffn1_conv_w']
TOK_TILE = 256
ADAM_ROWS = 512


def _ceil_to(size, unit):
    return -(-size // unit) * unit


def _flat_rows(pieces, seg_rows, total_rows):
    unit = seg_rows * LANES
    out, total = [], 0
    for p in pieces:
        p = p.reshape(-1)
        pad = _ceil_to(p.size, unit) - p.size
        out.append(jnp.pad(p, (0, pad)) if pad else p)
        total += p.size + pad
    tail = _ceil_to(total, total_rows * LANES) - total
    if tail:
        out.append(jnp.zeros((tail,), out[0].dtype))
    return jnp.concatenate(out).reshape(-1, LANES)


def _unflatten(flat2d, shapes, seg_rows):
    flat = flat2d.reshape(-1)
    out, off = [], 0
    for shp in shapes:
        size = math.prod(shp)
        out.append(flat[off:off + size].reshape(shp))
        off += _ceil_to(size, seg_rows * LANES)
    return out


def _to_heads(t2, bsz, h):
    n, w = t2.shape
    return t2.reshape(bsz, n // bsz, h, w // h).transpose(0, 2, 1, 3)


def _from_heads(t4):
    b, h, t, d = t4.shape
    return t4.transpose(0, 2, 1, 3).reshape(b * t, h * d)


def _row(v):
    return v.reshape(1, -1)


def _pad_lanes(v):
    return jnp.pad(v.reshape(1, -1), ((0, 0), (0, LANES - v.size)))


def _local_step(a, w, comm=None):
    x = a['x']
    bsz, t, d = x.shape
    n = bsz * t
    tm = TOK_TILE
    pos = a['positions'].reshape(bsz, t, 1)
    inv_freq = 1.0 / (ROPE_THETA ** (jnp.arange(0, MLA_ROPE, 2, dtype=F32) / MLA_ROPE))
    inv_freq = jnp.concatenate([inv_freq, inv_freq]).reshape(1, MLA_ROPE)
    target = a['loss_target'].reshape(n, d)

    wi0 = w['l0_w_in']
    win0 = jnp.concatenate([wi0[:, 0:1536], wi0[:, 1792:3328], wi0[:, 1536:1792], wi0[:, 3328:3336],
                            jnp.zeros((d, L0_PAD - 3336), wi0.dtype)], axis=1)
    w2p = jnp.concatenate([w['rwkv_w2'], jnp.zeros_like(w['rwkv_w2'])], axis=0)
    a2p = jnp.concatenate([jnp.zeros_like(w['rwkv_a2']), w['rwkv_a2']], axis=0)
    mix = a['rwkv_mix']
    taps = jnp.stack([mix, 1.0 - mix])
    zero_b = jnp.zeros((1, mix.size), F32)
    rw_map = lambda j: j + jnp.where(j >= 12, 12, 0)
    ssm_map = lambda j: j + 16
    gate_map = lambda j: j
    up_map = lambda j: j + D_FF // LANES
    dt_col = 3328 // LANES
    dtb, alog, dsk = _pad_lanes(a['ssm_dt_bias']), _pad_lanes(a['ssm_a_log']), _pad_lanes(a['ssm_d'])
    pre_p = [_row(a['rwkv_w0']), w2p, _row(a['rwkv_a0']), a2p, w['rwkv_g2'], _row(a['rwkv_k_k']), _row(a['rwkv_k_a'])]
    post_p = [_row(a['rwkv_ln_g']), _row(a['rwkv_ln_b']), _row(a['rwkv_r_k'])]
    sp_p = [dsk, _row(a['ssm_norm_g'])]

    def ln(name, h, y, layer, which):
        ps = [_row(a[f'l{layer}_ln{which}_g']), _row(a[f'l{layer}_ln{which}_b'])]
        return tok_fwd(name, f_ln, [(h, d, 0), (y, d, 0)], ps, [d], tm)[0]

    def ffn_fwd(layer, h):
        up = mm(h, w[f'ffn{layer}_w_up'], 'nn', f'ffn{layer}_up')
        act = dwconv_fwd(f'ffn{layer}_conv', up.reshape(bsz, t, 2 * D_FF), gate_map, w[f'ffn{layer}_conv_w'],
                         _row(a[f'ffn{layer}_conv_b']), 3, True, upmap=up_map)
        act = act.reshape(n, D_FF)
        return up, act, mm(act, w[f'ffn{layer}_w_down'], 'nn', f'ffn{layer}_down')

    x2 = x.reshape(n, d)
    proj0 = mm(x2, win0, 'nn', 'l0_in')
    p0 = proj0.reshape(bsz, t, L0_PAD)
    xs_r = dwconv_fwd('rwkv_shift', p0, rw_map, taps, zero_b, 2, False).reshape(n, 1792)
    pre_x = [(xs_r, 512, 0), (xs_r, 512, 1), (xs_r, 512, 2), (xs_r, LANES, 12), (xs_r, LANES, 13)]
    r_, v_, lw, kmod, al, be, gt = tok_fwd('rwkv_pre', f_rwkv_pre, pre_x, pre_p, [RWKV_DIM] * 7, tm)
    scan_in = [_to_heads(u, bsz, RWKV_HEADS) for u in (r_, lw, kmod, v_, al, be)]
    if comm is None:
        y_h, rstates = rwkv_scan_fwd(*scan_in)
    else:
        y_h, rstates, got = rwkv_scan_fwd(*scan_in, ride=comm.weights_ride(MID_GATHER))
        w = {**w, **comm.weights(MID_GATHER, got)}
    wi1 = w['l1_w_in']
    win1 = jnp.concatenate([wi1, jnp.zeros((d, L1_PAD - 1952), wi1.dtype)], axis=1)
    mla_p = [_row(a['mla_q_norm_g']), w['mla_w_uq'], _row(a['mla_kv_norm_g']), w['mla_w_ukv']]
    y_r = _from_heads(y_h)
    post_x = [(y_r, 512, 0), (r_, 512, 0), (kmod, 512, 0), (v_, 512, 0), (gt, 512, 0)]
    y_a = tok_fwd('rwkv_post', f_rwkv_post, post_x, post_p, [RWKV_DIM], tm)[0]
    xbc = dwconv_fwd('ssm_conv', p0, ssm_map, w['ssm_conv_w'], _row(a['ssm_conv_b']), 4, True)
    ys, sstates = ssd_fwd(xbc, p0, dt_col, dtb, alog)
    xbc2 = xbc.reshape(n, 2 * SSM_DIM)
    sp_x = [(ys.reshape(n, SSM_DIM), 512, 0), (xbc2, 512, 0), (proj0, 512, 3)]
    y_b = tok_fwd('ssd_post', f_ssd_post, sp_x, sp_p, [SSM_DIM], tm)[0]
    wo0 = w['l0_w_out']
    mixed0 = mm(y_b, wo0[512:], 'nn', 'l0_out_b', add=mm(y_a, wo0[:512], 'nn', 'l0_out_a'))
    h1 = ln('l0_ln1', x2, mixed0, 0, 1)
    up0, act0, f0 = ffn_fwd(0, h1)
    h2 = ln('l0_ln2', h1, f0, 0, 2)

    proj1 = mm(h2, win1, 'nn', 'l1_in')
    q_sb, k_sb, v_sb = (_to_heads(proj1[:, i * 512:(i + 1) * 512], bsz, 8) for i in range(3))
    if comm is None:
        (o_c,) = sb_fwd(q_sb, k_sb, v_sb)
    else:
        o_c, got = sb_fwd(q_sb, k_sb, v_sb, ride=comm.weights_ride(FFN1_GATHER))
        w = {**w, **comm.weights(FFN1_GATHER, got)}
    y_c = _from_heads(o_c)
    mla_x = [(proj1, 256, 6), (proj1, LANES, 14)]
    q_all, kv_all = tok_fwd('mla_pre', f_mla_pre, mla_x, mla_p, [768, 1024], tm)
    q4 = _to_heads(q_all, bsz, 8)
    kv4 = _to_heads(kv_all, bsz, 8)
    qn, qp_raw = q4[..., :MLA_NOPE], q4[..., MLA_NOPE:]
    kn, vv = kv4[..., :MLA_NOPE], kv4[..., MLA_NOPE:]
    kp_raw = proj1[:, 1920:1920 + MLA_ROPE].reshape(bsz, 1, t, MLA_ROPE)
    qp = rope('rope_q', qp_raw, pos, inv_freq, 1.0)
    kp = rope('rope_k', kp_raw, pos, inv_freq, 1.0)
    o_d, lse_d = mla_fwd(qn, qp, kn, kp, vv)
    y_d = _from_heads(o_d)
    wo1 = w['l1_w_out']
    mixed1 = mm(y_d, wo1[512:], 'nn', 'l1_out_b', add=mm(y_c, wo1[:512], 'nn', 'l1_out_a'))
    h3 = ln('l1_ln1', h2, mixed1, 1, 1)
    up1, act1, f1 = ffn_fwd(1, h3)
    h4 = ln('l1_ln2', h3, f1, 1, 2)
    dh4, loss = loss_head(h4, target, tm)

    g = {}

    def ln_bwd(name, h, y, layer, which, dout):
        ps = [_row(a[f'l{layer}_ln{which}_g']), _row(a[f'l{layer}_ln{which}_b'])]
        (dh, dy), (dg, db) = tok_bwd(name, f_ln, [(h, d, 0), (y, d, 0)], ps, [[dout]], tm)
        g[f'l{layer}_ln{which}_g'], g[f'l{layer}_ln{which}_b'] = dg.reshape(-1), db.reshape(-1)
        return dh, dy

    def ffn_bwd(layer, h, up, act, df, dh_res):
        wup, wdown = w[f'ffn{layer}_w_up'], w[f'ffn{layer}_w_down']
        g[f'ffn{layer}_w_down'] = mm(act, df, 'tn', f'ffn{layer}_dwdown')
        dact = mm(df, wdown, 'nt', f'ffn{layer}_dact').reshape(bsz, t, D_FF)
        dgate, dcw, dcb, dup = dwconv_bwd(f'ffn{layer}_conv_bwd', up.reshape(bsz, t, 2 * D_FF), gate_map,
                                          w[f'ffn{layer}_conv_w'], _row(a[f'ffn{layer}_conv_b']), 3, True, dact,
                                          upmap=up_map)
        dgate, dup = dgate.reshape(n, D_FF), dup.reshape(n, D_FF)
        g[f'ffn{layer}_conv_w'], g[f'ffn{layer}_conv_b'] = dcw, dcb.reshape(-1)
        g[f'ffn{layer}_w_up'] = (mm(h, dgate, 'tn', f'ffn{layer}_dwgate'), mm(h, dup, 'tn', f'ffn{layer}_dwup'))
        dh = mm(dgate, wup[:, :D_FF], 'nt', f'ffn{layer}_dh_gate', add=dh_res)
        return mm(dup, wup[:, D_FF:], 'nt', f'ffn{layer}_dh_up', add=dh)

    dh3_res, df1 = ln_bwd('l1_ln2_bwd', h3, f1, 1, 2, dh4)
    dh3 = ffn_bwd(1, h3, up1, act1, df1, dh3_res)
    dh2_res, dmixed1 = ln_bwd('l1_ln1_bwd', h2, mixed1, 1, 1, dh3)
    g['l1_w_out'] = (mm(y_c, dmixed1, 'tn', 'l1_dwout_a'), mm(y_d, dmixed1, 'tn', 'l1_dwout_b'))
    dy_c = mm(dmixed1, wo1[:512], 'nt', 'l1_dy_c')
    dy_d = mm(dmixed1, wo1[512:], 'nt', 'l1_dy_d')
    parts = {}
    if comm is None:
        dq_sb, dk_sb, dv_sb = sb_bwd(q_sb, k_sb, v_sb, _to_heads(dy_c, bsz, 8))
    else:
        dq_sb, dk_sb, dv_sb, parts['a'] = sb_bwd(q_sb, k_sb, v_sb, _to_heads(dy_c, bsz, 8),
                                                 ride=comm.grad_ride('a', g))
    dqn, dqp, dkn, dkp, dvv = mla_bwd(qn, qp, kn, kp, vv, o_d, lse_d, _to_heads(dy_d, bsz, 8))
    dqp_raw = rope('rope_q_bwd', dqp, pos, inv_freq, -1.0)
    dkp_raw = rope('rope_k_bwd', dkp, pos, inv_freq, -1.0).reshape(n, MLA_ROPE)
    dq_all = _from_heads(jnp.concatenate([dqn, dqp_raw], axis=-1))
    dkv_all = _from_heads(jnp.concatenate([dkn, dvv], axis=-1))
    (dcq, dckv), (dqg, dwuq, dkvg, dwukv) = tok_bwd('mla_pre_bwd', f_mla_pre, mla_x, mla_p,
                                                    [[dq_all], [dkv_all]], tm)
    g['mla_q_norm_g'], g['mla_w_uq'] = dqg.reshape(-1), dwuq
    g['mla_kv_norm_g'], g['mla_w_ukv'] = dkvg.reshape(-1), dwukv
    dproj1 = jnp.concatenate([_from_heads(dq_sb), _from_heads(dk_sb), _from_heads(dv_sb), dcq, dckv,
                              jnp.pad(dkp_raw, ((0, 0), (0, LANES - MLA_ROPE)))], axis=1)
    g['l1_w_in'] = mm(h2, dproj1, 'tn', 'l1_dwin')[:, :1952]
    dh2 = mm(dproj1, win1, 'nt', 'l1_dh', add=dh2_res)

    dh1_res, df0 = ln_bwd('l0_ln2_bwd', h1, f0, 0, 2, dh2)
    dh1 = ffn_bwd(0, h1, up0, act0, df0, dh1_res)
    dx_res, dmixed0 = ln_bwd('l0_ln1_bwd', x2, mixed0, 0, 1, dh1)
    g['l0_w_out'] = (mm(y_a, dmixed0, 'tn', 'l0_dwout_a'), mm(y_b, dmixed0, 'tn', 'l0_dwout_b'))
    dy_a = mm(dmixed0, wo0[:512], 'nt', 'l0_dy_a')
    dy_b = mm(dmixed0, wo0[512:], 'nt', 'l0_dy_b')
    (dy_r, dr1, dkm1, dv1, dgt), (dlng, dlnb, drk) = tok_bwd('rwkv_post_bwd', f_rwkv_post, post_x, post_p,
                                                            [[dy_a]], tm)
    g['rwkv_ln_g'], g['rwkv_ln_b'] = dlng.reshape(-1), dlnb.reshape(-1)
    g['rwkv_r_k'] = drk.reshape(RWKV_HEADS, HEAD_DIM)
    (dys, dxs_skip, dz), (ddsk, dng) = tok_bwd('ssd_post_bwd', f_ssd_post, sp_x, sp_p, [[dy_b]], tm)
    g['ssm_d'], g['ssm_norm_g'] = ddsk[0, :SSM_HEADS], dng.reshape(-1)
    dxbc_act, ddtr, ddtb, dalog = ssd_bwd(xbc, p0, dt_col, dtb, alog, sstates, dys.reshape(bsz, t, SSM_DIM),
                                          dxs_skip.reshape(bsz, t, SSM_DIM))
    g['ssm_dt_bias'], g['ssm_a_log'] = ddtb[0, :SSM_HEADS], dalog[0, :SSM_HEADS]
    dxbc, dscw, dscb = dwconv_bwd('ssm_conv_bwd', p0, ssm_map, w['ssm_conv_w'], _row(a['ssm_conv_b']), 4, True,
                                  dxbc_act)
    g['ssm_conv_w'], g['ssm_conv_b'] = dscw, dscb.reshape(-1)
    if comm is None:
        dscan = rwkv_scan_bwd(*scan_in, rstates, _to_heads(dy_r, bsz, RWKV_HEADS))
    else:
        *dscan, parts['b'] = rwkv_scan_bwd(*scan_in, rstates, _to_heads(dy_r, bsz, RWKV_HEADS),
                                           ride=comm.grad_ride('b', g))
    dr2, dlw, dk2, dv2, dal, dbe = (_from_heads(u) for u in dscan)
    pre_ct = [[dr1, dr2], [dv1, dv2], [dlw], [dkm1, dk2], [dal], [dbe], [dgt]]
    dpre_x, dpre_p = tok_bwd('rwkv_pre_bwd', f_rwkv_pre, pre_x, pre_p, pre_ct, tm)
    g['rwkv_w0'], g['rwkv_a0'] = dpre_p[0].reshape(-1), dpre_p[2].reshape(-1)
    g['rwkv_w2'], g['rwkv_a2'], g['rwkv_g2'] = dpre_p[1][:64], dpre_p[3][64:], dpre_p[4]
    g['rwkv_k_k'], g['rwkv_k_a'] = dpre_p[5].reshape(-1), dpre_p[6].reshape(-1)
    dxs_r = jnp.concatenate(dpre_x, axis=1).reshape(bsz, t, 1792)
    d_rw, dtaps, _ = dwconv_bwd('rwkv_shift_bwd', p0, rw_map, taps, zero_b, 2, False, dxs_r)
    d_rw = d_rw.reshape(n, 1792)
    g['rwkv_mix'] = dtaps[0] - dtaps[1]
    dproj0 =jnp.concatenate([d_rw[:, :1536], dz, dxbc.reshape(n, 2 * SSM_DIM), d_rw[:, 1536:],
                              ddtr.reshape(n, LANES)], axis=1)
    dwin0 = mm(x2, dproj0, 'tn', 'l0_dwin')
    g['l0_w_in'] = jnp.concatenate([dwin0[:, 0:1536], dwin0[:, 3072:3328], dwin0[:, 1536:3072],
                                    dwin0[:, 3328:3336]], axis=1)
    dx = mm(dproj0, win0, 'nt', 'l0_dx', add=dx_res)
    if comm is not None:
        parts['c'] = peer_exchange('grad_exchange_c', comm.grad_ride('c', g)[0], False)
    return loss, dx.reshape(bsz, t, d), g, parts


GRAD_GROUPS = {
    'a': ['ffn1_w_up', 'ffn1_conv_w', 'ffn1_conv_b', 'ffn1_w_down', 'l1_ln2_g', 'l1_ln2_b'],
    'c': ['l0_w_in', 'rwkv_mix', 'rwkv_w0', 'rwkv_w2', 'rwkv_a0', 'rwkv_a2', 'rwkv_g2', 'rwkv_k_k', 'rwkv_k_a'],
}
GRAD_GROUPS['b'] = [nm for nm in WEIGHTS if nm not in GRAD_GROUPS['a'] + GRAD_GROUPS['c']]
FIRST_GATHER = ['l0_w_in', 'rwkv_w2', 'rwkv_a2', 'rwkv_g2'] + CONV_W
MID_GATHER = ['l0_w_out', 'ffn0_w_up', 'ffn0_w_down', 'l1_w_in', 'mla_w_uq', 'mla_w_ukv', 'l1_w_out']
FFN1_GATHER = ['ffn1_w_up', 'ffn1_w_down']
BF16_ROWS = 16


class _Comm:
    def __init__(self, a):
        self.a = a

    def _pieces(self, names):
        return [lax.bitcast_convert_type(self.a[nm], BF16) if nm in CONV_W else self.a[nm].astype(BF16)
                for nm in names]

    def _unpack(self, names, got):
        shapes = [p.shape for p in self._pieces(names)]
        blocks = [_unflatten(got[k], shapes, BF16_ROWS) for k in range(N_DEV)]
        out = {}
        for i, nm in enumerate(names):
            blk = [blocks[k][i] for k in range(N_DEV)]
            if nm in CONV_W:
                blk = [lax.bitcast_convert_type(b, F32) for b in blk]
            out[nm] = jnp.concatenate(blk, axis=SHARD_AXIS[nm])
        return out

    def first_weights(self):
        return self.weights(FIRST_GATHER, peer_exchange('gather_first_weights', self.weights_ride(FIRST_GATHER)[0], True))

    def weights_ride(self, names):
        return _flat_rows(self._pieces(names), BF16_ROWS, BF16_ROWS), True

    def weights(self, names, got):
        return self._unpack(names, got)

    def grad_ride(self, group, g):
        def shard_of(nm, k):
            gv = g[nm]
            if nm not in SHARD_AXIS:
                return gv
            per = N_DEV
            if isinstance(gv, tuple):
                gv, k, per = gv[k // 4], k % 4, 4
            width = gv.shape[SHARD_AXIS[nm]] // per
            return lax.slice_in_dim(gv, k * width, (k + 1) * width, axis=SHARD_AXIS[nm])

        names = GRAD_GROUPS[group]
        return jnp.stack([_flat_rows([shard_of(nm, k) for nm in names], SUBLANES, ADAM_ROWS)
                          for k in range(N_DEV)]), False


def _step(a):
    comm = _Comm(a)
    loss, dx, _, parts = _local_step(a, comm.first_weights(), comm)
    loss = lax.psum(loss, ('x', 'y', 'c'))
    res = {}
    for group, names in GRAD_GROUPS.items():
        flat = lambda prefix: _flat_rows([a[prefix + nm] for nm in names], SUBLANES, ADAM_ROWS)
        outs = adamw_sum(f'adamw_{group}', parts[group], flat(''), flat('m_'), flat('v_'))
        per_out = [_unflatten(o, [a[nm].shape for nm in names], SUBLANES) for o in outs]
        for i, nm in enumerate(names):
            res[nm] = [per_out[j][i] for j in range(4)]
    return (loss, dx, *[res[nm][j] for j in range(4) for nm in WEIGHTS])


def kernel(x, positions, l0_w_in, rwkv_mix, rwkv_w0, rwkv_w2, rwkv_a0, rwkv_a2, rwkv_g2, rwkv_k_k, rwkv_k_a, rwkv_r_k, rwkv_ln_g, rwkv_ln_b, ssm_conv_w, ssm_conv_b, ssm_dt_bias, ssm_a_log, ssm_d, ssm_norm_g, l0_w_out, l0_ln1_g, l0_ln1_b, ffn0_w_up, ffn0_conv_w, ffn0_conv_b, ffn0_w_down, l0_ln2_g, l0_ln2_b, l1_w_in, mla_q_norm_g, mla_w_uq, mla_kv_norm_g, mla_w_ukv, l1_w_out, l1_ln1_g, l1_ln1_b, ffn1_w_up, ffn1_conv_w, ffn1_conv_b, ffn1_w_down, l1_ln2_g, l1_ln2_b, loss_target, m_l0_w_in, m_rwkv_mix, m_rwkv_w0, m_rwkv_w2, m_rwkv_a0, m_rwkv_a2, m_rwkv_g2, m_rwkv_k_k, m_rwkv_k_a, m_rwkv_r_k, m_rwkv_ln_g, m_rwkv_ln_b, m_ssm_conv_w, m_ssm_conv_b, m_ssm_dt_bias, m_ssm_a_log, m_ssm_d, m_ssm_norm_g, m_l0_w_out, m_l0_ln1_g, m_l0_ln1_b, m_ffn0_w_up, m_ffn0_conv_w, m_ffn0_conv_b, m_ffn0_w_down, m_l0_ln2_g, m_l0_ln2_b, m_l1_w_in, m_mla_q_norm_g, m_mla_w_uq, m_mla_kv_norm_g, m_mla_w_ukv, m_l1_w_out, m_l1_ln1_g, m_l1_ln1_b, m_ffn1_w_up, m_ffn1_conv_w, m_ffn1_conv_b, m_ffn1_w_down, m_l1_ln2_g, m_l1_ln2_b, v_l0_w_in, v_rwkv_mix, v_rwkv_w0, v_rwkv_w2, v_rwkv_a0, v_rwkv_a2, v_rwkv_g2, v_rwkv_k_k, v_rwkv_k_a, v_rwkv_r_k, v_rwkv_ln_g, v_rwkv_ln_b, v_ssm_conv_w, v_ssm_conv_b, v_ssm_dt_bias, v_ssm_a_log, v_ssm_d, v_ssm_norm_g, v_l0_w_out, v_l0_ln1_g, v_l0_ln1_b, v_ffn0_w_up, v_ffn0_conv_w, v_ffn0_conv_b, v_ffn0_w_down, v_l0_ln2_g, v_l0_ln2_b, v_l1_w_in, v_mla_q_norm_g, v_mla_w_uq, v_mla_kv_norm_g, v_mla_w_ukv, v_l1_w_out, v_l1_ln1_g, v_l1_ln1_b, v_ffn1_w_up, v_ffn1_conv_w, v_ffn1_conv_b, v_ffn1_w_down, v_l1_ln2_g, v_l1_ln2_b):
    return _step(dict(locals()))
```

```python
import functools
import math

import jax
import jax.numpy as jnp
from jax import lax
from jax.experimental import pallas as pl
from jax.experimental.pallas import tpu as pltpu

F32 = jnp.float32
BF16 = jnp.bfloat16
HI = lax.Precision.HIGHEST

V7X_VMEM_BYTES = 64 * 1024 * 1024
VMEM_LIMIT = V7X_VMEM_BYTES - 8 * 1024 * 1024
LANES = 128
SUBLANES = 8
N_DEV = 8

D_MODEL = 1024
HEAD_DIM = 64
RWKV_DIM = 512
RWKV_HEADS = 8
RWKV_GN_EPS = 64e-5
RWKV_CHUNK = 64
SSM_DIM = 512
SSM_HEADS = 8
SSM_CHUNK = 128
SSM_STATE = 128
Q_BLOCK = 128
SB_HEADS_PER_STEP = 4
MLA_HEADS_PER_STEP = 4
ATTN_FWD_HEADS_PER_STEP = 8
MLA_NOPE = 64
MLA_ROPE = 32
ROPE_THETA = 10000.0
D_FF = 2816
DEPTH = 2
ALPHA = (2 * DEPTH) ** 0.25
L0_PAD = 3456
L1_PAD = 2048

ADAM_LR = 0.001
ADAM_B1 = 0.9
ADAM_B2 = 0.999
ADAM_EPS = 1e-08
ADAM_WD = 0.01
ADAM_STEP = 10

NEG_BIG = -1e30


def _params(sem=None):
    return pltpu.CompilerParams(dimension_semantics=sem, vmem_limit_bytes=VMEM_LIMIT)


P_F32, P_BF16, P_BF16X3 = 0, 1, 2


def _dg_raw(a, b, ca, cb, fast):
    dims = (((ca,), (cb,)), ((), ()))
    if fast == P_BF16:
        return lax.dot_general(a.astype(BF16), b.astype(BF16), dims, preferred_element_type=F32)
    prec = HI if fast == P_F32 else lax.Precision.HIGH
    return lax.dot_general(a, b, dims, precision=prec, preferred_element_type=F32)


@functools.partial(jax.custom_vjp, nondiff_argnums=(2, 3, 4))
def dg(a, b, ca, cb, fast):
    return _dg_raw(a, b, ca, cb, fast)


def _dg_fwd(a, b, ca, cb, fast):
    return _dg_raw(a, b, ca, cb, fast), (a, b)


def _dg_bwd(ca, cb, fast, res, ct):
    a, b = res
    fa, fb = 1 - ca, 1 - cb
    da = _dg_raw(ct, b, 1, fb, fast) if ca == 1 else _dg_raw(b, ct, fb, 1, fast)
    db = _dg_raw(a, ct, fa, 0, fast) if cb == 0 else _dg_raw(ct, a, 0, fa, fast)
    return da.astype(a.dtype), db.astype(b.dtype)


dg.defvjp(_dg_fwd, _dg_bwd)


def mmb(a, b):
    return dg(a, b, 1, 0, P_BF16)


def mmf(a, b):
    return dg(a, b, 1, 0, P_F32)


def mmf_nt(a, b):
    return dg(a, b, 1, 1, P_F32)


def mmf_tn(a, b):
    return dg(a, b, 0, 0, P_F32)


def mm3(a, b):
    return dg(a, b, 1, 0, P_BF16X3)


def mm3_nt(a, b):
    return dg(a, b, 1, 1, P_BF16X3)


def mm3_tn(a, b):
    return dg(a, b, 0, 0, P_BF16X3)


def _split3_dot(x, m01, cb):
    hi = x.astype(BF16)
    r1 = x - hi.astype(F32)
    mid = r1.astype(BF16)
    lo = (r1 - mid.astype(F32)).astype(BF16)
    rows = x.shape[0]
    out = lax.dot_general(jnp.concatenate([hi, mid, lo], axis=0), m01.astype(BF16), (((1,), (cb,)), ((), ())),
                          preferred_element_type=F32)
    return out[:rows] + out[rows:2 * rows] + out[2 * rows:]


def _lower_ones(n):
    return jnp.where(_iota((n, n), 0) >= _iota((n, n), 1), 1.0, 0.0)


@jax.custom_vjp
def suffix_sum(x):
    return _split3_dot(x, _lower_ones(x.shape[1]), 0)


def _suffix_sum_fwd(x):
    return suffix_sum(x), None


def _suffix_sum_bwd(_, ct):
    return (_split3_dot(ct, _lower_ones(ct.shape[1]), 1),)


suffix_sum.defvjp(_suffix_sum_fwd, _suffix_sum_bwd)


def _iota(shape, dim):
    return lax.broadcasted_iota(jnp.int32, shape, dim)


def _softplus(x):
    return jnp.maximum(x, 0.0) + jnp.log1p(jnp.exp(-jnp.abs(x)))


def _silu(x):
    return x * jax.nn.sigmoid(x)


def _largest_tile(n, cap, mult):
    best = None
    t = mult
    while t <= min(n, cap):
        if n % t == 0:
            best = t
        t += mult
    return n if best is None else best


MM_VMEM_BUDGET = 40 * 1024 * 1024
V7X_HBM_BYTES_PER_S = 3.2e12
GRID_STEP_S = 0.35e-6


def _mm_tiles(M, N, K, a_bytes, b_bytes, has_add):
    def divs(n):
        return [d for d in range(LANES, n + 1, LANES) if n % d == 0] or [n]

    best = None
    for tm in divs(M):
        for tn in divs(N):
            if tm * tn * 4 > 12 * 1024 * 1024:
                continue
            for tk in divs(K):
                vmem = (2 * (tm * tk * a_bytes + tk * tn * b_bytes) + 2 * tm * tn * 4 * (2 if has_add else 1)
                        + (tm * tk + tk * tn) * 2 + tm * tn * 4)
                if vmem > MM_VMEM_BUDGET:
                    continue
                ni, nj, nk = M // tm, N // tn, K // tk
                a_reads = M * K * a_bytes * (1 if nk == 1 else nj)
                b_reads = K * N * b_bytes * (1 if (nk == 1 and nj == 1) else ni)
                traffic = a_reads + b_reads + M * N * 4 * (2 if has_add else 1)
                cost = traffic / V7X_HBM_BYTES_PER_S + ni * nj * nk * GRID_STEP_S
                if min(tm, tn, tk) < 256 and min(M, N, K) >= 256:
                    cost *= 1.5
                if best is None or cost < best[0]:
                    best = (cost, tm, tn, tk)
    return best[1:]


def mm(a, b, mode, name, add=None):
    if mode == "nn":
        (M, K), N = a.shape, b.shape[1]
    elif mode == "nt":
        (M, K), N = a.shape, b.shape[0]
    else:
        (K, M), N = a.shape, b.shape[1]
    has_add = add is not None
    tm, tn, tk = _mm_tiles(M, N, K, a.dtype.itemsize, b.dtype.itemsize, has_add)
    nk = K // tk
    keep_a = nk == 1 and N // tn > 1 and a.dtype != BF16
    if mode == "nn":
        a_spec = pl.BlockSpec((tm, tk), lambda i, j, k: (i, k))
        b_spec = pl.BlockSpec((tk, tn), lambda i, j, k: (k, j))
        dims = (((1,), (0,)), ((), ()))
    elif mode == "nt":
        a_spec = pl.BlockSpec((tm, tk), lambda i, j, k: (i, k))
        b_spec = pl.BlockSpec((tn, tk), lambda i, j, k: (j, k))
        dims = (((1,), (1,)), ((), ()))
    else:
        a_spec = pl.BlockSpec((tk, tm), lambda i, j, k: (k, i))
        b_spec = pl.BlockSpec((tk, tn), lambda i, j, k: (k, j))
        dims = (((0,), (0,)), ((), ()))
    o_spec = pl.BlockSpec((tm, tn), lambda i, j, k: (i, j))

    def body(a_ref, b_ref, *rest):
        o_ref = rest[1] if has_add else rest[0]
        k = pl.program_id(2)
        if keep_a:
            a_bf = rest[-1]

            @pl.when(pl.program_id(1) == 0)
            def _():
                a_bf[...] = a_ref[...].astype(BF16)

            av = a_bf[...]
        else:
            av = a_ref[...].astype(BF16)
        part = lax.dot_general(av, b_ref[...].astype(BF16), dims, preferred_element_type=F32)

        @pl.when(k == 0)
        def _():
            o_ref[...] = part + rest[0][...] if has_add else part

        @pl.when(k > 0)
        def _():
            o_ref[...] += part

    ins = [a, b] + ([add] if has_add else [])
    specs = [a_spec, b_spec] + ([o_spec] if has_add else [])
    return pl.pallas_call(
        body, name=name, grid=(M // tm, N // tn, nk), in_specs=specs, out_specs=o_spec,
        out_shape=jax.ShapeDtypeStruct((M, N), F32),
        scratch_shapes=[pltpu.VMEM(a_spec.block_shape, BF16)] if keep_a else [],
        compiler_params=_params(("parallel", "arbitrary", "arbitrary")),
    )(*ins)


def _is_heads(x):
    return not isinstance(x, tuple)


def _tok_arr(x):
    return x if _is_heads(x) else x[0]


def _tok_width(x):
    return x.shape[1] * x.shape[3] if _is_heads(x) else x[1]


def _heads_spec(h, dh, tm, tiles_per_seq):
    return pl.BlockSpec((None, h, tm, dh), lambda i: (i // tiles_per_seq, 0, i % tiles_per_seq, 0))


def _x_spec(x, tm, tiles_per_seq):
    if _is_heads(x):
        return _heads_spec(x.shape[1], x.shape[3], tm, tiles_per_seq)
    return pl.BlockSpec((tm, x[1]), functools.partial(lambda i, cb: (i, cb), cb=x[2]))


def _out_spec_shape(layout, n, seq, tm):
    if isinstance(layout, tuple):
        h, dh = layout
        return _heads_spec(h, dh, tm, seq // tm), jax.ShapeDtypeStruct((n // seq, h, seq, dh), F32)
    return pl.BlockSpec((tm, layout), lambda i: (i, 0)), jax.ShapeDtypeStruct((n, layout), F32)


def _tok_load(ref):
    if len(ref.shape) == 3:
        return jnp.concatenate([ref[hh] for hh in range(ref.shape[0])], axis=1)
    return ref[...]


def _tok_store(ref, val):
    if len(ref.shape) == 3:
        dh = ref.shape[2]
        for hh in range(ref.shape[0]):
            ref[hh] = val[:, hh * dh:(hh + 1) * dh]
    else:
        ref[...] = val


def _p_specs(ps):
    return [pl.BlockSpec(p.shape, lambda i: (0, 0)) for p in ps]


def tok_fwd(name, f, xs, ps, out_layouts, tm, n, seq):
    nx, npar = len(xs), len(ps)
    outs = [_out_spec_shape(lay, n, seq, tm) for lay in out_layouts]

    def body(*refs):
        xv = [_tok_load(r) for r in refs[:nx]]
        pv = [r[...].astype(F32) for r in refs[nx:nx + npar]]
        for o, r in zip(f(*xv, *pv), refs[nx + npar:]):
            _tok_store(r, o)

    return pl.pallas_call(
        body, name=name, grid=(n // tm,),
        in_specs=[_x_spec(x, tm, seq // tm) for x in xs] + _p_specs(ps),
        out_specs=[o[0] for o in outs], out_shape=[o[1] for o in outs],
        compiler_params=_params(("parallel",)),
    )(*[_tok_arr(x) for x in xs], *ps)


def tok_bwd(name, f, xs, ps, cts, tm, n, seq, dx_layouts=None):
    nx, npar = len(xs), len(ps)
    ct_flat = [c for group in cts for c in group]
    nct = len(ct_flat)
    dx_layouts = dx_layouts or [None] * nx
    dxs = [_out_spec_shape(lay if lay else _tok_width(x), n, seq, tm) for x, lay in zip(xs, dx_layouts)]

    def body(*refs):
        xv = [_tok_load(r) for r in refs[:nx]]
        pv = [r[...].astype(F32) for r in refs[nx:nx + npar]]
        ct_refs = refs[nx + npar:nx + npar + nct]
        dx_refs = refs[nx + npar + nct:nx + npar + nct + nx]
        dp_refs = refs[nx + npar + nct + nx:]
        cv, pos = [], 0
        for group in cts:
            acc = _tok_load(ct_refs[pos])
            for r in ct_refs[pos + 1:pos + len(group)]:
                acc = acc + _tok_load(r)
            cv.append(acc)
            pos += len(group)
        _, vjp = jax.vjp(f, *xv, *pv)
        grads = vjp(tuple(cv))
        for g, r in zip(grads[:nx], dx_refs):
            _tok_store(r, g)

        @pl.when(pl.program_id(0) == 0)
        def _():
            for r in dp_refs:
                r[...] = jnp.zeros_like(r)

        for g, r in zip(grads[nx:], dp_refs):
            r[...] += g

    ct_specs = [_heads_spec(c.shape[1], c.shape[3], tm, seq // tm) if c.ndim == 4
                else pl.BlockSpec((tm, c.shape[1]), lambda i: (i, 0)) for c in ct_flat]
    outs = pl.pallas_call(
        body, name=name, grid=(n // tm,),
        in_specs=[_x_spec(x, tm, seq // tm) for x in xs] + _p_specs(ps) + ct_specs,
        out_specs=[d[0] for d in dxs] + _p_specs(ps),
        out_shape=[d[1] for d in dxs] + [jax.ShapeDtypeStruct(p.shape, F32) for p in ps],
        compiler_params=_params(("arbitrary",)),
    )(*[_tok_arr(x) for x in xs], *ps, *ct_flat)
    return outs[:nx], outs[nx:]


def f_ln(h, y, g, b):
    pre = ALPHA * h + y
    mu = jnp.mean(pre, axis=-1, keepdims=True)
    xc = pre - mu
    var = jnp.mean(xc * xc, axis=-1, keepdims=True)
    return (xc * lax.rsqrt(var + 1e-5) * g + b,)


def _head_sel(width, nheads_pad, per):
    return jnp.where(_iota((width, nheads_pad), 0) // per == _iota((width, nheads_pad), 1), 1.0, 0.0).astype(F32)


def _head_sel_t(nheads_pad, width, per):
    return jnp.where(_iota((nheads_pad, width), 1) // per == _iota((nheads_pad, width), 0), 1.0, 0.0).astype(F32)


@jax.custom_vjp
def head_sum(x):
    return _split3_dot(x, _head_sel(RWKV_DIM, LANES, HEAD_DIM), 0)


@jax.custom_vjp
def head_spread(y):
    return _split3_dot(y, _head_sel(RWKV_DIM, LANES, HEAD_DIM), 1)


head_sum.defvjp(lambda x: (head_sum(x), None), lambda _, ct: (head_spread(ct),))
head_spread.defvjp(lambda y: (head_spread(y), None), lambda _, ct: (head_sum(ct),))


def f_rwkv_pre(r, k, v, lora, glo, w0, w2p, a0, a2p, g2, k_k, k_a):
    lane = _iota(lora.shape, 1)
    tw = jnp.where(lane < 64, jnp.tanh(lora), 0.0)
    ta = jnp.where(lane >= 64, lora, 0.0)
    log_w = -_softplus(-(w0 + mmb(tw, w2p))) - 0.5
    lw = -jnp.exp(log_w)
    a = jax.nn.sigmoid(a0 + mmb(ta, a2p))
    g = mmb(jax.nn.sigmoid(glo), g2)
    kk = k * k_k
    nrm = jnp.sqrt(jnp.maximum(head_sum(kk * kk), 1e-24))
    kkn = kk * head_spread(1.0 / nrm)
    kmod = k * (1.0 + (a - 1.0) * k_a)
    return r, v, lw, kmod, -kkn, kkn * a, g


def f_rwkv_post(y, r, kmod, v, g, ln_g, ln_b, r_k):
    inv = 1.0 / HEAD_DIM
    mu = head_spread(head_sum(y) * inv)
    yc = y - mu
    var = head_sum(yc * yc) * inv
    rstd = head_spread(lax.rsqrt(var + RWKV_GN_EPS))
    yn = yc * rstd * ln_g + ln_b
    bonus = head_spread(head_sum(r * kmod * r_k)) * v
    return ((yn + bonus) * g,)


def f_ssd_post(y, xs, z, d_skip, norm_g):
    sel_t = _head_sel_t(LANES, SSM_DIM, HEAD_DIM)
    d_e = jnp.sum(mmf(jnp.broadcast_to(d_skip, (SUBLANES, LANES)), sel_t), axis=0, keepdims=True) * (1.0 / SUBLANES)
    u = (y + xs * d_e) * _silu(z)
    first = _iota(u.shape, 1) < (SSM_DIM // 2)
    uu = u * u
    inv = 2.0 / SSM_DIM
    ms0 = jnp.sum(jnp.where(first, uu, 0.0), axis=-1, keepdims=True) * inv
    ms1 = jnp.sum(jnp.where(first, 0.0, uu), axis=-1, keepdims=True) * inv
    ms = jnp.where(first, ms0, ms1)
    return (u * lax.rsqrt(ms + 1e-5) * norm_g,)


def f_mla_pre(cq, ckv, qg, wq_nope, wq_rope, kvg, wk_nope, wv):
    def rms(x, g):
        return x * lax.rsqrt(jnp.mean(x * x, axis=-1, keepdims=True) + 1e-6) * g
    q_in, kv_in = rms(cq, qg), rms(ckv, kvg)
    return mmb(q_in, wq_nope), mmb(q_in, wq_rope), mmb(kv_in, wk_nope), mmb(kv_in, wv)


def f_same(*xs):
    return xs


def f_concat(*xs):
    return (jnp.concatenate(xs, axis=1),)


def _shift_down(x, s, row):
    return x if s == 0 else jnp.where(row >= s, pltpu.roll(x, s, 0), 0.0)


def _shift_up(x, s, row, t):
    return x if s == 0 else jnp.where(row < t - s, pltpu.roll(x, t - s, 0), 0.0)


def dwconv_fwd(name, u, colmap, w, b, taps, silu, upmap=None):
    bsz, t, _ = u.shape
    c = w.shape[1]
    tc = LANES
    has_up = upmap is not None

    def body(*refs):
        u_ref, w_ref, b_ref = refs[:3]
        o_ref = refs[-1]
        uv = u_ref[...]
        wv = w_ref[...]
        row = _iota(uv.shape, 0)
        acc = jnp.broadcast_to(b_ref[...], uv.shape)
        for i in range(taps):
            acc = acc + wv[i:i + 1, :] * _shift_down(uv, taps - 1 - i, row)
        if silu:
            acc = _silu(acc)
        if has_up:
            acc = acc * refs[3][...]
        o_ref[...] = acc

    specs = [pl.BlockSpec((None, t, tc), lambda bb, j: (bb, 0, colmap(j))),
             pl.BlockSpec((taps, tc), lambda bb, j: (0, j)),
             pl.BlockSpec((1, tc), lambda bb, j: (0, j))]
    ins = [u, w, b]
    if has_up:
        specs.append(pl.BlockSpec((None, t, tc), lambda bb, j: (bb, 0, upmap(j))))
        ins.append(u)
    return pl.pallas_call(
        body, name=name, grid=(bsz, c // tc), in_specs=specs,
        out_specs=pl.BlockSpec((None, t, tc), lambda bb, j: (bb, 0, j)),
        out_shape=jax.ShapeDtypeStruct((bsz, t, c), F32),
        compiler_params=_params(("parallel", "parallel")),
    )(*ins)


def dwconv_bwd(name, u, colmap, w, b, taps, silu, dout, upmap=None):
    bsz, t, _ = u.shape
    c = w.shape[1]
    tc = LANES
    has_up = upmap is not None

    def body(*refs):
        u_ref, w_ref, b_ref, d_ref = refs[:4]
        nin = 5 if has_up else 4
        du_ref, dw_ref, db_ref = refs[nin:nin + 3]
        uv = u_ref[...]
        wv = w_ref[...]
        dv = d_ref[...]
        row = _iota(uv.shape, 0)
        shifted = [_shift_down(uv, taps - 1 - i, row) for i in range(taps)]
        cg = jnp.broadcast_to(b_ref[...], uv.shape)
        for i in range(taps):
            cg = cg + wv[i:i + 1, :] * shifted[i]
        if silu:
            sg = jax.nn.sigmoid(cg)
            act = cg * sg
            dact_dcg = sg * (1.0 + cg * (1.0 - sg))
        else:
            act = cg
            dact_dcg = None
        if has_up:
            refs[nin + 3][...] = dv * act
            dv = dv * refs[4][...]
        dcg = dv * dact_dcg if silu else dv
        du = jnp.zeros_like(uv)
        for i in range(taps):
            du = du + wv[i:i + 1, :] * _shift_up(dcg, taps - 1 - i, row, t)
        du_ref[...] = du

        @pl.when(pl.program_id(1) == 0)
        def _():
            dw_ref[...] = jnp.zeros_like(dw_ref)
            db_ref[...] = jnp.zeros_like(db_ref)

        for i in range(taps):
            dw_ref[i:i + 1, :] += jnp.sum(dcg * shifted[i], axis=0, keepdims=True)
        db_ref[...] += jnp.sum(dcg, axis=0, keepdims=True)

    specs = [pl.BlockSpec((None, t, tc), lambda j, bb: (bb, 0, colmap(j))),
             pl.BlockSpec((taps, tc), lambda j, bb: (0, j)),
             pl.BlockSpec((1, tc), lambda j, bb: (0, j)),
             pl.BlockSpec((None, t, tc), lambda j, bb: (bb, 0, j))]
    ins = [u, w, b, dout]
    if has_up:
        specs.append(pl.BlockSpec((None, t, tc), lambda j, bb: (bb, 0, upmap(j))))
        ins.append(u)
    big = pl.BlockSpec((None, t, tc), lambda j, bb: (bb, 0, j))
    out_specs = [big, pl.BlockSpec((taps, tc), lambda j, bb: (0, j)), pl.BlockSpec((1, tc), lambda j, bb: (0, j))]
    out_shape = [jax.ShapeDtypeStruct((bsz, t, c), F32), jax.ShapeDtypeStruct((taps, c), F32),
                 jax.ShapeDtypeStruct((1, c), F32)]
    if has_up:
        out_specs.append(big)
        out_shape.append(jax.ShapeDtypeStruct((bsz, t, c), F32))
    return pl.pallas_call(
        body, name=name, grid=(c // tc, bsz), in_specs=specs, out_specs=out_specs, out_shape=out_shape,
        compiler_params=_params(("parallel", "arbitrary")),
    )(*ins)


def _each(f, *lists):
    return [f(*xs) for xs in zip(*lists)]


def rwkv_chunk(s0, r, lw, k, v, al, be):
    c = r[0].shape[0]
    ii, jj = _iota((c, c), 0), _iota((c, c), 1)
    incl, strict = ii >= jj, ii > jj
    ones_incl = jnp.where(incl, 1.0, 0.0)
    eye = jnp.where(ii == jj, 1.0, 0.0)
    cum = _each(lambda x: mmf(ones_incl, x), lw)
    gam_inv = _each(lambda x: jnp.exp(-x), cum)
    at = _each(lambda a_, c_, l_: a_ * jnp.exp(c_ - l_), al, cum, lw)
    rt = _each(lambda r_, c_: r_ * jnp.exp(c_), r, cum)
    bt = _each(lambda b_, g_: b_ * g_, be, gam_inv)
    kt = _each(lambda k_, g_: k_ * g_, k, gam_inv)
    a_b = _each(lambda x, y_: jnp.where(strict, mm3_nt(x, y_), 0.0), at, bt)
    a_k = _each(lambda x, y_: jnp.where(strict, mm3_nt(x, y_), 0.0), at, kt)
    rhs0 = _each(mm3_nt, at, s0)
    rhs = _each(lambda x, a_, v_: x + mm3(a_, v_), rhs0, a_k, v)
    p = _each(lambda x: eye + x, a_b)
    m = a_b
    for _ in range(int(math.log2(c)) - 1):
        m = _each(mm3, m, m)
        p = _each(lambda p_, m_: p_ + mm3(p_, m_), p, m)
    u = _each(mm3, p, rhs)
    r_b = _each(lambda x, y_: jnp.where(incl, mm3_nt(x, y_), 0.0), rt, bt)
    r_k = _each(lambda x, y_: jnp.where(incl, mm3_nt(x, y_), 0.0), rt, kt)
    y0 = _each(mm3_nt, rt, s0)
    y1 = _each(lambda y_, b_, u_: y_ + mm3(b_, u_), y0, r_b, u)
    y = _each(lambda y_, k_, v_: y_ + mm3(k_, v_), y1, r_k, v)
    su = _each(mm3_tn, u, bt)
    sv = _each(mm3_tn, v, kt)
    s1 = _each(lambda s_, a_, b_, l_: (s_ + a_ + b_) * jnp.exp(jnp.sum(l_, axis=0, keepdims=True)), s0, su, sv, lw)
    return y, s1


def rwkv_scan_fwd(r, lw, k, v, al, be, ride=None):
    bsz, h, t, d = r.shape
    c = RWKV_CHUNK
    nc = t // c
    grid = (bsz, nc)
    r_in, r_specs, r_out, r_ospecs, r_scr = _ride_args(ride)

    def body(*refs):
        r_ref, lw_ref, k_ref, v_ref, al_ref, be_ref = refs[:6]
        y_ref, st_ref = refs[6 + len(r_in):8 + len(r_in)]
        s_scr = refs[8 + 2 * len(r_in)]
        if ride is not None:
            first, last = _grid_first_last(grid)
            copies = _ride_start((refs[6], refs[8 + len(r_in)], *refs[-3:]), ride[1], first)

        @pl.when(pl.program_id(1) == 0)
        def _():
            s_scr[...] = jnp.zeros_like(s_scr)

        heads = lambda ref: [ref[hh] for hh in range(h)]
        s0 = heads(s_scr)
        y, s1 = rwkv_chunk(s0, heads(r_ref), heads(lw_ref), heads(k_ref), heads(v_ref), heads(al_ref),
                           heads(be_ref))
        for hh in range(h):
            st_ref[hh] = s0[hh]
            y_ref[hh] = y[hh]
            s_scr[hh] = s1[hh]
        if ride is not None:
            _ride_wait(copies, last)

    seq = pl.BlockSpec((None, h, c, d), lambda b, i: (b, 0, i, 0))
    return pl.pallas_call(
        body, name="rwkv_scan_fwd", grid=grid, in_specs=[seq] * 6 + r_specs,
        out_specs=[seq, pl.BlockSpec((None, h, None, d, d), lambda b, i: (b, 0, i, 0, 0))] + r_ospecs,
        out_shape=[jax.ShapeDtypeStruct((bsz, h, t, d), F32), jax.ShapeDtypeStruct((bsz, h, nc, d, d), F32)] + r_out,
        scratch_shapes=[pltpu.VMEM((h, d, d), F32)] + r_scr,
        compiler_params=_params(("arbitrary", "arbitrary")),
    )(r, lw, k, v, al, be, *r_in)


def rwkv_scan_bwd(r, lw, k, v, al, be, states, dy, ride=None):
    bsz, h, t, d = r.shape
    c = RWKV_CHUNK
    nc = t // c
    grid = (bsz, nc)
    r_in, r_specs, r_out, r_ospecs, r_scr = _ride_args(ride)

    def body(*refs):
        r_ref, lw_ref, k_ref, v_ref, al_ref, be_ref, st_ref, dy_ref = refs[:8]
        nin = 8 + len(r_in)
        dr_ref, dlw_ref, dk_ref, dv_ref, dal_ref, dbe_ref = refs[nin:nin + 6]
        ds_scr = refs[nin + 6 + len(r_in)]
        if ride is not None:
            first, last = _grid_first_last(grid)
            copies = _ride_start((refs[8], refs[nin + 6], *refs[-3:]), ride[1], first)

        @pl.when(pl.program_id(1) == 0)
        def _():
            ds_scr[...] = jnp.zeros_like(ds_scr)

        heads = lambda ref: [ref[hh] for hh in range(h)]
        _, vjp = jax.vjp(rwkv_chunk, heads(st_ref), heads(r_ref), heads(lw_ref), heads(k_ref), heads(v_ref),
                         heads(al_ref), heads(be_ref))
        grads = vjp((heads(dy_ref), heads(ds_scr)))
        for ref, gl in zip((ds_scr, dr_ref, dlw_ref, dk_ref, dv_ref, dal_ref, dbe_ref), grads):
            for hh in range(h):
                ref[hh] = gl[hh]
        if ride is not None:
            _ride_wait(copies, last)

    seq = pl.BlockSpec((None, h, c, d), lambda b, i: (b, 0, nc - 1 - i, 0))
    st = pl.BlockSpec((None, h, None, d, d), lambda b, i: (b, 0, nc - 1 - i, 0, 0))
    return pl.pallas_call(
        body, name="rwkv_scan_bwd", grid=grid, in_specs=[seq] * 6 + [st, seq] + r_specs,
        out_specs=[seq] * 6 + r_ospecs, out_shape=[jax.ShapeDtypeStruct((bsz, h, t, d), F32)] * 6 + r_out,
        scratch_shapes=[pltpu.VMEM((h, d, d), F32)] + r_scr,
        compiler_params=_params(("arbitrary", "arbitrary")),
    )(r, lw, k, v, al, be, states, dy, *r_in)


def ssd_chunk(st, xs, bm, cm, dtr, dt_bias, a_log):
    n = SSM_CHUNK
    ii, jj = _iota((n, n), 0), _iota((n, n), 1)
    incl = ii >= jj
    lane = _iota((n, LANES), 1)
    dt = _softplus(dtr + dt_bias)
    a = dt * (-jnp.exp(a_log))
    acum = mmf(jnp.where(incl, 1.0, 0.0), a)
    last_row = jnp.where(jj == n - 1, 1.0, 0.0)
    cb = [mmf_nt(cm[g], bm[g]) for g in range(2)]
    pairs, heads = range(4), range(SSM_HEADS)
    e_m = [jnp.where(_iota((LANES, LANES), 0) == 2 * m + _iota((LANES, LANES), 1) // HEAD_DIM, 1.0, 0.0)
           for m in pairs]
    dt_m = [mmf(dt, e_m[m]) for m in pairs]
    ac_m = [mmf(acum, e_m[m]) for m in pairs]
    x = [xs[m] * dt_m[m] for m in pairs]
    last_m = [mmf(last_row, ac_m[m]) for m in pairs]
    colb = [mmf(acum, jnp.where(_iota((LANES, n), 0) == h, 1.0, 0.0)) for h in heads]
    decay = [jnp.exp(jnp.where(incl, colb[h] - colb[h].T, NEG_BIG)) for h in heads]
    yh = [mmf(cb[h // 4] * decay[h], x[h // 2]) for h in heads]
    y_off = [mmf(cm[m // 2], st[m]) for m in pairs]
    ys = [jnp.where(lane // HEAD_DIM == 0, yh[2 * m], yh[2 * m + 1]) + jnp.exp(ac_m[m]) * y_off[m] for m in pairs]
    st_in = [mmf_tn(bm[m // 2], x[m] * jnp.exp(last_m[m] - ac_m[m])) for m in pairs]
    st_new = [jnp.exp(last_m[m]) * st[m] + st_in[m] for m in pairs]
    return tuple(ys), tuple(st_new)


def _ssd_load(xbc_ref, dtr_ref):
    xs = tuple(xbc_ref[:, m * LANES:(m + 1) * LANES] for m in range(4))
    bm = tuple(xbc_ref[:, SSM_DIM + g * LANES:SSM_DIM + (g + 1) * LANES] for g in range(2))
    cm = tuple(xbc_ref[:, SSM_DIM + 2 * LANES + g * LANES:SSM_DIM + 2 * LANES + (g + 1) * LANES] for g in range(2))
    return xs, bm, cm, dtr_ref[...]


def ssd_fwd(xbc, proj, dt_col, dt_bias, a_log):
    bsz, t, _ = xbc.shape
    n = SSM_CHUNK
    nc = t // n

    def body(xbc_ref, dtr_ref, dtb_ref, al_ref, y_ref, st_ref, s_scr):
        @pl.when(pl.program_id(1) == 0)
        def _():
            s_scr[...] = jnp.zeros_like(s_scr)

        st = tuple(s_scr[m] for m in range(4))
        for m in range(4):
            st_ref[m] = st[m]
        xs, bm, cm, dtr = _ssd_load(xbc_ref, dtr_ref)
        ys, st_new = ssd_chunk(st, xs, bm, cm, dtr, dtb_ref[...], al_ref[...])
        for m in range(4):
            y_ref[:, m * LANES:(m + 1) * LANES] = ys[m]
            s_scr[m] = st_new[m]

    vec = pl.BlockSpec((1, LANES), lambda b, i: (0, 0))
    return pl.pallas_call(
        body, name="ssd_fwd", grid=(bsz, nc),
        in_specs=[pl.BlockSpec((None, n, 2 * SSM_DIM), lambda b, i: (b, i, 0)),
                  pl.BlockSpec((None, n, LANES), lambda b, i: (b, i, dt_col)), vec, vec],
        out_specs=[pl.BlockSpec((None, n, SSM_DIM), lambda b, i: (b, i, 0)),
                   pl.BlockSpec((None, None, 4, SSM_STATE, LANES), lambda b, i: (b, i, 0, 0, 0))],
        out_shape=[jax.ShapeDtypeStruct((bsz, t, SSM_DIM), F32),
                   jax.ShapeDtypeStruct((bsz, nc, 4, SSM_STATE, LANES), F32)],
        scratch_shapes=[pltpu.VMEM((4, SSM_STATE, LANES), F32)],
        compiler_params=_params(("parallel", "arbitrary")),
    )(xbc, proj, dt_bias, a_log)


def ssd_bwd(xbc, proj, dt_col, dt_bias, a_log, states, dy, dxs_extra):
    bsz, t, _ = xbc.shape
    n = SSM_CHUNK
    nc = t // n

    def body(xbc_ref, dtr_ref, dtb_ref, al_ref, st_ref, dy_ref, ex_ref,
             dxbc_ref, ddtr_ref, ddtb_ref, dal_ref, ds_scr):
        first = jnp.logical_and(pl.program_id(0) == 0, pl.program_id(1) == 0)

        @pl.when(pl.program_id(1) == 0)
        def _():
            ds_scr[...] = jnp.zeros_like(ds_scr)

        @pl.when(first)
        def _():
            ddtb_ref[...] = jnp.zeros_like(ddtb_ref)
            dal_ref[...] = jnp.zeros_like(dal_ref)

        st = tuple(st_ref[m] for m in range(4))
        xs, bm, cm, dtr = _ssd_load(xbc_ref, dtr_ref)
        _, vjp = jax.vjp(ssd_chunk, st, xs, bm, cm, dtr, dtb_ref[...], al_ref[...])
        dys = tuple(dy_ref[:, m * LANES:(m + 1) * LANES] for m in range(4))
        dst_in = tuple(ds_scr[m] for m in range(4))
        dst, dxs, dbm, dcm, ddtr, ddtb, dal = vjp((dys, dst_in))
        for m in range(4):
            ds_scr[m] = dst[m]
            sl = slice(m * LANES, (m + 1) * LANES)
            dxbc_ref[:, sl] = dxs[m] + ex_ref[:, sl]
        for g in range(2):
            dxbc_ref[:, SSM_DIM + g * LANES:SSM_DIM + (g + 1) * LANES] = dbm[g]
            dxbc_ref[:, SSM_DIM + 2 * LANES + g * LANES:SSM_DIM + 2 * LANES + (g + 1) * LANES] = dcm[g]
        ddtr_ref[...] = ddtr
        ddtb_ref[...] += ddtb
        dal_ref[...] += dal

    vec = pl.BlockSpec((1, LANES), lambda b, i: (0, 0))
    rev = lambda b, i: (b, nc - 1 - i, 0)
    return pl.pallas_call(
        body, name="ssd_bwd", grid=(bsz, nc),
        in_specs=[pl.BlockSpec((None, n, 2 * SSM_DIM), rev),
                  pl.BlockSpec((None, n, LANES), lambda b, i: (b, nc - 1 - i, dt_col)), vec, vec,
                  pl.BlockSpec((None, None, 4, SSM_STATE, LANES), lambda b, i: (b, nc - 1 - i, 0, 0, 0)),
                  pl.BlockSpec((None, n, SSM_DIM), rev), pl.BlockSpec((None, n, SSM_DIM), rev)],
        out_specs=[pl.BlockSpec((None, n, 2 * SSM_DIM), rev), pl.BlockSpec((None, n, LANES), rev), vec, vec],
        out_shape=[jax.ShapeDtypeStruct((bsz, t, 2 * SSM_DIM), F32), jax.ShapeDtypeStruct((bsz, t, LANES), F32),
                   jax.ShapeDtypeStruct((1, LANES), F32), jax.ShapeDtypeStruct((1, LANES), F32)],
        scratch_shapes=[pltpu.VMEM((4, SSM_STATE, LANES), F32)],
        compiler_params=_params(("arbitrary", "arbitrary")),
    )(xbc, proj, dt_bias, a_log, states, dy, dxs_extra)


def sb_block(q, kj, vj, carry, maskf):
    mask = maskf > 0.5
    z = _each(lambda q_, k_: dg(q_, k_, 1, 1, P_BF16) * (HEAD_DIM ** -0.5), q, kj)
    ls = _each(lambda z_: -_softplus(-z_), z)
    lk = _each(lambda l_, z_: jnp.where(mask, l_ - z_, 0.0), ls, z)
    sfx = _each(suffix_sum, lk)
    att = _each(lambda l_, c_, s_, k_: jnp.where(mask, jnp.exp(l_ + c_ + s_ - k_), 0.0), ls, carry, sfx, lk)
    out = _each(mmb, att, vj)
    return out, _each(lambda c_, k_: c_ + jnp.sum(k_, axis=1, keepdims=True), carry, lk)


def _sb_mask(qi, j):
    n = Q_BLOCK
    return jnp.where(j * n + _iota((n, n), 1) < qi * n + _iota((n, n), 0), 1.0, 0.0)


def sb_fwd(q, k, v, ride=None):
    bsz, h, t, d = q.shape
    n = Q_BLOCK
    hp = ATTN_FWD_HEADS_PER_STEP
    grid = (bsz, h // hp, t // n)
    r_in, r_specs, r_out, r_ospecs, r_scr = _ride_args(ride)

    def body(*refs):
        q_ref, k_ref, v_ref = refs[:3]
        o_ref = refs[3 + len(r_in)]
        if ride is not None:
            first, last = _grid_first_last(grid)
            copies = _ride_start((refs[3], refs[4 + len(r_in)], *refs[-3:]), ride[1], first)
        qi = pl.program_id(2)

        def step(i, state):
            acc, carry = state
            j = qi - i
            rows = pl.ds(pl.multiple_of(j * n, n), n)
            o, carry = sb_block([q_ref[hh] for hh in range(hp)], [k_ref[hh, rows, :] for hh in range(hp)],
                                [v_ref[hh, rows, :] for hh in range(hp)], carry, _sb_mask(qi, j))
            return [a_ + o_ for a_, o_ in zip(acc, o)], carry

        init = ([jnp.zeros((n, d), F32) for _ in range(hp)], [jnp.zeros((n, 1), F32) for _ in range(hp)])
        acc, _ = lax.fori_loop(0, qi + 1, step, init)
        for hh in range(hp):
            o_ref[hh] = acc[hh]
        if ride is not None:
            _ride_wait(copies, last)

    blk = pl.BlockSpec((None, hp, n, d), lambda b, hg, i: (b, hg, i, 0))
    full = pl.BlockSpec((None, hp, t, d), lambda b, hg, i: (b, hg, 0, 0))
    return pl.pallas_call(
        body, name="sb_fwd", grid=grid, in_specs=[blk, full, full] + r_specs, out_specs=[blk] + r_ospecs,
        out_shape=[jax.ShapeDtypeStruct((bsz, h, t, d), F32)] + r_out, scratch_shapes=r_scr,
        compiler_params=_params(("arbitrary", "arbitrary", "arbitrary")),
    )(q, k, v, *r_in)


def sb_bwd(q, k, v, do, ride=None):
    bsz, h, t, d = q.shape
    n = Q_BLOCK
    hp = SB_HEADS_PER_STEP
    grid = (bsz, h // hp, t // n)
    r_in, r_specs, r_out, r_ospecs, r_scr = _ride_args(ride)

    def body(*refs):
        q_ref, k_ref, v_ref, do_ref = refs[:4]
        nin = 4 + len(r_in)
        dq_ref, dk_ref, dv_ref = refs[nin:nin + 3]
        carries = refs[nin + 3 + len(r_in)]
        if ride is not None:
            first, last = _grid_first_last(grid)
            copies = _ride_start((refs[4], refs[nin + 3], *refs[-3:]), ride[1], first)
        qi = pl.program_id(2)

        @pl.when(qi == 0)
        def _():
            dk_ref[...] = jnp.zeros_like(dk_ref)
            dv_ref[...] = jnp.zeros_like(dv_ref)

        heads = range(hp)
        qv = [q_ref[hh] for hh in heads]

        def fwd_step(i, carry):
            j = qi - i
            rows = pl.ds(pl.multiple_of(j * n, n), n)
            for hh in heads:
                carries[hh, j] = carry[hh]
            return sb_block(qv, [k_ref[hh, rows, :] for hh in heads], [v_ref[hh, rows, :] for hh in heads],
                            carry, _sb_mask(qi, j))[1]

        lax.fori_loop(0, qi + 1, fwd_step, [jnp.zeros((n, 1), F32) for _ in heads])

        def bwd_step(j, state):
            dq, dcarry = state
            rows = pl.ds(pl.multiple_of(j * n, n), n)
            _, vjp = jax.vjp(sb_block, qv, [k_ref[hh, rows, :] for hh in heads],
                             [v_ref[hh, rows, :] for hh in heads], [carries[hh, j] for hh in heads], _sb_mask(qi, j))
            dqj, dkj, dvj, dc, _ = vjp(([do_ref[hh] for hh in heads], dcarry))
            for hh in heads:
                dk_ref[hh, rows, :] += dkj[hh]
                dv_ref[hh, rows, :] += dvj[hh]
            return [a_ + b_ for a_, b_ in zip(dq, dqj)], dc

        init = ([jnp.zeros((n, d), F32) for _ in heads], [jnp.zeros((n, 1), F32) for _ in heads])
        dq, _ = lax.fori_loop(0, qi + 1, bwd_step, init)
        for hh in heads:
            dq_ref[hh] = dq[hh]
        if ride is not None:
            _ride_wait(copies, last)

    blk = pl.BlockSpec((None, hp, n, d), lambda b, hg, i: (b, hg, i, 0))
    full = pl.BlockSpec((None, hp, t, d), lambda b, hg, i: (b, hg, 0, 0))
    shp = jax.ShapeDtypeStruct((bsz, h, t, d), F32)
    return pl.pallas_call(
        body, name="sb_bwd", grid=grid, in_specs=[blk, full, full, blk] + r_specs,
        out_specs=[blk, full, full] + r_ospecs, out_shape=[shp, shp, shp] + r_out,
        scratch_shapes=[pltpu.VMEM((hp, t // n, n, 1), F32)] + r_scr,
        compiler_params=_params(("arbitrary", "arbitrary", "arbitrary")),
    )(q, k, v, do, *r_in)


def _bdot(a, b, ca, cb):
    return _dg_raw(a, b, ca, cb, P_BF16)


def _mla_scores(qn, qp, knj, kpj, qi, j):
    n = Q_BLOCK
    mask = j * n + _iota((n, n), 1) <= qi * n + _iota((n, n), 0)
    scale = (MLA_NOPE + MLA_ROPE) ** -0.5
    return _each(lambda a_, b_, k_: jnp.where(mask, (_bdot(a_, k_, 1, 1) + _bdot(b_, kpj, 1, 1)) * scale, NEG_BIG),
                 qn, qp, knj)


def _mla_specs(t, hp):
    n = Q_BLOCK
    return (pl.BlockSpec((None, hp, n, MLA_NOPE), lambda b, hg, i: (b, hg, i, 0)),
            pl.BlockSpec((None, hp, n, MLA_ROPE), lambda b, hg, i: (b, hg, i, 0)),
            pl.BlockSpec((None, hp, t, MLA_NOPE), lambda b, hg, i: (b, hg, 0, 0)),
            pl.BlockSpec((None, None, t, MLA_ROPE), lambda b, hg, i: (b, 0, 0, 0)),
            pl.BlockSpec((None, hp, n, 1), lambda b, hg, i: (b, hg, i, 0)))


def mla_fwd(qn, qp, kn, kp, v):
    bsz, h, t, _ = qn.shape
    n, hp = Q_BLOCK, ATTN_FWD_HEADS_PER_STEP
    heads = range(hp)

    def body(qn_ref, qp_ref, kn_ref, kp_ref, v_ref, o_ref, lse_ref):
        qi = pl.program_id(2)
        qn_v, qp_v = [qn_ref[hh] for hh in heads], [qp_ref[hh] for hh in heads]

        def step(j, state):
            m, l, acc = state
            rows = pl.ds(pl.multiple_of(j * n, n), n)
            s = _mla_scores(qn_v, qp_v, [kn_ref[hh, rows, :] for hh in heads], kp_ref[rows, :], qi, j)
            m_new = _each(lambda m_, s_: jnp.maximum(m_, jnp.max(s_, axis=1, keepdims=True)), m, s)
            p = _each(lambda s_, m_: jnp.exp(s_ - m_), s, m_new)
            corr = _each(lambda a_, b_: jnp.exp(a_ - b_), m, m_new)
            l = _each(lambda l_, c_, p_: l_ * c_ + jnp.sum(p_, axis=1, keepdims=True), l, corr, p)
            pv = _each(lambda p_, v_: _bdot(p_, v_, 1, 0), p, [v_ref[hh, rows, :] for hh in heads])
            acc = _each(lambda a_, c_, x_: a_ * c_ + x_, acc, corr, pv)
            return m_new, l, acc

        init = ([jnp.full((n, 1), NEG_BIG, F32) for _ in heads], [jnp.zeros((n, 1), F32) for _ in heads],
                [jnp.zeros((n, MLA_NOPE), F32) for _ in heads])
        m, l, acc = lax.fori_loop(0, qi + 1, step, init)
        for hh in heads:
            o_ref[hh] = acc[hh] / l[hh]
            lse_ref[hh] = m[hh] + jnp.log(l[hh])

    qn_s, qp_s, kn_s, kp_s, row_s = _mla_specs(t, hp)
    return pl.pallas_call(
        body, name="mla_fwd", grid=(bsz, h // hp, t // n), in_specs=[qn_s, qp_s, kn_s, kp_s, kn_s],
        out_specs=[qn_s, row_s],
        out_shape=[jax.ShapeDtypeStruct(qn.shape, F32), jax.ShapeDtypeStruct((bsz, h, t, 1), F32)],
        compiler_params=_params(("parallel", "parallel", "arbitrary")),
    )(qn, qp, kn, kp, v)


def mla_bwd(qn, qp, kn, kp, v, o, lse, do):
    bsz, h, t, _ = qn.shape
    n, hp = Q_BLOCK, MLA_HEADS_PER_STEP
    heads = range(hp)
    scale = (MLA_NOPE + MLA_ROPE) ** -0.5

    def body(qn_ref, qp_ref, kn_ref, kp_ref, v_ref, o_ref, lse_ref, do_ref,
             dqn_ref, dqp_ref, dkn_ref, dkp_ref, dv_ref):
        hg, qi = pl.program_id(1), pl.program_id(2)

        @pl.when(qi == 0)
        def _():
            dkn_ref[...] = jnp.zeros_like(dkn_ref)
            dv_ref[...] = jnp.zeros_like(dv_ref)

        @pl.when(jnp.logical_and(qi == 0, hg == 0))
        def _():
            dkp_ref[...] = jnp.zeros_like(dkp_ref)

        qn_v, qp_v = [qn_ref[hh] for hh in heads], [qp_ref[hh] for hh in heads]
        do_v, lse_v = [do_ref[hh] for hh in heads], [lse_ref[hh] for hh in heads]
        dsum = [jnp.sum(do_v[hh] * o_ref[hh], axis=1, keepdims=True) for hh in heads]

        def step(j, state):
            dqn, dqp = state
            rows = pl.ds(pl.multiple_of(j * n, n), n)
            knj, vj, kpj = [kn_ref[hh, rows, :] for hh in heads], [v_ref[hh, rows, :] for hh in heads], kp_ref[rows, :]
            s = _mla_scores(qn_v, qp_v, knj, kpj, qi, j)
            p = _each(lambda s_, l_: jnp.exp(s_ - l_), s, lse_v)
            dp = _each(lambda d_, v_: _bdot(d_, v_, 1, 1), do_v, vj)
            ds = _each(lambda p_, dp_, d_: p_ * (dp_ - d_) * scale, p, dp, dsum)
            dqn = _each(lambda a_, ds_, k_: a_ + _bdot(ds_, k_, 1, 0), dqn, ds, knj)
            dqp = _each(lambda a_, ds_: a_ + _bdot(ds_, kpj, 1, 0), dqp, ds)
            dkn = _each(lambda ds_, q_: _bdot(ds_, q_, 0, 0), ds, qn_v)
            dv = _each(lambda p_, d_: _bdot(p_, d_, 0, 0), p, do_v)
            dkp = _each(lambda ds_, q_: _bdot(ds_, q_, 0, 0), ds, qp_v)
            for hh in heads:
                dkn_ref[hh, rows, :] += dkn[hh]
                dv_ref[hh, rows, :] += dv[hh]
            dkp_ref[rows, :] += functools.reduce(lambda a_, b_: a_ + b_, dkp)
            return dqn, dqp

        init = ([jnp.zeros((n, MLA_NOPE), F32) for _ in heads], [jnp.zeros((n, MLA_ROPE), F32) for _ in heads])
        dqn, dqp = lax.fori_loop(0, qi + 1, step, init)
        for hh in heads:
            dqn_ref[hh] = dqn[hh]
            dqp_ref[hh] = dqp[hh]

    qn_s, qp_s, kn_s, kp_s, row_s = _mla_specs(t, hp)
    return pl.pallas_call(
        body, name="mla_bwd", grid=(bsz, h // hp, t // n),
        in_specs=[qn_s, qp_s, kn_s, kp_s, kn_s, qn_s, row_s, qn_s],
        out_specs=[qn_s, qp_s, kn_s, kp_s, kn_s],
        out_shape=[jax.ShapeDtypeStruct(qn.shape, F32), jax.ShapeDtypeStruct(qp.shape, F32),
                   jax.ShapeDtypeStruct(kn.shape, F32), jax.ShapeDtypeStruct(kp.shape, F32),
                   jax.ShapeDtypeStruct(v.shape, F32)],
        compiler_params=_params(("parallel", "arbitrary", "arbitrary")),
    )(qn, qp, kn, kp, v, o, lse, do)


def rope(name, x, pos, inv_freq, sign):
    bsz, hx, t, d = x.shape
    half = d // 2

    tt = _largest_tile(t, 512, SUBLANES)

    def body(x_ref, pos_ref, f_ref, o_ref):
        ang = pos_ref[...].astype(F32) * f_ref[...]
        cos, sin = jnp.cos(ang), sign * jnp.sin(ang)
        ri, ci = _iota((d, d), 0), _iota((d, d), 1)
        rot = jnp.where(ri == ci + half, -1.0, 0.0) + jnp.where(ri + half == ci, 1.0, 0.0)
        for hh in range(hx):
            xv = x_ref[hh]
            o_ref[hh] = xv * cos + mmf(xv, rot) * sin

    blk = pl.BlockSpec((None, hx, tt, d), lambda b, i: (b, 0, i, 0))
    return pl.pallas_call(
        body, name=name, grid=(bsz, t // tt),
        in_specs=[blk, pl.BlockSpec((None, tt, 1), lambda b, i: (b, i, 0)), pl.BlockSpec((1, d), lambda b, i: (0, 0))],
        out_specs=blk, out_shape=jax.ShapeDtypeStruct(x.shape, F32),
        compiler_params=_params(("parallel", "parallel")),
    )(x, pos, inv_freq)


def loss_head(h, target, tm):
    n, d = h.shape

    def body(h_ref, t_ref, dh_ref, l_ref):
        @pl.when(pl.program_id(0) == 0)
        def _():
            l_ref[...] = jnp.zeros_like(l_ref)

        e = h_ref[...] - t_ref[...]
        dh_ref[...] = e * (1.0 / d)
        l_ref[...] += jnp.sum(e * e, axis=(0, 1), keepdims=True) * (0.5 / d)

    row = pl.BlockSpec((tm, d), lambda i: (i, 0))
    dh, l = pl.pallas_call(
        body, name="loss_head", grid=(n // tm,), in_specs=[row, row],
        out_specs=[row, pl.BlockSpec((SUBLANES, LANES), lambda i: (0, 0))],
        out_shape=[jax.ShapeDtypeStruct((n, d), F32), jax.ShapeDtypeStruct((SUBLANES, LANES), F32)],
        compiler_params=_params(("arbitrary",)),
    )(h, target)
    return dh, l[0, 0]


def _exchange_copies(src_ref, out_ref, send_sems, recv_sems, local_sem, gather):
    x, y, c = lax.axis_index("x"), lax.axis_index("y"), lax.axis_index("c")
    me = 4 * x + 2 * y + c
    copies = [pltpu.make_async_copy(src_ref if gather else src_ref.at[me], out_ref.at[me], local_sem)]
    for m in range(1, N_DEV):
        px, py, pc = x ^ (m >> 2), y ^ ((m >> 1) & 1), c ^ (m & 1)
        peer = 4 * px + 2 * py + pc
        copies.append(pltpu.make_async_remote_copy(
            src_ref=src_ref if gather else src_ref.at[peer], dst_ref=out_ref.at[me],
            send_sem=send_sems.at[m], recv_sem=recv_sems.at[m],
            device_id=(px, py, pc), device_id_type=pl.DeviceIdType.MESH))
    return copies


def _exchange_start(copies):
    for cp in copies:
        cp.start()


def _exchange_wait(copies):
    for cp in copies[1:]:
        cp.wait_recv()
    for cp in copies[1:]:
        cp.wait_send()
    copies[0].wait()


EXCHANGE_SCRATCH = [pltpu.SemaphoreType.DMA((N_DEV,)), pltpu.SemaphoreType.DMA((N_DEV,)),
                    pltpu.SemaphoreType.DMA(())]


def _exchange_out(src):
    return jax.ShapeDtypeStruct((N_DEV, src.shape[-2], LANES), src.dtype)


def peer_exchange(name, src, gather):
    def body(src_ref, out_ref, send_sems, recv_sems, local_sem):
        copies = _exchange_copies(src_ref, out_ref, send_sems, recv_sems, local_sem, gather)
        _exchange_start(copies)
        _exchange_wait(copies)

    return pl.pallas_call(
        body, name=name,
        in_specs=[pl.BlockSpec(memory_space=pl.ANY)], out_specs=pl.BlockSpec(memory_space=pl.ANY),
        out_shape=_exchange_out(src), scratch_shapes=list(EXCHANGE_SCRATCH),
    )(src)


def _grid_first_last(grid):
    ids = [pl.program_id(a) for a in range(len(grid))]
    first = functools.reduce(jnp.logical_and, [i == 0 for i in ids])
    last = functools.reduce(jnp.logical_and, [i == g - 1 for i, g in zip(ids, grid)])
    return first, last


def _ride_start(refs, gather, first):
    copies = _exchange_copies(*refs, gather)

    @pl.when(first)
    def _():
        _exchange_start(copies)

    return copies


def _ride_wait(copies, last):
    @pl.when(last)
    def _():
        _exchange_wait(copies)


def _ride_args(ride):
    if ride is None:
        return [], [], [], [], []
    hbm = pl.BlockSpec(memory_space=pl.ANY)
    return [ride[0]], [hbm], [_exchange_out(ride[0])], [hbm], list(EXCHANGE_SCRATCH)


def adamw_sum(name, parts, w, m, v):
    r = w.shape[0]
    tr = ADAM_ROWS
    assert r % tr == 0

    def body(p_ref, w_ref, m_ref, v_ref, g_ref, d_ref, nm_ref, nv_ref):
        g = p_ref[0]
        for j in range(1, N_DEV):
            g = g + p_ref[j]
        mm_ = ADAM_B1 * m_ref[...] + (1.0 - ADAM_B1) * g
        vv = ADAM_B2 * v_ref[...] + (1.0 - ADAM_B2) * (g * g)
        m_hat = mm_ / (1.0 - ADAM_B1 ** ADAM_STEP)
        v_hat = vv / (1.0 - ADAM_B2 ** ADAM_STEP)
        g_ref[...] = g
        d_ref[...] = -ADAM_LR * (m_hat / (jnp.sqrt(v_hat) + ADAM_EPS) + ADAM_WD * w_ref[...])
        nm_ref[...] = mm_
        nv_ref[...] = vv

    row = pl.BlockSpec((tr, LANES), lambda i: (i, 0))
    shp = jax.ShapeDtypeStruct((r, LANES), F32)
    return pl.pallas_call(
        body, name=name, grid=(r // tr,),
        in_specs=[pl.BlockSpec((N_DEV, tr, LANES), lambda i: (0, i, 0)), row, row, row],
        out_specs=[row] * 4, out_shape=[shp] * 4,
        compiler_params=_params(("parallel",)),
    )(parts, w, m, v)


WEIGHTS = ['l0_w_in', 'rwkv_mix', 'rwkv_w0', 'rwkv_w2', 'rwkv_a0', 'rwkv_a2', 'rwkv_g2', 'rwkv_k_k', 'rwkv_k_a',
           'rwkv_r_k', 'rwkv_ln_g', 'rwkv_ln_b', 'ssm_conv_w', 'ssm_conv_b', 'ssm_dt_bias', 'ssm_a_log', 'ssm_d',
           'ssm_norm_g', 'l0_w_out', 'l0_ln1_g', 'l0_ln1_b', 'ffn0_w_up', 'ffn0_conv_w', 'ffn0_conv_b',
           'ffn0_w_down', 'l0_ln2_g', 'l0_ln2_b', 'l1_w_in', 'mla_q_norm_g', 'mla_w_uq', 'mla_kv_norm_g',
           'mla_w_ukv', 'l1_w_out', 'l1_ln1_g', 'l1_ln1_b', 'ffn1_w_up', 'ffn1_conv_w', 'ffn1_conv_b',
           'ffn1_w_down', 'l1_ln2_g', 'l1_ln2_b']
SHARD_AXIS = {'l0_w_in': 1, 'rwkv_w2': 1, 'rwkv_a2': 1, 'rwkv_g2': 1, 'ssm_conv_w': 1, 'l0_w_out': 0,
              'ffn0_w_up': 1, 'ffn0_conv_w': 1, 'ffn0_w_down': 0, 'l1_w_in': 1, 'mla_w_uq': 1, 'mla_w_ukv': 1,
              'l1_w_out': 0, 'ffn1_w_up': 1, 'ffn1_conv_w': 1, 'ffn1_w_down': 0}
MATMUL_W = ['l0_w_in', 'rwkv_w2', 'rwkv_a2', 'rwkv_g2', 'l0_w_out', 'ffn0_w_up', 'ffn0_w_down', 'l1_w_in',
            'mla_w_uq', 'mla_w_ukv', 'l1_w_out', 'ffn1_w_up', 'ffn1_w_down']
CONV_W = ['ssm_conv_w', 'ffn0_conv_w', 'ffn1_conv_w']
TOK_TILE = 256
ADAM_ROWS = 512


def _ceil_to(size, unit):
    return -(-size // unit) * unit


def _flat_rows(pieces, seg_rows, total_rows):
    unit = seg_rows * LANES
    out, total = [], 0
    for p in pieces:
        p = p.reshape(-1)
        pad = _ceil_to(p.size, unit) - p.size
        out.append(jnp.pad(p, (0, pad)) if pad else p)
        total += p.size + pad
    tail = _ceil_to(total, total_rows * LANES) - total
    if tail:
        out.append(jnp.zeros((tail,), out[0].dtype))
    return jnp.concatenate(out).reshape(-1, LANES)


def _unflatten(flat2d, shapes, seg_rows):
    flat = flat2d.reshape(-1)
    out, off = [], 0
    for shp in shapes:
        size = math.prod(shp)
        out.append(flat[off:off + size].reshape(shp))
        off += _ceil_to(size, seg_rows * LANES)
    return out


def _to_heads(t2, bsz, h):
    n, w = t2.shape
    return t2.reshape(bsz, n // bsz, h, w // h).transpose(0, 2, 1, 3)


def _from_heads(t4):
    b, h, t, d = t4.shape
    return t4.transpose(0, 2, 1, 3).reshape(b * t, h * d)


def _row(v):
    return v.reshape(1, -1)


def _pad_lanes(v):
    return jnp.pad(v.reshape(1, -1), ((0, 0), (0, LANES - v.size)))


def _local_step(a, w, comm=None):
    x = a['x']
    bsz, t, d = x.shape
    n = bsz * t
    tm = TOK_TILE
    pos = a['positions'].reshape(bsz, t, 1)
    inv_freq = 1.0 / (ROPE_THETA ** (jnp.arange(0, MLA_ROPE, 2, dtype=F32) / MLA_ROPE))
    inv_freq = jnp.concatenate([inv_freq, inv_freq]).reshape(1, MLA_ROPE)
    target = a['loss_target'].reshape(n, d)

    wi0 = w['l0_w_in']
    win0 = jnp.concatenate([wi0[:, 0:1536], wi0[:, 1792:3328], wi0[:, 1536:1792], wi0[:, 3328:3336],
                            jnp.zeros((d, L0_PAD - 3336), wi0.dtype)], axis=1)
    w2p = jnp.concatenate([w['rwkv_w2'], jnp.zeros_like(w['rwkv_w2'])], axis=0)
    a2p = jnp.concatenate([jnp.zeros_like(w['rwkv_a2']), w['rwkv_a2']], axis=0)
    mix = a['rwkv_mix']
    taps = jnp.stack([mix, 1.0 - mix])
    zero_b = jnp.zeros((1, mix.size), F32)
    rw_map = lambda j: j + jnp.where(j >= 12, 12, 0)
    ssm_map = lambda j: j + 16
    gate_map = lambda j: j
    up_map = lambda j: j + D_FF // LANES
    dt_col = 3328 // LANES
    dtb, alog, dsk = _pad_lanes(a['ssm_dt_bias']), _pad_lanes(a['ssm_a_log']), _pad_lanes(a['ssm_d'])
    pre_p = [_row(a['rwkv_w0']), w2p, _row(a['rwkv_a0']), a2p, w['rwkv_g2'], _row(a['rwkv_k_k']), _row(a['rwkv_k_a'])]
    post_p = [_row(a['rwkv_ln_g']), _row(a['rwkv_ln_b']), _row(a['rwkv_r_k'])]
    sp_p = [dsk, _row(a['ssm_norm_g'])]

    def ln(name, h, y, layer, which):
        ps = [_row(a[f'l{layer}_ln{which}_g']), _row(a[f'l{layer}_ln{which}_b'])]
        return tok_fwd(name, f_ln, [(h, d, 0), (y, d, 0)], ps, [d], tm, n, t)[0]

    def ffn_fwd(layer, h):
        up = mm(h, w[f'ffn{layer}_w_up'], 'nn', f'ffn{layer}_up')
        act = dwconv_fwd(f'ffn{layer}_conv', up.reshape(bsz, t, 2 * D_FF), gate_map, w[f'ffn{layer}_conv_w'],
                         _row(a[f'ffn{layer}_conv_b']), 3, True, upmap=up_map)
        act = act.reshape(n, D_FF)
        return up, act, mm(act, w[f'ffn{layer}_w_down'], 'nn', f'ffn{layer}_down')

    x2 = x.reshape(n, d)
    proj0 = mm(x2, win0, 'nn', 'l0_in')
    p0 = proj0.reshape(bsz, t, L0_PAD)
    xs_r = dwconv_fwd('rwkv_shift', p0, rw_map, taps, zero_b, 2, False).reshape(n, 1792)
    pre_x = [(xs_r, 512, 0), (xs_r, 512, 1), (xs_r, 512, 2), (xs_r, LANES, 12), (xs_r, LANES, 13)]
    heads64 = (RWKV_HEADS, HEAD_DIM)
    r_, v_, lw, kmod, al, be, gt = tok_fwd('rwkv_pre', f_rwkv_pre, pre_x, pre_p, [heads64] * 6 + [RWKV_DIM],
                                           tm, n, t)
    scan_in = [r_, lw, kmod, v_, al, be]
    if comm is None:
        y_h, rstates = rwkv_scan_fwd(*scan_in)
    else:
        y_h, rstates, got = rwkv_scan_fwd(*scan_in, ride=comm.weights_ride(MID_GATHER))
        w = {**w, **comm.weights(MID_GATHER, got)}
    wi1 = w['l1_w_in']
    win1 = jnp.concatenate([wi1, jnp.zeros((d, L1_PAD - 1952), wi1.dtype)], axis=1)
    wq3 = w['mla_w_uq'].reshape(-1, 8, MLA_NOPE + MLA_ROPE)
    wkv3 = w['mla_w_ukv'].reshape(-1, 8, 2 * MLA_NOPE)
    mla_p = [_row(a['mla_q_norm_g']), wq3[:, :, :MLA_NOPE].reshape(-1, 512), wq3[:, :, MLA_NOPE:].reshape(-1, 256),
             _row(a['mla_kv_norm_g']), wkv3[:, :, :MLA_NOPE].reshape(-1, 512), wkv3[:, :, MLA_NOPE:].reshape(-1, 512)]
    post_x = [y_h, r_, kmod, v_, (gt, 512, 0)]
    y_a = tok_fwd('rwkv_post', f_rwkv_post, post_x, post_p, [RWKV_DIM], tm, n, t)[0]
    xbc = dwconv_fwd('ssm_conv', p0, ssm_map, w['ssm_conv_w'], _row(a['ssm_conv_b']), 4, True)
    ys, sstates = ssd_fwd(xbc, p0, dt_col, dtb, alog)
    xbc2 = xbc.reshape(n, 2 * SSM_DIM)
    sp_x = [(ys.reshape(n, SSM_DIM), 512, 0), (xbc2, 512, 0), (proj0, 512, 3)]
    y_b = tok_fwd('ssd_post', f_ssd_post, sp_x, sp_p, [SSM_DIM], tm, n, t)[0]
    wo0 = w['l0_w_out']
    mixed0 = mm(y_b, wo0[512:], 'nn', 'l0_out_b', add=mm(y_a, wo0[:512], 'nn', 'l0_out_a'))
    h1 = ln('l0_ln1', x2, mixed0, 0, 1)
    up0, act0, f0 = ffn_fwd(0, h1)
    h2 = ln('l0_ln2', h1, f0, 0, 2)

    proj1 = mm(h2, win1, 'nn', 'l1_in')
    q_sb, k_sb, v_sb = tok_fwd('sb_split', f_same, [(proj1, 512, 0), (proj1, 512, 1), (proj1, 512, 2)], [],
                               [heads64] * 3, tm, n, t)
    if comm is None:
        (o_c,) = sb_fwd(q_sb, k_sb, v_sb)
    else:
        o_c, got = sb_fwd(q_sb, k_sb, v_sb, ride=comm.weights_ride(FFN1_GATHER))
        w = {**w, **comm.weights(FFN1_GATHER, got)}
    mla_x = [(proj1, 256, 6), (proj1, LANES, 14)]
    qn, qp_raw, kn, vv = tok_fwd('mla_pre', f_mla_pre, mla_x, mla_p, [heads64, (8, MLA_ROPE), heads64, heads64],
                                 tm, n, t)
    kp_raw = proj1[:, 1920:1920 + MLA_ROPE].reshape(bsz, 1, t, MLA_ROPE)
    qp = rope('rope_q', qp_raw, pos, inv_freq, 1.0)
    kp = rope('rope_k', kp_raw, pos, inv_freq, 1.0)
    o_d, lse_d = mla_fwd(qn, qp, kn, kp, vv)
    y_cd = tok_fwd('attn_merge', f_concat, [o_c, o_d], [], [2 * RWKV_DIM], tm, n, t)[0]
    wo1 = w['l1_w_out']
    mixed1 = mm(y_cd, wo1, 'nn', 'l1_out')
    h3 = ln('l1_ln1', h2, mixed1, 1, 1)
    up1, act1, f1 = ffn_fwd(1, h3)
    h4 = ln('l1_ln2', h3, f1, 1, 2)
    dh4, loss = loss_head(h4, target, tm)

    g = {}

    def ln_bwd(name, h, y, layer, which, dout):
        ps = [_row(a[f'l{layer}_ln{which}_g']), _row(a[f'l{layer}_ln{which}_b'])]
        (dh, dy), (dg, db) = tok_bwd(name, f_ln, [(h, d, 0), (y, d, 0)], ps, [[dout]], tm, n, t)
        g[f'l{layer}_ln{which}_g'], g[f'l{layer}_ln{which}_b'] = dg.reshape(-1), db.reshape(-1)
        return dh, dy

    def ffn_bwd(layer, h, up, act, df, dh_res):
        wup, wdown = w[f'ffn{layer}_w_up'], w[f'ffn{layer}_w_down']
        g[f'ffn{layer}_w_down'] = mm(act, df, 'tn', f'ffn{layer}_dwdown')
        dact = mm(df, wdown, 'nt', f'ffn{layer}_dact').reshape(bsz, t, D_FF)
        dgate, dcw, dcb, dup = dwconv_bwd(f'ffn{layer}_conv_bwd', up.reshape(bsz, t, 2 * D_FF), gate_map,
                                          w[f'ffn{layer}_conv_w'], _row(a[f'ffn{layer}_conv_b']), 3, True, dact,
                                          upmap=up_map)
        dgate, dup = dgate.reshape(n, D_FF), dup.reshape(n, D_FF)
        g[f'ffn{layer}_conv_w'], g[f'ffn{layer}_conv_b'] = dcw, dcb.reshape(-1)
        g[f'ffn{layer}_w_up'] = (mm(h, dgate, 'tn', f'ffn{layer}_dwgate'), mm(h, dup, 'tn', f'ffn{layer}_dwup'))
        dh = mm(dgate, wup[:, :D_FF], 'nt', f'ffn{layer}_dh_gate', add=dh_res)
        return mm(dup, wup[:, D_FF:], 'nt', f'ffn{layer}_dh_up', add=dh)

    dh3_res, df1 = ln_bwd('l1_ln2_bwd', h3, f1, 1, 2, dh4)
    dh3 = ffn_bwd(1, h3, up1, act1, df1, dh3_res)
    dh2_res, dmixed1 = ln_bwd('l1_ln1_bwd', h2, mixed1, 1, 1, dh3)
    g['l1_w_out'] = mm(y_cd, dmixed1, 'tn', 'l1_dwout')
    dy_cd = mm(dmixed1, wo1, 'nt', 'l1_dy')
    dy_c, dy_d = tok_fwd('attn_split', f_same, [(dy_cd, 512, 0), (dy_cd, 512, 1)], [], [heads64] * 2, tm, n, t)
    parts = {}
    if comm is None:
        dq_sb, dk_sb, dv_sb = sb_bwd(q_sb, k_sb, v_sb, dy_c)
    else:
        dq_sb, dk_sb, dv_sb, parts['a'] = sb_bwd(q_sb, k_sb, v_sb, dy_c, ride=comm.grad_ride('a', g))
    dqn, dqp, dkn, dkp, dvv = mla_bwd(qn, qp, kn, kp, vv, o_d, lse_d, dy_d)
    dqp_raw = rope('rope_q_bwd', dqp, pos, inv_freq, -1.0)
    dkp_raw = rope('rope_k_bwd', dkp, pos, inv_freq, -1.0).reshape(n, MLA_ROPE)
    (dcq, dckv), (dqg, dwq_n, dwq_p, dkvg, dwk, dwv) = tok_bwd('mla_pre_bwd', f_mla_pre, mla_x, mla_p,
                                                               [[dqn], [dqp_raw], [dkn], [dvv]], tm, n, t)
    g['mla_q_norm_g'], g['mla_kv_norm_g'] = dqg.reshape(-1), dkvg.reshape(-1)
    g['mla_w_uq'] = jnp.concatenate([dwq_n.reshape(-1, 8, MLA_NOPE), dwq_p.reshape(-1, 8, MLA_ROPE)],
                                    axis=2).reshape(-1, 8 * (MLA_NOPE + MLA_ROPE))
    g['mla_w_ukv'] = jnp.concatenate([dwk.reshape(-1, 8, MLA_NOPE), dwv.reshape(-1, 8, MLA_NOPE)],
                                     axis=2).reshape(-1, 16 * MLA_NOPE)
    dkp_pad = jnp.pad(dkp_raw, ((0, 0), (0, LANES - MLA_ROPE)))
    dproj1 = tok_fwd('l1_dproj', f_concat, [dq_sb, dk_sb, dv_sb, (dcq, 256, 0), (dckv, LANES, 0),
                                            (dkp_pad, LANES, 0)], [], [L1_PAD], tm, n, t)[0]
    g['l1_w_in'] = mm(h2, dproj1, 'tn', 'l1_dwin')[:, :1952]
    dh2 = mm(dproj1, win1, 'nt', 'l1_dh', add=dh2_res)

    dh1_res, df0 = ln_bwd('l0_ln2_bwd', h1, f0, 0, 2, dh2)
    dh1 = ffn_bwd(0, h1, up0, act0, df0, dh1_res)
    dx_res, dmixed0 = ln_bwd('l0_ln1_bwd', x2, mixed0, 0, 1, dh1)
    g['l0_w_out'] = (mm(y_a, dmixed0, 'tn', 'l0_dwout_a'), mm(y_b, dmixed0, 'tn', 'l0_dwout_b'))
    dy_a = mm(dmixed0, wo0[:512], 'nt', 'l0_dy_a')
    dy_b = mm(dmixed0, wo0[512:], 'nt', 'l0_dy_b')
    (dy_r, dr1, dkm1, dv1, dgt), (dlng, dlnb, drk) = tok_bwd('rwkv_post_bwd', f_rwkv_post, post_x, post_p, [[dy_a]],
                                                            tm, n, t, dx_layouts=[heads64] * 4 + [None])
    g['rwkv_ln_g'], g['rwkv_ln_b'] = dlng.reshape(-1), dlnb.reshape(-1)
    g['rwkv_r_k'] = drk.reshape(RWKV_HEADS, HEAD_DIM)
    (dys, dxs_skip, dz), (ddsk, dng) = tok_bwd('ssd_post_bwd', f_ssd_post, sp_x, sp_p, [[dy_b]], tm, n, t)
    g['ssm_d'], g['ssm_norm_g'] = ddsk[0, :SSM_HEADS], dng.reshape(-1)
    dxbc_act, ddtr, ddtb, dalog = ssd_bwd(xbc, p0, dt_col, dtb, alog, sstates, dys.reshape(bsz, t, SSM_DIM),
                                          dxs_skip.reshape(bsz, t, SSM_DIM))
    g['ssm_dt_bias'], g['ssm_a_log'] = ddtb[0, :SSM_HEADS], dalog[0, :SSM_HEADS]
    dxbc, dscw, dscb = dwconv_bwd('ssm_conv_bwd', p0, ssm_map, w['ssm_conv_w'], _row(a['ssm_conv_b']), 4, True,
                                  dxbc_act)
    g['ssm_conv_w'], g['ssm_conv_b'] = dscw, dscb.reshape(-1)
    if comm is None:
        dscan = rwkv_scan_bwd(*scan_in, rstates, dy_r)
    else:
        *dscan, parts['b'] = rwkv_scan_bwd(*scan_in, rstates, dy_r, ride=comm.grad_ride('b', g))
    dr2, dlw, dk2, dv2, dal, dbe = dscan
    pre_ct = [[dr1, dr2], [dv1, dv2], [dlw], [dkm1, dk2], [dal], [dbe], [dgt]]
    dpre_x, dpre_p = tok_bwd('rwkv_pre_bwd', f_rwkv_pre, pre_x, pre_p, pre_ct, tm, n, t)
    g['rwkv_w0'], g['rwkv_a0'] = dpre_p[0].reshape(-1), dpre_p[2].reshape(-1)
    g['rwkv_w2'], g['rwkv_a2'], g['rwkv_g2'] = dpre_p[1][:64], dpre_p[3][64:], dpre_p[4]
    g['rwkv_k_k'], g['rwkv_k_a'] = dpre_p[5].reshape(-1), dpre_p[6].reshape(-1)
    dxs_r = jnp.concatenate(dpre_x, axis=1).reshape(bsz, t, 1792)
    d_rw, dtaps, _ = dwconv_bwd('rwkv_shift_bwd', p0, rw_map, taps, zero_b, 2, False, dxs_r)
    d_rw = d_rw.reshape(n, 1792)
    g['rwkv_mix'] = dtaps[0] - dtaps[1]
    dproj0 = tok_fwd('l0_dproj', f_concat, [(d_rw, 1536, 0), (dz, 512, 0), (dxbc.reshape(n, 2 * SSM_DIM), 1024, 0),
                                            (d_rw, 256, 6), (ddtr.reshape(n, LANES), LANES, 0)],
                     [], [L0_PAD], tm, n, t)[0]
    dwin0 = mm(x2, dproj0, 'tn', 'l0_dwin')
    g['l0_w_in'] = jnp.concatenate([dwin0[:, 0:1536], dwin0[:, 3072:3328], dwin0[:, 1536:3072],
                                    dwin0[:, 3328:3336]], axis=1)
    dx = mm(dproj0, win0, 'nt', 'l0_dx', add=dx_res)
    if comm is not None:
        parts['c'] = peer_exchange('grad_exchange_c', comm.grad_ride('c', g)[0], False)
    return loss, dx.reshape(bsz, t, d), g, parts


GRAD_GROUPS = {
    'a': ['ffn1_w_up', 'ffn1_conv_w', 'ffn1_conv_b', 'ffn1_w_down', 'l1_ln2_g', 'l1_ln2_b'],
    'c': ['l0_w_in', 'rwkv_mix', 'rwkv_w0', 'rwkv_w2', 'rwkv_a0', 'rwkv_a2', 'rwkv_g2', 'rwkv_k_k', 'rwkv_k_a'],
}
GRAD_GROUPS['b'] = [nm for nm in WEIGHTS if nm not in GRAD_GROUPS['a'] + GRAD_GROUPS['c']]
FIRST_GATHER = ['l0_w_in', 'rwkv_w2', 'rwkv_a2', 'rwkv_g2'] + CONV_W
MID_GATHER = ['l0_w_out', 'ffn0_w_up', 'ffn0_w_down', 'l1_w_in', 'mla_w_uq', 'mla_w_ukv', 'l1_w_out']
FFN1_GATHER = ['ffn1_w_up', 'ffn1_w_down']
BF16_ROWS = 16


class _Comm:
    def __init__(self, a):
        self.a = a

    def _pieces(self, names):
        return [lax.bitcast_convert_type(self.a[nm], BF16) if nm in CONV_W else self.a[nm].astype(BF16)
                for nm in names]

    def _unpack(self, names, got):
        shapes = [p.shape for p in self._pieces(names)]
        blocks = [_unflatten(got[k], shapes, BF16_ROWS) for k in range(N_DEV)]
        out = {}
        for i, nm in enumerate(names):
            blk = [blocks[k][i] for k in range(N_DEV)]
            if nm in CONV_W:
                blk = [lax.bitcast_convert_type(b, F32) for b in blk]
            out[nm] = jnp.concatenate(blk, axis=SHARD_AXIS[nm])
        return out

    def first_weights(self):
        return self.weights(FIRST_GATHER, peer_exchange('gather_first_weights', self.weights_ride(FIRST_GATHER)[0], True))

    def weights_ride(self, names):
        return _flat_rows(self._pieces(names), BF16_ROWS, BF16_ROWS), True

    def weights(self, names, got):
        return self._unpack(names, got)

    def grad_ride(self, group, g):
        def shard_of(nm, k):
            gv = g[nm]
            if nm not in SHARD_AXIS:
                return gv
            per = N_DEV
            if isinstance(gv, tuple):
                gv, k, per = gv[k // 4], k % 4, 4
            width = gv.shape[SHARD_AXIS[nm]] // per
            return lax.slice_in_dim(gv, k * width, (k + 1) * width, axis=SHARD_AXIS[nm])

        names = GRAD_GROUPS[group]
        return jnp.stack([_flat_rows([shard_of(nm, k) for nm in names], SUBLANES, ADAM_ROWS)
                          for k in range(N_DEV)]), False


def _step(a):
    comm = _Comm(a)
    loss, dx, _, parts = _local_step(a, comm.first_weights(), comm)
    loss = lax.psum(loss, ('x', 'y', 'c'))
    res = {}
    for group, names in GRAD_GROUPS.items():
        flat = lambda prefix: _flat_rows([a[prefix + nm] for nm in names], SUBLANES, ADAM_ROWS)
        outs = adamw_sum(f'adamw_{group}', parts[group], flat(''), flat('m_'), flat('v_'))
        per_out = [_unflatten(o, [a[nm].shape for nm in names], SUBLANES) for o in outs]
        for i, nm in enumerate(names):
            res[nm] = [per_out[j][i] for j in range(4)]
    return (loss, dx, *[res[nm][j] for j in range(4) for nm in WEIGHTS])


def kernel(x, positions, l0_w_in, rwkv_mix, rwkv_w0, rwkv_w2, rwkv_a0, rwkv_a2, rwkv_g2, rwkv_k_k, rwkv_k_a, rwkv_r_k, rwkv_ln_g, rwkv_ln_b, ssm_conv_w, ssm_conv_b, ssm_dt_bias, ssm_a_log, ssm_d, ssm_norm_g, l0_w_out, l0_ln1_g, l0_ln1_b, ffn0_w_up, ffn0_conv_w, ffn0_conv_b, ffn0_w_down, l0_ln2_g, l0_ln2_b, l1_w_in, mla_q_norm_g, mla_w_uq, mla_kv_norm_g, mla_w_ukv, l1_w_out, l1_ln1_g, l1_ln1_b, ffn1_w_up, ffn1_conv_w, ffn1_conv_b, ffn1_w_down, l1_ln2_g, l1_ln2_b, loss_target, m_l0_w_in, m_rwkv_mix, m_rwkv_w0, m_rwkv_w2, m_rwkv_a0, m_rwkv_a2, m_rwkv_g2, m_rwkv_k_k, m_rwkv_k_a, m_rwkv_r_k, m_rwkv_ln_g, m_rwkv_ln_b, m_ssm_conv_w, m_ssm_conv_b, m_ssm_dt_bias, m_ssm_a_log, m_ssm_d, m_ssm_norm_g, m_l0_w_out, m_l0_ln1_g, m_l0_ln1_b, m_ffn0_w_up, m_ffn0_conv_w, m_ffn0_conv_b, m_ffn0_w_down, m_l0_ln2_g, m_l0_ln2_b, m_l1_w_in, m_mla_q_norm_g, m_mla_w_uq, m_mla_kv_norm_g, m_mla_w_ukv, m_l1_w_out, m_l1_ln1_g, m_l1_ln1_b, m_ffn1_w_up, m_ffn1_conv_w, m_ffn1_conv_b, m_ffn1_w_down, m_l1_ln2_g, m_l1_ln2_b, v_l0_w_in, v_rwkv_mix, v_rwkv_w0, v_rwkv_w2, v_rwkv_a0, v_rwkv_a2, v_rwkv_g2, v_rwkv_k_k, v_rwkv_k_a, v_rwkv_r_k, v_rwkv_ln_g, v_rwkv_ln_b, v_ssm_conv_w, v_ssm_conv_b, v_ssm_dt_bias, v_ssm_a_log, v_ssm_d, v_ssm_norm_g, v_l0_w_out, v_l0_ln1_g, v_l0_ln1_b, v_ffn0_w_up, v_ffn0_conv_w, v_ffn0_conv_b, v_ffn0_w_down, v_l0_ln2_g, v_l0_ln2_b, v_l1_w_in, v_mla_q_norm_g, v_mla_w_uq, v_mla_kv_norm_g, v_mla_w_ukv, v_l1_w_out, v_l1_ln1_g, v_l1_ln1_b, v_ffn1_w_up, v_ffn1_conv_w, v_ffn1_conv_b, v_ffn1_w_down, v_l1_ln2_g, v_l1_ln2_b):
    return _step(dict(locals()))
```

```python
import functools
import math

import jax
import jax.numpy as jnp
from jax import lax
from jax.experimental import pallas as pl
from jax.experimental.pallas import tpu as pltpu

F32 = jnp.float32
BF16 = jnp.bfloat16
HI = lax.Precision.HIGHEST

V7X_VMEM_BYTES = 64 * 1024 * 1024
VMEM_LIMIT = V7X_VMEM_BYTES - 8 * 1024 * 1024
LANES = 128
SUBLANES = 8
N_DEV = 8

D_MODEL = 1024
HEAD_DIM = 64
RWKV_DIM = 512
RWKV_HEADS = 8
RWKV_GN_EPS = 64e-5
RWKV_CHUNK = 64
SSM_DIM = 512
SSM_HEADS = 8
SSM_CHUNK = 128
SSM_STATE = 128
Q_BLOCK = 128
SB_HEADS_PER_STEP = 4
MLA_HEADS_PER_STEP = 4
ATTN_FWD_HEADS_PER_STEP = 8
MLA_NOPE = 64
MLA_ROPE = 32
ROPE_THETA = 10000.0
D_FF = 2816
DEPTH = 2
ALPHA = (2 * DEPTH) ** 0.25
L0_PAD = 3456
L1_PAD = 2048

ADAM_LR = 0.001
ADAM_B1 = 0.9
ADAM_B2 = 0.999
ADAM_EPS = 1e-08
ADAM_WD = 0.01
ADAM_STEP = 10

NEG_BIG = -1e30


def _params(sem=None):
    return pltpu.CompilerParams(dimension_semantics=sem, vmem_limit_bytes=VMEM_LIMIT)


P_F32, P_BF16, P_BF16X3 = 0, 1, 2


def _dg_raw(a, b, ca, cb, fast):
    dims = (((ca,), (cb,)), ((), ()))
    if fast == P_BF16:
        return lax.dot_general(a.astype(BF16), b.astype(BF16), dims, preferred_element_type=F32)
    prec = HI if fast == P_F32 else lax.Precision.HIGH
    return lax.dot_general(a, b, dims, precision=prec, preferred_element_type=F32)


@functools.partial(jax.custom_vjp, nondiff_argnums=(2, 3, 4))
def dg(a, b, ca, cb, fast):
    return _dg_raw(a, b, ca, cb, fast)


def _dg_fwd(a, b, ca, cb, fast):
    return _dg_raw(a, b, ca, cb, fast), (a, b)


def _dg_bwd(ca, cb, fast, res, ct):
    a, b = res
    fa, fb = 1 - ca, 1 - cb
    da = _dg_raw(ct, b, 1, fb, fast) if ca == 1 else _dg_raw(b, ct, fb, 1, fast)
    db = _dg_raw(a, ct, fa, 0, fast) if cb == 0 else _dg_raw(ct, a, 0, fa, fast)
    return da.astype(a.dtype), db.astype(b.dtype)


dg.defvjp(_dg_fwd, _dg_bwd)


def mmb(a, b):
    return dg(a, b, 1, 0, P_BF16)


def mmf(a, b):
    return dg(a, b, 1, 0, P_F32)


def mmf_nt(a, b):
    return dg(a, b, 1, 1, P_F32)


def mmf_tn(a, b):
    return dg(a, b, 0, 0, P_F32)


def mm3(a, b):
    return dg(a, b, 1, 0, P_BF16X3)


def mm3_nt(a, b):
    return dg(a, b, 1, 1, P_BF16X3)


def mm3_tn(a, b):
    return dg(a, b, 0, 0, P_BF16X3)


def _split3_dot(x, m01, cb):
    hi = x.astype(BF16)
    r1 = x - hi.astype(F32)
    mid = r1.astype(BF16)
    lo = (r1 - mid.astype(F32)).astype(BF16)
    rows = x.shape[0]
    out = lax.dot_general(jnp.concatenate([hi, mid, lo], axis=0), m01.astype(BF16), (((1,), (cb,)), ((), ())),
                          preferred_element_type=F32)
    return out[:rows] + out[rows:2 * rows] + out[2 * rows:]


def _lower_ones(n):
    return jnp.where(_iota((n, n), 0) >= _iota((n, n), 1), 1.0, 0.0)


@jax.custom_vjp
def suffix_sum(x):
    return _split3_dot(x, _lower_ones(x.shape[1]), 0)


def _suffix_sum_fwd(x):
    return suffix_sum(x), None


def _suffix_sum_bwd(_, ct):
    return (_split3_dot(ct, _lower_ones(ct.shape[1]), 1),)


suffix_sum.defvjp(_suffix_sum_fwd, _suffix_sum_bwd)


def _iota(shape, dim):
    return lax.broadcasted_iota(jnp.int32, shape, dim)


def _softplus(x):
    return jnp.maximum(x, 0.0) + jnp.log1p(jnp.exp(-jnp.abs(x)))


def _silu(x):
    return x * jax.nn.sigmoid(x)


def _largest_tile(n, cap, mult):
    best = None
    t = mult
    while t <= min(n, cap):
        if n % t == 0:
            best = t
        t += mult
    return n if best is None else best


MM_VMEM_BUDGET = 40 * 1024 * 1024
V7X_HBM_BYTES_PER_S = 3.2e12
GRID_STEP_S = 0.35e-6


def _mm_tiles(M, N, K, a_bytes, b_bytes, has_add):
    def divs(n):
        return [d for d in range(LANES, n + 1, LANES) if n % d == 0] or [n]

    best = None
    for tm in divs(M):
        for tn in divs(N):
            if tm * tn * 4 > 12 * 1024 * 1024:
                continue
            for tk in divs(K):
                vmem = (2 * (tm * tk * a_bytes + tk * tn * b_bytes) + 2 * tm * tn * 4 * (2 if has_add else 1)
                        + (tm * tk + tk * tn) * 2 + tm * tn * 4)
                if vmem > MM_VMEM_BUDGET:
                    continue
                ni, nj, nk = M // tm, N // tn, K // tk
                a_reads = M * K * a_bytes * (1 if nk == 1 else nj)
                b_reads = K * N * b_bytes * (1 if (nk == 1 and nj == 1) else ni)
                traffic = a_reads + b_reads + M * N * 4 * (2 if has_add else 1)
                cost = traffic / V7X_HBM_BYTES_PER_S + ni * nj * nk * GRID_STEP_S
                if min(tm, tn, tk) < 256 and min(M, N, K) >= 256:
                    cost *= 1.5
                if best is None or cost < best[0]:
                    best = (cost, tm, tn, tk)
    return best[1:]


def mm(a, b, mode, name, add=None, b_rows=None, b_cols=None):
    r0, nr = b_rows or (0, b.shape[0])
    c0, nc = b_cols or (0, b.shape[1])
    if mode == "nn":
        (M, K), N = a.shape, nc
        assert nr == K
    elif mode == "nt":
        (M, K), N = a.shape, nr
        assert nc == K
    else:
        (K, M), N = a.shape, b.shape[1]
        assert b_rows is None and b_cols is None
    has_add = add is not None
    tm, tn, tk = _mm_tiles(M, N, K, a.dtype.itemsize, b.dtype.itemsize, has_add)
    nk = K // tk
    keep_a = nk == 1 and N // tn > 1 and a.dtype != BF16
    if mode == "nn":
        assert r0 % tk == 0 and c0 % tn == 0
        a_spec = pl.BlockSpec((tm, tk), lambda i, j, k: (i, k))
        b_spec = pl.BlockSpec((tk, tn), lambda i, j, k: (k + r0 // tk, j + c0 // tn))
        dims = (((1,), (0,)), ((), ()))
    elif mode == "nt":
        assert r0 % tn == 0 and c0 % tk == 0
        a_spec = pl.BlockSpec((tm, tk), lambda i, j, k: (i, k))
        b_spec = pl.BlockSpec((tn, tk), lambda i, j, k: (j + r0 // tn, k + c0 // tk))
        dims = (((1,), (1,)), ((), ()))
    else:
        a_spec = pl.BlockSpec((tk, tm), lambda i, j, k: (k, i))
        b_spec = pl.BlockSpec((tk, tn), lambda i, j, k: (k, j))
        dims = (((0,), (0,)), ((), ()))
    o_spec = pl.BlockSpec((tm, tn), lambda i, j, k: (i, j))

    def body(a_ref, b_ref, *rest):
        o_ref = rest[1] if has_add else rest[0]
        k = pl.program_id(2)
        if keep_a:
            a_bf = rest[-1]

            @pl.when(pl.program_id(1) == 0)
            def _():
                a_bf[...] = a_ref[...].astype(BF16)

            av = a_bf[...]
        else:
            av = a_ref[...].astype(BF16)
        part = lax.dot_general(av, b_ref[...].astype(BF16), dims, preferred_element_type=F32)

        @pl.when(k == 0)
        def _():
            o_ref[...] = part + rest[0][...] if has_add else part

        @pl.when(k > 0)
        def _():
            o_ref[...] += part

    ins = [a, b] + ([add] if has_add else [])
    specs = [a_spec, b_spec] + ([o_spec] if has_add else [])
    return pl.pallas_call(
        body, name=name, grid=(M // tm, N // tn, nk), in_specs=specs, out_specs=o_spec,
        out_shape=jax.ShapeDtypeStruct((M, N), F32),
        scratch_shapes=[pltpu.VMEM(a_spec.block_shape, BF16)] if keep_a else [],
        compiler_params=_params(("parallel", "arbitrary", "arbitrary")),
    )(*ins)


def _is_heads(x):
    return not isinstance(x, tuple)


def _tok_arr(x):
    return x if _is_heads(x) else x[0]


def _tok_width(x):
    return x.shape[1] * x.shape[3] if _is_heads(x) else x[1]


def _heads_spec(h, dh, tm, tiles_per_seq):
    return pl.BlockSpec((None, h, tm, dh), lambda i: (i // tiles_per_seq, 0, i % tiles_per_seq, 0))


def _x_spec(x, tm, tiles_per_seq):
    if _is_heads(x):
        return _heads_spec(x.shape[1], x.shape[3], tm, tiles_per_seq)
    return pl.BlockSpec((tm, x[1]), functools.partial(lambda i, cb: (i, cb), cb=x[2]))


def _out_spec_shape(layout, n, seq, tm):
    if isinstance(layout, tuple):
        h, dh = layout
        return _heads_spec(h, dh, tm, seq // tm), jax.ShapeDtypeStruct((n // seq, h, seq, dh), F32)
    return pl.BlockSpec((tm, layout), lambda i: (i, 0)), jax.ShapeDtypeStruct((n, layout), F32)


def _tok_load(ref):
    if len(ref.shape) == 3:
        return jnp.concatenate([ref[hh] for hh in range(ref.shape[0])], axis=1)
    return ref[...]


def _tok_store(ref, val):
    if len(ref.shape) == 3:
        dh = ref.shape[2]
        for hh in range(ref.shape[0]):
            ref[hh] = val[:, hh * dh:(hh + 1) * dh]
    else:
        ref[...] = val


def _p_specs(ps):
    return [pl.BlockSpec(p.shape, lambda i: (0, 0)) for p in ps]


def tok_fwd(name, f, xs, ps, out_layouts, tm, n, seq):
    nx, npar = len(xs), len(ps)
    outs = [_out_spec_shape(lay, n, seq, tm) for lay in out_layouts]

    def body(*refs):
        xv = [_tok_load(r) for r in refs[:nx]]
        pv = [r[...].astype(F32) for r in refs[nx:nx + npar]]
        for o, r in zip(f(*xv, *pv), refs[nx + npar:]):
            _tok_store(r, o)

    return pl.pallas_call(
        body, name=name, grid=(n // tm,),
        in_specs=[_x_spec(x, tm, seq // tm) for x in xs] + _p_specs(ps),
        out_specs=[o[0] for o in outs], out_shape=[o[1] for o in outs],
        compiler_params=_params(("parallel",)),
    )(*[_tok_arr(x) for x in xs], *ps)


def tok_bwd(name, f, xs, ps, cts, tm, n, seq, dx_layouts=None):
    nx, npar = len(xs), len(ps)
    ct_flat = [c for group in cts for c in group]
    nct = len(ct_flat)
    dx_layouts = dx_layouts or [None] * nx
    dxs = [_out_spec_shape(lay if lay else _tok_width(x), n, seq, tm) for x, lay in zip(xs, dx_layouts)]

    def body(*refs):
        xv = [_tok_load(r) for r in refs[:nx]]
        pv = [r[...].astype(F32) for r in refs[nx:nx + npar]]
        ct_refs = refs[nx + npar:nx + npar + nct]
        dx_refs = refs[nx + npar + nct:nx + npar + nct + nx]
        dp_refs = refs[nx + npar + nct + nx:]
        cv, pos = [], 0
        for group in cts:
            acc = _tok_load(ct_refs[pos])
            for r in ct_refs[pos + 1:pos + len(group)]:
                acc = acc + _tok_load(r)
            cv.append(acc)
            pos += len(group)
        _, vjp = jax.vjp(f, *xv, *pv)
        grads = vjp(tuple(cv))
        for g, r in zip(grads[:nx], dx_refs):
            _tok_store(r, g)

        @pl.when(pl.program_id(0) == 0)
        def _():
            for r in dp_refs:
                r[...] = jnp.zeros_like(r)

        for g, r in zip(grads[nx:], dp_refs):
            r[...] += g

    ct_specs = [_heads_spec(c.shape[1], c.shape[3], tm, seq // tm) if c.ndim == 4
                else pl.BlockSpec((tm, c.shape[1]), lambda i: (i, 0)) for c in ct_flat]
    outs = pl.pallas_call(
        body, name=name, grid=(n // tm,),
        in_specs=[_x_spec(x, tm, seq // tm) for x in xs] + _p_specs(ps) + ct_specs,
        out_specs=[d[0] for d in dxs] + _p_specs(ps),
        out_shape=[d[1] for d in dxs] + [jax.ShapeDtypeStruct(p.shape, F32) for p in ps],
        compiler_params=_params(("arbitrary",)),
    )(*[_tok_arr(x) for x in xs], *ps, *ct_flat)
    return outs[:nx], outs[nx:]


def f_ln(h, y, g, b):
    pre = ALPHA * h + y
    mu = jnp.mean(pre, axis=-1, keepdims=True)
    xc = pre - mu
    var = jnp.mean(xc * xc, axis=-1, keepdims=True)
    return (xc * lax.rsqrt(var + 1e-5) * g + b,)


def _head_sel(width, nheads_pad, per):
    return jnp.where(_iota((width, nheads_pad), 0) // per == _iota((width, nheads_pad), 1), 1.0, 0.0).astype(F32)


def _head_sel_t(nheads_pad, width, per):
    return jnp.where(_iota((nheads_pad, width), 1) // per == _iota((nheads_pad, width), 0), 1.0, 0.0).astype(F32)


@jax.custom_vjp
def head_sum(x):
    return _split3_dot(x, _head_sel(RWKV_DIM, LANES, HEAD_DIM), 0)


@jax.custom_vjp
def head_spread(y):
    return _split3_dot(y, _head_sel(RWKV_DIM, LANES, HEAD_DIM), 1)


head_sum.defvjp(lambda x: (head_sum(x), None), lambda _, ct: (head_spread(ct),))
head_spread.defvjp(lambda y: (head_spread(y), None), lambda _, ct: (head_sum(ct),))


def f_rwkv_pre(r, k, v, lora, glo, w0, w2p, a0, a2p, g2, k_k, k_a):
    lane = _iota(lora.shape, 1)
    tw = jnp.where(lane < 64, jnp.tanh(lora), 0.0)
    ta = jnp.where(lane >= 64, lora, 0.0)
    log_w = -_softplus(-(w0 + mmb(tw, w2p))) - 0.5
    lw = -jnp.exp(log_w)
    a = jax.nn.sigmoid(a0 + mmb(ta, a2p))
    g = mmb(jax.nn.sigmoid(glo), g2)
    kk = k * k_k
    nrm = jnp.sqrt(jnp.maximum(head_sum(kk * kk), 1e-24))
    kkn = kk * head_spread(1.0 / nrm)
    kmod = k * (1.0 + (a - 1.0) * k_a)
    return r, v, lw, kmod, -kkn, kkn * a, g


def f_rwkv_post(y, r, kmod, v, g, ln_g, ln_b, r_k):
    inv = 1.0 / HEAD_DIM
    mu = head_spread(head_sum(y) * inv)
    yc = y - mu
    var = head_sum(yc * yc) * inv
    rstd = head_spread(lax.rsqrt(var + RWKV_GN_EPS))
    yn = yc * rstd * ln_g + ln_b
    bonus = head_spread(head_sum(r * kmod * r_k)) * v
    return ((yn + bonus) * g,)


def f_ssd_post(y, xs, z, d_skip, norm_g):
    sel_t = _head_sel_t(LANES, SSM_DIM, HEAD_DIM)
    d_e = jnp.sum(mmf(jnp.broadcast_to(d_skip, (SUBLANES, LANES)), sel_t), axis=0, keepdims=True) * (1.0 / SUBLANES)
    u = (y + xs * d_e) * _silu(z)
    first = _iota(u.shape, 1) < (SSM_DIM // 2)
    uu = u * u
    inv = 2.0 / SSM_DIM
    ms0 = jnp.sum(jnp.where(first, uu, 0.0), axis=-1, keepdims=True) * inv
    ms1 = jnp.sum(jnp.where(first, 0.0, uu), axis=-1, keepdims=True) * inv
    ms = jnp.where(first, ms0, ms1)
    return (u * lax.rsqrt(ms + 1e-5) * norm_g,)


def f_mla_pre(cq, ckv, qg, wq_nope, wq_rope, kvg, wk_nope, wv):
    def rms(x, g):
        return x * lax.rsqrt(jnp.mean(x * x, axis=-1, keepdims=True) + 1e-6) * g
    q_in, kv_in = rms(cq, qg), rms(ckv, kvg)
    return mmb(q_in, wq_nope), mmb(q_in, wq_rope), mmb(kv_in, wk_nope), mmb(kv_in, wv)


def f_same(*xs):
    return xs


def f_concat(*xs):
    return (jnp.concatenate(xs, axis=1),)


def _shift_down(x, s, row):
    return x if s == 0 else jnp.where(row >= s, pltpu.roll(x, s, 0), 0.0)


def _shift_up(x, s, row, t):
    return x if s == 0 else jnp.where(row < t - s, pltpu.roll(x, t - s, 0), 0.0)


def dwconv_fwd(name, u, colmap, w, b, taps, silu, upmap=None):
    bsz, t, _ = u.shape
    c = w.shape[1]
    tc = LANES
    has_up = upmap is not None

    def body(*refs):
        u_ref, w_ref, b_ref = refs[:3]
        o_ref = refs[-1]
        uv = u_ref[...]
        wv = w_ref[...]
        row = _iota(uv.shape, 0)
        acc = jnp.broadcast_to(b_ref[...], uv.shape)
        for i in range(taps):
            acc = acc + wv[i:i + 1, :] * _shift_down(uv, taps - 1 - i, row)
        if silu:
            acc = _silu(acc)
        if has_up:
            acc = acc * refs[3][...]
        o_ref[...] = acc

    specs = [pl.BlockSpec((None, t, tc), lambda bb, j: (bb, 0, colmap(j))),
             pl.BlockSpec((taps, tc), lambda bb, j: (0, j)),
             pl.BlockSpec((1, tc), lambda bb, j: (0, j))]
    ins = [u, w, b]
    if has_up:
        specs.append(pl.BlockSpec((None, t, tc), lambda bb, j: (bb, 0, upmap(j))))
        ins.append(u)
    return pl.pallas_call(
        body, name=name, grid=(bsz, c // tc), in_specs=specs,
        out_specs=pl.BlockSpec((None, t, tc), lambda bb, j: (bb, 0, j)),
        out_shape=jax.ShapeDtypeStruct((bsz, t, c), F32),
        compiler_params=_params(("parallel", "parallel")),
    )(*ins)


def dwconv_bwd(name, u, colmap, w, b, taps, silu, dout, upmap=None):
    bsz, t, _ = u.shape
    c = w.shape[1]
    tc = LANES
    has_up = upmap is not None

    def body(*refs):
        u_ref, w_ref, b_ref, d_ref = refs[:4]
        nin = 5 if has_up else 4
        du_ref, dw_ref, db_ref = refs[nin:nin + 3]
        uv = u_ref[...]
        wv = w_ref[...]
        dv = d_ref[...]
        row = _iota(uv.shape, 0)
        shifted = [_shift_down(uv, taps - 1 - i, row) for i in range(taps)]
        cg = jnp.broadcast_to(b_ref[...], uv.shape)
        for i in range(taps):
            cg = cg + wv[i:i + 1, :] * shifted[i]
        if silu:
            sg = jax.nn.sigmoid(cg)
            act = cg * sg
            dact_dcg = sg * (1.0 + cg * (1.0 - sg))
        else:
            act = cg
            dact_dcg = None
        if has_up:
            refs[nin + 3][...] = dv * act
            dv = dv * refs[4][...]
        dcg = dv * dact_dcg if silu else dv
        du = jnp.zeros_like(uv)
        for i in range(taps):
            du = du + wv[i:i + 1, :] * _shift_up(dcg, taps - 1 - i, row, t)
        du_ref[...] = du

        @pl.when(pl.program_id(1) == 0)
        def _():
            dw_ref[...] = jnp.zeros_like(dw_ref)
            db_ref[...] = jnp.zeros_like(db_ref)

        for i in range(taps):
            dw_ref[i:i + 1, :] += jnp.sum(dcg * shifted[i], axis=0, keepdims=True)
        db_ref[...] += jnp.sum(dcg, axis=0, keepdims=True)

    specs = [pl.BlockSpec((None, t, tc), lambda j, bb: (bb, 0, colmap(j))),
             pl.BlockSpec((taps, tc), lambda j, bb: (0, j)),
             pl.BlockSpec((1, tc), lambda j, bb: (0, j)),
             pl.BlockSpec((None, t, tc), lambda j, bb: (bb, 0, j))]
    ins = [u, w, b, dout]
    if has_up:
        specs.append(pl.BlockSpec((None, t, tc), lambda j, bb: (bb, 0, upmap(j))))
        ins.append(u)
    big = pl.BlockSpec((None, t, tc), lambda j, bb: (bb, 0, j))
    out_specs = [big, pl.BlockSpec((taps, tc), lambda j, bb: (0, j)), pl.BlockSpec((1, tc), lambda j, bb: (0, j))]
    out_shape = [jax.ShapeDtypeStruct((bsz, t, c), F32), jax.ShapeDtypeStruct((taps, c), F32),
                 jax.ShapeDtypeStruct((1, c), F32)]
    if has_up:
        out_specs.append(big)
        out_shape.append(jax.ShapeDtypeStruct((bsz, t, c), F32))
    return pl.pallas_call(
        body, name=name, grid=(c // tc, bsz), in_specs=specs, out_specs=out_specs, out_shape=out_shape,
        compiler_params=_params(("parallel", "arbitrary")),
    )(*ins)


def _each(f, *lists):
    return [f(*xs) for xs in zip(*lists)]


def rwkv_chunk(s0, r, lw, k, v, al, be):
    c = r[0].shape[0]
    ii, jj = _iota((c, c), 0), _iota((c, c), 1)
    incl, strict = ii >= jj, ii > jj
    ones_incl = jnp.where(incl, 1.0, 0.0)
    eye = jnp.where(ii == jj, 1.0, 0.0)
    cum = _each(lambda x: mmf(ones_incl, x), lw)
    gam_inv = _each(lambda x: jnp.exp(-x), cum)
    at = _each(lambda a_, c_, l_: a_ * jnp.exp(c_ - l_), al, cum, lw)
    rt = _each(lambda r_, c_: r_ * jnp.exp(c_), r, cum)
    bt = _each(lambda b_, g_: b_ * g_, be, gam_inv)
    kt = _each(lambda k_, g_: k_ * g_, k, gam_inv)
    a_b = _each(lambda x, y_: jnp.where(strict, mm3_nt(x, y_), 0.0), at, bt)
    a_k = _each(lambda x, y_: jnp.where(strict, mm3_nt(x, y_), 0.0), at, kt)
    rhs0 = _each(mm3_nt, at, s0)
    rhs = _each(lambda x, a_, v_: x + mm3(a_, v_), rhs0, a_k, v)
    p = _each(lambda x: eye + x, a_b)
    m = a_b
    for _ in range(int(math.log2(c)) - 1):
        m = _each(mm3, m, m)
        p = _each(lambda p_, m_: p_ + mm3(p_, m_), p, m)
    u = _each(mm3, p, rhs)
    r_b = _each(lambda x, y_: jnp.where(incl, mm3_nt(x, y_), 0.0), rt, bt)
    r_k = _each(lambda x, y_: jnp.where(incl, mm3_nt(x, y_), 0.0), rt, kt)
    y0 = _each(mm3_nt, rt, s0)
    y1 = _each(lambda y_, b_, u_: y_ + mm3(b_, u_), y0, r_b, u)
    y = _each(lambda y_, k_, v_: y_ + mm3(k_, v_), y1, r_k, v)
    su = _each(mm3_tn, u, bt)
    sv = _each(mm3_tn, v, kt)
    s1 = _each(lambda s_, a_, b_, l_: (s_ + a_ + b_) * jnp.exp(jnp.sum(l_, axis=0, keepdims=True)), s0, su, sv, lw)
    return y, s1


def rwkv_scan_fwd(r, lw, k, v, al, be, ride=None):
    bsz, h, t, d = r.shape
    c = RWKV_CHUNK
    nc = t // c
    grid = (bsz, nc)
    r_in, r_specs, r_out, r_ospecs, r_scr = _ride_args(ride)

    def body(*refs):
        r_ref, lw_ref, k_ref, v_ref, al_ref, be_ref = refs[:6]
        y_ref, st_ref = refs[6 + len(r_in):8 + len(r_in)]
        s_scr = refs[8 + 2 * len(r_in)]
        if ride is not None:
            first, last = _grid_first_last(grid)
            copies = _ride_start((refs[6], refs[8 + len(r_in)], *refs[-3:]), ride[1], first)

        @pl.when(pl.program_id(1) == 0)
        def _():
            s_scr[...] = jnp.zeros_like(s_scr)

        heads = lambda ref: [ref[hh] for hh in range(h)]
        s0 = heads(s_scr)
        y, s1 = rwkv_chunk(s0, heads(r_ref), heads(lw_ref), heads(k_ref), heads(v_ref), heads(al_ref),
                           heads(be_ref))
        for hh in range(h):
            st_ref[hh] = s0[hh]
            y_ref[hh] = y[hh]
            s_scr[hh] = s1[hh]
        if ride is not None:
            _ride_wait(copies, last)

    seq = pl.BlockSpec((None, h, c, d), lambda b, i: (b, 0, i, 0))
    return pl.pallas_call(
        body, name="rwkv_scan_fwd", grid=grid, in_specs=[seq] * 6 + r_specs,
        out_specs=[seq, pl.BlockSpec((None, h, None, d, d), lambda b, i: (b, 0, i, 0, 0))] + r_ospecs,
        out_shape=[jax.ShapeDtypeStruct((bsz, h, t, d), F32), jax.ShapeDtypeStruct((bsz, h, nc, d, d), F32)] + r_out,
        scratch_shapes=[pltpu.VMEM((h, d, d), F32)] + r_scr,
        compiler_params=_params(("arbitrary", "arbitrary")),
    )(r, lw, k, v, al, be, *r_in)


def rwkv_scan_bwd(r, lw, k, v, al, be, states, dy, ride=None):
    bsz, h, t, d = r.shape
    c = RWKV_CHUNK
    nc = t // c
    grid = (bsz, nc)
    r_in, r_specs, r_out, r_ospecs, r_scr = _ride_args(ride)

    def body(*refs):
        r_ref, lw_ref, k_ref, v_ref, al_ref, be_ref, st_ref, dy_ref = refs[:8]
        nin = 8 + len(r_in)
        dr_ref, dlw_ref, dk_ref, dv_ref, dal_ref, dbe_ref = refs[nin:nin + 6]
        ds_scr = refs[nin + 6 + len(r_in)]
        if ride is not None:
            first, last = _grid_first_last(grid)
            copies = _ride_start((refs[8], refs[nin + 6], *refs[-3:]), ride[1], first)

        @pl.when(pl.program_id(1) == 0)
        def _():
            ds_scr[...] = jnp.zeros_like(ds_scr)

        heads = lambda ref: [ref[hh] for hh in range(h)]
        _, vjp = jax.vjp(rwkv_chunk, heads(st_ref), heads(r_ref), heads(lw_ref), heads(k_ref), heads(v_ref),
                         heads(al_ref), heads(be_ref))
        grads = vjp((heads(dy_ref), heads(ds_scr)))
        for ref, gl in zip((ds_scr, dr_ref, dlw_ref, dk_ref, dv_ref, dal_ref, dbe_ref), grads):
            for hh in range(h):
                ref[hh] = gl[hh]
        if ride is not None:
            _ride_wait(copies, last)

    seq = pl.BlockSpec((None, h, c, d), lambda b, i: (b, 0, nc - 1 - i, 0))
    st = pl.BlockSpec((None, h, None, d, d), lambda b, i: (b, 0, nc - 1 - i, 0, 0))
    return pl.pallas_call(
        body, name="rwkv_scan_bwd", grid=grid, in_specs=[seq] * 6 + [st, seq] + r_specs,
        out_specs=[seq] * 6 + r_ospecs, out_shape=[jax.ShapeDtypeStruct((bsz, h, t, d), F32)] * 6 + r_out,
        scratch_shapes=[pltpu.VMEM((h, d, d), F32)] + r_scr,
        compiler_params=_params(("arbitrary", "arbitrary")),
    )(r, lw, k, v, al, be, states, dy, *r_in)


def ssd_chunk(st, xs, bm, cm, dtr, dt_bias, a_log):
    n = SSM_CHUNK
    ii, jj = _iota((n, n), 0), _iota((n, n), 1)
    incl = ii >= jj
    lane = _iota((n, LANES), 1)
    dt = _softplus(dtr + dt_bias)
    a = dt * (-jnp.exp(a_log))
    acum = mmf(jnp.where(incl, 1.0, 0.0), a)
    last_row = jnp.where(jj == n - 1, 1.0, 0.0)
    cb = [mmf_nt(cm[g], bm[g]) for g in range(2)]
    pairs, heads = range(4), range(SSM_HEADS)
    e_m = [jnp.where(_iota((LANES, LANES), 0) == 2 * m + _iota((LANES, LANES), 1) // HEAD_DIM, 1.0, 0.0)
           for m in pairs]
    dt_m = [mmf(dt, e_m[m]) for m in pairs]
    ac_m = [mmf(acum, e_m[m]) for m in pairs]
    x = [xs[m] * dt_m[m] for m in pairs]
    last_m = [mmf(last_row, ac_m[m]) for m in pairs]
    colb = [mmf(acum, jnp.where(_iota((LANES, n), 0) == h, 1.0, 0.0)) for h in heads]
    decay = [jnp.exp(jnp.where(incl, colb[h] - colb[h].T, NEG_BIG)) for h in heads]
    yh = [mmf(cb[h // 4] * decay[h], x[h // 2]) for h in heads]
    y_off = [mmf(cm[m // 2], st[m]) for m in pairs]
    ys = [jnp.where(lane // HEAD_DIM == 0, yh[2 * m], yh[2 * m + 1]) + jnp.exp(ac_m[m]) * y_off[m] for m in pairs]
    st_in = [mmf_tn(bm[m // 2], x[m] * jnp.exp(last_m[m] - ac_m[m])) for m in pairs]
    st_new = [jnp.exp(last_m[m]) * st[m] + st_in[m] for m in pairs]
    return tuple(ys), tuple(st_new)


def _ssd_load(xbc_ref, dtr_ref):
    xs = tuple(xbc_ref[:, m * LANES:(m + 1) * LANES] for m in range(4))
    bm = tuple(xbc_ref[:, SSM_DIM + g * LANES:SSM_DIM + (g + 1) * LANES] for g in range(2))
    cm = tuple(xbc_ref[:, SSM_DIM + 2 * LANES + g * LANES:SSM_DIM + 2 * LANES + (g + 1) * LANES] for g in range(2))
    return xs, bm, cm, dtr_ref[...]


def ssd_fwd(xbc, proj, dt_col, dt_bias, a_log):
    bsz, t, _ = xbc.shape
    n = SSM_CHUNK
    nc = t // n

    def body(xbc_ref, dtr_ref, dtb_ref, al_ref, y_ref, st_ref, s_scr):
        @pl.when(pl.program_id(1) == 0)
        def _():
            s_scr[...] = jnp.zeros_like(s_scr)

        st = tuple(s_scr[m] for m in range(4))
        for m in range(4):
            st_ref[m] = st[m]
        xs, bm, cm, dtr = _ssd_load(xbc_ref, dtr_ref)
        ys, st_new = ssd_chunk(st, xs, bm, cm, dtr, dtb_ref[...], al_ref[...])
        for m in range(4):
            y_ref[:, m * LANES:(m + 1) * LANES] = ys[m]
            s_scr[m] = st_new[m]

    vec = pl.BlockSpec((1, LANES), lambda b, i: (0, 0))
    return pl.pallas_call(
        body, name="ssd_fwd", grid=(bsz, nc),
        in_specs=[pl.BlockSpec((None, n, 2 * SSM_DIM), lambda b, i: (b, i, 0)),
                  pl.BlockSpec((None, n, LANES), lambda b, i: (b, i, dt_col)), vec, vec],
        out_specs=[pl.BlockSpec((None, n, SSM_DIM), lambda b, i: (b, i, 0)),
                   pl.BlockSpec((None, None, 4, SSM_STATE, LANES), lambda b, i: (b, i, 0, 0, 0))],
        out_shape=[jax.ShapeDtypeStruct((bsz, t, SSM_DIM), F32),
                   jax.ShapeDtypeStruct((bsz, nc, 4, SSM_STATE, LANES), F32)],
        scratch_shapes=[pltpu.VMEM((4, SSM_STATE, LANES), F32)],
        compiler_params=_params(("parallel", "arbitrary")),
    )(xbc, proj, dt_bias, a_log)


def ssd_bwd(xbc, proj, dt_col, dt_bias, a_log, states, dy, dxs_extra):
    bsz, t, _ = xbc.shape
    n = SSM_CHUNK
    nc = t // n

    def body(xbc_ref, dtr_ref, dtb_ref, al_ref, st_ref, dy_ref, ex_ref,
             dxbc_ref, ddtr_ref, ddtb_ref, dal_ref, ds_scr):
        first = jnp.logical_and(pl.program_id(0) == 0, pl.program_id(1) == 0)

        @pl.when(pl.program_id(1) == 0)
        def _():
            ds_scr[...] = jnp.zeros_like(ds_scr)

        @pl.when(first)
        def _():
            ddtb_ref[...] = jnp.zeros_like(ddtb_ref)
            dal_ref[...] = jnp.zeros_like(dal_ref)

        st = tuple(st_ref[m] for m in range(4))
        xs, bm, cm, dtr = _ssd_load(xbc_ref, dtr_ref)
        _, vjp = jax.vjp(ssd_chunk, st, xs, bm, cm, dtr, dtb_ref[...], al_ref[...])
        dys = tuple(dy_ref[:, m * LANES:(m + 1) * LANES] for m in range(4))
        dst_in = tuple(ds_scr[m] for m in range(4))
        dst, dxs, dbm, dcm, ddtr, ddtb, dal = vjp((dys, dst_in))
        for m in range(4):
            ds_scr[m] = dst[m]
            sl = slice(m * LANES, (m + 1) * LANES)
            dxbc_ref[:, sl] = dxs[m] + ex_ref[:, sl]
        for g in range(2):
            dxbc_ref[:, SSM_DIM + g * LANES:SSM_DIM + (g + 1) * LANES] = dbm[g]
            dxbc_ref[:, SSM_DIM + 2 * LANES + g * LANES:SSM_DIM + 2 * LANES + (g + 1) * LANES] = dcm[g]
        ddtr_ref[...] = ddtr
        ddtb_ref[...] += ddtb
        dal_ref[...] += dal

    vec = pl.BlockSpec((1, LANES), lambda b, i: (0, 0))
    rev = lambda b, i: (b, nc - 1 - i, 0)
    return pl.pallas_call(
        body, name="ssd_bwd", grid=(bsz, nc),
        in_specs=[pl.BlockSpec((None, n, 2 * SSM_DIM), rev),
                  pl.BlockSpec((None, n, LANES), lambda b, i: (b, nc - 1 - i, dt_col)), vec, vec,
                  pl.BlockSpec((None, None, 4, SSM_STATE, LANES), lambda b, i: (b, nc - 1 - i, 0, 0, 0)),
                  pl.BlockSpec((None, n, SSM_DIM), rev), pl.BlockSpec((None, n, SSM_DIM), rev)],
        out_specs=[pl.BlockSpec((None, n, 2 * SSM_DIM), rev), pl.BlockSpec((None, n, LANES), rev), vec, vec],
        out_shape=[jax.ShapeDtypeStruct((bsz, t, 2 * SSM_DIM), F32), jax.ShapeDtypeStruct((bsz, t, LANES), F32),
                   jax.ShapeDtypeStruct((1, LANES), F32), jax.ShapeDtypeStruct((1, LANES), F32)],
        scratch_shapes=[pltpu.VMEM((4, SSM_STATE, LANES), F32)],
        compiler_params=_params(("arbitrary", "arbitrary")),
    )(xbc, proj, dt_bias, a_log, states, dy, dxs_extra)


def sb_block(q, kj, vj, carry, maskf):
    mask = maskf > 0.5
    z = _each(lambda q_, k_: dg(q_, k_, 1, 1, P_BF16) * (HEAD_DIM ** -0.5), q, kj)
    ls = _each(lambda z_: -_softplus(-z_), z)
    lk = _each(lambda l_, z_: jnp.where(mask, l_ - z_, 0.0), ls, z)
    sfx = _each(suffix_sum, lk)
    att = _each(lambda l_, c_, s_, k_: jnp.where(mask, jnp.exp(l_ + c_ + s_ - k_), 0.0), ls, carry, sfx, lk)
    out = _each(mmb, att, vj)
    return out, _each(lambda c_, k_: c_ + jnp.sum(k_, axis=1, keepdims=True), carry, lk)


def _sb_mask(qi, j):
    n = Q_BLOCK
    return jnp.where(j * n + _iota((n, n), 1) < qi * n + _iota((n, n), 0), 1.0, 0.0)


def sb_fwd(q, k, v, ride=None):
    bsz, h, t, d = q.shape
    n = Q_BLOCK
    hp = ATTN_FWD_HEADS_PER_STEP
    grid = (bsz, h // hp, t // n)
    r_in, r_specs, r_out, r_ospecs, r_scr = _ride_args(ride)

    def body(*refs):
        q_ref, k_ref, v_ref = refs[:3]
        o_ref, c_ref = refs[3 + len(r_in):5 + len(r_in)]
        if ride is not None:
            first, last = _grid_first_last(grid)
            copies = _ride_start((refs[3], refs[5 + len(r_in)], *refs[-3:]), ride[1], first)
        qi = pl.program_id(2)
        lane = _iota((n, LANES), 1)

        c_ref[...] = jnp.zeros_like(c_ref)

        def step(i, state):
            acc, carry = state
            j = qi - i
            rows = pl.ds(pl.multiple_of(j * n, n), n)
            for hh in range(hp):
                c_ref[hh] = jnp.where(lane == j, carry[hh], c_ref[hh])
            o, carry = sb_block([q_ref[hh] for hh in range(hp)], [k_ref[hh, rows, :] for hh in range(hp)],
                                [v_ref[hh, rows, :] for hh in range(hp)], carry, _sb_mask(qi, j))
            return [a_ + o_ for a_, o_ in zip(acc, o)], carry

        init = ([jnp.zeros((n, d), F32) for _ in range(hp)], [jnp.zeros((n, 1), F32) for _ in range(hp)])
        acc, _ = lax.fori_loop(0, qi + 1, step, init)
        for hh in range(hp):
            o_ref[hh] = acc[hh]
        if ride is not None:
            _ride_wait(copies, last)

    blk = pl.BlockSpec((None, hp, n, d), lambda b, hg, i: (b, hg, i, 0))
    cblk = pl.BlockSpec((None, hp, n, LANES), lambda b, hg, i: (b, hg, i, 0))
    full = pl.BlockSpec((None, hp, t, d), lambda b, hg, i: (b, hg, 0, 0))
    return pl.pallas_call(
        body, name="sb_fwd", grid=grid, in_specs=[blk, full, full] + r_specs, out_specs=[blk, cblk] + r_ospecs,
        out_shape=[jax.ShapeDtypeStruct((bsz, h, t, d), F32), jax.ShapeDtypeStruct((bsz, h, t, LANES), F32)] + r_out,
        scratch_shapes=r_scr, compiler_params=_params(("arbitrary", "arbitrary", "arbitrary")),
    )(q, k, v, *r_in)


def sb_bwd(q, k, v, kept, do, ride=None):
    bsz, h, t, d = q.shape
    n = Q_BLOCK
    hp = SB_HEADS_PER_STEP
    grid = (bsz, h // hp, t // n)
    r_in, r_specs, r_out, r_ospecs, r_scr = _ride_args(ride)

    def body(*refs):
        q_ref, k_ref, v_ref, c_ref, do_ref = refs[:5]
        nin = 5 + len(r_in)
        dq_ref, dk_ref, dv_ref = refs[nin:nin + 3]
        if ride is not None:
            first, last = _grid_first_last(grid)
            copies = _ride_start((refs[5], refs[nin + 3], *refs[-3:]), ride[1], first)
        qi = pl.program_id(2)

        @pl.when(qi == 0)
        def _():
            dk_ref[...] = jnp.zeros_like(dk_ref)
            dv_ref[...] = jnp.zeros_like(dv_ref)

        heads = range(hp)
        qv = [q_ref[hh] for hh in heads]
        kept_v = [c_ref[hh] for hh in heads]
        lane = _iota((n, LANES), 1)

        def bwd_step(j, state):
            dq, dcarry = state
            rows = pl.ds(pl.multiple_of(j * n, n), n)
            carry_in = [jnp.sum(jnp.where(lane == j, t_, 0.0), axis=1, keepdims=True) for t_ in kept_v]
            _, vjp = jax.vjp(sb_block, qv, [k_ref[hh, rows, :] for hh in heads],
                             [v_ref[hh, rows, :] for hh in heads], carry_in, _sb_mask(qi, j))
            dqj, dkj, dvj, dc, _ = vjp(([do_ref[hh] for hh in heads], dcarry))
            for hh in heads:
                dk_ref[hh, rows, :] += dkj[hh]
                dv_ref[hh, rows, :] += dvj[hh]
            return [a_ + b_ for a_, b_ in zip(dq, dqj)], dc

        init = ([jnp.zeros((n, d), F32) for _ in heads], [jnp.zeros((n, 1), F32) for _ in heads])
        dq, _ = lax.fori_loop(0, qi + 1, bwd_step, init)
        for hh in heads:
            dq_ref[hh] = dq[hh]
        if ride is not None:
            _ride_wait(copies, last)

    blk = pl.BlockSpec((None, hp, n, d), lambda b, hg, i: (b, hg, i, 0))
    cblk = pl.BlockSpec((None, hp, n, LANES), lambda b, hg, i: (b, hg, i, 0))
    full = pl.BlockSpec((None, hp, t, d), lambda b, hg, i: (b, hg, 0, 0))
    shp = jax.ShapeDtypeStruct((bsz, h, t, d), F32)
    return pl.pallas_call(
        body, name="sb_bwd", grid=grid, in_specs=[blk, full, full, cblk, blk] + r_specs,
        out_specs=[blk, full, full] + r_ospecs, out_shape=[shp, shp, shp] + r_out,
        scratch_shapes=r_scr, compiler_params=_params(("arbitrary", "arbitrary", "arbitrary")),
    )(q, k, v, kept, do, *r_in)


def _bdot(a, b, ca, cb):
    return _dg_raw(a, b, ca, cb, P_BF16)


def _mla_scores(qn, qp, knj, kpj, qi, j):
    n = Q_BLOCK
    mask = j * n + _iota((n, n), 1) <= qi * n + _iota((n, n), 0)
    scale = (MLA_NOPE + MLA_ROPE) ** -0.5
    return _each(lambda a_, b_, k_: jnp.where(mask, (_bdot(a_, k_, 1, 1) + _bdot(b_, kpj, 1, 1)) * scale, NEG_BIG),
                 qn, qp, knj)


def _mla_specs(t, hp):
    n = Q_BLOCK
    return (pl.BlockSpec((None, hp, n, MLA_NOPE), lambda b, hg, i: (b, hg, i, 0)),
            pl.BlockSpec((None, hp, n, MLA_ROPE), lambda b, hg, i: (b, hg, i, 0)),
            pl.BlockSpec((None, hp, t, MLA_NOPE), lambda b, hg, i: (b, hg, 0, 0)),
            pl.BlockSpec((None, None, t, MLA_ROPE), lambda b, hg, i: (b, 0, 0, 0)),
            pl.BlockSpec((None, hp, n, 1), lambda b, hg, i: (b, hg, i, 0)))


def mla_fwd(qn, qp, kn, kp, v):
    bsz, h, t, _ = qn.shape
    n, hp = Q_BLOCK, ATTN_FWD_HEADS_PER_STEP
    heads = range(hp)

    def body(qn_ref, qp_ref, kn_ref, kp_ref, v_ref, o_ref, lse_ref):
        qi = pl.program_id(2)
        qn_v, qp_v = [qn_ref[hh] for hh in heads], [qp_ref[hh] for hh in heads]

        def step(j, state):
            m, l, acc = state
            rows = pl.ds(pl.multiple_of(j * n, n), n)
            s = _mla_scores(qn_v, qp_v, [kn_ref[hh, rows, :] for hh in heads], kp_ref[rows, :], qi, j)
            m_new = _each(lambda m_, s_: jnp.maximum(m_, jnp.max(s_, axis=1, keepdims=True)), m, s)
            p = _each(lambda s_, m_: jnp.exp(s_ - m_), s, m_new)
            corr = _each(lambda a_, b_: jnp.exp(a_ - b_), m, m_new)
            l = _each(lambda l_, c_, p_: l_ * c_ + jnp.sum(p_, axis=1, keepdims=True), l, corr, p)
            pv = _each(lambda p_, v_: _bdot(p_, v_, 1, 0), p, [v_ref[hh, rows, :] for hh in heads])
            acc = _each(lambda a_, c_, x_: a_ * c_ + x_, acc, corr, pv)
            return m_new, l, acc

        init = ([jnp.full((n, 1), NEG_BIG, F32) for _ in heads], [jnp.zeros((n, 1), F32) for _ in heads],
                [jnp.zeros((n, MLA_NOPE), F32) for _ in heads])
        m, l, acc = lax.fori_loop(0, qi + 1, step, init)
        for hh in heads:
            o_ref[hh] = acc[hh] / l[hh]
            lse_ref[hh] = m[hh] + jnp.log(l[hh])

    qn_s, qp_s, kn_s, kp_s, row_s = _mla_specs(t, hp)
    return pl.pallas_call(
        body, name="mla_fwd", grid=(bsz, h // hp, t // n), in_specs=[qn_s, qp_s, kn_s, kp_s, kn_s],
        out_specs=[qn_s, row_s],
        out_shape=[jax.ShapeDtypeStruct(qn.shape, F32), jax.ShapeDtypeStruct((bsz, h, t, 1), F32)],
        compiler_params=_params(("parallel", "parallel", "arbitrary")),
    )(qn, qp, kn, kp, v)


def mla_bwd(qn, qp, kn, kp, v, o, lse, do):
    bsz, h, t, _ = qn.shape
    n, hp = Q_BLOCK, MLA_HEADS_PER_STEP
    heads = range(hp)
    scale = (MLA_NOPE + MLA_ROPE) ** -0.5

    def body(qn_ref, qp_ref, kn_ref, kp_ref, v_ref, o_ref, lse_ref, do_ref,
             dqn_ref, dqp_ref, dkn_ref, dkp_ref, dv_ref):
        hg, qi = pl.program_id(1), pl.program_id(2)

        @pl.when(qi == 0)
        def _():
            dkn_ref[...] = jnp.zeros_like(dkn_ref)
            dv_ref[...] = jnp.zeros_like(dv_ref)

        @pl.when(jnp.logical_and(qi == 0, hg == 0))
        def _():
            dkp_ref[...] = jnp.zeros_like(dkp_ref)

        qn_v, qp_v = [qn_ref[hh] for hh in heads], [qp_ref[hh] for hh in heads]
        do_v, lse_v = [do_ref[hh] for hh in heads], [lse_ref[hh] for hh in heads]
        dsum = [jnp.sum(do_v[hh] * o_ref[hh], axis=1, keepdims=True) for hh in heads]

        def step(j, state):
            dqn, dqp = state
            rows = pl.ds(pl.multiple_of(j * n, n), n)
            knj, vj, kpj = [kn_ref[hh, rows, :] for hh in heads], [v_ref[hh, rows, :] for hh in heads], kp_ref[rows, :]
            s = _mla_scores(qn_v, qp_v, knj, kpj, qi, j)
            p = _each(lambda s_, l_: jnp.exp(s_ - l_), s, lse_v)
            dp = _each(lambda d_, v_: _bdot(d_, v_, 1, 1), do_v, vj)
            ds = _each(lambda p_, dp_, d_: p_ * (dp_ - d_) * scale, p, dp, dsum)
            dqn = _each(lambda a_, ds_, k_: a_ + _bdot(ds_, k_, 1, 0), dqn, ds, knj)
            dqp = _each(lambda a_, ds_: a_ + _bdot(ds_, kpj, 1, 0), dqp, ds)
            dkn = _each(lambda ds_, q_: _bdot(ds_, q_, 0, 0), ds, qn_v)
            dv = _each(lambda p_, d_: _bdot(p_, d_, 0, 0), p, do_v)
            dkp = _each(lambda ds_, q_: _bdot(ds_, q_, 0, 0), ds, qp_v)
            for hh in heads:
                dkn_ref[hh, rows, :] += dkn[hh]
                dv_ref[hh, rows, :] += dv[hh]
            dkp_ref[rows, :] += functools.reduce(lambda a_, b_: a_ + b_, dkp)
            return dqn, dqp

        init = ([jnp.zeros((n, MLA_NOPE), F32) for _ in heads], [jnp.zeros((n, MLA_ROPE), F32) for _ in heads])
        dqn, dqp = lax.fori_loop(0, qi + 1, step, init)
        for hh in heads:
            dqn_ref[hh] = dqn[hh]
            dqp_ref[hh] = dqp[hh]

    qn_s, qp_s, kn_s, kp_s, row_s = _mla_specs(t, hp)
    return pl.pallas_call(
        body, name="mla_bwd", grid=(bsz, h // hp, t // n),
        in_specs=[qn_s, qp_s, kn_s, kp_s, kn_s, qn_s, row_s, qn_s],
        out_specs=[qn_s, qp_s, kn_s, kp_s, kn_s],
        out_shape=[jax.ShapeDtypeStruct(qn.shape, F32), jax.ShapeDtypeStruct(qp.shape, F32),
                   jax.ShapeDtypeStruct(kn.shape, F32), jax.ShapeDtypeStruct(kp.shape, F32),
                   jax.ShapeDtypeStruct(v.shape, F32)],
        compiler_params=_params(("parallel", "arbitrary", "arbitrary")),
    )(qn, qp, kn, kp, v, o, lse, do)


def rope(name, x, pos, inv_freq, sign):
    bsz, hx, t, d = x.shape
    half = d // 2

    tt = _largest_tile(t, 512, SUBLANES)

    def body(x_ref, pos_ref, f_ref, o_ref):
        ang = pos_ref[...].astype(F32) * f_ref[...]
        cos, sin = jnp.cos(ang), sign * jnp.sin(ang)
        ri, ci = _iota((d, d), 0), _iota((d, d), 1)
        rot = jnp.where(ri == ci + half, -1.0, 0.0) + jnp.where(ri + half == ci, 1.0, 0.0)
        for hh in range(hx):
            xv = x_ref[hh]
            o_ref[hh] = xv * cos + mmf(xv, rot) * sin

    blk = pl.BlockSpec((None, hx, tt, d), lambda b, i: (b, 0, i, 0))
    return pl.pallas_call(
        body, name=name, grid=(bsz, t // tt),
        in_specs=[blk, pl.BlockSpec((None, tt, 1), lambda b, i: (b, i, 0)), pl.BlockSpec((1, d), lambda b, i: (0, 0))],
        out_specs=blk, out_shape=jax.ShapeDtypeStruct(x.shape, F32),
        compiler_params=_params(("parallel", "parallel")),
    )(x, pos, inv_freq)


def loss_head(h, target, tm):
    n, d = h.shape

    def body(h_ref, t_ref, dh_ref, l_ref):
        @pl.when(pl.program_id(0) == 0)
        def _():
            l_ref[...] = jnp.zeros_like(l_ref)

        e = h_ref[...] - t_ref[...]
        dh_ref[...] = e * (1.0 / d)
        l_ref[...] += jnp.sum(e * e, axis=(0, 1), keepdims=True) * (0.5 / d)

    row = pl.BlockSpec((tm, d), lambda i: (i, 0))
    dh, l = pl.pallas_call(
        body, name="loss_head", grid=(n // tm,), in_specs=[row, row],
        out_specs=[row, pl.BlockSpec((SUBLANES, LANES), lambda i: (0, 0))],
        out_shape=[jax.ShapeDtypeStruct((n, d), F32), jax.ShapeDtypeStruct((SUBLANES, LANES), F32)],
        compiler_params=_params(("arbitrary",)),
    )(h, target)
    return dh, l[0, 0]


def _exchange_copies(src_ref, out_ref, send_sems, recv_sems, local_sem, gather):
    x, y, c = lax.axis_index("x"), lax.axis_index("y"), lax.axis_index("c")
    me = 4 * x + 2 * y + c
    copies = [pltpu.make_async_copy(src_ref if gather else src_ref.at[me], out_ref.at[me], local_sem)]
    for m in range(1, N_DEV):
        px, py, pc = x ^ (m >> 2), y ^ ((m >> 1) & 1), c ^ (m & 1)
        peer = 4 * px + 2 * py + pc
        copies.append(pltpu.make_async_remote_copy(
            src_ref=src_ref if gather else src_ref.at[peer], dst_ref=out_ref.at[me],
            send_sem=send_sems.at[m], recv_sem=recv_sems.at[m],
            device_id=(px, py, pc), device_id_type=pl.DeviceIdType.MESH))
    return copies


def _exchange_start(copies):
    for cp in copies:
        cp.start()


def _exchange_wait(copies):
    for cp in copies[1:]:
        cp.wait_recv()
    for cp in copies[1:]:
        cp.wait_send()
    copies[0].wait()


EXCHANGE_SCRATCH = [pltpu.SemaphoreType.DMA((N_DEV,)), pltpu.SemaphoreType.DMA((N_DEV,)),
                    pltpu.SemaphoreType.DMA(())]


def _exchange_out(src):
    return jax.ShapeDtypeStruct((N_DEV, src.shape[-2], LANES), src.dtype)


def peer_exchange(name, src, gather):
    def body(src_ref, out_ref, send_sems, recv_sems, local_sem):
        copies = _exchange_copies(src_ref, out_ref, send_sems, recv_sems, local_sem, gather)
        _exchange_start(copies)
        _exchange_wait(copies)

    return pl.pallas_call(
        body, name=name,
        in_specs=[pl.BlockSpec(memory_space=pl.ANY)], out_specs=pl.BlockSpec(memory_space=pl.ANY),
        out_shape=_exchange_out(src), scratch_shapes=list(EXCHANGE_SCRATCH),
    )(src)


def _grid_first_last(grid):
    ids = [pl.program_id(a) for a in range(len(grid))]
    first = functools.reduce(jnp.logical_and, [i == 0 for i in ids])
    last = functools.reduce(jnp.logical_and, [i == g - 1 for i, g in zip(ids, grid)])
    return first, last


def _ride_start(refs, gather, first):
    copies = _exchange_copies(*refs, gather)

    @pl.when(first)
    def _():
        _exchange_start(copies)

    return copies


def _ride_wait(copies, last):
    @pl.when(last)
    def _():
        _exchange_wait(copies)


def _ride_args(ride):
    if ride is None:
        return [], [], [], [], []
    hbm = pl.BlockSpec(memory_space=pl.ANY)
    return [ride[0]], [hbm], [_exchange_out(ride[0])], [hbm], list(EXCHANGE_SCRATCH)


def adamw_sum(name, parts, w, m, v):
    r = w.shape[0]
    tr = ADAM_ROWS
    assert r % tr == 0

    def body(p_ref, w_ref, m_ref, v_ref, g_ref, d_ref, nm_ref, nv_ref):
        g = p_ref[0]
        for j in range(1, N_DEV):
            g = g + p_ref[j]
        mm_ = ADAM_B1 * m_ref[...] + (1.0 - ADAM_B1) * g
        vv = ADAM_B2 * v_ref[...] + (1.0 - ADAM_B2) * (g * g)
        m_hat = mm_ / (1.0 - ADAM_B1 ** ADAM_STEP)
        v_hat = vv / (1.0 - ADAM_B2 ** ADAM_STEP)
        g_ref[...] = g
        d_ref[...] = -ADAM_LR * (m_hat / (jnp.sqrt(v_hat) + ADAM_EPS) + ADAM_WD * w_ref[...])
        nm_ref[...] = mm_
        nv_ref[...] = vv

    row = pl.BlockSpec((tr, LANES), lambda i: (i, 0))
    shp = jax.ShapeDtypeStruct((r, LANES), F32)
    return pl.pallas_call(
        body, name=name, grid=(r // tr,),
        in_specs=[pl.BlockSpec((N_DEV, tr, LANES), lambda i: (0, i, 0)), row, row, row],
        out_specs=[row] * 4, out_shape=[shp] * 4,
        compiler_params=_params(("parallel",)),
    )(parts, w, m, v)


WEIGHTS = ['l0_w_in', 'rwkv_mix', 'rwkv_w0', 'rwkv_w2', 'rwkv_a0', 'rwkv_a2', 'rwkv_g2', 'rwkv_k_k', 'rwkv_k_a',
           'rwkv_r_k', 'rwkv_ln_g', 'rwkv_ln_b', 'ssm_conv_w', 'ssm_conv_b', 'ssm_dt_bias', 'ssm_a_log', 'ssm_d',
           'ssm_norm_g', 'l0_w_out', 'l0_ln1_g', 'l0_ln1_b', 'ffn0_w_up', 'ffn0_conv_w', 'ffn0_conv_b',
           'ffn0_w_down', 'l0_ln2_g', 'l0_ln2_b', 'l1_w_in', 'mla_q_norm_g', 'mla_w_uq', 'mla_kv_norm_g',
           'mla_w_ukv', 'l1_w_out', 'l1_ln1_g', 'l1_ln1_b', 'ffn1_w_up', 'ffn1_conv_w', 'ffn1_conv_b',
           'ffn1_w_down', 'l1_ln2_g', 'l1_ln2_b']
SHARD_AXIS = {'l0_w_in': 1, 'rwkv_w2': 1, 'rwkv_a2': 1, 'rwkv_g2': 1, 'ssm_conv_w': 1, 'l0_w_out': 0,
              'ffn0_w_up': 1, 'ffn0_conv_w': 1, 'ffn0_w_down': 0, 'l1_w_in': 1, 'mla_w_uq': 1, 'mla_w_ukv': 1,
              'l1_w_out': 0, 'ffn1_w_up': 1, 'ffn1_conv_w': 1, 'ffn1_w_down': 0}
MATMUL_W = ['l0_w_in', 'rwkv_w2', 'rwkv_a2', 'rwkv_g2', 'l0_w_out', 'ffn0_w_up', 'ffn0_w_down', 'l1_w_in',
            'mla_w_uq', 'mla_w_ukv', 'l1_w_out', 'ffn1_w_up', 'ffn1_w_down']
CONV_W = ['ssm_conv_w', 'ffn0_conv_w', 'ffn1_conv_w']
TOK_TILE = 256
ADAM_ROWS = 512


def _ceil_to(size, unit):
    return -(-size // unit) * unit


def _flat_rows(pieces, seg_rows, total_rows):
    unit = seg_rows * LANES
    out, total = [], 0
    for p in pieces:
        p = p.reshape(-1)
        pad = _ceil_to(p.size, unit) - p.size
        out.append(jnp.pad(p, (0, pad)) if pad else p)
        total += p.size + pad
    tail = _ceil_to(total, total_rows * LANES) - total
    if tail:
        out.append(jnp.zeros((tail,), out[0].dtype))
    return jnp.concatenate(out).reshape(-1, LANES)


def _unflatten(flat2d, shapes, seg_rows):
    flat = flat2d.reshape(-1)
    out, off = [], 0
    for shp in shapes:
        size = math.prod(shp)
        out.append(flat[off:off + size].reshape(shp))
        off += _ceil_to(size, seg_rows * LANES)
    return out


def _to_heads(t2, bsz, h):
    n, w = t2.shape
    return t2.reshape(bsz, n // bsz, h, w // h).transpose(0, 2, 1, 3)


def _from_heads(t4):
    b, h, t, d = t4.shape
    return t4.transpose(0, 2, 1, 3).reshape(b * t, h * d)


def _row(v):
    return v.reshape(1, -1)


def _pad_lanes(v):
    return jnp.pad(v.reshape(1, -1), ((0, 0), (0, LANES - v.size)))


def _local_step(a, w, comm=None):
    x = a['x']
    bsz, t, d = x.shape
    n = bsz * t
    tm = TOK_TILE
    pos = a['positions'].reshape(bsz, t, 1)
    inv_freq = 1.0 / (ROPE_THETA ** (jnp.arange(0, MLA_ROPE, 2, dtype=F32) / MLA_ROPE))
    inv_freq = jnp.concatenate([inv_freq, inv_freq]).reshape(1, MLA_ROPE)
    target = a['loss_target'].reshape(n, d)

    wi0 = w['l0_w_in']
    win0 = jnp.concatenate([wi0[:, 0:1536], wi0[:, 1792:3328], wi0[:, 1536:1792], wi0[:, 3328:3336],
                            jnp.zeros((d, L0_PAD - 3336), wi0.dtype)], axis=1)
    w2p = jnp.concatenate([w['rwkv_w2'], jnp.zeros_like(w['rwkv_w2'])], axis=0)
    a2p = jnp.concatenate([jnp.zeros_like(w['rwkv_a2']), w['rwkv_a2']], axis=0)
    mix = a['rwkv_mix']
    taps = jnp.stack([mix, 1.0 - mix])
    zero_b = jnp.zeros((1, mix.size), F32)
    rw_map = lambda j: j + jnp.where(j >= 12, 12, 0)
    ssm_map = lambda j: j + 16
    gate_map = lambda j: j
    up_map = lambda j: j + D_FF // LANES
    dt_col = 3328 // LANES
    dtb, alog, dsk = _pad_lanes(a['ssm_dt_bias']), _pad_lanes(a['ssm_a_log']), _pad_lanes(a['ssm_d'])
    pre_p = [_row(a['rwkv_w0']), w2p, _row(a['rwkv_a0']), a2p, w['rwkv_g2'], _row(a['rwkv_k_k']), _row(a['rwkv_k_a'])]
    post_p = [_row(a['rwkv_ln_g']), _row(a['rwkv_ln_b']), _row(a['rwkv_r_k'])]
    sp_p = [dsk, _row(a['ssm_norm_g'])]

    def ln(name, h, y, layer, which):
        ps = [_row(a[f'l{layer}_ln{which}_g']), _row(a[f'l{layer}_ln{which}_b'])]
        return tok_fwd(name, f_ln, [(h, d, 0), (y, d, 0)], ps, [d], tm, n, t)[0]

    def ffn_fwd(layer, h):
        up = mm(h, w[f'ffn{layer}_w_up'], 'nn', f'ffn{layer}_up')
        act = dwconv_fwd(f'ffn{layer}_conv', up.reshape(bsz, t, 2 * D_FF), gate_map, w[f'ffn{layer}_conv_w'],
                         _row(a[f'ffn{layer}_conv_b']), 3, True, upmap=up_map)
        act = act.reshape(n, D_FF)
        return up, act, mm(act, w[f'ffn{layer}_w_down'], 'nn', f'ffn{layer}_down')

    x2 = x.reshape(n, d)
    proj0 = mm(x2, win0, 'nn', 'l0_in')
    p0 = proj0.reshape(bsz, t, L0_PAD)
    xs_r = dwconv_fwd('rwkv_shift', p0, rw_map, taps, zero_b, 2, False).reshape(n, 1792)
    pre_x = [(xs_r, 512, 0), (xs_r, 512, 1), (xs_r, 512, 2), (xs_r, LANES, 12), (xs_r, LANES, 13)]
    heads64 = (RWKV_HEADS, HEAD_DIM)
    r_, v_, lw, kmod, al, be, gt = tok_fwd('rwkv_pre', f_rwkv_pre, pre_x, pre_p, [heads64] * 6 + [RWKV_DIM],
                                           tm, n, t)
    scan_in = [r_, lw, kmod, v_, al, be]
    if comm is None:
        y_h, rstates = rwkv_scan_fwd(*scan_in)
    else:
        y_h, rstates, got = rwkv_scan_fwd(*scan_in, ride=comm.weights_ride(MID_GATHER))
        w = {**w, **comm.weights(MID_GATHER, got)}
    wi1 = w['l1_w_in']
    win1 = jnp.concatenate([wi1, jnp.zeros((d, L1_PAD - 1952), wi1.dtype)], axis=1)
    wq3 = w['mla_w_uq'].reshape(-1, 8, MLA_NOPE + MLA_ROPE)
    wkv3 = w['mla_w_ukv'].reshape(-1, 8, 2 * MLA_NOPE)
    mla_p = [_row(a['mla_q_norm_g']), wq3[:, :, :MLA_NOPE].reshape(-1, 512), wq3[:, :, MLA_NOPE:].reshape(-1, 256),
             _row(a['mla_kv_norm_g']), wkv3[:, :, :MLA_NOPE].reshape(-1, 512), wkv3[:, :, MLA_NOPE:].reshape(-1, 512)]
    post_x = [y_h, r_, kmod, v_, (gt, 512, 0)]
    y_a = tok_fwd('rwkv_post', f_rwkv_post, post_x, post_p, [RWKV_DIM], tm, n, t)[0]
    xbc = dwconv_fwd('ssm_conv', p0, ssm_map, w['ssm_conv_w'], _row(a['ssm_conv_b']), 4, True)
    ys, sstates = ssd_fwd(xbc, p0, dt_col, dtb, alog)
    xbc2 = xbc.reshape(n, 2 * SSM_DIM)
    sp_x = [(ys.reshape(n, SSM_DIM), 512, 0), (xbc2, 512, 0), (proj0, 512, 3)]
    y_b = tok_fwd('ssd_post', f_ssd_post, sp_x, sp_p, [SSM_DIM], tm, n, t)[0]
    wo0 = w['l0_w_out']
    mixed0 = mm(y_b, wo0, 'nn', 'l0_out_b', b_rows=(512, 512), add=mm(y_a, wo0, 'nn', 'l0_out_a', b_rows=(0, 512)))
    h1 = ln('l0_ln1', x2, mixed0, 0, 1)
    up0, act0, f0 = ffn_fwd(0, h1)
    h2 = ln('l0_ln2', h1, f0, 0, 2)

    proj1 = mm(h2, win1, 'nn', 'l1_in')
    q_sb, k_sb, v_sb = tok_fwd('sb_split', f_same, [(proj1, 512, 0), (proj1, 512, 1), (proj1, 512, 2)], [],
                               [heads64] * 3, tm, n, t)
    if comm is None:
        o_c, sb_kept = sb_fwd(q_sb, k_sb, v_sb)
    else:
        o_c, sb_kept, got = sb_fwd(q_sb, k_sb, v_sb, ride=comm.weights_ride(FFN1_GATHER))
        w = {**w, **comm.weights(FFN1_GATHER, got)}
    mla_x = [(proj1, 256, 6), (proj1, LANES, 14)]
    qn, qp_raw, kn, vv = tok_fwd('mla_pre', f_mla_pre, mla_x, mla_p, [heads64, (8, MLA_ROPE), heads64, heads64],
                                 tm, n, t)
    kp_raw = proj1[:, 1920:1920 + MLA_ROPE].reshape(bsz, 1, t, MLA_ROPE)
    qp = rope('rope_q', qp_raw, pos, inv_freq, 1.0)
    kp = rope('rope_k', kp_raw, pos, inv_freq, 1.0)
    o_d, lse_d = mla_fwd(qn, qp, kn, kp, vv)
    y_cd = tok_fwd('attn_merge', f_concat, [o_c, o_d], [], [2 * RWKV_DIM], tm, n, t)[0]
    wo1 = w['l1_w_out']
    mixed1 = mm(y_cd, wo1, 'nn', 'l1_out')
    h3 = ln('l1_ln1', h2, mixed1, 1, 1)
    up1, act1, f1 = ffn_fwd(1, h3)
    h4 = ln('l1_ln2', h3, f1, 1, 2)
    dh4, loss = loss_head(h4, target, tm)

    g = {}

    def ln_bwd(name, h, y, layer, which, dout):
        ps = [_row(a[f'l{layer}_ln{which}_g']), _row(a[f'l{layer}_ln{which}_b'])]
        (dh, dy), (dg, db) = tok_bwd(name, f_ln, [(h, d, 0), (y, d, 0)], ps, [[dout]], tm, n, t)
        g[f'l{layer}_ln{which}_g'], g[f'l{layer}_ln{which}_b'] = dg.reshape(-1), db.reshape(-1)
        return dh, dy

    def ffn_bwd(layer, h, up, act, df, dh_res):
        wup, wdown = w[f'ffn{layer}_w_up'], w[f'ffn{layer}_w_down']
        g[f'ffn{layer}_w_down'] = mm(act, df, 'tn', f'ffn{layer}_dwdown')
        dact = mm(df, wdown, 'nt', f'ffn{layer}_dact').reshape(bsz, t, D_FF)
        dgate, dcw, dcb, dup = dwconv_bwd(f'ffn{layer}_conv_bwd', up.reshape(bsz, t, 2 * D_FF), gate_map,
                                          w[f'ffn{layer}_conv_w'], _row(a[f'ffn{layer}_conv_b']), 3, True, dact,
                                          upmap=up_map)
        dgate, dup = dgate.reshape(n, D_FF), dup.reshape(n, D_FF)
        g[f'ffn{layer}_conv_w'], g[f'ffn{layer}_conv_b'] = dcw, dcb.reshape(-1)
        g[f'ffn{layer}_w_up'] = (mm(h, dgate, 'tn', f'ffn{layer}_dwgate'), mm(h, dup, 'tn', f'ffn{layer}_dwup'))
        dh = mm(dgate, wup, 'nt', f'ffn{layer}_dh_gate', add=dh_res, b_cols=(0, D_FF))
        return mm(dup, wup, 'nt', f'ffn{layer}_dh_up', add=dh, b_cols=(D_FF, D_FF))

    dh3_res, df1 = ln_bwd('l1_ln2_bwd', h3, f1, 1, 2, dh4)
    dh3 = ffn_bwd(1, h3, up1, act1, df1, dh3_res)
    dh2_res, dmixed1 = ln_bwd('l1_ln1_bwd', h2, mixed1, 1, 1, dh3)
    g['l1_w_out'] = mm(y_cd, dmixed1, 'tn', 'l1_dwout')
    dy_cd = mm(dmixed1, wo1, 'nt', 'l1_dy')
    dy_c, dy_d = tok_fwd('attn_split', f_same, [(dy_cd, 512, 0), (dy_cd, 512, 1)], [], [heads64] * 2, tm, n, t)
    parts = {}
    if comm is None:
        dq_sb, dk_sb, dv_sb = sb_bwd(q_sb, k_sb, v_sb, sb_kept, dy_c)
    else:
        dq_sb, dk_sb, dv_sb, parts['a'] = sb_bwd(q_sb, k_sb, v_sb, sb_kept, dy_c, ride=comm.grad_ride('a', g))
    dqn, dqp, dkn, dkp, dvv = mla_bwd(qn, qp, kn, kp, vv, o_d, lse_d, dy_d)
    dqp_raw = rope('rope_q_bwd', dqp, pos, inv_freq, -1.0)
    dkp_raw = rope('rope_k_bwd', dkp, pos, inv_freq, -1.0).reshape(n, MLA_ROPE)
    (dcq, dckv), (dqg, dwq_n, dwq_p, dkvg, dwk, dwv) = tok_bwd('mla_pre_bwd', f_mla_pre, mla_x, mla_p,
                                                               [[dqn], [dqp_raw], [dkn], [dvv]], tm, n, t)
    g['mla_q_norm_g'], g['mla_kv_norm_g'] = dqg.reshape(-1), dkvg.reshape(-1)
    g['mla_w_uq'] = jnp.concatenate([dwq_n.reshape(-1, 8, MLA_NOPE), dwq_p.reshape(-1, 8, MLA_ROPE)],
                                    axis=2).reshape(-1, 8 * (MLA_NOPE + MLA_ROPE))
    g['mla_w_ukv'] = jnp.concatenate([dwk.reshape(-1, 8, MLA_NOPE), dwv.reshape(-1, 8, MLA_NOPE)],
                                     axis=2).reshape(-1, 16 * MLA_NOPE)
    dkp_pad = jnp.pad(dkp_raw, ((0, 0), (0, LANES - MLA_ROPE)))
    dproj1 = tok_fwd('l1_dproj', f_concat, [dq_sb, dk_sb, dv_sb, (dcq, 256, 0), (dckv, LANES, 0),
                                            (dkp_pad, LANES, 0)], [], [L1_PAD], tm, n, t)[0]
    g['l1_w_in'] = mm(h2, dproj1, 'tn', 'l1_dwin')[:, :1952]
    dh2 = mm(dproj1, win1, 'nt', 'l1_dh', add=dh2_res)

    dh1_res, df0 = ln_bwd('l0_ln2_bwd', h1, f0, 0, 2, dh2)
    dh1 = ffn_bwd(0, h1, up0, act0, df0, dh1_res)
    dx_res, dmixed0 = ln_bwd('l0_ln1_bwd', x2, mixed0, 0, 1, dh1)
    g['l0_w_out'] = (mm(y_a, dmixed0, 'tn', 'l0_dwout_a'), mm(y_b, dmixed0, 'tn', 'l0_dwout_b'))
    dy_a = mm(dmixed0, wo0, 'nt', 'l0_dy_a', b_rows=(0, 512))
    dy_b = mm(dmixed0, wo0, 'nt', 'l0_dy_b', b_rows=(512, 512))
    (dy_r, dr1, dkm1, dv1, dgt), (dlng, dlnb, drk) = tok_bwd('rwkv_post_bwd', f_rwkv_post, post_x, post_p, [[dy_a]],
                                                            tm, n, t, dx_layouts=[heads64] * 4 + [None])
    g['rwkv_ln_g'], g['rwkv_ln_b'] = dlng.reshape(-1), dlnb.reshape(-1)
    g['rwkv_r_k'] = drk.reshape(RWKV_HEADS, HEAD_DIM)
    (dys, dxs_skip, dz), (ddsk, dng) = tok_bwd('ssd_post_bwd', f_ssd_post, sp_x, sp_p, [[dy_b]], tm, n, t)
    g['ssm_d'], g['ssm_norm_g'] = ddsk[0, :SSM_HEADS], dng.reshape(-1)
    dxbc_act, ddtr, ddtb, dalog = ssd_bwd(xbc, p0, dt_col, dtb, alog, sstates, dys.reshape(bsz, t, SSM_DIM),
                                          dxs_skip.reshape(bsz, t, SSM_DIM))
    g['ssm_dt_bias'], g['ssm_a_log'] = ddtb[0, :SSM_HEADS], dalog[0, :SSM_HEADS]
    dxbc, dscw, dscb = dwconv_bwd('ssm_conv_bwd', p0, ssm_map, w['ssm_conv_w'], _row(a['ssm_conv_b']), 4, True,
                                  dxbc_act)
    g['ssm_conv_w'], g['ssm_conv_b'] = dscw, dscb.reshape(-1)
    if comm is None:
        dscan = rwkv_scan_bwd(*scan_in, rstates, dy_r)
    else:
        *dscan, parts['b'] = rwkv_scan_bwd(*scan_in, rstates, dy_r, ride=comm.grad_ride('b', g))
    dr2, dlw, dk2, dv2, dal, dbe = dscan
    pre_ct = [[dr1, dr2], [dv1, dv2], [dlw], [dkm1, dk2], [dal], [dbe], [dgt]]
    dpre_x, dpre_p = tok_bwd('rwkv_pre_bwd', f_rwkv_pre, pre_x, pre_p, pre_ct, tm, n, t)
    g['rwkv_w0'], g['rwkv_a0'] = dpre_p[0].reshape(-1), dpre_p[2].reshape(-1)
    g['rwkv_w2'], g['rwkv_a2'], g['rwkv_g2'] = dpre_p[1][:64], dpre_p[3][64:], dpre_p[4]
    g['rwkv_k_k'], g['rwkv_k_a'] = dpre_p[5].reshape(-1), dpre_p[6].reshape(-1)
    dxs_r = jnp.concatenate(dpre_x, axis=1).reshape(bsz, t, 1792)
    d_rw, dtaps, _ = dwconv_bwd('rwkv_shift_bwd', p0, rw_map, taps, zero_b, 2, False, dxs_r)
    d_rw = d_rw.reshape(n, 1792)
    g['rwkv_mix'] = dtaps[0] - dtaps[1]
    dproj0 = tok_fwd('l0_dproj', f_concat, [(d_rw, 1536, 0), (dz, 512, 0), (dxbc.reshape(n, 2 * SSM_DIM), 1024, 0),
                                            (d_rw, 256, 6), (ddtr.reshape(n, LANES), LANES, 0)],
                     [], [L0_PAD], tm, n, t)[0]
    dwin0 = mm(x2, dproj0, 'tn', 'l0_dwin')
    g['l0_w_in'] = jnp.concatenate([dwin0[:, 0:1536], dwin0[:, 3072:3328], dwin0[:, 1536:3072],
                                    dwin0[:, 3328:3336]], axis=1)
    dx = mm(dproj0, win0, 'nt', 'l0_dx', add=dx_res)
    if comm is not None:
        parts['c'] = peer_exchange('grad_exchange_c', comm.grad_ride('c', g)[0], False)
    return loss, dx.reshape(bsz, t, d), g, parts


GRAD_GROUPS = {
    'a': ['ffn1_w_up', 'ffn1_conv_w', 'ffn1_conv_b', 'ffn1_w_down', 'l1_ln2_g', 'l1_ln2_b'],
    'c': ['l0_w_in', 'rwkv_mix', 'rwkv_w0', 'rwkv_w2', 'rwkv_a0', 'rwkv_a2', 'rwkv_g2', 'rwkv_k_k', 'rwkv_k_a'],
}
GRAD_GROUPS['b'] = [nm for nm in WEIGHTS if nm not in GRAD_GROUPS['a'] + GRAD_GROUPS['c']]
FIRST_GATHER = ['l0_w_in', 'rwkv_w2', 'rwkv_a2', 'rwkv_g2'] + CONV_W
MID_GATHER = ['l0_w_out', 'ffn0_w_up', 'ffn0_w_down', 'l1_w_in', 'mla_w_uq', 'mla_w_ukv', 'l1_w_out']
FFN1_GATHER = ['ffn1_w_up', 'ffn1_w_down']
BF16_ROWS = 16


class _Comm:
    def __init__(self, a):
        self.a = a

    def _pieces(self, names):
        return [lax.bitcast_convert_type(self.a[nm], BF16) if nm in CONV_W else self.a[nm].astype(BF16)
                for nm in names]

    def _unpack(self, names, got):
        shapes = [p.shape for p in self._pieces(names)]
        blocks = [_unflatten(got[k], shapes, BF16_ROWS) for k in range(N_DEV)]
        out = {}
        for i, nm in enumerate(names):
            blk = [blocks[k][i] for k in range(N_DEV)]
            if nm in CONV_W:
                blk = [lax.bitcast_convert_type(b, F32) for b in blk]
            out[nm] = jnp.concatenate(blk, axis=SHARD_AXIS[nm])
        return out

    def first_weights(self):
        return self.weights(FIRST_GATHER, peer_exchange('gather_first_weights', self.weights_ride(FIRST_GATHER)[0], True))

    def weights_ride(self, names):
        return _flat_rows(self._pieces(names), BF16_ROWS, BF16_ROWS), True

    def weights(self, names, got):
        return self._unpack(names, got)

    def grad_ride(self, group, g):
        def shard_of(nm, k):
            gv = g[nm]
            if nm not in SHARD_AXIS:
                return gv
            per = N_DEV
            if isinstance(gv, tuple):
                gv, k, per = gv[k // 4], k % 4, 4
            width = gv.shape[SHARD_AXIS[nm]] // per
            return lax.slice_in_dim(gv, k * width, (k + 1) * width, axis=SHARD_AXIS[nm])

        names = GRAD_GROUPS[group]
        return jnp.stack([_flat_rows([shard_of(nm, k) for nm in names], SUBLANES, ADAM_ROWS)
                          for k in range(N_DEV)]), False


def _step(a):
    comm = _Comm(a)
    loss, dx, _, parts = _local_step(a, comm.first_weights(), comm)
    loss = lax.psum(loss, ('x', 'y', 'c'))
    res = {}
    for group, names in GRAD_GROUPS.items():
        flat = lambda prefix: _flat_rows([a[prefix + nm] for nm in names], SUBLANES, ADAM_ROWS)
        outs = adamw_sum(f'adamw_{group}', parts[group], flat(''), flat('m_'), flat('v_'))
        per_out = [_unflatten(o, [a[nm].shape for nm in names], SUBLANES) for o in outs]
        for i, nm in enumerate(names):
            res[nm] = [per_out[j][i] for j in range(4)]
    return (loss, dx, *[res[nm][j] for j in range(4) for nm in WEIGHTS])


def kernel(x, positions, l0_w_in, rwkv_mix, rwkv_w0, rwkv_w2, rwkv_a0, rwkv_a2, rwkv_g2, rwkv_k_k, rwkv_k_a, rwkv_r_k, rwkv_ln_g, rwkv_ln_b, ssm_conv_w, ssm_conv_b, ssm_dt_bias, ssm_a_log, ssm_d, ssm_norm_g, l0_w_out, l0_ln1_g, l0_ln1_b, ffn0_w_up, ffn0_conv_w, ffn0_conv_b, ffn0_w_down, l0_ln2_g, l0_ln2_b, l1_w_in, mla_q_norm_g, mla_w_uq, mla_kv_norm_g, mla_w_ukv, l1_w_out, l1_ln1_g, l1_ln1_b, ffn1_w_up, ffn1_conv_w, ffn1_conv_b, ffn1_w_down, l1_ln2_g, l1_ln2_b, loss_target, m_l0_w_in, m_rwkv_mix, m_rwkv_w0, m_rwkv_w2, m_rwkv_a0, m_rwkv_a2, m_rwkv_g2, m_rwkv_k_k, m_rwkv_k_a, m_rwkv_r_k, m_rwkv_ln_g, m_rwkv_ln_b, m_ssm_conv_w, m_ssm_conv_b, m_ssm_dt_bias, m_ssm_a_log, m_ssm_d, m_ssm_norm_g, m_l0_w_out, m_l0_ln1_g, m_l0_ln1_b, m_ffn0_w_up, m_ffn0_conv_w, m_ffn0_conv_b, m_ffn0_w_down, m_l0_ln2_g, m_l0_ln2_b, m_l1_w_in, m_mla_q_norm_g, m_mla_w_uq, m_mla_kv_norm_g, m_mla_w_ukv, m_l1_w_out, m_l1_ln1_g, m_l1_ln1_b, m_ffn1_w_up, m_ffn1_conv_w, m_ffn1_conv_b, m_ffn1_w_down, m_l1_ln2_g, m_l1_ln2_b, v_l0_w_in, v_rwkv_mix, v_rwkv_w0, v_rwkv_w2, v_rwkv_a0, v_rwkv_a2, v_rwkv_g2, v_rwkv_k_k, v_rwkv_k_a, v_rwkv_r_k, v_rwkv_ln_g, v_rwkv_ln_b, v_ssm_conv_w, v_ssm_conv_b, v_ssm_dt_bias, v_ssm_a_log, v_ssm_d, v_ssm_norm_g, v_l0_w_out, v_l0_ln1_g, v_l0_ln1_b, v_ffn0_w_up, v_ffn0_conv_w, v_ffn0_conv_b, v_ffn0_w_down, v_l0_ln2_g, v_l0_ln2_b, v_l1_w_in, v_mla_q_norm_g, v_mla_w_uq, v_mla_kv_norm_g, v_mla_w_ukv, v_l1_w_out, v_l1_ln1_g, v_l1_ln1_b, v_ffn1_w_up, v_ffn1_conv_w, v_ffn1_conv_b, v_ffn1_w_down, v_l1_ln2_g, v_l1_ln2_b):
    return _step(dict(locals()))
```

```python
import functools
import math

import jax
import jax.numpy as jnp
from jax import lax
from jax.experimental import pallas as pl
from jax.experimental.pallas import tpu as pltpu

F32 = jnp.float32
BF16 = jnp.bfloat16
HI = lax.Precision.HIGHEST

V7X_VMEM_BYTES = 64 * 1024 * 1024
VMEM_LIMIT = V7X_VMEM_BYTES - 8 * 1024 * 1024
LANES = 128
SUBLANES = 8
N_DEV = 8

D_MODEL = 1024
HEAD_DIM = 64
RWKV_DIM = 512
RWKV_HEADS = 8
RWKV_GN_EPS = 64e-5
RWKV_CHUNK = 64
SSM_DIM = 512
SSM_HEADS = 8
SSM_CHUNK = 128
SSM_STATE = 128
Q_BLOCK = 128
SB_HEADS_PER_STEP = 4
MLA_HEADS_PER_STEP = 4
ATTN_FWD_HEADS_PER_STEP = 8
MLA_NOPE = 64
MLA_ROPE = 32
ROPE_THETA = 10000.0
D_FF = 2816
DEPTH = 2
ALPHA = (2 * DEPTH) ** 0.25
L0_PAD = 3456
L1_PAD = 2048

ADAM_LR = 0.001
ADAM_B1 = 0.9
ADAM_B2 = 0.999
ADAM_EPS = 1e-08
ADAM_WD = 0.01
ADAM_STEP = 10

NEG_BIG = -1e30


def _params(sem=None):
    return pltpu.CompilerParams(dimension_semantics=sem, vmem_limit_bytes=VMEM_LIMIT)


P_F32, P_BF16, P_BF16X3 = 0, 1, 2


def _dg_raw(a, b, ca, cb, fast):
    dims = (((ca,), (cb,)), ((), ()))
    if fast == P_BF16:
        return lax.dot_general(a.astype(BF16), b.astype(BF16), dims, preferred_element_type=F32)
    prec = HI if fast == P_F32 else lax.Precision.HIGH
    return lax.dot_general(a, b, dims, precision=prec, preferred_element_type=F32)


@functools.partial(jax.custom_vjp, nondiff_argnums=(2, 3, 4))
def dg(a, b, ca, cb, fast):
    return _dg_raw(a, b, ca, cb, fast)


def _dg_fwd(a, b, ca, cb, fast):
    return _dg_raw(a, b, ca, cb, fast), (a, b)


def _dg_bwd(ca, cb, fast, res, ct):
    a, b = res
    fa, fb = 1 - ca, 1 - cb
    da = _dg_raw(ct, b, 1, fb, fast) if ca == 1 else _dg_raw(b, ct, fb, 1, fast)
    db = _dg_raw(a, ct, fa, 0, fast) if cb == 0 else _dg_raw(ct, a, 0, fa, fast)
    return da.astype(a.dtype), db.astype(b.dtype)


dg.defvjp(_dg_fwd, _dg_bwd)


def mmb(a, b):
    return dg(a, b, 1, 0, P_BF16)


def mmf(a, b):
    return dg(a, b, 1, 0, P_F32)


def mmf_nt(a, b):
    return dg(a, b, 1, 1, P_F32)


def mmf_tn(a, b):
    return dg(a, b, 0, 0, P_F32)


def mm3(a, b):
    return dg(a, b, 1, 0, P_BF16X3)


def mm3_nt(a, b):
    return dg(a, b, 1, 1, P_BF16X3)


def mm3_tn(a, b):
    return dg(a, b, 0, 0, P_BF16X3)


def _split3_dot(x, m01, cb, terms=3):
    parts, rest = [], x
    for i in range(terms):
        parts.append(rest.astype(BF16))
        if i + 1 < terms:
            rest = rest - parts[-1].astype(F32)
    rows = x.shape[0]
    out = lax.dot_general(jnp.concatenate(parts, axis=0), m01.astype(BF16), (((1,), (cb,)), ((), ())),
                          preferred_element_type=F32)
    return functools.reduce(lambda a_, b_: a_ + b_, [out[i * rows:(i + 1) * rows] for i in range(terms)])


def _lower_ones(n):
    return jnp.where(_iota((n, n), 0) >= _iota((n, n), 1), 1.0, 0.0)


SUFFIX_TERMS = 2


@jax.custom_vjp
def suffix_sum(x):
    return _split3_dot(x, _lower_ones(x.shape[1]), 0, SUFFIX_TERMS)


def _suffix_sum_fwd(x):
    return suffix_sum(x), None


def _suffix_sum_bwd(_, ct):
    return (_split3_dot(ct, _lower_ones(ct.shape[1]), 1, SUFFIX_TERMS),)


suffix_sum.defvjp(_suffix_sum_fwd, _suffix_sum_bwd)


def _iota(shape, dim):
    return lax.broadcasted_iota(jnp.int32, shape, dim)


def _softplus(x):
    return jnp.maximum(x, 0.0) + jnp.log1p(jnp.exp(-jnp.abs(x)))


def _silu(x):
    return x * jax.nn.sigmoid(x)


def _largest_tile(n, cap, mult):
    best = None
    t = mult
    while t <= min(n, cap):
        if n % t == 0:
            best = t
        t += mult
    return n if best is None else best


MM_VMEM_BUDGET = 40 * 1024 * 1024
V7X_HBM_BYTES_PER_S = 3.2e12
GRID_STEP_S = 0.35e-6


def _mm_tiles(M, N, K, a_bytes, b_bytes, has_add):
    def divs(n):
        return [d for d in range(LANES, n + 1, LANES) if n % d == 0] or [n]

    best = None
    for tm in divs(M):
        for tn in divs(N):
            if tm * tn * 4 > 12 * 1024 * 1024:
                continue
            for tk in divs(K):
                vmem = (2 * (tm * tk * a_bytes + tk * tn * b_bytes) + 2 * tm * tn * 4 * (2 if has_add else 1)
                        + (tm * tk + tk * tn) * 2 + tm * tn * 4)
                if vmem > MM_VMEM_BUDGET:
                    continue
                ni, nj, nk = M // tm, N // tn, K // tk
                a_reads = M * K * a_bytes * (1 if nk == 1 else nj)
                b_reads = K * N * b_bytes * (1 if (nk == 1 and nj == 1) else ni)
                traffic = a_reads + b_reads + M * N * 4 * (2 if has_add else 1)
                cost = traffic / V7X_HBM_BYTES_PER_S + ni * nj * nk * GRID_STEP_S
                if min(tm, tn, tk) < 256 and min(M, N, K) >= 256:
                    cost *= 1.5
                if best is None or cost < best[0]:
                    best = (cost, tm, tn, tk)
    return best[1:]


def mm(a, b, mode, name, add=None, b_rows=None, b_cols=None):
    r0, nr = b_rows or (0, b.shape[0])
    c0, nc = b_cols or (0, b.shape[1])
    if mode == "nn":
        (M, K), N = a.shape, nc
        assert nr == K
    elif mode == "nt":
        (M, K), N = a.shape, nr
        assert nc == K
    else:
        (K, M), N = a.shape, b.shape[1]
        assert b_rows is None and b_cols is None
    has_add = add is not None
    tm, tn, tk = _mm_tiles(M, N, K, a.dtype.itemsize, b.dtype.itemsize, has_add)
    nk = K // tk
    keep_a = nk == 1 and N // tn > 1 and a.dtype != BF16
    if mode == "nn":
        assert r0 % tk == 0 and c0 % tn == 0
        a_spec = pl.BlockSpec((tm, tk), lambda i, j, k: (i, k))
        b_spec = pl.BlockSpec((tk, tn), lambda i, j, k: (k + r0 // tk, j + c0 // tn))
        dims = (((1,), (0,)), ((), ()))
    elif mode == "nt":
        assert r0 % tn == 0 and c0 % tk == 0
        a_spec = pl.BlockSpec((tm, tk), lambda i, j, k: (i, k))
        b_spec = pl.BlockSpec((tn, tk), lambda i, j, k: (j + r0 // tn, k + c0 // tk))
        dims = (((1,), (1,)), ((), ()))
    else:
        a_spec = pl.BlockSpec((tk, tm), lambda i, j, k: (k, i))
        b_spec = pl.BlockSpec((tk, tn), lambda i, j, k: (k, j))
        dims = (((0,), (0,)), ((), ()))
    o_spec = pl.BlockSpec((tm, tn), lambda i, j, k: (i, j))

    def body(a_ref, b_ref, *rest):
        o_ref = rest[1] if has_add else rest[0]
        k = pl.program_id(2)
        if keep_a:
            a_bf = rest[-1]

            @pl.when(pl.program_id(1) == 0)
            def _():
                a_bf[...] = a_ref[...].astype(BF16)

            av = a_bf[...]
        else:
            av = a_ref[...].astype(BF16)
        part = lax.dot_general(av, b_ref[...].astype(BF16), dims, preferred_element_type=F32)

        @pl.when(k == 0)
        def _():
            o_ref[...] = part + rest[0][...] if has_add else part

        @pl.when(k > 0)
        def _():
            o_ref[...] += part

    ins = [a, b] + ([add] if has_add else [])
    specs = [a_spec, b_spec] + ([o_spec] if has_add else [])
    return pl.pallas_call(
        body, name=name, grid=(M // tm, N // tn, nk), in_specs=specs, out_specs=o_spec,
        out_shape=jax.ShapeDtypeStruct((M, N), F32),
        scratch_shapes=[pltpu.VMEM(a_spec.block_shape, BF16)] if keep_a else [],
        compiler_params=_params(("parallel", "arbitrary", "arbitrary")),
    )(*ins)


def _is_heads(x):
    return not isinstance(x, tuple)


def _tok_arr(x):
    return x if _is_heads(x) else x[0]


def _tok_width(x):
    return x.shape[1] * x.shape[3] if _is_heads(x) else x[1]


def _heads_spec(h, dh, tm, tiles_per_seq):
    return pl.BlockSpec((None, h, tm, dh), lambda i: (i // tiles_per_seq, 0, i % tiles_per_seq, 0))


def _x_spec(x, tm, tiles_per_seq):
    if _is_heads(x):
        return _heads_spec(x.shape[1], x.shape[3], tm, tiles_per_seq)
    return pl.BlockSpec((tm, x[1]), functools.partial(lambda i, cb: (i, cb), cb=x[2]))


def _out_spec_shape(layout, n, seq, tm):
    if isinstance(layout, tuple):
        h, dh = layout
        return _heads_spec(h, dh, tm, seq // tm), jax.ShapeDtypeStruct((n // seq, h, seq, dh), F32)
    return pl.BlockSpec((tm, layout), lambda i: (i, 0)), jax.ShapeDtypeStruct((n, layout), F32)


def _tok_load(ref):
    if len(ref.shape) == 3:
        return jnp.concatenate([ref[hh] for hh in range(ref.shape[0])], axis=1)
    return ref[...]


def _tok_store(ref, val):
    if len(ref.shape) == 3:
        dh = ref.shape[2]
        for hh in range(ref.shape[0]):
            ref[hh] = val[:, hh * dh:(hh + 1) * dh]
    else:
        ref[...] = val


def _p_specs(ps):
    return [pl.BlockSpec(p.shape, lambda i: (0, 0)) for p in ps]


def tok_fwd(name, f, xs, ps, out_layouts, tm, n, seq):
    nx, npar = len(xs), len(ps)
    outs = [_out_spec_shape(lay, n, seq, tm) for lay in out_layouts]

    def body(*refs):
        xv = [_tok_load(r) for r in refs[:nx]]
        pv = [r[...].astype(F32) for r in refs[nx:nx + npar]]
        for o, r in zip(f(*xv, *pv), refs[nx + npar:]):
            _tok_store(r, o)

    return pl.pallas_call(
        body, name=name, grid=(n // tm,),
        in_specs=[_x_spec(x, tm, seq // tm) for x in xs] + _p_specs(ps),
        out_specs=[o[0] for o in outs], out_shape=[o[1] for o in outs],
        compiler_params=_params(("parallel",)),
    )(*[_tok_arr(x) for x in xs], *ps)


def tok_bwd(name, f, xs, ps, cts, tm, n, seq, dx_layouts=None):
    nx, npar = len(xs), len(ps)
    ct_flat = [c for group in cts for c in group]
    nct = len(ct_flat)
    dx_layouts = dx_layouts or [None] * nx
    dxs = [_out_spec_shape(lay if lay else _tok_width(x), n, seq, tm) for x, lay in zip(xs, dx_layouts)]

    def body(*refs):
        xv = [_tok_load(r) for r in refs[:nx]]
        pv = [r[...].astype(F32) for r in refs[nx:nx + npar]]
        ct_refs = refs[nx + npar:nx + npar + nct]
        dx_refs = refs[nx + npar + nct:nx + npar + nct + nx]
        dp_refs = refs[nx + npar + nct + nx:]
        cv, pos = [], 0
        for group in cts:
            acc = _tok_load(ct_refs[pos])
            for r in ct_refs[pos + 1:pos + len(group)]:
                acc = acc + _tok_load(r)
            cv.append(acc)
            pos += len(group)
        _, vjp = jax.vjp(f, *xv, *pv)
        grads = vjp(tuple(cv))
        for g, r in zip(grads[:nx], dx_refs):
            _tok_store(r, g)

        @pl.when(pl.program_id(0) == 0)
        def _():
            for r in dp_refs:
                r[...] = jnp.zeros_like(r)

        for g, r in zip(grads[nx:], dp_refs):
            r[...] += g

    ct_specs = [_heads_spec(c.shape[1], c.shape[3], tm, seq // tm) if c.ndim == 4
                else pl.BlockSpec((tm, c.shape[1]), lambda i: (i, 0)) for c in ct_flat]
    outs = pl.pallas_call(
        body, name=name, grid=(n // tm,),
        in_specs=[_x_spec(x, tm, seq // tm) for x in xs] + _p_specs(ps) + ct_specs,
        out_specs=[d[0] for d in dxs] + _p_specs(ps),
        out_shape=[d[1] for d in dxs] + [jax.ShapeDtypeStruct(p.shape, F32) for p in ps],
        compiler_params=_params(("arbitrary",)),
    )(*[_tok_arr(x) for x in xs], *ps, *ct_flat)
    return outs[:nx], outs[nx:]


def f_ln(h, y, g, b):
    pre = ALPHA * h + y
    mu = jnp.mean(pre, axis=-1, keepdims=True)
    xc = pre - mu
    var = jnp.mean(xc * xc, axis=-1, keepdims=True)
    return (xc * lax.rsqrt(var + 1e-5) * g + b,)


def _head_sel(width, nheads_pad, per):
    return jnp.where(_iota((width, nheads_pad), 0) // per == _iota((width, nheads_pad), 1), 1.0, 0.0).astype(F32)


def _head_sel_t(nheads_pad, width, per):
    return jnp.where(_iota((nheads_pad, width), 1) // per == _iota((nheads_pad, width), 0), 1.0, 0.0).astype(F32)


@jax.custom_vjp
def head_sum(x):
    return _split3_dot(x, _head_sel(RWKV_DIM, LANES, HEAD_DIM), 0)


@jax.custom_vjp
def head_spread(y):
    return _split3_dot(y, _head_sel(RWKV_DIM, LANES, HEAD_DIM), 1)


head_sum.defvjp(lambda x: (head_sum(x), None), lambda _, ct: (head_spread(ct),))
head_spread.defvjp(lambda y: (head_spread(y), None), lambda _, ct: (head_sum(ct),))


def f_rwkv_pre(r, k, v, lora, glo, w0, w2p, a0, a2p, g2, k_k, k_a):
    lane = _iota(lora.shape, 1)
    tw = jnp.where(lane < 64, jnp.tanh(lora), 0.0)
    ta = jnp.where(lane >= 64, lora, 0.0)
    log_w = -_softplus(-(w0 + mmb(tw, w2p))) - 0.5
    lw = -jnp.exp(log_w)
    a = jax.nn.sigmoid(a0 + mmb(ta, a2p))
    g = mmb(jax.nn.sigmoid(glo), g2)
    kk = k * k_k
    nrm = jnp.sqrt(jnp.maximum(head_sum(kk * kk), 1e-24))
    kkn = kk * head_spread(1.0 / nrm)
    kmod = k * (1.0 + (a - 1.0) * k_a)
    return r, v, lw, kmod, -kkn, kkn * a, g


def f_rwkv_post(y, r, kmod, v, g, ln_g, ln_b, r_k):
    inv = 1.0 / HEAD_DIM
    mu = head_spread(head_sum(y) * inv)
    yc = y - mu
    var = head_sum(yc * yc) * inv
    rstd = head_spread(lax.rsqrt(var + RWKV_GN_EPS))
    yn = yc * rstd * ln_g + ln_b
    bonus = head_spread(head_sum(r * kmod * r_k)) * v
    return ((yn + bonus) * g,)


def f_ssd_post(y, xs, z, d_skip, norm_g):
    sel_t = _head_sel_t(LANES, SSM_DIM, HEAD_DIM)
    d_e = jnp.sum(mmf(jnp.broadcast_to(d_skip, (SUBLANES, LANES)), sel_t), axis=0, keepdims=True) * (1.0 / SUBLANES)
    u = (y + xs * d_e) * _silu(z)
    first = _iota(u.shape, 1) < (SSM_DIM // 2)
    uu = u * u
    inv = 2.0 / SSM_DIM
    ms0 = jnp.sum(jnp.where(first, uu, 0.0), axis=-1, keepdims=True) * inv
    ms1 = jnp.sum(jnp.where(first, 0.0, uu), axis=-1, keepdims=True) * inv
    ms = jnp.where(first, ms0, ms1)
    return (u * lax.rsqrt(ms + 1e-5) * norm_g,)


def f_mla_pre(cq, ckv, qg, wq_nope, wq_rope, kvg, wk_nope, wv):
    def rms(x, g):
        return x * lax.rsqrt(jnp.mean(x * x, axis=-1, keepdims=True) + 1e-6) * g
    q_in, kv_in = rms(cq, qg), rms(ckv, kvg)
    return mmb(q_in, wq_nope), mmb(q_in, wq_rope), mmb(kv_in, wk_nope), mmb(kv_in, wv)


def f_same(*xs):
    return xs


def f_concat(*xs):
    return (jnp.concatenate(xs, axis=1),)


def _shift_down(x, s, row):
    return x if s == 0 else jnp.where(row >= s, pltpu.roll(x, s, 0), 0.0)


def _shift_up(x, s, row, t):
    return x if s == 0 else jnp.where(row < t - s, pltpu.roll(x, t - s, 0), 0.0)


def dwconv_fwd(name, u, colmap, w, b, taps, silu, upmap=None):
    bsz, t, _ = u.shape
    c = w.shape[1]
    tc = LANES
    has_up = upmap is not None

    def body(*refs):
        u_ref, w_ref, b_ref = refs[:3]
        o_ref = refs[-1]
        uv = u_ref[...]
        wv = w_ref[...]
        row = _iota(uv.shape, 0)
        acc = jnp.broadcast_to(b_ref[...], uv.shape)
        for i in range(taps):
            acc = acc + wv[i:i + 1, :] * _shift_down(uv, taps - 1 - i, row)
        if silu:
            acc = _silu(acc)
        if has_up:
            acc = acc * refs[3][...]
        o_ref[...] = acc

    specs = [pl.BlockSpec((None, t, tc), lambda bb, j: (bb, 0, colmap(j))),
             pl.BlockSpec((taps, tc), lambda bb, j: (0, j)),
             pl.BlockSpec((1, tc), lambda bb, j: (0, j))]
    ins = [u, w, b]
    if has_up:
        specs.append(pl.BlockSpec((None, t, tc), lambda bb, j: (bb, 0, upmap(j))))
        ins.append(u)
    return pl.pallas_call(
        body, name=name, grid=(bsz, c // tc), in_specs=specs,
        out_specs=pl.BlockSpec((None, t, tc), lambda bb, j: (bb, 0, j)),
        out_shape=jax.ShapeDtypeStruct((bsz, t, c), F32),
        compiler_params=_params(("parallel", "parallel")),
    )(*ins)


def dwconv_bwd(name, u, colmap, w, b, taps, silu, dout, upmap=None):
    bsz, t, _ = u.shape
    c = w.shape[1]
    tc = LANES
    has_up = upmap is not None

    def body(*refs):
        u_ref, w_ref, b_ref, d_ref = refs[:4]
        nin = 5 if has_up else 4
        du_ref, dw_ref, db_ref = refs[nin:nin + 3]
        uv = u_ref[...]
        wv = w_ref[...]
        dv = d_ref[...]
        row = _iota(uv.shape, 0)
        shifted = [_shift_down(uv, taps - 1 - i, row) for i in range(taps)]
        cg = jnp.broadcast_to(b_ref[...], uv.shape)
        for i in range(taps):
            cg = cg + wv[i:i + 1, :] * shifted[i]
        if silu:
            sg = jax.nn.sigmoid(cg)
            act = cg * sg
            dact_dcg = sg * (1.0 + cg * (1.0 - sg))
        else:
            act = cg
            dact_dcg = None
        if has_up:
            refs[nin + 3][...] = dv * act
            dv = dv * refs[4][...]
        dcg = dv * dact_dcg if silu else dv
        du = jnp.zeros_like(uv)
        for i in range(taps):
            du = du + wv[i:i + 1, :] * _shift_up(dcg, taps - 1 - i, row, t)
        du_ref[...] = du

        @pl.when(pl.program_id(1) == 0)
        def _():
            dw_ref[...] = jnp.zeros_like(dw_ref)
            db_ref[...] = jnp.zeros_like(db_ref)

        for i in range(taps):
            dw_ref[i:i + 1, :] += jnp.sum(dcg * shifted[i], axis=0, keepdims=True)
        db_ref[...] += jnp.sum(dcg, axis=0, keepdims=True)

    specs = [pl.BlockSpec((None, t, tc), lambda j, bb: (bb, 0, colmap(j))),
             pl.BlockSpec((taps, tc), lambda j, bb: (0, j)),
             pl.BlockSpec((1, tc), lambda j, bb: (0, j)),
             pl.BlockSpec((None, t, tc), lambda j, bb: (bb, 0, j))]
    ins = [u, w, b, dout]
    if has_up:
        specs.append(pl.BlockSpec((None, t, tc), lambda j, bb: (bb, 0, upmap(j))))
        ins.append(u)
    big = pl.BlockSpec((None, t, tc), lambda j, bb: (bb, 0, j))
    out_specs = [big, pl.BlockSpec((taps, tc), lambda j, bb: (0, j)), pl.BlockSpec((1, tc), lambda j, bb: (0, j))]
    out_shape = [jax.ShapeDtypeStruct((bsz, t, c), F32), jax.ShapeDtypeStruct((taps, c), F32),
                 jax.ShapeDtypeStruct((1, c), F32)]
    if has_up:
        out_specs.append(big)
        out_shape.append(jax.ShapeDtypeStruct((bsz, t, c), F32))
    return pl.pallas_call(
        body, name=name, grid=(c // tc, bsz), in_specs=specs, out_specs=out_specs, out_shape=out_shape,
        compiler_params=_params(("parallel", "arbitrary")),
    )(*ins)


def _each(f, *lists):
    return [f(*xs) for xs in zip(*lists)]


def rwkv_chunk(s0, r, lw, k, v, al, be):
    c = r[0].shape[0]
    ii, jj = _iota((c, c), 0), _iota((c, c), 1)
    incl, strict = ii >= jj, ii > jj
    ones_incl = jnp.where(incl, 1.0, 0.0)
    eye = jnp.where(ii == jj, 1.0, 0.0)
    cum = _each(lambda x: mmf(ones_incl, x), lw)
    gam_inv = _each(lambda x: jnp.exp(-x), cum)
    at = _each(lambda a_, c_, l_: a_ * jnp.exp(c_ - l_), al, cum, lw)
    rt = _each(lambda r_, c_: r_ * jnp.exp(c_), r, cum)
    bt = _each(lambda b_, g_: b_ * g_, be, gam_inv)
    kt = _each(lambda k_, g_: k_ * g_, k, gam_inv)
    a_b = _each(lambda x, y_: jnp.where(strict, mm3_nt(x, y_), 0.0), at, bt)
    a_k = _each(lambda x, y_: jnp.where(strict, mm3_nt(x, y_), 0.0), at, kt)
    rhs0 = _each(mm3_nt, at, s0)
    rhs = _each(lambda x, a_, v_: x + mm3(a_, v_), rhs0, a_k, v)
    p = _each(lambda x: eye + x, a_b)
    m = a_b
    for _ in range(int(math.log2(c)) - 1):
        m = _each(mm3, m, m)
        p = _each(lambda p_, m_: p_ + mm3(p_, m_), p, m)
    u = _each(mm3, p, rhs)
    r_b = _each(lambda x, y_: jnp.where(incl, mm3_nt(x, y_), 0.0), rt, bt)
    r_k = _each(lambda x, y_: jnp.where(incl, mm3_nt(x, y_), 0.0), rt, kt)
    y0 = _each(mm3_nt, rt, s0)
    y1 = _each(lambda y_, b_, u_: y_ + mm3(b_, u_), y0, r_b, u)
    y = _each(lambda y_, k_, v_: y_ + mm3(k_, v_), y1, r_k, v)
    su = _each(mm3_tn, u, bt)
    sv = _each(mm3_tn, v, kt)
    s1 = _each(lambda s_, a_, b_, l_: (s_ + a_ + b_) * jnp.exp(jnp.sum(l_, axis=0, keepdims=True)), s0, su, sv, lw)
    return y, s1


def rwkv_scan_fwd(r, lw, k, v, al, be, ride=None):
    bsz, h, t, d = r.shape
    c = RWKV_CHUNK
    nc = t // c
    grid = (bsz, nc)
    r_in, r_specs, r_out, r_ospecs, r_scr = _ride_args(ride)

    def body(*refs):
        r_ref, lw_ref, k_ref, v_ref, al_ref, be_ref = refs[:6]
        y_ref, st_ref = refs[6 + len(r_in):8 + len(r_in)]
        s_scr = refs[8 + 2 * len(r_in)]
        if ride is not None:
            first, last = _grid_first_last(grid)
            copies = _ride_start((refs[6], refs[8 + len(r_in)], *refs[-3:]), ride[1], first)

        @pl.when(pl.program_id(1) == 0)
        def _():
            s_scr[...] = jnp.zeros_like(s_scr)

        heads = lambda ref: [ref[hh] for hh in range(h)]
        s0 = heads(s_scr)
        y, s1 = rwkv_chunk(s0, heads(r_ref), heads(lw_ref), heads(k_ref), heads(v_ref), heads(al_ref),
                           heads(be_ref))
        for hh in range(h):
            st_ref[hh] = s0[hh]
            y_ref[hh] = y[hh]
            s_scr[hh] = s1[hh]
        if ride is not None:
            _ride_wait(copies, last)

    seq = pl.BlockSpec((None, h, c, d), lambda b, i: (b, 0, i, 0))
    return pl.pallas_call(
        body, name="rwkv_scan_fwd", grid=grid, in_specs=[seq] * 6 + r_specs,
        out_specs=[seq, pl.BlockSpec((None, h, None, d, d), lambda b, i: (b, 0, i, 0, 0))] + r_ospecs,
        out_shape=[jax.ShapeDtypeStruct((bsz, h, t, d), F32), jax.ShapeDtypeStruct((bsz, h, nc, d, d), F32)] + r_out,
        scratch_shapes=[pltpu.VMEM((h, d, d), F32)] + r_scr,
        compiler_params=_params(("arbitrary", "arbitrary")),
    )(r, lw, k, v, al, be, *r_in)


def rwkv_scan_bwd(r, lw, k, v, al, be, states, dy, ride=None):
    bsz, h, t, d = r.shape
    c = RWKV_CHUNK
    nc = t // c
    grid = (bsz, nc)
    r_in, r_specs, r_out, r_ospecs, r_scr = _ride_args(ride)

    def body(*refs):
        r_ref, lw_ref, k_ref, v_ref, al_ref, be_ref, st_ref, dy_ref = refs[:8]
        nin = 8 + len(r_in)
        dr_ref, dlw_ref, dk_ref, dv_ref, dal_ref, dbe_ref = refs[nin:nin + 6]
        ds_scr = refs[nin + 6 + len(r_in)]
        if ride is not None:
            first, last = _grid_first_last(grid)
            copies = _ride_start((refs[8], refs[nin + 6], *refs[-3:]), ride[1], first)

        @pl.when(pl.program_id(1) == 0)
        def _():
            ds_scr[...] = jnp.zeros_like(ds_scr)

        heads = lambda ref: [ref[hh] for hh in range(h)]
        _, vjp = jax.vjp(rwkv_chunk, heads(st_ref), heads(r_ref), heads(lw_ref), heads(k_ref), heads(v_ref),
                         heads(al_ref), heads(be_ref))
        grads = vjp((heads(dy_ref), heads(ds_scr)))
        for ref, gl in zip((ds_scr, dr_ref, dlw_ref, dk_ref, dv_ref, dal_ref, dbe_ref), grads):
            for hh in range(h):
                ref[hh] = gl[hh]
        if ride is not None:
            _ride_wait(copies, last)

    seq = pl.BlockSpec((None, h, c, d), lambda b, i: (b, 0, nc - 1 - i, 0))
    st = pl.BlockSpec((None, h, None, d, d), lambda b, i: (b, 0, nc - 1 - i, 0, 0))
    return pl.pallas_call(
        body, name="rwkv_scan_bwd", grid=grid, in_specs=[seq] * 6 + [st, seq] + r_specs,
        out_specs=[seq] * 6 + r_ospecs, out_shape=[jax.ShapeDtypeStruct((bsz, h, t, d), F32)] * 6 + r_out,
        scratch_shapes=[pltpu.VMEM((h, d, d), F32)] + r_scr,
        compiler_params=_params(("arbitrary", "arbitrary")),
    )(r, lw, k, v, al, be, states, dy, *r_in)


def ssd_chunk(st, xs, bm, cm, dtr, dt_bias, a_log):
    n = SSM_CHUNK
    ii, jj = _iota((n, n), 0), _iota((n, n), 1)
    incl = ii >= jj
    lane = _iota((n, LANES), 1)
    dt = _softplus(dtr + dt_bias)
    a = dt * (-jnp.exp(a_log))
    acum = mmf(jnp.where(incl, 1.0, 0.0), a)
    last_row = jnp.where(jj == n - 1, 1.0, 0.0)
    cb = [mmf_nt(cm[g], bm[g]) for g in range(2)]
    pairs, heads = range(4), range(SSM_HEADS)
    e_m = [jnp.where(_iota((LANES, LANES), 0) == 2 * m + _iota((LANES, LANES), 1) // HEAD_DIM, 1.0, 0.0)
           for m in pairs]
    dt_m = [mmf(dt, e_m[m]) for m in pairs]
    ac_m = [mmf(acum, e_m[m]) for m in pairs]
    x = [xs[m] * dt_m[m] for m in pairs]
    last_m = [mmf(last_row, ac_m[m]) for m in pairs]
    colb = [mmf(acum, jnp.where(_iota((LANES, n), 0) == h, 1.0, 0.0)) for h in heads]
    decay = [jnp.exp(jnp.where(incl, colb[h] - colb[h].T, NEG_BIG)) for h in heads]
    yh = [mmf(cb[h // 4] * decay[h], x[h // 2]) for h in heads]
    y_off = [mmf(cm[m // 2], st[m]) for m in pairs]
    ys = [jnp.where(lane // HEAD_DIM == 0, yh[2 * m], yh[2 * m + 1]) + jnp.exp(ac_m[m]) * y_off[m] for m in pairs]
    st_in = [mmf_tn(bm[m // 2], x[m] * jnp.exp(last_m[m] - ac_m[m])) for m in pairs]
    st_new = [jnp.exp(last_m[m]) * st[m] + st_in[m] for m in pairs]
    return tuple(ys), tuple(st_new)


def _ssd_load(xbc_ref, dtr_ref):
    xs = tuple(xbc_ref[:, m * LANES:(m + 1) * LANES] for m in range(4))
    bm = tuple(xbc_ref[:, SSM_DIM + g * LANES:SSM_DIM + (g + 1) * LANES] for g in range(2))
    cm = tuple(xbc_ref[:, SSM_DIM + 2 * LANES + g * LANES:SSM_DIM + 2 * LANES + (g + 1) * LANES] for g in range(2))
    return xs, bm, cm, dtr_ref[...]


def ssd_fwd(xbc, proj, dt_col, dt_bias, a_log):
    bsz, t, _ = xbc.shape
    n = SSM_CHUNK
    nc = t // n

    def body(xbc_ref, dtr_ref, dtb_ref, al_ref, y_ref, st_ref, s_scr):
        @pl.when(pl.program_id(1) == 0)
        def _():
            s_scr[...] = jnp.zeros_like(s_scr)

        st = tuple(s_scr[m] for m in range(4))
        for m in range(4):
            st_ref[m] = st[m]
        xs, bm, cm, dtr = _ssd_load(xbc_ref, dtr_ref)
        ys, st_new = ssd_chunk(st, xs, bm, cm, dtr, dtb_ref[...], al_ref[...])
        for m in range(4):
            y_ref[:, m * LANES:(m + 1) * LANES] = ys[m]
            s_scr[m] = st_new[m]

    vec = pl.BlockSpec((1, LANES), lambda b, i: (0, 0))
    return pl.pallas_call(
        body, name="ssd_fwd", grid=(bsz, nc),
        in_specs=[pl.BlockSpec((None, n, 2 * SSM_DIM), lambda b, i: (b, i, 0)),
                  pl.BlockSpec((None, n, LANES), lambda b, i: (b, i, dt_col)), vec, vec],
        out_specs=[pl.BlockSpec((None, n, SSM_DIM), lambda b, i: (b, i, 0)),
                   pl.BlockSpec((None, None, 4, SSM_STATE, LANES), lambda b, i: (b, i, 0, 0, 0))],
        out_shape=[jax.ShapeDtypeStruct((bsz, t, SSM_DIM), F32),
                   jax.ShapeDtypeStruct((bsz, nc, 4, SSM_STATE, LANES), F32)],
        scratch_shapes=[pltpu.VMEM((4, SSM_STATE, LANES), F32)],
        compiler_params=_params(("parallel", "arbitrary")),
    )(xbc, proj, dt_bias, a_log)


def ssd_bwd(xbc, proj, dt_col, dt_bias, a_log, states, dy, dxs_extra):
    bsz, t, _ = xbc.shape
    n = SSM_CHUNK
    nc = t // n

    def body(xbc_ref, dtr_ref, dtb_ref, al_ref, st_ref, dy_ref, ex_ref,
             dxbc_ref, ddtr_ref, ddtb_ref, dal_ref, ds_scr):
        first = jnp.logical_and(pl.program_id(0) == 0, pl.program_id(1) == 0)

        @pl.when(pl.program_id(1) == 0)
        def _():
            ds_scr[...] = jnp.zeros_like(ds_scr)

        @pl.when(first)
        def _():
            ddtb_ref[...] = jnp.zeros_like(ddtb_ref)
            dal_ref[...] = jnp.zeros_like(dal_ref)

        st = tuple(st_ref[m] for m in range(4))
        xs, bm, cm, dtr = _ssd_load(xbc_ref, dtr_ref)
        _, vjp = jax.vjp(ssd_chunk, st, xs, bm, cm, dtr, dtb_ref[...], al_ref[...])
        dys = tuple(dy_ref[:, m * LANES:(m + 1) * LANES] for m in range(4))
        dst_in = tuple(ds_scr[m] for m in range(4))
        dst, dxs, dbm, dcm, ddtr, ddtb, dal = vjp((dys, dst_in))
        for m in range(4):
            ds_scr[m] = dst[m]
            sl = slice(m * LANES, (m + 1) * LANES)
            dxbc_ref[:, sl] = dxs[m] + ex_ref[:, sl]
        for g in range(2):
            dxbc_ref[:, SSM_DIM + g * LANES:SSM_DIM + (g + 1) * LANES] = dbm[g]
            dxbc_ref[:, SSM_DIM + 2 * LANES + g * LANES:SSM_DIM + 2 * LANES + (g + 1) * LANES] = dcm[g]
        ddtr_ref[...] = ddtr
        ddtb_ref[...] += ddtb
        dal_ref[...] += dal

    vec = pl.BlockSpec((1, LANES), lambda b, i: (0, 0))
    rev = lambda b, i: (b, nc - 1 - i, 0)
    return pl.pallas_call(
        body, name="ssd_bwd", grid=(bsz, nc),
        in_specs=[pl.BlockSpec((None, n, 2 * SSM_DIM), rev),
                  pl.BlockSpec((None, n, LANES), lambda b, i: (b, nc - 1 - i, dt_col)), vec, vec,
                  pl.BlockSpec((None, None, 4, SSM_STATE, LANES), lambda b, i: (b, nc - 1 - i, 0, 0, 0)),
                  pl.BlockSpec((None, n, SSM_DIM), rev), pl.BlockSpec((None, n, SSM_DIM), rev)],
        out_specs=[pl.BlockSpec((None, n, 2 * SSM_DIM), rev), pl.BlockSpec((None, n, LANES), rev), vec, vec],
        out_shape=[jax.ShapeDtypeStruct((bsz, t, 2 * SSM_DIM), F32), jax.ShapeDtypeStruct((bsz, t, LANES), F32),
                   jax.ShapeDtypeStruct((1, LANES), F32), jax.ShapeDtypeStruct((1, LANES), F32)],
        scratch_shapes=[pltpu.VMEM((4, SSM_STATE, LANES), F32)],
        compiler_params=_params(("arbitrary", "arbitrary")),
    )(xbc, proj, dt_bias, a_log, states, dy, dxs_extra)


def sb_block(q, kj, vj, carry, maskf):
    mask = maskf > 0.5
    z = _each(lambda q_, k_: dg(q_, k_, 1, 1, P_BF16) * (HEAD_DIM ** -0.5), q, kj)
    lk = _each(lambda z_: jnp.where(mask, -_softplus(z_), 0.0), z)
    sfx = _each(suffix_sum, lk)
    att = _each(lambda z_, c_, s_: jnp.exp(jnp.where(mask, z_ + c_ + s_, NEG_BIG)), z, carry, sfx)
    out = _each(mmb, att, vj)
    return out, _each(lambda c_, k_: c_ + jnp.sum(k_, axis=1, keepdims=True), carry, lk)


def _sb_mask(qi, j):
    n = Q_BLOCK
    return jnp.where(j * n + _iota((n, n), 1) < qi * n + _iota((n, n), 0), 1.0, 0.0)


def sb_fwd(q, k, v, ride=None):
    bsz, h, t, d = q.shape
    n = Q_BLOCK
    hp = ATTN_FWD_HEADS_PER_STEP
    grid = (bsz, h // hp, t // n)
    r_in, r_specs, r_out, r_ospecs, r_scr = _ride_args(ride)

    def body(*refs):
        q_ref, k_ref, v_ref = refs[:3]
        o_ref, c_ref = refs[3 + len(r_in):5 + len(r_in)]
        if ride is not None:
            first, last = _grid_first_last(grid)
            copies = _ride_start((refs[3], refs[5 + len(r_in)], *refs[-3:]), ride[1], first)
        qi = pl.program_id(2)
        lane = _iota((n, LANES), 1)

        c_ref[...] = jnp.zeros_like(c_ref)

        def step(i, state):
            acc, carry = state
            j = qi - i
            rows = pl.ds(pl.multiple_of(j * n, n), n)
            for hh in range(hp):
                c_ref[hh] = jnp.where(lane == j, carry[hh], c_ref[hh])
            o, carry = sb_block([q_ref[hh] for hh in range(hp)], [k_ref[hh, rows, :] for hh in range(hp)],
                                [v_ref[hh, rows, :] for hh in range(hp)], carry, _sb_mask(qi, j))
            return [a_ + o_ for a_, o_ in zip(acc, o)], carry

        init = ([jnp.zeros((n, d), F32) for _ in range(hp)], [jnp.zeros((n, 1), F32) for _ in range(hp)])
        acc, _ = lax.fori_loop(0, qi + 1, step, init)
        for hh in range(hp):
            o_ref[hh] = acc[hh]
        if ride is not None:
            _ride_wait(copies, last)

    blk = pl.BlockSpec((None, hp, n, d), lambda b, hg, i: (b, hg, i, 0))
    cblk = pl.BlockSpec((None, hp, n, LANES), lambda b, hg, i: (b, hg, i, 0))
    full = pl.BlockSpec((None, hp, t, d), lambda b, hg, i: (b, hg, 0, 0))
    return pl.pallas_call(
        body, name="sb_fwd", grid=grid, in_specs=[blk, full, full] + r_specs, out_specs=[blk, cblk] + r_ospecs,
        out_shape=[jax.ShapeDtypeStruct((bsz, h, t, d), F32), jax.ShapeDtypeStruct((bsz, h, t, LANES), F32)] + r_out,
        scratch_shapes=r_scr, compiler_params=_params(("arbitrary", "arbitrary", "arbitrary")),
    )(q, k, v, *r_in)


def sb_bwd(q, k, v, kept, do, ride=None):
    bsz, h, t, d = q.shape
    n = Q_BLOCK
    hp = SB_HEADS_PER_STEP
    grid = (bsz, h // hp, t // n)
    r_in, r_specs, r_out, r_ospecs, r_scr = _ride_args(ride)

    def body(*refs):
        q_ref, k_ref, v_ref, c_ref, do_ref = refs[:5]
        nin = 5 + len(r_in)
        dq_ref, dk_ref, dv_ref = refs[nin:nin + 3]
        if ride is not None:
            first, last = _grid_first_last(grid)
            copies = _ride_start((refs[5], refs[nin + 3], *refs[-3:]), ride[1], first)
        qi = pl.program_id(2)

        @pl.when(qi == 0)
        def _():
            dk_ref[...] = jnp.zeros_like(dk_ref)
            dv_ref[...] = jnp.zeros_like(dv_ref)

        heads = range(hp)
        qv = [q_ref[hh] for hh in heads]
        kept_v = [c_ref[hh] for hh in heads]
        lane = _iota((n, LANES), 1)

        def bwd_step(j, state):
            dq, dcarry = state
            rows = pl.ds(pl.multiple_of(j * n, n), n)
            carry_in = [jnp.sum(jnp.where(lane == j, t_, 0.0), axis=1, keepdims=True) for t_ in kept_v]
            _, vjp = jax.vjp(sb_block, qv, [k_ref[hh, rows, :] for hh in heads],
                             [v_ref[hh, rows, :] for hh in heads], carry_in, _sb_mask(qi, j))
            dqj, dkj, dvj, dc, _ = vjp(([do_ref[hh] for hh in heads], dcarry))
            for hh in heads:
                dk_ref[hh, rows, :] += dkj[hh]
                dv_ref[hh, rows, :] += dvj[hh]
            return [a_ + b_ for a_, b_ in zip(dq, dqj)], dc

        init = ([jnp.zeros((n, d), F32) for _ in heads], [jnp.zeros((n, 1), F32) for _ in heads])
        dq, _ = lax.fori_loop(0, qi + 1, bwd_step, init)
        for hh in heads:
            dq_ref[hh] = dq[hh]
        if ride is not None:
            _ride_wait(copies, last)

    blk = pl.BlockSpec((None, hp, n, d), lambda b, hg, i: (b, hg, i, 0))
    cblk = pl.BlockSpec((None, hp, n, LANES), lambda b, hg, i: (b, hg, i, 0))
    full = pl.BlockSpec((None, hp, t, d), lambda b, hg, i: (b, hg, 0, 0))
    shp = jax.ShapeDtypeStruct((bsz, h, t, d), F32)
    return pl.pallas_call(
        body, name="sb_bwd", grid=grid, in_specs=[blk, full, full, cblk, blk] + r_specs,
        out_specs=[blk, full, full] + r_ospecs, out_shape=[shp, shp, shp] + r_out,
        scratch_shapes=r_scr, compiler_params=_params(("arbitrary", "arbitrary", "arbitrary")),
    )(q, k, v, kept, do, *r_in)


def _bdot(a, b, ca, cb):
    return _dg_raw(a, b, ca, cb, P_BF16)


def _mla_scores(qn, qp, knj, kpj, qi, j):
    n = Q_BLOCK
    mask = j * n + _iota((n, n), 1) <= qi * n + _iota((n, n), 0)
    scale = (MLA_NOPE + MLA_ROPE) ** -0.5
    return _each(lambda a_, b_, k_: jnp.where(mask, (_bdot(a_, k_, 1, 1) + _bdot(b_, kpj, 1, 1)) * scale, NEG_BIG),
                 qn, qp, knj)


def _mla_specs(t, hp):
    n = Q_BLOCK
    return (pl.BlockSpec((None, hp, n, MLA_NOPE), lambda b, hg, i: (b, hg, i, 0)),
            pl.BlockSpec((None, hp, n, MLA_ROPE), lambda b, hg, i: (b, hg, i, 0)),
            pl.BlockSpec((None, hp, t, MLA_NOPE), lambda b, hg, i: (b, hg, 0, 0)),
            pl.BlockSpec((None, None, t, MLA_ROPE), lambda b, hg, i: (b, 0, 0, 0)),
            pl.BlockSpec((None, hp, n, 1), lambda b, hg, i: (b, hg, i, 0)))


def mla_fwd(qn, qp, kn, kp, v):
    bsz, h, t, _ = qn.shape
    n, hp = Q_BLOCK, ATTN_FWD_HEADS_PER_STEP
    heads = range(hp)

    def body(qn_ref, qp_ref, kn_ref, kp_ref, v_ref, o_ref, lse_ref):
        qi = pl.program_id(2)
        qn_v, qp_v = [qn_ref[hh] for hh in heads], [qp_ref[hh] for hh in heads]

        def step(j, state):
            m, l, acc = state
            rows = pl.ds(pl.multiple_of(j * n, n), n)
            s = _mla_scores(qn_v, qp_v, [kn_ref[hh, rows, :] for hh in heads], kp_ref[rows, :], qi, j)
            m_new = _each(lambda m_, s_: jnp.maximum(m_, jnp.max(s_, axis=1, keepdims=True)), m, s)
            p = _each(lambda s_, m_: jnp.exp(s_ - m_), s, m_new)
            corr = _each(lambda a_, b_: jnp.exp(a_ - b_), m, m_new)
            l = _each(lambda l_, c_, p_: l_ * c_ + jnp.sum(p_, axis=1, keepdims=True), l, corr, p)
            pv = _each(lambda p_, v_: _bdot(p_, v_, 1, 0), p, [v_ref[hh, rows, :] for hh in heads])
            acc = _each(lambda a_, c_, x_: a_ * c_ + x_, acc, corr, pv)
            return m_new, l, acc

        init = ([jnp.full((n, 1), NEG_BIG, F32) for _ in heads], [jnp.zeros((n, 1), F32) for _ in heads],
                [jnp.zeros((n, MLA_NOPE), F32) for _ in heads])
        m, l, acc = lax.fori_loop(0, qi + 1, step, init)
        for hh in heads:
            o_ref[hh] = acc[hh] / l[hh]
            lse_ref[hh] = m[hh] + jnp.log(l[hh])

    qn_s, qp_s, kn_s, kp_s, row_s = _mla_specs(t, hp)
    return pl.pallas_call(
        body, name="mla_fwd", grid=(bsz, h // hp, t // n), in_specs=[qn_s, qp_s, kn_s, kp_s, kn_s],
        out_specs=[qn_s, row_s],
        out_shape=[jax.ShapeDtypeStruct(qn.shape, F32), jax.ShapeDtypeStruct((bsz, h, t, 1), F32)],
        compiler_params=_params(("parallel", "parallel", "arbitrary")),
    )(qn, qp, kn, kp, v)


def mla_bwd(qn, qp, kn, kp, v, o, lse, do):
    bsz, h, t, _ = qn.shape
    n, hp = Q_BLOCK, MLA_HEADS_PER_STEP
    heads = range(hp)
    scale = (MLA_NOPE + MLA_ROPE) ** -0.5

    def body(qn_ref, qp_ref, kn_ref, kp_ref, v_ref, o_ref, lse_ref, do_ref,
             dqn_ref, dqp_ref, dkn_ref, dkp_ref, dv_ref):
        hg, qi = pl.program_id(1), pl.program_id(2)

        @pl.when(qi == 0)
        def _():
            dkn_ref[...] = jnp.zeros_like(dkn_ref)
            dv_ref[...] = jnp.zeros_like(dv_ref)

        @pl.when(jnp.logical_and(qi == 0, hg == 0))
        def _():
            dkp_ref[...] = jnp.zeros_like(dkp_ref)

        qn_v, qp_v = [qn_ref[hh] for hh in heads], [qp_ref[hh] for hh in heads]
        do_v, lse_v = [do_ref[hh] for hh in heads], [lse_ref[hh] for hh in heads]
        dsum = [jnp.sum(do_v[hh] * o_ref[hh], axis=1, keepdims=True) for hh in heads]

        def step(j, state):
            dqn, dqp = state
            rows = pl.ds(pl.multiple_of(j * n, n), n)
            knj, vj, kpj = [kn_ref[hh, rows, :] for hh in heads], [v_ref[hh, rows, :] for hh in heads], kp_ref[rows, :]
            s = _mla_scores(qn_v, qp_v, knj, kpj, qi, j)
            p = _each(lambda s_, l_: jnp.exp(s_ - l_), s, lse_v)
            dp = _each(lambda d_, v_: _bdot(d_, v_, 1, 1), do_v, vj)
            ds = _each(lambda p_, dp_, d_: p_ * (dp_ - d_) * scale, p, dp, dsum)
            dqn = _each(lambda a_, ds_, k_: a_ + _bdot(ds_, k_, 1, 0), dqn, ds, knj)
            dqp = _each(lambda a_, ds_: a_ + _bdot(ds_, kpj, 1, 0), dqp, ds)
            dkn = _each(lambda ds_, q_: _bdot(ds_, q_, 0, 0), ds, qn_v)
            dv = _each(lambda p_, d_: _bdot(p_, d_, 0, 0), p, do_v)
            dkp = _each(lambda ds_, q_: _bdot(ds_, q_, 0, 0), ds, qp_v)
            for hh in heads:
                dkn_ref[hh, rows, :] += dkn[hh]
                dv_ref[hh, rows, :] += dv[hh]
            dkp_ref[rows, :] += functools.reduce(lambda a_, b_: a_ + b_, dkp)
            return dqn, dqp

        init = ([jnp.zeros((n, MLA_NOPE), F32) for _ in heads], [jnp.zeros((n, MLA_ROPE), F32) for _ in heads])
        dqn, dqp = lax.fori_loop(0, qi + 1, step, init)
        for hh in heads:
            dqn_ref[hh] = dqn[hh]
            dqp_ref[hh] = dqp[hh]

    qn_s, qp_s, kn_s, kp_s, row_s = _mla_specs(t, hp)
    return pl.pallas_call(
        body, name="mla_bwd", grid=(bsz, h // hp, t // n),
        in_specs=[qn_s, qp_s, kn_s, kp_s, kn_s, qn_s, row_s, qn_s],
        out_specs=[qn_s, qp_s, kn_s, kp_s, kn_s],
        out_shape=[jax.ShapeDtypeStruct(qn.shape, F32), jax.ShapeDtypeStruct(qp.shape, F32),
                   jax.ShapeDtypeStruct(kn.shape, F32), jax.ShapeDtypeStruct(kp.shape, F32),
                   jax.ShapeDtypeStruct(v.shape, F32)],
        compiler_params=_params(("parallel", "arbitrary", "arbitrary")),
    )(qn, qp, kn, kp, v, o, lse, do)


def rope(name, x, pos, inv_freq, sign):
    bsz, hx, t, d = x.shape
    half = d // 2

    tt = _largest_tile(t, 512, SUBLANES)

    def body(x_ref, pos_ref, f_ref, o_ref):
        ang = pos_ref[...].astype(F32) * f_ref[...]
        cos, sin = jnp.cos(ang), sign * jnp.sin(ang)
        ri, ci = _iota((d, d), 0), _iota((d, d), 1)
        rot = jnp.where(ri == ci + half, -1.0, 0.0) + jnp.where(ri + half == ci, 1.0, 0.0)
        for hh in range(hx):
            xv = x_ref[hh]
            o_ref[hh] = xv * cos + mmf(xv, rot) * sin

    blk = pl.BlockSpec((None, hx, tt, d), lambda b, i: (b, 0, i, 0))
    return pl.pallas_call(
        body, name=name, grid=(bsz, t // tt),
        in_specs=[blk, pl.BlockSpec((None, tt, 1), lambda b, i: (b, i, 0)), pl.BlockSpec((1, d), lambda b, i: (0, 0))],
        out_specs=blk, out_shape=jax.ShapeDtypeStruct(x.shape, F32),
        compiler_params=_params(("parallel", "parallel")),
    )(x, pos, inv_freq)


def loss_head(h, target, tm):
    n, d = h.shape

    def body(h_ref, t_ref, dh_ref, l_ref):
        @pl.when(pl.program_id(0) == 0)
        def _():
            l_ref[...] = jnp.zeros_like(l_ref)

        e = h_ref[...] - t_ref[...]
        dh_ref[...] = e * (1.0 / d)
        l_ref[...] += jnp.sum(e * e, axis=(0, 1), keepdims=True) * (0.5 / d)

    row = pl.BlockSpec((tm, d), lambda i: (i, 0))
    dh, l = pl.pallas_call(
        body, name="loss_head", grid=(n // tm,), in_specs=[row, row],
        out_specs=[row, pl.BlockSpec((SUBLANES, LANES), lambda i: (0, 0))],
        out_shape=[jax.ShapeDtypeStruct((n, d), F32), jax.ShapeDtypeStruct((SUBLANES, LANES), F32)],
        compiler_params=_params(("arbitrary",)),
    )(h, target)
    return dh, l[0, 0]


def _exchange_copies(src_ref, out_ref, send_sems, recv_sems, local_sem, gather):
    x, y, c = lax.axis_index("x"), lax.axis_index("y"), lax.axis_index("c")
    me = 4 * x + 2 * y + c
    copies = [pltpu.make_async_copy(src_ref if gather else src_ref.at[me], out_ref.at[me], local_sem)]
    for m in range(1, N_DEV):
        px, py, pc = x ^ (m >> 2), y ^ ((m >> 1) & 1), c ^ (m & 1)
        peer = 4 * px + 2 * py + pc
        copies.append(pltpu.make_async_remote_copy(
            src_ref=src_ref if gather else src_ref.at[peer], dst_ref=out_ref.at[me],
            send_sem=send_sems.at[m], recv_sem=recv_sems.at[m],
            device_id=(px, py, pc), device_id_type=pl.DeviceIdType.MESH))
    return copies


def _exchange_start(copies):
    for cp in copies:
        cp.start()


def _exchange_wait(copies):
    for cp in copies[1:]:
        cp.wait_recv()
    for cp in copies[1:]:
        cp.wait_send()
    copies[0].wait()


EXCHANGE_SCRATCH = [pltpu.SemaphoreType.DMA((N_DEV,)), pltpu.SemaphoreType.DMA((N_DEV,)),
                    pltpu.SemaphoreType.DMA(())]


def _exchange_out(src):
    return jax.ShapeDtypeStruct((N_DEV, src.shape[-2], LANES), src.dtype)


def peer_exchange(name, src, gather):
    def body(src_ref, out_ref, send_sems, recv_sems, local_sem):
        copies = _exchange_copies(src_ref, out_ref, send_sems, recv_sems, local_sem, gather)
        _exchange_start(copies)
        _exchange_wait(copies)

    return pl.pallas_call(
        body, name=name,
        in_specs=[pl.BlockSpec(memory_space=pl.ANY)], out_specs=pl.BlockSpec(memory_space=pl.ANY),
        out_shape=_exchange_out(src), scratch_shapes=list(EXCHANGE_SCRATCH),
    )(src)


def _grid_first_last(grid):
    ids = [pl.program_id(a) for a in range(len(grid))]
    first = functools.reduce(jnp.logical_and, [i == 0 for i in ids])
    last = functools.reduce(jnp.logical_and, [i == g - 1 for i, g in zip(ids, grid)])
    return first, last


def _ride_start(refs, gather, first):
    copies = _exchange_copies(*refs, gather)

    @pl.when(first)
    def _():
        _exchange_start(copies)

    return copies


def _ride_wait(copies, last):
    @pl.when(last)
    def _():
        _exchange_wait(copies)


def _ride_args(ride):
    if ride is None:
        return [], [], [], [], []
    hbm = pl.BlockSpec(memory_space=pl.ANY)
    return [ride[0]], [hbm], [_exchange_out(ride[0])], [hbm], list(EXCHANGE_SCRATCH)


def adamw_sum(name, parts, w, m, v):
    r = w.shape[0]
    tr = ADAM_ROWS
    assert r % tr == 0

    def body(p_ref, w_ref, m_ref, v_ref, g_ref, d_ref, nm_ref, nv_ref):
        g = p_ref[0]
        for j in range(1, N_DEV):
            g = g + p_ref[j]
        mm_ = ADAM_B1 * m_ref[...] + (1.0 - ADAM_B1) * g
        vv = ADAM_B2 * v_ref[...] + (1.0 - ADAM_B2) * (g * g)
        m_hat = mm_ / (1.0 - ADAM_B1 ** ADAM_STEP)
        v_hat = vv / (1.0 - ADAM_B2 ** ADAM_STEP)
        g_ref[...] = g
        d_ref[...] = -ADAM_LR * (m_hat / (jnp.sqrt(v_hat) + ADAM_EPS) + ADAM_WD * w_ref[...])
        nm_ref[...] = mm_
        nv_ref[...] = vv

    row = pl.BlockSpec((tr, LANES), lambda i: (i, 0))
    shp = jax.ShapeDtypeStruct((r, LANES), F32)
    return pl.pallas_call(
        body, name=name, grid=(r // tr,),
        in_specs=[pl.BlockSpec((N_DEV, tr, LANES), lambda i: (0, i, 0)), row, row, row],
        out_specs=[row] * 4, out_shape=[shp] * 4,
        compiler_params=_params(("parallel",)),
    )(parts, w, m, v)


WEIGHTS = ['l0_w_in', 'rwkv_mix', 'rwkv_w0', 'rwkv_w2', 'rwkv_a0', 'rwkv_a2', 'rwkv_g2', 'rwkv_k_k', 'rwkv_k_a',
           'rwkv_r_k', 'rwkv_ln_g', 'rwkv_ln_b', 'ssm_conv_w', 'ssm_conv_b', 'ssm_dt_bias', 'ssm_a_log', 'ssm_d',
           'ssm_norm_g', 'l0_w_out', 'l0_ln1_g', 'l0_ln1_b', 'ffn0_w_up', 'ffn0_conv_w', 'ffn0_conv_b',
           'ffn0_w_down', 'l0_ln2_g', 'l0_ln2_b', 'l1_w_in', 'mla_q_norm_g', 'mla_w_uq', 'mla_kv_norm_g',
           'mla_w_ukv', 'l1_w_out', 'l1_ln1_g', 'l1_ln1_b', 'ffn1_w_up', 'ffn1_conv_w', 'ffn1_conv_b',
           'ffn1_w_down', 'l1_ln2_g', 'l1_ln2_b']
SHARD_AXIS = {'l0_w_in': 1, 'rwkv_w2': 1, 'rwkv_a2': 1, 'rwkv_g2': 1, 'ssm_conv_w': 1, 'l0_w_out': 0,
              'ffn0_w_up': 1, 'ffn0_conv_w': 1, 'ffn0_w_down': 0, 'l1_w_in': 1, 'mla_w_uq': 1, 'mla_w_ukv': 1,
              'l1_w_out': 0, 'ffn1_w_up': 1, 'ffn1_conv_w': 1, 'ffn1_w_down': 0}
MATMUL_W = ['l0_w_in', 'rwkv_w2', 'rwkv_a2', 'rwkv_g2', 'l0_w_out', 'ffn0_w_up', 'ffn0_w_down', 'l1_w_in',
            'mla_w_uq', 'mla_w_ukv', 'l1_w_out', 'ffn1_w_up', 'ffn1_w_down']
CONV_W = ['ssm_conv_w', 'ffn0_conv_w', 'ffn1_conv_w']
TOK_TILE = 256
ADAM_ROWS = 512


def _ceil_to(size, unit):
    return -(-size // unit) * unit


def _flat_rows(pieces, seg_rows, total_rows):
    unit = seg_rows * LANES
    out, total = [], 0
    for p in pieces:
        p = jnp.concatenate([q.reshape(-1) for q in p]) if isinstance(p, list) else p.reshape(-1)
        pad = _ceil_to(p.size, unit) - p.size
        out.append(jnp.pad(p, (0, pad)) if pad else p)
        total += p.size + pad
    tail = _ceil_to(total, total_rows * LANES) - total
    if tail:
        out.append(jnp.zeros((tail,), out[0].dtype))
    return jnp.concatenate(out).reshape(-1, LANES)


def _unflatten(flat2d, shapes, seg_rows):
    flat = flat2d.reshape(-1)
    out, off = [], 0
    for shp in shapes:
        if isinstance(shp, list):
            seg, pos = [], off
            for s_ in shp:
                seg.append(flat[pos:pos + math.prod(s_)].reshape(s_))
                pos += math.prod(s_)
            out.append(seg)
            size = pos - off
        else:
            size = math.prod(shp)
            out.append(flat[off:off + size].reshape(shp))
        off += _ceil_to(size, seg_rows * LANES)
    return out


def _by_size(names):
    return [nm for nm in names if nm in SHARD_AXIS], [nm for nm in names if nm not in SHARD_AXIS]


def _to_heads(t2, bsz, h):
    n, w = t2.shape
    return t2.reshape(bsz, n // bsz, h, w // h).transpose(0, 2, 1, 3)


def _from_heads(t4):
    b, h, t, d = t4.shape
    return t4.transpose(0, 2, 1, 3).reshape(b * t, h * d)


def _row(v):
    return v.reshape(1, -1)


def _pad_lanes(v):
    return jnp.pad(v.reshape(1, -1), ((0, 0), (0, LANES - v.size)))


def _local_step(a, w, comm=None):
    x = a['x']
    bsz, t, d = x.shape
    n = bsz * t
    tm = TOK_TILE
    pos = a['positions'].reshape(bsz, t, 1)
    inv_freq = 1.0 / (ROPE_THETA ** (jnp.arange(0, MLA_ROPE, 2, dtype=F32) / MLA_ROPE))
    inv_freq = jnp.concatenate([inv_freq, inv_freq]).reshape(1, MLA_ROPE)
    target = a['loss_target'].reshape(n, d)

    wi0 = w['l0_w_in']
    win0 = jnp.concatenate([wi0[:, 0:1536], wi0[:, 1792:3328], wi0[:, 1536:1792], wi0[:, 3328:3336],
                            jnp.zeros((d, L0_PAD - 3336), wi0.dtype)], axis=1)
    w2p = jnp.concatenate([w['rwkv_w2'], jnp.zeros_like(w['rwkv_w2'])], axis=0)
    a2p = jnp.concatenate([jnp.zeros_like(w['rwkv_a2']), w['rwkv_a2']], axis=0)
    mix = a['rwkv_mix']
    taps = jnp.stack([mix, 1.0 - mix])
    zero_b = jnp.zeros((1, mix.size), F32)
    rw_map = lambda j: j + jnp.where(j >= 12, 12, 0)
    ssm_map = lambda j: j + 16
    gate_map = lambda j: j
    up_map = lambda j: j + D_FF // LANES
    dt_col = 3328 // LANES
    dtb, alog, dsk = _pad_lanes(a['ssm_dt_bias']), _pad_lanes(a['ssm_a_log']), _pad_lanes(a['ssm_d'])
    pre_p = [_row(a['rwkv_w0']), w2p, _row(a['rwkv_a0']), a2p, w['rwkv_g2'], _row(a['rwkv_k_k']), _row(a['rwkv_k_a'])]
    post_p = [_row(a['rwkv_ln_g']), _row(a['rwkv_ln_b']), _row(a['rwkv_r_k'])]
    sp_p = [dsk, _row(a['ssm_norm_g'])]

    def ln(name, h, y, layer, which):
        ps = [_row(a[f'l{layer}_ln{which}_g']), _row(a[f'l{layer}_ln{which}_b'])]
        return tok_fwd(name, f_ln, [(h, d, 0), (y, d, 0)], ps, [d], tm, n, t)[0]

    def ffn_fwd(layer, h):
        up = mm(h, w[f'ffn{layer}_w_up'], 'nn', f'ffn{layer}_up')
        act = dwconv_fwd(f'ffn{layer}_conv', up.reshape(bsz, t, 2 * D_FF), gate_map, w[f'ffn{layer}_conv_w'],
                         _row(a[f'ffn{layer}_conv_b']), 3, True, upmap=up_map)
        act = act.reshape(n, D_FF)
        return up, act, mm(act, w[f'ffn{layer}_w_down'], 'nn', f'ffn{layer}_down')

    x2 = x.reshape(n, d)
    proj0 = mm(x2, win0, 'nn', 'l0_in')
    p0 = proj0.reshape(bsz, t, L0_PAD)
    xs_r = dwconv_fwd('rwkv_shift', p0, rw_map, taps, zero_b, 2, False).reshape(n, 1792)
    pre_x = [(xs_r, 512, 0), (xs_r, 512, 1), (xs_r, 512, 2), (xs_r, LANES, 12), (xs_r, LANES, 13)]
    heads64 = (RWKV_HEADS, HEAD_DIM)
    r_, v_, lw, kmod, al, be, gt = tok_fwd('rwkv_pre', f_rwkv_pre, pre_x, pre_p, [heads64] * 6 + [RWKV_DIM],
                                           tm, n, t)
    scan_in = [r_, lw, kmod, v_, al, be]
    if comm is None:
        y_h, rstates = rwkv_scan_fwd(*scan_in)
    else:
        y_h, rstates, got = rwkv_scan_fwd(*scan_in, ride=comm.weights_ride(MID_GATHER))
        w = {**w, **comm.weights(MID_GATHER, got)}
    wi1 = w['l1_w_in']
    win1 = jnp.concatenate([wi1, jnp.zeros((d, L1_PAD - 1952), wi1.dtype)], axis=1)
    wq3 = w['mla_w_uq'].reshape(-1, 8, MLA_NOPE + MLA_ROPE)
    wkv3 = w['mla_w_ukv'].reshape(-1, 8, 2 * MLA_NOPE)
    mla_p = [_row(a['mla_q_norm_g']), wq3[:, :, :MLA_NOPE].reshape(-1, 512), wq3[:, :, MLA_NOPE:].reshape(-1, 256),
             _row(a['mla_kv_norm_g']), wkv3[:, :, :MLA_NOPE].reshape(-1, 512), wkv3[:, :, MLA_NOPE:].reshape(-1, 512)]
    post_x = [y_h, r_, kmod, v_, (gt, 512, 0)]
    y_a = tok_fwd('rwkv_post', f_rwkv_post, post_x, post_p, [RWKV_DIM], tm, n, t)[0]
    xbc = dwconv_fwd('ssm_conv', p0, ssm_map, w['ssm_conv_w'], _row(a['ssm_conv_b']), 4, True)
    ys, sstates = ssd_fwd(xbc, p0, dt_col, dtb, alog)
    xbc2 = xbc.reshape(n, 2 * SSM_DIM)
    sp_x = [(ys.reshape(n, SSM_DIM), 512, 0), (xbc2, 512, 0), (proj0, 512, 3)]
    y_b = tok_fwd('ssd_post', f_ssd_post, sp_x, sp_p, [SSM_DIM], tm, n, t)[0]
    wo0 = w['l0_w_out']
    mixed0 = mm(y_b, wo0, 'nn', 'l0_out_b', b_rows=(512, 512), add=mm(y_a, wo0, 'nn', 'l0_out_a', b_rows=(0, 512)))
    h1 = ln('l0_ln1', x2, mixed0, 0, 1)
    up0, act0, f0 = ffn_fwd(0, h1)
    h2 = ln('l0_ln2', h1, f0, 0, 2)

    proj1 = mm(h2, win1, 'nn', 'l1_in')
    q_sb, k_sb, v_sb = tok_fwd('sb_split', f_same, [(proj1, 512, 0), (proj1, 512, 1), (proj1, 512, 2)], [],
                               [heads64] * 3, tm, n, t)
    if comm is None:
        o_c, sb_kept = sb_fwd(q_sb, k_sb, v_sb)
    else:
        o_c, sb_kept, got = sb_fwd(q_sb, k_sb, v_sb, ride=comm.weights_ride(FFN1_GATHER))
        w = {**w, **comm.weights(FFN1_GATHER, got)}
    mla_x = [(proj1, 256, 6), (proj1, LANES, 14)]
    qn, qp_raw, kn, vv = tok_fwd('mla_pre', f_mla_pre, mla_x, mla_p, [heads64, (8, MLA_ROPE), heads64, heads64],
                                 tm, n, t)
    kp_raw = proj1[:, 1920:1920 + MLA_ROPE].reshape(bsz, 1, t, MLA_ROPE)
    qp = rope('rope_q', qp_raw, pos, inv_freq, 1.0)
    kp = rope('rope_k', kp_raw, pos, inv_freq, 1.0)
    o_d, lse_d = mla_fwd(qn, qp, kn, kp, vv)
    y_cd = tok_fwd('attn_merge', f_concat, [o_c, o_d], [], [2 * RWKV_DIM], tm, n, t)[0]
    wo1 = w['l1_w_out']
    mixed1 = mm(y_cd, wo1, 'nn', 'l1_out')
    h3 = ln('l1_ln1', h2, mixed1, 1, 1)
    up1, act1, f1 = ffn_fwd(1, h3)
    h4 = ln('l1_ln2', h3, f1, 1, 2)
    dh4, loss = loss_head(h4, target, tm)

    g = {}

    def ln_bwd(name, h, y, layer, which, dout):
        ps = [_row(a[f'l{layer}_ln{which}_g']), _row(a[f'l{layer}_ln{which}_b'])]
        (dh, dy), (dg, db) = tok_bwd(name, f_ln, [(h, d, 0), (y, d, 0)], ps, [[dout]], tm, n, t)
        g[f'l{layer}_ln{which}_g'], g[f'l{layer}_ln{which}_b'] = dg.reshape(-1), db.reshape(-1)
        return dh, dy

    def ffn_bwd(layer, h, up, act, df, dh_res):
        wup, wdown = w[f'ffn{layer}_w_up'], w[f'ffn{layer}_w_down']
        g[f'ffn{layer}_w_down'] = mm(act, df, 'tn', f'ffn{layer}_dwdown')
        dact = mm(df, wdown, 'nt', f'ffn{layer}_dact').reshape(bsz, t, D_FF)
        dgate, dcw, dcb, dup = dwconv_bwd(f'ffn{layer}_conv_bwd', up.reshape(bsz, t, 2 * D_FF), gate_map,
                                          w[f'ffn{layer}_conv_w'], _row(a[f'ffn{layer}_conv_b']), 3, True, dact,
                                          upmap=up_map)
        dgate, dup = dgate.reshape(n, D_FF), dup.reshape(n, D_FF)
        g[f'ffn{layer}_conv_w'], g[f'ffn{layer}_conv_b'] = dcw, dcb.reshape(-1)
        g[f'ffn{layer}_w_up'] = (mm(h, dgate, 'tn', f'ffn{layer}_dwgate'), mm(h, dup, 'tn', f'ffn{layer}_dwup'))
        dh = mm(dgate, wup, 'nt', f'ffn{layer}_dh_gate', add=dh_res, b_cols=(0, D_FF))
        return mm(dup, wup, 'nt', f'ffn{layer}_dh_up', add=dh, b_cols=(D_FF, D_FF))

    dh3_res, df1 = ln_bwd('l1_ln2_bwd', h3, f1, 1, 2, dh4)
    dh3 = ffn_bwd(1, h3, up1, act1, df1, dh3_res)
    dh2_res, dmixed1 = ln_bwd('l1_ln1_bwd', h2, mixed1, 1, 1, dh3)
    g['l1_w_out'] = mm(y_cd, dmixed1, 'tn', 'l1_dwout')
    dy_cd = mm(dmixed1, wo1, 'nt', 'l1_dy')
    dy_c, dy_d = tok_fwd('attn_split', f_same, [(dy_cd, 512, 0), (dy_cd, 512, 1)], [], [heads64] * 2, tm, n, t)
    parts = {}
    if comm is None:
        dq_sb, dk_sb, dv_sb = sb_bwd(q_sb, k_sb, v_sb, sb_kept, dy_c)
    else:
        dq_sb, dk_sb, dv_sb, parts['a'] = sb_bwd(q_sb, k_sb, v_sb, sb_kept, dy_c, ride=comm.grad_ride('a', g))
    dqn, dqp, dkn, dkp, dvv = mla_bwd(qn, qp, kn, kp, vv, o_d, lse_d, dy_d)
    dqp_raw = rope('rope_q_bwd', dqp, pos, inv_freq, -1.0)
    dkp_raw = rope('rope_k_bwd', dkp, pos, inv_freq, -1.0).reshape(n, MLA_ROPE)
    (dcq, dckv), (dqg, dwq_n, dwq_p, dkvg, dwk, dwv) = tok_bwd('mla_pre_bwd', f_mla_pre, mla_x, mla_p,
                                                               [[dqn], [dqp_raw], [dkn], [dvv]], tm, n, t)
    g['mla_q_norm_g'], g['mla_kv_norm_g'] = dqg.reshape(-1), dkvg.reshape(-1)
    g['mla_w_uq'] = jnp.concatenate([dwq_n.reshape(-1, 8, MLA_NOPE), dwq_p.reshape(-1, 8, MLA_ROPE)],
                                    axis=2).reshape(-1, 8 * (MLA_NOPE + MLA_ROPE))
    g['mla_w_ukv'] = jnp.concatenate([dwk.reshape(-1, 8, MLA_NOPE), dwv.reshape(-1, 8, MLA_NOPE)],
                                     axis=2).reshape(-1, 16 * MLA_NOPE)
    dkp_pad = jnp.pad(dkp_raw, ((0, 0), (0, LANES - MLA_ROPE)))
    dproj1 = tok_fwd('l1_dproj', f_concat, [dq_sb, dk_sb, dv_sb, (dcq, 256, 0), (dckv, LANES, 0),
                                            (dkp_pad, LANES, 0)], [], [L1_PAD], tm, n, t)[0]
    g['l1_w_in'] = mm(h2, dproj1, 'tn', 'l1_dwin')[:, :1952]
    dh2 = mm(dproj1, win1, 'nt', 'l1_dh', add=dh2_res)

    dh1_res, df0 = ln_bwd('l0_ln2_bwd', h1, f0, 0, 2, dh2)
    dh1 = ffn_bwd(0, h1, up0, act0, df0, dh1_res)
    dx_res, dmixed0 = ln_bwd('l0_ln1_bwd', x2, mixed0, 0, 1, dh1)
    g['l0_w_out'] = (mm(y_a, dmixed0, 'tn', 'l0_dwout_a'), mm(y_b, dmixed0, 'tn', 'l0_dwout_b'))
    dy_a = mm(dmixed0, wo0, 'nt', 'l0_dy_a', b_rows=(0, 512))
    dy_b = mm(dmixed0, wo0, 'nt', 'l0_dy_b', b_rows=(512, 512))
    (dy_r, dr1, dkm1, dv1, dgt), (dlng, dlnb, drk) = tok_bwd('rwkv_post_bwd', f_rwkv_post, post_x, post_p, [[dy_a]],
                                                            tm, n, t, dx_layouts=[heads64] * 4 + [None])
    g['rwkv_ln_g'], g['rwkv_ln_b'] = dlng.reshape(-1), dlnb.reshape(-1)
    g['rwkv_r_k'] = drk.reshape(RWKV_HEADS, HEAD_DIM)
    (dys, dxs_skip, dz), (ddsk, dng) = tok_bwd('ssd_post_bwd', f_ssd_post, sp_x, sp_p, [[dy_b]], tm, n, t)
    g['ssm_d'], g['ssm_norm_g'] = ddsk[0, :SSM_HEADS], dng.reshape(-1)
    dxbc_act, ddtr, ddtb, dalog = ssd_bwd(xbc, p0, dt_col, dtb, alog, sstates, dys.reshape(bsz, t, SSM_DIM),
                                          dxs_skip.reshape(bsz, t, SSM_DIM))
    g['ssm_dt_bias'], g['ssm_a_log'] = ddtb[0, :SSM_HEADS], dalog[0, :SSM_HEADS]
    dxbc, dscw, dscb = dwconv_bwd('ssm_conv_bwd', p0, ssm_map, w['ssm_conv_w'], _row(a['ssm_conv_b']), 4, True,
                                  dxbc_act)
    g['ssm_conv_w'], g['ssm_conv_b'] = dscw, dscb.reshape(-1)
    if comm is None:
        dscan = rwkv_scan_bwd(*scan_in, rstates, dy_r)
    else:
        *dscan, parts['b'] = rwkv_scan_bwd(*scan_in, rstates, dy_r, ride=comm.grad_ride('b', g))
    dr2, dlw, dk2, dv2, dal, dbe = dscan
    pre_ct = [[dr1, dr2], [dv1, dv2], [dlw], [dkm1, dk2], [dal], [dbe], [dgt]]
    dpre_x, dpre_p = tok_bwd('rwkv_pre_bwd', f_rwkv_pre, pre_x, pre_p, pre_ct, tm, n, t)
    g['rwkv_w0'], g['rwkv_a0'] = dpre_p[0].reshape(-1), dpre_p[2].reshape(-1)
    g['rwkv_w2'], g['rwkv_a2'], g['rwkv_g2'] = dpre_p[1][:64], dpre_p[3][64:], dpre_p[4]
    g['rwkv_k_k'], g['rwkv_k_a'] = dpre_p[5].reshape(-1), dpre_p[6].reshape(-1)
    dxs_r = jnp.concatenate(dpre_x, axis=1).reshape(bsz, t, 1792)
    d_rw, dtaps, _ = dwconv_bwd('rwkv_shift_bwd', p0, rw_map, taps, zero_b, 2, False, dxs_r)
    d_rw = d_rw.reshape(n, 1792)
    g['rwkv_mix'] = dtaps[0] - dtaps[1]
    dproj0 = tok_fwd('l0_dproj', f_concat, [(d_rw, 1536, 0), (dz, 512, 0), (dxbc.reshape(n, 2 * SSM_DIM), 1024, 0),
                                            (d_rw, 256, 6), (ddtr.reshape(n, LANES), LANES, 0)],
                     [], [L0_PAD], tm, n, t)[0]
    dwin0 = mm(x2, dproj0, 'tn', 'l0_dwin')
    g['l0_w_in'] = jnp.concatenate([dwin0[:, 0:1536], dwin0[:, 3072:3328], dwin0[:, 1536:3072],
                                    dwin0[:, 3328:3336]], axis=1)
    dx = mm(dproj0, win0, 'nt', 'l0_dx', add=dx_res)
    if comm is not None:
        parts['c'] = peer_exchange('grad_exchange_c', comm.grad_ride('c', g)[0], False)
    return loss, dx.reshape(bsz, t, d), g, parts


GRAD_GROUPS = {
    'a': ['ffn1_w_up', 'ffn1_conv_w', 'ffn1_conv_b', 'ffn1_w_down', 'l1_ln2_g', 'l1_ln2_b'],
    'c': ['l0_w_in', 'rwkv_mix', 'rwkv_w0', 'rwkv_w2', 'rwkv_a0', 'rwkv_a2', 'rwkv_g2', 'rwkv_k_k', 'rwkv_k_a'],
}
GRAD_GROUPS['b'] = [nm for nm in WEIGHTS if nm not in GRAD_GROUPS['a'] + GRAD_GROUPS['c']]
FIRST_GATHER = ['l0_w_in', 'rwkv_w2', 'rwkv_a2', 'rwkv_g2'] + CONV_W
MID_GATHER = ['l0_w_out', 'ffn0_w_up', 'ffn0_w_down', 'l1_w_in', 'mla_w_uq', 'mla_w_ukv', 'l1_w_out']
FFN1_GATHER = ['ffn1_w_up', 'ffn1_w_down']
BF16_ROWS = 16


class _Comm:
    def __init__(self, a):
        self.a = a

    def _pieces(self, names):
        return [lax.bitcast_convert_type(self.a[nm], BF16) if nm in CONV_W else self.a[nm].astype(BF16)
                for nm in names]

    def _unpack(self, names, got):
        shapes = [p.shape for p in self._pieces(names)]
        blocks = [_unflatten(got[k], shapes, BF16_ROWS) for k in range(N_DEV)]
        out = {}
        for i, nm in enumerate(names):
            blk = [blocks[k][i] for k in range(N_DEV)]
            if nm in CONV_W:
                blk = [lax.bitcast_convert_type(b, F32) for b in blk]
            out[nm] = jnp.concatenate(blk, axis=SHARD_AXIS[nm])
        return out

    def first_weights(self):
        return self.weights(FIRST_GATHER, peer_exchange('gather_first_weights', self.weights_ride(FIRST_GATHER)[0], True))

    def weights_ride(self, names):
        return _flat_rows(self._pieces(names), BF16_ROWS, BF16_ROWS), True

    def weights(self, names, got):
        return self._unpack(names, got)

    def grad_ride(self, group, g):
        def shard_of(nm, k):
            gv = g[nm]
            if nm not in SHARD_AXIS:
                return gv
            per = N_DEV
            if isinstance(gv, tuple):
                gv, k, per = gv[k // 4], k % 4, 4
            width = gv.shape[SHARD_AXIS[nm]] // per
            return lax.slice_in_dim(gv, k * width, (k + 1) * width, axis=SHARD_AXIS[nm])

        big, small = _by_size(GRAD_GROUPS[group])
        return jnp.stack([_flat_rows([shard_of(nm, k) for nm in big] + [[g[nm] for nm in small]], SUBLANES, ADAM_ROWS)
                          for k in range(N_DEV)]), False


def _step(a):
    comm = _Comm(a)
    loss, dx, _, parts = _local_step(a, comm.first_weights(), comm)
    loss = lax.psum(loss, ('x', 'y', 'c'))
    res = {}
    for group, names in GRAD_GROUPS.items():
        big, small = _by_size(names)
        flat = lambda prefix: _flat_rows([a[prefix + nm] for nm in big] + [[a[prefix + nm] for nm in small]],
                                         SUBLANES, ADAM_ROWS)
        outs = adamw_sum(f'adamw_{group}', parts[group], flat(''), flat('m_'), flat('v_'))
        shapes = [a[nm].shape for nm in big] + [[a[nm].shape for nm in small]]
        per_out = [_unflatten(o, shapes, SUBLANES) for o in outs]
        for i, nm in enumerate(big):
            res[nm] = [per_out[j][i] for j in range(4)]
        for i, nm in enumerate(small):
            res[nm] = [per_out[j][-1][i] for j in range(4)]
    return (loss, dx, *[res[nm][j] for j in range(4) for nm in WEIGHTS])


def kernel(x, positions, l0_w_in, rwkv_mix, rwkv_w0, rwkv_w2, rwkv_a0, rwkv_a2, rwkv_g2, rwkv_k_k, rwkv_k_a, rwkv_r_k, rwkv_ln_g, rwkv_ln_b, ssm_conv_w, ssm_conv_b, ssm_dt_bias, ssm_a_log, ssm_d, ssm_norm_g, l0_w_out, l0_ln1_g, l0_ln1_b, ffn0_w_up, ffn0_conv_w, ffn0_conv_b, ffn0_w_down, l0_ln2_g, l0_ln2_b, l1_w_in, mla_q_norm_g, mla_w_uq, mla_kv_norm_g, mla_w_ukv, l1_w_out, l1_ln1_g, l1_ln1_b, ffn1_w_up, ffn1_conv_w, ffn1_conv_b, ffn1_w_down, l1_ln2_g, l1_ln2_b, loss_target, m_l0_w_in, m_rwkv_mix, m_rwkv_w0, m_rwkv_w2, m_rwkv_a0, m_rwkv_a2, m_rwkv_g2, m_rwkv_k_k, m_rwkv_k_a, m_rwkv_r_k, m_rwkv_ln_g, m_rwkv_ln_b, m_ssm_conv_w, m_ssm_conv_b, m_ssm_dt_bias, m_ssm_a_log, m_ssm_d, m_ssm_norm_g, m_l0_w_out, m_l0_ln1_g, m_l0_ln1_b, m_ffn0_w_up, m_ffn0_conv_w, m_ffn0_conv_b, m_ffn0_w_down, m_l0_ln2_g, m_l0_ln2_b, m_l1_w_in, m_mla_q_norm_g, m_mla_w_uq, m_mla_kv_norm_g, m_mla_w_ukv, m_l1_w_out, m_l1_ln1_g, m_l1_ln1_b, m_ffn1_w_up, m_ffn1_conv_w, m_ffn1_conv_b, m_ffn1_w_down, m_l1_ln2_g, m_l1_ln2_b, v_l0_w_in, v_rwkv_mix, v_rwkv_w0, v_rwkv_w2, v_rwkv_a0, v_rwkv_a2, v_rwkv_g2, v_rwkv_k_k, v_rwkv_k_a, v_rwkv_r_k, v_rwkv_ln_g, v_rwkv_ln_b, v_ssm_conv_w, v_ssm_conv_b, v_ssm_dt_bias, v_ssm_a_log, v_ssm_d, v_ssm_norm_g, v_l0_w_out, v_l0_ln1_g, v_l0_ln1_b, v_ffn0_w_up, v_ffn0_conv_w, v_ffn0_conv_b, v_ffn0_w_down, v_l0_ln2_g, v_l0_ln2_b, v_l1_w_in, v_mla_q_norm_g, v_mla_w_uq, v_mla_kv_norm_g, v_mla_w_ukv, v_l1_w_out, v_l1_ln1_g, v_l1_ln1_b, v_ffn1_w_up, v_ffn1_conv_w, v_ffn1_conv_b, v_ffn1_w_down, v_l1_ln2_g, v_l1_ln2_b):
    return _step(dict(locals()))
```

```python
import functools
import math

import jax
import jax.numpy as jnp
from jax import lax
from jax.experimental import pallas as pl
from jax.experimental.pallas import tpu as pltpu

F32 = jnp.float32
BF16 = jnp.bfloat16
HI = lax.Precision.HIGHEST

V7X_VMEM_BYTES = 64 * 1024 * 1024
VMEM_LIMIT = V7X_VMEM_BYTES - 8 * 1024 * 1024
LANES = 128
SUBLANES = 8
N_DEV = 8

D_MODEL = 1024
HEAD_DIM = 64
RWKV_DIM = 512
RWKV_HEADS = 8
RWKV_GN_EPS = 64e-5
RWKV_CHUNK = 64
SSM_DIM = 512
SSM_HEADS = 8
SSM_CHUNK = 128
SSM_STATE = 128
Q_BLOCK = 128
SB_HEADS_PER_STEP = 4
MLA_HEADS_PER_STEP = 4
ATTN_FWD_HEADS_PER_STEP = 8
MLA_NOPE = 64
MLA_ROPE = 32
ROPE_THETA = 10000.0
D_FF = 2816
DEPTH = 2
ALPHA = (2 * DEPTH) ** 0.25
L0_PAD = 3456
L1_PAD = 2048

ADAM_LR = 0.001
ADAM_B1 = 0.9
ADAM_B2 = 0.999
ADAM_EPS = 1e-08
ADAM_WD = 0.01
ADAM_STEP = 10

NEG_BIG = -1e30


def _params(sem=None):
    return pltpu.CompilerParams(dimension_semantics=sem, vmem_limit_bytes=VMEM_LIMIT)


P_F32, P_BF16, P_BF16X3 = 0, 1, 2


def _dg_raw(a, b, ca, cb, fast):
    dims = (((ca,), (cb,)), ((), ()))
    if fast == P_BF16:
        return lax.dot_general(a.astype(BF16), b.astype(BF16), dims, preferred_element_type=F32)
    prec = HI if fast == P_F32 else lax.Precision.HIGH
    return lax.dot_general(a, b, dims, precision=prec, preferred_element_type=F32)


@functools.partial(jax.custom_vjp, nondiff_argnums=(2, 3, 4))
def dg(a, b, ca, cb, fast):
    return _dg_raw(a, b, ca, cb, fast)


def _dg_fwd(a, b, ca, cb, fast):
    return _dg_raw(a, b, ca, cb, fast), (a, b)


def _dg_bwd(ca, cb, fast, res, ct):
    a, b = res
    fa, fb = 1 - ca, 1 - cb
    da = _dg_raw(ct, b, 1, fb, fast) if ca == 1 else _dg_raw(b, ct, fb, 1, fast)
    db = _dg_raw(a, ct, fa, 0, fast) if cb == 0 else _dg_raw(ct, a, 0, fa, fast)
    return da.astype(a.dtype), db.astype(b.dtype)


dg.defvjp(_dg_fwd, _dg_bwd)


def mmb(a, b):
    return dg(a, b, 1, 0, P_BF16)


def mmf(a, b):
    return dg(a, b, 1, 0, P_F32)


def mmf_nt(a, b):
    return dg(a, b, 1, 1, P_F32)


def mmf_tn(a, b):
    return dg(a, b, 0, 0, P_F32)


def mm3(a, b):
    return dg(a, b, 1, 0, P_BF16X3)


def mm3_nt(a, b):
    return dg(a, b, 1, 1, P_BF16X3)


def mm3_tn(a, b):
    return dg(a, b, 0, 0, P_BF16X3)


def _split3_dot(x, m01, cb, terms=3):
    parts, rest = [], x
    for i in range(terms):
        parts.append(rest.astype(BF16))
        if i + 1 < terms:
            rest = rest - parts[-1].astype(F32)
    rows = x.shape[0]
    out = lax.dot_general(jnp.concatenate(parts, axis=0), m01.astype(BF16), (((1,), (cb,)), ((), ())),
                          preferred_element_type=F32)
    return functools.reduce(lambda a_, b_: a_ + b_, [out[i * rows:(i + 1) * rows] for i in range(terms)])


def _lower_ones(n):
    return jnp.where(_iota((n, n), 0) >= _iota((n, n), 1), 1.0, 0.0)


SUFFIX_TERMS = 2


@jax.custom_vjp
def suffix_sum(x):
    return _split3_dot(x, _lower_ones(x.shape[1]), 0, SUFFIX_TERMS)


def _suffix_sum_fwd(x):
    return suffix_sum(x), None


def _suffix_sum_bwd(_, ct):
    return (_split3_dot(ct, _lower_ones(ct.shape[1]), 1, SUFFIX_TERMS),)


suffix_sum.defvjp(_suffix_sum_fwd, _suffix_sum_bwd)


def _iota(shape, dim):
    return lax.broadcasted_iota(jnp.int32, shape, dim)


def _softplus(x):
    return jnp.maximum(x, 0.0) + jnp.log1p(jnp.exp(-jnp.abs(x)))


def _silu(x):
    return x * jax.nn.sigmoid(x)


def _largest_tile(n, cap, mult):
    best = None
    t = mult
    while t <= min(n, cap):
        if n % t == 0:
            best = t
        t += mult
    return n if best is None else best


MM_VMEM_BUDGET = 40 * 1024 * 1024
V7X_HBM_BYTES_PER_S = 3.2e12
GRID_STEP_S = 0.35e-6


def _mm_tiles(M, N, K, a_bytes, b_bytes, has_add):
    def divs(n):
        return [d for d in range(LANES, n + 1, LANES) if n % d == 0] or [n]

    best = None
    for tm in divs(M):
        for tn in divs(N):
            if tm * tn * 4 > 12 * 1024 * 1024:
                continue
            for tk in divs(K):
                vmem = (2 * (tm * tk * a_bytes + tk * tn * b_bytes) + 2 * tm * tn * 4 * (2 if has_add else 1)
                        + (tm * tk + tk * tn) * 2 + tm * tn * 4)
                if vmem > MM_VMEM_BUDGET:
                    continue
                ni, nj, nk = M // tm, N // tn, K // tk
                a_reads = M * K * a_bytes * (1 if nk == 1 else nj)
                b_reads = K * N * b_bytes * (1 if (nk == 1 and nj == 1) else ni)
                traffic = a_reads + b_reads + M * N * 4 * (2 if has_add else 1)
                cost = traffic / V7X_HBM_BYTES_PER_S + ni * nj * nk * GRID_STEP_S
                if min(tm, tn, tk) < 256 and min(M, N, K) >= 256:
                    cost *= 1.5
                if best is None or cost < best[0]:
                    best = (cost, tm, tn, tk)
    return best[1:]


def mm(a, b, mode, name, add=None, b_rows=None, b_cols=None):
    r0, nr = b_rows or (0, b.shape[0])
    c0, nc = b_cols or (0, b.shape[1])
    if mode == "nn":
        (M, K), N = a.shape, nc
        assert nr == K
    elif mode == "nt":
        (M, K), N = a.shape, nr
        assert nc == K
    else:
        (K, M), N = a.shape, b.shape[1]
        assert b_rows is None and b_cols is None
    has_add = add is not None
    tm, tn, tk = _mm_tiles(M, N, K, a.dtype.itemsize, b.dtype.itemsize, has_add)
    nk = K // tk
    keep_a = nk == 1 and N // tn > 1 and a.dtype != BF16
    if mode == "nn":
        assert r0 % tk == 0 and c0 % tn == 0
        a_spec = pl.BlockSpec((tm, tk), lambda i, j, k: (i, k))
        b_spec = pl.BlockSpec((tk, tn), lambda i, j, k: (k + r0 // tk, j + c0 // tn))
        dims = (((1,), (0,)), ((), ()))
    elif mode == "nt":
        assert r0 % tn == 0 and c0 % tk == 0
        a_spec = pl.BlockSpec((tm, tk), lambda i, j, k: (i, k))
        b_spec = pl.BlockSpec((tn, tk), lambda i, j, k: (j + r0 // tn, k + c0 // tk))
        dims = (((1,), (1,)), ((), ()))
    else:
        a_spec = pl.BlockSpec((tk, tm), lambda i, j, k: (k, i))
        b_spec = pl.BlockSpec((tk, tn), lambda i, j, k: (k, j))
        dims = (((0,), (0,)), ((), ()))
    o_spec = pl.BlockSpec((tm, tn), lambda i, j, k: (i, j))

    def body(a_ref, b_ref, *rest):
        o_ref = rest[1] if has_add else rest[0]
        k = pl.program_id(2)
        if keep_a:
            a_bf = rest[-1]

            @pl.when(pl.program_id(1) == 0)
            def _():
                a_bf[...] = a_ref[...].astype(BF16)

            av = a_bf[...]
        else:
            av = a_ref[...].astype(BF16)
        part = lax.dot_general(av, b_ref[...].astype(BF16), dims, preferred_element_type=F32)

        @pl.when(k == 0)
        def _():
            o_ref[...] = part + rest[0][...] if has_add else part

        @pl.when(k > 0)
        def _():
            o_ref[...] += part

    ins = [a, b] + ([add] if has_add else [])
    specs = [a_spec, b_spec] + ([o_spec] if has_add else [])
    return pl.pallas_call(
        body, name=name, grid=(M // tm, N // tn, nk), in_specs=specs, out_specs=o_spec,
        out_shape=jax.ShapeDtypeStruct((M, N), F32),
        scratch_shapes=[pltpu.VMEM(a_spec.block_shape, BF16)] if keep_a else [],
        compiler_params=_params(("parallel", "arbitrary", "arbitrary")),
    )(*ins)


def _is_heads(x):
    return not isinstance(x, tuple)


def _tok_arr(x):
    return x if _is_heads(x) else x[0]


def _tok_width(x):
    return x.shape[1] * x.shape[3] if _is_heads(x) else x[1]


def _heads_spec(h, dh, tm, tiles_per_seq):
    return pl.BlockSpec((None, h, tm, dh), lambda i: (i // tiles_per_seq, 0, i % tiles_per_seq, 0))


def _x_spec(x, tm, tiles_per_seq):
    if _is_heads(x):
        return _heads_spec(x.shape[1], x.shape[3], tm, tiles_per_seq)
    return pl.BlockSpec((tm, x[1]), functools.partial(lambda i, cb: (i, cb), cb=x[2]))


def _out_spec_shape(layout, n, seq, tm, dtype=F32):
    if isinstance(layout, tuple):
        h, dh = layout
        return _heads_spec(h, dh, tm, seq // tm), jax.ShapeDtypeStruct((n // seq, h, seq, dh), dtype)
    return pl.BlockSpec((tm, layout), lambda i: (i, 0)), jax.ShapeDtypeStruct((n, layout), dtype)


def _tok_load(ref):
    if len(ref.shape) == 3:
        return jnp.concatenate([ref[hh] for hh in range(ref.shape[0])], axis=1)
    return ref[...]


def _tok_store(ref, val):
    if len(ref.shape) == 3:
        dh = ref.shape[2]
        for hh in range(ref.shape[0]):
            ref[hh] = val[:, hh * dh:(hh + 1) * dh]
    else:
        ref[...] = val


def _p_specs(ps):
    return [pl.BlockSpec(p.shape, lambda i: (0, 0)) for p in ps]


def tok_fwd(name, f, xs, ps, out_layouts, tm, n, seq, out_dtype=F32):
    nx, npar = len(xs), len(ps)
    outs = [_out_spec_shape(lay, n, seq, tm, out_dtype) for lay in out_layouts]

    def body(*refs):
        xv = [_tok_load(r) for r in refs[:nx]]
        pv = [r[...].astype(F32) for r in refs[nx:nx + npar]]
        for o, r in zip(f(*xv, *pv), refs[nx + npar:]):
            _tok_store(r, o.astype(out_dtype))

    return pl.pallas_call(
        body, name=name, grid=(n // tm,),
        in_specs=[_x_spec(x, tm, seq // tm) for x in xs] + _p_specs(ps),
        out_specs=[o[0] for o in outs], out_shape=[o[1] for o in outs],
        compiler_params=_params(("parallel",)),
    )(*[_tok_arr(x) for x in xs], *ps)


def tok_bwd(name, f, xs, ps, cts, tm, n, seq, dx_layouts=None):
    nx, npar = len(xs), len(ps)
    ct_flat = [c for group in cts for c in group]
    nct = len(ct_flat)
    dx_layouts = dx_layouts or [None] * nx
    dxs = [_out_spec_shape(lay if lay else _tok_width(x), n, seq, tm) for x, lay in zip(xs, dx_layouts)]

    def body(*refs):
        xv = [_tok_load(r) for r in refs[:nx]]
        pv = [r[...].astype(F32) for r in refs[nx:nx + npar]]
        ct_refs = refs[nx + npar:nx + npar + nct]
        dx_refs = refs[nx + npar + nct:nx + npar + nct + nx]
        dp_refs = refs[nx + npar + nct + nx:]
        cv, pos = [], 0
        for group in cts:
            acc = _tok_load(ct_refs[pos])
            for r in ct_refs[pos + 1:pos + len(group)]:
                acc = acc + _tok_load(r)
            cv.append(acc)
            pos += len(group)
        _, vjp = jax.vjp(f, *xv, *pv)
        grads = vjp(tuple(cv))
        for g, r in zip(grads[:nx], dx_refs):
            _tok_store(r, g)

        @pl.when(pl.program_id(0) == 0)
        def _():
            for r in dp_refs:
                r[...] = jnp.zeros_like(r)

        for g, r in zip(grads[nx:], dp_refs):
            r[...] += g

    ct_specs = [_heads_spec(c.shape[1], c.shape[3], tm, seq // tm) if c.ndim == 4
                else pl.BlockSpec((tm, c.shape[1]), lambda i: (i, 0)) for c in ct_flat]
    outs = pl.pallas_call(
        body, name=name, grid=(n // tm,),
        in_specs=[_x_spec(x, tm, seq // tm) for x in xs] + _p_specs(ps) + ct_specs,
        out_specs=[d[0] for d in dxs] + _p_specs(ps),
        out_shape=[d[1] for d in dxs] + [jax.ShapeDtypeStruct(p.shape, F32) for p in ps],
        compiler_params=_params(("arbitrary",)),
    )(*[_tok_arr(x) for x in xs], *ps, *ct_flat)
    return outs[:nx], outs[nx:]


def f_ln(h, y, g, b):
    pre = ALPHA * h + y
    mu = jnp.mean(pre, axis=-1, keepdims=True)
    xc = pre - mu
    var = jnp.mean(xc * xc, axis=-1, keepdims=True)
    return (xc * lax.rsqrt(var + 1e-5) * g + b,)


def _head_sel(width, nheads_pad, per):
    return jnp.where(_iota((width, nheads_pad), 0) // per == _iota((width, nheads_pad), 1), 1.0, 0.0).astype(F32)


def _head_sel_t(nheads_pad, width, per):
    return jnp.where(_iota((nheads_pad, width), 1) // per == _iota((nheads_pad, width), 0), 1.0, 0.0).astype(F32)


@jax.custom_vjp
def head_sum(x):
    return _split3_dot(x, _head_sel(RWKV_DIM, LANES, HEAD_DIM), 0)


@jax.custom_vjp
def head_spread(y):
    return _split3_dot(y, _head_sel(RWKV_DIM, LANES, HEAD_DIM), 1)


head_sum.defvjp(lambda x: (head_sum(x), None), lambda _, ct: (head_spread(ct),))
head_spread.defvjp(lambda y: (head_spread(y), None), lambda _, ct: (head_sum(ct),))


def f_rwkv_pre(r, k, v, lora, glo, w0, w2p, a0, a2p, g2, k_k, k_a):
    lane = _iota(lora.shape, 1)
    tw = jnp.where(lane < 64, jnp.tanh(lora), 0.0)
    ta = jnp.where(lane >= 64, lora, 0.0)
    log_w = -_softplus(-(w0 + mmb(tw, w2p))) - 0.5
    lw = -jnp.exp(log_w)
    a = jax.nn.sigmoid(a0 + mmb(ta, a2p))
    g = mmb(jax.nn.sigmoid(glo), g2)
    kk = k * k_k
    nrm = jnp.sqrt(jnp.maximum(head_sum(kk * kk), 1e-24))
    kkn = kk * head_spread(1.0 / nrm)
    kmod = k * (1.0 + (a - 1.0) * k_a)
    return r, v, lw, kmod, -kkn, kkn * a, g


def f_rwkv_post(y, r, kmod, v, g, ln_g, ln_b, r_k):
    inv = 1.0 / HEAD_DIM
    mu = head_spread(head_sum(y) * inv)
    yc = y - mu
    var = head_sum(yc * yc) * inv
    rstd = head_spread(lax.rsqrt(var + RWKV_GN_EPS))
    yn = yc * rstd * ln_g + ln_b
    bonus = head_spread(head_sum(r * kmod * r_k)) * v
    return ((yn + bonus) * g,)


def f_ssd_post(y, xs, z, d_skip, norm_g):
    sel_t = _head_sel_t(LANES, SSM_DIM, HEAD_DIM)
    d_e = jnp.sum(mmf(jnp.broadcast_to(d_skip, (SUBLANES, LANES)), sel_t), axis=0, keepdims=True) * (1.0 / SUBLANES)
    u = (y + xs * d_e) * _silu(z)
    first = _iota(u.shape, 1) < (SSM_DIM // 2)
    uu = u * u
    inv = 2.0 / SSM_DIM
    ms0 = jnp.sum(jnp.where(first, uu, 0.0), axis=-1, keepdims=True) * inv
    ms1 = jnp.sum(jnp.where(first, 0.0, uu), axis=-1, keepdims=True) * inv
    ms = jnp.where(first, ms0, ms1)
    return (u * lax.rsqrt(ms + 1e-5) * norm_g,)


def f_mla_pre(cq, ckv, qg, wq_nope, wq_rope, kvg, wk_nope, wv):
    def rms(x, g):
        return x * lax.rsqrt(jnp.mean(x * x, axis=-1, keepdims=True) + 1e-6) * g
    q_in, kv_in = rms(cq, qg), rms(ckv, kvg)
    return mmb(q_in, wq_nope), mmb(q_in, wq_rope), mmb(kv_in, wk_nope), mmb(kv_in, wv)


def f_same(*xs):
    return xs


def f_concat(*xs):
    return (jnp.concatenate(xs, axis=1),)


def _shift_down(x, s, row):
    return x if s == 0 else jnp.where(row >= s, pltpu.roll(x, s, 0), 0.0)


def _shift_up(x, s, row, t):
    return x if s == 0 else jnp.where(row < t - s, pltpu.roll(x, t - s, 0), 0.0)


def dwconv_fwd(name, u, colmap, w, b, taps, silu, upmap=None, out_dtype=F32):
    bsz, t, _ = u.shape
    c = w.shape[1]
    tc = LANES
    has_up = upmap is not None

    def body(*refs):
        u_ref, w_ref, b_ref = refs[:3]
        o_ref = refs[-1]
        uv = u_ref[...]
        wv = w_ref[...]
        row = _iota(uv.shape, 0)
        acc = jnp.broadcast_to(b_ref[...], uv.shape)
        for i in range(taps):
            acc = acc + wv[i:i + 1, :] * _shift_down(uv, taps - 1 - i, row)
        if silu:
            acc = _silu(acc)
        if has_up:
            acc = acc * refs[3][...]
        o_ref[...] = acc.astype(out_dtype)

    specs = [pl.BlockSpec((None, t, tc), lambda bb, j: (bb, 0, colmap(j))),
             pl.BlockSpec((taps, tc), lambda bb, j: (0, j)),
             pl.BlockSpec((1, tc), lambda bb, j: (0, j))]
    ins = [u, w, b]
    if has_up:
        specs.append(pl.BlockSpec((None, t, tc), lambda bb, j: (bb, 0, upmap(j))))
        ins.append(u)
    return pl.pallas_call(
        body, name=name, grid=(bsz, c // tc), in_specs=specs,
        out_specs=pl.BlockSpec((None, t, tc), lambda bb, j: (bb, 0, j)),
        out_shape=jax.ShapeDtypeStruct((bsz, t, c), out_dtype),
        compiler_params=_params(("parallel", "parallel")),
    )(*ins)


def dwconv_bwd(name, u, colmap, w, b, taps, silu, dout, upmap=None, grad_dtype=F32):
    bsz, t, _ = u.shape
    c = w.shape[1]
    tc = LANES
    has_up = upmap is not None

    def body(*refs):
        u_ref, w_ref, b_ref, d_ref = refs[:4]
        nin = 5 if has_up else 4
        du_ref, dw_ref, db_ref = refs[nin:nin + 3]
        uv = u_ref[...]
        wv = w_ref[...]
        dv = d_ref[...]
        row = _iota(uv.shape, 0)
        shifted = [_shift_down(uv, taps - 1 - i, row) for i in range(taps)]
        cg = jnp.broadcast_to(b_ref[...], uv.shape)
        for i in range(taps):
            cg = cg + wv[i:i + 1, :] * shifted[i]
        if silu:
            sg = jax.nn.sigmoid(cg)
            act = cg * sg
            dact_dcg = sg * (1.0 + cg * (1.0 - sg))
        else:
            act = cg
            dact_dcg = None
        if has_up:
            refs[nin + 3][...] = (dv * act).astype(grad_dtype)
            dv = dv * refs[4][...]
        dcg = dv * dact_dcg if silu else dv
        du = jnp.zeros_like(uv)
        for i in range(taps):
            du = du + wv[i:i + 1, :] * _shift_up(dcg, taps - 1 - i, row, t)
        du_ref[...] = du.astype(grad_dtype)

        @pl.when(pl.program_id(1) == 0)
        def _():
            dw_ref[...] = jnp.zeros_like(dw_ref)
            db_ref[...] = jnp.zeros_like(db_ref)

        for i in range(taps):
            dw_ref[i:i + 1, :] += jnp.sum(dcg * shifted[i], axis=0, keepdims=True)
        db_ref[...] += jnp.sum(dcg, axis=0, keepdims=True)

    specs = [pl.BlockSpec((None, t, tc), lambda j, bb: (bb, 0, colmap(j))),
             pl.BlockSpec((taps, tc), lambda j, bb: (0, j)),
             pl.BlockSpec((1, tc), lambda j, bb: (0, j)),
             pl.BlockSpec((None, t, tc), lambda j, bb: (bb, 0, j))]
    ins = [u, w, b, dout]
    if has_up:
        specs.append(pl.BlockSpec((None, t, tc), lambda j, bb: (bb, 0, upmap(j))))
        ins.append(u)
    big = pl.BlockSpec((None, t, tc), lambda j, bb: (bb, 0, j))
    out_specs = [big, pl.BlockSpec((taps, tc), lambda j, bb: (0, j)), pl.BlockSpec((1, tc), lambda j, bb: (0, j))]
    out_shape = [jax.ShapeDtypeStruct((bsz, t, c), grad_dtype), jax.ShapeDtypeStruct((taps, c), F32),
                 jax.ShapeDtypeStruct((1, c), F32)]
    if has_up:
        out_specs.append(big)
        out_shape.append(jax.ShapeDtypeStruct((bsz, t, c), grad_dtype))
    return pl.pallas_call(
        body, name=name, grid=(c // tc, bsz), in_specs=specs, out_specs=out_specs, out_shape=out_shape,
        compiler_params=_params(("parallel", "arbitrary")),
    )(*ins)


def _each(f, *lists):
    return [f(*xs) for xs in zip(*lists)]


def rwkv_chunk(s0, r, lw, k, v, al, be):
    c = r[0].shape[0]
    ii, jj = _iota((c, c), 0), _iota((c, c), 1)
    incl, strict = ii >= jj, ii > jj
    ones_incl = jnp.where(incl, 1.0, 0.0)
    eye = jnp.where(ii == jj, 1.0, 0.0)
    cum = _each(lambda x: mmf(ones_incl, x), lw)
    gam_inv = _each(lambda x: jnp.exp(-x), cum)
    at = _each(lambda a_, c_, l_: a_ * jnp.exp(c_ - l_), al, cum, lw)
    rt = _each(lambda r_, c_: r_ * jnp.exp(c_), r, cum)
    bt = _each(lambda b_, g_: b_ * g_, be, gam_inv)
    kt = _each(lambda k_, g_: k_ * g_, k, gam_inv)
    a_b = _each(lambda x, y_: jnp.where(strict, mm3_nt(x, y_), 0.0), at, bt)
    a_k = _each(lambda x, y_: jnp.where(strict, mm3_nt(x, y_), 0.0), at, kt)
    rhs0 = _each(mm3_nt, at, s0)
    rhs = _each(lambda x, a_, v_: x + mm3(a_, v_), rhs0, a_k, v)
    p = _each(lambda x: eye + x, a_b)
    m = a_b
    for _ in range(int(math.log2(c)) - 1):
        m = _each(mm3, m, m)
        p = _each(lambda p_, m_: p_ + mm3(p_, m_), p, m)
    u = _each(mm3, p, rhs)
    r_b = _each(lambda x, y_: jnp.where(incl, mm3_nt(x, y_), 0.0), rt, bt)
    r_k = _each(lambda x, y_: jnp.where(incl, mm3_nt(x, y_), 0.0), rt, kt)
    y0 = _each(mm3_nt, rt, s0)
    y1 = _each(lambda y_, b_, u_: y_ + mm3(b_, u_), y0, r_b, u)
    y = _each(lambda y_, k_, v_: y_ + mm3(k_, v_), y1, r_k, v)
    su = _each(mm3_tn, u, bt)
    sv = _each(mm3_tn, v, kt)
    s1 = _each(lambda s_, a_, b_, l_: (s_ + a_ + b_) * jnp.exp(jnp.sum(l_, axis=0, keepdims=True)), s0, su, sv, lw)
    return y, s1


def rwkv_scan_fwd(r, lw, k, v, al, be, ride=None):
    bsz, h, t, d = r.shape
    c = RWKV_CHUNK
    nc = t // c
    grid = (bsz, nc)
    r_in, r_specs, r_out, r_ospecs, r_scr = _ride_args(ride)

    def body(*refs):
        r_ref, lw_ref, k_ref, v_ref, al_ref, be_ref = refs[:6]
        y_ref, st_ref = refs[6 + len(r_in):8 + len(r_in)]
        s_scr = refs[8 + 2 * len(r_in)]
        if ride is not None:
            first, last = _grid_first_last(grid)
            copies = _ride_start((refs[6], refs[8 + len(r_in)], *refs[-3:]), ride[1], first)

        @pl.when(pl.program_id(1) == 0)
        def _():
            s_scr[...] = jnp.zeros_like(s_scr)

        heads = lambda ref: [ref[hh] for hh in range(h)]
        s0 = heads(s_scr)
        y, s1 = rwkv_chunk(s0, heads(r_ref), heads(lw_ref), heads(k_ref), heads(v_ref), heads(al_ref),
                           heads(be_ref))
        for hh in range(h):
            st_ref[hh] = s0[hh]
            y_ref[hh] = y[hh]
            s_scr[hh] = s1[hh]
        if ride is not None:
            _ride_wait(copies, last)

    seq = pl.BlockSpec((None, h, c, d), lambda b, i: (b, 0, i, 0))
    return pl.pallas_call(
        body, name="rwkv_scan_fwd", grid=grid, in_specs=[seq] * 6 + r_specs,
        out_specs=[seq, pl.BlockSpec((None, h, None, d, d), lambda b, i: (b, 0, i, 0, 0))] + r_ospecs,
        out_shape=[jax.ShapeDtypeStruct((bsz, h, t, d), F32), jax.ShapeDtypeStruct((bsz, h, nc, d, d), F32)] + r_out,
        scratch_shapes=[pltpu.VMEM((h, d, d), F32)] + r_scr,
        compiler_params=_params(("arbitrary", "arbitrary")),
    )(r, lw, k, v, al, be, *r_in)


def rwkv_scan_bwd(r, lw, k, v, al, be, states, dy, ride=None):
    bsz, h, t, d = r.shape
    c = RWKV_CHUNK
    nc = t // c
    grid = (bsz, nc)
    r_in, r_specs, r_out, r_ospecs, r_scr = _ride_args(ride)

    def body(*refs):
        r_ref, lw_ref, k_ref, v_ref, al_ref, be_ref, st_ref, dy_ref = refs[:8]
        nin = 8 + len(r_in)
        dr_ref, dlw_ref, dk_ref, dv_ref, dal_ref, dbe_ref = refs[nin:nin + 6]
        ds_scr = refs[nin + 6 + len(r_in)]
        if ride is not None:
            first, last = _grid_first_last(grid)
            copies = _ride_start((refs[8], refs[nin + 6], *refs[-3:]), ride[1], first)

        @pl.when(pl.program_id(1) == 0)
        def _():
            ds_scr[...] = jnp.zeros_like(ds_scr)

        heads = lambda ref: [ref[hh] for hh in range(h)]
        _, vjp = jax.vjp(rwkv_chunk, heads(st_ref), heads(r_ref), heads(lw_ref), heads(k_ref), heads(v_ref),
                         heads(al_ref), heads(be_ref))
        grads = vjp((heads(dy_ref), heads(ds_scr)))
        for ref, gl in zip((ds_scr, dr_ref, dlw_ref, dk_ref, dv_ref, dal_ref, dbe_ref), grads):
            for hh in range(h):
                ref[hh] = gl[hh]
        if ride is not None:
            _ride_wait(copies, last)

    seq = pl.BlockSpec((None, h, c, d), lambda b, i: (b, 0, nc - 1 - i, 0))
    st = pl.BlockSpec((None, h, None, d, d), lambda b, i: (b, 0, nc - 1 - i, 0, 0))
    return pl.pallas_call(
        body, name="rwkv_scan_bwd", grid=grid, in_specs=[seq] * 6 + [st, seq] + r_specs,
        out_specs=[seq] * 6 + r_ospecs, out_shape=[jax.ShapeDtypeStruct((bsz, h, t, d), F32)] * 6 + r_out,
        scratch_shapes=[pltpu.VMEM((h, d, d), F32)] + r_scr,
        compiler_params=_params(("arbitrary", "arbitrary")),
    )(r, lw, k, v, al, be, states, dy, *r_in)


def ssd_chunk(st, xs, bm, cm, dtr, dt_bias, a_log):
    n = SSM_CHUNK
    ii, jj = _iota((n, n), 0), _iota((n, n), 1)
    incl = ii >= jj
    lane = _iota((n, LANES), 1)
    dt = _softplus(dtr + dt_bias)
    a = dt * (-jnp.exp(a_log))
    acum = mmf(jnp.where(incl, 1.0, 0.0), a)
    last_row = jnp.where(jj == n - 1, 1.0, 0.0)
    cb = [mm3_nt(cm[g], bm[g]) for g in range(2)]
    pairs, heads = range(4), range(SSM_HEADS)
    e_m = [jnp.where(_iota((LANES, LANES), 0) == 2 * m + _iota((LANES, LANES), 1) // HEAD_DIM, 1.0, 0.0)
           for m in pairs]
    dt_m = [mmf(dt, e_m[m]) for m in pairs]
    ac_m = [mmf(acum, e_m[m]) for m in pairs]
    x = [xs[m] * dt_m[m] for m in pairs]
    last_m = [mmf(last_row, ac_m[m]) for m in pairs]
    colb = [mmf(acum, jnp.where(_iota((LANES, n), 0) == h, 1.0, 0.0)) for h in heads]
    decay = [jnp.exp(jnp.where(incl, colb[h] - colb[h].T, NEG_BIG)) for h in heads]
    yh = [mm3(cb[h // 4] * decay[h], x[h // 2]) for h in heads]
    y_off = [mm3(cm[m // 2], st[m]) for m in pairs]
    ys = [jnp.where(lane // HEAD_DIM == 0, yh[2 * m], yh[2 * m + 1]) + jnp.exp(ac_m[m]) * y_off[m] for m in pairs]
    st_in = [mm3_tn(bm[m // 2], x[m] * jnp.exp(last_m[m] - ac_m[m])) for m in pairs]
    st_new = [jnp.exp(last_m[m]) * st[m] + st_in[m] for m in pairs]
    return tuple(ys), tuple(st_new)


def _ssd_load(xbc_ref, dtr_ref):
    xs = tuple(xbc_ref[:, m * LANES:(m + 1) * LANES] for m in range(4))
    bm = tuple(xbc_ref[:, SSM_DIM + g * LANES:SSM_DIM + (g + 1) * LANES] for g in range(2))
    cm = tuple(xbc_ref[:, SSM_DIM + 2 * LANES + g * LANES:SSM_DIM + 2 * LANES + (g + 1) * LANES] for g in range(2))
    return xs, bm, cm, dtr_ref[...]


def ssd_fwd(xbc, proj, dt_col, dt_bias, a_log):
    bsz, t, _ = xbc.shape
    n = SSM_CHUNK
    nc = t // n

    def body(xbc_ref, dtr_ref, dtb_ref, al_ref, y_ref, st_ref, s_scr):
        @pl.when(pl.program_id(1) == 0)
        def _():
            s_scr[...] = jnp.zeros_like(s_scr)

        st = tuple(s_scr[m] for m in range(4))
        for m in range(4):
            st_ref[m] = st[m]
        xs, bm, cm, dtr = _ssd_load(xbc_ref, dtr_ref)
        ys, st_new = ssd_chunk(st, xs, bm, cm, dtr, dtb_ref[...], al_ref[...])
        for m in range(4):
            y_ref[:, m * LANES:(m + 1) * LANES] = ys[m]
            s_scr[m] = st_new[m]

    vec = pl.BlockSpec((1, LANES), lambda b, i: (0, 0))
    return pl.pallas_call(
        body, name="ssd_fwd", grid=(bsz, nc),
        in_specs=[pl.BlockSpec((None, n, 2 * SSM_DIM), lambda b, i: (b, i, 0)),
                  pl.BlockSpec((None, n, LANES), lambda b, i: (b, i, dt_col)), vec, vec],
        out_specs=[pl.BlockSpec((None, n, SSM_DIM), lambda b, i: (b, i, 0)),
                   pl.BlockSpec((None, None, 4, SSM_STATE, LANES), lambda b, i: (b, i, 0, 0, 0))],
        out_shape=[jax.ShapeDtypeStruct((bsz, t, SSM_DIM), F32),
                   jax.ShapeDtypeStruct((bsz, nc, 4, SSM_STATE, LANES), F32)],
        scratch_shapes=[pltpu.VMEM((4, SSM_STATE, LANES), F32)],
        compiler_params=_params(("parallel", "arbitrary")),
    )(xbc, proj, dt_bias, a_log)


def ssd_bwd(xbc, proj, dt_col, dt_bias, a_log, states, dy, dxs_extra):
    bsz, t, _ = xbc.shape
    n = SSM_CHUNK
    nc = t // n

    def body(xbc_ref, dtr_ref, dtb_ref, al_ref, st_ref, dy_ref, ex_ref,
             dxbc_ref, ddtr_ref, ddtb_ref, dal_ref, ds_scr):
        first = jnp.logical_and(pl.program_id(0) == 0, pl.program_id(1) == 0)

        @pl.when(pl.program_id(1) == 0)
        def _():
            ds_scr[...] = jnp.zeros_like(ds_scr)

        @pl.when(first)
        def _():
            ddtb_ref[...] = jnp.zeros_like(ddtb_ref)
            dal_ref[...] = jnp.zeros_like(dal_ref)

        st = tuple(st_ref[m] for m in range(4))
        xs, bm, cm, dtr = _ssd_load(xbc_ref, dtr_ref)
        _, vjp = jax.vjp(ssd_chunk, st, xs, bm, cm, dtr, dtb_ref[...], al_ref[...])
        dys = tuple(dy_ref[:, m * LANES:(m + 1) * LANES] for m in range(4))
        dst_in = tuple(ds_scr[m] for m in range(4))
        dst, dxs, dbm, dcm, ddtr, ddtb, dal = vjp((dys, dst_in))
        for m in range(4):
            ds_scr[m] = dst[m]
            sl = slice(m * LANES, (m + 1) * LANES)
            dxbc_ref[:, sl] = dxs[m] + ex_ref[:, sl]
        for g in range(2):
            dxbc_ref[:, SSM_DIM + g * LANES:SSM_DIM + (g + 1) * LANES] = dbm[g]
            dxbc_ref[:, SSM_DIM + 2 * LANES + g * LANES:SSM_DIM + 2 * LANES + (g + 1) * LANES] = dcm[g]
        ddtr_ref[...] = ddtr
        ddtb_ref[...] += ddtb
        dal_ref[...] += dal

    vec = pl.BlockSpec((1, LANES), lambda b, i: (0, 0))
    rev = lambda b, i: (b, nc - 1 - i, 0)
    return pl.pallas_call(
        body, name="ssd_bwd", grid=(bsz, nc),
        in_specs=[pl.BlockSpec((None, n, 2 * SSM_DIM), rev),
                  pl.BlockSpec((None, n, LANES), lambda b, i: (b, nc - 1 - i, dt_col)), vec, vec,
                  pl.BlockSpec((None, None, 4, SSM_STATE, LANES), lambda b, i: (b, nc - 1 - i, 0, 0, 0)),
                  pl.BlockSpec((None, n, SSM_DIM), rev), pl.BlockSpec((None, n, SSM_DIM), rev)],
        out_specs=[pl.BlockSpec((None, n, 2 * SSM_DIM), rev), pl.BlockSpec((None, n, LANES), rev), vec, vec],
        out_shape=[jax.ShapeDtypeStruct((bsz, t, 2 * SSM_DIM), F32), jax.ShapeDtypeStruct((bsz, t, LANES), F32),
                   jax.ShapeDtypeStruct((1, LANES), F32), jax.ShapeDtypeStruct((1, LANES), F32)],
        scratch_shapes=[pltpu.VMEM((4, SSM_STATE, LANES), F32)],
        compiler_params=_params(("arbitrary", "arbitrary")),
    )(xbc, proj, dt_bias, a_log, states, dy, dxs_extra)


def sb_block(q, kj, vj, carry, maskf):
    mask = maskf > 0.5
    z = _each(lambda q_, k_: dg(q_, k_, 1, 1, P_BF16) * (HEAD_DIM ** -0.5), q, kj)
    lk = _each(lambda z_: jnp.where(mask, -_softplus(z_), 0.0), z)
    sfx = _each(suffix_sum, lk)
    att = _each(lambda z_, c_, s_: jnp.exp(jnp.where(mask, z_ + c_ + s_, NEG_BIG)), z, carry, sfx)
    out = _each(mmb, att, vj)
    return out, _each(lambda c_, k_: c_ + jnp.sum(k_, axis=1, keepdims=True), carry, lk)


def _sb_mask(qi, j):
    n = Q_BLOCK
    return jnp.where(j * n + _iota((n, n), 1) < qi * n + _iota((n, n), 0), 1.0, 0.0)


def sb_fwd(q, k, v, ride=None):
    bsz, h, t, d = q.shape
    n = Q_BLOCK
    hp = ATTN_FWD_HEADS_PER_STEP
    grid = (bsz, h // hp, t // n)
    r_in, r_specs, r_out, r_ospecs, r_scr = _ride_args(ride)

    def body(*refs):
        q_ref, k_ref, v_ref = refs[:3]
        o_ref, c_ref = refs[3 + len(r_in):5 + len(r_in)]
        if ride is not None:
            first, last = _grid_first_last(grid)
            copies = _ride_start((refs[3], refs[5 + len(r_in)], *refs[-3:]), ride[1], first)
        qi = pl.program_id(2)
        lane = _iota((n, LANES), 1)

        c_ref[...] = jnp.zeros_like(c_ref)

        def step(i, state):
            acc, carry = state
            j = qi - i
            rows = pl.ds(pl.multiple_of(j * n, n), n)
            for hh in range(hp):
                c_ref[hh] = jnp.where(lane == j, carry[hh], c_ref[hh])
            o, carry = sb_block([q_ref[hh] for hh in range(hp)], [k_ref[hh, rows, :] for hh in range(hp)],
                                [v_ref[hh, rows, :] for hh in range(hp)], carry, _sb_mask(qi, j))
            return [a_ + o_ for a_, o_ in zip(acc, o)], carry

        init = ([jnp.zeros((n, d), F32) for _ in range(hp)], [jnp.zeros((n, 1), F32) for _ in range(hp)])
        acc, _ = lax.fori_loop(0, qi + 1, step, init)
        for hh in range(hp):
            o_ref[hh] = acc[hh]
        if ride is not None:
            _ride_wait(copies, last)

    blk = pl.BlockSpec((None, hp, n, d), lambda b, hg, i: (b, hg, i, 0))
    cblk = pl.BlockSpec((None, hp, n, LANES), lambda b, hg, i: (b, hg, i, 0))
    full = pl.BlockSpec((None, hp, t, d), lambda b, hg, i: (b, hg, 0, 0))
    return pl.pallas_call(
        body, name="sb_fwd", grid=grid, in_specs=[blk, full, full] + r_specs, out_specs=[blk, cblk] + r_ospecs,
        out_shape=[jax.ShapeDtypeStruct((bsz, h, t, d), F32), jax.ShapeDtypeStruct((bsz, h, t, LANES), F32)] + r_out,
        scratch_shapes=r_scr, compiler_params=_params(("arbitrary", "arbitrary", "arbitrary")),
    )(q, k, v, *r_in)


def sb_bwd(q, k, v, kept, do, ride=None):
    bsz, h, t, d = q.shape
    n = Q_BLOCK
    hp = SB_HEADS_PER_STEP
    grid = (bsz, h // hp, t // n)
    r_in, r_specs, r_out, r_ospecs, r_scr = _ride_args(ride)

    def body(*refs):
        q_ref, k_ref, v_ref, c_ref, do_ref = refs[:5]
        nin = 5 + len(r_in)
        dq_ref, dk_ref, dv_ref = refs[nin:nin + 3]
        if ride is not None:
            first, last = _grid_first_last(grid)
            copies = _ride_start((refs[5], refs[nin + 3], *refs[-3:]), ride[1], first)
        qi = pl.program_id(2)

        @pl.when(qi == 0)
        def _():
            dk_ref[...] = jnp.zeros_like(dk_ref)
            dv_ref[...] = jnp.zeros_like(dv_ref)

        heads = range(hp)
        qv = [q_ref[hh] for hh in heads]
        kept_v = [c_ref[hh] for hh in heads]
        lane = _iota((n, LANES), 1)

        def bwd_step(j, state):
            dq, dcarry = state
            rows = pl.ds(pl.multiple_of(j * n, n), n)
            carry_in = [jnp.sum(jnp.where(lane == j, t_, 0.0), axis=1, keepdims=True) for t_ in kept_v]
            _, vjp = jax.vjp(sb_block, qv, [k_ref[hh, rows, :] for hh in heads],
                             [v_ref[hh, rows, :] for hh in heads], carry_in, _sb_mask(qi, j))
            dqj, dkj, dvj, dc, _ = vjp(([do_ref[hh] for hh in heads], dcarry))
            for hh in heads:
                dk_ref[hh, rows, :] += dkj[hh]
                dv_ref[hh, rows, :] += dvj[hh]
            return [a_ + b_ for a_, b_ in zip(dq, dqj)], dc

        init = ([jnp.zeros((n, d), F32) for _ in heads], [jnp.zeros((n, 1), F32) for _ in heads])
        dq, _ = lax.fori_loop(0, qi + 1, bwd_step, init)
        for hh in heads:
            dq_ref[hh] = dq[hh]
        if ride is not None:
            _ride_wait(copies, last)

    blk = pl.BlockSpec((None, hp, n, d), lambda b, hg, i: (b, hg, i, 0))
    cblk = pl.BlockSpec((None, hp, n, LANES), lambda b, hg, i: (b, hg, i, 0))
    full = pl.BlockSpec((None, hp, t, d), lambda b, hg, i: (b, hg, 0, 0))
    shp = jax.ShapeDtypeStruct((bsz, h, t, d), F32)
    return pl.pallas_call(
        body, name="sb_bwd", grid=grid, in_specs=[blk, full, full, cblk, blk] + r_specs,
        out_specs=[blk, full, full] + r_ospecs, out_shape=[shp, shp, shp] + r_out,
        scratch_shapes=r_scr, compiler_params=_params(("arbitrary", "arbitrary", "arbitrary")),
    )(q, k, v, kept, do, *r_in)


def _bdot(a, b, ca, cb):
    return _dg_raw(a, b, ca, cb, P_BF16)


def _mla_scores(qn, qp, knj, kpj, qi, j):
    n = Q_BLOCK
    mask = j * n + _iota((n, n), 1) <= qi * n + _iota((n, n), 0)
    scale = (MLA_NOPE + MLA_ROPE) ** -0.5
    return _each(lambda a_, b_, k_: jnp.where(mask, (_bdot(a_, k_, 1, 1) + _bdot(b_, kpj, 1, 1)) * scale, NEG_BIG),
                 qn, qp, knj)


def _mla_specs(t, hp):
    n = Q_BLOCK
    return (pl.BlockSpec((None, hp, n, MLA_NOPE), lambda b, hg, i: (b, hg, i, 0)),
            pl.BlockSpec((None, hp, n, MLA_ROPE), lambda b, hg, i: (b, hg, i, 0)),
            pl.BlockSpec((None, hp, t, MLA_NOPE), lambda b, hg, i: (b, hg, 0, 0)),
            pl.BlockSpec((None, None, t, MLA_ROPE), lambda b, hg, i: (b, 0, 0, 0)),
            pl.BlockSpec((None, hp, n, 1), lambda b, hg, i: (b, hg, i, 0)))


def mla_fwd(qn, qp, kn, kp, v):
    bsz, h, t, _ = qn.shape
    n, hp = Q_BLOCK, ATTN_FWD_HEADS_PER_STEP
    heads = range(hp)

    def body(qn_ref, qp_ref, kn_ref, kp_ref, v_ref, o_ref, lse_ref):
        qi = pl.program_id(2)
        qn_v, qp_v = [qn_ref[hh] for hh in heads], [qp_ref[hh] for hh in heads]

        def step(j, state):
            m, l, acc = state
            rows = pl.ds(pl.multiple_of(j * n, n), n)
            s = _mla_scores(qn_v, qp_v, [kn_ref[hh, rows, :] for hh in heads], kp_ref[rows, :], qi, j)
            m_new = _each(lambda m_, s_: jnp.maximum(m_, jnp.max(s_, axis=1, keepdims=True)), m, s)
            p = _each(lambda s_, m_: jnp.exp(s_ - m_), s, m_new)
            corr = _each(lambda a_, b_: jnp.exp(a_ - b_), m, m_new)
            l = _each(lambda l_, c_, p_: l_ * c_ + jnp.sum(p_, axis=1, keepdims=True), l, corr, p)
            pv = _each(lambda p_, v_: _bdot(p_, v_, 1, 0), p, [v_ref[hh, rows, :] for hh in heads])
            acc = _each(lambda a_, c_, x_: a_ * c_ + x_, acc, corr, pv)
            return m_new, l, acc

        init = ([jnp.full((n, 1), NEG_BIG, F32) for _ in heads], [jnp.zeros((n, 1), F32) for _ in heads],
                [jnp.zeros((n, MLA_NOPE), F32) for _ in heads])
        m, l, acc = lax.fori_loop(0, qi + 1, step, init)
        for hh in heads:
            o_ref[hh] = acc[hh] / l[hh]
            lse_ref[hh] = m[hh] + jnp.log(l[hh])

    qn_s, qp_s, kn_s, kp_s, row_s = _mla_specs(t, hp)
    return pl.pallas_call(
        body, name="mla_fwd", grid=(bsz, h // hp, t // n), in_specs=[qn_s, qp_s, kn_s, kp_s, kn_s],
        out_specs=[qn_s, row_s],
        out_shape=[jax.ShapeDtypeStruct(qn.shape, F32), jax.ShapeDtypeStruct((bsz, h, t, 1), F32)],
        compiler_params=_params(("parallel", "parallel", "arbitrary")),
    )(qn, qp, kn, kp, v)


def mla_bwd(qn, qp, kn, kp, v, o, lse, do):
    bsz, h, t, _ = qn.shape
    n, hp = Q_BLOCK, MLA_HEADS_PER_STEP
    heads = range(hp)
    scale = (MLA_NOPE + MLA_ROPE) ** -0.5

    def body(qn_ref, qp_ref, kn_ref, kp_ref, v_ref, o_ref, lse_ref, do_ref,
             dqn_ref, dqp_ref, dkn_ref, dkp_ref, dv_ref):
        hg, qi = pl.program_id(1), pl.program_id(2)

        @pl.when(qi == 0)
        def _():
            dkn_ref[...] = jnp.zeros_like(dkn_ref)
            dv_ref[...] = jnp.zeros_like(dv_ref)

        @pl.when(jnp.logical_and(qi == 0, hg == 0))
        def _():
            dkp_ref[...] = jnp.zeros_like(dkp_ref)

        qn_v, qp_v = [qn_ref[hh] for hh in heads], [qp_ref[hh] for hh in heads]
        do_v, lse_v = [do_ref[hh] for hh in heads], [lse_ref[hh] for hh in heads]
        dsum = [jnp.sum(do_v[hh] * o_ref[hh], axis=1, keepdims=True) for hh in heads]

        def step(j, state):
            dqn, dqp = state
            rows = pl.ds(pl.multiple_of(j * n, n), n)
            knj, vj, kpj = [kn_ref[hh, rows, :] for hh in heads], [v_ref[hh, rows, :] for hh in heads], kp_ref[rows, :]
            s = _mla_scores(qn_v, qp_v, knj, kpj, qi, j)
            p = _each(lambda s_, l_: jnp.exp(s_ - l_), s, lse_v)
            dp = _each(lambda d_, v_: _bdot(d_, v_, 1, 1), do_v, vj)
            ds = _each(lambda p_, dp_, d_: p_ * (dp_ - d_) * scale, p, dp, dsum)
            dqn = _each(lambda a_, ds_, k_: a_ + _bdot(ds_, k_, 1, 0), dqn, ds, knj)
            dqp = _each(lambda a_, ds_: a_ + _bdot(ds_, kpj, 1, 0), dqp, ds)
            dkn = _each(lambda ds_, q_: _bdot(ds_, q_, 0, 0), ds, qn_v)
            dv = _each(lambda p_, d_: _bdot(p_, d_, 0, 0), p, do_v)
            dkp = _each(lambda ds_, q_: _bdot(ds_, q_, 0, 0), ds, qp_v)
            for hh in heads:
                dkn_ref[hh, rows, :] += dkn[hh]
                dv_ref[hh, rows, :] += dv[hh]
            dkp_ref[rows, :] += functools.reduce(lambda a_, b_: a_ + b_, dkp)
            return dqn, dqp

        init = ([jnp.zeros((n, MLA_NOPE), F32) for _ in heads], [jnp.zeros((n, MLA_ROPE), F32) for _ in heads])
        dqn, dqp = lax.fori_loop(0, qi + 1, step, init)
        for hh in heads:
            dqn_ref[hh] = dqn[hh]
            dqp_ref[hh] = dqp[hh]

    qn_s, qp_s, kn_s, kp_s, row_s = _mla_specs(t, hp)
    return pl.pallas_call(
        body, name="mla_bwd", grid=(bsz, h // hp, t // n),
        in_specs=[qn_s, qp_s, kn_s, kp_s, kn_s, qn_s, row_s, qn_s],
        out_specs=[qn_s, qp_s, kn_s, kp_s, kn_s],
        out_shape=[jax.ShapeDtypeStruct(qn.shape, F32), jax.ShapeDtypeStruct(qp.shape, F32),
                   jax.ShapeDtypeStruct(kn.shape, F32), jax.ShapeDtypeStruct(kp.shape, F32),
                   jax.ShapeDtypeStruct(v.shape, F32)],
        compiler_params=_params(("parallel", "arbitrary", "arbitrary")),
    )(qn, qp, kn, kp, v, o, lse, do)


def rope(name, x, pos, inv_freq, sign):
    bsz, hx, t, d = x.shape
    half = d // 2

    tt = _largest_tile(t, 512, SUBLANES)

    def body(x_ref, pos_ref, f_ref, o_ref):
        ang = pos_ref[...].astype(F32) * f_ref[...]
        cos, sin = jnp.cos(ang), sign * jnp.sin(ang)
        ri, ci = _iota((d, d), 0), _iota((d, d), 1)
        rot = jnp.where(ri == ci + half, -1.0, 0.0) + jnp.where(ri + half == ci, 1.0, 0.0)
        for hh in range(hx):
            xv = x_ref[hh]
            o_ref[hh] = xv * cos + mmf(xv, rot) * sin

    blk = pl.BlockSpec((None, hx, tt, d), lambda b, i: (b, 0, i, 0))
    return pl.pallas_call(
        body, name=name, grid=(bsz, t // tt),
        in_specs=[blk, pl.BlockSpec((None, tt, 1), lambda b, i: (b, i, 0)), pl.BlockSpec((1, d), lambda b, i: (0, 0))],
        out_specs=blk, out_shape=jax.ShapeDtypeStruct(x.shape, F32),
        compiler_params=_params(("parallel", "parallel")),
    )(x, pos, inv_freq)


def loss_head(h, target, tm):
    n, d = h.shape

    def body(h_ref, t_ref, dh_ref, l_ref):
        @pl.when(pl.program_id(0) == 0)
        def _():
            l_ref[...] = jnp.zeros_like(l_ref)

        e = h_ref[...] - t_ref[...]
        dh_ref[...] = e * (1.0 / d)
        l_ref[...] += jnp.sum(e * e, axis=(0, 1), keepdims=True) * (0.5 / d)

    row = pl.BlockSpec((tm, d), lambda i: (i, 0))
    dh, l = pl.pallas_call(
        body, name="loss_head", grid=(n // tm,), in_specs=[row, row],
        out_specs=[row, pl.BlockSpec((SUBLANES, LANES), lambda i: (0, 0))],
        out_shape=[jax.ShapeDtypeStruct((n, d), F32), jax.ShapeDtypeStruct((SUBLANES, LANES), F32)],
        compiler_params=_params(("arbitrary",)),
    )(h, target)
    return dh, l[0, 0]


def _exchange_copies(src_ref, out_ref, send_sems, recv_sems, local_sem, gather):
    x, y, c = lax.axis_index("x"), lax.axis_index("y"), lax.axis_index("c")
    me = 4 * x + 2 * y + c
    copies = [pltpu.make_async_copy(src_ref if gather else src_ref.at[me], out_ref.at[me], local_sem)]
    for m in range(1, N_DEV):
        px, py, pc = x ^ (m >> 2), y ^ ((m >> 1) & 1), c ^ (m & 1)
        peer = 4 * px + 2 * py + pc
        copies.append(pltpu.make_async_remote_copy(
            src_ref=src_ref if gather else src_ref.at[peer], dst_ref=out_ref.at[me],
            send_sem=send_sems.at[m], recv_sem=recv_sems.at[m],
            device_id=(px, py, pc), device_id_type=pl.DeviceIdType.MESH))
    return copies


def _exchange_start(copies):
    for cp in copies:
        cp.start()


def _exchange_wait(copies):
    for cp in copies[1:]:
        cp.wait_recv()
    for cp in copies[1:]:
        cp.wait_send()
    copies[0].wait()


EXCHANGE_SCRATCH = [pltpu.SemaphoreType.DMA((N_DEV,)), pltpu.SemaphoreType.DMA((N_DEV,)),
                    pltpu.SemaphoreType.DMA(())]


def _exchange_out(src):
    return jax.ShapeDtypeStruct((N_DEV, src.shape[-2], LANES), src.dtype)


def peer_exchange(name, src, gather):
    def body(src_ref, out_ref, send_sems, recv_sems, local_sem):
        copies = _exchange_copies(src_ref, out_ref, send_sems, recv_sems, local_sem, gather)
        _exchange_start(copies)
        _exchange_wait(copies)

    return pl.pallas_call(
        body, name=name,
        in_specs=[pl.BlockSpec(memory_space=pl.ANY)], out_specs=pl.BlockSpec(memory_space=pl.ANY),
        out_shape=_exchange_out(src), scratch_shapes=list(EXCHANGE_SCRATCH),
    )(src)


def _grid_first_last(grid):
    ids = [pl.program_id(a) for a in range(len(grid))]
    first = functools.reduce(jnp.logical_and, [i == 0 for i in ids])
    last = functools.reduce(jnp.logical_and, [i == g - 1 for i, g in zip(ids, grid)])
    return first, last


def _ride_start(refs, gather, first):
    copies = _exchange_copies(*refs, gather)

    @pl.when(first)
    def _():
        _exchange_start(copies)

    return copies


def _ride_wait(copies, last):
    @pl.when(last)
    def _():
        _exchange_wait(copies)


def _ride_args(ride):
    if ride is None:
        return [], [], [], [], []
    hbm = pl.BlockSpec(memory_space=pl.ANY)
    return [ride[0]], [hbm], [_exchange_out(ride[0])], [hbm], list(EXCHANGE_SCRATCH)


def adamw_sum(name, parts, w, m, v):
    r = w.shape[0]
    tr = ADAM_ROWS
    assert r % tr == 0

    def body(p_ref, w_ref, m_ref, v_ref, g_ref, d_ref, nm_ref, nv_ref):
        g = p_ref[0]
        for j in range(1, N_DEV):
            g = g + p_ref[j]
        mm_ = ADAM_B1 * m_ref[...] + (1.0 - ADAM_B1) * g
        vv = ADAM_B2 * v_ref[...] + (1.0 - ADAM_B2) * (g * g)
        m_hat = mm_ / (1.0 - ADAM_B1 ** ADAM_STEP)
        v_hat = vv / (1.0 - ADAM_B2 ** ADAM_STEP)
        g_ref[...] = g
        d_ref[...] = -ADAM_LR * (m_hat / (jnp.sqrt(v_hat) + ADAM_EPS) + ADAM_WD * w_ref[...])
        nm_ref[...] = mm_
        nv_ref[...] = vv

    row = pl.BlockSpec((tr, LANES), lambda i: (i, 0))
    shp = jax.ShapeDtypeStruct((r, LANES), F32)
    return pl.pallas_call(
        body, name=name, grid=(r // tr,),
        in_specs=[pl.BlockSpec((N_DEV, tr, LANES), lambda i: (0, i, 0)), row, row, row],
        out_specs=[row] * 4, out_shape=[shp] * 4,
        compiler_params=_params(("parallel",)),
    )(parts, w, m, v)


WEIGHTS = ['l0_w_in', 'rwkv_mix', 'rwkv_w0', 'rwkv_w2', 'rwkv_a0', 'rwkv_a2', 'rwkv_g2', 'rwkv_k_k', 'rwkv_k_a',
           'rwkv_r_k', 'rwkv_ln_g', 'rwkv_ln_b', 'ssm_conv_w', 'ssm_conv_b', 'ssm_dt_bias', 'ssm_a_log', 'ssm_d',
           'ssm_norm_g', 'l0_w_out', 'l0_ln1_g', 'l0_ln1_b', 'ffn0_w_up', 'ffn0_conv_w', 'ffn0_conv_b',
           'ffn0_w_down', 'l0_ln2_g', 'l0_ln2_b', 'l1_w_in', 'mla_q_norm_g', 'mla_w_uq', 'mla_kv_norm_g',
           'mla_w_ukv', 'l1_w_out', 'l1_ln1_g', 'l1_ln1_b', 'ffn1_w_up', 'ffn1_conv_w', 'ffn1_conv_b',
           'ffn1_w_down', 'l1_ln2_g', 'l1_ln2_b']
SHARD_AXIS = {'l0_w_in': 1, 'rwkv_w2': 1, 'rwkv_a2': 1, 'rwkv_g2': 1, 'ssm_conv_w': 1, 'l0_w_out': 0,
              'ffn0_w_up': 1, 'ffn0_conv_w': 1, 'ffn0_w_down': 0, 'l1_w_in': 1, 'mla_w_uq': 1, 'mla_w_ukv': 1,
              'l1_w_out': 0, 'ffn1_w_up': 1, 'ffn1_conv_w': 1, 'ffn1_w_down': 0}
MATMUL_W = ['l0_w_in', 'rwkv_w2', 'rwkv_a2', 'rwkv_g2', 'l0_w_out', 'ffn0_w_up', 'ffn0_w_down', 'l1_w_in',
            'mla_w_uq', 'mla_w_ukv', 'l1_w_out', 'ffn1_w_up', 'ffn1_w_down']
CONV_W = ['ssm_conv_w', 'ffn0_conv_w', 'ffn1_conv_w']
TOK_TILE = 256
ADAM_ROWS = 512


def _ceil_to(size, unit):
    return -(-size // unit) * unit


def _flat_rows(pieces, seg_rows, total_rows):
    unit = seg_rows * LANES
    out, total = [], 0
    for p in pieces:
        p = jnp.concatenate([q.reshape(-1) for q in p]) if isinstance(p, list) else p.reshape(-1)
        pad = _ceil_to(p.size, unit) - p.size
        out.append(jnp.pad(p, (0, pad)) if pad else p)
        total += p.size + pad
    tail = _ceil_to(total, total_rows * LANES) - total
    if tail:
        out.append(jnp.zeros((tail,), out[0].dtype))
    return jnp.concatenate(out).reshape(-1, LANES)


def _unflatten(flat2d, shapes, seg_rows):
    flat = flat2d.reshape(-1)
    out, off = [], 0
    for shp in shapes:
        if isinstance(shp, list):
            seg, pos = [], off
            for s_ in shp:
                seg.append(flat[pos:pos + math.prod(s_)].reshape(s_))
                pos += math.prod(s_)
            out.append(seg)
            size = pos - off
        else:
            size = math.prod(shp)
            out.append(flat[off:off + size].reshape(shp))
        off += _ceil_to(size, seg_rows * LANES)
    return out


def _by_size(names):
    return [nm for nm in names if nm in SHARD_AXIS], [nm for nm in names if nm not in SHARD_AXIS]


def _to_heads(t2, bsz, h):
    n, w = t2.shape
    return t2.reshape(bsz, n // bsz, h, w // h).transpose(0, 2, 1, 3)


def _from_heads(t4):
    b, h, t, d = t4.shape
    return t4.transpose(0, 2, 1, 3).reshape(b * t, h * d)


def _row(v):
    return v.reshape(1, -1)


def _pad_lanes(v):
    return jnp.pad(v.reshape(1, -1), ((0, 0), (0, LANES - v.size)))


def _local_step(a, w, comm=None):
    x = a['x']
    bsz, t, d = x.shape
    n = bsz * t
    tm = TOK_TILE
    pos = a['positions'].reshape(bsz, t, 1)
    inv_freq = 1.0 / (ROPE_THETA ** (jnp.arange(0, MLA_ROPE, 2, dtype=F32) / MLA_ROPE))
    inv_freq = jnp.concatenate([inv_freq, inv_freq]).reshape(1, MLA_ROPE)
    target = a['loss_target'].reshape(n, d)

    wi0 = w['l0_w_in']
    win0 = jnp.concatenate([wi0[:, 0:1536], wi0[:, 1792:3328], wi0[:, 1536:1792], wi0[:, 3328:3336],
                            jnp.zeros((d, L0_PAD - 3336), wi0.dtype)], axis=1)
    w2p = jnp.concatenate([w['rwkv_w2'], jnp.zeros_like(w['rwkv_w2'])], axis=0)
    a2p = jnp.concatenate([jnp.zeros_like(w['rwkv_a2']), w['rwkv_a2']], axis=0)
    mix = a['rwkv_mix']
    taps = jnp.stack([mix, 1.0 - mix])
    zero_b = jnp.zeros((1, mix.size), F32)
    rw_map = lambda j: j + jnp.where(j >= 12, 12, 0)
    ssm_map = lambda j: j + 16
    gate_map = lambda j: j
    up_map = lambda j: j + D_FF // LANES
    dt_col = 3328 // LANES
    dtb, alog, dsk = _pad_lanes(a['ssm_dt_bias']), _pad_lanes(a['ssm_a_log']), _pad_lanes(a['ssm_d'])
    pre_p = [_row(a['rwkv_w0']), w2p, _row(a['rwkv_a0']), a2p, w['rwkv_g2'], _row(a['rwkv_k_k']), _row(a['rwkv_k_a'])]
    post_p = [_row(a['rwkv_ln_g']), _row(a['rwkv_ln_b']), _row(a['rwkv_r_k'])]
    sp_p = [dsk, _row(a['ssm_norm_g'])]

    def ln(name, h, y, layer, which):
        ps = [_row(a[f'l{layer}_ln{which}_g']), _row(a[f'l{layer}_ln{which}_b'])]
        return tok_fwd(name, f_ln, [(h, d, 0), (y, d, 0)], ps, [d], tm, n, t)[0]

    def ffn_fwd(layer, h):
        up = mm(h, w[f'ffn{layer}_w_up'], 'nn', f'ffn{layer}_up')
        act = dwconv_fwd(f'ffn{layer}_conv', up.reshape(bsz, t, 2 * D_FF), gate_map, w[f'ffn{layer}_conv_w'],
                         _row(a[f'ffn{layer}_conv_b']), 3, True, upmap=up_map, out_dtype=BF16)
        act = act.reshape(n, D_FF)
        return up, act, mm(act, w[f'ffn{layer}_w_down'], 'nn', f'ffn{layer}_down')

    x2 = x.reshape(n, d)
    proj0 = mm(x2, win0, 'nn', 'l0_in')
    p0 = proj0.reshape(bsz, t, L0_PAD)
    xs_r = dwconv_fwd('rwkv_shift', p0, rw_map, taps, zero_b, 2, False).reshape(n, 1792)
    pre_x = [(xs_r, 512, 0), (xs_r, 512, 1), (xs_r, 512, 2), (xs_r, LANES, 12), (xs_r, LANES, 13)]
    heads64 = (RWKV_HEADS, HEAD_DIM)
    r_, v_, lw, kmod, al, be, gt = tok_fwd('rwkv_pre', f_rwkv_pre, pre_x, pre_p, [heads64] * 6 + [RWKV_DIM],
                                           tm, n, t)
    scan_in = [r_, lw, kmod, v_, al, be]
    if comm is None:
        y_h, rstates = rwkv_scan_fwd(*scan_in)
    else:
        y_h, rstates, got = rwkv_scan_fwd(*scan_in, ride=comm.weights_ride(MID_GATHER))
        w = {**w, **comm.weights(MID_GATHER, got)}
    wi1 = w['l1_w_in']
    win1 = jnp.concatenate([wi1, jnp.zeros((d, L1_PAD - 1952), wi1.dtype)], axis=1)
    wq3 = w['mla_w_uq'].reshape(-1, 8, MLA_NOPE + MLA_ROPE)
    wkv3 = w['mla_w_ukv'].reshape(-1, 8, 2 * MLA_NOPE)
    mla_p = [_row(a['mla_q_norm_g']), wq3[:, :, :MLA_NOPE].reshape(-1, 512), wq3[:, :, MLA_NOPE:].reshape(-1, 256),
             _row(a['mla_kv_norm_g']), wkv3[:, :, :MLA_NOPE].reshape(-1, 512), wkv3[:, :, MLA_NOPE:].reshape(-1, 512)]
    post_x = [y_h, r_, kmod, v_, (gt, 512, 0)]
    y_a = tok_fwd('rwkv_post', f_rwkv_post, post_x, post_p, [RWKV_DIM], tm, n, t)[0]
    xbc = dwconv_fwd('ssm_conv', p0, ssm_map, w['ssm_conv_w'], _row(a['ssm_conv_b']), 4, True)
    ys, sstates = ssd_fwd(xbc, p0, dt_col, dtb, alog)
    xbc2 = xbc.reshape(n, 2 * SSM_DIM)
    sp_x = [(ys.reshape(n, SSM_DIM), 512, 0), (xbc2, 512, 0), (proj0, 512, 3)]
    y_b = tok_fwd('ssd_post', f_ssd_post, sp_x, sp_p, [SSM_DIM], tm, n, t)[0]
    wo0 = w['l0_w_out']
    mixed0 = mm(y_b, wo0, 'nn', 'l0_out_b', b_rows=(512, 512), add=mm(y_a, wo0, 'nn', 'l0_out_a', b_rows=(0, 512)))
    h1 = ln('l0_ln1', x2, mixed0, 0, 1)
    up0, act0, f0 = ffn_fwd(0, h1)
    h2 = ln('l0_ln2', h1, f0, 0, 2)

    proj1 = mm(h2, win1, 'nn', 'l1_in')
    q_sb, k_sb, v_sb = tok_fwd('sb_split', f_same, [(proj1, 512, 0), (proj1, 512, 1), (proj1, 512, 2)], [],
                               [heads64] * 3, tm, n, t)
    if comm is None:
        o_c, sb_kept = sb_fwd(q_sb, k_sb, v_sb)
    else:
        o_c, sb_kept, got = sb_fwd(q_sb, k_sb, v_sb, ride=comm.weights_ride(FFN1_GATHER))
        w = {**w, **comm.weights(FFN1_GATHER, got)}
    mla_x = [(proj1, 256, 6), (proj1, LANES, 14)]
    qn, qp_raw, kn, vv = tok_fwd('mla_pre', f_mla_pre, mla_x, mla_p, [heads64, (8, MLA_ROPE), heads64, heads64],
                                 tm, n, t)
    kp_raw = proj1[:, 1920:1920 + MLA_ROPE].reshape(bsz, 1, t, MLA_ROPE)
    qp = rope('rope_q', qp_raw, pos, inv_freq, 1.0)
    kp = rope('rope_k', kp_raw, pos, inv_freq, 1.0)
    o_d, lse_d = mla_fwd(qn, qp, kn, kp, vv)
    y_cd = tok_fwd('attn_merge', f_concat, [o_c, o_d], [], [2 * RWKV_DIM], tm, n, t, out_dtype=BF16)[0]
    wo1 = w['l1_w_out']
    mixed1 = mm(y_cd, wo1, 'nn', 'l1_out')
    h3 = ln('l1_ln1', h2, mixed1, 1, 1)
    up1, act1, f1 = ffn_fwd(1, h3)
    h4 = ln('l1_ln2', h3, f1, 1, 2)
    dh4, loss = loss_head(h4, target, tm)

    g = {}

    def ln_bwd(name, h, y, layer, which, dout):
        ps = [_row(a[f'l{layer}_ln{which}_g']), _row(a[f'l{layer}_ln{which}_b'])]
        (dh, dy), (dg, db) = tok_bwd(name, f_ln, [(h, d, 0), (y, d, 0)], ps, [[dout]], tm, n, t)
        g[f'l{layer}_ln{which}_g'], g[f'l{layer}_ln{which}_b'] = dg.reshape(-1), db.reshape(-1)
        return dh, dy

    def ffn_bwd(layer, h, up, act, df, dh_res):
        wup, wdown = w[f'ffn{layer}_w_up'], w[f'ffn{layer}_w_down']
        g[f'ffn{layer}_w_down'] = mm(act, df, 'tn', f'ffn{layer}_dwdown')
        dact = mm(df, wdown, 'nt', f'ffn{layer}_dact').reshape(bsz, t, D_FF)
        dgate, dcw, dcb, dup = dwconv_bwd(f'ffn{layer}_conv_bwd', up.reshape(bsz, t, 2 * D_FF), gate_map,
                                          w[f'ffn{layer}_conv_w'], _row(a[f'ffn{layer}_conv_b']), 3, True, dact,
                                          upmap=up_map, grad_dtype=BF16)
        dgate, dup = dgate.reshape(n, D_FF), dup.reshape(n, D_FF)
        g[f'ffn{layer}_conv_w'], g[f'ffn{layer}_conv_b'] = dcw, dcb.reshape(-1)
        g[f'ffn{layer}_w_up'] = (mm(h, dgate, 'tn', f'ffn{layer}_dwgate'), mm(h, dup, 'tn', f'ffn{layer}_dwup'))
        dh = mm(dgate, wup, 'nt', f'ffn{layer}_dh_gate', add=dh_res, b_cols=(0, D_FF))
        return mm(dup, wup, 'nt', f'ffn{layer}_dh_up', add=dh, b_cols=(D_FF, D_FF))

    dh3_res, df1 = ln_bwd('l1_ln2_bwd', h3, f1, 1, 2, dh4)
    dh3 = ffn_bwd(1, h3, up1, act1, df1, dh3_res)
    dh2_res, dmixed1 = ln_bwd('l1_ln1_bwd', h2, mixed1, 1, 1, dh3)
    g['l1_w_out'] = mm(y_cd, dmixed1, 'tn', 'l1_dwout')
    dy_cd = mm(dmixed1, wo1, 'nt', 'l1_dy')
    dy_c, dy_d = tok_fwd('attn_split', f_same, [(dy_cd, 512, 0), (dy_cd, 512, 1)], [], [heads64] * 2, tm, n, t)
    parts = {}
    if comm is None:
        dq_sb, dk_sb, dv_sb = sb_bwd(q_sb, k_sb, v_sb, sb_kept, dy_c)
    else:
        dq_sb, dk_sb, dv_sb, parts['a'] = sb_bwd(q_sb, k_sb, v_sb, sb_kept, dy_c, ride=comm.grad_ride('a', g))
    dqn, dqp, dkn, dkp, dvv = mla_bwd(qn, qp, kn, kp, vv, o_d, lse_d, dy_d)
    dqp_raw = rope('rope_q_bwd', dqp, pos, inv_freq, -1.0)
    dkp_raw = rope('rope_k_bwd', dkp, pos, inv_freq, -1.0).reshape(n, MLA_ROPE)
    (dcq, dckv), (dqg, dwq_n, dwq_p, dkvg, dwk, dwv) = tok_bwd('mla_pre_bwd', f_mla_pre, mla_x, mla_p,
                                                               [[dqn], [dqp_raw], [dkn], [dvv]], tm, n, t)
    g['mla_q_norm_g'], g['mla_kv_norm_g'] = dqg.reshape(-1), dkvg.reshape(-1)
    g['mla_w_uq'] = jnp.concatenate([dwq_n.reshape(-1, 8, MLA_NOPE), dwq_p.reshape(-1, 8, MLA_ROPE)],
                                    axis=2).reshape(-1, 8 * (MLA_NOPE + MLA_ROPE))
    g['mla_w_ukv'] = jnp.concatenate([dwk.reshape(-1, 8, MLA_NOPE), dwv.reshape(-1, 8, MLA_NOPE)],
                                     axis=2).reshape(-1, 16 * MLA_NOPE)
    dkp_pad = jnp.pad(dkp_raw, ((0, 0), (0, LANES - MLA_ROPE)))
    dproj1 = tok_fwd('l1_dproj', f_concat, [dq_sb, dk_sb, dv_sb, (dcq, 256, 0), (dckv, LANES, 0),
                                            (dkp_pad, LANES, 0)], [], [L1_PAD], tm, n, t, out_dtype=BF16)[0]
    g['l1_w_in'] = mm(h2, dproj1, 'tn', 'l1_dwin')[:, :1952]
    dh2 = mm(dproj1, win1, 'nt', 'l1_dh', add=dh2_res)

    dh1_res, df0 = ln_bwd('l0_ln2_bwd', h1, f0, 0, 2, dh2)
    dh1 = ffn_bwd(0, h1, up0, act0, df0, dh1_res)
    dx_res, dmixed0 = ln_bwd('l0_ln1_bwd', x2, mixed0, 0, 1, dh1)
    g['l0_w_out'] = (mm(y_a, dmixed0, 'tn', 'l0_dwout_a'), mm(y_b, dmixed0, 'tn', 'l0_dwout_b'))
    dy_a = mm(dmixed0, wo0, 'nt', 'l0_dy_a', b_rows=(0, 512))
    dy_b = mm(dmixed0, wo0, 'nt', 'l0_dy_b', b_rows=(512, 512))
    (dy_r, dr1, dkm1, dv1, dgt), (dlng, dlnb, drk) = tok_bwd('rwkv_post_bwd', f_rwkv_post, post_x, post_p, [[dy_a]],
                                                            tm, n, t, dx_layouts=[heads64] * 4 + [None])
    g['rwkv_ln_g'], g['rwkv_ln_b'] = dlng.reshape(-1), dlnb.reshape(-1)
    g['rwkv_r_k'] = drk.reshape(RWKV_HEADS, HEAD_DIM)
    (dys, dxs_skip, dz), (ddsk, dng) = tok_bwd('ssd_post_bwd', f_ssd_post, sp_x, sp_p, [[dy_b]], tm, n, t)
    g['ssm_d'], g['ssm_norm_g'] = ddsk[0, :SSM_HEADS], dng.reshape(-1)
    dxbc_act, ddtr, ddtb, dalog = ssd_bwd(xbc, p0, dt_col, dtb, alog, sstates, dys.reshape(bsz, t, SSM_DIM),
                                          dxs_skip.reshape(bsz, t, SSM_DIM))
    g['ssm_dt_bias'], g['ssm_a_log'] = ddtb[0, :SSM_HEADS], dalog[0, :SSM_HEADS]
    dxbc, dscw, dscb = dwconv_bwd('ssm_conv_bwd', p0, ssm_map, w['ssm_conv_w'], _row(a['ssm_conv_b']), 4, True,
                                  dxbc_act)
    g['ssm_conv_w'], g['ssm_conv_b'] = dscw, dscb.reshape(-1)
    if comm is None:
        dscan = rwkv_scan_bwd(*scan_in, rstates, dy_r)
    else:
        *dscan, parts['b'] = rwkv_scan_bwd(*scan_in, rstates, dy_r, ride=comm.grad_ride('b', g))
    dr2, dlw, dk2, dv2, dal, dbe = dscan
    pre_ct = [[dr1, dr2], [dv1, dv2], [dlw], [dkm1, dk2], [dal], [dbe], [dgt]]
    dpre_x, dpre_p = tok_bwd('rwkv_pre_bwd', f_rwkv_pre, pre_x, pre_p, pre_ct, tm, n, t)
    g['rwkv_w0'], g['rwkv_a0'] = dpre_p[0].reshape(-1), dpre_p[2].reshape(-1)
    g['rwkv_w2'], g['rwkv_a2'], g['rwkv_g2'] = dpre_p[1][:64], dpre_p[3][64:], dpre_p[4]
    g['rwkv_k_k'], g['rwkv_k_a'] = dpre_p[5].reshape(-1), dpre_p[6].reshape(-1)
    dxs_r = jnp.concatenate(dpre_x, axis=1).reshape(bsz, t, 1792)
    d_rw, dtaps, _ = dwconv_bwd('rwkv_shift_bwd', p0, rw_map, taps, zero_b, 2, False, dxs_r)
    d_rw = d_rw.reshape(n, 1792)
    g['rwkv_mix'] = dtaps[0] - dtaps[1]
    dproj0 = tok_fwd('l0_dproj', f_concat, [(d_rw, 1536, 0), (dz, 512, 0), (dxbc.reshape(n, 2 * SSM_DIM), 1024, 0),
                                            (d_rw, 256, 6), (ddtr.reshape(n, LANES), LANES, 0)],
                     [], [L0_PAD], tm, n, t, out_dtype=BF16)[0]
    dwin0 = mm(x2, dproj0, 'tn', 'l0_dwin')
    g['l0_w_in'] = jnp.concatenate([dwin0[:, 0:1536], dwin0[:, 3072:3328], dwin0[:, 1536:3072],
                                    dwin0[:, 3328:3336]], axis=1)
    dx = mm(dproj0, win0, 'nt', 'l0_dx', add=dx_res)
    if comm is not None:
        parts['c'] = peer_exchange('grad_exchange_c', comm.grad_ride('c', g)[0], False)
    return loss, dx.reshape(bsz, t, d), g, parts


GRAD_GROUPS = {
    'a': ['ffn1_w_up', 'ffn1_conv_w', 'ffn1_conv_b', 'ffn1_w_down', 'l1_ln2_g', 'l1_ln2_b'],
    'c': ['l0_w_in', 'rwkv_mix', 'rwkv_w0', 'rwkv_w2', 'rwkv_a0', 'rwkv_a2', 'rwkv_g2', 'rwkv_k_k', 'rwkv_k_a'],
}
GRAD_GROUPS['b'] = [nm for nm in WEIGHTS if nm not in GRAD_GROUPS['a'] + GRAD_GROUPS['c']]
FIRST_GATHER = ['l0_w_in', 'rwkv_w2', 'rwkv_a2', 'rwkv_g2'] + CONV_W
MID_GATHER = ['l0_w_out', 'ffn0_w_up', 'ffn0_w_down', 'l1_w_in', 'mla_w_uq', 'mla_w_ukv', 'l1_w_out']
FFN1_GATHER = ['ffn1_w_up', 'ffn1_w_down']
BF16_ROWS = 16


class _Comm:
    def __init__(self, a):
        self.a = a

    def _pieces(self, names):
        return [lax.bitcast_convert_type(self.a[nm], BF16) if nm in CONV_W else self.a[nm].astype(BF16)
                for nm in names]

    def _unpack(self, names, got):
        shapes = [p.shape for p in self._pieces(names)]
        blocks = [_unflatten(got[k], shapes, BF16_ROWS) for k in range(N_DEV)]
        out = {}
        for i, nm in enumerate(names):
            blk = [blocks[k][i] for k in range(N_DEV)]
            if nm in CONV_W:
                blk = [lax.bitcast_convert_type(b, F32) for b in blk]
            out[nm] = jnp.concatenate(blk, axis=SHARD_AXIS[nm])
        return out

    def first_weights(self):
        return self.weights(FIRST_GATHER, peer_exchange('gather_first_weights', self.weights_ride(FIRST_GATHER)[0], True))

    def weights_ride(self, names):
        return _flat_rows(self._pieces(names), BF16_ROWS, BF16_ROWS), True

    def weights(self, names, got):
        return self._unpack(names, got)

    def grad_ride(self, group, g):
        def shard_of(nm, k):
            gv = g[nm]
            if nm not in SHARD_AXIS:
                return gv
            per = N_DEV
            if isinstance(gv, tuple):
                gv, k, per = gv[k // 4], k % 4, 4
            width = gv.shape[SHARD_AXIS[nm]] // per
            return lax.slice_in_dim(gv, k * width, (k + 1) * width, axis=SHARD_AXIS[nm])

        big, small = _by_size(GRAD_GROUPS[group])
        return jnp.stack([_flat_rows([shard_of(nm, k) for nm in big] + [[g[nm] for nm in small]], SUBLANES, ADAM_ROWS)
                          for k in range(N_DEV)]), False


def _step(a):
    comm = _Comm(a)
    loss, dx, _, parts = _local_step(a, comm.first_weights(), comm)
    loss = lax.psum(loss, ('x', 'y', 'c'))
    res = {}
    for group, names in GRAD_GROUPS.items():
        big, small = _by_size(names)
        flat = lambda prefix: _flat_rows([a[prefix + nm] for nm in big] + [[a[prefix + nm] for nm in small]],
                                         SUBLANES, ADAM_ROWS)
        outs = adamw_sum(f'adamw_{group}', parts[group], flat(''), flat('m_'), flat('v_'))
        shapes = [a[nm].shape for nm in big] + [[a[nm].shape for nm in small]]
        per_out = [_unflatten(o, shapes, SUBLANES) for o in outs]
        for i, nm in enumerate(big):
            res[nm] = [per_out[j][i] for j in range(4)]
        for i, nm in enumerate(small):
            res[nm] = [per_out[j][-1][i] for j in range(4)]
    return (loss, dx, *[res[nm][j] for j in range(4) for nm in WEIGHTS])


def kernel(x, positions, l0_w_in, rwkv_mix, rwkv_w0, rwkv_w2, rwkv_a0, rwkv_a2, rwkv_g2, rwkv_k_k, rwkv_k_a, rwkv_r_k, rwkv_ln_g, rwkv_ln_b, ssm_conv_w, ssm_conv_b, ssm_dt_bias, ssm_a_log, ssm_d, ssm_norm_g, l0_w_out, l0_ln1_g, l0_ln1_b, ffn0_w_up, ffn0_conv_w, ffn0_conv_b, ffn0_w_down, l0_ln2_g, l0_ln2_b, l1_w_in, mla_q_norm_g, mla_w_uq, mla_kv_norm_g, mla_w_ukv, l1_w_out, l1_ln1_g, l1_ln1_b, ffn1_w_up, ffn1_conv_w, ffn1_conv_b, ffn1_w_down, l1_ln2_g, l1_ln2_b, loss_target, m_l0_w_in, m_rwkv_mix, m_rwkv_w0, m_rwkv_w2, m_rwkv_a0, m_rwkv_a2, m_rwkv_g2, m_rwkv_k_k, m_rwkv_k_a, m_rwkv_r_k, m_rwkv_ln_g, m_rwkv_ln_b, m_ssm_conv_w, m_ssm_conv_b, m_ssm_dt_bias, m_ssm_a_log, m_ssm_d, m_ssm_norm_g, m_l0_w_out, m_l0_ln1_g, m_l0_ln1_b, m_ffn0_w_up, m_ffn0_conv_w, m_ffn0_conv_b, m_ffn0_w_down, m_l0_ln2_g, m_l0_ln2_b, m_l1_w_in, m_mla_q_norm_g, m_mla_w_uq, m_mla_kv_norm_g, m_mla_w_ukv, m_l1_w_out, m_l1_ln1_g, m_l1_ln1_b, m_ffn1_w_up, m_ffn1_conv_w, m_ffn1_conv_b, m_ffn1_w_down, m_l1_ln2_g, m_l1_ln2_b, v_l0_w_in, v_rwkv_mix, v_rwkv_w0, v_rwkv_w2, v_rwkv_a0, v_rwkv_a2, v_rwkv_g2, v_rwkv_k_k, v_rwkv_k_a, v_rwkv_r_k, v_rwkv_ln_g, v_rwkv_ln_b, v_ssm_conv_w, v_ssm_conv_b, v_ssm_dt_bias, v_ssm_a_log, v_ssm_d, v_ssm_norm_g, v_l0_w_out, v_l0_ln1_g, v_l0_ln1_b, v_ffn0_w_up, v_ffn0_conv_w, v_ffn0_conv_b, v_ffn0_w_down, v_l0_ln2_g, v_l0_ln2_b, v_l1_w_in, v_mla_q_norm_g, v_mla_w_uq, v_mla_kv_norm_g, v_mla_w_ukv, v_l1_w_out, v_l1_ln1_g, v_l1_ln1_b, v_ffn1_w_up, v_ffn1_conv_w, v_ffn1_conv_b, v_ffn1_w_down, v_l1_ln2_g, v_l1_ln2_b):
    return _step(dict(locals()))
```

```python
import functools
import math

import jax
import jax.numpy as jnp
from jax import lax
from jax.experimental import pallas as pl
from jax.experimental.pallas import tpu as pltpu

F32 = jnp.float32
BF16 = jnp.bfloat16
HI = lax.Precision.HIGHEST

V7X_VMEM_BYTES = 64 * 1024 * 1024
VMEM_LIMIT = V7X_VMEM_BYTES - 8 * 1024 * 1024
LANES = 128
SUBLANES = 8
N_DEV = 8

D_MODEL = 1024
HEAD_DIM = 64
RWKV_DIM = 512
RWKV_HEADS = 8
RWKV_GN_EPS = 64e-5
RWKV_CHUNK = 64
SSM_DIM = 512
SSM_HEADS = 8
SSM_CHUNK = 128
SSM_STATE = 128
Q_BLOCK = 128
SB_HEADS_PER_STEP = 4
MLA_HEADS_PER_STEP = 4
ATTN_FWD_HEADS_PER_STEP = 8
MLA_NOPE = 64
MLA_ROPE = 32
ROPE_THETA = 10000.0
D_FF = 2816
DEPTH = 2
ALPHA = (2 * DEPTH) ** 0.25
L0_PAD = 3456
L1_PAD = 2048

ADAM_LR = 0.001
ADAM_B1 = 0.9
ADAM_B2 = 0.999
ADAM_EPS = 1e-08
ADAM_WD = 0.01
ADAM_STEP = 10

NEG_BIG = -1e30


def _params(sem=None):
    return pltpu.CompilerParams(dimension_semantics=sem, vmem_limit_bytes=VMEM_LIMIT)


P_F32, P_BF16, P_BF16X3 = 0, 1, 2


def _dg_raw(a, b, ca, cb, fast):
    dims = (((ca,), (cb,)), ((), ()))
    if fast == P_BF16:
        return lax.dot_general(a.astype(BF16), b.astype(BF16), dims, preferred_element_type=F32)
    prec = HI if fast == P_F32 else lax.Precision.HIGH
    return lax.dot_general(a, b, dims, precision=prec, preferred_element_type=F32)


@functools.partial(jax.custom_vjp, nondiff_argnums=(2, 3, 4))
def dg(a, b, ca, cb, fast):
    return _dg_raw(a, b, ca, cb, fast)


def _dg_fwd(a, b, ca, cb, fast):
    return _dg_raw(a, b, ca, cb, fast), (a, b)


def _dg_bwd(ca, cb, fast, res, ct):
    a, b = res
    fa, fb = 1 - ca, 1 - cb
    da = _dg_raw(ct, b, 1, fb, fast) if ca == 1 else _dg_raw(b, ct, fb, 1, fast)
    db = _dg_raw(a, ct, fa, 0, fast) if cb == 0 else _dg_raw(ct, a, 0, fa, fast)
    return da.astype(a.dtype), db.astype(b.dtype)


dg.defvjp(_dg_fwd, _dg_bwd)


def mmb(a, b):
    return dg(a, b, 1, 0, P_BF16)


def mmf(a, b):
    return dg(a, b, 1, 0, P_F32)


def mmf_nt(a, b):
    return dg(a, b, 1, 1, P_F32)


def mmf_tn(a, b):
    return dg(a, b, 0, 0, P_F32)


def mm3(a, b):
    return dg(a, b, 1, 0, P_BF16X3)


def mm3_nt(a, b):
    return dg(a, b, 1, 1, P_BF16X3)


def mm3_tn(a, b):
    return dg(a, b, 0, 0, P_BF16X3)


def _split3_dot(x, m01, cb, terms=3):
    parts, rest = [], x
    for i in range(terms):
        parts.append(rest.astype(BF16))
        if i + 1 < terms:
            rest = rest - parts[-1].astype(F32)
    rows = x.shape[0]
    out = lax.dot_general(jnp.concatenate(parts, axis=0), m01.astype(BF16), (((1,), (cb,)), ((), ())),
                          preferred_element_type=F32)
    return functools.reduce(lambda a_, b_: a_ + b_, [out[i * rows:(i + 1) * rows] for i in range(terms)])


def _lower_ones(n):
    return jnp.where(_iota((n, n), 0) >= _iota((n, n), 1), 1.0, 0.0)


SUFFIX_TERMS = 2


@jax.custom_vjp
def suffix_sum(x):
    return _split3_dot(x, _lower_ones(x.shape[1]), 0, SUFFIX_TERMS)


def _suffix_sum_fwd(x):
    return suffix_sum(x), None


def _suffix_sum_bwd(_, ct):
    return (_split3_dot(ct, _lower_ones(ct.shape[1]), 1, SUFFIX_TERMS),)


suffix_sum.defvjp(_suffix_sum_fwd, _suffix_sum_bwd)


def _iota(shape, dim):
    return lax.broadcasted_iota(jnp.int32, shape, dim)


def _softplus(x):
    return jnp.maximum(x, 0.0) + jnp.log1p(jnp.exp(-jnp.abs(x)))


def _silu(x):
    return x * jax.nn.sigmoid(x)


def _largest_tile(n, cap, mult):
    best = None
    t = mult
    while t <= min(n, cap):
        if n % t == 0:
            best = t
        t += mult
    return n if best is None else best


MM_VMEM_BUDGET = 40 * 1024 * 1024
V7X_HBM_BYTES_PER_S = 3.2e12
GRID_STEP_S = 0.35e-6


def _mm_tiles(M, N, K, a_bytes, b_bytes, has_add):
    def divs(n):
        return [d for d in range(LANES, n + 1, LANES) if n % d == 0] or [n]

    best = None
    for tm in divs(M):
        for tn in divs(N):
            if tm * tn * 4 > 12 * 1024 * 1024:
                continue
            for tk in divs(K):
                vmem = (2 * (tm * tk * a_bytes + tk * tn * b_bytes) + 2 * tm * tn * 4 * (2 if has_add else 1)
                        + (tm * tk + tk * tn) * 2 + tm * tn * 4)
                if vmem > MM_VMEM_BUDGET:
                    continue
                ni, nj, nk = M // tm, N // tn, K // tk
                a_reads = M * K * a_bytes * (1 if nk == 1 else nj)
                b_reads = K * N * b_bytes * (1 if (nk == 1 and nj == 1) else ni)
                traffic = a_reads + b_reads + M * N * 4 * (2 if has_add else 1)
                cost = traffic / V7X_HBM_BYTES_PER_S + ni * nj * nk * GRID_STEP_S
                if min(tm, tn, tk) < 256 and min(M, N, K) >= 256:
                    cost *= 1.5
                if best is None or cost < best[0]:
                    best = (cost, tm, tn, tk)
    return best[1:]


def mm(a, b, mode, name, add=None, b_rows=None, b_cols=None):
    r0, nr = b_rows or (0, b.shape[0])
    c0, nc = b_cols or (0, b.shape[1])
    if mode == "nn":
        (M, K), N = a.shape, nc
        assert nr == K
    elif mode == "nt":
        (M, K), N = a.shape, nr
        assert nc == K
    else:
        (K, M), N = a.shape, b.shape[1]
        assert b_rows is None and b_cols is None
    has_add = add is not None
    tm, tn, tk = _mm_tiles(M, N, K, a.dtype.itemsize, b.dtype.itemsize, has_add)
    nk = K // tk
    keep_a = nk == 1 and N // tn > 1 and a.dtype != BF16
    if mode == "nn":
        assert r0 % tk == 0 and c0 % tn == 0
        a_spec = pl.BlockSpec((tm, tk), lambda i, j, k: (i, k))
        b_spec = pl.BlockSpec((tk, tn), lambda i, j, k: (k + r0 // tk, j + c0 // tn))
        dims = (((1,), (0,)), ((), ()))
    elif mode == "nt":
        assert r0 % tn == 0 and c0 % tk == 0
        a_spec = pl.BlockSpec((tm, tk), lambda i, j, k: (i, k))
        b_spec = pl.BlockSpec((tn, tk), lambda i, j, k: (j + r0 // tn, k + c0 // tk))
        dims = (((1,), (1,)), ((), ()))
    else:
        a_spec = pl.BlockSpec((tk, tm), lambda i, j, k: (k, i))
        b_spec = pl.BlockSpec((tk, tn), lambda i, j, k: (k, j))
        dims = (((0,), (0,)), ((), ()))
    o_spec = pl.BlockSpec((tm, tn), lambda i, j, k: (i, j))

    def body(a_ref, b_ref, *rest):
        o_ref = rest[1] if has_add else rest[0]
        k = pl.program_id(2)
        if keep_a:
            a_bf = rest[-1]

            @pl.when(pl.program_id(1) == 0)
            def _():
                a_bf[...] = a_ref[...].astype(BF16)

            av = a_bf[...]
        else:
            av = a_ref[...].astype(BF16)
        part = lax.dot_general(av, b_ref[...].astype(BF16), dims, preferred_element_type=F32)

        @pl.when(k == 0)
        def _():
            o_ref[...] = part + rest[0][...] if has_add else part

        @pl.when(k > 0)
        def _():
            o_ref[...] += part

    ins = [a, b] + ([add] if has_add else [])
    specs = [a_spec, b_spec] + ([o_spec] if has_add else [])
    return pl.pallas_call(
        body, name=name, grid=(M // tm, N // tn, nk), in_specs=specs, out_specs=o_spec,
        out_shape=jax.ShapeDtypeStruct((M, N), F32),
        scratch_shapes=[pltpu.VMEM(a_spec.block_shape, BF16)] if keep_a else [],
        compiler_params=_params(("parallel", "arbitrary", "arbitrary")),
    )(*ins)


def _is_heads(x):
    return not isinstance(x, tuple)


def _tok_arr(x):
    return x if _is_heads(x) else x[0]


def _tok_width(x):
    return x.shape[1] * x.shape[3] if _is_heads(x) else x[1]


def _heads_spec(h, dh, tm, tiles_per_seq):
    return pl.BlockSpec((None, h, tm, dh), lambda i: (i // tiles_per_seq, 0, i % tiles_per_seq, 0))


def _x_spec(x, tm, tiles_per_seq):
    if _is_heads(x):
        return _heads_spec(x.shape[1], x.shape[3], tm, tiles_per_seq)
    return pl.BlockSpec((tm, x[1]), functools.partial(lambda i, cb: (i, cb), cb=x[2]))


def _out_spec_shape(layout, n, seq, tm, dtype=F32):
    if isinstance(layout, tuple):
        h, dh = layout
        return _heads_spec(h, dh, tm, seq // tm), jax.ShapeDtypeStruct((n // seq, h, seq, dh), dtype)
    return pl.BlockSpec((tm, layout), lambda i: (i, 0)), jax.ShapeDtypeStruct((n, layout), dtype)


def _tok_load(ref):
    if len(ref.shape) == 3:
        return jnp.concatenate([ref[hh] for hh in range(ref.shape[0])], axis=1)
    return ref[...]


def _tok_store(ref, val):
    if len(ref.shape) == 3:
        dh = ref.shape[2]
        for hh in range(ref.shape[0]):
            ref[hh] = val[:, hh * dh:(hh + 1) * dh]
    else:
        ref[...] = val


def _p_specs(ps):
    return [pl.BlockSpec(p.shape, lambda i: (0, 0)) for p in ps]


def tok_fwd(name, f, xs, ps, out_layouts, tm, n, seq, out_dtype=F32):
    nx, npar = len(xs), len(ps)
    outs = [_out_spec_shape(lay, n, seq, tm, out_dtype) for lay in out_layouts]

    def body(*refs):
        xv = [_tok_load(r) for r in refs[:nx]]
        pv = [r[...].astype(F32) for r in refs[nx:nx + npar]]
        for o, r in zip(f(*xv, *pv), refs[nx + npar:]):
            _tok_store(r, o.astype(out_dtype))

    return pl.pallas_call(
        body, name=name, grid=(n // tm,),
        in_specs=[_x_spec(x, tm, seq // tm) for x in xs] + _p_specs(ps),
        out_specs=[o[0] for o in outs], out_shape=[o[1] for o in outs],
        compiler_params=_params(("parallel",)),
    )(*[_tok_arr(x) for x in xs], *ps)


def tok_bwd(name, f, xs, ps, cts, tm, n, seq, dx_layouts=None):
    nx, npar = len(xs), len(ps)
    ct_flat = [c for group in cts for c in group]
    nct = len(ct_flat)
    dx_layouts = dx_layouts or [None] * nx
    dxs = [_out_spec_shape(lay if lay else _tok_width(x), n, seq, tm) for x, lay in zip(xs, dx_layouts)]

    def body(*refs):
        xv = [_tok_load(r) for r in refs[:nx]]
        pv = [r[...].astype(F32) for r in refs[nx:nx + npar]]
        ct_refs = refs[nx + npar:nx + npar + nct]
        dx_refs = refs[nx + npar + nct:nx + npar + nct + nx]
        dp_refs = refs[nx + npar + nct + nx:]
        cv, pos = [], 0
        for group in cts:
            acc = _tok_load(ct_refs[pos])
            for r in ct_refs[pos + 1:pos + len(group)]:
                acc = acc + _tok_load(r)
            cv.append(acc)
            pos += len(group)
        _, vjp = jax.vjp(f, *xv, *pv)
        grads = vjp(tuple(cv))
        for g, r in zip(grads[:nx], dx_refs):
            _tok_store(r, g)

        @pl.when(pl.program_id(0) == 0)
        def _():
            for r in dp_refs:
                r[...] = jnp.zeros_like(r)

        for g, r in zip(grads[nx:], dp_refs):
            r[...] += g

    ct_specs = [_heads_spec(c.shape[1], c.shape[3], tm, seq // tm) if c.ndim == 4
                else pl.BlockSpec((tm, c.shape[1]), lambda i: (i, 0)) for c in ct_flat]
    outs = pl.pallas_call(
        body, name=name, grid=(n // tm,),
        in_specs=[_x_spec(x, tm, seq // tm) for x in xs] + _p_specs(ps) + ct_specs,
        out_specs=[d[0] for d in dxs] + _p_specs(ps),
        out_shape=[d[1] for d in dxs] + [jax.ShapeDtypeStruct(p.shape, F32) for p in ps],
        compiler_params=_params(("arbitrary",)),
    )(*[_tok_arr(x) for x in xs], *ps, *ct_flat)
    return outs[:nx], outs[nx:]


def f_ln(h, y, g, b):
    pre = ALPHA * h + y
    mu = jnp.mean(pre, axis=-1, keepdims=True)
    xc = pre - mu
    var = jnp.mean(xc * xc, axis=-1, keepdims=True)
    return (xc * lax.rsqrt(var + 1e-5) * g + b,)


def _head_sel(width, nheads_pad, per):
    return jnp.where(_iota((width, nheads_pad), 0) // per == _iota((width, nheads_pad), 1), 1.0, 0.0).astype(F32)


def _head_sel_t(nheads_pad, width, per):
    return jnp.where(_iota((nheads_pad, width), 1) // per == _iota((nheads_pad, width), 0), 1.0, 0.0).astype(F32)


@jax.custom_vjp
def head_sum(x):
    return _split3_dot(x, _head_sel(RWKV_DIM, LANES, HEAD_DIM), 0)


@jax.custom_vjp
def head_spread(y):
    return _split3_dot(y, _head_sel(RWKV_DIM, LANES, HEAD_DIM), 1)


head_sum.defvjp(lambda x: (head_sum(x), None), lambda _, ct: (head_spread(ct),))
head_spread.defvjp(lambda y: (head_spread(y), None), lambda _, ct: (head_sum(ct),))


def f_rwkv_pre(r, k, v, lora, glo, w0, w2p, a0, a2p, g2, k_k, k_a):
    lane = _iota(lora.shape, 1)
    tw = jnp.where(lane < 64, jnp.tanh(lora), 0.0)
    ta = jnp.where(lane >= 64, lora, 0.0)
    log_w = -_softplus(-(w0 + mmb(tw, w2p))) - 0.5
    lw = -jnp.exp(log_w)
    a = jax.nn.sigmoid(a0 + mmb(ta, a2p))
    g = mmb(jax.nn.sigmoid(glo), g2)
    kk = k * k_k
    nrm = jnp.sqrt(jnp.maximum(head_sum(kk * kk), 1e-24))
    kkn = kk * head_spread(1.0 / nrm)
    kmod = k * (1.0 + (a - 1.0) * k_a)
    return r, v, lw, kmod, -kkn, kkn * a, g


def f_rwkv_post(y, r, kmod, v, g, ln_g, ln_b, r_k):
    inv = 1.0 / HEAD_DIM
    mu = head_spread(head_sum(y) * inv)
    yc = y - mu
    var = head_sum(yc * yc) * inv
    rstd = head_spread(lax.rsqrt(var + RWKV_GN_EPS))
    yn = yc * rstd * ln_g + ln_b
    bonus = head_spread(head_sum(r * kmod * r_k)) * v
    return ((yn + bonus) * g,)


def f_ssd_post(y, xs, z, d_skip, norm_g):
    sel_t = _head_sel_t(LANES, SSM_DIM, HEAD_DIM)
    d_e = jnp.sum(mmf(jnp.broadcast_to(d_skip, (SUBLANES, LANES)), sel_t), axis=0, keepdims=True) * (1.0 / SUBLANES)
    u = (y + xs * d_e) * _silu(z)
    first = _iota(u.shape, 1) < (SSM_DIM // 2)
    uu = u * u
    inv = 2.0 / SSM_DIM
    ms0 = jnp.sum(jnp.where(first, uu, 0.0), axis=-1, keepdims=True) * inv
    ms1 = jnp.sum(jnp.where(first, 0.0, uu), axis=-1, keepdims=True) * inv
    ms = jnp.where(first, ms0, ms1)
    return (u * lax.rsqrt(ms + 1e-5) * norm_g,)


def f_mla_pre(cq, ckv, qg, wq_nope, wq_rope, kvg, wk_nope, wv):
    def rms(x, g):
        return x * lax.rsqrt(jnp.mean(x * x, axis=-1, keepdims=True) + 1e-6) * g
    q_in, kv_in = rms(cq, qg), rms(ckv, kvg)
    return mmb(q_in, wq_nope), mmb(q_in, wq_rope), mmb(kv_in, wk_nope), mmb(kv_in, wv)


def f_same(*xs):
    return xs


def f_concat(*xs):
    return (jnp.concatenate(xs, axis=1),)


def _shift_down(x, s, row):
    return x if s == 0 else jnp.where(row >= s, pltpu.roll(x, s, 0), 0.0)


def _shift_up(x, s, row, t):
    return x if s == 0 else jnp.where(row < t - s, pltpu.roll(x, t - s, 0), 0.0)


def dwconv_fwd(name, u, colmap, w, b, taps, silu, upmap=None, out_dtype=F32):
    bsz, t, _ = u.shape
    c = w.shape[1]
    tc = LANES
    has_up = upmap is not None

    def body(*refs):
        u_ref, w_ref, b_ref = refs[:3]
        o_ref = refs[-1]
        uv = u_ref[...]
        wv = w_ref[...]
        row = _iota(uv.shape, 0)
        acc = jnp.broadcast_to(b_ref[...], uv.shape)
        for i in range(taps):
            acc = acc + wv[i:i + 1, :] * _shift_down(uv, taps - 1 - i, row)
        if silu:
            acc = _silu(acc)
        if has_up:
            acc = acc * refs[3][...]
        o_ref[...] = acc.astype(out_dtype)

    specs = [pl.BlockSpec((None, t, tc), lambda bb, j: (bb, 0, colmap(j))),
             pl.BlockSpec((taps, tc), lambda bb, j: (0, j)),
             pl.BlockSpec((1, tc), lambda bb, j: (0, j))]
    ins = [u, w, b]
    if has_up:
        specs.append(pl.BlockSpec((None, t, tc), lambda bb, j: (bb, 0, upmap(j))))
        ins.append(u)
    return pl.pallas_call(
        body, name=name, grid=(bsz, c // tc), in_specs=specs,
        out_specs=pl.BlockSpec((None, t, tc), lambda bb, j: (bb, 0, j)),
        out_shape=jax.ShapeDtypeStruct((bsz, t, c), out_dtype),
        compiler_params=_params(("parallel", "parallel")),
    )(*ins)


def dwconv_bwd(name, u, colmap, w, b, taps, silu, dout, upmap=None, grad_dtype=F32):
    bsz, t, _ = u.shape
    c = w.shape[1]
    tc = LANES
    has_up = upmap is not None

    def body(*refs):
        u_ref, w_ref, b_ref, d_ref = refs[:4]
        nin = 5 if has_up else 4
        du_ref, dw_ref, db_ref = refs[nin:nin + 3]
        uv = u_ref[...]
        wv = w_ref[...]
        dv = d_ref[...]
        row = _iota(uv.shape, 0)
        shifted = [_shift_down(uv, taps - 1 - i, row) for i in range(taps)]
        cg = jnp.broadcast_to(b_ref[...], uv.shape)
        for i in range(taps):
            cg = cg + wv[i:i + 1, :] * shifted[i]
        if silu:
            sg = jax.nn.sigmoid(cg)
            act = cg * sg
            dact_dcg = sg * (1.0 + cg * (1.0 - sg))
        else:
            act = cg
            dact_dcg = None
        if has_up:
            refs[nin + 3][...] = (dv * act).astype(grad_dtype)
            dv = dv * refs[4][...]
        dcg = dv * dact_dcg if silu else dv
        du = jnp.zeros_like(uv)
        for i in range(taps):
            du = du + wv[i:i + 1, :] * _shift_up(dcg, taps - 1 - i, row, t)
        du_ref[...] = du.astype(grad_dtype)

        @pl.when(pl.program_id(1) == 0)
        def _():
            dw_ref[...] = jnp.zeros_like(dw_ref)
            db_ref[...] = jnp.zeros_like(db_ref)

        for i in range(taps):
            dw_ref[i:i + 1, :] += jnp.sum(dcg * shifted[i], axis=0, keepdims=True)
        db_ref[...] += jnp.sum(dcg, axis=0, keepdims=True)

    specs = [pl.BlockSpec((None, t, tc), lambda j, bb: (bb, 0, colmap(j))),
             pl.BlockSpec((taps, tc), lambda j, bb: (0, j)),
             pl.BlockSpec((1, tc), lambda j, bb: (0, j)),
             pl.BlockSpec((None, t, tc), lambda j, bb: (bb, 0, j))]
    ins = [u, w, b, dout]
    if has_up:
        specs.append(pl.BlockSpec((None, t, tc), lambda j, bb: (bb, 0, upmap(j))))
        ins.append(u)
    big = pl.BlockSpec((None, t, tc), lambda j, bb: (bb, 0, j))
    out_specs = [big, pl.BlockSpec((taps, tc), lambda j, bb: (0, j)), pl.BlockSpec((1, tc), lambda j, bb: (0, j))]
    out_shape = [jax.ShapeDtypeStruct((bsz, t, c), grad_dtype), jax.ShapeDtypeStruct((taps, c), F32),
                 jax.ShapeDtypeStruct((1, c), F32)]
    if has_up:
        out_specs.append(big)
        out_shape.append(jax.ShapeDtypeStruct((bsz, t, c), grad_dtype))
    return pl.pallas_call(
        body, name=name, grid=(c // tc, bsz), in_specs=specs, out_specs=out_specs, out_shape=out_shape,
        compiler_params=_params(("parallel", "arbitrary")),
    )(*ins)


def _each(f, *lists):
    return [f(*xs) for xs in zip(*lists)]


def rwkv_chunk(s0, r, lw, k, v, al, be):
    c = r[0].shape[0]
    ii, jj = _iota((c, c), 0), _iota((c, c), 1)
    incl, strict = ii >= jj, ii > jj
    ones_incl = jnp.where(incl, 1.0, 0.0)
    eye = jnp.where(ii == jj, 1.0, 0.0)
    cum = _each(lambda x: mmf(ones_incl, x), lw)
    gam_inv = _each(lambda x: jnp.exp(-x), cum)
    at = _each(lambda a_, c_, l_: a_ * jnp.exp(c_ - l_), al, cum, lw)
    rt = _each(lambda r_, c_: r_ * jnp.exp(c_), r, cum)
    bt = _each(lambda b_, g_: b_ * g_, be, gam_inv)
    kt = _each(lambda k_, g_: k_ * g_, k, gam_inv)
    a_b = _each(lambda x, y_: jnp.where(strict, mm3_nt(x, y_), 0.0), at, bt)
    a_k = _each(lambda x, y_: jnp.where(strict, mm3_nt(x, y_), 0.0), at, kt)
    rhs0 = _each(mm3_nt, at, s0)
    rhs = _each(lambda x, a_, v_: x + mm3(a_, v_), rhs0, a_k, v)
    p = _each(lambda x: eye + x, a_b)
    m = a_b
    for _ in range(int(math.log2(c)) - 1):
        m = _each(mm3, m, m)
        p = _each(lambda p_, m_: p_ + mm3(p_, m_), p, m)
    u = _each(mm3, p, rhs)
    r_b = _each(lambda x, y_: jnp.where(incl, mm3_nt(x, y_), 0.0), rt, bt)
    r_k = _each(lambda x, y_: jnp.where(incl, mm3_nt(x, y_), 0.0), rt, kt)
    y0 = _each(mm3_nt, rt, s0)
    y1 = _each(lambda y_, b_, u_: y_ + mm3(b_, u_), y0, r_b, u)
    y = _each(lambda y_, k_, v_: y_ + mm3(k_, v_), y1, r_k, v)
    su = _each(mm3_tn, u, bt)
    sv = _each(mm3_tn, v, kt)
    s1 = _each(lambda s_, a_, b_, l_: (s_ + a_ + b_) * jnp.exp(jnp.sum(l_, axis=0, keepdims=True)), s0, su, sv, lw)
    return y, s1


def rwkv_scan_fwd(r, lw, k, v, al, be, ride=None):
    bsz, h, t, d = r.shape
    c = RWKV_CHUNK
    nc = t // c
    grid = (bsz, nc)
    r_in, r_specs, r_out, r_ospecs, r_scr = _ride_args(ride)

    def body(*refs):
        r_ref, lw_ref, k_ref, v_ref, al_ref, be_ref = refs[:6]
        y_ref, st_ref = refs[6 + len(r_in):8 + len(r_in)]
        s_scr = refs[8 + 2 * len(r_in)]
        if ride is not None:
            first, last = _grid_first_last(grid)
            copies = _ride_start((refs[6], refs[8 + len(r_in)], *refs[-3:]), ride[1], first)

        @pl.when(pl.program_id(1) == 0)
        def _():
            s_scr[...] = jnp.zeros_like(s_scr)

        heads = lambda ref: [ref[hh] for hh in range(h)]
        s0 = heads(s_scr)
        y, s1 = rwkv_chunk(s0, heads(r_ref), heads(lw_ref), heads(k_ref), heads(v_ref), heads(al_ref),
                           heads(be_ref))
        for hh in range(h):
            st_ref[hh] = s0[hh]
            y_ref[hh] = y[hh]
            s_scr[hh] = s1[hh]
        if ride is not None:
            _ride_wait(copies, last)

    seq = pl.BlockSpec((None, h, c, d), lambda b, i: (b, 0, i, 0))
    return pl.pallas_call(
        body, name="rwkv_scan_fwd", grid=grid, in_specs=[seq] * 6 + r_specs,
        out_specs=[seq, pl.BlockSpec((None, h, None, d, d), lambda b, i: (b, 0, i, 0, 0))] + r_ospecs,
        out_shape=[jax.ShapeDtypeStruct((bsz, h, t, d), F32), jax.ShapeDtypeStruct((bsz, h, nc, d, d), F32)] + r_out,
        scratch_shapes=[pltpu.VMEM((h, d, d), F32)] + r_scr,
        compiler_params=_params(("arbitrary", "arbitrary")),
    )(r, lw, k, v, al, be, *r_in)


def rwkv_scan_bwd(r, lw, k, v, al, be, states, dy, ride=None):
    bsz, h, t, d = r.shape
    c = RWKV_CHUNK
    nc = t // c
    grid = (bsz, nc)
    r_in, r_specs, r_out, r_ospecs, r_scr = _ride_args(ride)

    def body(*refs):
        r_ref, lw_ref, k_ref, v_ref, al_ref, be_ref, st_ref, dy_ref = refs[:8]
        nin = 8 + len(r_in)
        dr_ref, dlw_ref, dk_ref, dv_ref, dal_ref, dbe_ref = refs[nin:nin + 6]
        ds_scr = refs[nin + 6 + len(r_in)]
        if ride is not None:
            first, last = _grid_first_last(grid)
            copies = _ride_start((refs[8], refs[nin + 6], *refs[-3:]), ride[1], first)

        @pl.when(pl.program_id(1) == 0)
        def _():
            ds_scr[...] = jnp.zeros_like(ds_scr)

        heads = lambda ref: [ref[hh] for hh in range(h)]
        _, vjp = jax.vjp(rwkv_chunk, heads(st_ref), heads(r_ref), heads(lw_ref), heads(k_ref), heads(v_ref),
                         heads(al_ref), heads(be_ref))
        grads = vjp((heads(dy_ref), heads(ds_scr)))
        for ref, gl in zip((ds_scr, dr_ref, dlw_ref, dk_ref, dv_ref, dal_ref, dbe_ref), grads):
            for hh in range(h):
                ref[hh] = gl[hh]
        if ride is not None:
            _ride_wait(copies, last)

    seq = pl.BlockSpec((None, h, c, d), lambda b, i: (b, 0, nc - 1 - i, 0))
    st = pl.BlockSpec((None, h, None, d, d), lambda b, i: (b, 0, nc - 1 - i, 0, 0))
    return pl.pallas_call(
        body, name="rwkv_scan_bwd", grid=grid, in_specs=[seq] * 6 + [st, seq] + r_specs,
        out_specs=[seq] * 6 + r_ospecs, out_shape=[jax.ShapeDtypeStruct((bsz, h, t, d), F32)] * 6 + r_out,
        scratch_shapes=[pltpu.VMEM((h, d, d), F32)] + r_scr,
        compiler_params=_params(("arbitrary", "arbitrary")),
    )(r, lw, k, v, al, be, states, dy, *r_in)


def ssd_chunk(st, xs, bm, cm, dtr, dt_bias, a_log):
    n = SSM_CHUNK
    ii, jj = _iota((n, n), 0), _iota((n, n), 1)
    incl = ii >= jj
    lane = _iota((n, LANES), 1)
    dt = _softplus(dtr + dt_bias)
    a = dt * (-jnp.exp(a_log))
    acum = mmf(jnp.where(incl, 1.0, 0.0), a)
    last_row = jnp.where(jj == n - 1, 1.0, 0.0)
    cb = [mm3_nt(cm[g], bm[g]) for g in range(2)]
    pairs, heads = range(4), range(SSM_HEADS)
    e_m = [jnp.where(_iota((LANES, LANES), 0) == 2 * m + _iota((LANES, LANES), 1) // HEAD_DIM, 1.0, 0.0)
           for m in pairs]
    dt_m = [mmf(dt, e_m[m]) for m in pairs]
    ac_m = [mmf(acum, e_m[m]) for m in pairs]
    x = [xs[m] * dt_m[m] for m in pairs]
    last_m = [mmf(last_row, ac_m[m]) for m in pairs]
    colb = [mmf(acum, jnp.where(_iota((LANES, n), 0) == h, 1.0, 0.0)) for h in heads]
    decay = [jnp.exp(jnp.where(incl, colb[h] - colb[h].T, NEG_BIG)) for h in heads]
    yh = [mm3(cb[h // 4] * decay[h], x[h // 2]) for h in heads]
    y_off = [mm3(cm[m // 2], st[m]) for m in pairs]
    ys = [jnp.where(lane // HEAD_DIM == 0, yh[2 * m], yh[2 * m + 1]) + jnp.exp(ac_m[m]) * y_off[m] for m in pairs]
    st_in = [mm3_tn(bm[m // 2], x[m] * jnp.exp(last_m[m] - ac_m[m])) for m in pairs]
    st_new = [jnp.exp(last_m[m]) * st[m] + st_in[m] for m in pairs]
    return tuple(ys), tuple(st_new)


def _ssd_load(xbc_ref, dtr_ref):
    xs = tuple(xbc_ref[:, m * LANES:(m + 1) * LANES] for m in range(4))
    bm = tuple(xbc_ref[:, SSM_DIM + g * LANES:SSM_DIM + (g + 1) * LANES] for g in range(2))
    cm = tuple(xbc_ref[:, SSM_DIM + 2 * LANES + g * LANES:SSM_DIM + 2 * LANES + (g + 1) * LANES] for g in range(2))
    return xs, bm, cm, dtr_ref[...]


def ssd_fwd(xbc, proj, dt_col, dt_bias, a_log):
    bsz, t, _ = xbc.shape
    n = SSM_CHUNK
    nc = t // n

    def body(xbc_ref, dtr_ref, dtb_ref, al_ref, y_ref, st_ref, s_scr):
        @pl.when(pl.program_id(1) == 0)
        def _():
            s_scr[...] = jnp.zeros_like(s_scr)

        st = tuple(s_scr[m] for m in range(4))
        for m in range(4):
            st_ref[m] = st[m]
        xs, bm, cm, dtr = _ssd_load(xbc_ref, dtr_ref)
        ys, st_new = ssd_chunk(st, xs, bm, cm, dtr, dtb_ref[...], al_ref[...])
        for m in range(4):
            y_ref[:, m * LANES:(m + 1) * LANES] = ys[m]
            s_scr[m] = st_new[m]

    vec = pl.BlockSpec((1, LANES), lambda b, i: (0, 0))
    return pl.pallas_call(
        body, name="ssd_fwd", grid=(bsz, nc),
        in_specs=[pl.BlockSpec((None, n, 2 * SSM_DIM), lambda b, i: (b, i, 0)),
                  pl.BlockSpec((None, n, LANES), lambda b, i: (b, i, dt_col)), vec, vec],
        out_specs=[pl.BlockSpec((None, n, SSM_DIM), lambda b, i: (b, i, 0)),
                   pl.BlockSpec((None, None, 4, SSM_STATE, LANES), lambda b, i: (b, i, 0, 0, 0))],
        out_shape=[jax.ShapeDtypeStruct((bsz, t, SSM_DIM), F32),
                   jax.ShapeDtypeStruct((bsz, nc, 4, SSM_STATE, LANES), F32)],
        scratch_shapes=[pltpu.VMEM((4, SSM_STATE, LANES), F32)],
        compiler_params=_params(("parallel", "arbitrary")),
    )(xbc, proj, dt_bias, a_log)


def ssd_bwd(xbc, proj, dt_col, dt_bias, a_log, states, dy, dxs_extra):
    bsz, t, _ = xbc.shape
    n = SSM_CHUNK
    nc = t // n

    def body(xbc_ref, dtr_ref, dtb_ref, al_ref, st_ref, dy_ref, ex_ref,
             dxbc_ref, ddtr_ref, ddtb_ref, dal_ref, ds_scr):
        first = jnp.logical_and(pl.program_id(0) == 0, pl.program_id(1) == 0)

        @pl.when(pl.program_id(1) == 0)
        def _():
            ds_scr[...] = jnp.zeros_like(ds_scr)

        @pl.when(first)
        def _():
            ddtb_ref[...] = jnp.zeros_like(ddtb_ref)
            dal_ref[...] = jnp.zeros_like(dal_ref)

        st = tuple(st_ref[m] for m in range(4))
        xs, bm, cm, dtr = _ssd_load(xbc_ref, dtr_ref)
        _, vjp = jax.vjp(ssd_chunk, st, xs, bm, cm, dtr, dtb_ref[...], al_ref[...])
        dys = tuple(dy_ref[:, m * LANES:(m + 1) * LANES] for m in range(4))
        dst_in = tuple(ds_scr[m] for m in range(4))
        dst, dxs, dbm, dcm, ddtr, ddtb, dal = vjp((dys, dst_in))
        for m in range(4):
            ds_scr[m] = dst[m]
            sl = slice(m * LANES, (m + 1) * LANES)
            dxbc_ref[:, sl] = dxs[m] + ex_ref[:, sl]
        for g in range(2):
            dxbc_ref[:, SSM_DIM + g * LANES:SSM_DIM + (g + 1) * LANES] = dbm[g]
            dxbc_ref[:, SSM_DIM + 2 * LANES + g * LANES:SSM_DIM + 2 * LANES + (g + 1) * LANES] = dcm[g]
        ddtr_ref[...] = ddtr
        ddtb_ref[...] += ddtb
        dal_ref[...] += dal

    vec = pl.BlockSpec((1, LANES), lambda b, i: (0, 0))
    rev = lambda b, i: (b, nc - 1 - i, 0)
    return pl.pallas_call(
        body, name="ssd_bwd", grid=(bsz, nc),
        in_specs=[pl.BlockSpec((None, n, 2 * SSM_DIM), rev),
                  pl.BlockSpec((None, n, LANES), lambda b, i: (b, nc - 1 - i, dt_col)), vec, vec,
                  pl.BlockSpec((None, None, 4, SSM_STATE, LANES), lambda b, i: (b, nc - 1 - i, 0, 0, 0)),
                  pl.BlockSpec((None, n, SSM_DIM), rev), pl.BlockSpec((None, n, SSM_DIM), rev)],
        out_specs=[pl.BlockSpec((None, n, 2 * SSM_DIM), rev), pl.BlockSpec((None, n, LANES), rev), vec, vec],
        out_shape=[jax.ShapeDtypeStruct((bsz, t, 2 * SSM_DIM), F32), jax.ShapeDtypeStruct((bsz, t, LANES), F32),
                   jax.ShapeDtypeStruct((1, LANES), F32), jax.ShapeDtypeStruct((1, LANES), F32)],
        scratch_shapes=[pltpu.VMEM((4, SSM_STATE, LANES), F32)],
        compiler_params=_params(("arbitrary", "arbitrary")),
    )(xbc, proj, dt_bias, a_log, states, dy, dxs_extra)


def sb_block(q, kj, vj, carry, maskf):
    mask = maskf > 0.5
    z = _each(lambda q_, k_: dg(q_, k_, 1, 1, P_BF16) * (HEAD_DIM ** -0.5), q, kj)
    lk = _each(lambda z_: jnp.where(mask, -_softplus(z_), 0.0), z)
    sfx = _each(suffix_sum, lk)
    att = _each(lambda z_, c_, s_: jnp.exp(jnp.where(mask, z_ + c_ + s_, NEG_BIG)), z, carry, sfx)
    out = _each(mmb, att, vj)
    return out, _each(lambda c_, k_: c_ + jnp.sum(k_, axis=1, keepdims=True), carry, lk)


def _sb_mask(qi, j):
    n = Q_BLOCK
    return jnp.where(j * n + _iota((n, n), 1) < qi * n + _iota((n, n), 0), 1.0, 0.0)


def sb_fwd(q, k, v, ride=None):
    bsz, h, t, d = q.shape
    n = Q_BLOCK
    hp = ATTN_FWD_HEADS_PER_STEP
    grid = (bsz, h // hp, t // n)
    r_in, r_specs, r_out, r_ospecs, r_scr = _ride_args(ride)

    def body(*refs):
        q_ref, k_ref, v_ref = refs[:3]
        o_ref, c_ref = refs[3 + len(r_in):5 + len(r_in)]
        if ride is not None:
            first, last = _grid_first_last(grid)
            copies = _ride_start((refs[3], refs[5 + len(r_in)], *refs[-3:]), ride[1], first)
        qi = pl.program_id(2)
        lane = _iota((n, LANES), 1)

        c_ref[...] = jnp.zeros_like(c_ref)
        o_ref[...] = jnp.zeros_like(o_ref)

        def step(i, carry):
            j = qi - i
            rows = pl.ds(pl.multiple_of(j * n, n), n)
            for hh in range(hp):
                c_ref[hh] = jnp.where(lane == j, carry[hh], c_ref[hh])
            o, carry = sb_block([q_ref[hh] for hh in range(hp)], [k_ref[hh, rows, :] for hh in range(hp)],
                                [v_ref[hh, rows, :] for hh in range(hp)], carry, _sb_mask(qi, j))
            for hh in range(hp):
                o_ref[hh] += o[hh]
            return carry

        lax.fori_loop(0, qi + 1, step, [jnp.zeros((n, 1), F32) for _ in range(hp)])
        if ride is not None:
            _ride_wait(copies, last)

    blk = pl.BlockSpec((None, hp, n, d), lambda b, hg, i: (b, hg, i, 0))
    cblk = pl.BlockSpec((None, hp, n, LANES), lambda b, hg, i: (b, hg, i, 0))
    full = pl.BlockSpec((None, hp, t, d), lambda b, hg, i: (b, hg, 0, 0))
    return pl.pallas_call(
        body, name="sb_fwd", grid=grid, in_specs=[blk, full, full] + r_specs, out_specs=[blk, cblk] + r_ospecs,
        out_shape=[jax.ShapeDtypeStruct((bsz, h, t, d), F32), jax.ShapeDtypeStruct((bsz, h, t, LANES), F32)] + r_out,
        scratch_shapes=r_scr, compiler_params=_params(("arbitrary", "arbitrary", "arbitrary")),
    )(q, k, v, *r_in)


def sb_bwd(q, k, v, kept, do, ride=None):
    bsz, h, t, d = q.shape
    n = Q_BLOCK
    hp = SB_HEADS_PER_STEP
    grid = (bsz, h // hp, t // n)
    r_in, r_specs, r_out, r_ospecs, r_scr = _ride_args(ride)

    def body(*refs):
        q_ref, k_ref, v_ref, c_ref, do_ref = refs[:5]
        nin = 5 + len(r_in)
        dq_ref, dk_ref, dv_ref = refs[nin:nin + 3]
        if ride is not None:
            first, last = _grid_first_last(grid)
            copies = _ride_start((refs[5], refs[nin + 3], *refs[-3:]), ride[1], first)
        qi = pl.program_id(2)

        @pl.when(qi == 0)
        def _():
            dk_ref[...] = jnp.zeros_like(dk_ref)
            dv_ref[...] = jnp.zeros_like(dv_ref)

        heads = range(hp)
        qv = [q_ref[hh] for hh in heads]
        kept_v = [c_ref[hh] for hh in heads]
        lane = _iota((n, LANES), 1)

        dq_ref[...] = jnp.zeros_like(dq_ref)

        def bwd_step(j, dcarry):
            rows = pl.ds(pl.multiple_of(j * n, n), n)
            carry_in = [jnp.sum(jnp.where(lane == j, t_, 0.0), axis=1, keepdims=True) for t_ in kept_v]
            _, vjp = jax.vjp(sb_block, qv, [k_ref[hh, rows, :] for hh in heads],
                             [v_ref[hh, rows, :] for hh in heads], carry_in, _sb_mask(qi, j))
            dqj, dkj, dvj, dc, _ = vjp(([do_ref[hh] for hh in heads], dcarry))
            for hh in heads:
                dq_ref[hh] += dqj[hh]
                dk_ref[hh, rows, :] += dkj[hh]
                dv_ref[hh, rows, :] += dvj[hh]
            return dc

        lax.fori_loop(0, qi + 1, bwd_step, [jnp.zeros((n, 1), F32) for _ in heads])
        if ride is not None:
            _ride_wait(copies, last)

    blk = pl.BlockSpec((None, hp, n, d), lambda b, hg, i: (b, hg, i, 0))
    cblk = pl.BlockSpec((None, hp, n, LANES), lambda b, hg, i: (b, hg, i, 0))
    full = pl.BlockSpec((None, hp, t, d), lambda b, hg, i: (b, hg, 0, 0))
    shp = jax.ShapeDtypeStruct((bsz, h, t, d), F32)
    return pl.pallas_call(
        body, name="sb_bwd", grid=grid, in_specs=[blk, full, full, cblk, blk] + r_specs,
        out_specs=[blk, full, full] + r_ospecs, out_shape=[shp, shp, shp] + r_out,
        scratch_shapes=r_scr, compiler_params=_params(("arbitrary", "arbitrary", "arbitrary")),
    )(q, k, v, kept, do, *r_in)


def _bdot(a, b, ca, cb):
    return _dg_raw(a, b, ca, cb, P_BF16)


def _mla_scores(qn, qp, knj, kpj, qi, j):
    n = Q_BLOCK
    mask = j * n + _iota((n, n), 1) <= qi * n + _iota((n, n), 0)
    scale = (MLA_NOPE + MLA_ROPE) ** -0.5
    return _each(lambda a_, b_, k_: jnp.where(mask, (_bdot(a_, k_, 1, 1) + _bdot(b_, kpj, 1, 1)) * scale, NEG_BIG),
                 qn, qp, knj)


def _mla_specs(t, hp):
    n = Q_BLOCK
    return (pl.BlockSpec((None, hp, n, MLA_NOPE), lambda b, hg, i: (b, hg, i, 0)),
            pl.BlockSpec((None, hp, n, MLA_ROPE), lambda b, hg, i: (b, hg, i, 0)),
            pl.BlockSpec((None, hp, t, MLA_NOPE), lambda b, hg, i: (b, hg, 0, 0)),
            pl.BlockSpec((None, None, t, MLA_ROPE), lambda b, hg, i: (b, 0, 0, 0)),
            pl.BlockSpec((None, hp, n, 1), lambda b, hg, i: (b, hg, i, 0)))


def mla_fwd(qn, qp, kn, kp, v):
    bsz, h, t, _ = qn.shape
    n, hp = Q_BLOCK, ATTN_FWD_HEADS_PER_STEP
    heads = range(hp)

    def body(qn_ref, qp_ref, kn_ref, kp_ref, v_ref, o_ref, lse_ref):
        qi = pl.program_id(2)
        qn_v, qp_v = [qn_ref[hh] for hh in heads], [qp_ref[hh] for hh in heads]

        o_ref[...] = jnp.zeros_like(o_ref)

        def step(j, state):
            m, l = state
            rows = pl.ds(pl.multiple_of(j * n, n), n)
            s = _mla_scores(qn_v, qp_v, [kn_ref[hh, rows, :] for hh in heads], kp_ref[rows, :], qi, j)
            m_new = _each(lambda m_, s_: jnp.maximum(m_, jnp.max(s_, axis=1, keepdims=True)), m, s)
            p = _each(lambda s_, m_: jnp.exp(s_ - m_), s, m_new)
            corr = _each(lambda a_, b_: jnp.exp(a_ - b_), m, m_new)
            l = _each(lambda l_, c_, p_: l_ * c_ + jnp.sum(p_, axis=1, keepdims=True), l, corr, p)
            pv = _each(lambda p_, v_: _bdot(p_, v_, 1, 0), p, [v_ref[hh, rows, :] for hh in heads])
            for hh in heads:
                o_ref[hh] = o_ref[hh] * corr[hh] + pv[hh]
            return m_new, l

        init = ([jnp.full((n, 1), NEG_BIG, F32) for _ in heads], [jnp.zeros((n, 1), F32) for _ in heads])
        m, l = lax.fori_loop(0, qi + 1, step, init)
        for hh in heads:
            o_ref[hh] = o_ref[hh] / l[hh]
            lse_ref[hh] = m[hh] + jnp.log(l[hh])

    qn_s, qp_s, kn_s, kp_s, row_s = _mla_specs(t, hp)
    return pl.pallas_call(
        body, name="mla_fwd", grid=(bsz, h // hp, t // n), in_specs=[qn_s, qp_s, kn_s, kp_s, kn_s],
        out_specs=[qn_s, row_s],
        out_shape=[jax.ShapeDtypeStruct(qn.shape, F32), jax.ShapeDtypeStruct((bsz, h, t, 1), F32)],
        compiler_params=_params(("parallel", "parallel", "arbitrary")),
    )(qn, qp, kn, kp, v)


def mla_bwd(qn, qp, kn, kp, v, o, lse, do):
    bsz, h, t, _ = qn.shape
    n, hp = Q_BLOCK, MLA_HEADS_PER_STEP
    heads = range(hp)
    scale = (MLA_NOPE + MLA_ROPE) ** -0.5

    def body(qn_ref, qp_ref, kn_ref, kp_ref, v_ref, o_ref, lse_ref, do_ref,
             dqn_ref, dqp_ref, dkn_ref, dkp_ref, dv_ref):
        hg, qi = pl.program_id(1), pl.program_id(2)

        @pl.when(qi == 0)
        def _():
            dkn_ref[...] = jnp.zeros_like(dkn_ref)
            dv_ref[...] = jnp.zeros_like(dv_ref)

        @pl.when(jnp.logical_and(qi == 0, hg == 0))
        def _():
            dkp_ref[...] = jnp.zeros_like(dkp_ref)

        qn_v, qp_v = [qn_ref[hh] for hh in heads], [qp_ref[hh] for hh in heads]
        do_v, lse_v = [do_ref[hh] for hh in heads], [lse_ref[hh] for hh in heads]
        dsum = [jnp.sum(do_v[hh] * o_ref[hh], axis=1, keepdims=True) for hh in heads]

        dqn_ref[...] = jnp.zeros_like(dqn_ref)
        dqp_ref[...] = jnp.zeros_like(dqp_ref)

        def step(j, _):
            rows = pl.ds(pl.multiple_of(j * n, n), n)
            knj, vj, kpj = [kn_ref[hh, rows, :] for hh in heads], [v_ref[hh, rows, :] for hh in heads], kp_ref[rows, :]
            s = _mla_scores(qn_v, qp_v, knj, kpj, qi, j)
            p = _each(lambda s_, l_: jnp.exp(s_ - l_), s, lse_v)
            dp = _each(lambda d_, v_: _bdot(d_, v_, 1, 1), do_v, vj)
            ds = _each(lambda p_, dp_, d_: p_ * (dp_ - d_) * scale, p, dp, dsum)
            dqn = _each(lambda ds_, k_: _bdot(ds_, k_, 1, 0), ds, knj)
            dqp = _each(lambda ds_: _bdot(ds_, kpj, 1, 0), ds)
            dkn = _each(lambda ds_, q_: _bdot(ds_, q_, 0, 0), ds, qn_v)
            dv = _each(lambda p_, d_: _bdot(p_, d_, 0, 0), p, do_v)
            dkp = _each(lambda ds_, q_: _bdot(ds_, q_, 0, 0), ds, qp_v)
            for hh in heads:
                dqn_ref[hh] += dqn[hh]
                dqp_ref[hh] += dqp[hh]
                dkn_ref[hh, rows, :] += dkn[hh]
                dv_ref[hh, rows, :] += dv[hh]
            dkp_ref[rows, :] += functools.reduce(lambda a_, b_: a_ + b_, dkp)
            return 0

        lax.fori_loop(0, qi + 1, step, 0)

    qn_s, qp_s, kn_s, kp_s, row_s = _mla_specs(t, hp)
    return pl.pallas_call(
        body, name="mla_bwd", grid=(bsz, h // hp, t // n),
        in_specs=[qn_s, qp_s, kn_s, kp_s, kn_s, qn_s, row_s, qn_s],
        out_specs=[qn_s, qp_s, kn_s, kp_s, kn_s],
        out_shape=[jax.ShapeDtypeStruct(qn.shape, F32), jax.ShapeDtypeStruct(qp.shape, F32),
                   jax.ShapeDtypeStruct(kn.shape, F32), jax.ShapeDtypeStruct(kp.shape, F32),
                   jax.ShapeDtypeStruct(v.shape, F32)],
        compiler_params=_params(("parallel", "arbitrary", "arbitrary")),
    )(qn, qp, kn, kp, v, o, lse, do)


def rope(name, x, pos, inv_freq, sign):
    bsz, hx, t, d = x.shape
    half = d // 2

    tt = _largest_tile(t, 512, SUBLANES)

    def body(x_ref, pos_ref, f_ref, o_ref):
        ang = pos_ref[...].astype(F32) * f_ref[...]
        cos, sin = jnp.cos(ang), sign * jnp.sin(ang)
        ri, ci = _iota((d, d), 0), _iota((d, d), 1)
        rot = jnp.where(ri == ci + half, -1.0, 0.0) + jnp.where(ri + half == ci, 1.0, 0.0)
        for hh in range(hx):
            xv = x_ref[hh]
            o_ref[hh] = xv * cos + mmf(xv, rot) * sin

    blk = pl.BlockSpec((None, hx, tt, d), lambda b, i: (b, 0, i, 0))
    return pl.pallas_call(
        body, name=name, grid=(bsz, t // tt),
        in_specs=[blk, pl.BlockSpec((None, tt, 1), lambda b, i: (b, i, 0)), pl.BlockSpec((1, d), lambda b, i: (0, 0))],
        out_specs=blk, out_shape=jax.ShapeDtypeStruct(x.shape, F32),
        compiler_params=_params(("parallel", "parallel")),
    )(x, pos, inv_freq)


def loss_head(h, target, tm):
    n, d = h.shape

    def body(h_ref, t_ref, dh_ref, l_ref):
        @pl.when(pl.program_id(0) == 0)
        def _():
            l_ref[...] = jnp.zeros_like(l_ref)

        e = h_ref[...] - t_ref[...]
        dh_ref[...] = e * (1.0 / d)
        l_ref[...] += jnp.sum(e * e, axis=(0, 1), keepdims=True) * (0.5 / d)

    row = pl.BlockSpec((tm, d), lambda i: (i, 0))
    dh, l = pl.pallas_call(
        body, name="loss_head", grid=(n // tm,), in_specs=[row, row],
        out_specs=[row, pl.BlockSpec((SUBLANES, LANES), lambda i: (0, 0))],
        out_shape=[jax.ShapeDtypeStruct((n, d), F32), jax.ShapeDtypeStruct((SUBLANES, LANES), F32)],
        compiler_params=_params(("arbitrary",)),
    )(h, target)
    return dh, l[0, 0]


def _exchange_copies(src_ref, out_ref, send_sems, recv_sems, local_sem, gather):
    x, y, c = lax.axis_index("x"), lax.axis_index("y"), lax.axis_index("c")
    me = 4 * x + 2 * y + c
    copies = [pltpu.make_async_copy(src_ref if gather else src_ref.at[me], out_ref.at[me], local_sem)]
    for m in range(1, N_DEV):
        px, py, pc = x ^ (m >> 2), y ^ ((m >> 1) & 1), c ^ (m & 1)
        peer = 4 * px + 2 * py + pc
        copies.append(pltpu.make_async_remote_copy(
            src_ref=src_ref if gather else src_ref.at[peer], dst_ref=out_ref.at[me],
            send_sem=send_sems.at[m], recv_sem=recv_sems.at[m],
            device_id=(px, py, pc), device_id_type=pl.DeviceIdType.MESH))
    return copies


def _exchange_start(copies):
    for cp in copies:
        cp.start()


def _exchange_wait(copies):
    for cp in copies[1:]:
        cp.wait_recv()
    for cp in copies[1:]:
        cp.wait_send()
    copies[0].wait()


EXCHANGE_SCRATCH = [pltpu.SemaphoreType.DMA((N_DEV,)), pltpu.SemaphoreType.DMA((N_DEV,)),
                    pltpu.SemaphoreType.DMA(())]


def _exchange_out(src):
    return jax.ShapeDtypeStruct((N_DEV, src.shape[-2], LANES), src.dtype)


def peer_exchange(name, src, gather):
    def body(src_ref, out_ref, send_sems, recv_sems, local_sem):
        copies = _exchange_copies(src_ref, out_ref, send_sems, recv_sems, local_sem, gather)
        _exchange_start(copies)
        _exchange_wait(copies)

    return pl.pallas_call(
        body, name=name,
        in_specs=[pl.BlockSpec(memory_space=pl.ANY)], out_specs=pl.BlockSpec(memory_space=pl.ANY),
        out_shape=_exchange_out(src), scratch_shapes=list(EXCHANGE_SCRATCH),
    )(src)


def _grid_first_last(grid):
    ids = [pl.program_id(a) for a in range(len(grid))]
    first = functools.reduce(jnp.logical_and, [i == 0 for i in ids])
    last = functools.reduce(jnp.logical_and, [i == g - 1 for i, g in zip(ids, grid)])
    return first, last


def _ride_start(refs, gather, first):
    copies = _exchange_copies(*refs, gather)

    @pl.when(first)
    def _():
        _exchange_start(copies)

    return copies


def _ride_wait(copies, last):
    @pl.when(last)
    def _():
        _exchange_wait(copies)


def _ride_args(ride):
    if ride is None:
        return [], [], [], [], []
    hbm = pl.BlockSpec(memory_space=pl.ANY)
    return [ride[0]], [hbm], [_exchange_out(ride[0])], [hbm], list(EXCHANGE_SCRATCH)


def adamw_sum(name, parts, w, m, v):
    r = w.shape[0]
    tr = ADAM_ROWS
    assert r % tr == 0

    def body(p_ref, w_ref, m_ref, v_ref, g_ref, d_ref, nm_ref, nv_ref):
        g = p_ref[0]
        for j in range(1, N_DEV):
            g = g + p_ref[j]
        mm_ = ADAM_B1 * m_ref[...] + (1.0 - ADAM_B1) * g
        vv = ADAM_B2 * v_ref[...] + (1.0 - ADAM_B2) * (g * g)
        m_hat = mm_ / (1.0 - ADAM_B1 ** ADAM_STEP)
        v_hat = vv / (1.0 - ADAM_B2 ** ADAM_STEP)
        g_ref[...] = g
        d_ref[...] = -ADAM_LR * (m_hat / (jnp.sqrt(v_hat) + ADAM_EPS) + ADAM_WD * w_ref[...])
        nm_ref[...] = mm_
        nv_ref[...] = vv

    row = pl.BlockSpec((tr, LANES), lambda i: (i, 0))
    shp = jax.ShapeDtypeStruct((r, LANES), F32)
    return pl.pallas_call(
        body, name=name, grid=(r // tr,),
        in_specs=[pl.BlockSpec((N_DEV, tr, LANES), lambda i: (0, i, 0)), row, row, row],
        out_specs=[row] * 4, out_shape=[shp] * 4,
        compiler_params=_params(("parallel",)),
    )(parts, w, m, v)


WEIGHTS = ['l0_w_in', 'rwkv_mix', 'rwkv_w0', 'rwkv_w2', 'rwkv_a0', 'rwkv_a2', 'rwkv_g2', 'rwkv_k_k', 'rwkv_k_a',
           'rwkv_r_k', 'rwkv_ln_g', 'rwkv_ln_b', 'ssm_conv_w', 'ssm_conv_b', 'ssm_dt_bias', 'ssm_a_log', 'ssm_d',
           'ssm_norm_g', 'l0_w_out', 'l0_ln1_g', 'l0_ln1_b', 'ffn0_w_up', 'ffn0_conv_w', 'ffn0_conv_b',
           'ffn0_w_down', 'l0_ln2_g', 'l0_ln2_b', 'l1_w_in', 'mla_q_norm_g', 'mla_w_uq', 'mla_kv_norm_g',
           'mla_w_ukv', 'l1_w_out', 'l1_ln1_g', 'l1_ln1_b', 'ffn1_w_up', 'ffn1_conv_w', 'ffn1_conv_b',
           'ffn1_w_down', 'l1_ln2_g', 'l1_ln2_b']
SHARD_AXIS = {'l0_w_in': 1, 'rwkv_w2': 1, 'rwkv_a2': 1, 'rwkv_g2': 1, 'ssm_conv_w': 1, 'l0_w_out': 0,
              'ffn0_w_up': 1, 'ffn0_conv_w': 1, 'ffn0_w_down': 0, 'l1_w_in': 1, 'mla_w_uq': 1, 'mla_w_ukv': 1,
              'l1_w_out': 0, 'ffn1_w_up': 1, 'ffn1_conv_w': 1, 'ffn1_w_down': 0}
MATMUL_W = ['l0_w_in', 'rwkv_w2', 'rwkv_a2', 'rwkv_g2', 'l0_w_out', 'ffn0_w_up', 'ffn0_w_down', 'l1_w_in',
            'mla_w_uq', 'mla_w_ukv', 'l1_w_out', 'ffn1_w_up', 'ffn1_w_down']
CONV_W = ['ssm_conv_w', 'ffn0_conv_w', 'ffn1_conv_w']
TOK_TILE = 256
ADAM_ROWS = 512


def _ceil_to(size, unit):
    return -(-size // unit) * unit


def _flat_rows(pieces, seg_rows, total_rows):
    unit = seg_rows * LANES
    out, total = [], 0
    for p in pieces:
        p = jnp.concatenate([q.reshape(-1) for q in p]) if isinstance(p, list) else p.reshape(-1)
        pad = _ceil_to(p.size, unit) - p.size
        out.append(jnp.pad(p, (0, pad)) if pad else p)
        total += p.size + pad
    tail = _ceil_to(total, total_rows * LANES) - total
    if tail:
        out.append(jnp.zeros((tail,), out[0].dtype))
    return jnp.concatenate(out).reshape(-1, LANES)


def _unflatten(flat2d, shapes, seg_rows):
    flat = flat2d.reshape(-1)
    out, off = [], 0
    for shp in shapes:
        if isinstance(shp, list):
            seg, pos = [], off
            for s_ in shp:
                seg.append(flat[pos:pos + math.prod(s_)].reshape(s_))
                pos += math.prod(s_)
            out.append(seg)
            size = pos - off
        else:
            size = math.prod(shp)
            out.append(flat[off:off + size].reshape(shp))
        off += _ceil_to(size, seg_rows * LANES)
    return out


def _by_size(names):
    return [nm for nm in names if nm in SHARD_AXIS], [nm for nm in names if nm not in SHARD_AXIS]


def _to_heads(t2, bsz, h):
    n, w = t2.shape
    return t2.reshape(bsz, n // bsz, h, w // h).transpose(0, 2, 1, 3)


def _from_heads(t4):
    b, h, t, d = t4.shape
    return t4.transpose(0, 2, 1, 3).reshape(b * t, h * d)


def _row(v):
    return v.reshape(1, -1)


def _pad_lanes(v):
    return jnp.pad(v.reshape(1, -1), ((0, 0), (0, LANES - v.size)))


def _local_step(a, w, comm=None):
    x = a['x']
    bsz, t, d = x.shape
    n = bsz * t
    tm = TOK_TILE
    pos = a['positions'].reshape(bsz, t, 1)
    inv_freq = 1.0 / (ROPE_THETA ** (jnp.arange(0, MLA_ROPE, 2, dtype=F32) / MLA_ROPE))
    inv_freq = jnp.concatenate([inv_freq, inv_freq]).reshape(1, MLA_ROPE)
    target = a['loss_target'].reshape(n, d)

    wi0 = w['l0_w_in']
    win0 = jnp.concatenate([wi0[:, 0:1536], wi0[:, 1792:3328], wi0[:, 1536:1792], wi0[:, 3328:3336],
                            jnp.zeros((d, L0_PAD - 3336), wi0.dtype)], axis=1)
    w2p = jnp.concatenate([w['rwkv_w2'], jnp.zeros_like(w['rwkv_w2'])], axis=0)
    a2p = jnp.concatenate([jnp.zeros_like(w['rwkv_a2']), w['rwkv_a2']], axis=0)
    mix = a['rwkv_mix']
    taps = jnp.stack([mix, 1.0 - mix])
    zero_b = jnp.zeros((1, mix.size), F32)
    rw_map = lambda j: j + jnp.where(j >= 12, 12, 0)
    ssm_map = lambda j: j + 16
    gate_map = lambda j: j
    up_map = lambda j: j + D_FF // LANES
    dt_col = 3328 // LANES
    dtb, alog, dsk = _pad_lanes(a['ssm_dt_bias']), _pad_lanes(a['ssm_a_log']), _pad_lanes(a['ssm_d'])
    pre_p = [_row(a['rwkv_w0']), w2p, _row(a['rwkv_a0']), a2p, w['rwkv_g2'], _row(a['rwkv_k_k']), _row(a['rwkv_k_a'])]
    post_p = [_row(a['rwkv_ln_g']), _row(a['rwkv_ln_b']), _row(a['rwkv_r_k'])]
    sp_p = [dsk, _row(a['ssm_norm_g'])]

    def ln(name, h, y, layer, which):
        ps = [_row(a[f'l{layer}_ln{which}_g']), _row(a[f'l{layer}_ln{which}_b'])]
        return tok_fwd(name, f_ln, [(h, d, 0), (y, d, 0)], ps, [d], tm, n, t)[0]

    def ffn_fwd(layer, h):
        up = mm(h, w[f'ffn{layer}_w_up'], 'nn', f'ffn{layer}_up')
        act = dwconv_fwd(f'ffn{layer}_conv', up.reshape(bsz, t, 2 * D_FF), gate_map, w[f'ffn{layer}_conv_w'],
                         _row(a[f'ffn{layer}_conv_b']), 3, True, upmap=up_map, out_dtype=BF16)
        act = act.reshape(n, D_FF)
        return up, act, mm(act, w[f'ffn{layer}_w_down'], 'nn', f'ffn{layer}_down')

    x2 = x.reshape(n, d)
    proj0 = mm(x2, win0, 'nn', 'l0_in')
    p0 = proj0.reshape(bsz, t, L0_PAD)
    xs_r = dwconv_fwd('rwkv_shift', p0, rw_map, taps, zero_b, 2, False).reshape(n, 1792)
    pre_x = [(xs_r, 512, 0), (xs_r, 512, 1), (xs_r, 512, 2), (xs_r, LANES, 12), (xs_r, LANES, 13)]
    heads64 = (RWKV_HEADS, HEAD_DIM)
    r_, v_, lw, kmod, al, be, gt = tok_fwd('rwkv_pre', f_rwkv_pre, pre_x, pre_p, [heads64] * 6 + [RWKV_DIM],
                                           tm, n, t)
    scan_in = [r_, lw, kmod, v_, al, be]
    if comm is None:
        y_h, rstates = rwkv_scan_fwd(*scan_in)
    else:
        y_h, rstates, got = rwkv_scan_fwd(*scan_in, ride=comm.weights_ride(MID_GATHER))
        w = {**w, **comm.weights(MID_GATHER, got)}
    wi1 = w['l1_w_in']
    win1 = jnp.concatenate([wi1, jnp.zeros((d, L1_PAD - 1952), wi1.dtype)], axis=1)
    wq3 = w['mla_w_uq'].reshape(-1, 8, MLA_NOPE + MLA_ROPE)
    wkv3 = w['mla_w_ukv'].reshape(-1, 8, 2 * MLA_NOPE)
    mla_p = [_row(a['mla_q_norm_g']), wq3[:, :, :MLA_NOPE].reshape(-1, 512), wq3[:, :, MLA_NOPE:].reshape(-1, 256),
             _row(a['mla_kv_norm_g']), wkv3[:, :, :MLA_NOPE].reshape(-1, 512), wkv3[:, :, MLA_NOPE:].reshape(-1, 512)]
    post_x = [y_h, r_, kmod, v_, (gt, 512, 0)]
    y_a = tok_fwd('rwkv_post', f_rwkv_post, post_x, post_p, [RWKV_DIM], tm, n, t)[0]
    xbc = dwconv_fwd('ssm_conv', p0, ssm_map, w['ssm_conv_w'], _row(a['ssm_conv_b']), 4, True)
    ys, sstates = ssd_fwd(xbc, p0, dt_col, dtb, alog)
    xbc2 = xbc.reshape(n, 2 * SSM_DIM)
    sp_x = [(ys.reshape(n, SSM_DIM), 512, 0), (xbc2, 512, 0), (proj0, 512, 3)]
    y_b = tok_fwd('ssd_post', f_ssd_post, sp_x, sp_p, [SSM_DIM], tm, n, t)[0]
    wo0 = w['l0_w_out']
    mixed0 = mm(y_b, wo0, 'nn', 'l0_out_b', b_rows=(512, 512), add=mm(y_a, wo0, 'nn', 'l0_out_a', b_rows=(0, 512)))
    h1 = ln('l0_ln1', x2, mixed0, 0, 1)
    up0, act0, f0 = ffn_fwd(0, h1)
    h2 = ln('l0_ln2', h1, f0, 0, 2)

    proj1 = mm(h2, win1, 'nn', 'l1_in')
    q_sb, k_sb, v_sb = tok_fwd('sb_split', f_same, [(proj1, 512, 0), (proj1, 512, 1), (proj1, 512, 2)], [],
                               [heads64] * 3, tm, n, t)
    if comm is None:
        o_c, sb_kept = sb_fwd(q_sb, k_sb, v_sb)
    else:
        o_c, sb_kept, got = sb_fwd(q_sb, k_sb, v_sb, ride=comm.weights_ride(FFN1_GATHER))
        w = {**w, **comm.weights(FFN1_GATHER, got)}
    mla_x = [(proj1, 256, 6), (proj1, LANES, 14)]
    qn, qp_raw, kn, vv = tok_fwd('mla_pre', f_mla_pre, mla_x, mla_p, [heads64, (8, MLA_ROPE), heads64, heads64],
                                 tm, n, t)
    kp_raw = proj1[:, 1920:1920 + MLA_ROPE].reshape(bsz, 1, t, MLA_ROPE)
    qp = rope('rope_q', qp_raw, pos, inv_freq, 1.0)
    kp = rope('rope_k', kp_raw, pos, inv_freq, 1.0)
    o_d, lse_d = mla_fwd(qn, qp, kn, kp, vv)
    y_cd = tok_fwd('attn_merge', f_concat, [o_c, o_d], [], [2 * RWKV_DIM], tm, n, t, out_dtype=BF16)[0]
    wo1 = w['l1_w_out']
    mixed1 = mm(y_cd, wo1, 'nn', 'l1_out')
    h3 = ln('l1_ln1', h2, mixed1, 1, 1)
    up1, act1, f1 = ffn_fwd(1, h3)
    h4 = ln('l1_ln2', h3, f1, 1, 2)
    dh4, loss = loss_head(h4, target, tm)

    g = {}

    def ln_bwd(name, h, y, layer, which, dout):
        ps = [_row(a[f'l{layer}_ln{which}_g']), _row(a[f'l{layer}_ln{which}_b'])]
        (dh, dy), (dg, db) = tok_bwd(name, f_ln, [(h, d, 0), (y, d, 0)], ps, [[dout]], tm, n, t)
        g[f'l{layer}_ln{which}_g'], g[f'l{layer}_ln{which}_b'] = dg.reshape(-1), db.reshape(-1)
        return dh, dy

    def ffn_bwd(layer, h, up, act, df, dh_res):
        wup, wdown = w[f'ffn{layer}_w_up'], w[f'ffn{layer}_w_down']
        g[f'ffn{layer}_w_down'] = mm(act, df, 'tn', f'ffn{layer}_dwdown')
        dact = mm(df, wdown, 'nt', f'ffn{layer}_dact').reshape(bsz, t, D_FF)
        dgate, dcw, dcb, dup = dwconv_bwd(f'ffn{layer}_conv_bwd', up.reshape(bsz, t, 2 * D_FF), gate_map,
                                          w[f'ffn{layer}_conv_w'], _row(a[f'ffn{layer}_conv_b']), 3, True, dact,
                                          upmap=up_map, grad_dtype=BF16)
        dgate, dup = dgate.reshape(n, D_FF), dup.reshape(n, D_FF)
        g[f'ffn{layer}_conv_w'], g[f'ffn{layer}_conv_b'] = dcw, dcb.reshape(-1)
        g[f'ffn{layer}_w_up'] = (mm(h, dgate, 'tn', f'ffn{layer}_dwgate'), mm(h, dup, 'tn', f'ffn{layer}_dwup'))
        dh = mm(dgate, wup, 'nt', f'ffn{layer}_dh_gate', add=dh_res, b_cols=(0, D_FF))
        return mm(dup, wup, 'nt', f'ffn{layer}_dh_up', add=dh, b_cols=(D_FF, D_FF))

    dh3_res, df1 = ln_bwd('l1_ln2_bwd', h3, f1, 1, 2, dh4)
    dh3 = ffn_bwd(1, h3, up1, act1, df1, dh3_res)
    dh2_res, dmixed1 = ln_bwd('l1_ln1_bwd', h2, mixed1, 1, 1, dh3)
    g['l1_w_out'] = mm(y_cd, dmixed1, 'tn', 'l1_dwout')
    dy_cd = mm(dmixed1, wo1, 'nt', 'l1_dy')
    dy_c, dy_d = tok_fwd('attn_split', f_same, [(dy_cd, 512, 0), (dy_cd, 512, 1)], [], [heads64] * 2, tm, n, t)
    parts = {}
    if comm is None:
        dq_sb, dk_sb, dv_sb = sb_bwd(q_sb, k_sb, v_sb, sb_kept, dy_c)
    else:
        dq_sb, dk_sb, dv_sb, parts['a'] = sb_bwd(q_sb, k_sb, v_sb, sb_kept, dy_c, ride=comm.grad_ride('a', g))
    dqn, dqp, dkn, dkp, dvv = mla_bwd(qn, qp, kn, kp, vv, o_d, lse_d, dy_d)
    dqp_raw = rope('rope_q_bwd', dqp, pos, inv_freq, -1.0)
    dkp_raw = rope('rope_k_bwd', dkp, pos, inv_freq, -1.0).reshape(n, MLA_ROPE)
    (dcq, dckv), (dqg, dwq_n, dwq_p, dkvg, dwk, dwv) = tok_bwd('mla_pre_bwd', f_mla_pre, mla_x, mla_p,
                                                               [[dqn], [dqp_raw], [dkn], [dvv]], tm, n, t)
    g['mla_q_norm_g'], g['mla_kv_norm_g'] = dqg.reshape(-1), dkvg.reshape(-1)
    g['mla_w_uq'] = jnp.concatenate([dwq_n.reshape(-1, 8, MLA_NOPE), dwq_p.reshape(-1, 8, MLA_ROPE)],
                                    axis=2).reshape(-1, 8 * (MLA_NOPE + MLA_ROPE))
    g['mla_w_ukv'] = jnp.concatenate([dwk.reshape(-1, 8, MLA_NOPE), dwv.reshape(-1, 8, MLA_NOPE)],
                                     axis=2).reshape(-1, 16 * MLA_NOPE)
    dkp_pad = jnp.pad(dkp_raw, ((0, 0), (0, LANES - MLA_ROPE)))
    dproj1 = tok_fwd('l1_dproj', f_concat, [dq_sb, dk_sb, dv_sb, (dcq, 256, 0), (dckv, LANES, 0),
                                            (dkp_pad, LANES, 0)], [], [L1_PAD], tm, n, t, out_dtype=BF16)[0]
    g['l1_w_in'] = mm(h2, dproj1, 'tn', 'l1_dwin')[:, :1952]
    dh2 = mm(dproj1, win1, 'nt', 'l1_dh', add=dh2_res)

    dh1_res, df0 = ln_bwd('l0_ln2_bwd', h1, f0, 0, 2, dh2)
    dh1 = ffn_bwd(0, h1, up0, act0, df0, dh1_res)
    dx_res, dmixed0 = ln_bwd('l0_ln1_bwd', x2, mixed0, 0, 1, dh1)
    g['l0_w_out'] = (mm(y_a, dmixed0, 'tn', 'l0_dwout_a'), mm(y_b, dmixed0, 'tn', 'l0_dwout_b'))
    dy_a = mm(dmixed0, wo0, 'nt', 'l0_dy_a', b_rows=(0, 512))
    dy_b = mm(dmixed0, wo0, 'nt', 'l0_dy_b', b_rows=(512, 512))
    (dy_r, dr1, dkm1, dv1, dgt), (dlng, dlnb, drk) = tok_bwd('rwkv_post_bwd', f_rwkv_post, post_x, post_p, [[dy_a]],
                                                            tm, n, t, dx_layouts=[heads64] * 4 + [None])
    g['rwkv_ln_g'], g['rwkv_ln_b'] = dlng.reshape(-1), dlnb.reshape(-1)
    g['rwkv_r_k'] = drk.reshape(RWKV_HEADS, HEAD_DIM)
    (dys, dxs_skip, dz), (ddsk, dng) = tok_bwd('ssd_post_bwd', f_ssd_post, sp_x, sp_p, [[dy_b]], tm, n, t)
    g['ssm_d'], g['ssm_norm_g'] = ddsk[0, :SSM_HEADS], dng.reshape(-1)
    dxbc_act, ddtr, ddtb, dalog = ssd_bwd(xbc, p0, dt_col, dtb, alog, sstates, dys.reshape(bsz, t, SSM_DIM),
                                          dxs_skip.reshape(bsz, t, SSM_DIM))
    g['ssm_dt_bias'], g['ssm_a_log'] = ddtb[0, :SSM_HEADS], dalog[0, :SSM_HEADS]
    dxbc, dscw, dscb = dwconv_bwd('ssm_conv_bwd', p0, ssm_map, w['ssm_conv_w'], _row(a['ssm_conv_b']), 4, True,
                                  dxbc_act)
    g['ssm_conv_w'], g['ssm_conv_b'] = dscw, dscb.reshape(-1)
    if comm is None:
        dscan = rwkv_scan_bwd(*scan_in, rstates, dy_r)
    else:
        *dscan, parts['b'] = rwkv_scan_bwd(*scan_in, rstates, dy_r, ride=comm.grad_ride('b', g))
    dr2, dlw, dk2, dv2, dal, dbe = dscan
    pre_ct = [[dr1, dr2], [dv1, dv2], [dlw], [dkm1, dk2], [dal], [dbe], [dgt]]
    dpre_x, dpre_p = tok_bwd('rwkv_pre_bwd', f_rwkv_pre, pre_x, pre_p, pre_ct, tm, n, t)
    g['rwkv_w0'], g['rwkv_a0'] = dpre_p[0].reshape(-1), dpre_p[2].reshape(-1)
    g['rwkv_w2'], g['rwkv_a2'], g['rwkv_g2'] = dpre_p[1][:64], dpre_p[3][64:], dpre_p[4]
    g['rwkv_k_k'], g['rwkv_k_a'] = dpre_p[5].reshape(-1), dpre_p[6].reshape(-1)
    dxs_r = jnp.concatenate(dpre_x, axis=1).reshape(bsz, t, 1792)
    d_rw, dtaps, _ = dwconv_bwd('rwkv_shift_bwd', p0, rw_map, taps, zero_b, 2, False, dxs_r)
    d_rw = d_rw.reshape(n, 1792)
    g['rwkv_mix'] = dtaps[0] - dtaps[1]
    dproj0 = tok_fwd('l0_dproj', f_concat, [(d_rw, 1536, 0), (dz, 512, 0), (dxbc.reshape(n, 2 * SSM_DIM), 1024, 0),
                                            (d_rw, 256, 6), (ddtr.reshape(n, LANES), LANES, 0)],
                     [], [L0_PAD], tm, n, t, out_dtype=BF16)[0]
    dwin0 = mm(x2, dproj0, 'tn', 'l0_dwin')
    g['l0_w_in'] = jnp.concatenate([dwin0[:, 0:1536], dwin0[:, 3072:3328], dwin0[:, 1536:3072],
                                    dwin0[:, 3328:3336]], axis=1)
    dx = mm(dproj0, win0, 'nt', 'l0_dx', add=dx_res)
    if comm is not None:
        parts['c'] = peer_exchange('grad_exchange_c', comm.grad_ride('c', g)[0], False)
    return loss, dx.reshape(bsz, t, d), g, parts


GRAD_GROUPS = {
    'a': ['ffn1_w_up', 'ffn1_conv_w', 'ffn1_conv_b', 'ffn1_w_down', 'l1_ln2_g', 'l1_ln2_b'],
    'c': ['l0_w_in', 'rwkv_mix', 'rwkv_w0', 'rwkv_w2', 'rwkv_a0', 'rwkv_a2', 'rwkv_g2', 'rwkv_k_k', 'rwkv_k_a'],
}
GRAD_GROUPS['b'] = [nm for nm in WEIGHTS if nm not in GRAD_GROUPS['a'] + GRAD_GROUPS['c']]
FIRST_GATHER = ['l0_w_in', 'rwkv_w2', 'rwkv_a2', 'rwkv_g2'] + CONV_W
MID_GATHER = ['l0_w_out', 'ffn0_w_up', 'ffn0_w_down', 'l1_w_in', 'mla_w_uq', 'mla_w_ukv', 'l1_w_out']
FFN1_GATHER = ['ffn1_w_up', 'ffn1_w_down']
BF16_ROWS = 16


class _Comm:
    def __init__(self, a):
        self.a = a

    def _pieces(self, names):
        return [lax.bitcast_convert_type(self.a[nm], BF16) if nm in CONV_W else self.a[nm].astype(BF16)
                for nm in names]

    def _unpack(self, names, got):
        shapes = [p.shape for p in self._pieces(names)]
        blocks = [_unflatten(got[k], shapes, BF16_ROWS) for k in range(N_DEV)]
        out = {}
        for i, nm in enumerate(names):
            blk = [blocks[k][i] for k in range(N_DEV)]
            if nm in CONV_W:
                blk = [lax.bitcast_convert_type(b, F32) for b in blk]
            out[nm] = jnp.concatenate(blk, axis=SHARD_AXIS[nm])
        return out

    def first_weights(self):
        return self.weights(FIRST_GATHER, peer_exchange('gather_first_weights', self.weights_ride(FIRST_GATHER)[0], True))

    def weights_ride(self, names):
        return _flat_rows(self._pieces(names), BF16_ROWS, BF16_ROWS), True

    def weights(self, names, got):
        return self._unpack(names, got)

    def grad_ride(self, group, g):
        def shard_of(nm, k):
            gv = g[nm]
            if nm not in SHARD_AXIS:
                return gv
            per = N_DEV
            if isinstance(gv, tuple):
                gv, k, per = gv[k // 4], k % 4, 4
            width = gv.shape[SHARD_AXIS[nm]] // per
            return lax.slice_in_dim(gv, k * width, (k + 1) * width, axis=SHARD_AXIS[nm])

        big, small = _by_size(GRAD_GROUPS[group])
        return jnp.stack([_flat_rows([shard_of(nm, k) for nm in big] + [[g[nm] for nm in small]], SUBLANES, ADAM_ROWS)
                          for k in range(N_DEV)]), False


def _step(a):
    comm = _Comm(a)
    loss, dx, _, parts = _local_step(a, comm.first_weights(), comm)
    loss = lax.psum(loss, ('x', 'y', 'c'))
    res = {}
    for group, names in GRAD_GROUPS.items():
        big, small = _by_size(names)
        flat = lambda prefix: _flat_rows([a[prefix + nm] for nm in big] + [[a[prefix + nm] for nm in small]],
                                         SUBLANES, ADAM_ROWS)
        outs = adamw_sum(f'adamw_{group}', parts[group], flat(''), flat('m_'), flat('v_'))
        shapes = [a[nm].shape for nm in big] + [[a[nm].shape for nm in small]]
        per_out = [_unflatten(o, shapes, SUBLANES) for o in outs]
        for i, nm in enumerate(big):
            res[nm] = [per_out[j][i] for j in range(4)]
        for i, nm in enumerate(small):
            res[nm] = [per_out[j][-1][i] for j in range(4)]
    return (loss, dx, *[res[nm][j] for j in range(4) for nm in WEIGHTS])


def kernel(x, positions, l0_w_in, rwkv_mix, rwkv_w0, rwkv_w2, rwkv_a0, rwkv_a2, rwkv_g2, rwkv_k_k, rwkv_k_a, rwkv_r_k, rwkv_ln_g, rwkv_ln_b, ssm_conv_w, ssm_conv_b, ssm_dt_bias, ssm_a_log, ssm_d, ssm_norm_g, l0_w_out, l0_ln1_g, l0_ln1_b, ffn0_w_up, ffn0_conv_w, ffn0_conv_b, ffn0_w_down, l0_ln2_g, l0_ln2_b, l1_w_in, mla_q_norm_g, mla_w_uq, mla_kv_norm_g, mla_w_ukv, l1_w_out, l1_ln1_g, l1_ln1_b, ffn1_w_up, ffn1_conv_w, ffn1_conv_b, ffn1_w_down, l1_ln2_g, l1_ln2_b, loss_target, m_l0_w_in, m_rwkv_mix, m_rwkv_w0, m_rwkv_w2, m_rwkv_a0, m_rwkv_a2, m_rwkv_g2, m_rwkv_k_k, m_rwkv_k_a, m_rwkv_r_k, m_rwkv_ln_g, m_rwkv_ln_b, m_ssm_conv_w, m_ssm_conv_b, m_ssm_dt_bias, m_ssm_a_log, m_ssm_d, m_ssm_norm_g, m_l0_w_out, m_l0_ln1_g, m_l0_ln1_b, m_ffn0_w_up, m_ffn0_conv_w, m_ffn0_conv_b, m_ffn0_w_down, m_l0_ln2_g, m_l0_ln2_b, m_l1_w_in, m_mla_q_norm_g, m_mla_w_uq, m_mla_kv_norm_g, m_mla_w_ukv, m_l1_w_out, m_l1_ln1_g, m_l1_ln1_b, m_ffn1_w_up, m_ffn1_conv_w, m_ffn1_conv_b, m_ffn1_w_down, m_l1_ln2_g, m_l1_ln2_b, v_l0_w_in, v_rwkv_mix, v_rwkv_w0, v_rwkv_w2, v_rwkv_a0, v_rwkv_a2, v_rwkv_g2, v_rwkv_k_k, v_rwkv_k_a, v_rwkv_r_k, v_rwkv_ln_g, v_rwkv_ln_b, v_ssm_conv_w, v_ssm_conv_b, v_ssm_dt_bias, v_ssm_a_log, v_ssm_d, v_ssm_norm_g, v_l0_w_out, v_l0_ln1_g, v_l0_ln1_b, v_ffn0_w_up, v_ffn0_conv_w, v_ffn0_conv_b, v_ffn0_w_down, v_l0_ln2_g, v_l0_ln2_b, v_l1_w_in, v_mla_q_norm_g, v_mla_w_uq, v_mla_kv_norm_g, v_mla_w_ukv, v_l1_w_out, v_l1_ln1_g, v_l1_ln1_b, v_ffn1_w_up, v_ffn1_conv_w, v_ffn1_conv_b, v_ffn1_w_down, v_l1_ln2_g, v_l1_ln2_b):
    return _step(dict(locals()))
```

```python
import functools
import math

import jax
import jax.numpy as jnp
from jax import lax
from jax.experimental import pallas as pl
from jax.experimental.pallas import tpu as pltpu

F32 = jnp.float32
BF16 = jnp.bfloat16
HI = lax.Precision.HIGHEST

V7X_VMEM_BYTES = 64 * 1024 * 1024
VMEM_LIMIT = V7X_VMEM_BYTES - 8 * 1024 * 1024
LANES = 128
SUBLANES = 8
N_DEV = 8

D_MODEL = 1024
HEAD_DIM = 64
RWKV_DIM = 512
RWKV_HEADS = 8
RWKV_GN_EPS = 64e-5
RWKV_CHUNK = 64
SSM_DIM = 512
SSM_HEADS = 8
SSM_CHUNK = 128
SSM_STATE = 128
Q_BLOCK = 128
SB_HEADS_PER_STEP = 4
MLA_HEADS_PER_STEP = 4
ATTN_FWD_HEADS_PER_STEP = 8
MLA_NOPE = 64
MLA_ROPE = 32
ROPE_THETA = 10000.0
D_FF = 2816
DEPTH = 2
ALPHA = (2 * DEPTH) ** 0.25
L0_PAD = 3456
L1_PAD = 2048

ADAM_LR = 0.001
ADAM_B1 = 0.9
ADAM_B2 = 0.999
ADAM_EPS = 1e-08
ADAM_WD = 0.01
ADAM_STEP = 10

NEG_BIG = -1e30


def _params(sem=None):
    return pltpu.CompilerParams(dimension_semantics=sem, vmem_limit_bytes=VMEM_LIMIT)


P_F32, P_BF16, P_BF16X3 = 0, 1, 2


def _dg_raw(a, b, ca, cb, fast):
    dims = (((ca,), (cb,)), ((), ()))
    if fast == P_BF16:
        return lax.dot_general(a.astype(BF16), b.astype(BF16), dims, preferred_element_type=F32)
    prec = HI if fast == P_F32 else lax.Precision.HIGH
    return lax.dot_general(a, b, dims, precision=prec, preferred_element_type=F32)


@functools.partial(jax.custom_vjp, nondiff_argnums=(2, 3, 4))
def dg(a, b, ca, cb, fast):
    return _dg_raw(a, b, ca, cb, fast)


def _dg_fwd(a, b, ca, cb, fast):
    return _dg_raw(a, b, ca, cb, fast), (a, b)


def _dg_bwd(ca, cb, fast, res, ct):
    a, b = res
    fa, fb = 1 - ca, 1 - cb
    da = _dg_raw(ct, b, 1, fb, fast) if ca == 1 else _dg_raw(b, ct, fb, 1, fast)
    db = _dg_raw(a, ct, fa, 0, fast) if cb == 0 else _dg_raw(ct, a, 0, fa, fast)
    return da.astype(a.dtype), db.astype(b.dtype)


dg.defvjp(_dg_fwd, _dg_bwd)


def mmb(a, b):
    return dg(a, b, 1, 0, P_BF16)


def mmf(a, b):
    return dg(a, b, 1, 0, P_F32)


def mmf_nt(a, b):
    return dg(a, b, 1, 1, P_F32)


def mmf_tn(a, b):
    return dg(a, b, 0, 0, P_F32)


def mm3(a, b):
    return dg(a, b, 1, 0, P_BF16X3)


def mm3_nt(a, b):
    return dg(a, b, 1, 1, P_BF16X3)


def mm3_tn(a, b):
    return dg(a, b, 0, 0, P_BF16X3)


def _split3_dot(x, m01, cb, terms=3):
    parts, rest = [], x
    for i in range(terms):
        parts.append(rest.astype(BF16))
        if i + 1 < terms:
            rest = rest - parts[-1].astype(F32)
    rows = x.shape[0]
    out = lax.dot_general(jnp.concatenate(parts, axis=0), m01.astype(BF16), (((1,), (cb,)), ((), ())),
                          preferred_element_type=F32)
    return functools.reduce(lambda a_, b_: a_ + b_, [out[i * rows:(i + 1) * rows] for i in range(terms)])


def _lower_ones(n):
    return jnp.where(_iota((n, n), 0) >= _iota((n, n), 1), 1.0, 0.0)


SUFFIX_TERMS = 2


@jax.custom_vjp
def suffix_sum(x):
    return _split3_dot(x, _lower_ones(x.shape[1]), 0, SUFFIX_TERMS)


def _suffix_sum_fwd(x):
    return suffix_sum(x), None


def _suffix_sum_bwd(_, ct):
    return (_split3_dot(ct, _lower_ones(ct.shape[1]), 1, SUFFIX_TERMS),)


suffix_sum.defvjp(_suffix_sum_fwd, _suffix_sum_bwd)


def _iota(shape, dim):
    return lax.broadcasted_iota(jnp.int32, shape, dim)


def _softplus(x):
    return jnp.maximum(x, 0.0) + jnp.log1p(jnp.exp(-jnp.abs(x)))


def _silu(x):
    return x * jax.nn.sigmoid(x)


def _largest_tile(n, cap, mult):
    best = None
    t = mult
    while t <= min(n, cap):
        if n % t == 0:
            best = t
        t += mult
    return n if best is None else best


MM_VMEM_BUDGET = 40 * 1024 * 1024
V7X_HBM_BYTES_PER_S = 3.2e12
GRID_STEP_S = 0.35e-6


def _mm_tiles(M, N, K, a_bytes, b_bytes, has_add):
    def divs(n):
        return [d for d in range(LANES, n + 1, LANES) if n % d == 0] or [n]

    best = None
    for tm in divs(M):
        for tn in divs(N):
            if tm * tn * 4 > 12 * 1024 * 1024:
                continue
            for tk in divs(K):
                vmem = (2 * (tm * tk * a_bytes + tk * tn * b_bytes) + 2 * tm * tn * 4 * (2 if has_add else 1)
                        + (tm * tk + tk * tn) * 2 + tm * tn * 4)
                if vmem > MM_VMEM_BUDGET:
                    continue
                ni, nj, nk = M // tm, N // tn, K // tk
                a_reads = M * K * a_bytes * (1 if nk == 1 else nj)
                b_reads = K * N * b_bytes * (1 if (nk == 1 and nj == 1) else ni)
                traffic = a_reads + b_reads + M * N * 4 * (2 if has_add else 1)
                cost = traffic / V7X_HBM_BYTES_PER_S + ni * nj * nk * GRID_STEP_S
                if min(tm, tn, tk) < 256 and min(M, N, K) >= 256:
                    cost *= 1.5
                if best is None or cost < best[0]:
                    best = (cost, tm, tn, tk)
    return best[1:]


def mm(a, b, mode, name, add=None, b_rows=None, b_cols=None):
    r0, nr = b_rows or (0, b.shape[0])
    c0, nc = b_cols or (0, b.shape[1])
    if mode == "nn":
        (M, K), N = a.shape, nc
        assert nr == K
    elif mode == "nt":
        (M, K), N = a.shape, nr
        assert nc == K
    else:
        (K, M), N = a.shape, b.shape[1]
        assert b_rows is None and b_cols is None
    has_add = add is not None
    tm, tn, tk = _mm_tiles(M, N, K, a.dtype.itemsize, b.dtype.itemsize, has_add)
    nk = K // tk
    keep_a = nk == 1 and N // tn > 1 and a.dtype != BF16
    if mode == "nn":
        assert r0 % tk == 0 and c0 % tn == 0
        a_spec = pl.BlockSpec((tm, tk), lambda i, j, k: (i, k))
        b_spec = pl.BlockSpec((tk, tn), lambda i, j, k: (k + r0 // tk, j + c0 // tn))
        dims = (((1,), (0,)), ((), ()))
    elif mode == "nt":
        assert r0 % tn == 0 and c0 % tk == 0
        a_spec = pl.BlockSpec((tm, tk), lambda i, j, k: (i, k))
        b_spec = pl.BlockSpec((tn, tk), lambda i, j, k: (j + r0 // tn, k + c0 // tk))
        dims = (((1,), (1,)), ((), ()))
    else:
        a_spec = pl.BlockSpec((tk, tm), lambda i, j, k: (k, i))
        b_spec = pl.BlockSpec((tk, tn), lambda i, j, k: (k, j))
        dims = (((0,), (0,)), ((), ()))
    o_spec = pl.BlockSpec((tm, tn), lambda i, j, k: (i, j))

    def body(a_ref, b_ref, *rest):
        o_ref = rest[1] if has_add else rest[0]
        k = pl.program_id(2)
        if keep_a:
            a_bf = rest[-1]

            @pl.when(pl.program_id(1) == 0)
            def _():
                a_bf[...] = a_ref[...].astype(BF16)

            av = a_bf[...]
        else:
            av = a_ref[...].astype(BF16)
        part = lax.dot_general(av, b_ref[...].astype(BF16), dims, preferred_element_type=F32)

        @pl.when(k == 0)
        def _():
            o_ref[...] = part + rest[0][...] if has_add else part

        @pl.when(k > 0)
        def _():
            o_ref[...] += part

    ins = [a, b] + ([add] if has_add else [])
    specs = [a_spec, b_spec] + ([o_spec] if has_add else [])
    return pl.pallas_call(
        body, name=name, grid=(M // tm, N // tn, nk), in_specs=specs, out_specs=o_spec,
        out_shape=jax.ShapeDtypeStruct((M, N), F32),
        scratch_shapes=[pltpu.VMEM(a_spec.block_shape, BF16)] if keep_a else [],
        compiler_params=_params(("parallel", "arbitrary", "arbitrary")),
    )(*ins)


def _is_heads(x):
    return not isinstance(x, tuple)


def _tok_arr(x):
    return x if _is_heads(x) else x[0]


def _tok_width(x):
    return x.shape[1] * x.shape[3] if _is_heads(x) else x[1]


def _heads_spec(h, dh, tm, tiles_per_seq):
    return pl.BlockSpec((None, h, tm, dh), lambda i: (i // tiles_per_seq, 0, i % tiles_per_seq, 0))


def _x_spec(x, tm, tiles_per_seq):
    if _is_heads(x):
        return _heads_spec(x.shape[1], x.shape[3], tm, tiles_per_seq)
    return pl.BlockSpec((tm, x[1]), functools.partial(lambda i, cb: (i, cb), cb=x[2]))


def _out_spec_shape(layout, n, seq, tm, dtype=F32):
    if isinstance(layout, tuple):
        h, dh = layout
        return _heads_spec(h, dh, tm, seq // tm), jax.ShapeDtypeStruct((n // seq, h, seq, dh), dtype)
    return pl.BlockSpec((tm, layout), lambda i: (i, 0)), jax.ShapeDtypeStruct((n, layout), dtype)


def _tok_load(ref):
    if len(ref.shape) == 3:
        return jnp.concatenate([ref[hh] for hh in range(ref.shape[0])], axis=1)
    return ref[...]


def _tok_store(ref, val):
    if len(ref.shape) == 3:
        dh = ref.shape[2]
        for hh in range(ref.shape[0]):
            ref[hh] = val[:, hh * dh:(hh + 1) * dh]
    else:
        ref[...] = val


def _p_specs(ps):
    return [pl.BlockSpec(p.shape, lambda i: (0, 0)) for p in ps]


def tok_fwd(name, f, xs, ps, out_layouts, tm, n, seq, out_dtype=F32):
    nx, npar = len(xs), len(ps)
    outs = [_out_spec_shape(lay, n, seq, tm, out_dtype) for lay in out_layouts]

    def body(*refs):
        xv = [_tok_load(r) for r in refs[:nx]]
        pv = [r[...].astype(F32) for r in refs[nx:nx + npar]]
        for o, r in zip(f(*xv, *pv), refs[nx + npar:]):
            _tok_store(r, o.astype(out_dtype))

    return pl.pallas_call(
        body, name=name, grid=(n // tm,),
        in_specs=[_x_spec(x, tm, seq // tm) for x in xs] + _p_specs(ps),
        out_specs=[o[0] for o in outs], out_shape=[o[1] for o in outs],
        compiler_params=_params(("parallel",)),
    )(*[_tok_arr(x) for x in xs], *ps)


def tok_bwd(name, f, xs, ps, cts, tm, n, seq, dx_layouts=None):
    nx, npar = len(xs), len(ps)
    ct_flat = [c for group in cts for c in group]
    nct = len(ct_flat)
    dx_layouts = dx_layouts or [None] * nx
    dxs = [_out_spec_shape(lay if lay else _tok_width(x), n, seq, tm) for x, lay in zip(xs, dx_layouts)]

    def body(*refs):
        xv = [_tok_load(r) for r in refs[:nx]]
        pv = [r[...].astype(F32) for r in refs[nx:nx + npar]]
        ct_refs = refs[nx + npar:nx + npar + nct]
        dx_refs = refs[nx + npar + nct:nx + npar + nct + nx]
        dp_refs = refs[nx + npar + nct + nx:]
        cv, pos = [], 0
        for group in cts:
            acc = _tok_load(ct_refs[pos])
            for r in ct_refs[pos + 1:pos + len(group)]:
                acc = acc + _tok_load(r)
            cv.append(acc)
            pos += len(group)
        _, vjp = jax.vjp(f, *xv, *pv)
        grads = vjp(tuple(cv))
        for g, r in zip(grads[:nx], dx_refs):
            _tok_store(r, g)

        @pl.when(pl.program_id(0) == 0)
        def _():
            for r in dp_refs:
                r[...] = jnp.zeros_like(r)

        for g, r in zip(grads[nx:], dp_refs):
            r[...] += g

    ct_specs = [_heads_spec(c.shape[1], c.shape[3], tm, seq // tm) if c.ndim == 4
                else pl.BlockSpec((tm, c.shape[1]), lambda i: (i, 0)) for c in ct_flat]
    outs = pl.pallas_call(
        body, name=name, grid=(n // tm,),
        in_specs=[_x_spec(x, tm, seq // tm) for x in xs] + _p_specs(ps) + ct_specs,
        out_specs=[d[0] for d in dxs] + _p_specs(ps),
        out_shape=[d[1] for d in dxs] + [jax.ShapeDtypeStruct(p.shape, F32) for p in ps],
        compiler_params=_params(("arbitrary",)),
    )(*[_tok_arr(x) for x in xs], *ps, *ct_flat)
    return outs[:nx], outs[nx:]


def f_ln(h, y, g, b):
    pre = ALPHA * h + y
    mu = jnp.mean(pre, axis=-1, keepdims=True)
    xc = pre - mu
    var = jnp.mean(xc * xc, axis=-1, keepdims=True)
    return (xc * lax.rsqrt(var + 1e-5) * g + b,)


def _head_sel(width, nheads_pad, per):
    return jnp.where(_iota((width, nheads_pad), 0) // per == _iota((width, nheads_pad), 1), 1.0, 0.0).astype(F32)


def _head_sel_t(nheads_pad, width, per):
    return jnp.where(_iota((nheads_pad, width), 1) // per == _iota((nheads_pad, width), 0), 1.0, 0.0).astype(F32)


@jax.custom_vjp
def head_sum(x):
    return _split3_dot(x, _head_sel(RWKV_DIM, LANES, HEAD_DIM), 0)


@jax.custom_vjp
def head_spread(y):
    return _split3_dot(y, _head_sel(RWKV_DIM, LANES, HEAD_DIM), 1)


head_sum.defvjp(lambda x: (head_sum(x), None), lambda _, ct: (head_spread(ct),))
head_spread.defvjp(lambda y: (head_spread(y), None), lambda _, ct: (head_sum(ct),))


def f_rwkv_pre(r, k, v, lora, glo, w0, w2p, a0, a2p, g2, k_k, k_a):
    lane = _iota(lora.shape, 1)
    tw = jnp.where(lane < 64, jnp.tanh(lora), 0.0)
    ta = jnp.where(lane >= 64, lora, 0.0)
    log_w = -_softplus(-(w0 + mmb(tw, w2p))) - 0.5
    lw = -jnp.exp(log_w)
    a = jax.nn.sigmoid(a0 + mmb(ta, a2p))
    g = mmb(jax.nn.sigmoid(glo), g2)
    kk = k * k_k
    nrm = jnp.sqrt(jnp.maximum(head_sum(kk * kk), 1e-24))
    kkn = kk * head_spread(1.0 / nrm)
    kmod = k * (1.0 + (a - 1.0) * k_a)
    return r, v, lw, kmod, -kkn, kkn * a, g


def f_rwkv_post(y, r, kmod, v, g, ln_g, ln_b, r_k):
    inv = 1.0 / HEAD_DIM
    mu = head_spread(head_sum(y) * inv)
    yc = y - mu
    var = head_sum(yc * yc) * inv
    rstd = head_spread(lax.rsqrt(var + RWKV_GN_EPS))
    yn = yc * rstd * ln_g + ln_b
    bonus = head_spread(head_sum(r * kmod * r_k)) * v
    return ((yn + bonus) * g,)


def f_ssd_post(y, xs, z, d_skip, norm_g):
    sel_t = _head_sel_t(LANES, SSM_DIM, HEAD_DIM)
    d_e = jnp.sum(mmf(jnp.broadcast_to(d_skip, (SUBLANES, LANES)), sel_t), axis=0, keepdims=True) * (1.0 / SUBLANES)
    u = (y + xs * d_e) * _silu(z)
    first = _iota(u.shape, 1) < (SSM_DIM // 2)
    uu = u * u
    inv = 2.0 / SSM_DIM
    ms0 = jnp.sum(jnp.where(first, uu, 0.0), axis=-1, keepdims=True) * inv
    ms1 = jnp.sum(jnp.where(first, 0.0, uu), axis=-1, keepdims=True) * inv
    ms = jnp.where(first, ms0, ms1)
    return (u * lax.rsqrt(ms + 1e-5) * norm_g,)


def f_mla_pre(cq, ckv, qg, wq_nope, wq_rope, kvg, wk_nope, wv):
    def rms(x, g):
        return x * lax.rsqrt(jnp.mean(x * x, axis=-1, keepdims=True) + 1e-6) * g
    q_in, kv_in = rms(cq, qg), rms(ckv, kvg)
    return mmb(q_in, wq_nope), mmb(q_in, wq_rope), mmb(kv_in, wk_nope), mmb(kv_in, wv)


def f_same(*xs):
    return xs


def f_concat(*xs):
    return (jnp.concatenate(xs, axis=1),)


def _shift_down(x, s, row):
    return x if s == 0 else jnp.where(row >= s, pltpu.roll(x, s, 0), 0.0)


def _shift_up(x, s, row, t):
    return x if s == 0 else jnp.where(row < t - s, pltpu.roll(x, t - s, 0), 0.0)


def dwconv_fwd(name, u, colmap, w, b, taps, silu, upmap=None, out_dtype=F32):
    bsz, t, _ = u.shape
    c = w.shape[1]
    tc = LANES
    has_up = upmap is not None

    def body(*refs):
        u_ref, w_ref, b_ref = refs[:3]
        o_ref = refs[-1]
        uv = u_ref[...]
        wv = w_ref[...]
        row = _iota(uv.shape, 0)
        acc = jnp.broadcast_to(b_ref[...], uv.shape)
        for i in range(taps):
            acc = acc + wv[i:i + 1, :] * _shift_down(uv, taps - 1 - i, row)
        if silu:
            acc = _silu(acc)
        if has_up:
            acc = acc * refs[3][...]
        o_ref[...] = acc.astype(out_dtype)

    specs = [pl.BlockSpec((None, t, tc), lambda bb, j: (bb, 0, colmap(j))),
             pl.BlockSpec((taps, tc), lambda bb, j: (0, j)),
             pl.BlockSpec((1, tc), lambda bb, j: (0, j))]
    ins = [u, w, b]
    if has_up:
        specs.append(pl.BlockSpec((None, t, tc), lambda bb, j: (bb, 0, upmap(j))))
        ins.append(u)
    return pl.pallas_call(
        body, name=name, grid=(bsz, c // tc), in_specs=specs,
        out_specs=pl.BlockSpec((None, t, tc), lambda bb, j: (bb, 0, j)),
        out_shape=jax.ShapeDtypeStruct((bsz, t, c), out_dtype),
        compiler_params=_params(("parallel", "parallel")),
    )(*ins)


def dwconv_bwd(name, u, colmap, w, b, taps, silu, dout, upmap=None, grad_dtype=F32):
    bsz, t, _ = u.shape
    c = w.shape[1]
    tc = LANES
    has_up = upmap is not None

    def body(*refs):
        u_ref, w_ref, b_ref, d_ref = refs[:4]
        nin = 5 if has_up else 4
        du_ref, dw_ref, db_ref = refs[nin:nin + 3]
        uv = u_ref[...]
        wv = w_ref[...]
        dv = d_ref[...]
        row = _iota(uv.shape, 0)
        shifted = [_shift_down(uv, taps - 1 - i, row) for i in range(taps)]
        cg = jnp.broadcast_to(b_ref[...], uv.shape)
        for i in range(taps):
            cg = cg + wv[i:i + 1, :] * shifted[i]
        if silu:
            sg = jax.nn.sigmoid(cg)
            act = cg * sg
            dact_dcg = sg * (1.0 + cg * (1.0 - sg))
        else:
            act = cg
            dact_dcg = None
        if has_up:
            refs[nin + 3][...] = (dv * act).astype(grad_dtype)
            dv = dv * refs[4][...]
        dcg = dv * dact_dcg if silu else dv
        du = jnp.zeros_like(uv)
        for i in range(taps):
            du = du + wv[i:i + 1, :] * _shift_up(dcg, taps - 1 - i, row, t)
        du_ref[...] = du.astype(grad_dtype)

        @pl.when(pl.program_id(1) == 0)
        def _():
            dw_ref[...] = jnp.zeros_like(dw_ref)
            db_ref[...] = jnp.zeros_like(db_ref)

        for i in range(taps):
            dw_ref[i:i + 1, :] += jnp.sum(dcg * shifted[i], axis=0, keepdims=True)
        db_ref[...] += jnp.sum(dcg, axis=0, keepdims=True)

    specs = [pl.BlockSpec((None, t, tc), lambda j, bb: (bb, 0, colmap(j))),
             pl.BlockSpec((taps, tc), lambda j, bb: (0, j)),
             pl.BlockSpec((1, tc), lambda j, bb: (0, j)),
             pl.BlockSpec((None, t, tc), lambda j, bb: (bb, 0, j))]
    ins = [u, w, b, dout]
    if has_up:
        specs.append(pl.BlockSpec((None, t, tc), lambda j, bb: (bb, 0, upmap(j))))
        ins.append(u)
    big = pl.BlockSpec((None, t, tc), lambda j, bb: (bb, 0, j))
    out_specs = [big, pl.BlockSpec((taps, tc), lambda j, bb: (0, j)), pl.BlockSpec((1, tc), lambda j, bb: (0, j))]
    out_shape = [jax.ShapeDtypeStruct((bsz, t, c), grad_dtype), jax.ShapeDtypeStruct((taps, c), F32),
                 jax.ShapeDtypeStruct((1, c), F32)]
    if has_up:
        out_specs.append(big)
        out_shape.append(jax.ShapeDtypeStruct((bsz, t, c), grad_dtype))
    return pl.pallas_call(
        body, name=name, grid=(c // tc, bsz), in_specs=specs, out_specs=out_specs, out_shape=out_shape,
        compiler_params=_params(("parallel", "arbitrary")),
    )(*ins)


def _each(f, *lists):
    return [f(*xs) for xs in zip(*lists)]


def rwkv_chunk(s0, r, lw, k, v, al, be):
    c = r[0].shape[0]
    ii, jj = _iota((c, c), 0), _iota((c, c), 1)
    incl, strict = ii >= jj, ii > jj
    ones_incl = jnp.where(incl, 1.0, 0.0)
    eye = jnp.where(ii == jj, 1.0, 0.0)
    cum = _each(lambda x: mmf(ones_incl, x), lw)
    gam_inv = _each(lambda x: jnp.exp(-x), cum)
    at = _each(lambda a_, c_, l_: a_ * jnp.exp(c_ - l_), al, cum, lw)
    rt = _each(lambda r_, c_: r_ * jnp.exp(c_), r, cum)
    bt = _each(lambda b_, g_: b_ * g_, be, gam_inv)
    kt = _each(lambda k_, g_: k_ * g_, k, gam_inv)
    a_b = _each(lambda x, y_: jnp.where(strict, mm3_nt(x, y_), 0.0), at, bt)
    a_k = _each(lambda x, y_: jnp.where(strict, mm3_nt(x, y_), 0.0), at, kt)
    rhs0 = _each(mm3_nt, at, s0)
    rhs = _each(lambda x, a_, v_: x + mm3(a_, v_), rhs0, a_k, v)
    p = _each(lambda x: eye + x, a_b)
    m = a_b
    for _ in range(int(math.log2(c)) - 1):
        m = _each(mm3, m, m)
        p = _each(lambda p_, m_: p_ + mm3(p_, m_), p, m)
    u = _each(mm3, p, rhs)
    r_b = _each(lambda x, y_: jnp.where(incl, mm3_nt(x, y_), 0.0), rt, bt)
    r_k = _each(lambda x, y_: jnp.where(incl, mm3_nt(x, y_), 0.0), rt, kt)
    y0 = _each(mm3_nt, rt, s0)
    y1 = _each(lambda y_, b_, u_: y_ + mm3(b_, u_), y0, r_b, u)
    y = _each(lambda y_, k_, v_: y_ + mm3(k_, v_), y1, r_k, v)
    su = _each(mm3_tn, u, bt)
    sv = _each(mm3_tn, v, kt)
    s1 = _each(lambda s_, a_, b_, l_: (s_ + a_ + b_) * jnp.exp(jnp.sum(l_, axis=0, keepdims=True)), s0, su, sv, lw)
    return y, s1


def rwkv_scan_fwd(r, lw, k, v, al, be, ride=None):
    bsz, h, t, d = r.shape
    c = RWKV_CHUNK
    nc = t // c
    grid = (bsz, nc)
    r_in, r_specs, r_out, r_ospecs, r_scr = _ride_args(ride)

    def body(*refs):
        r_ref, lw_ref, k_ref, v_ref, al_ref, be_ref = refs[:6]
        y_ref, st_ref = refs[6 + len(r_in):8 + len(r_in)]
        s_scr = refs[8 + 2 * len(r_in)]
        if ride is not None:
            first, last = _grid_first_last(grid)
            copies = _ride_start(refs[6:6 + len(r_in)], refs[8 + len(r_in):8 + 2 * len(r_in)], refs[-3:], ride[1], first)

        @pl.when(pl.program_id(1) == 0)
        def _():
            s_scr[...] = jnp.zeros_like(s_scr)

        heads = lambda ref: [ref[hh] for hh in range(h)]
        s0 = heads(s_scr)
        y, s1 = rwkv_chunk(s0, heads(r_ref), heads(lw_ref), heads(k_ref), heads(v_ref), heads(al_ref),
                           heads(be_ref))
        for hh in range(h):
            st_ref[hh] = s0[hh]
            y_ref[hh] = y[hh]
            s_scr[hh] = s1[hh]
        if ride is not None:
            _ride_wait(copies, last)

    seq = pl.BlockSpec((None, h, c, d), lambda b, i: (b, 0, i, 0))
    return pl.pallas_call(
        body, name="rwkv_scan_fwd", grid=grid, in_specs=[seq] * 6 + r_specs,
        out_specs=[seq, pl.BlockSpec((None, h, None, d, d), lambda b, i: (b, 0, i, 0, 0))] + r_ospecs,
        out_shape=[jax.ShapeDtypeStruct((bsz, h, t, d), F32), jax.ShapeDtypeStruct((bsz, h, nc, d, d), F32)] + r_out,
        scratch_shapes=[pltpu.VMEM((h, d, d), F32)] + r_scr,
        compiler_params=_params(("arbitrary", "arbitrary")),
    )(r, lw, k, v, al, be, *r_in)


def rwkv_scan_bwd(r, lw, k, v, al, be, states, dy, ride=None):
    bsz, h, t, d = r.shape
    c = RWKV_CHUNK
    nc = t // c
    grid = (bsz, nc)
    r_in, r_specs, r_out, r_ospecs, r_scr = _ride_args(ride)

    def body(*refs):
        r_ref, lw_ref, k_ref, v_ref, al_ref, be_ref, st_ref, dy_ref = refs[:8]
        nin = 8 + len(r_in)
        dr_ref, dlw_ref, dk_ref, dv_ref, dal_ref, dbe_ref = refs[nin:nin + 6]
        ds_scr = refs[nin + 6 + len(r_in)]
        if ride is not None:
            first, last = _grid_first_last(grid)
            copies = _ride_start(refs[8:nin], refs[nin + 6:nin + 6 + len(r_in)], refs[-3:], ride[1], first)

        @pl.when(pl.program_id(1) == 0)
        def _():
            ds_scr[...] = jnp.zeros_like(ds_scr)

        heads = lambda ref: [ref[hh] for hh in range(h)]
        _, vjp = jax.vjp(rwkv_chunk, heads(st_ref), heads(r_ref), heads(lw_ref), heads(k_ref), heads(v_ref),
                         heads(al_ref), heads(be_ref))
        grads = vjp((heads(dy_ref), heads(ds_scr)))
        for ref, gl in zip((ds_scr, dr_ref, dlw_ref, dk_ref, dv_ref, dal_ref, dbe_ref), grads):
            for hh in range(h):
                ref[hh] = gl[hh]
        if ride is not None:
            _ride_wait(copies, last)

    seq = pl.BlockSpec((None, h, c, d), lambda b, i: (b, 0, nc - 1 - i, 0))
    st = pl.BlockSpec((None, h, None, d, d), lambda b, i: (b, 0, nc - 1 - i, 0, 0))
    return pl.pallas_call(
        body, name="rwkv_scan_bwd", grid=grid, in_specs=[seq] * 6 + [st, seq] + r_specs,
        out_specs=[seq] * 6 + r_ospecs, out_shape=[jax.ShapeDtypeStruct((bsz, h, t, d), F32)] * 6 + r_out,
        scratch_shapes=[pltpu.VMEM((h, d, d), F32)] + r_scr,
        compiler_params=_params(("arbitrary", "arbitrary")),
    )(r, lw, k, v, al, be, states, dy, *r_in)


def ssd_chunk(st, xs, bm, cm, dtr, dt_bias, a_log):
    n = SSM_CHUNK
    ii, jj = _iota((n, n), 0), _iota((n, n), 1)
    incl = ii >= jj
    lane = _iota((n, LANES), 1)
    dt = _softplus(dtr + dt_bias)
    a = dt * (-jnp.exp(a_log))
    acum = mmf(jnp.where(incl, 1.0, 0.0), a)
    last_row = jnp.where(jj == n - 1, 1.0, 0.0)
    cb = [mm3_nt(cm[g], bm[g]) for g in range(2)]
    pairs, heads = range(4), range(SSM_HEADS)
    e_m = [jnp.where(_iota((LANES, LANES), 0) == 2 * m + _iota((LANES, LANES), 1) // HEAD_DIM, 1.0, 0.0)
           for m in pairs]
    dt_m = [mmf(dt, e_m[m]) for m in pairs]
    ac_m = [mmf(acum, e_m[m]) for m in pairs]
    x = [xs[m] * dt_m[m] for m in pairs]
    last_m = [mmf(last_row, ac_m[m]) for m in pairs]
    colb = [mmf(acum, jnp.where(_iota((LANES, n), 0) == h, 1.0, 0.0)) for h in heads]
    decay = [jnp.exp(jnp.where(incl, colb[h] - colb[h].T, NEG_BIG)) for h in heads]
    yh = [mm3(cb[h // 4] * decay[h], x[h // 2]) for h in heads]
    y_off = [mm3(cm[m // 2], st[m]) for m in pairs]
    ys = [jnp.where(lane // HEAD_DIM == 0, yh[2 * m], yh[2 * m + 1]) + jnp.exp(ac_m[m]) * y_off[m] for m in pairs]
    st_in = [mm3_tn(bm[m // 2], x[m] * jnp.exp(last_m[m] - ac_m[m])) for m in pairs]
    st_new = [jnp.exp(last_m[m]) * st[m] + st_in[m] for m in pairs]
    return tuple(ys), tuple(st_new)


def _ssd_load(xbc_ref, dtr_ref):
    xs = tuple(xbc_ref[:, m * LANES:(m + 1) * LANES] for m in range(4))
    bm = tuple(xbc_ref[:, SSM_DIM + g * LANES:SSM_DIM + (g + 1) * LANES] for g in range(2))
    cm = tuple(xbc_ref[:, SSM_DIM + 2 * LANES + g * LANES:SSM_DIM + 2 * LANES + (g + 1) * LANES] for g in range(2))
    return xs, bm, cm, dtr_ref[...]


def ssd_fwd(xbc, proj, dt_col, dt_bias, a_log):
    bsz, t, _ = xbc.shape
    n = SSM_CHUNK
    nc = t // n

    def body(xbc_ref, dtr_ref, dtb_ref, al_ref, y_ref, st_ref, s_scr):
        @pl.when(pl.program_id(1) == 0)
        def _():
            s_scr[...] = jnp.zeros_like(s_scr)

        st = tuple(s_scr[m] for m in range(4))
        for m in range(4):
            st_ref[m] = st[m]
        xs, bm, cm, dtr = _ssd_load(xbc_ref, dtr_ref)
        ys, st_new = ssd_chunk(st, xs, bm, cm, dtr, dtb_ref[...], al_ref[...])
        for m in range(4):
            y_ref[:, m * LANES:(m + 1) * LANES] = ys[m]
            s_scr[m] = st_new[m]

    vec = pl.BlockSpec((1, LANES), lambda b, i: (0, 0))
    return pl.pallas_call(
        body, name="ssd_fwd", grid=(bsz, nc),
        in_specs=[pl.BlockSpec((None, n, 2 * SSM_DIM), lambda b, i: (b, i, 0)),
                  pl.BlockSpec((None, n, LANES), lambda b, i: (b, i, dt_col)), vec, vec],
        out_specs=[pl.BlockSpec((None, n, SSM_DIM), lambda b, i: (b, i, 0)),
                   pl.BlockSpec((None, None, 4, SSM_STATE, LANES), lambda b, i: (b, i, 0, 0, 0))],
        out_shape=[jax.ShapeDtypeStruct((bsz, t, SSM_DIM), F32),
                   jax.ShapeDtypeStruct((bsz, nc, 4, SSM_STATE, LANES), F32)],
        scratch_shapes=[pltpu.VMEM((4, SSM_STATE, LANES), F32)],
        compiler_params=_params(("parallel", "arbitrary")),
    )(xbc, proj, dt_bias, a_log)


def ssd_bwd(xbc, proj, dt_col, dt_bias, a_log, states, dy, dxs_extra):
    bsz, t, _ = xbc.shape
    n = SSM_CHUNK
    nc = t // n

    def body(xbc_ref, dtr_ref, dtb_ref, al_ref, st_ref, dy_ref, ex_ref,
             dxbc_ref, ddtr_ref, ddtb_ref, dal_ref, ds_scr):
        first = jnp.logical_and(pl.program_id(0) == 0, pl.program_id(1) == 0)

        @pl.when(pl.program_id(1) == 0)
        def _():
            ds_scr[...] = jnp.zeros_like(ds_scr)

        @pl.when(first)
        def _():
            ddtb_ref[...] = jnp.zeros_like(ddtb_ref)
            dal_ref[...] = jnp.zeros_like(dal_ref)

        st = tuple(st_ref[m] for m in range(4))
        xs, bm, cm, dtr = _ssd_load(xbc_ref, dtr_ref)
        _, vjp = jax.vjp(ssd_chunk, st, xs, bm, cm, dtr, dtb_ref[...], al_ref[...])
        dys = tuple(dy_ref[:, m * LANES:(m + 1) * LANES] for m in range(4))
        dst_in = tuple(ds_scr[m] for m in range(4))
        dst, dxs, dbm, dcm, ddtr, ddtb, dal = vjp((dys, dst_in))
        for m in range(4):
            ds_scr[m] = dst[m]
            sl = slice(m * LANES, (m + 1) * LANES)
            dxbc_ref[:, sl] = dxs[m] + ex_ref[:, sl]
        for g in range(2):
            dxbc_ref[:, SSM_DIM + g * LANES:SSM_DIM + (g + 1) * LANES] = dbm[g]
            dxbc_ref[:, SSM_DIM + 2 * LANES + g * LANES:SSM_DIM + 2 * LANES + (g + 1) * LANES] = dcm[g]
        ddtr_ref[...] = ddtr
        ddtb_ref[...] += ddtb
        dal_ref[...] += dal

    vec = pl.BlockSpec((1, LANES), lambda b, i: (0, 0))
    rev = lambda b, i: (b, nc - 1 - i, 0)
    return pl.pallas_call(
        body, name="ssd_bwd", grid=(bsz, nc),
        in_specs=[pl.BlockSpec((None, n, 2 * SSM_DIM), rev),
                  pl.BlockSpec((None, n, LANES), lambda b, i: (b, nc - 1 - i, dt_col)), vec, vec,
                  pl.BlockSpec((None, None, 4, SSM_STATE, LANES), lambda b, i: (b, nc - 1 - i, 0, 0, 0)),
                  pl.BlockSpec((None, n, SSM_DIM), rev), pl.BlockSpec((None, n, SSM_DIM), rev)],
        out_specs=[pl.BlockSpec((None, n, 2 * SSM_DIM), rev), pl.BlockSpec((None, n, LANES), rev), vec, vec],
        out_shape=[jax.ShapeDtypeStruct((bsz, t, 2 * SSM_DIM), F32), jax.ShapeDtypeStruct((bsz, t, LANES), F32),
                   jax.ShapeDtypeStruct((1, LANES), F32), jax.ShapeDtypeStruct((1, LANES), F32)],
        scratch_shapes=[pltpu.VMEM((4, SSM_STATE, LANES), F32)],
        compiler_params=_params(("arbitrary", "arbitrary")),
    )(xbc, proj, dt_bias, a_log, states, dy, dxs_extra)


def sb_block(q, kj, vj, carry, maskf):
    mask = maskf > 0.5
    z = _each(lambda q_, k_: dg(q_, k_, 1, 1, P_BF16) * (HEAD_DIM ** -0.5), q, kj)
    lk = _each(lambda z_: jnp.where(mask, -_softplus(z_), 0.0), z)
    sfx = _each(suffix_sum, lk)
    att = _each(lambda z_, c_, s_: jnp.exp(jnp.where(mask, z_ + c_ + s_, NEG_BIG)), z, carry, sfx)
    out = _each(mmb, att, vj)
    return out, _each(lambda c_, k_: c_ + jnp.sum(k_, axis=1, keepdims=True), carry, lk)


def _sb_mask(qi, j):
    n = Q_BLOCK
    return jnp.where(j * n + _iota((n, n), 1) < qi * n + _iota((n, n), 0), 1.0, 0.0)


def sb_fwd(q, k, v, ride=None):
    bsz, h, t, d = q.shape
    n = Q_BLOCK
    hp = ATTN_FWD_HEADS_PER_STEP
    grid = (bsz, h // hp, t // n)
    r_in, r_specs, r_out, r_ospecs, r_scr = _ride_args(ride)

    def body(*refs):
        q_ref, k_ref, v_ref = refs[:3]
        o_ref, c_ref = refs[3 + len(r_in):5 + len(r_in)]
        if ride is not None:
            first, last = _grid_first_last(grid)
            copies = _ride_start(refs[3:3 + len(r_in)], refs[5 + len(r_in):5 + 2 * len(r_in)], refs[-3:], ride[1], first)
        qi = pl.program_id(2)
        lane = _iota((n, LANES), 1)

        c_ref[...] = jnp.zeros_like(c_ref)
        o_ref[...] = jnp.zeros_like(o_ref)

        def step(i, carry):
            j = qi - i
            rows = pl.ds(pl.multiple_of(j * n, n), n)
            for hh in range(hp):
                c_ref[hh] = jnp.where(lane == j, carry[hh], c_ref[hh])
            o, carry = sb_block([q_ref[hh] for hh in range(hp)], [k_ref[hh, rows, :] for hh in range(hp)],
                                [v_ref[hh, rows, :] for hh in range(hp)], carry, _sb_mask(qi, j))
            for hh in range(hp):
                o_ref[hh] += o[hh]
            return carry

        lax.fori_loop(0, qi + 1, step, [jnp.zeros((n, 1), F32) for _ in range(hp)])
        if ride is not None:
            _ride_wait(copies, last)

    blk = pl.BlockSpec((None, hp, n, d), lambda b, hg, i: (b, hg, i, 0))
    cblk = pl.BlockSpec((None, hp, n, LANES), lambda b, hg, i: (b, hg, i, 0))
    full = pl.BlockSpec((None, hp, t, d), lambda b, hg, i: (b, hg, 0, 0))
    return pl.pallas_call(
        body, name="sb_fwd", grid=grid, in_specs=[blk, full, full] + r_specs, out_specs=[blk, cblk] + r_ospecs,
        out_shape=[jax.ShapeDtypeStruct((bsz, h, t, d), F32), jax.ShapeDtypeStruct((bsz, h, t, LANES), F32)] + r_out,
        scratch_shapes=r_scr, compiler_params=_params(("arbitrary", "arbitrary", "arbitrary")),
    )(q, k, v, *r_in)


def sb_bwd(q, k, v, kept, do, ride=None):
    bsz, h, t, d = q.shape
    n = Q_BLOCK
    hp = SB_HEADS_PER_STEP
    grid = (bsz, h // hp, t // n)
    r_in, r_specs, r_out, r_ospecs, r_scr = _ride_args(ride)

    def body(*refs):
        q_ref, k_ref, v_ref, c_ref, do_ref = refs[:5]
        nin = 5 + len(r_in)
        dq_ref, dk_ref, dv_ref = refs[nin:nin + 3]
        if ride is not None:
            first, last = _grid_first_last(grid)
            copies = _ride_start(refs[5:nin], refs[nin + 3:nin + 3 + len(r_in)], refs[-3:], ride[1], first)
        qi = pl.program_id(2)

        @pl.when(qi == 0)
        def _():
            dk_ref[...] = jnp.zeros_like(dk_ref)
            dv_ref[...] = jnp.zeros_like(dv_ref)

        heads = range(hp)
        qv = [q_ref[hh] for hh in heads]
        kept_v = [c_ref[hh] for hh in heads]
        lane = _iota((n, LANES), 1)

        dq_ref[...] = jnp.zeros_like(dq_ref)

        def bwd_step(j, dcarry):
            rows = pl.ds(pl.multiple_of(j * n, n), n)
            carry_in = [jnp.sum(jnp.where(lane == j, t_, 0.0), axis=1, keepdims=True) for t_ in kept_v]
            _, vjp = jax.vjp(sb_block, qv, [k_ref[hh, rows, :] for hh in heads],
                             [v_ref[hh, rows, :] for hh in heads], carry_in, _sb_mask(qi, j))
            dqj, dkj, dvj, dc, _ = vjp(([do_ref[hh] for hh in heads], dcarry))
            for hh in heads:
                dq_ref[hh] += dqj[hh]
                dk_ref[hh, rows, :] += dkj[hh]
                dv_ref[hh, rows, :] += dvj[hh]
            return dc

        lax.fori_loop(0, qi + 1, bwd_step, [jnp.zeros((n, 1), F32) for _ in heads])
        if ride is not None:
            _ride_wait(copies, last)

    blk = pl.BlockSpec((None, hp, n, d), lambda b, hg, i: (b, hg, i, 0))
    cblk = pl.BlockSpec((None, hp, n, LANES), lambda b, hg, i: (b, hg, i, 0))
    full = pl.BlockSpec((None, hp, t, d), lambda b, hg, i: (b, hg, 0, 0))
    shp = jax.ShapeDtypeStruct((bsz, h, t, d), F32)
    return pl.pallas_call(
        body, name="sb_bwd", grid=grid, in_specs=[blk, full, full, cblk, blk] + r_specs,
        out_specs=[blk, full, full] + r_ospecs, out_shape=[shp, shp, shp] + r_out,
        scratch_shapes=r_scr, compiler_params=_params(("arbitrary", "arbitrary", "arbitrary")),
    )(q, k, v, kept, do, *r_in)


def _bdot(a, b, ca, cb):
    return _dg_raw(a, b, ca, cb, P_BF16)


def _mla_scores(qn, qp, knj, kpj, qi, j):
    n = Q_BLOCK
    mask = j * n + _iota((n, n), 1) <= qi * n + _iota((n, n), 0)
    scale = (MLA_NOPE + MLA_ROPE) ** -0.5
    return _each(lambda a_, b_, k_: jnp.where(mask, (_bdot(a_, k_, 1, 1) + _bdot(b_, kpj, 1, 1)) * scale, NEG_BIG),
                 qn, qp, knj)


def _mla_specs(t, hp):
    n = Q_BLOCK
    return (pl.BlockSpec((None, hp, n, MLA_NOPE), lambda b, hg, i: (b, hg, i, 0)),
            pl.BlockSpec((None, hp, n, MLA_ROPE), lambda b, hg, i: (b, hg, i, 0)),
            pl.BlockSpec((None, hp, t, MLA_NOPE), lambda b, hg, i: (b, hg, 0, 0)),
            pl.BlockSpec((None, None, t, MLA_ROPE), lambda b, hg, i: (b, 0, 0, 0)),
            pl.BlockSpec((None, hp, n, 1), lambda b, hg, i: (b, hg, i, 0)))


def mla_fwd(qn, qp, kn, kp, v):
    bsz, h, t, _ = qn.shape
    n, hp = Q_BLOCK, ATTN_FWD_HEADS_PER_STEP
    heads = range(hp)

    def body(qn_ref, qp_ref, kn_ref, kp_ref, v_ref, o_ref, lse_ref):
        qi = pl.program_id(2)
        qn_v, qp_v = [qn_ref[hh] for hh in heads], [qp_ref[hh] for hh in heads]

        o_ref[...] = jnp.zeros_like(o_ref)

        def step(j, state):
            m, l = state
            rows = pl.ds(pl.multiple_of(j * n, n), n)
            s = _mla_scores(qn_v, qp_v, [kn_ref[hh, rows, :] for hh in heads], kp_ref[rows, :], qi, j)
            m_new = _each(lambda m_, s_: jnp.maximum(m_, jnp.max(s_, axis=1, keepdims=True)), m, s)
            p = _each(lambda s_, m_: jnp.exp(s_ - m_), s, m_new)
            corr = _each(lambda a_, b_: jnp.exp(a_ - b_), m, m_new)
            l = _each(lambda l_, c_, p_: l_ * c_ + jnp.sum(p_, axis=1, keepdims=True), l, corr, p)
            pv = _each(lambda p_, v_: _bdot(p_, v_, 1, 0), p, [v_ref[hh, rows, :] for hh in heads])
            for hh in heads:
                o_ref[hh] = o_ref[hh] * corr[hh] + pv[hh]
            return m_new, l

        init = ([jnp.full((n, 1), NEG_BIG, F32) for _ in heads], [jnp.zeros((n, 1), F32) for _ in heads])
        m, l = lax.fori_loop(0, qi + 1, step, init)
        for hh in heads:
            o_ref[hh] = o_ref[hh] / l[hh]
            lse_ref[hh] = m[hh] + jnp.log(l[hh])

    qn_s, qp_s, kn_s, kp_s, row_s = _mla_specs(t, hp)
    return pl.pallas_call(
        body, name="mla_fwd", grid=(bsz, h // hp, t // n), in_specs=[qn_s, qp_s, kn_s, kp_s, kn_s],
        out_specs=[qn_s, row_s],
        out_shape=[jax.ShapeDtypeStruct(qn.shape, F32), jax.ShapeDtypeStruct((bsz, h, t, 1), F32)],
        compiler_params=_params(("parallel", "parallel", "arbitrary")),
    )(qn, qp, kn, kp, v)


def mla_bwd(qn, qp, kn, kp, v, o, lse, do):
    bsz, h, t, _ = qn.shape
    n, hp = Q_BLOCK, MLA_HEADS_PER_STEP
    heads = range(hp)
    scale = (MLA_NOPE + MLA_ROPE) ** -0.5

    def body(qn_ref, qp_ref, kn_ref, kp_ref, v_ref, o_ref, lse_ref, do_ref,
             dqn_ref, dqp_ref, dkn_ref, dkp_ref, dv_ref):
        hg, qi = pl.program_id(1), pl.program_id(2)

        @pl.when(qi == 0)
        def _():
            dkn_ref[...] = jnp.zeros_like(dkn_ref)
            dv_ref[...] = jnp.zeros_like(dv_ref)

        @pl.when(jnp.logical_and(qi == 0, hg == 0))
        def _():
            dkp_ref[...] = jnp.zeros_like(dkp_ref)

        qn_v, qp_v = [qn_ref[hh] for hh in heads], [qp_ref[hh] for hh in heads]
        do_v, lse_v = [do_ref[hh] for hh in heads], [lse_ref[hh] for hh in heads]
        dsum = [jnp.sum(do_v[hh] * o_ref[hh], axis=1, keepdims=True) for hh in heads]

        dqn_ref[...] = jnp.zeros_like(dqn_ref)
        dqp_ref[...] = jnp.zeros_like(dqp_ref)

        def step(j, _):
            rows = pl.ds(pl.multiple_of(j * n, n), n)
            knj, vj, kpj = [kn_ref[hh, rows, :] for hh in heads], [v_ref[hh, rows, :] for hh in heads], kp_ref[rows, :]
            s = _mla_scores(qn_v, qp_v, knj, kpj, qi, j)
            p = _each(lambda s_, l_: jnp.exp(s_ - l_), s, lse_v)
            dp = _each(lambda d_, v_: _bdot(d_, v_, 1, 1), do_v, vj)
            ds = _each(lambda p_, dp_, d_: p_ * (dp_ - d_) * scale, p, dp, dsum)
            dqn = _each(lambda ds_, k_: _bdot(ds_, k_, 1, 0), ds, knj)
            dqp = _each(lambda ds_: _bdot(ds_, kpj, 1, 0), ds)
            dkn = _each(lambda ds_, q_: _bdot(ds_, q_, 0, 0), ds, qn_v)
            dv = _each(lambda p_, d_: _bdot(p_, d_, 0, 0), p, do_v)
            dkp = _each(lambda ds_, q_: _bdot(ds_, q_, 0, 0), ds, qp_v)
            for hh in heads:
                dqn_ref[hh] += dqn[hh]
                dqp_ref[hh] += dqp[hh]
                dkn_ref[hh, rows, :] += dkn[hh]
                dv_ref[hh, rows, :] += dv[hh]
            dkp_ref[rows, :] += functools.reduce(lambda a_, b_: a_ + b_, dkp)
            return 0

        lax.fori_loop(0, qi + 1, step, 0)

    qn_s, qp_s, kn_s, kp_s, row_s = _mla_specs(t, hp)
    return pl.pallas_call(
        body, name="mla_bwd", grid=(bsz, h // hp, t // n),
        in_specs=[qn_s, qp_s, kn_s, kp_s, kn_s, qn_s, row_s, qn_s],
        out_specs=[qn_s, qp_s, kn_s, kp_s, kn_s],
        out_shape=[jax.ShapeDtypeStruct(qn.shape, F32), jax.ShapeDtypeStruct(qp.shape, F32),
                   jax.ShapeDtypeStruct(kn.shape, F32), jax.ShapeDtypeStruct(kp.shape, F32),
                   jax.ShapeDtypeStruct(v.shape, F32)],
        compiler_params=_params(("parallel", "arbitrary", "arbitrary")),
    )(qn, qp, kn, kp, v, o, lse, do)


def rope(name, x, pos, inv_freq, sign):
    bsz, hx, t, d = x.shape
    half = d // 2

    tt = _largest_tile(t, 512, SUBLANES)

    def body(x_ref, pos_ref, f_ref, o_ref):
        ang = pos_ref[...].astype(F32) * f_ref[...]
        cos, sin = jnp.cos(ang), sign * jnp.sin(ang)
        ri, ci = _iota((d, d), 0), _iota((d, d), 1)
        rot = jnp.where(ri == ci + half, -1.0, 0.0) + jnp.where(ri + half == ci, 1.0, 0.0)
        for hh in range(hx):
            xv = x_ref[hh]
            o_ref[hh] = xv * cos + mmf(xv, rot) * sin

    blk = pl.BlockSpec((None, hx, tt, d), lambda b, i: (b, 0, i, 0))
    return pl.pallas_call(
        body, name=name, grid=(bsz, t // tt),
        in_specs=[blk, pl.BlockSpec((None, tt, 1), lambda b, i: (b, i, 0)), pl.BlockSpec((1, d), lambda b, i: (0, 0))],
        out_specs=blk, out_shape=jax.ShapeDtypeStruct(x.shape, F32),
        compiler_params=_params(("parallel", "parallel")),
    )(x, pos, inv_freq)


def loss_head(h, target, tm):
    n, d = h.shape

    def body(h_ref, t_ref, dh_ref, l_ref):
        @pl.when(pl.program_id(0) == 0)
        def _():
            l_ref[...] = jnp.zeros_like(l_ref)

        e = h_ref[...] - t_ref[...]
        dh_ref[...] = e * (1.0 / d)
        l_ref[...] += jnp.sum(e * e, axis=(0, 1), keepdims=True) * (0.5 / d)

    row = pl.BlockSpec((tm, d), lambda i: (i, 0))
    dh, l = pl.pallas_call(
        body, name="loss_head", grid=(n // tm,), in_specs=[row, row],
        out_specs=[row, pl.BlockSpec((SUBLANES, LANES), lambda i: (0, 0))],
        out_shape=[jax.ShapeDtypeStruct((n, d), F32), jax.ShapeDtypeStruct((SUBLANES, LANES), F32)],
        compiler_params=_params(("arbitrary",)),
    )(h, target)
    return dh, l[0, 0]


def _exchange_copies(src_refs, out_refs, send_sems, recv_sems, local_sems, gather):
    x, y, c = lax.axis_index("x"), lax.axis_index("y"), lax.axis_index("c")
    me = 4 * x + 2 * y + c
    local, remote = [], []
    for p, (src_ref, out_ref) in enumerate(zip(src_refs, out_refs)):
        local.append(pltpu.make_async_copy(src_ref if gather else src_ref.at[me], out_ref.at[me], local_sems.at[p]))
        for m in range(1, N_DEV):
            px, py, pc = x ^ (m >> 2), y ^ ((m >> 1) & 1), c ^ (m & 1)
            peer = 4 * px + 2 * py + pc
            remote.append(pltpu.make_async_remote_copy(
                src_ref=src_ref if gather else src_ref.at[peer], dst_ref=out_ref.at[me],
                send_sem=send_sems.at[p, m], recv_sem=recv_sems.at[p, m],
                device_id=(px, py, pc), device_id_type=pl.DeviceIdType.MESH))
    return local, remote


def _exchange_start(copies):
    local, remote = copies
    for cp in local + remote:
        cp.start()


def _exchange_wait(copies):
    local, remote = copies
    for cp in remote:
        cp.wait_recv()
    for cp in remote:
        cp.wait_send()
    for cp in local:
        cp.wait()


def _exchange_scratch(count):
    return [pltpu.SemaphoreType.DMA((count, N_DEV)), pltpu.SemaphoreType.DMA((count, N_DEV)),
            pltpu.SemaphoreType.DMA((count,))]


def _exchange_out(src, gather):
    return jax.ShapeDtypeStruct(((N_DEV,) + src.shape) if gather else src.shape, src.dtype)


def peer_exchange(name, srcs, gather):
    count = len(srcs)

    def body(*refs):
        copies = _exchange_copies(refs[:count], refs[count:2 * count], *refs[2 * count:], gather)
        _exchange_start(copies)
        _exchange_wait(copies)

    hbm = pl.BlockSpec(memory_space=pl.ANY)
    return pl.pallas_call(
        body, name=name, in_specs=[hbm] * count, out_specs=[hbm] * count,
        out_shape=[_exchange_out(s_, gather) for s_ in srcs], scratch_shapes=_exchange_scratch(count),
    )(*srcs)


def _grid_first_last(grid):
    ids = [pl.program_id(a) for a in range(len(grid))]
    first = functools.reduce(jnp.logical_and, [i == 0 for i in ids])
    last = functools.reduce(jnp.logical_and, [i == g - 1 for i, g in zip(ids, grid)])
    return first, last


def _ride_start(src_refs, out_refs, sem_refs, gather, first):
    copies = _exchange_copies(src_refs, out_refs, *sem_refs, gather)

    @pl.when(first)
    def _():
        _exchange_start(copies)

    return copies


def _ride_wait(copies, last):
    @pl.when(last)
    def _():
        _exchange_wait(copies)


def _ride_args(ride):
    if ride is None:
        return [], [], [], [], []
    srcs, gather = ride
    hbm = pl.BlockSpec(memory_space=pl.ANY)
    return (list(srcs), [hbm] * len(srcs), [_exchange_out(s_, gather) for s_ in srcs], [hbm] * len(srcs),
            _exchange_scratch(len(srcs)))


ADAM_BLOCK_BYTES = 4 * 1024 * 1024


def adamw_sum(name, parts, w, m, v):
    r, cols = w.shape
    tr = _largest_tile(r, min(ADAM_ROWS, max(SUBLANES, ADAM_BLOCK_BYTES // (N_DEV * cols * 4))), SUBLANES)

    def body(p_ref, w_ref, m_ref, v_ref, g_ref, d_ref, nm_ref, nv_ref):
        g = p_ref[0]
        for j in range(1, N_DEV):
            g = g + p_ref[j]
        mm_ = ADAM_B1 * m_ref[...] + (1.0 - ADAM_B1) * g
        vv = ADAM_B2 * v_ref[...] + (1.0 - ADAM_B2) * (g * g)
        m_hat = mm_ / (1.0 - ADAM_B1 ** ADAM_STEP)
        v_hat = vv / (1.0 - ADAM_B2 ** ADAM_STEP)
        g_ref[...] = g
        d_ref[...] = -ADAM_LR * (m_hat / (jnp.sqrt(v_hat) + ADAM_EPS) + ADAM_WD * w_ref[...])
        nm_ref[...] = mm_
        nv_ref[...] = vv

    row = pl.BlockSpec((tr, cols), lambda i: (i, 0))
    shp = jax.ShapeDtypeStruct((r, cols), F32)
    return pl.pallas_call(
        body, name=name, grid=(r // tr,),
        in_specs=[pl.BlockSpec((N_DEV, tr, cols), lambda i: (0, i, 0)), row, row, row],
        out_specs=[row] * 4, out_shape=[shp] * 4,
        compiler_params=_params(("parallel",)),
    )(parts, w, m, v)


WEIGHTS = ['l0_w_in', 'rwkv_mix', 'rwkv_w0', 'rwkv_w2', 'rwkv_a0', 'rwkv_a2', 'rwkv_g2', 'rwkv_k_k', 'rwkv_k_a',
           'rwkv_r_k', 'rwkv_ln_g', 'rwkv_ln_b', 'ssm_conv_w', 'ssm_conv_b', 'ssm_dt_bias', 'ssm_a_log', 'ssm_d',
           'ssm_norm_g', 'l0_w_out', 'l0_ln1_g', 'l0_ln1_b', 'ffn0_w_up', 'ffn0_conv_w', 'ffn0_conv_b',
           'ffn0_w_down', 'l0_ln2_g', 'l0_ln2_b', 'l1_w_in', 'mla_q_norm_g', 'mla_w_uq', 'mla_kv_norm_g',
           'mla_w_ukv', 'l1_w_out', 'l1_ln1_g', 'l1_ln1_b', 'ffn1_w_up', 'ffn1_conv_w', 'ffn1_conv_b',
           'ffn1_w_down', 'l1_ln2_g', 'l1_ln2_b']
SHARD_AXIS = {'l0_w_in': 1, 'rwkv_w2': 1, 'rwkv_a2': 1, 'rwkv_g2': 1, 'ssm_conv_w': 1, 'l0_w_out': 0,
              'ffn0_w_up': 1, 'ffn0_conv_w': 1, 'ffn0_w_down': 0, 'l1_w_in': 1, 'mla_w_uq': 1, 'mla_w_ukv': 1,
              'l1_w_out': 0, 'ffn1_w_up': 1, 'ffn1_conv_w': 1, 'ffn1_w_down': 0}
MATMUL_W = ['l0_w_in', 'rwkv_w2', 'rwkv_a2', 'rwkv_g2', 'l0_w_out', 'ffn0_w_up', 'ffn0_w_down', 'l1_w_in',
            'mla_w_uq', 'mla_w_ukv', 'l1_w_out', 'ffn1_w_up', 'ffn1_w_down']
CONV_W = ['ssm_conv_w', 'ffn0_conv_w', 'ffn1_conv_w']
TOK_TILE = 256
ADAM_ROWS = 512


def _ceil_to(size, unit):
    return -(-size // unit) * unit


def _flat_rows(pieces, seg_rows, total_rows):
    unit = seg_rows * LANES
    out, total = [], 0
    for p in pieces:
        p = jnp.concatenate([q.reshape(-1) for q in p]) if isinstance(p, list) else p.reshape(-1)
        pad = _ceil_to(p.size, unit) - p.size
        out.append(jnp.pad(p, (0, pad)) if pad else p)
        total += p.size + pad
    tail = _ceil_to(total, total_rows * LANES) - total
    if tail:
        out.append(jnp.zeros((tail,), out[0].dtype))
    return jnp.concatenate(out).reshape(-1, LANES)


def _unflatten(flat2d, shapes, seg_rows):
    flat = flat2d.reshape(-1)
    out, off = [], 0
    for shp in shapes:
        if isinstance(shp, list):
            seg, pos = [], off
            for s_ in shp:
                seg.append(flat[pos:pos + math.prod(s_)].reshape(s_))
                pos += math.prod(s_)
            out.append(seg)
            size = pos - off
        else:
            size = math.prod(shp)
            out.append(flat[off:off + size].reshape(shp))
        off += _ceil_to(size, seg_rows * LANES)
    return out


def _by_size(names):
    return [nm for nm in names if nm in SHARD_AXIS], [nm for nm in names if nm not in SHARD_AXIS]


def _to_heads(t2, bsz, h):
    n, w = t2.shape
    return t2.reshape(bsz, n // bsz, h, w // h).transpose(0, 2, 1, 3)


def _from_heads(t4):
    b, h, t, d = t4.shape
    return t4.transpose(0, 2, 1, 3).reshape(b * t, h * d)


def _row(v):
    return v.reshape(1, -1)


def _pad_lanes(v):
    return jnp.pad(v.reshape(1, -1), ((0, 0), (0, LANES - v.size)))


def _local_step(a, w, comm=None):
    x = a['x']
    bsz, t, d = x.shape
    n = bsz * t
    tm = TOK_TILE
    pos = a['positions'].reshape(bsz, t, 1)
    inv_freq = 1.0 / (ROPE_THETA ** (jnp.arange(0, MLA_ROPE, 2, dtype=F32) / MLA_ROPE))
    inv_freq = jnp.concatenate([inv_freq, inv_freq]).reshape(1, MLA_ROPE)
    target = a['loss_target'].reshape(n, d)

    wi0 = w['l0_w_in']
    win0 = jnp.concatenate([wi0[:, 0:1536], wi0[:, 1792:3328], wi0[:, 1536:1792], wi0[:, 3328:3336],
                            jnp.zeros((d, L0_PAD - 3336), wi0.dtype)], axis=1)
    w2p = jnp.concatenate([w['rwkv_w2'], jnp.zeros_like(w['rwkv_w2'])], axis=0)
    a2p = jnp.concatenate([jnp.zeros_like(w['rwkv_a2']), w['rwkv_a2']], axis=0)
    mix = a['rwkv_mix']
    taps = jnp.stack([mix, 1.0 - mix])
    zero_b = jnp.zeros((1, mix.size), F32)
    rw_map = lambda j: j + jnp.where(j >= 12, 12, 0)
    ssm_map = lambda j: j + 16
    gate_map = lambda j: j
    up_map = lambda j: j + D_FF // LANES
    dt_col = 3328 // LANES
    dtb, alog, dsk = _pad_lanes(a['ssm_dt_bias']), _pad_lanes(a['ssm_a_log']), _pad_lanes(a['ssm_d'])
    pre_p = [_row(a['rwkv_w0']), w2p, _row(a['rwkv_a0']), a2p, w['rwkv_g2'], _row(a['rwkv_k_k']), _row(a['rwkv_k_a'])]
    post_p = [_row(a['rwkv_ln_g']), _row(a['rwkv_ln_b']), _row(a['rwkv_r_k'])]
    sp_p = [dsk, _row(a['ssm_norm_g'])]

    def ln(name, h, y, layer, which):
        ps = [_row(a[f'l{layer}_ln{which}_g']), _row(a[f'l{layer}_ln{which}_b'])]
        return tok_fwd(name, f_ln, [(h, d, 0), (y, d, 0)], ps, [d], tm, n, t)[0]

    def ffn_fwd(layer, h):
        up = mm(h, w[f'ffn{layer}_w_up'], 'nn', f'ffn{layer}_up')
        act = dwconv_fwd(f'ffn{layer}_conv', up.reshape(bsz, t, 2 * D_FF), gate_map, w[f'ffn{layer}_conv_w'],
                         _row(a[f'ffn{layer}_conv_b']), 3, True, upmap=up_map, out_dtype=BF16)
        act = act.reshape(n, D_FF)
        return up, act, mm(act, w[f'ffn{layer}_w_down'], 'nn', f'ffn{layer}_down')

    x2 = x.reshape(n, d)
    proj0 = mm(x2, win0, 'nn', 'l0_in')
    p0 = proj0.reshape(bsz, t, L0_PAD)
    xs_r = dwconv_fwd('rwkv_shift', p0, rw_map, taps, zero_b, 2, False).reshape(n, 1792)
    pre_x = [(xs_r, 512, 0), (xs_r, 512, 1), (xs_r, 512, 2), (xs_r, LANES, 12), (xs_r, LANES, 13)]
    heads64 = (RWKV_HEADS, HEAD_DIM)
    r_, v_, lw, kmod, al, be, gt = tok_fwd('rwkv_pre', f_rwkv_pre, pre_x, pre_p, [heads64] * 6 + [RWKV_DIM],
                                           tm, n, t)
    scan_in = [r_, lw, kmod, v_, al, be]
    if comm is None:
        y_h, rstates = rwkv_scan_fwd(*scan_in)
    else:
        y_h, rstates, *gots = rwkv_scan_fwd(*scan_in, ride=comm.weights_ride(MID_GATHER))
        w = {**w, **comm.weights(MID_GATHER, gots)}
    wi1 = w['l1_w_in']
    win1 = jnp.concatenate([wi1, jnp.zeros((d, L1_PAD - 1952), wi1.dtype)], axis=1)
    wq3 = w['mla_w_uq'].reshape(-1, 8, MLA_NOPE + MLA_ROPE)
    wkv3 = w['mla_w_ukv'].reshape(-1, 8, 2 * MLA_NOPE)
    mla_p = [_row(a['mla_q_norm_g']), wq3[:, :, :MLA_NOPE].reshape(-1, 512), wq3[:, :, MLA_NOPE:].reshape(-1, 256),
             _row(a['mla_kv_norm_g']), wkv3[:, :, :MLA_NOPE].reshape(-1, 512), wkv3[:, :, MLA_NOPE:].reshape(-1, 512)]
    post_x = [y_h, r_, kmod, v_, (gt, 512, 0)]
    y_a = tok_fwd('rwkv_post', f_rwkv_post, post_x, post_p, [RWKV_DIM], tm, n, t)[0]
    xbc = dwconv_fwd('ssm_conv', p0, ssm_map, w['ssm_conv_w'], _row(a['ssm_conv_b']), 4, True)
    ys, sstates = ssd_fwd(xbc, p0, dt_col, dtb, alog)
    xbc2 = xbc.reshape(n, 2 * SSM_DIM)
    sp_x = [(ys.reshape(n, SSM_DIM), 512, 0), (xbc2, 512, 0), (proj0, 512, 3)]
    y_b = tok_fwd('ssd_post', f_ssd_post, sp_x, sp_p, [SSM_DIM], tm, n, t)[0]
    wo0 = w['l0_w_out']
    mixed0 = mm(y_b, wo0, 'nn', 'l0_out_b', b_rows=(512, 512), add=mm(y_a, wo0, 'nn', 'l0_out_a', b_rows=(0, 512)))
    h1 = ln('l0_ln1', x2, mixed0, 0, 1)
    up0, act0, f0 = ffn_fwd(0, h1)
    h2 = ln('l0_ln2', h1, f0, 0, 2)

    proj1 = mm(h2, win1, 'nn', 'l1_in')
    q_sb, k_sb, v_sb = tok_fwd('sb_split', f_same, [(proj1, 512, 0), (proj1, 512, 1), (proj1, 512, 2)], [],
                               [heads64] * 3, tm, n, t)
    if comm is None:
        o_c, sb_kept = sb_fwd(q_sb, k_sb, v_sb)
    else:
        o_c, sb_kept, *gots = sb_fwd(q_sb, k_sb, v_sb, ride=comm.weights_ride(FFN1_GATHER))
        w = {**w, **comm.weights(FFN1_GATHER, gots)}
    mla_x = [(proj1, 256, 6), (proj1, LANES, 14)]
    qn, qp_raw, kn, vv = tok_fwd('mla_pre', f_mla_pre, mla_x, mla_p, [heads64, (8, MLA_ROPE), heads64, heads64],
                                 tm, n, t)
    kp_raw = proj1[:, 1920:1920 + MLA_ROPE].reshape(bsz, 1, t, MLA_ROPE)
    qp = rope('rope_q', qp_raw, pos, inv_freq, 1.0)
    kp = rope('rope_k', kp_raw, pos, inv_freq, 1.0)
    o_d, lse_d = mla_fwd(qn, qp, kn, kp, vv)
    y_cd = tok_fwd('attn_merge', f_concat, [o_c, o_d], [], [2 * RWKV_DIM], tm, n, t, out_dtype=BF16)[0]
    wo1 = w['l1_w_out']
    mixed1 = mm(y_cd, wo1, 'nn', 'l1_out')
    h3 = ln('l1_ln1', h2, mixed1, 1, 1)
    up1, act1, f1 = ffn_fwd(1, h3)
    h4 = ln('l1_ln2', h3, f1, 1, 2)
    dh4, loss = loss_head(h4, target, tm)

    g = {}

    def ln_bwd(name, h, y, layer, which, dout):
        ps = [_row(a[f'l{layer}_ln{which}_g']), _row(a[f'l{layer}_ln{which}_b'])]
        (dh, dy), (dg, db) = tok_bwd(name, f_ln, [(h, d, 0), (y, d, 0)], ps, [[dout]], tm, n, t)
        g[f'l{layer}_ln{which}_g'], g[f'l{layer}_ln{which}_b'] = dg.reshape(-1), db.reshape(-1)
        return dh, dy

    def ffn_bwd(layer, h, up, act, df, dh_res):
        wup, wdown = w[f'ffn{layer}_w_up'], w[f'ffn{layer}_w_down']
        g[f'ffn{layer}_w_down'] = mm(act, df, 'tn', f'ffn{layer}_dwdown')
        dact = mm(df, wdown, 'nt', f'ffn{layer}_dact').reshape(bsz, t, D_FF)
        dgate, dcw, dcb, dup = dwconv_bwd(f'ffn{layer}_conv_bwd', up.reshape(bsz, t, 2 * D_FF), gate_map,
                                          w[f'ffn{layer}_conv_w'], _row(a[f'ffn{layer}_conv_b']), 3, True, dact,
                                          upmap=up_map, grad_dtype=BF16)
        dgate, dup = dgate.reshape(n, D_FF), dup.reshape(n, D_FF)
        g[f'ffn{layer}_conv_w'], g[f'ffn{layer}_conv_b'] = dcw, dcb.reshape(-1)
        g[f'ffn{layer}_w_up'] = (mm(h, dgate, 'tn', f'ffn{layer}_dwgate'), mm(h, dup, 'tn', f'ffn{layer}_dwup'))
        dh = mm(dgate, wup, 'nt', f'ffn{layer}_dh_gate', add=dh_res, b_cols=(0, D_FF))
        return mm(dup, wup, 'nt', f'ffn{layer}_dh_up', add=dh, b_cols=(D_FF, D_FF))

    dh3_res, df1 = ln_bwd('l1_ln2_bwd', h3, f1, 1, 2, dh4)
    dh3 = ffn_bwd(1, h3, up1, act1, df1, dh3_res)
    dh2_res, dmixed1 = ln_bwd('l1_ln1_bwd', h2, mixed1, 1, 1, dh3)
    g['l1_w_out'] = mm(y_cd, dmixed1, 'tn', 'l1_dwout')
    dy_cd = mm(dmixed1, wo1, 'nt', 'l1_dy')
    dy_c, dy_d = tok_fwd('attn_split', f_same, [(dy_cd, 512, 0), (dy_cd, 512, 1)], [], [heads64] * 2, tm, n, t)
    parts = {}
    if comm is None:
        dq_sb, dk_sb, dv_sb = sb_bwd(q_sb, k_sb, v_sb, sb_kept, dy_c)
    else:
        dq_sb, dk_sb, dv_sb, *parts['a'] = sb_bwd(q_sb, k_sb, v_sb, sb_kept, dy_c, ride=comm.grad_ride('a', g))
    dqn, dqp, dkn, dkp, dvv = mla_bwd(qn, qp, kn, kp, vv, o_d, lse_d, dy_d)
    dqp_raw = rope('rope_q_bwd', dqp, pos, inv_freq, -1.0)
    dkp_raw = rope('rope_k_bwd', dkp, pos, inv_freq, -1.0).reshape(n, MLA_ROPE)
    (dcq, dckv), (dqg, dwq_n, dwq_p, dkvg, dwk, dwv) = tok_bwd('mla_pre_bwd', f_mla_pre, mla_x, mla_p,
                                                               [[dqn], [dqp_raw], [dkn], [dvv]], tm, n, t)
    g['mla_q_norm_g'], g['mla_kv_norm_g'] = dqg.reshape(-1), dkvg.reshape(-1)
    g['mla_w_uq'] = jnp.concatenate([dwq_n.reshape(-1, 8, MLA_NOPE), dwq_p.reshape(-1, 8, MLA_ROPE)],
                                    axis=2).reshape(-1, 8 * (MLA_NOPE + MLA_ROPE))
    g['mla_w_ukv'] = jnp.concatenate([dwk.reshape(-1, 8, MLA_NOPE), dwv.reshape(-1, 8, MLA_NOPE)],
                                     axis=2).reshape(-1, 16 * MLA_NOPE)
    dkp_pad = jnp.pad(dkp_raw, ((0, 0), (0, LANES - MLA_ROPE)))
    dproj1 = tok_fwd('l1_dproj', f_concat, [dq_sb, dk_sb, dv_sb, (dcq, 256, 0), (dckv, LANES, 0),
                                            (dkp_pad, LANES, 0)], [], [L1_PAD], tm, n, t, out_dtype=BF16)[0]
    g['l1_w_in'] = mm(h2, dproj1, 'tn', 'l1_dwin')[:, :1952]
    dh2 = mm(dproj1, win1, 'nt', 'l1_dh', add=dh2_res)

    dh1_res, df0 = ln_bwd('l0_ln2_bwd', h1, f0, 0, 2, dh2)
    dh1 = ffn_bwd(0, h1, up0, act0, df0, dh1_res)
    dx_res, dmixed0 = ln_bwd('l0_ln1_bwd', x2, mixed0, 0, 1, dh1)
    g['l0_w_out'] = (mm(y_a, dmixed0, 'tn', 'l0_dwout_a'), mm(y_b, dmixed0, 'tn', 'l0_dwout_b'))
    dy_a = mm(dmixed0, wo0, 'nt', 'l0_dy_a', b_rows=(0, 512))
    dy_b = mm(dmixed0, wo0, 'nt', 'l0_dy_b', b_rows=(512, 512))
    (dy_r, dr1, dkm1, dv1, dgt), (dlng, dlnb, drk) = tok_bwd('rwkv_post_bwd', f_rwkv_post, post_x, post_p, [[dy_a]],
                                                            tm, n, t, dx_layouts=[heads64] * 4 + [None])
    g['rwkv_ln_g'], g['rwkv_ln_b'] = dlng.reshape(-1), dlnb.reshape(-1)
    g['rwkv_r_k'] = drk.reshape(RWKV_HEADS, HEAD_DIM)
    (dys, dxs_skip, dz), (ddsk, dng) = tok_bwd('ssd_post_bwd', f_ssd_post, sp_x, sp_p, [[dy_b]], tm, n, t)
    g['ssm_d'], g['ssm_norm_g'] = ddsk[0, :SSM_HEADS], dng.reshape(-1)
    dxbc_act, ddtr, ddtb, dalog = ssd_bwd(xbc, p0, dt_col, dtb, alog, sstates, dys.reshape(bsz, t, SSM_DIM),
                                          dxs_skip.reshape(bsz, t, SSM_DIM))
    g['ssm_dt_bias'], g['ssm_a_log'] = ddtb[0, :SSM_HEADS], dalog[0, :SSM_HEADS]
    dxbc, dscw, dscb = dwconv_bwd('ssm_conv_bwd', p0, ssm_map, w['ssm_conv_w'], _row(a['ssm_conv_b']), 4, True,
                                  dxbc_act)
    g['ssm_conv_w'], g['ssm_conv_b'] = dscw, dscb.reshape(-1)
    if comm is None:
        dscan = rwkv_scan_bwd(*scan_in, rstates, dy_r)
    else:
        outs = rwkv_scan_bwd(*scan_in, rstates, dy_r, ride=comm.grad_ride('b', g))
        dscan, parts['b'] = outs[:6], outs[6:]
    dr2, dlw, dk2, dv2, dal, dbe = dscan
    pre_ct = [[dr1, dr2], [dv1, dv2], [dlw], [dkm1, dk2], [dal], [dbe], [dgt]]
    dpre_x, dpre_p = tok_bwd('rwkv_pre_bwd', f_rwkv_pre, pre_x, pre_p, pre_ct, tm, n, t)
    g['rwkv_w0'], g['rwkv_a0'] = dpre_p[0].reshape(-1), dpre_p[2].reshape(-1)
    g['rwkv_w2'], g['rwkv_a2'], g['rwkv_g2'] = dpre_p[1][:64], dpre_p[3][64:], dpre_p[4]
    g['rwkv_k_k'], g['rwkv_k_a'] = dpre_p[5].reshape(-1), dpre_p[6].reshape(-1)
    dxs_r = jnp.concatenate(dpre_x, axis=1).reshape(bsz, t, 1792)
    d_rw, dtaps, _ = dwconv_bwd('rwkv_shift_bwd', p0, rw_map, taps, zero_b, 2, False, dxs_r)
    d_rw = d_rw.reshape(n, 1792)
    g['rwkv_mix'] = dtaps[0] - dtaps[1]
    dproj0 = tok_fwd('l0_dproj', f_concat, [(d_rw, 1536, 0), (dz, 512, 0), (dxbc.reshape(n, 2 * SSM_DIM), 1024, 0),
                                            (d_rw, 256, 6), (ddtr.reshape(n, LANES), LANES, 0)],
                     [], [L0_PAD], tm, n, t, out_dtype=BF16)[0]
    dwin0 = mm(x2, dproj0, 'tn', 'l0_dwin')
    g['l0_w_in'] = jnp.concatenate([dwin0[:, 0:1536], dwin0[:, 3072:3328], dwin0[:, 1536:3072],
                                    dwin0[:, 3328:3336]], axis=1)
    dx = mm(dproj0, win0, 'nt', 'l0_dx', add=dx_res)
    if comm is not None:
        parts['c'] = peer_exchange('grad_exchange_c', comm.grad_ride('c', g)[0], False)
    return loss, dx.reshape(bsz, t, d), g, parts


GRAD_GROUPS = {
    'a': ['ffn1_w_up', 'ffn1_conv_w', 'ffn1_conv_b', 'ffn1_w_down', 'l1_ln2_g', 'l1_ln2_b'],
    'c': ['l0_w_in', 'rwkv_mix', 'rwkv_w0', 'rwkv_w2', 'rwkv_a0', 'rwkv_a2', 'rwkv_g2', 'rwkv_k_k', 'rwkv_k_a'],
}
GRAD_GROUPS['b'] = [nm for nm in WEIGHTS if nm not in GRAD_GROUPS['a'] + GRAD_GROUPS['c']]
FIRST_GATHER = ['l0_w_in', 'rwkv_w2', 'rwkv_a2', 'rwkv_g2'] + CONV_W
MID_GATHER = ['l0_w_out', 'ffn0_w_up', 'ffn0_w_down', 'l1_w_in', 'mla_w_uq', 'mla_w_ukv', 'l1_w_out']
FFN1_GATHER = ['ffn1_w_up', 'ffn1_w_down']
BF16_ROWS = 16


def _name_kinds(names):
    return ([nm for nm in names if nm in MATMUL_W], [nm for nm in names if nm in SHARD_AXIS and nm not in MATMUL_W],
            [nm for nm in names if nm not in SHARD_AXIS])


class _Comm:
    def __init__(self, a):
        self.a = a

    def weights_ride(self, names):
        big, conv, _ = _name_kinds(names)
        srcs = [self.a[nm].astype(BF16) for nm in big]
        if conv:
            srcs.append(_flat_rows([lax.bitcast_convert_type(self.a[nm], BF16) for nm in conv], BF16_ROWS, BF16_ROWS))
        return srcs, True

    def weights(self, names, gots):
        big, conv, _ = _name_kinds(names)
        out = {}
        for nm, got in zip(big, gots):
            out[nm] = (got.reshape(-1, got.shape[2]) if SHARD_AXIS[nm] == 0
                       else jnp.concatenate([got[k] for k in range(N_DEV)], axis=1))
        if conv:
            shapes = [self.a[nm].shape + (2,) for nm in conv]
            blocks = [_unflatten(gots[-1][k], shapes, BF16_ROWS) for k in range(N_DEV)]
            for i, nm in enumerate(conv):
                out[nm] = jnp.concatenate([lax.bitcast_convert_type(blocks[k][i], F32) for k in range(N_DEV)], axis=1)
        return out

    def first_weights(self):
        return self.weights(FIRST_GATHER, peer_exchange('gather_first_weights', self.weights_ride(FIRST_GATHER)[0], True))

    def grad_ride(self, group, g):
        def shard_of(nm, k):
            gv = g[nm]
            per = N_DEV
            if isinstance(gv, tuple):
                gv, k, per = gv[k // 4], k % 4, 4
            width = gv.shape[SHARD_AXIS[nm]] // per
            return lax.slice_in_dim(gv, k * width, (k + 1) * width, axis=SHARD_AXIS[nm])

        big, conv, small = _name_kinds(GRAD_GROUPS[group])
        srcs = [jnp.stack([shard_of(nm, k) for k in range(N_DEV)]) for nm in big]
        srcs.append(jnp.stack([_flat_rows([shard_of(nm, k) for nm in conv] + [[g[nm] for nm in small]],
                                          SUBLANES, ADAM_ROWS) for k in range(N_DEV)]))
        return srcs, False


def _step(a):
    comm = _Comm(a)
    loss, dx, _, parts = _local_step(a, comm.first_weights(), comm)
    loss = lax.psum(loss, ('x', 'y', 'c'))
    res = {}
    for group, names in GRAD_GROUPS.items():
        big, conv, small = _name_kinds(names)
        for nm, got in zip(big, parts[group]):
            res[nm] = adamw_sum(f'adamw_{nm}', got, a[nm], a['m_' + nm], a['v_' + nm])
        flat = lambda prefix: _flat_rows([a[prefix + nm] for nm in conv] + [[a[prefix + nm] for nm in small]],
                                         SUBLANES, ADAM_ROWS)
        outs = adamw_sum(f'adamw_{group}', parts[group][-1], flat(''), flat('m_'), flat('v_'))
        shapes = [a[nm].shape for nm in conv] + [[a[nm].shape for nm in small]]
        per_out = [_unflatten(o, shapes, SUBLANES) for o in outs]
        for i, nm in enumerate(conv):
            res[nm] = [per_out[j][i] for j in range(4)]
        for i, nm in enumerate(small):
            res[nm] = [per_out[j][-1][i] for j in range(4)]
    return (loss, dx, *[res[nm][j] for j in range(4) for nm in WEIGHTS])


def kernel(x, positions, l0_w_in, rwkv_mix, rwkv_w0, rwkv_w2, rwkv_a0, rwkv_a2, rwkv_g2, rwkv_k_k, rwkv_k_a, rwkv_r_k, rwkv_ln_g, rwkv_ln_b, ssm_conv_w, ssm_conv_b, ssm_dt_bias, ssm_a_log, ssm_d, ssm_norm_g, l0_w_out, l0_ln1_g, l0_ln1_b, ffn0_w_up, ffn0_conv_w, ffn0_conv_b, ffn0_w_down, l0_ln2_g, l0_ln2_b, l1_w_in, mla_q_norm_g, mla_w_uq, mla_kv_norm_g, mla_w_ukv, l1_w_out, l1_ln1_g, l1_ln1_b, ffn1_w_up, ffn1_conv_w, ffn1_conv_b, ffn1_w_down, l1_ln2_g, l1_ln2_b, loss_target, m_l0_w_in, m_rwkv_mix, m_rwkv_w0, m_rwkv_w2, m_rwkv_a0, m_rwkv_a2, m_rwkv_g2, m_rwkv_k_k, m_rwkv_k_a, m_rwkv_r_k, m_rwkv_ln_g, m_rwkv_ln_b, m_ssm_conv_w, m_ssm_conv_b, m_ssm_dt_bias, m_ssm_a_log, m_ssm_d, m_ssm_norm_g, m_l0_w_out, m_l0_ln1_g, m_l0_ln1_b, m_ffn0_w_up, m_ffn0_conv_w, m_ffn0_conv_b, m_ffn0_w_down, m_l0_ln2_g, m_l0_ln2_b, m_l1_w_in, m_mla_q_norm_g, m_mla_w_uq, m_mla_kv_norm_g, m_mla_w_ukv, m_l1_w_out, m_l1_ln1_g, m_l1_ln1_b, m_ffn1_w_up, m_ffn1_conv_w, m_ffn1_conv_b, m_ffn1_w_down, m_l1_ln2_g, m_l1_ln2_b, v_l0_w_in, v_rwkv_mix, v_rwkv_w0, v_rwkv_w2, v_rwkv_a0, v_rwkv_a2, v_rwkv_g2, v_rwkv_k_k, v_rwkv_k_a, v_rwkv_r_k, v_rwkv_ln_g, v_rwkv_ln_b, v_ssm_conv_w, v_ssm_conv_b, v_ssm_dt_bias, v_ssm_a_log, v_ssm_d, v_ssm_norm_g, v_l0_w_out, v_l0_ln1_g, v_l0_ln1_b, v_ffn0_w_up, v_ffn0_conv_w, v_ffn0_conv_b, v_ffn0_w_down, v_l0_ln2_g, v_l0_ln2_b, v_l1_w_in, v_mla_q_norm_g, v_mla_w_uq, v_mla_kv_norm_g, v_mla_w_ukv, v_l1_w_out, v_l1_ln1_g, v_l1_ln1_b, v_ffn1_w_up, v_ffn1_conv_w, v_ffn1_conv_b, v_ffn1_w_down, v_l1_ln2_g, v_l1_ln2_b):
    return _step(dict(locals()))
```

```python
import functools
import math

import jax
import jax.numpy as jnp
from jax import lax
from jax.experimental import pallas as pl
from jax.experimental.pallas import tpu as pltpu

F32 = jnp.float32
BF16 = jnp.bfloat16
HI = lax.Precision.HIGHEST

V7X_VMEM_BYTES = 64 * 1024 * 1024
VMEM_LIMIT = V7X_VMEM_BYTES - 8 * 1024 * 1024
LANES = 128
SUBLANES = 8
N_DEV = 8

D_MODEL = 1024
HEAD_DIM = 64
RWKV_DIM = 512
RWKV_HEADS = 8
RWKV_GN_EPS = 64e-5
RWKV_CHUNK = 64
SSM_DIM = 512
SSM_HEADS = 8
SSM_CHUNK = 128
SSM_STATE = 128
Q_BLOCK = 128
SB_HEADS_PER_STEP = 4
MLA_HEADS_PER_STEP = 4
ATTN_FWD_HEADS_PER_STEP = 8
MLA_NOPE = 64
MLA_ROPE = 32
ROPE_THETA = 10000.0
D_FF = 2816
DEPTH = 2
ALPHA = (2 * DEPTH) ** 0.25
RKV_COLS = 3 * RWKV_DIM
RWKV_COLS = RKV_COLS + 64 + 64 + 128
L0_COLS = RWKV_COLS + SSM_DIM + 2 * SSM_DIM + SSM_HEADS
P0_Z = RKV_COLS
P0_LORA = P0_Z + 3 * SSM_DIM
P0_DT = P0_LORA + (RWKV_COLS - RKV_COLS)
L0_PAD = P0_DT + LANES
L1_COLS = 3 * RWKV_DIM + 256 + 128 + MLA_ROPE
P1_CQ, P1_CKV, P1_KPE = 3 * RWKV_DIM, 3 * RWKV_DIM + 256, 3 * RWKV_DIM + 256 + 128
L1_PAD = P1_KPE + LANES

ADAM_LR = 0.001
ADAM_B1 = 0.9
ADAM_B2 = 0.999
ADAM_EPS = 1e-08
ADAM_WD = 0.01
ADAM_STEP = 10

NEG_BIG = -1e30


def _params(sem=None):
    return pltpu.CompilerParams(dimension_semantics=sem, vmem_limit_bytes=VMEM_LIMIT)


P_F32, P_BF16, P_BF16X3 = 0, 1, 2


def _dg_raw(a, b, ca, cb, fast):
    dims = (((ca,), (cb,)), ((), ()))
    if fast == P_BF16:
        return lax.dot_general(a.astype(BF16), b.astype(BF16), dims, preferred_element_type=F32)
    prec = HI if fast == P_F32 else lax.Precision.HIGH
    return lax.dot_general(a, b, dims, precision=prec, preferred_element_type=F32)


@functools.partial(jax.custom_vjp, nondiff_argnums=(2, 3, 4))
def dg(a, b, ca, cb, fast):
    return _dg_raw(a, b, ca, cb, fast)


def _dg_fwd(a, b, ca, cb, fast):
    return _dg_raw(a, b, ca, cb, fast), (a, b)


def _dg_bwd(ca, cb, fast, res, ct):
    a, b = res
    fa, fb = 1 - ca, 1 - cb
    da = _dg_raw(ct, b, 1, fb, fast) if ca == 1 else _dg_raw(b, ct, fb, 1, fast)
    db = _dg_raw(a, ct, fa, 0, fast) if cb == 0 else _dg_raw(ct, a, 0, fa, fast)
    return da.astype(a.dtype), db.astype(b.dtype)


dg.defvjp(_dg_fwd, _dg_bwd)


def mmb(a, b):
    return dg(a, b, 1, 0, P_BF16)


def mmf(a, b):
    return dg(a, b, 1, 0, P_F32)


def mm3(a, b):
    return dg(a, b, 1, 0, P_BF16X3)


def mm3_nt(a, b):
    return dg(a, b, 1, 1, P_BF16X3)


def mm3_tn(a, b):
    return dg(a, b, 0, 0, P_BF16X3)


def _split3_dot(x, m01, cb, terms=3):
    parts, rest = [], x
    for i in range(terms):
        parts.append(rest.astype(BF16))
        if i + 1 < terms:
            rest = rest - parts[-1].astype(F32)
    rows = x.shape[0]
    out = lax.dot_general(jnp.concatenate(parts, axis=0), m01.astype(BF16), (((1,), (cb,)), ((), ())),
                          preferred_element_type=F32)
    return functools.reduce(lambda a_, b_: a_ + b_, [out[i * rows:(i + 1) * rows] for i in range(terms)])


def _lower_ones(n):
    return jnp.where(_iota((n, n), 0) >= _iota((n, n), 1), 1.0, 0.0)


SUFFIX_TERMS = 2


@jax.custom_vjp
def suffix_sum(x):
    return _split3_dot(x, _lower_ones(x.shape[1]), 0, SUFFIX_TERMS)


def _suffix_sum_fwd(x):
    return suffix_sum(x), None


def _suffix_sum_bwd(_, ct):
    return (_split3_dot(ct, _lower_ones(ct.shape[1]), 1, SUFFIX_TERMS),)


suffix_sum.defvjp(_suffix_sum_fwd, _suffix_sum_bwd)


def _iota(shape, dim):
    return lax.broadcasted_iota(jnp.int32, shape, dim)


def _softplus(x):
    return jnp.maximum(x, 0.0) + jnp.log1p(jnp.exp(-jnp.abs(x)))


def _silu(x):
    return x * jax.nn.sigmoid(x)


def _largest_tile(n, cap, mult):
    best = None
    t = mult
    while t <= min(n, cap):
        if n % t == 0:
            best = t
        t += mult
    return n if best is None else best


MM_VMEM_BUDGET = 40 * 1024 * 1024
V7X_HBM_BYTES_PER_S = 3.2e12
GRID_STEP_S = 0.35e-6


def _mm_tiles(M, N, K, a_bytes, b_bytes, has_add):
    def divs(n):
        return [d for d in range(LANES, n + 1, LANES) if n % d == 0] or [n]

    best = None
    for tm in divs(M):
        for tn in divs(N):
            if tm * tn * 4 > 12 * 1024 * 1024:
                continue
            for tk in divs(K):
                vmem = (2 * (tm * tk * a_bytes + tk * tn * b_bytes) + 2 * tm * tn * 4 * (2 if has_add else 1)
                        + (tm * tk + tk * tn) * 2 + tm * tn * 4)
                if vmem > MM_VMEM_BUDGET:
                    continue
                ni, nj, nk = M // tm, N // tn, K // tk
                a_reads = M * K * a_bytes * (1 if nk == 1 else nj)
                b_reads = K * N * b_bytes * (1 if (nk == 1 and nj == 1) else ni)
                traffic = a_reads + b_reads + M * N * 4 * (2 if has_add else 1)
                cost = traffic / V7X_HBM_BYTES_PER_S + ni * nj * nk * GRID_STEP_S
                if min(tm, tn, tk) < 256 and min(M, N, K) >= 256:
                    cost *= 1.5
                if best is None or cost < best[0]:
                    best = (cost, tm, tn, tk)
    return best[1:]


def mm(a, b, mode, name, add=None, b_rows=None, b_cols=None, ride=None):
    r0, nr = b_rows or (0, b.shape[0])
    c0, nc = b_cols or (0, b.shape[1])
    if mode == "nn":
        (M, K), N = a.shape, nc
        assert nr == K
    elif mode == "nt":
        (M, K), N = a.shape, nr
        assert nc == K
    else:
        (K, M), N = a.shape, b.shape[1]
        assert b_rows is None and b_cols is None
    has_add = add is not None
    tm, tn, tk = _mm_tiles(M, N, K, a.dtype.itemsize, b.dtype.itemsize, has_add)
    nk = K // tk
    keep_a = nk == 1 and N // tn > 1 and a.dtype != BF16
    if mode == "nn":
        assert r0 % tk == 0 and c0 % tn == 0
        a_spec = pl.BlockSpec((tm, tk), lambda i, j, k: (i, k))
        b_spec = pl.BlockSpec((tk, tn), lambda i, j, k: (k + r0 // tk, j + c0 // tn))
        dims = (((1,), (0,)), ((), ()))
    elif mode == "nt":
        assert r0 % tn == 0 and c0 % tk == 0
        a_spec = pl.BlockSpec((tm, tk), lambda i, j, k: (i, k))
        b_spec = pl.BlockSpec((tn, tk), lambda i, j, k: (j + r0 // tn, k + c0 // tk))
        dims = (((1,), (1,)), ((), ()))
    else:
        a_spec = pl.BlockSpec((tk, tm), lambda i, j, k: (k, i))
        b_spec = pl.BlockSpec((tk, tn), lambda i, j, k: (k, j))
        dims = (((0,), (0,)), ((), ()))
    o_spec = pl.BlockSpec((tm, tn), lambda i, j, k: (i, j))
    grid = (M // tm, N // tn, nk)
    r_in, r_specs, r_out, r_ospecs, r_scr = _ride_args(ride)
    n_add, n_ride = int(has_add), len(r_in)

    def body(a_ref, b_ref, *rest):
        o_ref = rest[n_add + n_ride]
        if ride is not None:
            first, last = _grid_first_last(grid)
            copies = _ride_start(rest[n_add:n_add + n_ride], rest[n_add + n_ride + 1:n_add + 2 * n_ride + 1],
                                 rest[-3:], ride[1], first)
        k = pl.program_id(2)
        if keep_a:
            a_bf = rest[n_add + 2 * n_ride + 1]

            @pl.when(pl.program_id(1) == 0)
            def _():
                a_bf[...] = a_ref[...].astype(BF16)

            av = a_bf[...]
        else:
            av = a_ref[...].astype(BF16)
        part = lax.dot_general(av, b_ref[...].astype(BF16), dims, preferred_element_type=F32)

        @pl.when(k == 0)
        def _():
            o_ref[...] = part + rest[0][...] if has_add else part

        @pl.when(k > 0)
        def _():
            o_ref[...] += part

        if ride is not None:
            _ride_wait(copies, last)

    ins = [a, b] + ([add] if has_add else []) + r_in
    specs = [a_spec, b_spec] + ([o_spec] if has_add else []) + r_specs
    outs = pl.pallas_call(
        body, name=name, grid=grid, in_specs=specs, out_specs=[o_spec] + r_ospecs,
        out_shape=[jax.ShapeDtypeStruct((M, N), F32)] + r_out,
        scratch_shapes=([pltpu.VMEM(a_spec.block_shape, BF16)] if keep_a else []) + r_scr,
        compiler_params=_params(("arbitrary" if ride is not None else "parallel", "arbitrary", "arbitrary")),
    )(*ins)
    return outs[0] if ride is None else outs


def _is_heads(x):
    return not isinstance(x, tuple)


def _tok_arr(x):
    return x if _is_heads(x) else x[0]


def _tok_width(x):
    return x.shape[1] * x.shape[3] if _is_heads(x) else x[1]


def _heads_spec(h, dh, tm, tiles_per_seq):
    return pl.BlockSpec((None, h, tm, dh), lambda i: (i // tiles_per_seq, 0, i % tiles_per_seq, 0))


def _x_spec(x, tm, tiles_per_seq):
    if _is_heads(x):
        return _heads_spec(x.shape[1], x.shape[3], tm, tiles_per_seq)
    return pl.BlockSpec((tm, x[1]), functools.partial(lambda i, cb: (i, cb), cb=x[2]))


def _out_spec_shape(layout, n, seq, tm, dtype=F32):
    if isinstance(layout, tuple):
        h, dh = layout
        return _heads_spec(h, dh, tm, seq // tm), jax.ShapeDtypeStruct((n // seq, h, seq, dh), dtype)
    return pl.BlockSpec((tm, layout), lambda i: (i, 0)), jax.ShapeDtypeStruct((n, layout), dtype)


def _tok_load(ref):
    if len(ref.shape) == 3:
        return jnp.concatenate([ref[hh] for hh in range(ref.shape[0])], axis=1)
    return ref[...]


def _tok_store(ref, val):
    if len(ref.shape) == 3:
        dh = ref.shape[2]
        for hh in range(ref.shape[0]):
            ref[hh] = val[:, hh * dh:(hh + 1) * dh]
    else:
        ref[...] = val


def _p_specs(ps):
    return [pl.BlockSpec(p.shape, lambda i: (0, 0)) for p in ps]


def tok_fwd(name, f, xs, ps, out_layouts, tm, n, seq, out_dtype=F32):
    nx, npar = len(xs), len(ps)
    outs = [_out_spec_shape(lay, n, seq, tm, out_dtype) for lay in out_layouts]

    def body(*refs):
        xv = [_tok_load(r) for r in refs[:nx]]
        pv = [r[...].astype(F32) for r in refs[nx:nx + npar]]
        for o, r in zip(f(*xv, *pv), refs[nx + npar:]):
            _tok_store(r, o.astype(out_dtype))

    return pl.pallas_call(
        body, name=name, grid=(n // tm,),
        in_specs=[_x_spec(x, tm, seq // tm) for x in xs] + _p_specs(ps),
        out_specs=[o[0] for o in outs], out_shape=[o[1] for o in outs],
        compiler_params=_params(("parallel",)),
    )(*[_tok_arr(x) for x in xs], *ps)


def tok_bwd(name, f, xs, ps, cts, tm, n, seq, dx_layouts=None):
    nx, npar = len(xs), len(ps)
    ct_flat = [c for group in cts for c in group]
    nct = len(ct_flat)
    dx_layouts = dx_layouts or [None] * nx
    dxs = [_out_spec_shape(lay if lay else _tok_width(x), n, seq, tm) for x, lay in zip(xs, dx_layouts)]

    def body(*refs):
        xv = [_tok_load(r) for r in refs[:nx]]
        pv = [r[...].astype(F32) for r in refs[nx:nx + npar]]
        ct_refs = refs[nx + npar:nx + npar + nct]
        dx_refs = refs[nx + npar + nct:nx + npar + nct + nx]
        dp_refs = refs[nx + npar + nct + nx:]
        cv, pos = [], 0
        for group in cts:
            acc = _tok_load(ct_refs[pos])
            for r in ct_refs[pos + 1:pos + len(group)]:
                acc = acc + _tok_load(r)
            cv.append(acc)
            pos += len(group)
        _, vjp = jax.vjp(f, *xv, *pv)
        grads = vjp(tuple(cv))
        for g, r in zip(grads[:nx], dx_refs):
            _tok_store(r, g)

        @pl.when(pl.program_id(0) == 0)
        def _():
            for r in dp_refs:
                r[...] = jnp.zeros_like(r)

        for g, r in zip(grads[nx:], dp_refs):
            r[...] += g

    ct_specs = [_heads_spec(c.shape[1], c.shape[3], tm, seq // tm) if c.ndim == 4
                else pl.BlockSpec((tm, c.shape[1]), lambda i: (i, 0)) for c in ct_flat]
    outs = pl.pallas_call(
        body, name=name, grid=(n // tm,),
        in_specs=[_x_spec(x, tm, seq // tm) for x in xs] + _p_specs(ps) + ct_specs,
        out_specs=[d[0] for d in dxs] + _p_specs(ps),
        out_shape=[d[1] for d in dxs] + [jax.ShapeDtypeStruct(p.shape, F32) for p in ps],
        compiler_params=_params(("arbitrary",)),
    )(*[_tok_arr(x) for x in xs], *ps, *ct_flat)
    return outs[:nx], outs[nx:]


def f_ln(h, y, g, b):
    pre = ALPHA * h + y
    mu = jnp.mean(pre, axis=-1, keepdims=True)
    xc = pre - mu
    var = jnp.mean(xc * xc, axis=-1, keepdims=True)
    return (xc * lax.rsqrt(var + 1e-5) * g + b,)


def _head_sel(width, nheads_pad, per):
    return jnp.where(_iota((width, nheads_pad), 0) // per == _iota((width, nheads_pad), 1), 1.0, 0.0).astype(F32)


def _head_sel_t(nheads_pad, width, per):
    return jnp.where(_iota((nheads_pad, width), 1) // per == _iota((nheads_pad, width), 0), 1.0, 0.0).astype(F32)


@jax.custom_vjp
def head_sum(x):
    return _split3_dot(x, _head_sel(RWKV_DIM, LANES, HEAD_DIM), 0)


@jax.custom_vjp
def head_spread(y):
    return _split3_dot(y, _head_sel(RWKV_DIM, LANES, HEAD_DIM), 1)


head_sum.defvjp(lambda x: (head_sum(x), None), lambda _, ct: (head_spread(ct),))
head_spread.defvjp(lambda y: (head_spread(y), None), lambda _, ct: (head_sum(ct),))


def f_rwkv_pre(r, k, v, lora, glo, w0, w2p, a0, a2p, g2, k_k, k_a):
    lane = _iota(lora.shape, 1)
    tw = jnp.where(lane < 64, jnp.tanh(lora), 0.0)
    ta = jnp.where(lane >= 64, lora, 0.0)
    log_w = -_softplus(-(w0 + mmb(tw, w2p))) - 0.5
    lw = -jnp.exp(log_w)
    a = jax.nn.sigmoid(a0 + mmb(ta, a2p))
    g = mmb(jax.nn.sigmoid(glo), g2)
    kk = k * k_k
    nrm = jnp.sqrt(jnp.maximum(head_sum(kk * kk), 1e-24))
    kkn = kk * head_spread(1.0 / nrm)
    kmod = k * (1.0 + (a - 1.0) * k_a)
    return r, v, lw, kmod, -kkn, kkn * a, g


def f_rwkv_post(y, r, kmod, v, g, ln_g, ln_b, r_k):
    inv = 1.0 / HEAD_DIM
    mu = head_spread(head_sum(y) * inv)
    yc = y - mu
    var = head_sum(yc * yc) * inv
    rstd = head_spread(lax.rsqrt(var + RWKV_GN_EPS))
    yn = yc * rstd * ln_g + ln_b
    bonus = head_spread(head_sum(r * kmod * r_k)) * v
    return ((yn + bonus) * g,)


def f_ssd_post(y, xs, z, d_skip, norm_g):
    sel_t = _head_sel_t(LANES, SSM_DIM, HEAD_DIM)
    d_e = jnp.sum(mmf(jnp.broadcast_to(d_skip, (SUBLANES, LANES)), sel_t), axis=0, keepdims=True) * (1.0 / SUBLANES)
    u = (y + xs * d_e) * _silu(z)
    first = _iota(u.shape, 1) < (SSM_DIM // 2)
    uu = u * u
    inv = 2.0 / SSM_DIM
    ms0 = jnp.sum(jnp.where(first, uu, 0.0), axis=-1, keepdims=True) * inv
    ms1 = jnp.sum(jnp.where(first, 0.0, uu), axis=-1, keepdims=True) * inv
    ms = jnp.where(first, ms0, ms1)
    return (u * lax.rsqrt(ms + 1e-5) * norm_g,)


def f_mla_pre(cq, ckv, qg, wq_nope, wq_rope, kvg, wk_nope, wv):
    def rms(x, g):
        return x * lax.rsqrt(jnp.mean(x * x, axis=-1, keepdims=True) + 1e-6) * g
    q_in, kv_in = rms(cq, qg), rms(ckv, kvg)
    return mmb(q_in, wq_nope), mmb(q_in, wq_rope), mmb(kv_in, wk_nope), mmb(kv_in, wv)


def f_same(*xs):
    return xs


def f_concat(*xs):
    return (jnp.concatenate(xs, axis=1),)


def _shift_down(x, s, row):
    return x if s == 0 else jnp.where(row >= s, pltpu.roll(x, s, 0), 0.0)


def _shift_up(x, s, row, t):
    return x if s == 0 else jnp.where(row < t - s, pltpu.roll(x, t - s, 0), 0.0)


def dwconv_fwd(name, u, colmap, w, b, taps, silu, upmap=None, out_dtype=F32):
    bsz, t, _ = u.shape
    c = w.shape[1]
    tc = LANES
    has_up = upmap is not None

    def body(*refs):
        u_ref, w_ref, b_ref = refs[:3]
        o_ref = refs[-1]
        uv = u_ref[...]
        wv = w_ref[...]
        row = _iota(uv.shape, 0)
        acc = jnp.broadcast_to(b_ref[...], uv.shape)
        for i in range(taps):
            acc = acc + wv[i:i + 1, :] * _shift_down(uv, taps - 1 - i, row)
        if silu:
            acc = _silu(acc)
        if has_up:
            acc = acc * refs[3][...]
        o_ref[...] = acc.astype(out_dtype)

    specs = [pl.BlockSpec((None, t, tc), lambda bb, j: (bb, 0, colmap(j))),
             pl.BlockSpec((taps, tc), lambda bb, j: (0, j)),
             pl.BlockSpec((1, tc), lambda bb, j: (0, j))]
    ins = [u, w, b]
    if has_up:
        specs.append(pl.BlockSpec((None, t, tc), lambda bb, j: (bb, 0, upmap(j))))
        ins.append(u)
    return pl.pallas_call(
        body, name=name, grid=(bsz, c // tc), in_specs=specs,
        out_specs=pl.BlockSpec((None, t, tc), lambda bb, j: (bb, 0, j)),
        out_shape=jax.ShapeDtypeStruct((bsz, t, c), out_dtype),
        compiler_params=_params(("parallel", "parallel")),
    )(*ins)


def dwconv_bwd(name, u, colmap, w, b, taps, silu, dout, upmap=None, grad_dtype=F32):
    bsz, t, _ = u.shape
    c = w.shape[1]
    tc = LANES
    has_up = upmap is not None

    def body(*refs):
        u_ref, w_ref, b_ref, d_ref = refs[:4]
        nin = 5 if has_up else 4
        du_ref, dw_ref, db_ref = refs[nin:nin + 3]
        uv = u_ref[...]
        wv = w_ref[...]
        dv = d_ref[...]
        row = _iota(uv.shape, 0)
        shifted = [_shift_down(uv, taps - 1 - i, row) for i in range(taps)]
        cg = jnp.broadcast_to(b_ref[...], uv.shape)
        for i in range(taps):
            cg = cg + wv[i:i + 1, :] * shifted[i]
        if silu:
            sg = jax.nn.sigmoid(cg)
            act = cg * sg
            dact_dcg = sg * (1.0 + cg * (1.0 - sg))
        else:
            act = cg
            dact_dcg = None
        if has_up:
            refs[nin + 3][...] = (dv * act).astype(grad_dtype)
            dv = dv * refs[4][...]
        dcg = dv * dact_dcg if silu else dv
        du = jnp.zeros_like(uv)
        for i in range(taps):
            du = du + wv[i:i + 1, :] * _shift_up(dcg, taps - 1 - i, row, t)
        du_ref[...] = du.astype(grad_dtype)

        @pl.when(pl.program_id(1) == 0)
        def _():
            dw_ref[...] = jnp.zeros_like(dw_ref)
            db_ref[...] = jnp.zeros_like(db_ref)

        for i in range(taps):
            dw_ref[i:i + 1, :] += jnp.sum(dcg * shifted[i], axis=0, keepdims=True)
        db_ref[...] += jnp.sum(dcg, axis=0, keepdims=True)

    specs = [pl.BlockSpec((None, t, tc), lambda j, bb: (bb, 0, colmap(j))),
             pl.BlockSpec((taps, tc), lambda j, bb: (0, j)),
             pl.BlockSpec((1, tc), lambda j, bb: (0, j)),
             pl.BlockSpec((None, t, tc), lambda j, bb: (bb, 0, j))]
    ins = [u, w, b, dout]
    if has_up:
        specs.append(pl.BlockSpec((None, t, tc), lambda j, bb: (bb, 0, upmap(j))))
        ins.append(u)
    big = pl.BlockSpec((None, t, tc), lambda j, bb: (bb, 0, j))
    out_specs = [big, pl.BlockSpec((taps, tc), lambda j, bb: (0, j)), pl.BlockSpec((1, tc), lambda j, bb: (0, j))]
    out_shape = [jax.ShapeDtypeStruct((bsz, t, c), grad_dtype), jax.ShapeDtypeStruct((taps, c), F32),
                 jax.ShapeDtypeStruct((1, c), F32)]
    if has_up:
        out_specs.append(big)
        out_shape.append(jax.ShapeDtypeStruct((bsz, t, c), grad_dtype))
    return pl.pallas_call(
        body, name=name, grid=(c // tc, bsz), in_specs=specs, out_specs=out_specs, out_shape=out_shape,
        compiler_params=_params(("parallel", "arbitrary")),
    )(*ins)


def _each(f, *lists):
    return [f(*xs) for xs in zip(*lists)]


def rwkv_chunk(s0, r, lw, k, v, al, be):
    c = r[0].shape[0]
    ii, jj = _iota((c, c), 0), _iota((c, c), 1)
    incl, strict = ii >= jj, ii > jj
    ones_incl = jnp.where(incl, 1.0, 0.0)
    eye = jnp.where(ii == jj, 1.0, 0.0)
    cum = _each(lambda x: mmf(ones_incl, x), lw)
    gam_inv = _each(lambda x: jnp.exp(-x), cum)
    at = _each(lambda a_, c_, l_: a_ * jnp.exp(c_ - l_), al, cum, lw)
    rt = _each(lambda r_, c_: r_ * jnp.exp(c_), r, cum)
    bt = _each(lambda b_, g_: b_ * g_, be, gam_inv)
    kt = _each(lambda k_, g_: k_ * g_, k, gam_inv)
    a_b = _each(lambda x, y_: jnp.where(strict, mm3_nt(x, y_), 0.0), at, bt)
    a_k = _each(lambda x, y_: jnp.where(strict, mm3_nt(x, y_), 0.0), at, kt)
    rhs0 = _each(mm3_nt, at, s0)
    rhs = _each(lambda x, a_, v_: x + mm3(a_, v_), rhs0, a_k, v)
    p = _each(lambda x: eye + x, a_b)
    m = a_b
    for _ in range(int(math.log2(c)) - 1):
        m = _each(mm3, m, m)
        p = _each(lambda p_, m_: p_ + mm3(p_, m_), p, m)
    u = _each(mm3, p, rhs)
    r_b = _each(lambda x, y_: jnp.where(incl, mm3_nt(x, y_), 0.0), rt, bt)
    r_k = _each(lambda x, y_: jnp.where(incl, mm3_nt(x, y_), 0.0), rt, kt)
    y0 = _each(mm3_nt, rt, s0)
    y1 = _each(lambda y_, b_, u_: y_ + mm3(b_, u_), y0, r_b, u)
    y = _each(lambda y_, k_, v_: y_ + mm3(k_, v_), y1, r_k, v)
    su = _each(mm3_tn, u, bt)
    sv = _each(mm3_tn, v, kt)
    s1 = _each(lambda s_, a_, b_, l_: (s_ + a_ + b_) * jnp.exp(jnp.sum(l_, axis=0, keepdims=True)), s0, su, sv, lw)
    return y, s1


def rwkv_scan_fwd(r, lw, k, v, al, be, ride=None):
    bsz, h, t, d = r.shape
    c = RWKV_CHUNK
    nc = t // c
    grid = (bsz, nc)
    r_in, r_specs, r_out, r_ospecs, r_scr = _ride_args(ride)

    def body(*refs):
        r_ref, lw_ref, k_ref, v_ref, al_ref, be_ref = refs[:6]
        y_ref, st_ref = refs[6 + len(r_in):8 + len(r_in)]
        s_scr = refs[8 + 2 * len(r_in)]
        if ride is not None:
            first, last = _grid_first_last(grid)
            copies = _ride_start(refs[6:6 + len(r_in)], refs[8 + len(r_in):8 + 2 * len(r_in)], refs[-3:], ride[1], first)

        @pl.when(pl.program_id(1) == 0)
        def _():
            s_scr[...] = jnp.zeros_like(s_scr)

        heads = lambda ref: [ref[hh] for hh in range(h)]
        s0 = heads(s_scr)
        y, s1 = rwkv_chunk(s0, heads(r_ref), heads(lw_ref), heads(k_ref), heads(v_ref), heads(al_ref),
                           heads(be_ref))
        for hh in range(h):
            st_ref[hh] = s0[hh]
            y_ref[hh] = y[hh]
            s_scr[hh] = s1[hh]
        if ride is not None:
            _ride_wait(copies, last)

    seq = pl.BlockSpec((None, h, c, d), lambda b, i: (b, 0, i, 0))
    return pl.pallas_call(
        body, name="rwkv_scan_fwd", grid=grid, in_specs=[seq] * 6 + r_specs,
        out_specs=[seq, pl.BlockSpec((None, h, None, d, d), lambda b, i: (b, 0, i, 0, 0))] + r_ospecs,
        out_shape=[jax.ShapeDtypeStruct((bsz, h, t, d), F32), jax.ShapeDtypeStruct((bsz, h, nc, d, d), F32)] + r_out,
        scratch_shapes=[pltpu.VMEM((h, d, d), F32)] + r_scr,
        compiler_params=_params(("arbitrary", "arbitrary")),
    )(r, lw, k, v, al, be, *r_in)


def rwkv_scan_bwd(r, lw, k, v, al, be, states, dy, ride=None):
    bsz, h, t, d = r.shape
    c = RWKV_CHUNK
    nc = t // c
    grid = (bsz, nc)
    r_in, r_specs, r_out, r_ospecs, r_scr = _ride_args(ride)

    def body(*refs):
        r_ref, lw_ref, k_ref, v_ref, al_ref, be_ref, st_ref, dy_ref = refs[:8]
        nin = 8 + len(r_in)
        dr_ref, dlw_ref, dk_ref, dv_ref, dal_ref, dbe_ref = refs[nin:nin + 6]
        ds_scr = refs[nin + 6 + len(r_in)]
        if ride is not None:
            first, last = _grid_first_last(grid)
            copies = _ride_start(refs[8:nin], refs[nin + 6:nin + 6 + len(r_in)], refs[-3:], ride[1], first)

        @pl.when(pl.program_id(1) == 0)
        def _():
            ds_scr[...] = jnp.zeros_like(ds_scr)

        heads = lambda ref: [ref[hh] for hh in range(h)]
        _, vjp = jax.vjp(rwkv_chunk, heads(st_ref), heads(r_ref), heads(lw_ref), heads(k_ref), heads(v_ref),
                         heads(al_ref), heads(be_ref))
        grads = vjp((heads(dy_ref), heads(ds_scr)))
        for ref, gl in zip((ds_scr, dr_ref, dlw_ref, dk_ref, dv_ref, dal_ref, dbe_ref), grads):
            for hh in range(h):
                ref[hh] = gl[hh]
        if ride is not None:
            _ride_wait(copies, last)

    seq = pl.BlockSpec((None, h, c, d), lambda b, i: (b, 0, nc - 1 - i, 0))
    st = pl.BlockSpec((None, h, None, d, d), lambda b, i: (b, 0, nc - 1 - i, 0, 0))
    return pl.pallas_call(
        body, name="rwkv_scan_bwd", grid=grid, in_specs=[seq] * 6 + [st, seq] + r_specs,
        out_specs=[seq] * 6 + r_ospecs, out_shape=[jax.ShapeDtypeStruct((bsz, h, t, d), F32)] * 6 + r_out,
        scratch_shapes=[pltpu.VMEM((h, d, d), F32)] + r_scr,
        compiler_params=_params(("arbitrary", "arbitrary")),
    )(r, lw, k, v, al, be, states, dy, *r_in)


def ssd_chunk(st, xs, bm, cm, dtr, dt_bias, a_log):
    n = SSM_CHUNK
    ii, jj = _iota((n, n), 0), _iota((n, n), 1)
    incl = ii >= jj
    lane = _iota((n, LANES), 1)
    dt = _softplus(dtr + dt_bias)
    a = dt * (-jnp.exp(a_log))
    acum = mmf(jnp.where(incl, 1.0, 0.0), a)
    last_row = jnp.where(jj == n - 1, 1.0, 0.0)
    cb = [mm3_nt(cm[g], bm[g]) for g in range(2)]
    pairs, heads = range(4), range(SSM_HEADS)
    e_m = [jnp.where(_iota((LANES, LANES), 0) == 2 * m + _iota((LANES, LANES), 1) // HEAD_DIM, 1.0, 0.0)
           for m in pairs]
    dt_m = [mmf(dt, e_m[m]) for m in pairs]
    ac_m = [mmf(acum, e_m[m]) for m in pairs]
    x = [xs[m] * dt_m[m] for m in pairs]
    last_m = [mmf(last_row, ac_m[m]) for m in pairs]
    colb = [mmf(acum, jnp.where(_iota((LANES, n), 0) == h, 1.0, 0.0)) for h in heads]
    decay = [jnp.exp(jnp.where(incl, colb[h] - colb[h].T, NEG_BIG)) for h in heads]
    yh = [mm3(cb[h // 4] * decay[h], x[h // 2]) for h in heads]
    y_off = [mm3(cm[m // 2], st[m]) for m in pairs]
    ys = [jnp.where(lane // HEAD_DIM == 0, yh[2 * m], yh[2 * m + 1]) + jnp.exp(ac_m[m]) * y_off[m] for m in pairs]
    st_in = [mm3_tn(bm[m // 2], x[m] * jnp.exp(last_m[m] - ac_m[m])) for m in pairs]
    st_new = [jnp.exp(last_m[m]) * st[m] + st_in[m] for m in pairs]
    return tuple(ys), tuple(st_new)


def _ssd_load(xbc_ref, dtr_ref):
    xs = tuple(xbc_ref[:, m * LANES:(m + 1) * LANES] for m in range(4))
    bm = tuple(xbc_ref[:, SSM_DIM + g * LANES:SSM_DIM + (g + 1) * LANES] for g in range(2))
    cm = tuple(xbc_ref[:, SSM_DIM + 2 * LANES + g * LANES:SSM_DIM + 2 * LANES + (g + 1) * LANES] for g in range(2))
    return xs, bm, cm, dtr_ref[...]


def ssd_fwd(xbc, proj, dt_col, dt_bias, a_log):
    bsz, t, _ = xbc.shape
    n = SSM_CHUNK
    nc = t // n

    def body(xbc_ref, dtr_ref, dtb_ref, al_ref, y_ref, st_ref, s_scr):
        @pl.when(pl.program_id(1) == 0)
        def _():
            s_scr[...] = jnp.zeros_like(s_scr)

        st = tuple(s_scr[m] for m in range(4))
        for m in range(4):
            st_ref[m] = st[m]
        xs, bm, cm, dtr = _ssd_load(xbc_ref, dtr_ref)
        ys, st_new = ssd_chunk(st, xs, bm, cm, dtr, dtb_ref[...], al_ref[...])
        for m in range(4):
            y_ref[:, m * LANES:(m + 1) * LANES] = ys[m]
            s_scr[m] = st_new[m]

    vec = pl.BlockSpec((1, LANES), lambda b, i: (0, 0))
    return pl.pallas_call(
        body, name="ssd_fwd", grid=(bsz, nc),
        in_specs=[pl.BlockSpec((None, n, 2 * SSM_DIM), lambda b, i: (b, i, 0)),
                  pl.BlockSpec((None, n, LANES), lambda b, i: (b, i, dt_col)), vec, vec],
        out_specs=[pl.BlockSpec((None, n, SSM_DIM), lambda b, i: (b, i, 0)),
                   pl.BlockSpec((None, None, 4, SSM_STATE, LANES), lambda b, i: (b, i, 0, 0, 0))],
        out_shape=[jax.ShapeDtypeStruct((bsz, t, SSM_DIM), F32),
                   jax.ShapeDtypeStruct((bsz, nc, 4, SSM_STATE, LANES), F32)],
        scratch_shapes=[pltpu.VMEM((4, SSM_STATE, LANES), F32)],
        compiler_params=_params(("parallel", "arbitrary")),
    )(xbc, proj, dt_bias, a_log)


def ssd_bwd(xbc, proj, dt_col, dt_bias, a_log, states, dy, dxs_extra):
    bsz, t, _ = xbc.shape
    n = SSM_CHUNK
    nc = t // n

    def body(xbc_ref, dtr_ref, dtb_ref, al_ref, st_ref, dy_ref, ex_ref,
             dxbc_ref, ddtr_ref, ddtb_ref, dal_ref, ds_scr):
        first = jnp.logical_and(pl.program_id(0) == 0, pl.program_id(1) == 0)

        @pl.when(pl.program_id(1) == 0)
        def _():
            ds_scr[...] = jnp.zeros_like(ds_scr)

        @pl.when(first)
        def _():
            ddtb_ref[...] = jnp.zeros_like(ddtb_ref)
            dal_ref[...] = jnp.zeros_like(dal_ref)

        st = tuple(st_ref[m] for m in range(4))
        xs, bm, cm, dtr = _ssd_load(xbc_ref, dtr_ref)
        _, vjp = jax.vjp(ssd_chunk, st, xs, bm, cm, dtr, dtb_ref[...], al_ref[...])
        dys = tuple(dy_ref[:, m * LANES:(m + 1) * LANES] for m in range(4))
        dst_in = tuple(ds_scr[m] for m in range(4))
        dst, dxs, dbm, dcm, ddtr, ddtb, dal = vjp((dys, dst_in))
        for m in range(4):
            ds_scr[m] = dst[m]
            sl = slice(m * LANES, (m + 1) * LANES)
            dxbc_ref[:, sl] = dxs[m] + ex_ref[:, sl]
        for g in range(2):
            dxbc_ref[:, SSM_DIM + g * LANES:SSM_DIM + (g + 1) * LANES] = dbm[g]
            dxbc_ref[:, SSM_DIM + 2 * LANES + g * LANES:SSM_DIM + 2 * LANES + (g + 1) * LANES] = dcm[g]
        ddtr_ref[...] = ddtr
        ddtb_ref[...] += ddtb
        dal_ref[...] += dal

    vec = pl.BlockSpec((1, LANES), lambda b, i: (0, 0))
    rev = lambda b, i: (b, nc - 1 - i, 0)
    return pl.pallas_call(
        body, name="ssd_bwd", grid=(bsz, nc),
        in_specs=[pl.BlockSpec((None, n, 2 * SSM_DIM), rev),
                  pl.BlockSpec((None, n, LANES), lambda b, i: (b, nc - 1 - i, dt_col)), vec, vec,
                  pl.BlockSpec((None, None, 4, SSM_STATE, LANES), lambda b, i: (b, nc - 1 - i, 0, 0, 0)),
                  pl.BlockSpec((None, n, SSM_DIM), rev), pl.BlockSpec((None, n, SSM_DIM), rev)],
        out_specs=[pl.BlockSpec((None, n, 2 * SSM_DIM), rev), pl.BlockSpec((None, n, LANES), rev), vec, vec],
        out_shape=[jax.ShapeDtypeStruct((bsz, t, 2 * SSM_DIM), F32), jax.ShapeDtypeStruct((bsz, t, LANES), F32),
                   jax.ShapeDtypeStruct((1, LANES), F32), jax.ShapeDtypeStruct((1, LANES), F32)],
        scratch_shapes=[pltpu.VMEM((4, SSM_STATE, LANES), F32)],
        compiler_params=_params(("arbitrary", "arbitrary")),
    )(xbc, proj, dt_bias, a_log, states, dy, dxs_extra)


def sb_block(q, kj, vj, carry, maskf):
    mask = maskf > 0.5
    z = _each(lambda q_, k_: dg(q_, k_, 1, 1, P_BF16) * (HEAD_DIM ** -0.5), q, kj)
    lk = _each(lambda z_: jnp.where(mask, -_softplus(z_), 0.0), z)
    sfx = _each(suffix_sum, lk)
    att = _each(lambda z_, c_, s_: jnp.exp(jnp.where(mask, z_ + c_ + s_, NEG_BIG)), z, carry, sfx)
    out = _each(mmb, att, vj)
    return out, _each(lambda c_, k_: c_ + jnp.sum(k_, axis=1, keepdims=True), carry, lk)


def _sb_mask(qi, j):
    n = Q_BLOCK
    return jnp.where(j * n + _iota((n, n), 1) < qi * n + _iota((n, n), 0), 1.0, 0.0)


def sb_fwd(q, k, v, ride=None):
    bsz, h, t, d = q.shape
    n = Q_BLOCK
    hp = ATTN_FWD_HEADS_PER_STEP
    grid = (bsz, h // hp, t // n)
    r_in, r_specs, r_out, r_ospecs, r_scr = _ride_args(ride)

    def body(*refs):
        q_ref, k_ref, v_ref = refs[:3]
        o_ref, c_ref = refs[3 + len(r_in):5 + len(r_in)]
        if ride is not None:
            first, last = _grid_first_last(grid)
            copies = _ride_start(refs[3:3 + len(r_in)], refs[5 + len(r_in):5 + 2 * len(r_in)], refs[-3:], ride[1], first)
        qi = pl.program_id(2)
        lane = _iota((n, LANES), 1)

        c_ref[...] = jnp.zeros_like(c_ref)
        o_ref[...] = jnp.zeros_like(o_ref)

        def step(i, carry):
            j = qi - i
            rows = pl.ds(pl.multiple_of(j * n, n), n)
            for hh in range(hp):
                c_ref[hh] = jnp.where(lane == j, carry[hh], c_ref[hh])
            o, carry = sb_block([q_ref[hh] for hh in range(hp)], [k_ref[hh, rows, :] for hh in range(hp)],
                                [v_ref[hh, rows, :] for hh in range(hp)], carry, _sb_mask(qi, j))
            for hh in range(hp):
                o_ref[hh] += o[hh]
            return carry

        lax.fori_loop(0, qi + 1, step, [jnp.zeros((n, 1), F32) for _ in range(hp)])
        if ride is not None:
            _ride_wait(copies, last)

    blk = pl.BlockSpec((None, hp, n, d), lambda b, hg, i: (b, hg, i, 0))
    cblk = pl.BlockSpec((None, hp, n, LANES), lambda b, hg, i: (b, hg, i, 0))
    full = pl.BlockSpec((None, hp, t, d), lambda b, hg, i: (b, hg, 0, 0))
    return pl.pallas_call(
        body, name="sb_fwd", grid=grid, in_specs=[blk, full, full] + r_specs, out_specs=[blk, cblk] + r_ospecs,
        out_shape=[jax.ShapeDtypeStruct((bsz, h, t, d), F32), jax.ShapeDtypeStruct((bsz, h, t, LANES), F32)] + r_out,
        scratch_shapes=r_scr, compiler_params=_params(("arbitrary", "arbitrary", "arbitrary")),
    )(q, k, v, *r_in)


def sb_bwd(q, k, v, kept, do, ride=None):
    bsz, h, t, d = q.shape
    n = Q_BLOCK
    hp = SB_HEADS_PER_STEP
    grid = (bsz, h // hp, t // n)
    r_in, r_specs, r_out, r_ospecs, r_scr = _ride_args(ride)

    def body(*refs):
        q_ref, k_ref, v_ref, c_ref, do_ref = refs[:5]
        nin = 5 + len(r_in)
        dq_ref, dk_ref, dv_ref = refs[nin:nin + 3]
        if ride is not None:
            first, last = _grid_first_last(grid)
            copies = _ride_start(refs[5:nin], refs[nin + 3:nin + 3 + len(r_in)], refs[-3:], ride[1], first)
        qi = pl.program_id(2)

        @pl.when(qi == 0)
        def _():
            dk_ref[...] = jnp.zeros_like(dk_ref)
            dv_ref[...] = jnp.zeros_like(dv_ref)

        heads = range(hp)
        qv = [q_ref[hh] for hh in heads]
        kept_v = [c_ref[hh] for hh in heads]
        lane = _iota((n, LANES), 1)

        dq_ref[...] = jnp.zeros_like(dq_ref)

        def bwd_step(j, dcarry):
            rows = pl.ds(pl.multiple_of(j * n, n), n)
            carry_in = [jnp.sum(jnp.where(lane == j, t_, 0.0), axis=1, keepdims=True) for t_ in kept_v]
            _, vjp = jax.vjp(sb_block, qv, [k_ref[hh, rows, :] for hh in heads],
                             [v_ref[hh, rows, :] for hh in heads], carry_in, _sb_mask(qi, j))
            dqj, dkj, dvj, dc, _ = vjp(([do_ref[hh] for hh in heads], dcarry))
            for hh in heads:
                dq_ref[hh] += dqj[hh]
                dk_ref[hh, rows, :] += dkj[hh]
                dv_ref[hh, rows, :] += dvj[hh]
            return dc

        lax.fori_loop(0, qi + 1, bwd_step, [jnp.zeros((n, 1), F32) for _ in heads])
        if ride is not None:
            _ride_wait(copies, last)

    blk = pl.BlockSpec((None, hp, n, d), lambda b, hg, i: (b, hg, i, 0))
    cblk = pl.BlockSpec((None, hp, n, LANES), lambda b, hg, i: (b, hg, i, 0))
    full = pl.BlockSpec((None, hp, t, d), lambda b, hg, i: (b, hg, 0, 0))
    shp = jax.ShapeDtypeStruct((bsz, h, t, d), F32)
    return pl.pallas_call(
        body, name="sb_bwd", grid=grid, in_specs=[blk, full, full, cblk, blk] + r_specs,
        out_specs=[blk, full, full] + r_ospecs, out_shape=[shp, shp, shp] + r_out,
        scratch_shapes=r_scr, compiler_params=_params(("arbitrary", "arbitrary", "arbitrary")),
    )(q, k, v, kept, do, *r_in)


def _bdot(a, b, ca, cb):
    return _dg_raw(a, b, ca, cb, P_BF16)


def _mla_scores(qn, qp, knj, kpj, qi, j):
    n = Q_BLOCK
    mask = j * n + _iota((n, n), 1) <= qi * n + _iota((n, n), 0)
    scale = (MLA_NOPE + MLA_ROPE) ** -0.5
    return _each(lambda a_, b_, k_: jnp.where(mask, (_bdot(a_, k_, 1, 1) + _bdot(b_, kpj, 1, 1)) * scale, NEG_BIG),
                 qn, qp, knj)


def _mla_specs(t, hp):
    n = Q_BLOCK
    return (pl.BlockSpec((None, hp, n, MLA_NOPE), lambda b, hg, i: (b, hg, i, 0)),
            pl.BlockSpec((None, hp, n, MLA_ROPE), lambda b, hg, i: (b, hg, i, 0)),
            pl.BlockSpec((None, hp, t, MLA_NOPE), lambda b, hg, i: (b, hg, 0, 0)),
            pl.BlockSpec((None, None, t, MLA_ROPE), lambda b, hg, i: (b, 0, 0, 0)),
            pl.BlockSpec((None, hp, n, 1), lambda b, hg, i: (b, hg, i, 0)))


def mla_fwd(qn, qp, kn, kp, v):
    bsz, h, t, _ = qn.shape
    n, hp = Q_BLOCK, ATTN_FWD_HEADS_PER_STEP
    heads = range(hp)

    def body(qn_ref, qp_ref, kn_ref, kp_ref, v_ref, o_ref, lse_ref):
        qi = pl.program_id(2)
        qn_v, qp_v = [qn_ref[hh] for hh in heads], [qp_ref[hh] for hh in heads]

        o_ref[...] = jnp.zeros_like(o_ref)

        def step(j, state):
            m, l = state
            rows = pl.ds(pl.multiple_of(j * n, n), n)
            s = _mla_scores(qn_v, qp_v, [kn_ref[hh, rows, :] for hh in heads], kp_ref[rows, :], qi, j)
            m_new = _each(lambda m_, s_: jnp.maximum(m_, jnp.max(s_, axis=1, keepdims=True)), m, s)
            p = _each(lambda s_, m_: jnp.exp(s_ - m_), s, m_new)
            corr = _each(lambda a_, b_: jnp.exp(a_ - b_), m, m_new)
            l = _each(lambda l_, c_, p_: l_ * c_ + jnp.sum(p_, axis=1, keepdims=True), l, corr, p)
            pv = _each(lambda p_, v_: _bdot(p_, v_, 1, 0), p, [v_ref[hh, rows, :] for hh in heads])
            for hh in heads:
                o_ref[hh] = o_ref[hh] * corr[hh] + pv[hh]
            return m_new, l

        init = ([jnp.full((n, 1), NEG_BIG, F32) for _ in heads], [jnp.zeros((n, 1), F32) for _ in heads])
        m, l = lax.fori_loop(0, qi + 1, step, init)
        for hh in heads:
            o_ref[hh] = o_ref[hh] / l[hh]
            lse_ref[hh] = m[hh] + jnp.log(l[hh])

    qn_s, qp_s, kn_s, kp_s, row_s = _mla_specs(t, hp)
    return pl.pallas_call(
        body, name="mla_fwd", grid=(bsz, h // hp, t // n), in_specs=[qn_s, qp_s, kn_s, kp_s, kn_s],
        out_specs=[qn_s, row_s],
        out_shape=[jax.ShapeDtypeStruct(qn.shape, F32), jax.ShapeDtypeStruct((bsz, h, t, 1), F32)],
        compiler_params=_params(("parallel", "parallel", "arbitrary")),
    )(qn, qp, kn, kp, v)


def mla_bwd(qn, qp, kn, kp, v, o, lse, do):
    bsz, h, t, _ = qn.shape
    n, hp = Q_BLOCK, MLA_HEADS_PER_STEP
    heads = range(hp)
    scale = (MLA_NOPE + MLA_ROPE) ** -0.5

    def body(qn_ref, qp_ref, kn_ref, kp_ref, v_ref, o_ref, lse_ref, do_ref,
             dqn_ref, dqp_ref, dkn_ref, dkp_ref, dv_ref):
        hg, qi = pl.program_id(1), pl.program_id(2)

        @pl.when(qi == 0)
        def _():
            dkn_ref[...] = jnp.zeros_like(dkn_ref)
            dv_ref[...] = jnp.zeros_like(dv_ref)

        @pl.when(jnp.logical_and(qi == 0, hg == 0))
        def _():
            dkp_ref[...] = jnp.zeros_like(dkp_ref)

        qn_v, qp_v = [qn_ref[hh] for hh in heads], [qp_ref[hh] for hh in heads]
        do_v, lse_v = [do_ref[hh] for hh in heads], [lse_ref[hh] for hh in heads]
        dsum = [jnp.sum(do_v[hh] * o_ref[hh], axis=1, keepdims=True) for hh in heads]

        dqn_ref[...] = jnp.zeros_like(dqn_ref)
        dqp_ref[...] = jnp.zeros_like(dqp_ref)

        def step(j, _):
            rows = pl.ds(pl.multiple_of(j * n, n), n)
            knj, vj, kpj = [kn_ref[hh, rows, :] for hh in heads], [v_ref[hh, rows, :] for hh in heads], kp_ref[rows, :]
            s = _mla_scores(qn_v, qp_v, knj, kpj, qi, j)
            p = _each(lambda s_, l_: jnp.exp(s_ - l_), s, lse_v)
            dp = _each(lambda d_, v_: _bdot(d_, v_, 1, 1), do_v, vj)
            ds = _each(lambda p_, dp_, d_: p_ * (dp_ - d_) * scale, p, dp, dsum)
            dqn = _each(lambda ds_, k_: _bdot(ds_, k_, 1, 0), ds, knj)
            dqp = _each(lambda ds_: _bdot(ds_, kpj, 1, 0), ds)
            dkn = _each(lambda ds_, q_: _bdot(ds_, q_, 0, 0), ds, qn_v)
            dv = _each(lambda p_, d_: _bdot(p_, d_, 0, 0), p, do_v)
            dkp = _each(lambda ds_, q_: _bdot(ds_, q_, 0, 0), ds, qp_v)
            for hh in heads:
                dqn_ref[hh] += dqn[hh]
                dqp_ref[hh] += dqp[hh]
                dkn_ref[hh, rows, :] += dkn[hh]
                dv_ref[hh, rows, :] += dv[hh]
            dkp_ref[rows, :] += functools.reduce(lambda a_, b_: a_ + b_, dkp)
            return 0

        lax.fori_loop(0, qi + 1, step, 0)

    qn_s, qp_s, kn_s, kp_s, row_s = _mla_specs(t, hp)
    return pl.pallas_call(
        body, name="mla_bwd", grid=(bsz, h // hp, t // n),
        in_specs=[qn_s, qp_s, kn_s, kp_s, kn_s, qn_s, row_s, qn_s],
        out_specs=[qn_s, qp_s, kn_s, kp_s, kn_s],
        out_shape=[jax.ShapeDtypeStruct(qn.shape, F32), jax.ShapeDtypeStruct(qp.shape, F32),
                   jax.ShapeDtypeStruct(kn.shape, F32), jax.ShapeDtypeStruct(kp.shape, F32),
                   jax.ShapeDtypeStruct(v.shape, F32)],
        compiler_params=_params(("parallel", "arbitrary", "arbitrary")),
    )(qn, qp, kn, kp, v, o, lse, do)


def rope(name, x, pos, inv_freq, sign):
    bsz, hx, t, d = x.shape
    half = d // 2

    tt = _largest_tile(t, 512, SUBLANES)

    def body(x_ref, pos_ref, f_ref, o_ref):
        ang = pos_ref[...].astype(F32) * f_ref[...]
        cos, sin = jnp.cos(ang), sign * jnp.sin(ang)
        ri, ci = _iota((d, d), 0), _iota((d, d), 1)
        rot = jnp.where(ri == ci + half, -1.0, 0.0) + jnp.where(ri + half == ci, 1.0, 0.0)
        for hh in range(hx):
            xv = x_ref[hh]
            o_ref[hh] = xv * cos + mmf(xv, rot) * sin

    blk = pl.BlockSpec((None, hx, tt, d), lambda b, i: (b, 0, i, 0))
    return pl.pallas_call(
        body, name=name, grid=(bsz, t // tt),
        in_specs=[blk, pl.BlockSpec((None, tt, 1), lambda b, i: (b, i, 0)), pl.BlockSpec((1, d), lambda b, i: (0, 0))],
        out_specs=blk, out_shape=jax.ShapeDtypeStruct(x.shape, F32),
        compiler_params=_params(("parallel", "parallel")),
    )(x, pos, inv_freq)


def loss_head(h, target, tm):
    n, d = h.shape

    def body(h_ref, t_ref, dh_ref, l_ref):
        @pl.when(pl.program_id(0) == 0)
        def _():
            l_ref[...] = jnp.zeros_like(l_ref)

        e = h_ref[...] - t_ref[...]
        dh_ref[...] = e * (1.0 / d)
        l_ref[...] += jnp.sum(e * e, axis=(0, 1), keepdims=True) * (0.5 / d)

    row = pl.BlockSpec((tm, d), lambda i: (i, 0))
    dh, l = pl.pallas_call(
        body, name="loss_head", grid=(n // tm,), in_specs=[row, row],
        out_specs=[row, pl.BlockSpec((SUBLANES, LANES), lambda i: (0, 0))],
        out_shape=[jax.ShapeDtypeStruct((n, d), F32), jax.ShapeDtypeStruct((SUBLANES, LANES), F32)],
        compiler_params=_params(("arbitrary",)),
    )(h, target)
    return dh, l[0, 0]


def _exchange_copies(src_refs, out_refs, send_sems, recv_sems, local_sems, gather):
    x, y, c = lax.axis_index("x"), lax.axis_index("y"), lax.axis_index("c")
    me = 4 * x + 2 * y + c
    local, remote = [], []
    for p, (src_ref, out_ref) in enumerate(zip(src_refs, out_refs)):
        local.append(pltpu.make_async_copy(src_ref if gather else src_ref.at[me], out_ref.at[me], local_sems.at[p]))
        for m in range(1, N_DEV):
            px, py, pc = x ^ (m >> 2), y ^ ((m >> 1) & 1), c ^ (m & 1)
            peer = 4 * px + 2 * py + pc
            remote.append(pltpu.make_async_remote_copy(
                src_ref=src_ref if gather else src_ref.at[peer], dst_ref=out_ref.at[me],
                send_sem=send_sems.at[p, m], recv_sem=recv_sems.at[p, m],
                device_id=(px, py, pc), device_id_type=pl.DeviceIdType.MESH))
    return local, remote


def _exchange_start(copies):
    local, remote = copies
    for cp in local + remote:
        cp.start()


def _exchange_wait(copies):
    local, remote = copies
    for cp in remote:
        cp.wait_recv()
    for cp in remote:
        cp.wait_send()
    for cp in local:
        cp.wait()


def _exchange_scratch(count):
    return [pltpu.SemaphoreType.DMA((count, N_DEV)), pltpu.SemaphoreType.DMA((count, N_DEV)),
            pltpu.SemaphoreType.DMA((count,))]


def _exchange_out(src, gather):
    return jax.ShapeDtypeStruct(((N_DEV,) + src.shape) if gather else src.shape, src.dtype)


def peer_exchange(name, srcs, gather):
    count = len(srcs)

    def body(*refs):
        copies = _exchange_copies(refs[:count], refs[count:2 * count], *refs[2 * count:], gather)
        _exchange_start(copies)
        _exchange_wait(copies)

    hbm = pl.BlockSpec(memory_space=pl.ANY)
    return pl.pallas_call(
        body, name=name, in_specs=[hbm] * count, out_specs=[hbm] * count,
        out_shape=[_exchange_out(s_, gather) for s_ in srcs], scratch_shapes=_exchange_scratch(count),
    )(*srcs)


def _grid_first_last(grid):
    ids = [pl.program_id(a) for a in range(len(grid))]
    first = functools.reduce(jnp.logical_and, [i == 0 for i in ids])
    last = functools.reduce(jnp.logical_and, [i == g - 1 for i, g in zip(ids, grid)])
    return first, last


def _ride_start(src_refs, out_refs, sem_refs, gather, first):
    copies = _exchange_copies(src_refs, out_refs, *sem_refs, gather)

    @pl.when(first)
    def _():
        _exchange_start(copies)

    return copies


def _ride_wait(copies, last):
    @pl.when(last)
    def _():
        _exchange_wait(copies)


def _ride_args(ride):
    if ride is None:
        return [], [], [], [], []
    srcs, gather = ride
    hbm = pl.BlockSpec(memory_space=pl.ANY)
    return (list(srcs), [hbm] * len(srcs), [_exchange_out(s_, gather) for s_ in srcs], [hbm] * len(srcs),
            _exchange_scratch(len(srcs)))


ADAM_BLOCK_BYTES = 4 * 1024 * 1024


def adamw_sum(name, parts, w, m, v):
    r, cols = w.shape
    tr = _largest_tile(r, min(ADAM_ROWS, max(SUBLANES, ADAM_BLOCK_BYTES // (N_DEV * cols * 4))), SUBLANES)

    def body(p_ref, w_ref, m_ref, v_ref, g_ref, d_ref, nm_ref, nv_ref):
        g = p_ref[0]
        for j in range(1, N_DEV):
            g = g + p_ref[j]
        mm_ = ADAM_B1 * m_ref[...] + (1.0 - ADAM_B1) * g
        vv = ADAM_B2 * v_ref[...] + (1.0 - ADAM_B2) * (g * g)
        m_hat = mm_ / (1.0 - ADAM_B1 ** ADAM_STEP)
        v_hat = vv / (1.0 - ADAM_B2 ** ADAM_STEP)
        g_ref[...] = g
        d_ref[...] = -ADAM_LR * (m_hat / (jnp.sqrt(v_hat) + ADAM_EPS) + ADAM_WD * w_ref[...])
        nm_ref[...] = mm_
        nv_ref[...] = vv

    row = pl.BlockSpec((tr, cols), lambda i: (i, 0))
    shp = jax.ShapeDtypeStruct((r, cols), F32)
    return pl.pallas_call(
        body, name=name, grid=(r // tr,),
        in_specs=[pl.BlockSpec((N_DEV, tr, cols), lambda i: (0, i, 0)), row, row, row],
        out_specs=[row] * 4, out_shape=[shp] * 4,
        compiler_params=_params(("parallel",)),
    )(parts, w, m, v)


WEIGHTS = ['l0_w_in', 'rwkv_mix', 'rwkv_w0', 'rwkv_w2', 'rwkv_a0', 'rwkv_a2', 'rwkv_g2', 'rwkv_k_k', 'rwkv_k_a',
           'rwkv_r_k', 'rwkv_ln_g', 'rwkv_ln_b', 'ssm_conv_w', 'ssm_conv_b', 'ssm_dt_bias', 'ssm_a_log', 'ssm_d',
           'ssm_norm_g', 'l0_w_out', 'l0_ln1_g', 'l0_ln1_b', 'ffn0_w_up', 'ffn0_conv_w', 'ffn0_conv_b',
           'ffn0_w_down', 'l0_ln2_g', 'l0_ln2_b', 'l1_w_in', 'mla_q_norm_g', 'mla_w_uq', 'mla_kv_norm_g',
           'mla_w_ukv', 'l1_w_out', 'l1_ln1_g', 'l1_ln1_b', 'ffn1_w_up', 'ffn1_conv_w', 'ffn1_conv_b',
           'ffn1_w_down', 'l1_ln2_g', 'l1_ln2_b']
SHARD_AXIS = {'l0_w_in': 1, 'rwkv_w2': 1, 'rwkv_a2': 1, 'rwkv_g2': 1, 'ssm_conv_w': 1, 'l0_w_out': 0,
              'ffn0_w_up': 1, 'ffn0_conv_w': 1, 'ffn0_w_down': 0, 'l1_w_in': 1, 'mla_w_uq': 1, 'mla_w_ukv': 1,
              'l1_w_out': 0, 'ffn1_w_up': 1, 'ffn1_conv_w': 1, 'ffn1_w_down': 0}
MATMUL_W = ['l0_w_in', 'rwkv_w2', 'rwkv_a2', 'rwkv_g2', 'l0_w_out', 'ffn0_w_up', 'ffn0_w_down', 'l1_w_in',
            'mla_w_uq', 'mla_w_ukv', 'l1_w_out', 'ffn1_w_up', 'ffn1_w_down']
CONV_W = ['ssm_conv_w', 'ffn0_conv_w', 'ffn1_conv_w']
TOK_TILE = 256
ADAM_ROWS = 512


def _ceil_to(size, unit):
    return -(-size // unit) * unit


def _flat_rows(pieces, seg_rows, total_rows):
    unit = seg_rows * LANES
    out, total = [], 0
    for p in pieces:
        p = jnp.concatenate([q.reshape(-1) for q in p]) if isinstance(p, list) else p.reshape(-1)
        pad = _ceil_to(p.size, unit) - p.size
        out.append(jnp.pad(p, (0, pad)) if pad else p)
        total += p.size + pad
    tail = _ceil_to(total, total_rows * LANES) - total
    if tail:
        out.append(jnp.zeros((tail,), out[0].dtype))
    return jnp.concatenate(out).reshape(-1, LANES)


def _unflatten(flat2d, shapes, seg_rows):
    flat = flat2d.reshape(-1)
    out, off = [], 0
    for shp in shapes:
        if isinstance(shp, list):
            seg, pos = [], off
            for s_ in shp:
                seg.append(flat[pos:pos + math.prod(s_)].reshape(s_))
                pos += math.prod(s_)
            out.append(seg)
            size = pos - off
        else:
            size = math.prod(shp)
            out.append(flat[off:off + size].reshape(shp))
        off += _ceil_to(size, seg_rows * LANES)
    return out


def _row(v):
    return v.reshape(1, -1)


def _pad_lanes(v):
    return jnp.pad(v.reshape(1, -1), ((0, 0), (0, LANES - v.size)))


def _local_step(a, w, comm=None):
    x = a['x']
    bsz, t, d = x.shape
    n = bsz * t
    tm = TOK_TILE
    pos = a['positions'].reshape(bsz, t, 1)
    inv_freq = 1.0 / (ROPE_THETA ** (jnp.arange(0, MLA_ROPE, 2, dtype=F32) / MLA_ROPE))
    inv_freq = jnp.concatenate([inv_freq, inv_freq]).reshape(1, MLA_ROPE)
    target = a['loss_target'].reshape(n, d)

    wi0 = w['l0_w_in']
    win0 = jnp.concatenate([wi0[:, :RKV_COLS], wi0[:, RWKV_COLS:L0_COLS - SSM_HEADS], wi0[:, RKV_COLS:RWKV_COLS],
                            wi0[:, L0_COLS - SSM_HEADS:], jnp.zeros((d, L0_PAD - L0_COLS), wi0.dtype)], axis=1)
    w2p = jnp.concatenate([w['rwkv_w2'], jnp.zeros_like(w['rwkv_w2'])], axis=0)
    a2p = jnp.concatenate([jnp.zeros_like(w['rwkv_a2']), w['rwkv_a2']], axis=0)
    mix = a['rwkv_mix']
    taps = jnp.stack([mix, 1.0 - mix])
    zero_b = jnp.zeros((1, mix.size), F32)
    rw_map = lambda j: j + jnp.where(j >= RKV_COLS // LANES, (P0_LORA - RKV_COLS) // LANES, 0)
    ssm_map = lambda j: j + (P0_Z + SSM_DIM) // LANES
    gate_map = lambda j: j
    up_map = lambda j: j + D_FF // LANES
    dt_col = P0_DT // LANES
    dtb, alog, dsk = _pad_lanes(a['ssm_dt_bias']), _pad_lanes(a['ssm_a_log']), _pad_lanes(a['ssm_d'])
    pre_p = [_row(a['rwkv_w0']), w2p, _row(a['rwkv_a0']), a2p, w['rwkv_g2'], _row(a['rwkv_k_k']), _row(a['rwkv_k_a'])]
    post_p = [_row(a['rwkv_ln_g']), _row(a['rwkv_ln_b']), _row(a['rwkv_r_k'])]
    sp_p = [dsk, _row(a['ssm_norm_g'])]

    def ln(name, h, y, layer, which):
        ps = [_row(a[f'l{layer}_ln{which}_g']), _row(a[f'l{layer}_ln{which}_b'])]
        return tok_fwd(name, f_ln, [(h, d, 0), (y, d, 0)], ps, [d], tm, n, t)[0]

    def ffn_fwd(layer, h):
        up = mm(h, w[f'ffn{layer}_w_up'], 'nn', f'ffn{layer}_up')
        act = dwconv_fwd(f'ffn{layer}_conv', up.reshape(bsz, t, 2 * D_FF), gate_map, w[f'ffn{layer}_conv_w'],
                         _row(a[f'ffn{layer}_conv_b']), 3, True, upmap=up_map, out_dtype=BF16)
        act = act.reshape(n, D_FF)
        return up, act, mm(act, w[f'ffn{layer}_w_down'], 'nn', f'ffn{layer}_down')

    x2 = x.reshape(n, d)
    proj0 = mm(x2, win0, 'nn', 'l0_in')
    p0 = proj0.reshape(bsz, t, L0_PAD)
    xs_r = dwconv_fwd('rwkv_shift', p0, rw_map, taps, zero_b, 2, False).reshape(n, RWKV_COLS)
    lora_blk = RKV_COLS // LANES
    pre_x = [(xs_r, RWKV_DIM, 0), (xs_r, RWKV_DIM, 1), (xs_r, RWKV_DIM, 2), (xs_r, LANES, lora_blk),
             (xs_r, LANES, lora_blk + 1)]
    heads64 = (RWKV_HEADS, HEAD_DIM)
    r_, v_, lw, kmod, al, be, gt = tok_fwd('rwkv_pre', f_rwkv_pre, pre_x, pre_p, [heads64] * 6 + [RWKV_DIM],
                                           tm, n, t)
    scan_in = [r_, lw, kmod, v_, al, be]
    if comm is None:
        y_h, rstates = rwkv_scan_fwd(*scan_in)
    else:
        y_h, rstates, *gots = rwkv_scan_fwd(*scan_in, ride=comm.weights_ride(MID_GATHER))
        w = {**w, **comm.weights(MID_GATHER, gots)}
    wi1 = w['l1_w_in']
    win1 = jnp.concatenate([wi1, jnp.zeros((d, L1_PAD - L1_COLS), wi1.dtype)], axis=1)
    wq3 = w['mla_w_uq'].reshape(-1, 8, MLA_NOPE + MLA_ROPE)
    wkv3 = w['mla_w_ukv'].reshape(-1, 8, 2 * MLA_NOPE)
    mla_p = [_row(a['mla_q_norm_g']), wq3[:, :, :MLA_NOPE].reshape(-1, 512), wq3[:, :, MLA_NOPE:].reshape(-1, 256),
             _row(a['mla_kv_norm_g']), wkv3[:, :, :MLA_NOPE].reshape(-1, 512), wkv3[:, :, MLA_NOPE:].reshape(-1, 512)]
    post_x = [y_h, r_, kmod, v_, (gt, 512, 0)]
    y_a = tok_fwd('rwkv_post', f_rwkv_post, post_x, post_p, [RWKV_DIM], tm, n, t)[0]
    xbc = dwconv_fwd('ssm_conv', p0, ssm_map, w['ssm_conv_w'], _row(a['ssm_conv_b']), 4, True)
    ys, sstates = ssd_fwd(xbc, p0, dt_col, dtb, alog)
    xbc2 = xbc.reshape(n, 2 * SSM_DIM)
    sp_x = [(ys.reshape(n, SSM_DIM), SSM_DIM, 0), (xbc2, SSM_DIM, 0), (proj0, SSM_DIM, P0_Z // SSM_DIM)]
    y_b = tok_fwd('ssd_post', f_ssd_post, sp_x, sp_p, [SSM_DIM], tm, n, t)[0]
    wo0 = w['l0_w_out']
    mixed0 = mm(y_b, wo0, 'nn', 'l0_out_b', b_rows=(512, 512), add=mm(y_a, wo0, 'nn', 'l0_out_a', b_rows=(0, 512)))
    h1 = ln('l0_ln1', x2, mixed0, 0, 1)
    up0, act0, f0 = ffn_fwd(0, h1)
    h2 = ln('l0_ln2', h1, f0, 0, 2)

    proj1 = mm(h2, win1, 'nn', 'l1_in')
    q_sb, k_sb, v_sb = tok_fwd('sb_split', f_same, [(proj1, 512, 0), (proj1, 512, 1), (proj1, 512, 2)], [],
                               [heads64] * 3, tm, n, t)
    if comm is None:
        o_c, sb_kept = sb_fwd(q_sb, k_sb, v_sb)
    else:
        o_c, sb_kept, *gots = sb_fwd(q_sb, k_sb, v_sb, ride=comm.weights_ride(FFN1_GATHER))
        w = {**w, **comm.weights(FFN1_GATHER, gots)}
    mla_x = [(proj1, P1_CKV - P1_CQ, P1_CQ // (P1_CKV - P1_CQ)), (proj1, LANES, P1_CKV // LANES)]
    qn, qp_raw, kn, vv = tok_fwd('mla_pre', f_mla_pre, mla_x, mla_p, [heads64, (8, MLA_ROPE), heads64, heads64],
                                 tm, n, t)
    kp_raw = proj1[:, P1_KPE:P1_KPE + MLA_ROPE].reshape(bsz, 1, t, MLA_ROPE)
    qp = rope('rope_q', qp_raw, pos, inv_freq, 1.0)
    kp = rope('rope_k', kp_raw, pos, inv_freq, 1.0)
    o_d, lse_d = mla_fwd(qn, qp, kn, kp, vv)
    y_cd = tok_fwd('attn_merge', f_concat, [o_c, o_d], [], [2 * RWKV_DIM], tm, n, t, out_dtype=BF16)[0]
    wo1 = w['l1_w_out']
    mixed1 = mm(y_cd, wo1, 'nn', 'l1_out')
    h3 = ln('l1_ln1', h2, mixed1, 1, 1)
    up1, act1, f1 = ffn_fwd(1, h3)
    h4 = ln('l1_ln2', h3, f1, 1, 2)
    dh4, loss = loss_head(h4, target, tm)

    g = {}

    def ln_bwd(name, h, y, layer, which, dout):
        ps = [_row(a[f'l{layer}_ln{which}_g']), _row(a[f'l{layer}_ln{which}_b'])]
        (dh, dy), (dg, db) = tok_bwd(name, f_ln, [(h, d, 0), (y, d, 0)], ps, [[dout]], tm, n, t)
        g[f'l{layer}_ln{which}_g'], g[f'l{layer}_ln{which}_b'] = dg.reshape(-1), db.reshape(-1)
        return dh, dy

    def ffn_bwd(layer, h, up, act, df, dh_res):
        wup, wdown = w[f'ffn{layer}_w_up'], w[f'ffn{layer}_w_down']
        g[f'ffn{layer}_w_down'] = mm(act, df, 'tn', f'ffn{layer}_dwdown')
        dact = mm(df, wdown, 'nt', f'ffn{layer}_dact').reshape(bsz, t, D_FF)
        dgate, dcw, dcb, dup = dwconv_bwd(f'ffn{layer}_conv_bwd', up.reshape(bsz, t, 2 * D_FF), gate_map,
                                          w[f'ffn{layer}_conv_w'], _row(a[f'ffn{layer}_conv_b']), 3, True, dact,
                                          upmap=up_map, grad_dtype=BF16)
        dgate, dup = dgate.reshape(n, D_FF), dup.reshape(n, D_FF)
        g[f'ffn{layer}_conv_w'], g[f'ffn{layer}_conv_b'] = dcw, dcb.reshape(-1)
        g[f'ffn{layer}_w_up'] = (mm(h, dgate, 'tn', f'ffn{layer}_dwgate'), mm(h, dup, 'tn', f'ffn{layer}_dwup'))
        dh = mm(dgate, wup, 'nt', f'ffn{layer}_dh_gate', add=dh_res, b_cols=(0, D_FF))
        return mm(dup, wup, 'nt', f'ffn{layer}_dh_up', add=dh, b_cols=(D_FF, D_FF))

    dh3_res, df1 = ln_bwd('l1_ln2_bwd', h3, f1, 1, 2, dh4)
    dh3 = ffn_bwd(1, h3, up1, act1, df1, dh3_res)
    dh2_res, dmixed1 = ln_bwd('l1_ln1_bwd', h2, mixed1, 1, 1, dh3)
    g['l1_w_out'] = mm(y_cd, dmixed1, 'tn', 'l1_dwout')
    dy_cd = mm(dmixed1, wo1, 'nt', 'l1_dy')
    dy_c, dy_d = tok_fwd('attn_split', f_same, [(dy_cd, 512, 0), (dy_cd, 512, 1)], [], [heads64] * 2, tm, n, t)
    parts = {}
    if comm is None:
        dq_sb, dk_sb, dv_sb = sb_bwd(q_sb, k_sb, v_sb, sb_kept, dy_c)
    else:
        dq_sb, dk_sb, dv_sb, *parts['a'] = sb_bwd(q_sb, k_sb, v_sb, sb_kept, dy_c, ride=comm.grad_ride('a', g))
    dqn, dqp, dkn, dkp, dvv = mla_bwd(qn, qp, kn, kp, vv, o_d, lse_d, dy_d)
    dqp_raw = rope('rope_q_bwd', dqp, pos, inv_freq, -1.0)
    dkp_raw = rope('rope_k_bwd', dkp, pos, inv_freq, -1.0).reshape(n, MLA_ROPE)
    (dcq, dckv), (dqg, dwq_n, dwq_p, dkvg, dwk, dwv) = tok_bwd('mla_pre_bwd', f_mla_pre, mla_x, mla_p,
                                                               [[dqn], [dqp_raw], [dkn], [dvv]], tm, n, t)
    g['mla_q_norm_g'], g['mla_kv_norm_g'] = dqg.reshape(-1), dkvg.reshape(-1)
    g['mla_w_uq'] = jnp.concatenate([dwq_n.reshape(-1, 8, MLA_NOPE), dwq_p.reshape(-1, 8, MLA_ROPE)],
                                    axis=2).reshape(-1, 8 * (MLA_NOPE + MLA_ROPE))
    g['mla_w_ukv'] = jnp.concatenate([dwk.reshape(-1, 8, MLA_NOPE), dwv.reshape(-1, 8, MLA_NOPE)],
                                     axis=2).reshape(-1, 16 * MLA_NOPE)
    dkp_pad = jnp.pad(dkp_raw, ((0, 0), (0, LANES - MLA_ROPE)))
    dproj1 = tok_fwd('l1_dproj', f_concat, [dq_sb, dk_sb, dv_sb, (dcq, 256, 0), (dckv, LANES, 0),
                                            (dkp_pad, LANES, 0)], [], [L1_PAD], tm, n, t, out_dtype=BF16)[0]
    g['l1_w_in'] = mm(h2, dproj1, 'tn', 'l1_dwin')[:, :L1_COLS]
    dh2 = mm(dproj1, win1, 'nt', 'l1_dh', add=dh2_res)

    dh1_res, df0 = ln_bwd('l0_ln2_bwd', h1, f0, 0, 2, dh2)
    dh1 = ffn_bwd(0, h1, up0, act0, df0, dh1_res)
    dx_res, dmixed0 = ln_bwd('l0_ln1_bwd', x2, mixed0, 0, 1, dh1)
    g['l0_w_out'] = (mm(y_a, dmixed0, 'tn', 'l0_dwout_a'), mm(y_b, dmixed0, 'tn', 'l0_dwout_b'))
    dy_a = mm(dmixed0, wo0, 'nt', 'l0_dy_a', b_rows=(0, 512))
    dy_b = mm(dmixed0, wo0, 'nt', 'l0_dy_b', b_rows=(512, 512))
    (dy_r, dr1, dkm1, dv1, dgt), (dlng, dlnb, drk) = tok_bwd('rwkv_post_bwd', f_rwkv_post, post_x, post_p, [[dy_a]],
                                                            tm, n, t, dx_layouts=[heads64] * 4 + [None])
    g['rwkv_ln_g'], g['rwkv_ln_b'] = dlng.reshape(-1), dlnb.reshape(-1)
    g['rwkv_r_k'] = drk.reshape(RWKV_HEADS, HEAD_DIM)
    (dys, dxs_skip, dz), (ddsk, dng) = tok_bwd('ssd_post_bwd', f_ssd_post, sp_x, sp_p, [[dy_b]], tm, n, t)
    g['ssm_d'], g['ssm_norm_g'] = ddsk[0, :SSM_HEADS], dng.reshape(-1)
    dxbc_act, ddtr, ddtb, dalog = ssd_bwd(xbc, p0, dt_col, dtb, alog, sstates, dys.reshape(bsz, t, SSM_DIM),
                                          dxs_skip.reshape(bsz, t, SSM_DIM))
    g['ssm_dt_bias'], g['ssm_a_log'] = ddtb[0, :SSM_HEADS], dalog[0, :SSM_HEADS]
    dxbc, dscw, dscb = dwconv_bwd('ssm_conv_bwd', p0, ssm_map, w['ssm_conv_w'], _row(a['ssm_conv_b']), 4, True,
                                  dxbc_act)
    g['ssm_conv_w'], g['ssm_conv_b'] = dscw, dscb.reshape(-1)
    if comm is None:
        dscan = rwkv_scan_bwd(*scan_in, rstates, dy_r)
    else:
        outs = rwkv_scan_bwd(*scan_in, rstates, dy_r, ride=comm.grad_ride('b', g))
        dscan, parts['b'] = outs[:6], outs[6:]
    dr2, dlw, dk2, dv2, dal, dbe = dscan
    pre_ct = [[dr1, dr2], [dv1, dv2], [dlw], [dkm1, dk2], [dal], [dbe], [dgt]]
    dpre_x, dpre_p = tok_bwd('rwkv_pre_bwd', f_rwkv_pre, pre_x, pre_p, pre_ct, tm, n, t)
    g['rwkv_w0'], g['rwkv_a0'] = dpre_p[0].reshape(-1), dpre_p[2].reshape(-1)
    g['rwkv_w2'], g['rwkv_a2'], g['rwkv_g2'] = dpre_p[1][:64], dpre_p[3][64:], dpre_p[4]
    g['rwkv_k_k'], g['rwkv_k_a'] = dpre_p[5].reshape(-1), dpre_p[6].reshape(-1)
    dxs_r = jnp.concatenate(dpre_x, axis=1).reshape(bsz, t, RWKV_COLS)
    d_rw, dtaps, _ = dwconv_bwd('rwkv_shift_bwd', p0, rw_map, taps, zero_b, 2, False, dxs_r)
    d_rw = d_rw.reshape(n, RWKV_COLS)
    g['rwkv_mix'] = dtaps[0] - dtaps[1]
    lora_cols = RWKV_COLS - RKV_COLS
    dproj0 = tok_fwd('l0_dproj', f_concat, [(d_rw, RKV_COLS, 0), (dz, SSM_DIM, 0),
                                            (dxbc.reshape(n, 2 * SSM_DIM), 2 * SSM_DIM, 0),
                                            (d_rw, lora_cols, RKV_COLS // lora_cols), (ddtr.reshape(n, LANES), LANES, 0)],
                     [], [L0_PAD], tm, n, t, out_dtype=BF16)[0]
    dwin0 = mm(x2, dproj0, 'tn', 'l0_dwin')
    g['l0_w_in'] = jnp.concatenate([dwin0[:, :RKV_COLS], dwin0[:, P0_LORA:P0_DT], dwin0[:, P0_Z:P0_LORA],
                                    dwin0[:, P0_DT:P0_DT + SSM_HEADS]], axis=1)
    if comm is None:
        dx = mm(dproj0, win0, 'nt', 'l0_dx', add=dx_res)
    else:
        dx, *parts['c'] = mm(dproj0, win0, 'nt', 'l0_dx', add=dx_res, ride=comm.grad_ride('c', g))
    return loss, dx.reshape(bsz, t, d), g, parts


GRAD_GROUPS = {
    'a': ['ffn1_w_up', 'ffn1_conv_w', 'ffn1_conv_b', 'ffn1_w_down', 'l1_ln2_g', 'l1_ln2_b'],
    'c': ['l0_w_in', 'rwkv_mix', 'rwkv_w0', 'rwkv_w2', 'rwkv_a0', 'rwkv_a2', 'rwkv_g2', 'rwkv_k_k', 'rwkv_k_a'],
}
GRAD_GROUPS['b'] = [nm for nm in WEIGHTS if nm not in GRAD_GROUPS['a'] + GRAD_GROUPS['c']]
FIRST_GATHER = ['l0_w_in', 'rwkv_w2', 'rwkv_a2', 'rwkv_g2'] + CONV_W
MID_GATHER = ['l0_w_out', 'ffn0_w_up', 'ffn0_w_down', 'l1_w_in', 'mla_w_uq', 'mla_w_ukv', 'l1_w_out']
FFN1_GATHER = ['ffn1_w_up', 'ffn1_w_down']
BF16_ROWS = 16


def _name_kinds(names):
    return ([nm for nm in names if nm in MATMUL_W], [nm for nm in names if nm in SHARD_AXIS and nm not in MATMUL_W],
            [nm for nm in names if nm not in SHARD_AXIS])


class _Comm:
    def __init__(self, a):
        self.a = a

    def weights_ride(self, names):
        big, conv, _ = _name_kinds(names)
        srcs = [self.a[nm].astype(BF16) for nm in big]
        if conv:
            srcs.append(_flat_rows([lax.bitcast_convert_type(self.a[nm], BF16) for nm in conv], BF16_ROWS, BF16_ROWS))
        return srcs, True

    def weights(self, names, gots):
        big, conv, _ = _name_kinds(names)
        out = {}
        for nm, got in zip(big, gots):
            out[nm] = (got.reshape(-1, got.shape[2]) if SHARD_AXIS[nm] == 0
                       else jnp.concatenate([got[k] for k in range(N_DEV)], axis=1))
        if conv:
            shapes = [self.a[nm].shape + (2,) for nm in conv]
            blocks = [_unflatten(gots[-1][k], shapes, BF16_ROWS) for k in range(N_DEV)]
            for i, nm in enumerate(conv):
                out[nm] = jnp.concatenate([lax.bitcast_convert_type(blocks[k][i], F32) for k in range(N_DEV)], axis=1)
        return out

    def first_weights(self):
        return self.weights(FIRST_GATHER, peer_exchange('gather_first_weights', self.weights_ride(FIRST_GATHER)[0], True))

    def grad_ride(self, group, g):
        def shard_of(nm, k):
            gv = g[nm]
            per = N_DEV
            if isinstance(gv, tuple):
                gv, k, per = gv[k // 4], k % 4, 4
            width = gv.shape[SHARD_AXIS[nm]] // per
            return lax.slice_in_dim(gv, k * width, (k + 1) * width, axis=SHARD_AXIS[nm])

        big, conv, small = _name_kinds(GRAD_GROUPS[group])
        srcs = [jnp.stack([shard_of(nm, k) for k in range(N_DEV)]) for nm in big]
        srcs.append(jnp.stack([_flat_rows([shard_of(nm, k) for nm in conv] + [[g[nm] for nm in small]],
                                          SUBLANES, ADAM_ROWS) for k in range(N_DEV)]))
        return srcs, False


def _step(a):
    comm = _Comm(a)
    loss, dx, _, parts = _local_step(a, comm.first_weights(), comm)
    loss = lax.psum(loss, ('x', 'y', 'c'))
    res = {}
    for group, names in GRAD_GROUPS.items():
        big, conv, small = _name_kinds(names)
        for nm, got in zip(big, parts[group]):
            res[nm] = adamw_sum(f'adamw_{nm}', got, a[nm], a['m_' + nm], a['v_' + nm])
        flat = lambda prefix: _flat_rows([a[prefix + nm] for nm in conv] + [[a[prefix + nm] for nm in small]],
                                         SUBLANES, ADAM_ROWS)
        outs = adamw_sum(f'adamw_{group}', parts[group][-1], flat(''), flat('m_'), flat('v_'))
        shapes = [a[nm].shape for nm in conv] + [[a[nm].shape for nm in small]]
        per_out = [_unflatten(o, shapes, SUBLANES) for o in outs]
        for i, nm in enumerate(conv):
            res[nm] = [per_out[j][i] for j in range(4)]
        for i, nm in enumerate(small):
            res[nm] = [per_out[j][-1][i] for j in range(4)]
    return (loss, dx, *[res[nm][j] for j in range(4) for nm in WEIGHTS])


def kernel(x, positions, l0_w_in, rwkv_mix, rwkv_w0, rwkv_w2, rwkv_a0, rwkv_a2, rwkv_g2, rwkv_k_k, rwkv_k_a, rwkv_r_k, rwkv_ln_g, rwkv_ln_b, ssm_conv_w, ssm_conv_b, ssm_dt_bias, ssm_a_log, ssm_d, ssm_norm_g, l0_w_out, l0_ln1_g, l0_ln1_b, ffn0_w_up, ffn0_conv_w, ffn0_conv_b, ffn0_w_down, l0_ln2_g, l0_ln2_b, l1_w_in, mla_q_norm_g, mla_w_uq, mla_kv_norm_g, mla_w_ukv, l1_w_out, l1_ln1_g, l1_ln1_b, ffn1_w_up, ffn1_conv_w, ffn1_conv_b, ffn1_w_down, l1_ln2_g, l1_ln2_b, loss_target, m_l0_w_in, m_rwkv_mix, m_rwkv_w0, m_rwkv_w2, m_rwkv_a0, m_rwkv_a2, m_rwkv_g2, m_rwkv_k_k, m_rwkv_k_a, m_rwkv_r_k, m_rwkv_ln_g, m_rwkv_ln_b, m_ssm_conv_w, m_ssm_conv_b, m_ssm_dt_bias, m_ssm_a_log, m_ssm_d, m_ssm_norm_g, m_l0_w_out, m_l0_ln1_g, m_l0_ln1_b, m_ffn0_w_up, m_ffn0_conv_w, m_ffn0_conv_b, m_ffn0_w_down, m_l0_ln2_g, m_l0_ln2_b, m_l1_w_in, m_mla_q_norm_g, m_mla_w_uq, m_mla_kv_norm_g, m_mla_w_ukv, m_l1_w_out, m_l1_ln1_g, m_l1_ln1_b, m_ffn1_w_up, m_ffn1_conv_w, m_ffn1_conv_b, m_ffn1_w_down, m_l1_ln2_g, m_l1_ln2_b, v_l0_w_in, v_rwkv_mix, v_rwkv_w0, v_rwkv_w2, v_rwkv_a0, v_rwkv_a2, v_rwkv_g2, v_rwkv_k_k, v_rwkv_k_a, v_rwkv_r_k, v_rwkv_ln_g, v_rwkv_ln_b, v_ssm_conv_w, v_ssm_conv_b, v_ssm_dt_bias, v_ssm_a_log, v_ssm_d, v_ssm_norm_g, v_l0_w_out, v_l0_ln1_g, v_l0_ln1_b, v_ffn0_w_up, v_ffn0_conv_w, v_ffn0_conv_b, v_ffn0_w_down, v_l0_ln2_g, v_l0_ln2_b, v_l1_w_in, v_mla_q_norm_g, v_mla_w_uq, v_mla_kv_norm_g, v_mla_w_ukv, v_l1_w_out, v_l1_ln1_g, v_l1_ln1_b, v_ffn1_w_up, v_ffn1_conv_w, v_ffn1_conv_b, v_ffn1_w_down, v_l1_ln2_g, v_l1_ln2_b):
    return _step(dict(locals()))
```

```python
import functools
import math

import jax
import jax.numpy as jnp
from jax import lax
from jax.experimental import pallas as pl
from jax.experimental.pallas import tpu as pltpu

F32 = jnp.float32
BF16 = jnp.bfloat16
HI = lax.Precision.HIGHEST

V7X_VMEM_BYTES = 64 * 1024 * 1024
VMEM_LIMIT = V7X_VMEM_BYTES - 8 * 1024 * 1024
LANES = 128
SUBLANES = 8
N_DEV = 8

D_MODEL = 1024
HEAD_DIM = 64
RWKV_DIM = 512
RWKV_HEADS = 8
RWKV_GN_EPS = 64e-5
RWKV_CHUNK = 64
SSM_DIM = 512
SSM_HEADS = 8
SSM_CHUNK = 128
SSM_STATE = 128
Q_BLOCK = 128
SB_HEADS_PER_STEP = 4
MLA_HEADS_PER_STEP = 4
ATTN_FWD_HEADS_PER_STEP = 8
MLA_NOPE = 64
MLA_ROPE = 32
ROPE_THETA = 10000.0
D_FF = 2816
DEPTH = 2
ALPHA = (2 * DEPTH) ** 0.25
RKV_COLS = 3 * RWKV_DIM
RWKV_COLS = RKV_COLS + 64 + 64 + 128
L0_COLS = RWKV_COLS + SSM_DIM + 2 * SSM_DIM + SSM_HEADS
P0_Z = RKV_COLS
P0_LORA = P0_Z + 3 * SSM_DIM
P0_DT = P0_LORA + (RWKV_COLS - RKV_COLS)
L0_PAD = P0_DT + LANES
L1_COLS = 3 * RWKV_DIM + 256 + 128 + MLA_ROPE
P1_CQ, P1_CKV, P1_KPE = 3 * RWKV_DIM, 3 * RWKV_DIM + 256, 3 * RWKV_DIM + 256 + 128
L1_PAD = P1_KPE + LANES

ADAM_LR = 0.001
ADAM_B1 = 0.9
ADAM_B2 = 0.999
ADAM_EPS = 1e-08
ADAM_WD = 0.01
ADAM_STEP = 10

NEG_BIG = -1e30


def _params(sem=None):
    return pltpu.CompilerParams(dimension_semantics=sem, vmem_limit_bytes=VMEM_LIMIT)


P_F32, P_BF16, P_BF16X3 = 0, 1, 2


def _dg_raw(a, b, ca, cb, fast):
    dims = (((ca,), (cb,)), ((), ()))
    if fast == P_BF16:
        return lax.dot_general(a.astype(BF16), b.astype(BF16), dims, preferred_element_type=F32)
    prec = HI if fast == P_F32 else lax.Precision.HIGH
    return lax.dot_general(a, b, dims, precision=prec, preferred_element_type=F32)


@functools.partial(jax.custom_vjp, nondiff_argnums=(2, 3, 4))
def dg(a, b, ca, cb, fast):
    return _dg_raw(a, b, ca, cb, fast)


def _dg_fwd(a, b, ca, cb, fast):
    return _dg_raw(a, b, ca, cb, fast), (a, b)


def _dg_bwd(ca, cb, fast, res, ct):
    a, b = res
    fa, fb = 1 - ca, 1 - cb
    da = _dg_raw(ct, b, 1, fb, fast) if ca == 1 else _dg_raw(b, ct, fb, 1, fast)
    db = _dg_raw(a, ct, fa, 0, fast) if cb == 0 else _dg_raw(ct, a, 0, fa, fast)
    return da.astype(a.dtype), db.astype(b.dtype)


dg.defvjp(_dg_fwd, _dg_bwd)


def mmb(a, b):
    return dg(a, b, 1, 0, P_BF16)


def mmf(a, b):
    return dg(a, b, 1, 0, P_F32)


def mm3(a, b):
    return dg(a, b, 1, 0, P_BF16X3)


def mm3_nt(a, b):
    return dg(a, b, 1, 1, P_BF16X3)


def mm3_tn(a, b):
    return dg(a, b, 0, 0, P_BF16X3)


def _split3_dot(x, m01, cb, terms=3):
    parts, rest = [], x
    for i in range(terms):
        parts.append(rest.astype(BF16))
        if i + 1 < terms:
            rest = rest - parts[-1].astype(F32)
    rows = x.shape[0]
    out = lax.dot_general(jnp.concatenate(parts, axis=0), m01.astype(BF16), (((1,), (cb,)), ((), ())),
                          preferred_element_type=F32)
    return functools.reduce(lambda a_, b_: a_ + b_, [out[i * rows:(i + 1) * rows] for i in range(terms)])


def _lower_ones(n):
    return jnp.where(_iota((n, n), 0) >= _iota((n, n), 1), 1.0, 0.0)


SUFFIX_TERMS = 2


@jax.custom_vjp
def suffix_sum(x):
    return _split3_dot(x, _lower_ones(x.shape[1]), 0, SUFFIX_TERMS)


def _suffix_sum_fwd(x):
    return suffix_sum(x), None


def _suffix_sum_bwd(_, ct):
    return (_split3_dot(ct, _lower_ones(ct.shape[1]), 1, SUFFIX_TERMS),)


suffix_sum.defvjp(_suffix_sum_fwd, _suffix_sum_bwd)


def _iota(shape, dim):
    return lax.broadcasted_iota(jnp.int32, shape, dim)


def _softplus(x):
    return jnp.maximum(x, 0.0) + jnp.log1p(jnp.exp(-jnp.abs(x)))


def _silu(x):
    return x * jax.nn.sigmoid(x)


def _largest_tile(n, cap, mult):
    best = None
    t = mult
    while t <= min(n, cap):
        if n % t == 0:
            best = t
        t += mult
    return n if best is None else best


MM_VMEM_BUDGET = 40 * 1024 * 1024
V7X_HBM_BYTES_PER_S = 3.2e12
GRID_STEP_S = 0.35e-6


def _mm_tiles(M, N, K, a_bytes, b_bytes, has_add):
    def divs(n):
        return [d for d in range(LANES, n + 1, LANES) if n % d == 0] or [n]

    best = None
    for tm in divs(M):
        for tn in divs(N):
            if tm * tn * 4 > 12 * 1024 * 1024:
                continue
            for tk in divs(K):
                vmem = (2 * (tm * tk * a_bytes + tk * tn * b_bytes) + 2 * tm * tn * 4 * (2 if has_add else 1)
                        + (tm * tk + tk * tn) * 2 + tm * tn * 4)
                if vmem > MM_VMEM_BUDGET:
                    continue
                ni, nj, nk = M // tm, N // tn, K // tk
                a_reads = M * K * a_bytes * (1 if nk == 1 else nj)
                b_reads = K * N * b_bytes * (1 if (nk == 1 and nj == 1) else ni)
                traffic = a_reads + b_reads + M * N * 4 * (2 if has_add else 1)
                cost = traffic / V7X_HBM_BYTES_PER_S + ni * nj * nk * GRID_STEP_S
                if min(tm, tn, tk) < 256 and min(M, N, K) >= 256:
                    cost *= 1.5
                if best is None or cost < best[0]:
                    best = (cost, tm, tn, tk)
    return best[1:]


def mm(a, b, mode, name, add=None, b_rows=None, b_cols=None, ride=None):
    r0, nr = b_rows or (0, b.shape[0])
    c0, nc = b_cols or (0, b.shape[1])
    if mode == "nn":
        (M, K), N = a.shape, nc
        assert nr == K
    elif mode == "nt":
        (M, K), N = a.shape, nr
        assert nc == K
    else:
        (K, M), N = a.shape, b.shape[1]
        assert b_rows is None and b_cols is None
    has_add = add is not None
    tm, tn, tk = _mm_tiles(M, N, K, a.dtype.itemsize, b.dtype.itemsize, has_add)
    nk = K // tk
    keep_a = nk == 1 and N // tn > 1 and a.dtype != BF16
    if mode == "nn":
        assert r0 % tk == 0 and c0 % tn == 0
        a_spec = pl.BlockSpec((tm, tk), lambda i, j, k: (i, k))
        b_spec = pl.BlockSpec((tk, tn), lambda i, j, k: (k + r0 // tk, j + c0 // tn))
        dims = (((1,), (0,)), ((), ()))
    elif mode == "nt":
        assert r0 % tn == 0 and c0 % tk == 0
        a_spec = pl.BlockSpec((tm, tk), lambda i, j, k: (i, k))
        b_spec = pl.BlockSpec((tn, tk), lambda i, j, k: (j + r0 // tn, k + c0 // tk))
        dims = (((1,), (1,)), ((), ()))
    else:
        a_spec = pl.BlockSpec((tk, tm), lambda i, j, k: (k, i))
        b_spec = pl.BlockSpec((tk, tn), lambda i, j, k: (k, j))
        dims = (((0,), (0,)), ((), ()))
    o_spec = pl.BlockSpec((tm, tn), lambda i, j, k: (i, j))
    grid = (M // tm, N // tn, nk)
    r_in, r_specs, r_out, r_ospecs, r_scr = _ride_args(ride)
    n_add, n_ride = int(has_add), len(r_in)

    def body(a_ref, b_ref, *rest):
        o_ref = rest[n_add + n_ride]
        if ride is not None:
            first, last = _grid_first_last(grid)
            copies = _ride_start(rest[n_add:n_add + n_ride], rest[n_add + n_ride + 1:n_add + 2 * n_ride + 1],
                                 rest[-3:], ride[1], first)
        k = pl.program_id(2)
        if keep_a:
            a_bf = rest[n_add + 2 * n_ride + 1]

            @pl.when(pl.program_id(1) == 0)
            def _():
                a_bf[...] = a_ref[...].astype(BF16)

            av = a_bf[...]
        else:
            av = a_ref[...].astype(BF16)
        part = lax.dot_general(av, b_ref[...].astype(BF16), dims, preferred_element_type=F32)

        @pl.when(k == 0)
        def _():
            o_ref[...] = part + rest[0][...] if has_add else part

        @pl.when(k > 0)
        def _():
            o_ref[...] += part

        if ride is not None:
            _ride_wait(copies, last)

    ins = [a, b] + ([add] if has_add else []) + r_in
    specs = [a_spec, b_spec] + ([o_spec] if has_add else []) + r_specs
    outs = pl.pallas_call(
        body, name=name, grid=grid, in_specs=specs, out_specs=[o_spec] + r_ospecs,
        out_shape=[jax.ShapeDtypeStruct((M, N), F32)] + r_out,
        scratch_shapes=([pltpu.VMEM(a_spec.block_shape, BF16)] if keep_a else []) + r_scr,
        compiler_params=_params(("arbitrary" if ride is not None else "parallel", "arbitrary", "arbitrary")),
    )(*ins)
    return outs[0] if ride is None else outs


def _is_heads(x):
    return not isinstance(x, tuple)


def _tok_arr(x):
    return x if _is_heads(x) else x[0]


def _tok_width(x):
    return x.shape[1] * x.shape[3] if _is_heads(x) else x[1]


def _heads_spec(h, dh, tm, tiles_per_seq):
    return pl.BlockSpec((None, h, tm, dh), lambda i: (i // tiles_per_seq, 0, i % tiles_per_seq, 0))


def _x_spec(x, tm, tiles_per_seq):
    if _is_heads(x):
        return _heads_spec(x.shape[1], x.shape[3], tm, tiles_per_seq)
    return pl.BlockSpec((tm, x[1]), functools.partial(lambda i, cb: (i, cb), cb=x[2]))


def _out_spec_shape(layout, n, seq, tm, dtype=F32):
    if isinstance(layout, tuple):
        h, dh = layout
        return _heads_spec(h, dh, tm, seq // tm), jax.ShapeDtypeStruct((n // seq, h, seq, dh), dtype)
    return pl.BlockSpec((tm, layout), lambda i: (i, 0)), jax.ShapeDtypeStruct((n, layout), dtype)


def _tok_load(ref):
    if len(ref.shape) == 3:
        return jnp.concatenate([ref[hh] for hh in range(ref.shape[0])], axis=1)
    return ref[...]


def _tok_store(ref, val):
    if len(ref.shape) == 3:
        dh = ref.shape[2]
        for hh in range(ref.shape[0]):
            ref[hh] = val[:, hh * dh:(hh + 1) * dh]
    else:
        ref[...] = val


def _p_specs(ps):
    return [pl.BlockSpec(p.shape, lambda i: (0, 0)) for p in ps]


def tok_fwd(name, f, xs, ps, out_layouts, tm, n, seq, out_dtype=F32):
    nx, npar = len(xs), len(ps)
    outs = [_out_spec_shape(lay, n, seq, tm, out_dtype) for lay in out_layouts]

    def body(*refs):
        xv = [_tok_load(r) for r in refs[:nx]]
        pv = [r[...].astype(F32) for r in refs[nx:nx + npar]]
        for o, r in zip(f(*xv, *pv), refs[nx + npar:]):
            _tok_store(r, o.astype(out_dtype))

    return pl.pallas_call(
        body, name=name, grid=(n // tm,),
        in_specs=[_x_spec(x, tm, seq // tm) for x in xs] + _p_specs(ps),
        out_specs=[o[0] for o in outs], out_shape=[o[1] for o in outs],
        compiler_params=_params(("parallel",)),
    )(*[_tok_arr(x) for x in xs], *ps)


def tok_bwd(name, f, xs, ps, cts, tm, n, seq, dx_layouts=None):
    nx, npar = len(xs), len(ps)
    ct_flat = [c for group in cts for c in group]
    nct = len(ct_flat)
    dx_layouts = dx_layouts or [None] * nx
    dxs = [_out_spec_shape(lay if lay else _tok_width(x), n, seq, tm) for x, lay in zip(xs, dx_layouts)]

    def body(*refs):
        xv = [_tok_load(r) for r in refs[:nx]]
        pv = [r[...].astype(F32) for r in refs[nx:nx + npar]]
        ct_refs = refs[nx + npar:nx + npar + nct]
        dx_refs = refs[nx + npar + nct:nx + npar + nct + nx]
        dp_refs = refs[nx + npar + nct + nx:]
        cv, pos = [], 0
        for group in cts:
            acc = _tok_load(ct_refs[pos])
            for r in ct_refs[pos + 1:pos + len(group)]:
                acc = acc + _tok_load(r)
            cv.append(acc)
            pos += len(group)
        _, vjp = jax.vjp(f, *xv, *pv)
        grads = vjp(tuple(cv))
        for g, r in zip(grads[:nx], dx_refs):
            _tok_store(r, g)

        @pl.when(pl.program_id(0) == 0)
        def _():
            for r in dp_refs:
                r[...] = jnp.zeros_like(r)

        for g, r in zip(grads[nx:], dp_refs):
            r[...] += g

    ct_specs = [_heads_spec(c.shape[1], c.shape[3], tm, seq // tm) if c.ndim == 4
                else pl.BlockSpec((tm, c.shape[1]), lambda i: (i, 0)) for c in ct_flat]
    outs = pl.pallas_call(
        body, name=name, grid=(n // tm,),
        in_specs=[_x_spec(x, tm, seq // tm) for x in xs] + _p_specs(ps) + ct_specs,
        out_specs=[d[0] for d in dxs] + _p_specs(ps),
        out_shape=[d[1] for d in dxs] + [jax.ShapeDtypeStruct(p.shape, F32) for p in ps],
        compiler_params=_params(("arbitrary",)),
    )(*[_tok_arr(x) for x in xs], *ps, *ct_flat)
    return outs[:nx], outs[nx:]


def f_ln(h, y, g, b):
    pre = ALPHA * h + y
    mu = jnp.mean(pre, axis=-1, keepdims=True)
    xc = pre - mu
    var = jnp.mean(xc * xc, axis=-1, keepdims=True)
    return (xc * lax.rsqrt(var + 1e-5) * g + b,)


def _head_sel(width, nheads_pad, per):
    return jnp.where(_iota((width, nheads_pad), 0) // per == _iota((width, nheads_pad), 1), 1.0, 0.0).astype(F32)


def _head_sel_t(nheads_pad, width, per):
    return jnp.where(_iota((nheads_pad, width), 1) // per == _iota((nheads_pad, width), 0), 1.0, 0.0).astype(F32)


@jax.custom_vjp
def head_sum(x):
    return _split3_dot(x, _head_sel(RWKV_DIM, LANES, HEAD_DIM), 0)


@jax.custom_vjp
def head_spread(y):
    return _split3_dot(y, _head_sel(RWKV_DIM, LANES, HEAD_DIM), 1)


head_sum.defvjp(lambda x: (head_sum(x), None), lambda _, ct: (head_spread(ct),))
head_spread.defvjp(lambda y: (head_spread(y), None), lambda _, ct: (head_sum(ct),))


def f_rwkv_pre(r, k, v, lora, glo, w0, w2p, a0, a2p, g2, k_k, k_a):
    lane = _iota(lora.shape, 1)
    tw = jnp.where(lane < 64, jnp.tanh(lora), 0.0)
    ta = jnp.where(lane >= 64, lora, 0.0)
    log_w = -_softplus(-(w0 + mmb(tw, w2p))) - 0.5
    lw = -jnp.exp(log_w)
    a = jax.nn.sigmoid(a0 + mmb(ta, a2p))
    g = mmb(jax.nn.sigmoid(glo), g2)
    kk = k * k_k
    nrm = jnp.sqrt(jnp.maximum(head_sum(kk * kk), 1e-24))
    kkn = kk * head_spread(1.0 / nrm)
    kmod = k * (1.0 + (a - 1.0) * k_a)
    return r, v, lw, kmod, -kkn, kkn * a, g


def f_rwkv_post(y, r, kmod, v, g, ln_g, ln_b, r_k):
    inv = 1.0 / HEAD_DIM
    mu = head_spread(head_sum(y) * inv)
    yc = y - mu
    var = head_sum(yc * yc) * inv
    rstd = head_spread(lax.rsqrt(var + RWKV_GN_EPS))
    yn = yc * rstd * ln_g + ln_b
    bonus = head_spread(head_sum(r * kmod * r_k)) * v
    return ((yn + bonus) * g,)


def f_ssd_post(y, xs, z, d_skip, norm_g):
    sel_t = _head_sel_t(LANES, SSM_DIM, HEAD_DIM)
    d_e = jnp.sum(mmf(jnp.broadcast_to(d_skip, (SUBLANES, LANES)), sel_t), axis=0, keepdims=True) * (1.0 / SUBLANES)
    u = (y + xs * d_e) * _silu(z)
    first = _iota(u.shape, 1) < (SSM_DIM // 2)
    uu = u * u
    inv = 2.0 / SSM_DIM
    ms0 = jnp.sum(jnp.where(first, uu, 0.0), axis=-1, keepdims=True) * inv
    ms1 = jnp.sum(jnp.where(first, 0.0, uu), axis=-1, keepdims=True) * inv
    ms = jnp.where(first, ms0, ms1)
    return (u * lax.rsqrt(ms + 1e-5) * norm_g,)


def f_mla_pre(cq, ckv, qg, wq_nope, wq_rope, kvg, wk_nope, wv):
    def rms(x, g):
        return x * lax.rsqrt(jnp.mean(x * x, axis=-1, keepdims=True) + 1e-6) * g
    q_in, kv_in = rms(cq, qg), rms(ckv, kvg)
    return mmb(q_in, wq_nope), mmb(q_in, wq_rope), mmb(kv_in, wk_nope), mmb(kv_in, wv)


def f_same(*xs):
    return xs


def f_concat(*xs):
    return (jnp.concatenate(xs, axis=1),)


CONV_TILE = 256


def _shift_down(x, s, row):
    return x if s == 0 else jnp.where(row >= s, pltpu.roll(x, s, 0), 0.0)


def _shift_up(x, s, row, t):
    return x if s == 0 else jnp.where(row < t - s, pltpu.roll(x, t - s, 0), 0.0)


def dwconv_fwd(name, u, colmap, w, b, taps, silu, upmap=None, out_dtype=F32):
    bsz, t, _ = u.shape
    c = w.shape[1]
    tc = CONV_TILE
    has_up = upmap is not None

    def body(*refs):
        u_ref, w_ref, b_ref = refs[:3]
        o_ref = refs[-1]
        uv = u_ref[...]
        wv = w_ref[...]
        row = _iota(uv.shape, 0)
        acc = jnp.broadcast_to(b_ref[...], uv.shape)
        for i in range(taps):
            acc = acc + wv[i:i + 1, :] * _shift_down(uv, taps - 1 - i, row)
        if silu:
            acc = _silu(acc)
        if has_up:
            acc = acc * refs[3][...]
        o_ref[...] = acc.astype(out_dtype)

    specs = [pl.BlockSpec((None, t, tc), lambda bb, j: (bb, 0, colmap(j))),
             pl.BlockSpec((taps, tc), lambda bb, j: (0, j)),
             pl.BlockSpec((1, tc), lambda bb, j: (0, j))]
    ins = [u, w, b]
    if has_up:
        specs.append(pl.BlockSpec((None, t, tc), lambda bb, j: (bb, 0, upmap(j))))
        ins.append(u)
    return pl.pallas_call(
        body, name=name, grid=(bsz, c // tc), in_specs=specs,
        out_specs=pl.BlockSpec((None, t, tc), lambda bb, j: (bb, 0, j)),
        out_shape=jax.ShapeDtypeStruct((bsz, t, c), out_dtype),
        compiler_params=_params(("parallel", "parallel")),
    )(*ins)


def dwconv_bwd(name, u, colmap, w, b, taps, silu, dout, upmap=None, grad_dtype=F32):
    bsz, t, _ = u.shape
    c = w.shape[1]
    tc = CONV_TILE
    has_up = upmap is not None

    def body(*refs):
        u_ref, w_ref, b_ref, d_ref = refs[:4]
        nin = 5 if has_up else 4
        du_ref, dw_ref, db_ref = refs[nin:nin + 3]
        uv = u_ref[...]
        wv = w_ref[...]
        dv = d_ref[...]
        row = _iota(uv.shape, 0)
        shifted = [_shift_down(uv, taps - 1 - i, row) for i in range(taps)]
        cg = jnp.broadcast_to(b_ref[...], uv.shape)
        for i in range(taps):
            cg = cg + wv[i:i + 1, :] * shifted[i]
        if silu:
            sg = jax.nn.sigmoid(cg)
            act = cg * sg
            dact_dcg = sg * (1.0 + cg * (1.0 - sg))
        else:
            act = cg
            dact_dcg = None
        if has_up:
            refs[nin + 3][...] = (dv * act).astype(grad_dtype)
            dv = dv * refs[4][...]
        dcg = dv * dact_dcg if silu else dv
        du = jnp.zeros_like(uv)
        for i in range(taps):
            du = du + wv[i:i + 1, :] * _shift_up(dcg, taps - 1 - i, row, t)
        du_ref[...] = du.astype(grad_dtype)

        @pl.when(pl.program_id(1) == 0)
        def _():
            dw_ref[...] = jnp.zeros_like(dw_ref)
            db_ref[...] = jnp.zeros_like(db_ref)

        for i in range(taps):
            dw_ref[i:i + 1, :] += jnp.sum(dcg * shifted[i], axis=0, keepdims=True)
        db_ref[...] += jnp.sum(dcg, axis=0, keepdims=True)

    specs = [pl.BlockSpec((None, t, tc), lambda j, bb: (bb, 0, colmap(j))),
             pl.BlockSpec((taps, tc), lambda j, bb: (0, j)),
             pl.BlockSpec((1, tc), lambda j, bb: (0, j)),
             pl.BlockSpec((None, t, tc), lambda j, bb: (bb, 0, j))]
    ins = [u, w, b, dout]
    if has_up:
        specs.append(pl.BlockSpec((None, t, tc), lambda j, bb: (bb, 0, upmap(j))))
        ins.append(u)
    big = pl.BlockSpec((None, t, tc), lambda j, bb: (bb, 0, j))
    out_specs = [big, pl.BlockSpec((taps, tc), lambda j, bb: (0, j)), pl.BlockSpec((1, tc), lambda j, bb: (0, j))]
    out_shape = [jax.ShapeDtypeStruct((bsz, t, c), grad_dtype), jax.ShapeDtypeStruct((taps, c), F32),
                 jax.ShapeDtypeStruct((1, c), F32)]
    if has_up:
        out_specs.append(big)
        out_shape.append(jax.ShapeDtypeStruct((bsz, t, c), grad_dtype))
    return pl.pallas_call(
        body, name=name, grid=(c // tc, bsz), in_specs=specs, out_specs=out_specs, out_shape=out_shape,
        compiler_params=_params(("parallel", "arbitrary")),
    )(*ins)


def _each(f, *lists):
    return [f(*xs) for xs in zip(*lists)]


def rwkv_chunk(s0, r, lw, k, v, al, be):
    c = r[0].shape[0]
    ii, jj = _iota((c, c), 0), _iota((c, c), 1)
    incl, strict = ii >= jj, ii > jj
    ones_incl = jnp.where(incl, 1.0, 0.0)
    eye = jnp.where(ii == jj, 1.0, 0.0)
    cum = _each(lambda x: mmf(ones_incl, x), lw)
    gam_inv = _each(lambda x: jnp.exp(-x), cum)
    at = _each(lambda a_, c_, l_: a_ * jnp.exp(c_ - l_), al, cum, lw)
    rt = _each(lambda r_, c_: r_ * jnp.exp(c_), r, cum)
    bt = _each(lambda b_, g_: b_ * g_, be, gam_inv)
    kt = _each(lambda k_, g_: k_ * g_, k, gam_inv)
    a_b = _each(lambda x, y_: jnp.where(strict, mm3_nt(x, y_), 0.0), at, bt)
    a_k = _each(lambda x, y_: jnp.where(strict, mm3_nt(x, y_), 0.0), at, kt)
    rhs0 = _each(mm3_nt, at, s0)
    rhs = _each(lambda x, a_, v_: x + mm3(a_, v_), rhs0, a_k, v)
    p = _each(lambda x: eye + x, a_b)
    m = a_b
    for _ in range(int(math.log2(c)) - 1):
        m = _each(mm3, m, m)
        p = _each(lambda p_, m_: p_ + mm3(p_, m_), p, m)
    u = _each(mm3, p, rhs)
    r_b = _each(lambda x, y_: jnp.where(incl, mm3_nt(x, y_), 0.0), rt, bt)
    r_k = _each(lambda x, y_: jnp.where(incl, mm3_nt(x, y_), 0.0), rt, kt)
    y0 = _each(mm3_nt, rt, s0)
    y1 = _each(lambda y_, b_, u_: y_ + mm3(b_, u_), y0, r_b, u)
    y = _each(lambda y_, k_, v_: y_ + mm3(k_, v_), y1, r_k, v)
    su = _each(mm3_tn, u, bt)
    sv = _each(mm3_tn, v, kt)
    s1 = _each(lambda s_, a_, b_, l_: (s_ + a_ + b_) * jnp.exp(jnp.sum(l_, axis=0, keepdims=True)), s0, su, sv, lw)
    return y, s1


def rwkv_scan_fwd(r, lw, k, v, al, be, ride=None):
    bsz, h, t, d = r.shape
    c = RWKV_CHUNK
    nc = t // c
    grid = (bsz, nc)
    r_in, r_specs, r_out, r_ospecs, r_scr = _ride_args(ride)

    def body(*refs):
        r_ref, lw_ref, k_ref, v_ref, al_ref, be_ref = refs[:6]
        y_ref, st_ref = refs[6 + len(r_in):8 + len(r_in)]
        s_scr = refs[8 + 2 * len(r_in)]
        if ride is not None:
            first, last = _grid_first_last(grid)
            copies = _ride_start(refs[6:6 + len(r_in)], refs[8 + len(r_in):8 + 2 * len(r_in)], refs[-3:], ride[1], first)

        @pl.when(pl.program_id(1) == 0)
        def _():
            s_scr[...] = jnp.zeros_like(s_scr)

        heads = lambda ref: [ref[hh] for hh in range(h)]
        s0 = heads(s_scr)
        y, s1 = rwkv_chunk(s0, heads(r_ref), heads(lw_ref), heads(k_ref), heads(v_ref), heads(al_ref),
                           heads(be_ref))
        for hh in range(h):
            st_ref[hh] = s0[hh]
            y_ref[hh] = y[hh]
            s_scr[hh] = s1[hh]
        if ride is not None:
            _ride_wait(copies, last)

    seq = pl.BlockSpec((None, h, c, d), lambda b, i: (b, 0, i, 0))
    return pl.pallas_call(
        body, name="rwkv_scan_fwd", grid=grid, in_specs=[seq] * 6 + r_specs,
        out_specs=[seq, pl.BlockSpec((None, h, None, d, d), lambda b, i: (b, 0, i, 0, 0))] + r_ospecs,
        out_shape=[jax.ShapeDtypeStruct((bsz, h, t, d), F32), jax.ShapeDtypeStruct((bsz, h, nc, d, d), F32)] + r_out,
        scratch_shapes=[pltpu.VMEM((h, d, d), F32)] + r_scr,
        compiler_params=_params(("arbitrary", "arbitrary")),
    )(r, lw, k, v, al, be, *r_in)


def rwkv_scan_bwd(r, lw, k, v, al, be, states, dy, ride=None):
    bsz, h, t, d = r.shape
    c = RWKV_CHUNK
    nc = t // c
    grid = (bsz, nc)
    r_in, r_specs, r_out, r_ospecs, r_scr = _ride_args(ride)

    def body(*refs):
        r_ref, lw_ref, k_ref, v_ref, al_ref, be_ref, st_ref, dy_ref = refs[:8]
        nin = 8 + len(r_in)
        dr_ref, dlw_ref, dk_ref, dv_ref, dal_ref, dbe_ref = refs[nin:nin + 6]
        ds_scr = refs[nin + 6 + len(r_in)]
        if ride is not None:
            first, last = _grid_first_last(grid)
            copies = _ride_start(refs[8:nin], refs[nin + 6:nin + 6 + len(r_in)], refs[-3:], ride[1], first)

        @pl.when(pl.program_id(1) == 0)
        def _():
            ds_scr[...] = jnp.zeros_like(ds_scr)

        heads = lambda ref: [ref[hh] for hh in range(h)]
        _, vjp = jax.vjp(rwkv_chunk, heads(st_ref), heads(r_ref), heads(lw_ref), heads(k_ref), heads(v_ref),
                         heads(al_ref), heads(be_ref))
        grads = vjp((heads(dy_ref), heads(ds_scr)))
        for ref, gl in zip((ds_scr, dr_ref, dlw_ref, dk_ref, dv_ref, dal_ref, dbe_ref), grads):
            for hh in range(h):
                ref[hh] = gl[hh]
        if ride is not None:
            _ride_wait(copies, last)

    seq = pl.BlockSpec((None, h, c, d), lambda b, i: (b, 0, nc - 1 - i, 0))
    st = pl.BlockSpec((None, h, None, d, d), lambda b, i: (b, 0, nc - 1 - i, 0, 0))
    return pl.pallas_call(
        body, name="rwkv_scan_bwd", grid=grid, in_specs=[seq] * 6 + [st, seq] + r_specs,
        out_specs=[seq] * 6 + r_ospecs, out_shape=[jax.ShapeDtypeStruct((bsz, h, t, d), F32)] * 6 + r_out,
        scratch_shapes=[pltpu.VMEM((h, d, d), F32)] + r_scr,
        compiler_params=_params(("arbitrary", "arbitrary")),
    )(r, lw, k, v, al, be, states, dy, *r_in)


def ssd_chunk(st, xs, bm, cm, dtr, dt_bias, a_log):
    n = SSM_CHUNK
    ii, jj = _iota((n, n), 0), _iota((n, n), 1)
    incl = ii >= jj
    lane = _iota((n, LANES), 1)
    dt = _softplus(dtr + dt_bias)
    a = dt * (-jnp.exp(a_log))
    acum = mmf(jnp.where(incl, 1.0, 0.0), a)
    last_row = jnp.where(jj == n - 1, 1.0, 0.0)
    cb = [mm3_nt(cm[g], bm[g]) for g in range(2)]
    pairs, heads = range(4), range(SSM_HEADS)
    e_m = [jnp.where(_iota((LANES, LANES), 0) == 2 * m + _iota((LANES, LANES), 1) // HEAD_DIM, 1.0, 0.0)
           for m in pairs]
    dt_m = [mmf(dt, e_m[m]) for m in pairs]
    ac_m = [mmf(acum, e_m[m]) for m in pairs]
    x = [xs[m] * dt_m[m] for m in pairs]
    last_m = [mmf(last_row, ac_m[m]) for m in pairs]
    colb = [mmf(acum, jnp.where(_iota((LANES, n), 0) == h, 1.0, 0.0)) for h in heads]
    decay = [jnp.exp(jnp.where(incl, colb[h] - colb[h].T, NEG_BIG)) for h in heads]
    yh = [mm3(cb[h // 4] * decay[h], x[h // 2]) for h in heads]
    y_off = [mm3(cm[m // 2], st[m]) for m in pairs]
    ys = [jnp.where(lane // HEAD_DIM == 0, yh[2 * m], yh[2 * m + 1]) + jnp.exp(ac_m[m]) * y_off[m] for m in pairs]
    st_in = [mm3_tn(bm[m // 2], x[m] * jnp.exp(last_m[m] - ac_m[m])) for m in pairs]
    st_new = [jnp.exp(last_m[m]) * st[m] + st_in[m] for m in pairs]
    return tuple(ys), tuple(st_new)


def _ssd_load(xbc_ref, dtr_ref):
    xs = tuple(xbc_ref[:, m * LANES:(m + 1) * LANES] for m in range(4))
    bm = tuple(xbc_ref[:, SSM_DIM + g * LANES:SSM_DIM + (g + 1) * LANES] for g in range(2))
    cm = tuple(xbc_ref[:, SSM_DIM + 2 * LANES + g * LANES:SSM_DIM + 2 * LANES + (g + 1) * LANES] for g in range(2))
    return xs, bm, cm, dtr_ref[...]


def ssd_fwd(xbc, proj, dt_col, dt_bias, a_log):
    bsz, t, _ = xbc.shape
    n = SSM_CHUNK
    nc = t // n

    def body(xbc_ref, dtr_ref, dtb_ref, al_ref, y_ref, st_ref, s_scr):
        @pl.when(pl.program_id(1) == 0)
        def _():
            s_scr[...] = jnp.zeros_like(s_scr)

        st = tuple(s_scr[m] for m in range(4))
        for m in range(4):
            st_ref[m] = st[m]
        xs, bm, cm, dtr = _ssd_load(xbc_ref, dtr_ref)
        ys, st_new = ssd_chunk(st, xs, bm, cm, dtr, dtb_ref[...], al_ref[...])
        for m in range(4):
            y_ref[:, m * LANES:(m + 1) * LANES] = ys[m]
            s_scr[m] = st_new[m]

    vec = pl.BlockSpec((1, LANES), lambda b, i: (0, 0))
    return pl.pallas_call(
        body, name="ssd_fwd", grid=(bsz, nc),
        in_specs=[pl.BlockSpec((None, n, 2 * SSM_DIM), lambda b, i: (b, i, 0)),
                  pl.BlockSpec((None, n, LANES), lambda b, i: (b, i, dt_col)), vec, vec],
        out_specs=[pl.BlockSpec((None, n, SSM_DIM), lambda b, i: (b, i, 0)),
                   pl.BlockSpec((None, None, 4, SSM_STATE, LANES), lambda b, i: (b, i, 0, 0, 0))],
        out_shape=[jax.ShapeDtypeStruct((bsz, t, SSM_DIM), F32),
                   jax.ShapeDtypeStruct((bsz, nc, 4, SSM_STATE, LANES), F32)],
        scratch_shapes=[pltpu.VMEM((4, SSM_STATE, LANES), F32)],
        compiler_params=_params(("parallel", "arbitrary")),
    )(xbc, proj, dt_bias, a_log)


def ssd_bwd(xbc, proj, dt_col, dt_bias, a_log, states, dy, dxs_extra):
    bsz, t, _ = xbc.shape
    n = SSM_CHUNK
    nc = t // n

    def body(xbc_ref, dtr_ref, dtb_ref, al_ref, st_ref, dy_ref, ex_ref,
             dxbc_ref, ddtr_ref, ddtb_ref, dal_ref, ds_scr):
        first = jnp.logical_and(pl.program_id(0) == 0, pl.program_id(1) == 0)

        @pl.when(pl.program_id(1) == 0)
        def _():
            ds_scr[...] = jnp.zeros_like(ds_scr)

        @pl.when(first)
        def _():
            ddtb_ref[...] = jnp.zeros_like(ddtb_ref)
            dal_ref[...] = jnp.zeros_like(dal_ref)

        st = tuple(st_ref[m] for m in range(4))
        xs, bm, cm, dtr = _ssd_load(xbc_ref, dtr_ref)
        _, vjp = jax.vjp(ssd_chunk, st, xs, bm, cm, dtr, dtb_ref[...], al_ref[...])
        dys = tuple(dy_ref[:, m * LANES:(m + 1) * LANES] for m in range(4))
        dst_in = tuple(ds_scr[m] for m in range(4))
        dst, dxs, dbm, dcm, ddtr, ddtb, dal = vjp((dys, dst_in))
        for m in range(4):
            ds_scr[m] = dst[m]
            sl = slice(m * LANES, (m + 1) * LANES)
            dxbc_ref[:, sl] = dxs[m] + ex_ref[:, sl]
        for g in range(2):
            dxbc_ref[:, SSM_DIM + g * LANES:SSM_DIM + (g + 1) * LANES] = dbm[g]
            dxbc_ref[:, SSM_DIM + 2 * LANES + g * LANES:SSM_DIM + 2 * LANES + (g + 1) * LANES] = dcm[g]
        ddtr_ref[...] = ddtr
        ddtb_ref[...] += ddtb
        dal_ref[...] += dal

    vec = pl.BlockSpec((1, LANES), lambda b, i: (0, 0))
    rev = lambda b, i: (b, nc - 1 - i, 0)
    return pl.pallas_call(
        body, name="ssd_bwd", grid=(bsz, nc),
        in_specs=[pl.BlockSpec((None, n, 2 * SSM_DIM), rev),
                  pl.BlockSpec((None, n, LANES), lambda b, i: (b, nc - 1 - i, dt_col)), vec, vec,
                  pl.BlockSpec((None, None, 4, SSM_STATE, LANES), lambda b, i: (b, nc - 1 - i, 0, 0, 0)),
                  pl.BlockSpec((None, n, SSM_DIM), rev), pl.BlockSpec((None, n, SSM_DIM), rev)],
        out_specs=[pl.BlockSpec((None, n, 2 * SSM_DIM), rev), pl.BlockSpec((None, n, LANES), rev), vec, vec],
        out_shape=[jax.ShapeDtypeStruct((bsz, t, 2 * SSM_DIM), F32), jax.ShapeDtypeStruct((bsz, t, LANES), F32),
                   jax.ShapeDtypeStruct((1, LANES), F32), jax.ShapeDtypeStruct((1, LANES), F32)],
        scratch_shapes=[pltpu.VMEM((4, SSM_STATE, LANES), F32)],
        compiler_params=_params(("arbitrary", "arbitrary")),
    )(xbc, proj, dt_bias, a_log, states, dy, dxs_extra)


def sb_block(q, kj, vj, carry, maskf):
    mask = maskf > 0.5
    z = _each(lambda q_, k_: dg(q_, k_, 1, 1, P_BF16) * (HEAD_DIM ** -0.5), q, kj)
    lk = _each(lambda z_: jnp.where(mask, -_softplus(z_), 0.0), z)
    sfx = _each(suffix_sum, lk)
    att = _each(lambda z_, c_, s_: jnp.exp(jnp.where(mask, z_ + c_ + s_, NEG_BIG)), z, carry, sfx)
    out = _each(mmb, att, vj)
    return out, _each(lambda c_, k_: c_ + jnp.sum(k_, axis=1, keepdims=True), carry, lk)


def _sb_mask(qi, j):
    n = Q_BLOCK
    return jnp.where(j * n + _iota((n, n), 1) < qi * n + _iota((n, n), 0), 1.0, 0.0)


def sb_fwd(q, k, v, ride=None):
    bsz, h, t, d = q.shape
    n = Q_BLOCK
    hp = ATTN_FWD_HEADS_PER_STEP
    grid = (bsz, h // hp, t // n)
    r_in, r_specs, r_out, r_ospecs, r_scr = _ride_args(ride)

    def body(*refs):
        q_ref, k_ref, v_ref = refs[:3]
        o_ref, c_ref = refs[3 + len(r_in):5 + len(r_in)]
        if ride is not None:
            first, last = _grid_first_last(grid)
            copies = _ride_start(refs[3:3 + len(r_in)], refs[5 + len(r_in):5 + 2 * len(r_in)], refs[-3:], ride[1], first)
        qi = pl.program_id(2)
        lane = _iota((n, LANES), 1)

        c_ref[...] = jnp.zeros_like(c_ref)
        o_ref[...] = jnp.zeros_like(o_ref)

        def step(i, carry):
            j = qi - i
            rows = pl.ds(pl.multiple_of(j * n, n), n)
            for hh in range(hp):
                c_ref[hh] = jnp.where(lane == j, carry[hh], c_ref[hh])
            o, carry = sb_block([q_ref[hh] for hh in range(hp)], [k_ref[hh, rows, :] for hh in range(hp)],
                                [v_ref[hh, rows, :] for hh in range(hp)], carry, _sb_mask(qi, j))
            for hh in range(hp):
                o_ref[hh] += o[hh]
            return carry

        lax.fori_loop(0, qi + 1, step, [jnp.zeros((n, 1), F32) for _ in range(hp)])
        if ride is not None:
            _ride_wait(copies, last)

    blk = pl.BlockSpec((None, hp, n, d), lambda b, hg, i: (b, hg, i, 0))
    cblk = pl.BlockSpec((None, hp, n, LANES), lambda b, hg, i: (b, hg, i, 0))
    full = pl.BlockSpec((None, hp, t, d), lambda b, hg, i: (b, hg, 0, 0))
    return pl.pallas_call(
        body, name="sb_fwd", grid=grid, in_specs=[blk, full, full] + r_specs, out_specs=[blk, cblk] + r_ospecs,
        out_shape=[jax.ShapeDtypeStruct((bsz, h, t, d), F32), jax.ShapeDtypeStruct((bsz, h, t, LANES), F32)] + r_out,
        scratch_shapes=r_scr, compiler_params=_params(("arbitrary", "arbitrary", "arbitrary")),
    )(q, k, v, *r_in)


def sb_bwd(q, k, v, kept, do, ride=None):
    bsz, h, t, d = q.shape
    n = Q_BLOCK
    hp = SB_HEADS_PER_STEP
    grid = (bsz, h // hp, t // n)
    r_in, r_specs, r_out, r_ospecs, r_scr = _ride_args(ride)

    def body(*refs):
        q_ref, k_ref, v_ref, c_ref, do_ref = refs[:5]
        nin = 5 + len(r_in)
        dq_ref, dk_ref, dv_ref = refs[nin:nin + 3]
        if ride is not None:
            first, last = _grid_first_last(grid)
            copies = _ride_start(refs[5:nin], refs[nin + 3:nin + 3 + len(r_in)], refs[-3:], ride[1], first)
        qi = pl.program_id(2)

        @pl.when(qi == 0)
        def _():
            dk_ref[...] = jnp.zeros_like(dk_ref)
            dv_ref[...] = jnp.zeros_like(dv_ref)

        heads = range(hp)
        qv = [q_ref[hh] for hh in heads]
        kept_v = [c_ref[hh] for hh in heads]
        lane = _iota((n, LANES), 1)

        dq_ref[...] = jnp.zeros_like(dq_ref)

        def bwd_step(j, dcarry):
            rows = pl.ds(pl.multiple_of(j * n, n), n)
            carry_in = [jnp.sum(jnp.where(lane == j, t_, 0.0), axis=1, keepdims=True) for t_ in kept_v]
            _, vjp = jax.vjp(sb_block, qv, [k_ref[hh, rows, :] for hh in heads],
                             [v_ref[hh, rows, :] for hh in heads], carry_in, _sb_mask(qi, j))
            dqj, dkj, dvj, dc, _ = vjp(([do_ref[hh] for hh in heads], dcarry))
            for hh in heads:
                dq_ref[hh] += dqj[hh]
                dk_ref[hh, rows, :] += dkj[hh]
                dv_ref[hh, rows, :] += dvj[hh]
            return dc

        lax.fori_loop(0, qi + 1, bwd_step, [jnp.zeros((n, 1), F32) for _ in heads])
        if ride is not None:
            _ride_wait(copies, last)

    blk = pl.BlockSpec((None, hp, n, d), lambda b, hg, i: (b, hg, i, 0))
    cblk = pl.BlockSpec((None, hp, n, LANES), lambda b, hg, i: (b, hg, i, 0))
    full = pl.BlockSpec((None, hp, t, d), lambda b, hg, i: (b, hg, 0, 0))
    shp = jax.ShapeDtypeStruct((bsz, h, t, d), F32)
    return pl.pallas_call(
        body, name="sb_bwd", grid=grid, in_specs=[blk, full, full, cblk, blk] + r_specs,
        out_specs=[blk, full, full] + r_ospecs, out_shape=[shp, shp, shp] + r_out,
        scratch_shapes=r_scr, compiler_params=_params(("arbitrary", "arbitrary", "arbitrary")),
    )(q, k, v, kept, do, *r_in)


def _bdot(a, b, ca, cb):
    return _dg_raw(a, b, ca, cb, P_BF16)


def _mla_scores(qn, qp, knj, kpj, qi, j):
    n = Q_BLOCK
    mask = j * n + _iota((n, n), 1) <= qi * n + _iota((n, n), 0)
    scale = (MLA_NOPE + MLA_ROPE) ** -0.5
    return _each(lambda a_, b_, k_: jnp.where(mask, (_bdot(a_, k_, 1, 1) + _bdot(b_, kpj, 1, 1)) * scale, NEG_BIG),
                 qn, qp, knj)


def _mla_specs(t, hp):
    n = Q_BLOCK
    return (pl.BlockSpec((None, hp, n, MLA_NOPE), lambda b, hg, i: (b, hg, i, 0)),
            pl.BlockSpec((None, hp, n, MLA_ROPE), lambda b, hg, i: (b, hg, i, 0)),
            pl.BlockSpec((None, hp, t, MLA_NOPE), lambda b, hg, i: (b, hg, 0, 0)),
            pl.BlockSpec((None, None, t, MLA_ROPE), lambda b, hg, i: (b, 0, 0, 0)),
            pl.BlockSpec((None, hp, n, 1), lambda b, hg, i: (b, hg, i, 0)))


def mla_fwd(qn, qp, kn, kp, v):
    bsz, h, t, _ = qn.shape
    n, hp = Q_BLOCK, ATTN_FWD_HEADS_PER_STEP
    heads = range(hp)

    def body(qn_ref, qp_ref, kn_ref, kp_ref, v_ref, o_ref, lse_ref):
        qi = pl.program_id(2)
        qn_v, qp_v = [qn_ref[hh] for hh in heads], [qp_ref[hh] for hh in heads]

        o_ref[...] = jnp.zeros_like(o_ref)

        def step(j, state):
            m, l = state
            rows = pl.ds(pl.multiple_of(j * n, n), n)
            s = _mla_scores(qn_v, qp_v, [kn_ref[hh, rows, :] for hh in heads], kp_ref[rows, :], qi, j)
            m_new = _each(lambda m_, s_: jnp.maximum(m_, jnp.max(s_, axis=1, keepdims=True)), m, s)
            p = _each(lambda s_, m_: jnp.exp(s_ - m_), s, m_new)
            corr = _each(lambda a_, b_: jnp.exp(a_ - b_), m, m_new)
            l = _each(lambda l_, c_, p_: l_ * c_ + jnp.sum(p_, axis=1, keepdims=True), l, corr, p)
            pv = _each(lambda p_, v_: _bdot(p_, v_, 1, 0), p, [v_ref[hh, rows, :] for hh in heads])
            for hh in heads:
                o_ref[hh] = o_ref[hh] * corr[hh] + pv[hh]
            return m_new, l

        init = ([jnp.full((n, 1), NEG_BIG, F32) for _ in heads], [jnp.zeros((n, 1), F32) for _ in heads])
        m, l = lax.fori_loop(0, qi + 1, step, init)
        for hh in heads:
            o_ref[hh] = o_ref[hh] / l[hh]
            lse_ref[hh] = m[hh] + jnp.log(l[hh])

    qn_s, qp_s, kn_s, kp_s, row_s = _mla_specs(t, hp)
    return pl.pallas_call(
        body, name="mla_fwd", grid=(bsz, h // hp, t // n), in_specs=[qn_s, qp_s, kn_s, kp_s, kn_s],
        out_specs=[qn_s, row_s],
        out_shape=[jax.ShapeDtypeStruct(qn.shape, F32), jax.ShapeDtypeStruct((bsz, h, t, 1), F32)],
        compiler_params=_params(("parallel", "parallel", "arbitrary")),
    )(qn, qp, kn, kp, v)


def mla_bwd(qn, qp, kn, kp, v, o, lse, do):
    bsz, h, t, _ = qn.shape
    n, hp = Q_BLOCK, MLA_HEADS_PER_STEP
    heads = range(hp)
    scale = (MLA_NOPE + MLA_ROPE) ** -0.5

    def body(qn_ref, qp_ref, kn_ref, kp_ref, v_ref, o_ref, lse_ref, do_ref,
             dqn_ref, dqp_ref, dkn_ref, dkp_ref, dv_ref):
        hg, qi = pl.program_id(1), pl.program_id(2)

        @pl.when(qi == 0)
        def _():
            dkn_ref[...] = jnp.zeros_like(dkn_ref)
            dv_ref[...] = jnp.zeros_like(dv_ref)

        @pl.when(jnp.logical_and(qi == 0, hg == 0))
        def _():
            dkp_ref[...] = jnp.zeros_like(dkp_ref)

        qn_v, qp_v = [qn_ref[hh] for hh in heads], [qp_ref[hh] for hh in heads]
        do_v, lse_v = [do_ref[hh] for hh in heads], [lse_ref[hh] for hh in heads]
        dsum = [jnp.sum(do_v[hh] * o_ref[hh], axis=1, keepdims=True) for hh in heads]

        dqn_ref[...] = jnp.zeros_like(dqn_ref)
        dqp_ref[...] = jnp.zeros_like(dqp_ref)

        def step(j, _):
            rows = pl.ds(pl.multiple_of(j * n, n), n)
            knj, vj, kpj = [kn_ref[hh, rows, :] for hh in heads], [v_ref[hh, rows, :] for hh in heads], kp_ref[rows, :]
            s = _mla_scores(qn_v, qp_v, knj, kpj, qi, j)
            p = _each(lambda s_, l_: jnp.exp(s_ - l_), s, lse_v)
            dp = _each(lambda d_, v_: _bdot(d_, v_, 1, 1), do_v, vj)
            ds = _each(lambda p_, dp_, d_: p_ * (dp_ - d_) * scale, p, dp, dsum)
            dqn = _each(lambda ds_, k_: _bdot(ds_, k_, 1, 0), ds, knj)
            dqp = _each(lambda ds_: _bdot(ds_, kpj, 1, 0), ds)
            dkn = _each(lambda ds_, q_: _bdot(ds_, q_, 0, 0), ds, qn_v)
            dv = _each(lambda p_, d_: _bdot(p_, d_, 0, 0), p, do_v)
            dkp = _each(lambda ds_, q_: _bdot(ds_, q_, 0, 0), ds, qp_v)
            for hh in heads:
                dqn_ref[hh] += dqn[hh]
                dqp_ref[hh] += dqp[hh]
                dkn_ref[hh, rows, :] += dkn[hh]
                dv_ref[hh, rows, :] += dv[hh]
            dkp_ref[rows, :] += functools.reduce(lambda a_, b_: a_ + b_, dkp)
            return 0

        lax.fori_loop(0, qi + 1, step, 0)

    qn_s, qp_s, kn_s, kp_s, row_s = _mla_specs(t, hp)
    return pl.pallas_call(
        body, name="mla_bwd", grid=(bsz, h // hp, t // n),
        in_specs=[qn_s, qp_s, kn_s, kp_s, kn_s, qn_s, row_s, qn_s],
        out_specs=[qn_s, qp_s, kn_s, kp_s, kn_s],
        out_shape=[jax.ShapeDtypeStruct(qn.shape, F32), jax.ShapeDtypeStruct(qp.shape, F32),
                   jax.ShapeDtypeStruct(kn.shape, F32), jax.ShapeDtypeStruct(kp.shape, F32),
                   jax.ShapeDtypeStruct(v.shape, F32)],
        compiler_params=_params(("parallel", "arbitrary", "arbitrary")),
    )(qn, qp, kn, kp, v, o, lse, do)


def rope(name, x, pos, inv_freq, sign):
    bsz, hx, t, d = x.shape
    half = d // 2

    tt = _largest_tile(t, 512, SUBLANES)

    def body(x_ref, pos_ref, f_ref, o_ref):
        ang = pos_ref[...].astype(F32) * f_ref[...]
        cos, sin = jnp.cos(ang), sign * jnp.sin(ang)
        ri, ci = _iota((d, d), 0), _iota((d, d), 1)
        rot = jnp.where(ri == ci + half, -1.0, 0.0) + jnp.where(ri + half == ci, 1.0, 0.0)
        for hh in range(hx):
            xv = x_ref[hh]
            o_ref[hh] = xv * cos + mmf(xv, rot) * sin

    blk = pl.BlockSpec((None, hx, tt, d), lambda b, i: (b, 0, i, 0))
    return pl.pallas_call(
        body, name=name, grid=(bsz, t // tt),
        in_specs=[blk, pl.BlockSpec((None, tt, 1), lambda b, i: (b, i, 0)), pl.BlockSpec((1, d), lambda b, i: (0, 0))],
        out_specs=blk, out_shape=jax.ShapeDtypeStruct(x.shape, F32),
        compiler_params=_params(("parallel", "parallel")),
    )(x, pos, inv_freq)


def loss_head(h, target, tm):
    n, d = h.shape

    def body(h_ref, t_ref, dh_ref, l_ref):
        @pl.when(pl.program_id(0) == 0)
        def _():
            l_ref[...] = jnp.zeros_like(l_ref)

        e = h_ref[...] - t_ref[...]
        dh_ref[...] = e * (1.0 / d)
        l_ref[...] += jnp.sum(e * e, axis=(0, 1), keepdims=True) * (0.5 / d)

    row = pl.BlockSpec((tm, d), lambda i: (i, 0))
    dh, l = pl.pallas_call(
        body, name="loss_head", grid=(n // tm,), in_specs=[row, row],
        out_specs=[row, pl.BlockSpec((SUBLANES, LANES), lambda i: (0, 0))],
        out_shape=[jax.ShapeDtypeStruct((n, d), F32), jax.ShapeDtypeStruct((SUBLANES, LANES), F32)],
        compiler_params=_params(("arbitrary",)),
    )(h, target)
    return dh, l[0, 0]


def _exchange_copies(src_refs, out_refs, send_sems, recv_sems, local_sems, gather):
    x, y, c = lax.axis_index("x"), lax.axis_index("y"), lax.axis_index("c")
    me = 4 * x + 2 * y + c
    local, remote = [], []
    for p, (src_ref, out_ref) in enumerate(zip(src_refs, out_refs)):
        local.append(pltpu.make_async_copy(src_ref if gather else src_ref.at[me], out_ref.at[me], local_sems.at[p]))
        for m in range(1, N_DEV):
            px, py, pc = x ^ (m >> 2), y ^ ((m >> 1) & 1), c ^ (m & 1)
            peer = 4 * px + 2 * py + pc
            remote.append(pltpu.make_async_remote_copy(
                src_ref=src_ref if gather else src_ref.at[peer], dst_ref=out_ref.at[me],
                send_sem=send_sems.at[p, m], recv_sem=recv_sems.at[p, m],
                device_id=(px, py, pc), device_id_type=pl.DeviceIdType.MESH))
    return local, remote


def _exchange_start(copies):
    local, remote = copies
    for cp in local + remote:
        cp.start()


def _exchange_wait(copies):
    local, remote = copies
    for cp in remote:
        cp.wait_recv()
    for cp in remote:
        cp.wait_send()
    for cp in local:
        cp.wait()


def _exchange_scratch(count):
    return [pltpu.SemaphoreType.DMA((count, N_DEV)), pltpu.SemaphoreType.DMA((count, N_DEV)),
            pltpu.SemaphoreType.DMA((count,))]


def _exchange_out(src, gather):
    return jax.ShapeDtypeStruct(((N_DEV,) + src.shape) if gather else src.shape, src.dtype)


def peer_exchange(name, srcs, gather):
    count = len(srcs)

    def body(*refs):
        copies = _exchange_copies(refs[:count], refs[count:2 * count], *refs[2 * count:], gather)
        _exchange_start(copies)
        _exchange_wait(copies)

    hbm = pl.BlockSpec(memory_space=pl.ANY)
    return pl.pallas_call(
        body, name=name, in_specs=[hbm] * count, out_specs=[hbm] * count,
        out_shape=[_exchange_out(s_, gather) for s_ in srcs], scratch_shapes=_exchange_scratch(count),
    )(*srcs)


def _grid_first_last(grid):
    ids = [pl.program_id(a) for a in range(len(grid))]
    first = functools.reduce(jnp.logical_and, [i == 0 for i in ids])
    last = functools.reduce(jnp.logical_and, [i == g - 1 for i, g in zip(ids, grid)])
    return first, last


def _ride_start(src_refs, out_refs, sem_refs, gather, first):
    copies = _exchange_copies(src_refs, out_refs, *sem_refs, gather)

    @pl.when(first)
    def _():
        _exchange_start(copies)

    return copies


def _ride_wait(copies, last):
    @pl.when(last)
    def _():
        _exchange_wait(copies)


def _ride_args(ride):
    if ride is None:
        return [], [], [], [], []
    srcs, gather = ride
    hbm = pl.BlockSpec(memory_space=pl.ANY)
    return (list(srcs), [hbm] * len(srcs), [_exchange_out(s_, gather) for s_ in srcs], [hbm] * len(srcs),
            _exchange_scratch(len(srcs)))


ADAM_BLOCK_BYTES = 4 * 1024 * 1024


def adamw_sum(name, parts, w, m, v):
    r, cols = w.shape
    tr = _largest_tile(r, min(ADAM_ROWS, max(SUBLANES, ADAM_BLOCK_BYTES // (N_DEV * cols * 4))), SUBLANES)

    def body(p_ref, w_ref, m_ref, v_ref, g_ref, d_ref, nm_ref, nv_ref):
        g = p_ref[0]
        for j in range(1, N_DEV):
            g = g + p_ref[j]
        mm_ = ADAM_B1 * m_ref[...] + (1.0 - ADAM_B1) * g
        vv = ADAM_B2 * v_ref[...] + (1.0 - ADAM_B2) * (g * g)
        m_hat = mm_ / (1.0 - ADAM_B1 ** ADAM_STEP)
        v_hat = vv / (1.0 - ADAM_B2 ** ADAM_STEP)
        g_ref[...] = g
        d_ref[...] = -ADAM_LR * (m_hat / (jnp.sqrt(v_hat) + ADAM_EPS) + ADAM_WD * w_ref[...])
        nm_ref[...] = mm_
        nv_ref[...] = vv

    row = pl.BlockSpec((tr, cols), lambda i: (i, 0))
    shp = jax.ShapeDtypeStruct((r, cols), F32)
    return pl.pallas_call(
        body, name=name, grid=(r // tr,),
        in_specs=[pl.BlockSpec((N_DEV, tr, cols), lambda i: (0, i, 0)), row, row, row],
        out_specs=[row] * 4, out_shape=[shp] * 4,
        compiler_params=_params(("parallel",)),
    )(parts, w, m, v)


WEIGHTS = ['l0_w_in', 'rwkv_mix', 'rwkv_w0', 'rwkv_w2', 'rwkv_a0', 'rwkv_a2', 'rwkv_g2', 'rwkv_k_k', 'rwkv_k_a',
           'rwkv_r_k', 'rwkv_ln_g', 'rwkv_ln_b', 'ssm_conv_w', 'ssm_conv_b', 'ssm_dt_bias', 'ssm_a_log', 'ssm_d',
           'ssm_norm_g', 'l0_w_out', 'l0_ln1_g', 'l0_ln1_b', 'ffn0_w_up', 'ffn0_conv_w', 'ffn0_conv_b',
           'ffn0_w_down', 'l0_ln2_g', 'l0_ln2_b', 'l1_w_in', 'mla_q_norm_g', 'mla_w_uq', 'mla_kv_norm_g',
           'mla_w_ukv', 'l1_w_out', 'l1_ln1_g', 'l1_ln1_b', 'ffn1_w_up', 'ffn1_conv_w', 'ffn1_conv_b',
           'ffn1_w_down', 'l1_ln2_g', 'l1_ln2_b']
SHARD_AXIS = {'l0_w_in': 1, 'rwkv_w2': 1, 'rwkv_a2': 1, 'rwkv_g2': 1, 'ssm_conv_w': 1, 'l0_w_out': 0,
              'ffn0_w_up': 1, 'ffn0_conv_w': 1, 'ffn0_w_down': 0, 'l1_w_in': 1, 'mla_w_uq': 1, 'mla_w_ukv': 1,
              'l1_w_out': 0, 'ffn1_w_up': 1, 'ffn1_conv_w': 1, 'ffn1_w_down': 0}
MATMUL_W = ['l0_w_in', 'rwkv_w2', 'rwkv_a2', 'rwkv_g2', 'l0_w_out', 'ffn0_w_up', 'ffn0_w_down', 'l1_w_in',
            'mla_w_uq', 'mla_w_ukv', 'l1_w_out', 'ffn1_w_up', 'ffn1_w_down']
CONV_W = ['ssm_conv_w', 'ffn0_conv_w', 'ffn1_conv_w']
TOK_TILE = 256
ADAM_ROWS = 512


def _ceil_to(size, unit):
    return -(-size // unit) * unit


def _flat_rows(pieces, seg_rows, total_rows):
    unit = seg_rows * LANES
    out, total = [], 0
    for p in pieces:
        p = jnp.concatenate([q.reshape(-1) for q in p]) if isinstance(p, list) else p.reshape(-1)
        pad = _ceil_to(p.size, unit) - p.size
        out.append(jnp.pad(p, (0, pad)) if pad else p)
        total += p.size + pad
    tail = _ceil_to(total, total_rows * LANES) - total
    if tail:
        out.append(jnp.zeros((tail,), out[0].dtype))
    return jnp.concatenate(out).reshape(-1, LANES)


def _unflatten(flat2d, shapes, seg_rows):
    flat = flat2d.reshape(-1)
    out, off = [], 0
    for shp in shapes:
        if isinstance(shp, list):
            seg, pos = [], off
            for s_ in shp:
                seg.append(flat[pos:pos + math.prod(s_)].reshape(s_))
                pos += math.prod(s_)
            out.append(seg)
            size = pos - off
        else:
            size = math.prod(shp)
            out.append(flat[off:off + size].reshape(shp))
        off += _ceil_to(size, seg_rows * LANES)
    return out


def _row(v):
    return v.reshape(1, -1)


def _pad_lanes(v):
    return jnp.pad(v.reshape(1, -1), ((0, 0), (0, LANES - v.size)))


def _local_step(a, w, comm=None):
    x = a['x']
    bsz, t, d = x.shape
    n = bsz * t
    tm = TOK_TILE
    pos = a['positions'].reshape(bsz, t, 1)
    inv_freq = 1.0 / (ROPE_THETA ** (jnp.arange(0, MLA_ROPE, 2, dtype=F32) / MLA_ROPE))
    inv_freq = jnp.concatenate([inv_freq, inv_freq]).reshape(1, MLA_ROPE)
    target = a['loss_target'].reshape(n, d)

    wi0 = w['l0_w_in']
    win0 = jnp.concatenate([wi0[:, :RKV_COLS], wi0[:, RWKV_COLS:L0_COLS - SSM_HEADS], wi0[:, RKV_COLS:RWKV_COLS],
                            wi0[:, L0_COLS - SSM_HEADS:], jnp.zeros((d, L0_PAD - L0_COLS), wi0.dtype)], axis=1)
    w2p = jnp.concatenate([w['rwkv_w2'], jnp.zeros_like(w['rwkv_w2'])], axis=0)
    a2p = jnp.concatenate([jnp.zeros_like(w['rwkv_a2']), w['rwkv_a2']], axis=0)
    mix = a['rwkv_mix']
    taps = jnp.stack([mix, 1.0 - mix])
    zero_b = jnp.zeros((1, mix.size), F32)
    rw_map = lambda j: j + jnp.where(j >= RKV_COLS // CONV_TILE, (P0_LORA - RKV_COLS) // CONV_TILE, 0)
    ssm_map = lambda j: j + (P0_Z + SSM_DIM) // CONV_TILE
    gate_map = lambda j: j
    up_map = lambda j: j + D_FF // CONV_TILE
    dt_col = P0_DT // LANES
    dtb, alog, dsk = _pad_lanes(a['ssm_dt_bias']), _pad_lanes(a['ssm_a_log']), _pad_lanes(a['ssm_d'])
    pre_p = [_row(a['rwkv_w0']), w2p, _row(a['rwkv_a0']), a2p, w['rwkv_g2'], _row(a['rwkv_k_k']), _row(a['rwkv_k_a'])]
    post_p = [_row(a['rwkv_ln_g']), _row(a['rwkv_ln_b']), _row(a['rwkv_r_k'])]
    sp_p = [dsk, _row(a['ssm_norm_g'])]

    def ln(name, h, y, layer, which):
        ps = [_row(a[f'l{layer}_ln{which}_g']), _row(a[f'l{layer}_ln{which}_b'])]
        return tok_fwd(name, f_ln, [(h, d, 0), (y, d, 0)], ps, [d], 2 * tm, n, t)[0]

    def ffn_fwd(layer, h):
        up = mm(h, w[f'ffn{layer}_w_up'], 'nn', f'ffn{layer}_up')
        act = dwconv_fwd(f'ffn{layer}_conv', up.reshape(bsz, t, 2 * D_FF), gate_map, w[f'ffn{layer}_conv_w'],
                         _row(a[f'ffn{layer}_conv_b']), 3, True, upmap=up_map, out_dtype=BF16)
        act = act.reshape(n, D_FF)
        return up, act, mm(act, w[f'ffn{layer}_w_down'], 'nn', f'ffn{layer}_down')

    x2 = x.reshape(n, d)
    proj0 = mm(x2, win0, 'nn', 'l0_in')
    p0 = proj0.reshape(bsz, t, L0_PAD)
    xs_r = dwconv_fwd('rwkv_shift', p0, rw_map, taps, zero_b, 2, False).reshape(n, RWKV_COLS)
    lora_blk = RKV_COLS // LANES
    pre_x = [(xs_r, RWKV_DIM, 0), (xs_r, RWKV_DIM, 1), (xs_r, RWKV_DIM, 2), (xs_r, LANES, lora_blk),
             (xs_r, LANES, lora_blk + 1)]
    heads64 = (RWKV_HEADS, HEAD_DIM)
    r_, v_, lw, kmod, al, be, gt = tok_fwd('rwkv_pre', f_rwkv_pre, pre_x, pre_p, [heads64] * 6 + [RWKV_DIM],
                                           tm, n, t)
    scan_in = [r_, lw, kmod, v_, al, be]
    if comm is None:
        y_h, rstates = rwkv_scan_fwd(*scan_in)
    else:
        y_h, rstates, *gots = rwkv_scan_fwd(*scan_in, ride=comm.weights_ride(MID_GATHER))
        w = {**w, **comm.weights(MID_GATHER, gots)}
    wi1 = w['l1_w_in']
    win1 = jnp.concatenate([wi1, jnp.zeros((d, L1_PAD - L1_COLS), wi1.dtype)], axis=1)
    wq3 = w['mla_w_uq'].reshape(-1, 8, MLA_NOPE + MLA_ROPE)
    wkv3 = w['mla_w_ukv'].reshape(-1, 8, 2 * MLA_NOPE)
    mla_p = [_row(a['mla_q_norm_g']), wq3[:, :, :MLA_NOPE].reshape(-1, 512), wq3[:, :, MLA_NOPE:].reshape(-1, 256),
             _row(a['mla_kv_norm_g']), wkv3[:, :, :MLA_NOPE].reshape(-1, 512), wkv3[:, :, MLA_NOPE:].reshape(-1, 512)]
    post_x = [y_h, r_, kmod, v_, (gt, 512, 0)]
    y_a = tok_fwd('rwkv_post', f_rwkv_post, post_x, post_p, [RWKV_DIM], tm, n, t)[0]
    xbc = dwconv_fwd('ssm_conv', p0, ssm_map, w['ssm_conv_w'], _row(a['ssm_conv_b']), 4, True)
    ys, sstates = ssd_fwd(xbc, p0, dt_col, dtb, alog)
    xbc2 = xbc.reshape(n, 2 * SSM_DIM)
    sp_x = [(ys.reshape(n, SSM_DIM), SSM_DIM, 0), (xbc2, SSM_DIM, 0), (proj0, SSM_DIM, P0_Z // SSM_DIM)]
    y_b = tok_fwd('ssd_post', f_ssd_post, sp_x, sp_p, [SSM_DIM], tm, n, t)[0]
    wo0 = w['l0_w_out']
    mixed0 = mm(y_b, wo0, 'nn', 'l0_out_b', b_rows=(512, 512), add=mm(y_a, wo0, 'nn', 'l0_out_a', b_rows=(0, 512)))
    h1 = ln('l0_ln1', x2, mixed0, 0, 1)
    up0, act0, f0 = ffn_fwd(0, h1)
    h2 = ln('l0_ln2', h1, f0, 0, 2)

    proj1 = mm(h2, win1, 'nn', 'l1_in')
    q_sb, k_sb, v_sb = tok_fwd('sb_split', f_same, [(proj1, 512, 0), (proj1, 512, 1), (proj1, 512, 2)], [],
                               [heads64] * 3, tm, n, t)
    if comm is None:
        o_c, sb_kept = sb_fwd(q_sb, k_sb, v_sb)
    else:
        o_c, sb_kept, *gots = sb_fwd(q_sb, k_sb, v_sb, ride=comm.weights_ride(FFN1_GATHER))
        w = {**w, **comm.weights(FFN1_GATHER, gots)}
    mla_x = [(proj1, P1_CKV - P1_CQ, P1_CQ // (P1_CKV - P1_CQ)), (proj1, LANES, P1_CKV // LANES)]
    qn, qp_raw, kn, vv = tok_fwd('mla_pre', f_mla_pre, mla_x, mla_p, [heads64, (8, MLA_ROPE), heads64, heads64],
                                 tm, n, t)
    kp_raw = proj1[:, P1_KPE:P1_KPE + MLA_ROPE].reshape(bsz, 1, t, MLA_ROPE)
    qp = rope('rope_q', qp_raw, pos, inv_freq, 1.0)
    kp = rope('rope_k', kp_raw, pos, inv_freq, 1.0)
    o_d, lse_d = mla_fwd(qn, qp, kn, kp, vv)
    y_cd = tok_fwd('attn_merge', f_concat, [o_c, o_d], [], [2 * RWKV_DIM], tm, n, t, out_dtype=BF16)[0]
    wo1 = w['l1_w_out']
    mixed1 = mm(y_cd, wo1, 'nn', 'l1_out')
    h3 = ln('l1_ln1', h2, mixed1, 1, 1)
    up1, act1, f1 = ffn_fwd(1, h3)
    h4 = ln('l1_ln2', h3, f1, 1, 2)
    dh4, loss = loss_head(h4, target, tm)

    g = {}

    def ln_bwd(name, h, y, layer, which, dout):
        ps = [_row(a[f'l{layer}_ln{which}_g']), _row(a[f'l{layer}_ln{which}_b'])]
        (dh, dy), (dg, db) = tok_bwd(name, f_ln, [(h, d, 0), (y, d, 0)], ps, [[dout]], 2 * tm, n, t)
        g[f'l{layer}_ln{which}_g'], g[f'l{layer}_ln{which}_b'] = dg.reshape(-1), db.reshape(-1)
        return dh, dy

    def ffn_bwd(layer, h, up, act, df, dh_res):
        wup, wdown = w[f'ffn{layer}_w_up'], w[f'ffn{layer}_w_down']
        g[f'ffn{layer}_w_down'] = mm(act, df, 'tn', f'ffn{layer}_dwdown')
        dact = mm(df, wdown, 'nt', f'ffn{layer}_dact').reshape(bsz, t, D_FF)
        dgate, dcw, dcb, dup = dwconv_bwd(f'ffn{layer}_conv_bwd', up.reshape(bsz, t, 2 * D_FF), gate_map,
                                          w[f'ffn{layer}_conv_w'], _row(a[f'ffn{layer}_conv_b']), 3, True, dact,
                                          upmap=up_map, grad_dtype=BF16)
        dgate, dup = dgate.reshape(n, D_FF), dup.reshape(n, D_FF)
        g[f'ffn{layer}_conv_w'], g[f'ffn{layer}_conv_b'] = dcw, dcb.reshape(-1)
        g[f'ffn{layer}_w_up'] = (mm(h, dgate, 'tn', f'ffn{layer}_dwgate'), mm(h, dup, 'tn', f'ffn{layer}_dwup'))
        dh = mm(dgate, wup, 'nt', f'ffn{layer}_dh_gate', add=dh_res, b_cols=(0, D_FF))
        return mm(dup, wup, 'nt', f'ffn{layer}_dh_up', add=dh, b_cols=(D_FF, D_FF))

    dh3_res, df1 = ln_bwd('l1_ln2_bwd', h3, f1, 1, 2, dh4)
    dh3 = ffn_bwd(1, h3, up1, act1, df1, dh3_res)
    dh2_res, dmixed1 = ln_bwd('l1_ln1_bwd', h2, mixed1, 1, 1, dh3)
    g['l1_w_out'] = mm(y_cd, dmixed1, 'tn', 'l1_dwout')
    dy_cd = mm(dmixed1, wo1, 'nt', 'l1_dy')
    dy_c, dy_d = tok_fwd('attn_split', f_same, [(dy_cd, 512, 0), (dy_cd, 512, 1)], [], [heads64] * 2, tm, n, t)
    parts = {}
    if comm is None:
        dq_sb, dk_sb, dv_sb = sb_bwd(q_sb, k_sb, v_sb, sb_kept, dy_c)
    else:
        dq_sb, dk_sb, dv_sb, *parts['a'] = sb_bwd(q_sb, k_sb, v_sb, sb_kept, dy_c, ride=comm.grad_ride('a', g))
    dqn, dqp, dkn, dkp, dvv = mla_bwd(qn, qp, kn, kp, vv, o_d, lse_d, dy_d)
    dqp_raw = rope('rope_q_bwd', dqp, pos, inv_freq, -1.0)
    dkp_raw = rope('rope_k_bwd', dkp, pos, inv_freq, -1.0).reshape(n, MLA_ROPE)
    (dcq, dckv), (dqg, dwq_n, dwq_p, dkvg, dwk, dwv) = tok_bwd('mla_pre_bwd', f_mla_pre, mla_x, mla_p,
                                                               [[dqn], [dqp_raw], [dkn], [dvv]], tm, n, t)
    g['mla_q_norm_g'], g['mla_kv_norm_g'] = dqg.reshape(-1), dkvg.reshape(-1)
    g['mla_w_uq'] = jnp.concatenate([dwq_n.reshape(-1, 8, MLA_NOPE), dwq_p.reshape(-1, 8, MLA_ROPE)],
                                    axis=2).reshape(-1, 8 * (MLA_NOPE + MLA_ROPE))
    g['mla_w_ukv'] = jnp.concatenate([dwk.reshape(-1, 8, MLA_NOPE), dwv.reshape(-1, 8, MLA_NOPE)],
                                     axis=2).reshape(-1, 16 * MLA_NOPE)
    dkp_pad = jnp.pad(dkp_raw, ((0, 0), (0, LANES - MLA_ROPE)))
    dproj1 = tok_fwd('l1_dproj', f_concat, [dq_sb, dk_sb, dv_sb, (dcq, 256, 0), (dckv, LANES, 0),
                                            (dkp_pad, LANES, 0)], [], [L1_PAD], tm, n, t, out_dtype=BF16)[0]
    g['l1_w_in'] = mm(h2, dproj1, 'tn', 'l1_dwin')[:, :L1_COLS]
    dh2 = mm(dproj1, win1, 'nt', 'l1_dh', add=dh2_res)

    dh1_res, df0 = ln_bwd('l0_ln2_bwd', h1, f0, 0, 2, dh2)
    dh1 = ffn_bwd(0, h1, up0, act0, df0, dh1_res)
    dx_res, dmixed0 = ln_bwd('l0_ln1_bwd', x2, mixed0, 0, 1, dh1)
    g['l0_w_out'] = (mm(y_a, dmixed0, 'tn', 'l0_dwout_a'), mm(y_b, dmixed0, 'tn', 'l0_dwout_b'))
    dy_a = mm(dmixed0, wo0, 'nt', 'l0_dy_a', b_rows=(0, 512))
    dy_b = mm(dmixed0, wo0, 'nt', 'l0_dy_b', b_rows=(512, 512))
    (dy_r, dr1, dkm1, dv1, dgt), (dlng, dlnb, drk) = tok_bwd('rwkv_post_bwd', f_rwkv_post, post_x, post_p, [[dy_a]],
                                                            tm, n, t, dx_layouts=[heads64] * 4 + [None])
    g['rwkv_ln_g'], g['rwkv_ln_b'] = dlng.reshape(-1), dlnb.reshape(-1)
    g['rwkv_r_k'] = drk.reshape(RWKV_HEADS, HEAD_DIM)
    (dys, dxs_skip, dz), (ddsk, dng) = tok_bwd('ssd_post_bwd', f_ssd_post, sp_x, sp_p, [[dy_b]], tm, n, t)
    g['ssm_d'], g['ssm_norm_g'] = ddsk[0, :SSM_HEADS], dng.reshape(-1)
    dxbc_act, ddtr, ddtb, dalog = ssd_bwd(xbc, p0, dt_col, dtb, alog, sstates, dys.reshape(bsz, t, SSM_DIM),
                                          dxs_skip.reshape(bsz, t, SSM_DIM))
    g['ssm_dt_bias'], g['ssm_a_log'] = ddtb[0, :SSM_HEADS], dalog[0, :SSM_HEADS]
    dxbc, dscw, dscb = dwconv_bwd('ssm_conv_bwd', p0, ssm_map, w['ssm_conv_w'], _row(a['ssm_conv_b']), 4, True,
                                  dxbc_act)
    g['ssm_conv_w'], g['ssm_conv_b'] = dscw, dscb.reshape(-1)
    if comm is None:
        dscan = rwkv_scan_bwd(*scan_in, rstates, dy_r)
    else:
        outs = rwkv_scan_bwd(*scan_in, rstates, dy_r, ride=comm.grad_ride('b', g))
        dscan, parts['b'] = outs[:6], outs[6:]
    dr2, dlw, dk2, dv2, dal, dbe = dscan
    pre_ct = [[dr1, dr2], [dv1, dv2], [dlw], [dkm1, dk2], [dal], [dbe], [dgt]]
    dpre_x, dpre_p = tok_bwd('rwkv_pre_bwd', f_rwkv_pre, pre_x, pre_p, pre_ct, tm, n, t)
    g['rwkv_w0'], g['rwkv_a0'] = dpre_p[0].reshape(-1), dpre_p[2].reshape(-1)
    g['rwkv_w2'], g['rwkv_a2'], g['rwkv_g2'] = dpre_p[1][:64], dpre_p[3][64:], dpre_p[4]
    g['rwkv_k_k'], g['rwkv_k_a'] = dpre_p[5].reshape(-1), dpre_p[6].reshape(-1)
    dxs_r = jnp.concatenate(dpre_x, axis=1).reshape(bsz, t, RWKV_COLS)
    d_rw, dtaps, _ = dwconv_bwd('rwkv_shift_bwd', p0, rw_map, taps, zero_b, 2, False, dxs_r)
    d_rw = d_rw.reshape(n, RWKV_COLS)
    g['rwkv_mix'] = dtaps[0] - dtaps[1]
    lora_cols = RWKV_COLS - RKV_COLS
    dproj0 = tok_fwd('l0_dproj', f_concat, [(d_rw, RKV_COLS, 0), (dz, SSM_DIM, 0),
                                            (dxbc.reshape(n, 2 * SSM_DIM), 2 * SSM_DIM, 0),
                                            (d_rw, lora_cols, RKV_COLS // lora_cols), (ddtr.reshape(n, LANES), LANES, 0)],
                     [], [L0_PAD], tm, n, t, out_dtype=BF16)[0]
    dwin0 = mm(x2, dproj0, 'tn', 'l0_dwin')
    g['l0_w_in'] = jnp.concatenate([dwin0[:, :RKV_COLS], dwin0[:, P0_LORA:P0_DT], dwin0[:, P0_Z:P0_LORA],
                                    dwin0[:, P0_DT:P0_DT + SSM_HEADS]], axis=1)
    if comm is None:
        dx = mm(dproj0, win0, 'nt', 'l0_dx', add=dx_res)
    else:
        dx, *parts['c'] = mm(dproj0, win0, 'nt', 'l0_dx', add=dx_res, ride=comm.grad_ride('c', g))
    return loss, dx.reshape(bsz, t, d), g, parts


GRAD_GROUPS = {
    'a': ['ffn1_w_up', 'ffn1_conv_w', 'ffn1_conv_b', 'ffn1_w_down', 'l1_ln2_g', 'l1_ln2_b'],
    'c': ['l0_w_in', 'rwkv_mix', 'rwkv_w0', 'rwkv_w2', 'rwkv_a0', 'rwkv_a2', 'rwkv_g2', 'rwkv_k_k', 'rwkv_k_a'],
}
GRAD_GROUPS['b'] = [nm for nm in WEIGHTS if nm not in GRAD_GROUPS['a'] + GRAD_GROUPS['c']]
FIRST_GATHER = ['l0_w_in', 'rwkv_w2', 'rwkv_a2', 'rwkv_g2'] + CONV_W
MID_GATHER = ['l0_w_out', 'ffn0_w_up', 'ffn0_w_down', 'l1_w_in', 'mla_w_uq', 'mla_w_ukv', 'l1_w_out']
FFN1_GATHER = ['ffn1_w_up', 'ffn1_w_down']
BF16_ROWS = 16


def _name_kinds(names):
    return ([nm for nm in names if nm in MATMUL_W], [nm for nm in names if nm in SHARD_AXIS and nm not in MATMUL_W],
            [nm for nm in names if nm not in SHARD_AXIS])


class _Comm:
    def __init__(self, a):
        self.a = a

    def weights_ride(self, names):
        big, conv, _ = _name_kinds(names)
        srcs = [self.a[nm].astype(BF16) for nm in big]
        if conv:
            srcs.append(_flat_rows([lax.bitcast_convert_type(self.a[nm], BF16) for nm in conv], BF16_ROWS, BF16_ROWS))
        return srcs, True

    def weights(self, names, gots):
        big, conv, _ = _name_kinds(names)
        out = {}
        for nm, got in zip(big, gots):
            out[nm] = (got.reshape(-1, got.shape[2]) if SHARD_AXIS[nm] == 0
                       else jnp.concatenate([got[k] for k in range(N_DEV)], axis=1))
        if conv:
            shapes = [self.a[nm].shape + (2,) for nm in conv]
            blocks = [_unflatten(gots[-1][k], shapes, BF16_ROWS) for k in range(N_DEV)]
            for i, nm in enumerate(conv):
                out[nm] = jnp.concatenate([lax.bitcast_convert_type(blocks[k][i], F32) for k in range(N_DEV)], axis=1)
        return out

    def first_weights(self):
        return self.weights(FIRST_GATHER, peer_exchange('gather_first_weights', self.weights_ride(FIRST_GATHER)[0], True))

    def grad_ride(self, group, g):
        def shard_of(nm, k):
            gv = g[nm]
            per = N_DEV
            if isinstance(gv, tuple):
                gv, k, per = gv[k // 4], k % 4, 4
            width = gv.shape[SHARD_AXIS[nm]] // per
            return lax.slice_in_dim(gv, k * width, (k + 1) * width, axis=SHARD_AXIS[nm])

        big, conv, small = _name_kinds(GRAD_GROUPS[group])
        srcs = [jnp.stack([shard_of(nm, k) for k in range(N_DEV)]) for nm in big]
        srcs.append(jnp.stack([_flat_rows([shard_of(nm, k) for nm in conv] + [[g[nm] for nm in small]],
                                          SUBLANES, ADAM_ROWS) for k in range(N_DEV)]))
        return srcs, False


def _step(a):
    comm = _Comm(a)
    loss, dx, _, parts = _local_step(a, comm.first_weights(), comm)
    loss = lax.psum(loss, ('x', 'y', 'c'))
    res = {}
    for group, names in GRAD_GROUPS.items():
        big, conv, small = _name_kinds(names)
        for nm, got in zip(big, parts[group]):
            res[nm] = adamw_sum(f'adamw_{nm}', got, a[nm], a['m_' + nm], a['v_' + nm])
        flat = lambda prefix: _flat_rows([a[prefix + nm] for nm in conv] + [[a[prefix + nm] for nm in small]],
                                         SUBLANES, ADAM_ROWS)
        outs = adamw_sum(f'adamw_{group}', parts[group][-1], flat(''), flat('m_'), flat('v_'))
        shapes = [a[nm].shape for nm in conv] + [[a[nm].shape for nm in small]]
        per_out = [_unflatten(o, shapes, SUBLANES) for o in outs]
        for i, nm in enumerate(conv):
            res[nm] = [per_out[j][i] for j in range(4)]
        for i, nm in enumerate(small):
            res[nm] = [per_out[j][-1][i] for j in range(4)]
    return (loss, dx, *[res[nm][j] for j in range(4) for nm in WEIGHTS])


def kernel(x, positions, l0_w_in, rwkv_mix, rwkv_w0, rwkv_w2, rwkv_a0, rwkv_a2, rwkv_g2, rwkv_k_k, rwkv_k_a, rwkv_r_k, rwkv_ln_g, rwkv_ln_b, ssm_conv_w, ssm_conv_b, ssm_dt_bias, ssm_a_log, ssm_d, ssm_norm_g, l0_w_out, l0_ln1_g, l0_ln1_b, ffn0_w_up, ffn0_conv_w, ffn0_conv_b, ffn0_w_down, l0_ln2_g, l0_ln2_b, l1_w_in, mla_q_norm_g, mla_w_uq, mla_kv_norm_g, mla_w_ukv, l1_w_out, l1_ln1_g, l1_ln1_b, ffn1_w_up, ffn1_conv_w, ffn1_conv_b, ffn1_w_down, l1_ln2_g, l1_ln2_b, loss_target, m_l0_w_in, m_rwkv_mix, m_rwkv_w0, m_rwkv_w2, m_rwkv_a0, m_rwkv_a2, m_rwkv_g2, m_rwkv_k_k, m_rwkv_k_a, m_rwkv_r_k, m_rwkv_ln_g, m_rwkv_ln_b, m_ssm_conv_w, m_ssm_conv_b, m_ssm_dt_bias, m_ssm_a_log, m_ssm_d, m_ssm_norm_g, m_l0_w_out, m_l0_ln1_g, m_l0_ln1_b, m_ffn0_w_up, m_ffn0_conv_w, m_ffn0_conv_b, m_ffn0_w_down, m_l0_ln2_g, m_l0_ln2_b, m_l1_w_in, m_mla_q_norm_g, m_mla_w_uq, m_mla_kv_norm_g, m_mla_w_ukv, m_l1_w_out, m_l1_ln1_g, m_l1_ln1_b, m_ffn1_w_up, m_ffn1_conv_w, m_ffn1_conv_b, m_ffn1_w_down, m_l1_ln2_g, m_l1_ln2_b, v_l0_w_in, v_rwkv_mix, v_rwkv_w0, v_rwkv_w2, v_rwkv_a0, v_rwkv_a2, v_rwkv_g2, v_rwkv_k_k, v_rwkv_k_a, v_rwkv_r_k, v_rwkv_ln_g, v_rwkv_ln_b, v_ssm_conv_w, v_ssm_conv_b, v_ssm_dt_bias, v_ssm_a_log, v_ssm_d, v_ssm_norm_g, v_l0_w_out, v_l0_ln1_g, v_l0_ln1_b, v_ffn0_w_up, v_ffn0_conv_w, v_ffn0_conv_b, v_ffn0_w_down, v_l0_ln2_g, v_l0_ln2_b, v_l1_w_in, v_mla_q_norm_g, v_mla_w_uq, v_mla_kv_norm_g, v_mla_w_ukv, v_l1_w_out, v_l1_ln1_g, v_l1_ln1_b, v_ffn1_w_up, v_ffn1_conv_w, v_ffn1_conv_b, v_ffn1_w_down, v_l1_ln2_g, v_l1_ln2_b):
    return _step(dict(locals()))
```

```python
import functools
import math

import jax
import jax.numpy as jnp
from jax import lax
from jax.experimental import pallas as pl
from jax.experimental.pallas import tpu as pltpu

F32 = jnp.float32
BF16 = jnp.bfloat16
HI = lax.Precision.HIGHEST

V7X_VMEM_BYTES = 64 * 1024 * 1024
VMEM_LIMIT = V7X_VMEM_BYTES - 8 * 1024 * 1024
LANES = 128
SUBLANES = 8
N_DEV = 8

D_MODEL = 1024
HEAD_DIM = 64
RWKV_DIM = 512
RWKV_HEADS = 8
RWKV_GN_EPS = 64e-5
RWKV_CHUNK = 64
SSM_DIM = 512
SSM_HEADS = 8
SSM_CHUNK = 128
SSM_STATE = 128
Q_BLOCK = 128
SB_HEADS_PER_STEP = 4
MLA_HEADS_PER_STEP = 4
ATTN_FWD_HEADS_PER_STEP = 8
MLA_NOPE = 64
MLA_ROPE = 32
ROPE_THETA = 10000.0
D_FF = 2816
DEPTH = 2
ALPHA = (2 * DEPTH) ** 0.25
RKV_COLS = 3 * RWKV_DIM
RWKV_COLS = RKV_COLS + 64 + 64 + 128
L0_COLS = RWKV_COLS + SSM_DIM + 2 * SSM_DIM + SSM_HEADS
P0_Z = RKV_COLS
P0_LORA = P0_Z + 3 * SSM_DIM
P0_DT = P0_LORA + (RWKV_COLS - RKV_COLS)
L0_PAD = P0_DT + LANES
L1_COLS = 3 * RWKV_DIM + 256 + 128 + MLA_ROPE
P1_CQ, P1_CKV, P1_KPE = 3 * RWKV_DIM, 3 * RWKV_DIM + 256, 3 * RWKV_DIM + 256 + 128
L1_PAD = P1_KPE + LANES

ADAM_LR = 0.001
ADAM_B1 = 0.9
ADAM_B2 = 0.999
ADAM_EPS = 1e-08
ADAM_WD = 0.01
ADAM_STEP = 10

NEG_BIG = -1e30


def _params(sem=None):
    return pltpu.CompilerParams(dimension_semantics=sem, vmem_limit_bytes=VMEM_LIMIT)


P_F32, P_BF16, P_BF16X3 = 0, 1, 2


def _dg_raw(a, b, ca, cb, fast):
    dims = (((ca,), (cb,)), ((), ()))
    if fast == P_BF16:
        return lax.dot_general(a.astype(BF16), b.astype(BF16), dims, preferred_element_type=F32)
    prec = HI if fast == P_F32 else lax.Precision.HIGH
    return lax.dot_general(a, b, dims, precision=prec, preferred_element_type=F32)


@functools.partial(jax.custom_vjp, nondiff_argnums=(2, 3, 4))
def dg(a, b, ca, cb, fast):
    return _dg_raw(a, b, ca, cb, fast)


def _dg_fwd(a, b, ca, cb, fast):
    return _dg_raw(a, b, ca, cb, fast), (a, b)


def _dg_bwd(ca, cb, fast, res, ct):
    a, b = res
    fa, fb = 1 - ca, 1 - cb
    da = _dg_raw(ct, b, 1, fb, fast) if ca == 1 else _dg_raw(b, ct, fb, 1, fast)
    db = _dg_raw(a, ct, fa, 0, fast) if cb == 0 else _dg_raw(ct, a, 0, fa, fast)
    return da.astype(a.dtype), db.astype(b.dtype)


dg.defvjp(_dg_fwd, _dg_bwd)


def mmb(a, b):
    return dg(a, b, 1, 0, P_BF16)


def mmf(a, b):
    return dg(a, b, 1, 0, P_F32)


def mm3(a, b):
    return dg(a, b, 1, 0, P_BF16X3)


def mm3_nt(a, b):
    return dg(a, b, 1, 1, P_BF16X3)


def mm3_tn(a, b):
    return dg(a, b, 0, 0, P_BF16X3)


def _split3_dot(x, m01, cb, terms=3):
    parts, rest = [], x
    for i in range(terms):
        parts.append(rest.astype(BF16))
        if i + 1 < terms:
            rest = rest - parts[-1].astype(F32)
    rows = x.shape[0]
    out = lax.dot_general(jnp.concatenate(parts, axis=0), m01.astype(BF16), (((1,), (cb,)), ((), ())),
                          preferred_element_type=F32)
    return functools.reduce(lambda a_, b_: a_ + b_, [out[i * rows:(i + 1) * rows] for i in range(terms)])


def _lower_ones(n):
    return jnp.where(_iota((n, n), 0) >= _iota((n, n), 1), 1.0, 0.0)


SUFFIX_TERMS = 2


@jax.custom_vjp
def suffix_sum(x):
    return _split3_dot(x, _lower_ones(x.shape[1]), 0, SUFFIX_TERMS)


def _suffix_sum_fwd(x):
    return suffix_sum(x), None


def _suffix_sum_bwd(_, ct):
    return (_split3_dot(ct, _lower_ones(ct.shape[1]), 1, SUFFIX_TERMS),)


suffix_sum.defvjp(_suffix_sum_fwd, _suffix_sum_bwd)


def _iota(shape, dim):
    return lax.broadcasted_iota(jnp.int32, shape, dim)


def _softplus(x):
    return jnp.maximum(x, 0.0) + jnp.log1p(jnp.exp(-jnp.abs(x)))


def _silu(x):
    return x * jax.nn.sigmoid(x)


def _largest_tile(n, cap, mult):
    best = None
    t = mult
    while t <= min(n, cap):
        if n % t == 0:
            best = t
        t += mult
    return n if best is None else best


MM_VMEM_BUDGET = 40 * 1024 * 1024
V7X_HBM_BYTES_PER_S = 3.2e12
V7X_VMEM_STORE_BYTES_PER_S = 7e12
GRID_STEP_S = 0.35e-6


def _mm_tiles(M, N, K, a_bytes, b_bytes, has_add):
    def divs(n):
        return [d for d in range(LANES, n + 1, LANES) if n % d == 0] or [n]

    best = None
    for tm in divs(M):
        for tn in divs(N):
            if tm * tn * 4 > 12 * 1024 * 1024:
                continue
            for tk in divs(K):
                vmem = (2 * (tm * tk * a_bytes + tk * tn * b_bytes) + 2 * tm * tn * 4 * (2 if has_add else 1)
                        + (tm * tk + tk * tn) * 2 + tm * tn * 4)
                if vmem > MM_VMEM_BUDGET:
                    continue
                ni, nj, nk = M // tm, N // tn, K // tk
                a_reads = M * K * a_bytes * (1 if nk == 1 else nj)
                b_reads = K * N * b_bytes * (1 if (nk == 1 and nj == 1) else ni)
                traffic = a_reads + b_reads + M * N * 4 * (2 if has_add else 1)
                cost = (traffic / V7X_HBM_BYTES_PER_S + ni * nj * nk * GRID_STEP_S
                        + M * N * 4 * nk / V7X_VMEM_STORE_BYTES_PER_S)
                if min(tm, tn, tk) < 256 and min(M, N, K) >= 256:
                    cost *= 1.5
                if best is None or cost < best[0]:
                    best = (cost, tm, tn, tk)
    return best[1:]


def mm(a, b, mode, name, add=None, b_rows=None, b_cols=None, ride=None):
    r0, nr = b_rows or (0, b.shape[0])
    c0, nc = b_cols or (0, b.shape[1])
    if mode == "nn":
        (M, K), N = a.shape, nc
        assert nr == K
    elif mode == "nt":
        (M, K), N = a.shape, nr
        assert nc == K
    else:
        (K, M), N = a.shape, b.shape[1]
        assert b_rows is None and b_cols is None
    has_add = add is not None
    tm, tn, tk = _mm_tiles(M, N, K, a.dtype.itemsize, b.dtype.itemsize, has_add)
    nk = K // tk
    keep_a = nk == 1 and N // tn > 1 and a.dtype != BF16
    if mode == "nn":
        assert r0 % tk == 0 and c0 % tn == 0
        a_spec = pl.BlockSpec((tm, tk), lambda i, j, k: (i, k))
        b_spec = pl.BlockSpec((tk, tn), lambda i, j, k: (k + r0 // tk, j + c0 // tn))
        dims = (((1,), (0,)), ((), ()))
    elif mode == "nt":
        assert r0 % tn == 0 and c0 % tk == 0
        a_spec = pl.BlockSpec((tm, tk), lambda i, j, k: (i, k))
        b_spec = pl.BlockSpec((tn, tk), lambda i, j, k: (j + r0 // tn, k + c0 // tk))
        dims = (((1,), (1,)), ((), ()))
    else:
        a_spec = pl.BlockSpec((tk, tm), lambda i, j, k: (k, i))
        b_spec = pl.BlockSpec((tk, tn), lambda i, j, k: (k, j))
        dims = (((0,), (0,)), ((), ()))
    o_spec = pl.BlockSpec((tm, tn), lambda i, j, k: (i, j))
    grid = (M // tm, N // tn, nk)
    r_in, r_specs, r_out, r_ospecs, r_scr = _ride_args(ride)
    n_add, n_ride = int(has_add), len(r_in)

    def body(a_ref, b_ref, *rest):
        o_ref = rest[n_add + n_ride]
        if ride is not None:
            first, last = _grid_first_last(grid)
            copies = _ride_start(rest[n_add:n_add + n_ride], rest[n_add + n_ride + 1:n_add + 2 * n_ride + 1],
                                 rest[-3:], ride[1], first)
        k = pl.program_id(2)
        if keep_a:
            a_bf = rest[n_add + 2 * n_ride + 1]

            @pl.when(pl.program_id(1) == 0)
            def _():
                a_bf[...] = a_ref[...].astype(BF16)

            av = a_bf[...]
        else:
            av = a_ref[...].astype(BF16)
        part = lax.dot_general(av, b_ref[...].astype(BF16), dims, preferred_element_type=F32)

        @pl.when(k == 0)
        def _():
            o_ref[...] = part + rest[0][...] if has_add else part

        @pl.when(k > 0)
        def _():
            o_ref[...] += part

        if ride is not None:
            _ride_wait(copies, last)

    ins = [a, b] + ([add] if has_add else []) + r_in
    specs = [a_spec, b_spec] + ([o_spec] if has_add else []) + r_specs
    outs = pl.pallas_call(
        body, name=name, grid=grid, in_specs=specs, out_specs=[o_spec] + r_ospecs,
        out_shape=[jax.ShapeDtypeStruct((M, N), F32)] + r_out,
        scratch_shapes=([pltpu.VMEM(a_spec.block_shape, BF16)] if keep_a else []) + r_scr,
        compiler_params=_params(("arbitrary" if ride is not None else "parallel", "arbitrary", "arbitrary")),
    )(*ins)
    return outs[0] if ride is None else outs


def _is_heads(x):
    return not isinstance(x, tuple)


def _tok_arr(x):
    return x if _is_heads(x) else x[0]


def _tok_width(x):
    return x.shape[1] * x.shape[3] if _is_heads(x) else x[1]


def _heads_spec(h, dh, tm, tiles_per_seq):
    return pl.BlockSpec((None, h, tm, dh), lambda i: (i // tiles_per_seq, 0, i % tiles_per_seq, 0))


def _x_spec(x, tm, tiles_per_seq):
    if _is_heads(x):
        return _heads_spec(x.shape[1], x.shape[3], tm, tiles_per_seq)
    return pl.BlockSpec((tm, x[1]), functools.partial(lambda i, cb: (i, cb), cb=x[2]))


def _out_spec_shape(layout, n, seq, tm, dtype=F32):
    if isinstance(layout, tuple):
        h, dh = layout
        return _heads_spec(h, dh, tm, seq // tm), jax.ShapeDtypeStruct((n // seq, h, seq, dh), dtype)
    return pl.BlockSpec((tm, layout), lambda i: (i, 0)), jax.ShapeDtypeStruct((n, layout), dtype)


def _tok_load(ref):
    if len(ref.shape) == 3:
        return jnp.concatenate([ref[hh] for hh in range(ref.shape[0])], axis=1)
    return ref[...]


def _tok_store(ref, val):
    if len(ref.shape) == 3:
        dh = ref.shape[2]
        for hh in range(ref.shape[0]):
            ref[hh] = val[:, hh * dh:(hh + 1) * dh]
    else:
        ref[...] = val


def _p_specs(ps):
    return [pl.BlockSpec(p.shape, lambda i: (0, 0)) for p in ps]


def tok_fwd(name, f, xs, ps, out_layouts, tm, n, seq, out_dtype=F32):
    nx, npar = len(xs), len(ps)
    outs = [_out_spec_shape(lay, n, seq, tm, out_dtype) for lay in out_layouts]

    def body(*refs):
        xv = [_tok_load(r) for r in refs[:nx]]
        pv = [r[...].astype(F32) for r in refs[nx:nx + npar]]
        for o, r in zip(f(*xv, *pv), refs[nx + npar:]):
            _tok_store(r, o.astype(out_dtype))

    return pl.pallas_call(
        body, name=name, grid=(n // tm,),
        in_specs=[_x_spec(x, tm, seq // tm) for x in xs] + _p_specs(ps),
        out_specs=[o[0] for o in outs], out_shape=[o[1] for o in outs],
        compiler_params=_params(("parallel",)),
    )(*[_tok_arr(x) for x in xs], *ps)


def tok_bwd(name, f, xs, ps, cts, tm, n, seq, dx_layouts=None):
    nx, npar = len(xs), len(ps)
    ct_flat = [c for group in cts for c in group]
    nct = len(ct_flat)
    dx_layouts = dx_layouts or [None] * nx
    dxs = [_out_spec_shape(lay if lay else _tok_width(x), n, seq, tm) for x, lay in zip(xs, dx_layouts)]

    def body(*refs):
        xv = [_tok_load(r) for r in refs[:nx]]
        pv = [r[...].astype(F32) for r in refs[nx:nx + npar]]
        ct_refs = refs[nx + npar:nx + npar + nct]
        dx_refs = refs[nx + npar + nct:nx + npar + nct + nx]
        dp_refs = refs[nx + npar + nct + nx:]
        cv, pos = [], 0
        for group in cts:
            acc = _tok_load(ct_refs[pos])
            for r in ct_refs[pos + 1:pos + len(group)]:
                acc = acc + _tok_load(r)
            cv.append(acc)
            pos += len(group)
        _, vjp = jax.vjp(f, *xv, *pv)
        grads = vjp(tuple(cv))
        for g, r in zip(grads[:nx], dx_refs):
            _tok_store(r, g)

        @pl.when(pl.program_id(0) == 0)
        def _():
            for r in dp_refs:
                r[...] = jnp.zeros_like(r)

        for g, r in zip(grads[nx:], dp_refs):
            r[...] += g

    ct_specs = [_heads_spec(c.shape[1], c.shape[3], tm, seq // tm) if c.ndim == 4
                else pl.BlockSpec((tm, c.shape[1]), lambda i: (i, 0)) for c in ct_flat]
    outs = pl.pallas_call(
        body, name=name, grid=(n // tm,),
        in_specs=[_x_spec(x, tm, seq // tm) for x in xs] + _p_specs(ps) + ct_specs,
        out_specs=[d[0] for d in dxs] + _p_specs(ps),
        out_shape=[d[1] for d in dxs] + [jax.ShapeDtypeStruct(p.shape, F32) for p in ps],
        compiler_params=_params(("arbitrary",)),
    )(*[_tok_arr(x) for x in xs], *ps, *ct_flat)
    return outs[:nx], outs[nx:]


def f_ln(h, y, g, b):
    pre = ALPHA * h + y
    mu = jnp.mean(pre, axis=-1, keepdims=True)
    xc = pre - mu
    var = jnp.mean(xc * xc, axis=-1, keepdims=True)
    return (xc * lax.rsqrt(var + 1e-5) * g + b,)


def _head_sel(width, nheads_pad, per):
    return jnp.where(_iota((width, nheads_pad), 0) // per == _iota((width, nheads_pad), 1), 1.0, 0.0).astype(F32)


def _head_sel_t(nheads_pad, width, per):
    return jnp.where(_iota((nheads_pad, width), 1) // per == _iota((nheads_pad, width), 0), 1.0, 0.0).astype(F32)


@jax.custom_vjp
def head_sum(x):
    return _split3_dot(x, _head_sel(RWKV_DIM, LANES, HEAD_DIM), 0)


@jax.custom_vjp
def head_spread(y):
    return _split3_dot(y, _head_sel(RWKV_DIM, LANES, HEAD_DIM), 1)


head_sum.defvjp(lambda x: (head_sum(x), None), lambda _, ct: (head_spread(ct),))
head_spread.defvjp(lambda y: (head_spread(y), None), lambda _, ct: (head_sum(ct),))


def f_rwkv_pre(r, k, v, lora, glo, w0, w2p, a0, a2p, g2, k_k, k_a):
    lane = _iota(lora.shape, 1)
    tw = jnp.where(lane < 64, jnp.tanh(lora), 0.0)
    ta = jnp.where(lane >= 64, lora, 0.0)
    log_w = -_softplus(-(w0 + mmb(tw, w2p))) - 0.5
    lw = -jnp.exp(log_w)
    a = jax.nn.sigmoid(a0 + mmb(ta, a2p))
    g = mmb(jax.nn.sigmoid(glo), g2)
    kk = k * k_k
    nrm = jnp.sqrt(jnp.maximum(head_sum(kk * kk), 1e-24))
    kkn = kk * head_spread(1.0 / nrm)
    kmod = k * (1.0 + (a - 1.0) * k_a)
    return r, v, lw, kmod, -kkn, kkn * a, g


def f_rwkv_post(y, r, kmod, v, g, ln_g, ln_b, r_k):
    inv = 1.0 / HEAD_DIM
    mu = head_spread(head_sum(y) * inv)
    yc = y - mu
    var = head_sum(yc * yc) * inv
    rstd = head_spread(lax.rsqrt(var + RWKV_GN_EPS))
    yn = yc * rstd * ln_g + ln_b
    bonus = head_spread(head_sum(r * kmod * r_k)) * v
    return ((yn + bonus) * g,)


def f_ssd_post(y, xs, z, d_skip, norm_g):
    sel_t = _head_sel_t(LANES, SSM_DIM, HEAD_DIM)
    d_e = jnp.sum(mmf(jnp.broadcast_to(d_skip, (SUBLANES, LANES)), sel_t), axis=0, keepdims=True) * (1.0 / SUBLANES)
    u = (y + xs * d_e) * _silu(z)
    first = _iota(u.shape, 1) < (SSM_DIM // 2)
    uu = u * u
    inv = 2.0 / SSM_DIM
    ms0 = jnp.sum(jnp.where(first, uu, 0.0), axis=-1, keepdims=True) * inv
    ms1 = jnp.sum(jnp.where(first, 0.0, uu), axis=-1, keepdims=True) * inv
    ms = jnp.where(first, ms0, ms1)
    return (u * lax.rsqrt(ms + 1e-5) * norm_g,)


def f_mla_pre(cq, ckv, qg, wq_nope, wq_rope, kvg, wk_nope, wv):
    def rms(x, g):
        return x * lax.rsqrt(jnp.mean(x * x, axis=-1, keepdims=True) + 1e-6) * g
    q_in, kv_in = rms(cq, qg), rms(ckv, kvg)
    return mmb(q_in, wq_nope), mmb(q_in, wq_rope), mmb(kv_in, wk_nope), mmb(kv_in, wv)


def f_same(*xs):
    return xs


def f_concat(*xs):
    return (jnp.concatenate(xs, axis=1),)


CONV_TILE = 256


def _shift_down(x, s, row):
    return x if s == 0 else jnp.where(row >= s, pltpu.roll(x, s, 0), 0.0)


def _shift_up(x, s, row, t):
    return x if s == 0 else jnp.where(row < t - s, pltpu.roll(x, t - s, 0), 0.0)


def dwconv_fwd(name, u, colmap, w, b, taps, silu, upmap=None, out_dtype=F32):
    bsz, t, _ = u.shape
    c = w.shape[1]
    tc = CONV_TILE
    has_up = upmap is not None

    def body(*refs):
        u_ref, w_ref, b_ref = refs[:3]
        o_ref = refs[-1]
        uv = u_ref[...]
        wv = w_ref[...]
        row = _iota(uv.shape, 0)
        acc = jnp.broadcast_to(b_ref[...], uv.shape)
        for i in range(taps):
            acc = acc + wv[i:i + 1, :] * _shift_down(uv, taps - 1 - i, row)
        if silu:
            acc = _silu(acc)
        if has_up:
            acc = acc * refs[3][...]
        o_ref[...] = acc.astype(out_dtype)

    specs = [pl.BlockSpec((None, t, tc), lambda bb, j: (bb, 0, colmap(j))),
             pl.BlockSpec((taps, tc), lambda bb, j: (0, j)),
             pl.BlockSpec((1, tc), lambda bb, j: (0, j))]
    ins = [u, w, b]
    if has_up:
        specs.append(pl.BlockSpec((None, t, tc), lambda bb, j: (bb, 0, upmap(j))))
        ins.append(u)
    return pl.pallas_call(
        body, name=name, grid=(bsz, c // tc), in_specs=specs,
        out_specs=pl.BlockSpec((None, t, tc), lambda bb, j: (bb, 0, j)),
        out_shape=jax.ShapeDtypeStruct((bsz, t, c), out_dtype),
        compiler_params=_params(("parallel", "parallel")),
    )(*ins)


def dwconv_bwd(name, u, colmap, w, b, taps, silu, dout, upmap=None, grad_dtype=F32):
    bsz, t, _ = u.shape
    c = w.shape[1]
    tc = CONV_TILE
    has_up = upmap is not None

    def body(*refs):
        u_ref, w_ref, b_ref, d_ref = refs[:4]
        nin = 5 if has_up else 4
        du_ref, dw_ref, db_ref = refs[nin:nin + 3]
        uv = u_ref[...]
        wv = w_ref[...]
        dv = d_ref[...]
        row = _iota(uv.shape, 0)
        shifted = [_shift_down(uv, taps - 1 - i, row) for i in range(taps)]
        cg = jnp.broadcast_to(b_ref[...], uv.shape)
        for i in range(taps):
            cg = cg + wv[i:i + 1, :] * shifted[i]
        if silu:
            sg = jax.nn.sigmoid(cg)
            act = cg * sg
            dact_dcg = sg * (1.0 + cg * (1.0 - sg))
        else:
            act = cg
            dact_dcg = None
        if has_up:
            refs[nin + 3][...] = (dv * act).astype(grad_dtype)
            dv = dv * refs[4][...]
        dcg = dv * dact_dcg if silu else dv
        du = jnp.zeros_like(uv)
        for i in range(taps):
            du = du + wv[i:i + 1, :] * _shift_up(dcg, taps - 1 - i, row, t)
        du_ref[...] = du.astype(grad_dtype)

        @pl.when(pl.program_id(1) == 0)
        def _():
            dw_ref[...] = jnp.zeros_like(dw_ref)
            db_ref[...] = jnp.zeros_like(db_ref)

        for i in range(taps):
            dw_ref[i:i + 1, :] += jnp.sum(dcg * shifted[i], axis=0, keepdims=True)
        db_ref[...] += jnp.sum(dcg, axis=0, keepdims=True)

    specs = [pl.BlockSpec((None, t, tc), lambda j, bb: (bb, 0, colmap(j))),
             pl.BlockSpec((taps, tc), lambda j, bb: (0, j)),
             pl.BlockSpec((1, tc), lambda j, bb: (0, j)),
             pl.BlockSpec((None, t, tc), lambda j, bb: (bb, 0, j))]
    ins = [u, w, b, dout]
    if has_up:
        specs.append(pl.BlockSpec((None, t, tc), lambda j, bb: (bb, 0, upmap(j))))
        ins.append(u)
    big = pl.BlockSpec((None, t, tc), lambda j, bb: (bb, 0, j))
    out_specs = [big, pl.BlockSpec((taps, tc), lambda j, bb: (0, j)), pl.BlockSpec((1, tc), lambda j, bb: (0, j))]
    out_shape = [jax.ShapeDtypeStruct((bsz, t, c), grad_dtype), jax.ShapeDtypeStruct((taps, c), F32),
                 jax.ShapeDtypeStruct((1, c), F32)]
    if has_up:
        out_specs.append(big)
        out_shape.append(jax.ShapeDtypeStruct((bsz, t, c), grad_dtype))
    return pl.pallas_call(
        body, name=name, grid=(c // tc, bsz), in_specs=specs, out_specs=out_specs, out_shape=out_shape,
        compiler_params=_params(("parallel", "arbitrary")),
    )(*ins)


def _each(f, *lists):
    return [f(*xs) for xs in zip(*lists)]


def rwkv_chunk(s0, r, lw, k, v, al, be):
    c = r[0].shape[0]
    ii, jj = _iota((c, c), 0), _iota((c, c), 1)
    incl, strict = ii >= jj, ii > jj
    ones_incl = jnp.where(incl, 1.0, 0.0)
    eye = jnp.where(ii == jj, 1.0, 0.0)
    cum = _each(lambda x: mmf(ones_incl, x), lw)
    gam_inv = _each(lambda x: jnp.exp(-x), cum)
    at = _each(lambda a_, c_, l_: a_ * jnp.exp(c_ - l_), al, cum, lw)
    rt = _each(lambda r_, c_: r_ * jnp.exp(c_), r, cum)
    bt = _each(lambda b_, g_: b_ * g_, be, gam_inv)
    kt = _each(lambda k_, g_: k_ * g_, k, gam_inv)
    a_b = _each(lambda x, y_: jnp.where(strict, mm3_nt(x, y_), 0.0), at, bt)
    a_k = _each(lambda x, y_: jnp.where(strict, mm3_nt(x, y_), 0.0), at, kt)
    rhs0 = _each(mm3_nt, at, s0)
    rhs = _each(lambda x, a_, v_: x + mm3(a_, v_), rhs0, a_k, v)
    p = _each(lambda x: eye + x, a_b)
    m = a_b
    for _ in range(int(math.log2(c)) - 1):
        m = _each(mm3, m, m)
        p = _each(lambda p_, m_: p_ + mm3(p_, m_), p, m)
    u = _each(mm3, p, rhs)
    r_b = _each(lambda x, y_: jnp.where(incl, mm3_nt(x, y_), 0.0), rt, bt)
    r_k = _each(lambda x, y_: jnp.where(incl, mm3_nt(x, y_), 0.0), rt, kt)
    y0 = _each(mm3_nt, rt, s0)
    y1 = _each(lambda y_, b_, u_: y_ + mm3(b_, u_), y0, r_b, u)
    y = _each(lambda y_, k_, v_: y_ + mm3(k_, v_), y1, r_k, v)
    su = _each(mm3_tn, u, bt)
    sv = _each(mm3_tn, v, kt)
    s1 = _each(lambda s_, a_, b_, l_: (s_ + a_ + b_) * jnp.exp(jnp.sum(l_, axis=0, keepdims=True)), s0, su, sv, lw)
    return y, s1


def rwkv_scan_fwd(r, lw, k, v, al, be, ride=None):
    bsz, h, t, d = r.shape
    c = RWKV_CHUNK
    nc = t // c
    grid = (bsz, nc)
    r_in, r_specs, r_out, r_ospecs, r_scr = _ride_args(ride)

    def body(*refs):
        r_ref, lw_ref, k_ref, v_ref, al_ref, be_ref = refs[:6]
        y_ref, st_ref = refs[6 + len(r_in):8 + len(r_in)]
        s_scr = refs[8 + 2 * len(r_in)]
        if ride is not None:
            first, last = _grid_first_last(grid)
            copies = _ride_start(refs[6:6 + len(r_in)], refs[8 + len(r_in):8 + 2 * len(r_in)], refs[-3:], ride[1], first)

        @pl.when(pl.program_id(1) == 0)
        def _():
            s_scr[...] = jnp.zeros_like(s_scr)

        heads = lambda ref: [ref[hh] for hh in range(h)]
        s0 = heads(s_scr)
        y, s1 = rwkv_chunk(s0, heads(r_ref), heads(lw_ref), heads(k_ref), heads(v_ref), heads(al_ref),
                           heads(be_ref))
        for hh in range(h):
            st_ref[hh] = s0[hh]
            y_ref[hh] = y[hh]
            s_scr[hh] = s1[hh]
        if ride is not None:
            _ride_wait(copies, last)

    seq = pl.BlockSpec((None, h, c, d), lambda b, i: (b, 0, i, 0))
    return pl.pallas_call(
        body, name="rwkv_scan_fwd", grid=grid, in_specs=[seq] * 6 + r_specs,
        out_specs=[seq, pl.BlockSpec((None, h, None, d, d), lambda b, i: (b, 0, i, 0, 0))] + r_ospecs,
        out_shape=[jax.ShapeDtypeStruct((bsz, h, t, d), F32), jax.ShapeDtypeStruct((bsz, h, nc, d, d), F32)] + r_out,
        scratch_shapes=[pltpu.VMEM((h, d, d), F32)] + r_scr,
        compiler_params=_params(("arbitrary", "arbitrary")),
    )(r, lw, k, v, al, be, *r_in)


def rwkv_scan_bwd(r, lw, k, v, al, be, states, dy, ride=None):
    bsz, h, t, d = r.shape
    c = RWKV_CHUNK
    nc = t // c
    grid = (bsz, nc)
    r_in, r_specs, r_out, r_ospecs, r_scr = _ride_args(ride)

    def body(*refs):
        r_ref, lw_ref, k_ref, v_ref, al_ref, be_ref, st_ref, dy_ref = refs[:8]
        nin = 8 + len(r_in)
        dr_ref, dlw_ref, dk_ref, dv_ref, dal_ref, dbe_ref = refs[nin:nin + 6]
        ds_scr = refs[nin + 6 + len(r_in)]
        if ride is not None:
            first, last = _grid_first_last(grid)
            copies = _ride_start(refs[8:nin], refs[nin + 6:nin + 6 + len(r_in)], refs[-3:], ride[1], first)

        @pl.when(pl.program_id(1) == 0)
        def _():
            ds_scr[...] = jnp.zeros_like(ds_scr)

        heads = lambda ref: [ref[hh] for hh in range(h)]
        _, vjp = jax.vjp(rwkv_chunk, heads(st_ref), heads(r_ref), heads(lw_ref), heads(k_ref), heads(v_ref),
                         heads(al_ref), heads(be_ref))
        grads = vjp((heads(dy_ref), heads(ds_scr)))
        for ref, gl in zip((ds_scr, dr_ref, dlw_ref, dk_ref, dv_ref, dal_ref, dbe_ref), grads):
            for hh in range(h):
                ref[hh] = gl[hh]
        if ride is not None:
            _ride_wait(copies, last)

    seq = pl.BlockSpec((None, h, c, d), lambda b, i: (b, 0, nc - 1 - i, 0))
    st = pl.BlockSpec((None, h, None, d, d), lambda b, i: (b, 0, nc - 1 - i, 0, 0))
    return pl.pallas_call(
        body, name="rwkv_scan_bwd", grid=grid, in_specs=[seq] * 6 + [st, seq] + r_specs,
        out_specs=[seq] * 6 + r_ospecs, out_shape=[jax.ShapeDtypeStruct((bsz, h, t, d), F32)] * 6 + r_out,
        scratch_shapes=[pltpu.VMEM((h, d, d), F32)] + r_scr,
        compiler_params=_params(("arbitrary", "arbitrary")),
    )(r, lw, k, v, al, be, states, dy, *r_in)


def ssd_chunk(st, xs, bm, cm, dtr, dt_bias, a_log):
    n = SSM_CHUNK
    ii, jj = _iota((n, n), 0), _iota((n, n), 1)
    incl = ii >= jj
    lane = _iota((n, LANES), 1)
    dt = _softplus(dtr + dt_bias)
    a = dt * (-jnp.exp(a_log))
    acum = mmf(jnp.where(incl, 1.0, 0.0), a)
    last_row = jnp.where(jj == n - 1, 1.0, 0.0)
    cb = [mm3_nt(cm[g], bm[g]) for g in range(2)]
    pairs, heads = range(4), range(SSM_HEADS)
    e_m = [jnp.where(_iota((LANES, LANES), 0) == 2 * m + _iota((LANES, LANES), 1) // HEAD_DIM, 1.0, 0.0)
           for m in pairs]
    dt_m = [mmf(dt, e_m[m]) for m in pairs]
    ac_m = [mmf(acum, e_m[m]) for m in pairs]
    x = [xs[m] * dt_m[m] for m in pairs]
    last_m = [mmf(last_row, ac_m[m]) for m in pairs]
    colb = [mmf(acum, jnp.where(_iota((LANES, n), 0) == h, 1.0, 0.0)) for h in heads]
    decay = [jnp.exp(jnp.where(incl, colb[h] - colb[h].T, NEG_BIG)) for h in heads]
    yh = [mm3(cb[h // 4] * decay[h], x[h // 2]) for h in heads]
    y_off = [mm3(cm[m // 2], st[m]) for m in pairs]
    ys = [jnp.where(lane // HEAD_DIM == 0, yh[2 * m], yh[2 * m + 1]) + jnp.exp(ac_m[m]) * y_off[m] for m in pairs]
    st_in = [mm3_tn(bm[m // 2], x[m] * jnp.exp(last_m[m] - ac_m[m])) for m in pairs]
    st_new = [jnp.exp(last_m[m]) * st[m] + st_in[m] for m in pairs]
    return tuple(ys), tuple(st_new)


def _ssd_load(xbc_ref, dtr_ref):
    xs = tuple(xbc_ref[:, m * LANES:(m + 1) * LANES] for m in range(4))
    bm = tuple(xbc_ref[:, SSM_DIM + g * LANES:SSM_DIM + (g + 1) * LANES] for g in range(2))
    cm = tuple(xbc_ref[:, SSM_DIM + 2 * LANES + g * LANES:SSM_DIM + 2 * LANES + (g + 1) * LANES] for g in range(2))
    return xs, bm, cm, dtr_ref[...]


def ssd_fwd(xbc, proj, dt_col, dt_bias, a_log):
    bsz, t, _ = xbc.shape
    n = SSM_CHUNK
    nc = t // n

    def body(xbc_ref, dtr_ref, dtb_ref, al_ref, y_ref, st_ref, s_scr):
        @pl.when(pl.program_id(1) == 0)
        def _():
            s_scr[...] = jnp.zeros_like(s_scr)

        st = tuple(s_scr[m] for m in range(4))
        for m in range(4):
            st_ref[m] = st[m]
        xs, bm, cm, dtr = _ssd_load(xbc_ref, dtr_ref)
        ys, st_new = ssd_chunk(st, xs, bm, cm, dtr, dtb_ref[...], al_ref[...])
        for m in range(4):
            y_ref[:, m * LANES:(m + 1) * LANES] = ys[m]
            s_scr[m] = st_new[m]

    vec = pl.BlockSpec((1, LANES), lambda b, i: (0, 0))
    return pl.pallas_call(
        body, name="ssd_fwd", grid=(bsz, nc),
        in_specs=[pl.BlockSpec((None, n, 2 * SSM_DIM), lambda b, i: (b, i, 0)),
                  pl.BlockSpec((None, n, LANES), lambda b, i: (b, i, dt_col)), vec, vec],
        out_specs=[pl.BlockSpec((None, n, SSM_DIM), lambda b, i: (b, i, 0)),
                   pl.BlockSpec((None, None, 4, SSM_STATE, LANES), lambda b, i: (b, i, 0, 0, 0))],
        out_shape=[jax.ShapeDtypeStruct((bsz, t, SSM_DIM), F32),
                   jax.ShapeDtypeStruct((bsz, nc, 4, SSM_STATE, LANES), F32)],
        scratch_shapes=[pltpu.VMEM((4, SSM_STATE, LANES), F32)],
        compiler_params=_params(("parallel", "arbitrary")),
    )(xbc, proj, dt_bias, a_log)


def ssd_bwd(xbc, proj, dt_col, dt_bias, a_log, states, dy, dxs_extra):
    bsz, t, _ = xbc.shape
    n = SSM_CHUNK
    nc = t // n

    def body(xbc_ref, dtr_ref, dtb_ref, al_ref, st_ref, dy_ref, ex_ref,
             dxbc_ref, ddtr_ref, ddtb_ref, dal_ref, ds_scr):
        first = jnp.logical_and(pl.program_id(0) == 0, pl.program_id(1) == 0)

        @pl.when(pl.program_id(1) == 0)
        def _():
            ds_scr[...] = jnp.zeros_like(ds_scr)

        @pl.when(first)
        def _():
            ddtb_ref[...] = jnp.zeros_like(ddtb_ref)
            dal_ref[...] = jnp.zeros_like(dal_ref)

        st = tuple(st_ref[m] for m in range(4))
        xs, bm, cm, dtr = _ssd_load(xbc_ref, dtr_ref)
        _, vjp = jax.vjp(ssd_chunk, st, xs, bm, cm, dtr, dtb_ref[...], al_ref[...])
        dys = tuple(dy_ref[:, m * LANES:(m + 1) * LANES] for m in range(4))
        dst_in = tuple(ds_scr[m] for m in range(4))
        dst, dxs, dbm, dcm, ddtr, ddtb, dal = vjp((dys, dst_in))
        for m in range(4):
            ds_scr[m] = dst[m]
            sl = slice(m * LANES, (m + 1) * LANES)
            dxbc_ref[:, sl] = dxs[m] + ex_ref[:, sl]
        for g in range(2):
            dxbc_ref[:, SSM_DIM + g * LANES:SSM_DIM + (g + 1) * LANES] = dbm[g]
            dxbc_ref[:, SSM_DIM + 2 * LANES + g * LANES:SSM_DIM + 2 * LANES + (g + 1) * LANES] = dcm[g]
        ddtr_ref[...] = ddtr
        ddtb_ref[...] += ddtb
        dal_ref[...] += dal

    vec = pl.BlockSpec((1, LANES), lambda b, i: (0, 0))
    rev = lambda b, i: (b, nc - 1 - i, 0)
    return pl.pallas_call(
        body, name="ssd_bwd", grid=(bsz, nc),
        in_specs=[pl.BlockSpec((None, n, 2 * SSM_DIM), rev),
                  pl.BlockSpec((None, n, LANES), lambda b, i: (b, nc - 1 - i, dt_col)), vec, vec,
                  pl.BlockSpec((None, None, 4, SSM_STATE, LANES), lambda b, i: (b, nc - 1 - i, 0, 0, 0)),
                  pl.BlockSpec((None, n, SSM_DIM), rev), pl.BlockSpec((None, n, SSM_DIM), rev)],
        out_specs=[pl.BlockSpec((None, n, 2 * SSM_DIM), rev), pl.BlockSpec((None, n, LANES), rev), vec, vec],
        out_shape=[jax.ShapeDtypeStruct((bsz, t, 2 * SSM_DIM), F32), jax.ShapeDtypeStruct((bsz, t, LANES), F32),
                   jax.ShapeDtypeStruct((1, LANES), F32), jax.ShapeDtypeStruct((1, LANES), F32)],
        scratch_shapes=[pltpu.VMEM((4, SSM_STATE, LANES), F32)],
        compiler_params=_params(("arbitrary", "arbitrary")),
    )(xbc, proj, dt_bias, a_log, states, dy, dxs_extra)


def sb_block(q, kj, vj, carry, maskf):
    mask = maskf > 0.5
    z = _each(lambda q_, k_: dg(q_, k_, 1, 1, P_BF16) * (HEAD_DIM ** -0.5), q, kj)
    lk = _each(lambda z_: jnp.where(mask, -_softplus(z_), 0.0), z)
    sfx = _each(suffix_sum, lk)
    att = _each(lambda z_, c_, s_: jnp.exp(jnp.where(mask, z_ + c_ + s_, NEG_BIG)), z, carry, sfx)
    out = _each(mmb, att, vj)
    return out, _each(lambda c_, k_: c_ + jnp.sum(k_, axis=1, keepdims=True), carry, lk)


def _sb_mask(qi, j):
    n = Q_BLOCK
    return jnp.where(j * n + _iota((n, n), 1) < qi * n + _iota((n, n), 0), 1.0, 0.0)


def sb_fwd(q, k, v, ride=None):
    bsz, h, t, d = q.shape
    n = Q_BLOCK
    hp = ATTN_FWD_HEADS_PER_STEP
    grid = (bsz, h // hp, t // n)
    r_in, r_specs, r_out, r_ospecs, r_scr = _ride_args(ride)

    def body(*refs):
        q_ref, k_ref, v_ref = refs[:3]
        o_ref, c_ref = refs[3 + len(r_in):5 + len(r_in)]
        if ride is not None:
            first, last = _grid_first_last(grid)
            copies = _ride_start(refs[3:3 + len(r_in)], refs[5 + len(r_in):5 + 2 * len(r_in)], refs[-3:], ride[1], first)
        qi = pl.program_id(2)
        lane = _iota((n, LANES), 1)

        c_ref[...] = jnp.zeros_like(c_ref)
        o_ref[...] = jnp.zeros_like(o_ref)

        def step(i, carry):
            j = qi - i
            rows = pl.ds(pl.multiple_of(j * n, n), n)
            for hh in range(hp):
                c_ref[hh] = jnp.where(lane == j, carry[hh], c_ref[hh])
            o, carry = sb_block([q_ref[hh] for hh in range(hp)], [k_ref[hh, rows, :] for hh in range(hp)],
                                [v_ref[hh, rows, :] for hh in range(hp)], carry, _sb_mask(qi, j))
            for hh in range(hp):
                o_ref[hh] += o[hh]
            return carry

        lax.fori_loop(0, qi + 1, step, [jnp.zeros((n, 1), F32) for _ in range(hp)])
        if ride is not None:
            _ride_wait(copies, last)

    blk = pl.BlockSpec((None, hp, n, d), lambda b, hg, i: (b, hg, i, 0))
    cblk = pl.BlockSpec((None, hp, n, LANES), lambda b, hg, i: (b, hg, i, 0))
    full = pl.BlockSpec((None, hp, t, d), lambda b, hg, i: (b, hg, 0, 0))
    return pl.pallas_call(
        body, name="sb_fwd", grid=grid, in_specs=[blk, full, full] + r_specs, out_specs=[blk, cblk] + r_ospecs,
        out_shape=[jax.ShapeDtypeStruct((bsz, h, t, d), F32), jax.ShapeDtypeStruct((bsz, h, t, LANES), F32)] + r_out,
        scratch_shapes=r_scr, compiler_params=_params(("arbitrary", "arbitrary", "arbitrary")),
    )(q, k, v, *r_in)


def sb_bwd(q, k, v, kept, do, ride=None):
    bsz, h, t, d = q.shape
    n = Q_BLOCK
    hp = SB_HEADS_PER_STEP
    grid = (bsz, h // hp, t // n)
    r_in, r_specs, r_out, r_ospecs, r_scr = _ride_args(ride)

    def body(*refs):
        q_ref, k_ref, v_ref, c_ref, do_ref = refs[:5]
        nin = 5 + len(r_in)
        dq_ref, dk_ref, dv_ref = refs[nin:nin + 3]
        if ride is not None:
            first, last = _grid_first_last(grid)
            copies = _ride_start(refs[5:nin], refs[nin + 3:nin + 3 + len(r_in)], refs[-3:], ride[1], first)
        qi = pl.program_id(2)

        @pl.when(qi == 0)
        def _():
            dk_ref[...] = jnp.zeros_like(dk_ref)
            dv_ref[...] = jnp.zeros_like(dv_ref)

        heads = range(hp)
        qv = [q_ref[hh] for hh in heads]
        kept_v = [c_ref[hh] for hh in heads]
        lane = _iota((n, LANES), 1)

        dq_ref[...] = jnp.zeros_like(dq_ref)

        def bwd_step(j, dcarry):
            rows = pl.ds(pl.multiple_of(j * n, n), n)
            carry_in = [jnp.sum(jnp.where(lane == j, t_, 0.0), axis=1, keepdims=True) for t_ in kept_v]
            _, vjp = jax.vjp(sb_block, qv, [k_ref[hh, rows, :] for hh in heads],
                             [v_ref[hh, rows, :] for hh in heads], carry_in, _sb_mask(qi, j))
            dqj, dkj, dvj, dc, _ = vjp(([do_ref[hh] for hh in heads], dcarry))
            for hh in heads:
                dq_ref[hh] += dqj[hh]
                dk_ref[hh, rows, :] += dkj[hh]
                dv_ref[hh, rows, :] += dvj[hh]
            return dc

        lax.fori_loop(0, qi + 1, bwd_step, [jnp.zeros((n, 1), F32) for _ in heads])
        if ride is not None:
            _ride_wait(copies, last)

    blk = pl.BlockSpec((None, hp, n, d), lambda b, hg, i: (b, hg, i, 0))
    cblk = pl.BlockSpec((None, hp, n, LANES), lambda b, hg, i: (b, hg, i, 0))
    full = pl.BlockSpec((None, hp, t, d), lambda b, hg, i: (b, hg, 0, 0))
    shp = jax.ShapeDtypeStruct((bsz, h, t, d), F32)
    return pl.pallas_call(
        body, name="sb_bwd", grid=grid, in_specs=[blk, full, full, cblk, blk] + r_specs,
        out_specs=[blk, full, full] + r_ospecs, out_shape=[shp, shp, shp] + r_out,
        scratch_shapes=r_scr, compiler_params=_params(("arbitrary", "arbitrary", "arbitrary")),
    )(q, k, v, kept, do, *r_in)


def _bdot(a, b, ca, cb):
    return _dg_raw(a, b, ca, cb, P_BF16)


def _mla_scores(qn, qp, knj, kpj, qi, j):
    n = Q_BLOCK
    mask = j * n + _iota((n, n), 1) <= qi * n + _iota((n, n), 0)
    scale = (MLA_NOPE + MLA_ROPE) ** -0.5
    return _each(lambda a_, b_, k_: jnp.where(mask, (_bdot(a_, k_, 1, 1) + _bdot(b_, kpj, 1, 1)) * scale, NEG_BIG),
                 qn, qp, knj)


def _mla_specs(t, hp):
    n = Q_BLOCK
    return (pl.BlockSpec((None, hp, n, MLA_NOPE), lambda b, hg, i: (b, hg, i, 0)),
            pl.BlockSpec((None, hp, n, MLA_ROPE), lambda b, hg, i: (b, hg, i, 0)),
            pl.BlockSpec((None, hp, t, MLA_NOPE), lambda b, hg, i: (b, hg, 0, 0)),
            pl.BlockSpec((None, None, t, MLA_ROPE), lambda b, hg, i: (b, 0, 0, 0)),
            pl.BlockSpec((None, hp, n, 1), lambda b, hg, i: (b, hg, i, 0)))


def mla_fwd(qn, qp, kn, kp, v):
    bsz, h, t, _ = qn.shape
    n, hp = Q_BLOCK, ATTN_FWD_HEADS_PER_STEP
    heads = range(hp)

    def body(qn_ref, qp_ref, kn_ref, kp_ref, v_ref, o_ref, lse_ref):
        qi = pl.program_id(2)
        qn_v, qp_v = [qn_ref[hh] for hh in heads], [qp_ref[hh] for hh in heads]

        o_ref[...] = jnp.zeros_like(o_ref)

        def step(j, state):
            m, l = state
            rows = pl.ds(pl.multiple_of(j * n, n), n)
            s = _mla_scores(qn_v, qp_v, [kn_ref[hh, rows, :] for hh in heads], kp_ref[rows, :], qi, j)
            m_new = _each(lambda m_, s_: jnp.maximum(m_, jnp.max(s_, axis=1, keepdims=True)), m, s)
            p = _each(lambda s_, m_: jnp.exp(s_ - m_), s, m_new)
            corr = _each(lambda a_, b_: jnp.exp(a_ - b_), m, m_new)
            l = _each(lambda l_, c_, p_: l_ * c_ + jnp.sum(p_, axis=1, keepdims=True), l, corr, p)
            pv = _each(lambda p_, v_: _bdot(p_, v_, 1, 0), p, [v_ref[hh, rows, :] for hh in heads])
            for hh in heads:
                o_ref[hh] = o_ref[hh] * corr[hh] + pv[hh]
            return m_new, l

        init = ([jnp.full((n, 1), NEG_BIG, F32) for _ in heads], [jnp.zeros((n, 1), F32) for _ in heads])
        m, l = lax.fori_loop(0, qi + 1, step, init)
        for hh in heads:
            o_ref[hh] = o_ref[hh] / l[hh]
            lse_ref[hh] = m[hh] + jnp.log(l[hh])

    qn_s, qp_s, kn_s, kp_s, row_s = _mla_specs(t, hp)
    return pl.pallas_call(
        body, name="mla_fwd", grid=(bsz, h // hp, t // n), in_specs=[qn_s, qp_s, kn_s, kp_s, kn_s],
        out_specs=[qn_s, row_s],
        out_shape=[jax.ShapeDtypeStruct(qn.shape, F32), jax.ShapeDtypeStruct((bsz, h, t, 1), F32)],
        compiler_params=_params(("parallel", "parallel", "arbitrary")),
    )(qn, qp, kn, kp, v)


def mla_bwd(qn, qp, kn, kp, v, o, lse, do):
    bsz, h, t, _ = qn.shape
    n, hp = Q_BLOCK, MLA_HEADS_PER_STEP
    heads = range(hp)
    scale = (MLA_NOPE + MLA_ROPE) ** -0.5

    def body(qn_ref, qp_ref, kn_ref, kp_ref, v_ref, o_ref, lse_ref, do_ref,
             dqn_ref, dqp_ref, dkn_ref, dkp_ref, dv_ref):
        hg, qi = pl.program_id(1), pl.program_id(2)

        @pl.when(qi == 0)
        def _():
            dkn_ref[...] = jnp.zeros_like(dkn_ref)
            dv_ref[...] = jnp.zeros_like(dv_ref)

        @pl.when(jnp.logical_and(qi == 0, hg == 0))
        def _():
            dkp_ref[...] = jnp.zeros_like(dkp_ref)

        qn_v, qp_v = [qn_ref[hh] for hh in heads], [qp_ref[hh] for hh in heads]
        do_v, lse_v = [do_ref[hh] for hh in heads], [lse_ref[hh] for hh in heads]
        dsum = [jnp.sum(do_v[hh] * o_ref[hh], axis=1, keepdims=True) for hh in heads]

        dqn_ref[...] = jnp.zeros_like(dqn_ref)
        dqp_ref[...] = jnp.zeros_like(dqp_ref)

        def step(j, _):
            rows = pl.ds(pl.multiple_of(j * n, n), n)
            knj, vj, kpj = [kn_ref[hh, rows, :] for hh in heads], [v_ref[hh, rows, :] for hh in heads], kp_ref[rows, :]
            s = _mla_scores(qn_v, qp_v, knj, kpj, qi, j)
            p = _each(lambda s_, l_: jnp.exp(s_ - l_), s, lse_v)
            dp = _each(lambda d_, v_: _bdot(d_, v_, 1, 1), do_v, vj)
            ds = _each(lambda p_, dp_, d_: p_ * (dp_ - d_) * scale, p, dp, dsum)
            dqn = _each(lambda ds_, k_: _bdot(ds_, k_, 1, 0), ds, knj)
            dqp = _each(lambda ds_: _bdot(ds_, kpj, 1, 0), ds)
            dkn = _each(lambda ds_, q_: _bdot(ds_, q_, 0, 0), ds, qn_v)
            dv = _each(lambda p_, d_: _bdot(p_, d_, 0, 0), p, do_v)
            dkp = _each(lambda ds_, q_: _bdot(ds_, q_, 0, 0), ds, qp_v)
            for hh in heads:
                dqn_ref[hh] += dqn[hh]
                dqp_ref[hh] += dqp[hh]
                dkn_ref[hh, rows, :] += dkn[hh]
                dv_ref[hh, rows, :] += dv[hh]
            dkp_ref[rows, :] += functools.reduce(lambda a_, b_: a_ + b_, dkp)
            return 0

        lax.fori_loop(0, qi + 1, step, 0)

    qn_s, qp_s, kn_s, kp_s, row_s = _mla_specs(t, hp)
    return pl.pallas_call(
        body, name="mla_bwd", grid=(bsz, h // hp, t // n),
        in_specs=[qn_s, qp_s, kn_s, kp_s, kn_s, qn_s, row_s, qn_s],
        out_specs=[qn_s, qp_s, kn_s, kp_s, kn_s],
        out_shape=[jax.ShapeDtypeStruct(qn.shape, F32), jax.ShapeDtypeStruct(qp.shape, F32),
                   jax.ShapeDtypeStruct(kn.shape, F32), jax.ShapeDtypeStruct(kp.shape, F32),
                   jax.ShapeDtypeStruct(v.shape, F32)],
        compiler_params=_params(("parallel", "arbitrary", "arbitrary")),
    )(qn, qp, kn, kp, v, o, lse, do)


def rope(name, x, pos, inv_freq, sign):
    bsz, hx, t, d = x.shape
    half = d // 2

    tt = _largest_tile(t, 512, SUBLANES)

    def body(x_ref, pos_ref, f_ref, o_ref):
        ang = pos_ref[...].astype(F32) * f_ref[...]
        cos, sin = jnp.cos(ang), sign * jnp.sin(ang)
        ri, ci = _iota((d, d), 0), _iota((d, d), 1)
        rot = jnp.where(ri == ci + half, -1.0, 0.0) + jnp.where(ri + half == ci, 1.0, 0.0)
        for hh in range(hx):
            xv = x_ref[hh]
            o_ref[hh] = xv * cos + mmf(xv, rot) * sin

    blk = pl.BlockSpec((None, hx, tt, d), lambda b, i: (b, 0, i, 0))
    return pl.pallas_call(
        body, name=name, grid=(bsz, t // tt),
        in_specs=[blk, pl.BlockSpec((None, tt, 1), lambda b, i: (b, i, 0)), pl.BlockSpec((1, d), lambda b, i: (0, 0))],
        out_specs=blk, out_shape=jax.ShapeDtypeStruct(x.shape, F32),
        compiler_params=_params(("parallel", "parallel")),
    )(x, pos, inv_freq)


def loss_head(h, target, tm):
    n, d = h.shape

    def body(h_ref, t_ref, dh_ref, l_ref):
        @pl.when(pl.program_id(0) == 0)
        def _():
            l_ref[...] = jnp.zeros_like(l_ref)

        e = h_ref[...] - t_ref[...]
        dh_ref[...] = e * (1.0 / d)
        l_ref[...] += jnp.sum(e * e, axis=(0, 1), keepdims=True) * (0.5 / d)

    row = pl.BlockSpec((tm, d), lambda i: (i, 0))
    dh, l = pl.pallas_call(
        body, name="loss_head", grid=(n // tm,), in_specs=[row, row],
        out_specs=[row, pl.BlockSpec((SUBLANES, LANES), lambda i: (0, 0))],
        out_shape=[jax.ShapeDtypeStruct((n, d), F32), jax.ShapeDtypeStruct((SUBLANES, LANES), F32)],
        compiler_params=_params(("arbitrary",)),
    )(h, target)
    return dh, l[0, 0]


def _exchange_copies(src_refs, out_refs, send_sems, recv_sems, local_sems, gather):
    x, y, c = lax.axis_index("x"), lax.axis_index("y"), lax.axis_index("c")
    me = 4 * x + 2 * y + c
    local, remote = [], []
    for p, (src_ref, out_ref) in enumerate(zip(src_refs, out_refs)):
        local.append(pltpu.make_async_copy(src_ref if gather else src_ref.at[me], out_ref.at[me], local_sems.at[p]))
        for m in range(1, N_DEV):
            px, py, pc = x ^ (m >> 2), y ^ ((m >> 1) & 1), c ^ (m & 1)
            peer = 4 * px + 2 * py + pc
            remote.append(pltpu.make_async_remote_copy(
                src_ref=src_ref if gather else src_ref.at[peer], dst_ref=out_ref.at[me],
                send_sem=send_sems.at[p, m], recv_sem=recv_sems.at[p, m],
                device_id=(px, py, pc), device_id_type=pl.DeviceIdType.MESH))
    return local, remote


def _exchange_start(copies):
    local, remote = copies
    for cp in local + remote:
        cp.start()


def _exchange_wait(copies):
    local, remote = copies
    for cp in remote:
        cp.wait_recv()
    for cp in remote:
        cp.wait_send()
    for cp in local:
        cp.wait()


def _exchange_scratch(count):
    return [pltpu.SemaphoreType.DMA((count, N_DEV)), pltpu.SemaphoreType.DMA((count, N_DEV)),
            pltpu.SemaphoreType.DMA((count,))]


def _exchange_out(src, gather):
    return jax.ShapeDtypeStruct(((N_DEV,) + src.shape) if gather else src.shape, src.dtype)


def peer_exchange(name, srcs, gather):
    count = len(srcs)

    def body(*refs):
        copies = _exchange_copies(refs[:count], refs[count:2 * count], *refs[2 * count:], gather)
        _exchange_start(copies)
        _exchange_wait(copies)

    hbm = pl.BlockSpec(memory_space=pl.ANY)
    return pl.pallas_call(
        body, name=name, in_specs=[hbm] * count, out_specs=[hbm] * count,
        out_shape=[_exchange_out(s_, gather) for s_ in srcs], scratch_shapes=_exchange_scratch(count),
    )(*srcs)


def _grid_first_last(grid):
    ids = [pl.program_id(a) for a in range(len(grid))]
    first = functools.reduce(jnp.logical_and, [i == 0 for i in ids])
    last = functools.reduce(jnp.logical_and, [i == g - 1 for i, g in zip(ids, grid)])
    return first, last


def _ride_start(src_refs, out_refs, sem_refs, gather, first):
    copies = _exchange_copies(src_refs, out_refs, *sem_refs, gather)

    @pl.when(first)
    def _():
        _exchange_start(copies)

    return copies


def _ride_wait(copies, last):
    @pl.when(last)
    def _():
        _exchange_wait(copies)


def _ride_args(ride):
    if ride is None:
        return [], [], [], [], []
    srcs, gather = ride
    hbm = pl.BlockSpec(memory_space=pl.ANY)
    return (list(srcs), [hbm] * len(srcs), [_exchange_out(s_, gather) for s_ in srcs], [hbm] * len(srcs),
            _exchange_scratch(len(srcs)))


ADAM_BLOCK_BYTES = 4 * 1024 * 1024


def adamw_sum(name, parts, w, m, v):
    r, cols = w.shape
    tr = _largest_tile(r, min(ADAM_ROWS, max(SUBLANES, ADAM_BLOCK_BYTES // (N_DEV * cols * 4))), SUBLANES)

    def body(p_ref, w_ref, m_ref, v_ref, g_ref, d_ref, nm_ref, nv_ref):
        g = p_ref[0]
        for j in range(1, N_DEV):
            g = g + p_ref[j]
        mm_ = ADAM_B1 * m_ref[...] + (1.0 - ADAM_B1) * g
        vv = ADAM_B2 * v_ref[...] + (1.0 - ADAM_B2) * (g * g)
        m_hat = mm_ / (1.0 - ADAM_B1 ** ADAM_STEP)
        v_hat = vv / (1.0 - ADAM_B2 ** ADAM_STEP)
        g_ref[...] = g
        d_ref[...] = -ADAM_LR * (m_hat / (jnp.sqrt(v_hat) + ADAM_EPS) + ADAM_WD * w_ref[...])
        nm_ref[...] = mm_
        nv_ref[...] = vv

    row = pl.BlockSpec((tr, cols), lambda i: (i, 0))
    shp = jax.ShapeDtypeStruct((r, cols), F32)
    return pl.pallas_call(
        body, name=name, grid=(r // tr,),
        in_specs=[pl.BlockSpec((N_DEV, tr, cols), lambda i: (0, i, 0)), row, row, row],
        out_specs=[row] * 4, out_shape=[shp] * 4,
        compiler_params=_params(("parallel",)),
    )(parts, w, m, v)


WEIGHTS = ['l0_w_in', 'rwkv_mix', 'rwkv_w0', 'rwkv_w2', 'rwkv_a0', 'rwkv_a2', 'rwkv_g2', 'rwkv_k_k', 'rwkv_k_a',
           'rwkv_r_k', 'rwkv_ln_g', 'rwkv_ln_b', 'ssm_conv_w', 'ssm_conv_b', 'ssm_dt_bias', 'ssm_a_log', 'ssm_d',
           'ssm_norm_g', 'l0_w_out', 'l0_ln1_g', 'l0_ln1_b', 'ffn0_w_up', 'ffn0_conv_w', 'ffn0_conv_b',
           'ffn0_w_down', 'l0_ln2_g', 'l0_ln2_b', 'l1_w_in', 'mla_q_norm_g', 'mla_w_uq', 'mla_kv_norm_g',
           'mla_w_ukv', 'l1_w_out', 'l1_ln1_g', 'l1_ln1_b', 'ffn1_w_up', 'ffn1_conv_w', 'ffn1_conv_b',
           'ffn1_w_down', 'l1_ln2_g', 'l1_ln2_b']
SHARD_AXIS = {'l0_w_in': 1, 'rwkv_w2': 1, 'rwkv_a2': 1, 'rwkv_g2': 1, 'ssm_conv_w': 1, 'l0_w_out': 0,
              'ffn0_w_up': 1, 'ffn0_conv_w': 1, 'ffn0_w_down': 0, 'l1_w_in': 1, 'mla_w_uq': 1, 'mla_w_ukv': 1,
              'l1_w_out': 0, 'ffn1_w_up': 1, 'ffn1_conv_w': 1, 'ffn1_w_down': 0}
MATMUL_W = ['l0_w_in', 'rwkv_w2', 'rwkv_a2', 'rwkv_g2', 'l0_w_out', 'ffn0_w_up', 'ffn0_w_down', 'l1_w_in',
            'mla_w_uq', 'mla_w_ukv', 'l1_w_out', 'ffn1_w_up', 'ffn1_w_down']
CONV_W = ['ssm_conv_w', 'ffn0_conv_w', 'ffn1_conv_w']
TOK_TILE = 256
ADAM_ROWS = 512


def _ceil_to(size, unit):
    return -(-size // unit) * unit


def _flat_rows(pieces, seg_rows, total_rows):
    unit = seg_rows * LANES
    out, total = [], 0
    for p in pieces:
        p = jnp.concatenate([q.reshape(-1) for q in p]) if isinstance(p, list) else p.reshape(-1)
        pad = _ceil_to(p.size, unit) - p.size
        out.append(jnp.pad(p, (0, pad)) if pad else p)
        total += p.size + pad
    tail = _ceil_to(total, total_rows * LANES) - total
    if tail:
        out.append(jnp.zeros((tail,), out[0].dtype))
    return jnp.concatenate(out).reshape(-1, LANES)


def _unflatten(flat2d, shapes, seg_rows):
    flat = flat2d.reshape(-1)
    out, off = [], 0
    for shp in shapes:
        if isinstance(shp, list):
            seg, pos = [], off
            for s_ in shp:
                seg.append(flat[pos:pos + math.prod(s_)].reshape(s_))
                pos += math.prod(s_)
            out.append(seg)
            size = pos - off
        else:
            size = math.prod(shp)
            out.append(flat[off:off + size].reshape(shp))
        off += _ceil_to(size, seg_rows * LANES)
    return out


def _row(v):
    return v.reshape(1, -1)


def _pad_lanes(v):
    return jnp.pad(v.reshape(1, -1), ((0, 0), (0, LANES - v.size)))


def _local_step(a, w, comm=None):
    x = a['x']
    bsz, t, d = x.shape
    n = bsz * t
    tm = TOK_TILE
    pos = a['positions'].reshape(bsz, t, 1)
    inv_freq = 1.0 / (ROPE_THETA ** (jnp.arange(0, MLA_ROPE, 2, dtype=F32) / MLA_ROPE))
    inv_freq = jnp.concatenate([inv_freq, inv_freq]).reshape(1, MLA_ROPE)
    target = a['loss_target'].reshape(n, d)

    wi0 = w['l0_w_in']
    win0 = jnp.concatenate([wi0[:, :RKV_COLS], wi0[:, RWKV_COLS:L0_COLS - SSM_HEADS], wi0[:, RKV_COLS:RWKV_COLS],
                            wi0[:, L0_COLS - SSM_HEADS:], jnp.zeros((d, L0_PAD - L0_COLS), wi0.dtype)], axis=1)
    w2p = jnp.concatenate([w['rwkv_w2'], jnp.zeros_like(w['rwkv_w2'])], axis=0)
    a2p = jnp.concatenate([jnp.zeros_like(w['rwkv_a2']), w['rwkv_a2']], axis=0)
    mix = a['rwkv_mix']
    taps = jnp.stack([mix, 1.0 - mix])
    zero_b = jnp.zeros((1, mix.size), F32)
    rw_map = lambda j: j + jnp.where(j >= RKV_COLS // CONV_TILE, (P0_LORA - RKV_COLS) // CONV_TILE, 0)
    ssm_map = lambda j: j + (P0_Z + SSM_DIM) // CONV_TILE
    gate_map = lambda j: j
    up_map = lambda j: j + D_FF // CONV_TILE
    dt_col = P0_DT // LANES
    dtb, alog, dsk = _pad_lanes(a['ssm_dt_bias']), _pad_lanes(a['ssm_a_log']), _pad_lanes(a['ssm_d'])
    pre_p = [_row(a['rwkv_w0']), w2p, _row(a['rwkv_a0']), a2p, w['rwkv_g2'], _row(a['rwkv_k_k']), _row(a['rwkv_k_a'])]
    post_p = [_row(a['rwkv_ln_g']), _row(a['rwkv_ln_b']), _row(a['rwkv_r_k'])]
    sp_p = [dsk, _row(a['ssm_norm_g'])]

    def ln(name, h, y, layer, which):
        ps = [_row(a[f'l{layer}_ln{which}_g']), _row(a[f'l{layer}_ln{which}_b'])]
        return tok_fwd(name, f_ln, [(h, d, 0), (y, d, 0)], ps, [d], 2 * tm, n, t)[0]

    def ffn_fwd(layer, h):
        up = mm(h, w[f'ffn{layer}_w_up'], 'nn', f'ffn{layer}_up')
        act = dwconv_fwd(f'ffn{layer}_conv', up.reshape(bsz, t, 2 * D_FF), gate_map, w[f'ffn{layer}_conv_w'],
                         _row(a[f'ffn{layer}_conv_b']), 3, True, upmap=up_map, out_dtype=BF16)
        act = act.reshape(n, D_FF)
        return up, act, mm(act, w[f'ffn{layer}_w_down'], 'nn', f'ffn{layer}_down')

    x2 = x.reshape(n, d)
    proj0 = mm(x2, win0, 'nn', 'l0_in')
    p0 = proj0.reshape(bsz, t, L0_PAD)
    xs_r = dwconv_fwd('rwkv_shift', p0, rw_map, taps, zero_b, 2, False).reshape(n, RWKV_COLS)
    lora_blk = RKV_COLS // LANES
    pre_x = [(xs_r, RWKV_DIM, 0), (xs_r, RWKV_DIM, 1), (xs_r, RWKV_DIM, 2), (xs_r, LANES, lora_blk),
             (xs_r, LANES, lora_blk + 1)]
    heads64 = (RWKV_HEADS, HEAD_DIM)
    r_, v_, lw, kmod, al, be, gt = tok_fwd('rwkv_pre', f_rwkv_pre, pre_x, pre_p, [heads64] * 6 + [RWKV_DIM],
                                           tm, n, t)
    scan_in = [r_, lw, kmod, v_, al, be]
    if comm is None:
        y_h, rstates = rwkv_scan_fwd(*scan_in)
    else:
        y_h, rstates, *gots = rwkv_scan_fwd(*scan_in, ride=comm.weights_ride(MID_GATHER))
        w = {**w, **comm.weights(MID_GATHER, gots)}
    wi1 = w['l1_w_in']
    win1 = jnp.concatenate([wi1, jnp.zeros((d, L1_PAD - L1_COLS), wi1.dtype)], axis=1)
    wq3 = w['mla_w_uq'].reshape(-1, 8, MLA_NOPE + MLA_ROPE)
    wkv3 = w['mla_w_ukv'].reshape(-1, 8, 2 * MLA_NOPE)
    mla_p = [_row(a['mla_q_norm_g']), wq3[:, :, :MLA_NOPE].reshape(-1, 512), wq3[:, :, MLA_NOPE:].reshape(-1, 256),
             _row(a['mla_kv_norm_g']), wkv3[:, :, :MLA_NOPE].reshape(-1, 512), wkv3[:, :, MLA_NOPE:].reshape(-1, 512)]
    post_x = [y_h, r_, kmod, v_, (gt, 512, 0)]
    y_a = tok_fwd('rwkv_post', f_rwkv_post, post_x, post_p, [RWKV_DIM], tm, n, t)[0]
    xbc = dwconv_fwd('ssm_conv', p0, ssm_map, w['ssm_conv_w'], _row(a['ssm_conv_b']), 4, True)
    ys, sstates = ssd_fwd(xbc, p0, dt_col, dtb, alog)
    xbc2 = xbc.reshape(n, 2 * SSM_DIM)
    sp_x = [(ys.reshape(n, SSM_DIM), SSM_DIM, 0), (xbc2, SSM_DIM, 0), (proj0, SSM_DIM, P0_Z // SSM_DIM)]
    y_b = tok_fwd('ssd_post', f_ssd_post, sp_x, sp_p, [SSM_DIM], tm, n, t)[0]
    wo0 = w['l0_w_out']
    mixed0 = mm(y_b, wo0, 'nn', 'l0_out_b', b_rows=(512, 512), add=mm(y_a, wo0, 'nn', 'l0_out_a', b_rows=(0, 512)))
    h1 = ln('l0_ln1', x2, mixed0, 0, 1)
    up0, act0, f0 = ffn_fwd(0, h1)
    h2 = ln('l0_ln2', h1, f0, 0, 2)

    proj1 = mm(h2, win1, 'nn', 'l1_in')
    q_sb, k_sb, v_sb = tok_fwd('sb_split', f_same, [(proj1, 512, 0), (proj1, 512, 1), (proj1, 512, 2)], [],
                               [heads64] * 3, tm, n, t)
    if comm is None:
        o_c, sb_kept = sb_fwd(q_sb, k_sb, v_sb)
    else:
        o_c, sb_kept, *gots = sb_fwd(q_sb, k_sb, v_sb, ride=comm.weights_ride(FFN1_GATHER))
        w = {**w, **comm.weights(FFN1_GATHER, gots)}
    mla_x = [(proj1, P1_CKV - P1_CQ, P1_CQ // (P1_CKV - P1_CQ)), (proj1, LANES, P1_CKV // LANES)]
    qn, qp_raw, kn, vv = tok_fwd('mla_pre', f_mla_pre, mla_x, mla_p, [heads64, (8, MLA_ROPE), heads64, heads64],
                                 tm, n, t)
    kp_raw = proj1[:, P1_KPE:P1_KPE + MLA_ROPE].reshape(bsz, 1, t, MLA_ROPE)
    qp = rope('rope_q', qp_raw, pos, inv_freq, 1.0)
    kp = rope('rope_k', kp_raw, pos, inv_freq, 1.0)
    o_d, lse_d = mla_fwd(qn, qp, kn, kp, vv)
    y_cd = tok_fwd('attn_merge', f_concat, [o_c, o_d], [], [2 * RWKV_DIM], tm, n, t, out_dtype=BF16)[0]
    wo1 = w['l1_w_out']
    mixed1 = mm(y_cd, wo1, 'nn', 'l1_out')
    h3 = ln('l1_ln1', h2, mixed1, 1, 1)
    up1, act1, f1 = ffn_fwd(1, h3)
    h4 = ln('l1_ln2', h3, f1, 1, 2)
    dh4, loss = loss_head(h4, target, tm)

    g = {}

    def ln_bwd(name, h, y, layer, which, dout):
        ps = [_row(a[f'l{layer}_ln{which}_g']), _row(a[f'l{layer}_ln{which}_b'])]
        (dh, dy), (dg, db) = tok_bwd(name, f_ln, [(h, d, 0), (y, d, 0)], ps, [[dout]], 2 * tm, n, t)
        g[f'l{layer}_ln{which}_g'], g[f'l{layer}_ln{which}_b'] = dg.reshape(-1), db.reshape(-1)
        return dh, dy

    def ffn_bwd(layer, h, up, act, df, dh_res):
        wup, wdown = w[f'ffn{layer}_w_up'], w[f'ffn{layer}_w_down']
        g[f'ffn{layer}_w_down'] = mm(act, df, 'tn', f'ffn{layer}_dwdown')
        dact = mm(df, wdown, 'nt', f'ffn{layer}_dact').reshape(bsz, t, D_FF)
        dgate, dcw, dcb, dup = dwconv_bwd(f'ffn{layer}_conv_bwd', up.reshape(bsz, t, 2 * D_FF), gate_map,
                                          w[f'ffn{layer}_conv_w'], _row(a[f'ffn{layer}_conv_b']), 3, True, dact,
                                          upmap=up_map, grad_dtype=BF16)
        dgate, dup = dgate.reshape(n, D_FF), dup.reshape(n, D_FF)
        g[f'ffn{layer}_conv_w'], g[f'ffn{layer}_conv_b'] = dcw, dcb.reshape(-1)
        g[f'ffn{layer}_w_up'] = (mm(h, dgate, 'tn', f'ffn{layer}_dwgate'), mm(h, dup, 'tn', f'ffn{layer}_dwup'))
        dh = mm(dgate, wup, 'nt', f'ffn{layer}_dh_gate', add=dh_res, b_cols=(0, D_FF))
        return mm(dup, wup, 'nt', f'ffn{layer}_dh_up', add=dh, b_cols=(D_FF, D_FF))

    dh3_res, df1 = ln_bwd('l1_ln2_bwd', h3, f1, 1, 2, dh4)
    dh3 = ffn_bwd(1, h3, up1, act1, df1, dh3_res)
    dh2_res, dmixed1 = ln_bwd('l1_ln1_bwd', h2, mixed1, 1, 1, dh3)
    g['l1_w_out'] = mm(y_cd, dmixed1, 'tn', 'l1_dwout')
    dy_cd = mm(dmixed1, wo1, 'nt', 'l1_dy')
    dy_c, dy_d = tok_fwd('attn_split', f_same, [(dy_cd, 512, 0), (dy_cd, 512, 1)], [], [heads64] * 2, tm, n, t)
    parts = {}
    if comm is None:
        dq_sb, dk_sb, dv_sb = sb_bwd(q_sb, k_sb, v_sb, sb_kept, dy_c)
    else:
        dq_sb, dk_sb, dv_sb, *parts['a'] = sb_bwd(q_sb, k_sb, v_sb, sb_kept, dy_c, ride=comm.grad_ride('a', g))
    dqn, dqp, dkn, dkp, dvv = mla_bwd(qn, qp, kn, kp, vv, o_d, lse_d, dy_d)
    dqp_raw = rope('rope_q_bwd', dqp, pos, inv_freq, -1.0)
    dkp_raw = rope('rope_k_bwd', dkp, pos, inv_freq, -1.0).reshape(n, MLA_ROPE)
    (dcq, dckv), (dqg, dwq_n, dwq_p, dkvg, dwk, dwv) = tok_bwd('mla_pre_bwd', f_mla_pre, mla_x, mla_p,
                                                               [[dqn], [dqp_raw], [dkn], [dvv]], tm, n, t)
    g['mla_q_norm_g'], g['mla_kv_norm_g'] = dqg.reshape(-1), dkvg.reshape(-1)
    g['mla_w_uq'] = jnp.concatenate([dwq_n.reshape(-1, 8, MLA_NOPE), dwq_p.reshape(-1, 8, MLA_ROPE)],
                                    axis=2).reshape(-1, 8 * (MLA_NOPE + MLA_ROPE))
    g['mla_w_ukv'] = jnp.concatenate([dwk.reshape(-1, 8, MLA_NOPE), dwv.reshape(-1, 8, MLA_NOPE)],
                                     axis=2).reshape(-1, 16 * MLA_NOPE)
    dkp_pad = jnp.pad(dkp_raw, ((0, 0), (0, LANES - MLA_ROPE)))
    dproj1 = tok_fwd('l1_dproj', f_concat, [dq_sb, dk_sb, dv_sb, (dcq, 256, 0), (dckv, LANES, 0),
                                            (dkp_pad, LANES, 0)], [], [L1_PAD], tm, n, t, out_dtype=BF16)[0]
    g['l1_w_in'] = mm(h2, dproj1, 'tn', 'l1_dwin')[:, :L1_COLS]
    dh2 = mm(dproj1, win1, 'nt', 'l1_dh', add=dh2_res)

    dh1_res, df0 = ln_bwd('l0_ln2_bwd', h1, f0, 0, 2, dh2)
    dh1 = ffn_bwd(0, h1, up0, act0, df0, dh1_res)
    dx_res, dmixed0 = ln_bwd('l0_ln1_bwd', x2, mixed0, 0, 1, dh1)
    g['l0_w_out'] = (mm(y_a, dmixed0, 'tn', 'l0_dwout_a'), mm(y_b, dmixed0, 'tn', 'l0_dwout_b'))
    dy_a = mm(dmixed0, wo0, 'nt', 'l0_dy_a', b_rows=(0, 512))
    dy_b = mm(dmixed0, wo0, 'nt', 'l0_dy_b', b_rows=(512, 512))
    (dy_r, dr1, dkm1, dv1, dgt), (dlng, dlnb, drk) = tok_bwd('rwkv_post_bwd', f_rwkv_post, post_x, post_p, [[dy_a]],
                                                            tm, n, t, dx_layouts=[heads64] * 4 + [None])
    g['rwkv_ln_g'], g['rwkv_ln_b'] = dlng.reshape(-1), dlnb.reshape(-1)
    g['rwkv_r_k'] = drk.reshape(RWKV_HEADS, HEAD_DIM)
    (dys, dxs_skip, dz), (ddsk, dng) = tok_bwd('ssd_post_bwd', f_ssd_post, sp_x, sp_p, [[dy_b]], tm, n, t)
    g['ssm_d'], g['ssm_norm_g'] = ddsk[0, :SSM_HEADS], dng.reshape(-1)
    dxbc_act, ddtr, ddtb, dalog = ssd_bwd(xbc, p0, dt_col, dtb, alog, sstates, dys.reshape(bsz, t, SSM_DIM),
                                          dxs_skip.reshape(bsz, t, SSM_DIM))
    g['ssm_dt_bias'], g['ssm_a_log'] = ddtb[0, :SSM_HEADS], dalog[0, :SSM_HEADS]
    dxbc, dscw, dscb = dwconv_bwd('ssm_conv_bwd', p0, ssm_map, w['ssm_conv_w'], _row(a['ssm_conv_b']), 4, True,
                                  dxbc_act)
    g['ssm_conv_w'], g['ssm_conv_b'] = dscw, dscb.reshape(-1)
    if comm is None:
        dscan = rwkv_scan_bwd(*scan_in, rstates, dy_r)
    else:
        outs = rwkv_scan_bwd(*scan_in, rstates, dy_r, ride=comm.grad_ride('b', g))
        dscan, parts['b'] = outs[:6], outs[6:]
    dr2, dlw, dk2, dv2, dal, dbe = dscan
    pre_ct = [[dr1, dr2], [dv1, dv2], [dlw], [dkm1, dk2], [dal], [dbe], [dgt]]
    dpre_x, dpre_p = tok_bwd('rwkv_pre_bwd', f_rwkv_pre, pre_x, pre_p, pre_ct, tm, n, t)
    g['rwkv_w0'], g['rwkv_a0'] = dpre_p[0].reshape(-1), dpre_p[2].reshape(-1)
    g['rwkv_w2'], g['rwkv_a2'], g['rwkv_g2'] = dpre_p[1][:64], dpre_p[3][64:], dpre_p[4]
    g['rwkv_k_k'], g['rwkv_k_a'] = dpre_p[5].reshape(-1), dpre_p[6].reshape(-1)
    dxs_r = jnp.concatenate(dpre_x, axis=1).reshape(bsz, t, RWKV_COLS)
    d_rw, dtaps, _ = dwconv_bwd('rwkv_shift_bwd', p0, rw_map, taps, zero_b, 2, False, dxs_r)
    d_rw = d_rw.reshape(n, RWKV_COLS)
    g['rwkv_mix'] = dtaps[0] - dtaps[1]
    lora_cols = RWKV_COLS - RKV_COLS
    dproj0 = tok_fwd('l0_dproj', f_concat, [(d_rw, RKV_COLS, 0), (dz, SSM_DIM, 0),
                                            (dxbc.reshape(n, 2 * SSM_DIM), 2 * SSM_DIM, 0),
                                            (d_rw, lora_cols, RKV_COLS // lora_cols), (ddtr.reshape(n, LANES), LANES, 0)],
                     [], [L0_PAD], tm, n, t, out_dtype=BF16)[0]
    dwin0 = mm(x2, dproj0, 'tn', 'l0_dwin')
    g['l0_w_in'] = jnp.concatenate([dwin0[:, :RKV_COLS], dwin0[:, P0_LORA:P0_DT], dwin0[:, P0_Z:P0_LORA],
                                    dwin0[:, P0_DT:P0_DT + SSM_HEADS]], axis=1)
    if comm is None:
        dx = mm(dproj0, win0, 'nt', 'l0_dx', add=dx_res)
    else:
        dx, *parts['c'] = mm(dproj0, win0, 'nt', 'l0_dx', add=dx_res, ride=comm.grad_ride('c', g))
    return loss, dx.reshape(bsz, t, d), g, parts


GRAD_GROUPS = {
    'a': ['ffn1_w_up', 'ffn1_conv_w', 'ffn1_conv_b', 'ffn1_w_down', 'l1_ln2_g', 'l1_ln2_b'],
    'c': ['l0_w_in', 'rwkv_mix', 'rwkv_w0', 'rwkv_w2', 'rwkv_a0', 'rwkv_a2', 'rwkv_g2', 'rwkv_k_k', 'rwkv_k_a'],
}
GRAD_GROUPS['b'] = [nm for nm in WEIGHTS if nm not in GRAD_GROUPS['a'] + GRAD_GROUPS['c']]
FIRST_GATHER = ['l0_w_in', 'rwkv_w2', 'rwkv_a2', 'rwkv_g2'] + CONV_W
MID_GATHER = ['l0_w_out', 'ffn0_w_up', 'ffn0_w_down', 'l1_w_in', 'mla_w_uq', 'mla_w_ukv', 'l1_w_out']
FFN1_GATHER = ['ffn1_w_up', 'ffn1_w_down']
BF16_ROWS = 16


def _name_kinds(names):
    return ([nm for nm in names if nm in MATMUL_W], [nm for nm in names if nm in SHARD_AXIS and nm not in MATMUL_W],
            [nm for nm in names if nm not in SHARD_AXIS])


class _Comm:
    def __init__(self, a):
        self.a = a

    def weights_ride(self, names):
        big, conv, _ = _name_kinds(names)
        srcs = [self.a[nm].astype(BF16) for nm in big]
        if conv:
            srcs.append(_flat_rows([lax.bitcast_convert_type(self.a[nm], BF16) for nm in conv], BF16_ROWS, BF16_ROWS))
        return srcs, True

    def weights(self, names, gots):
        big, conv, _ = _name_kinds(names)
        out = {}
        for nm, got in zip(big, gots):
            out[nm] = (got.reshape(-1, got.shape[2]) if SHARD_AXIS[nm] == 0
                       else jnp.concatenate([got[k] for k in range(N_DEV)], axis=1))
        if conv:
            shapes = [self.a[nm].shape + (2,) for nm in conv]
            blocks = [_unflatten(gots[-1][k], shapes, BF16_ROWS) for k in range(N_DEV)]
            for i, nm in enumerate(conv):
                out[nm] = jnp.concatenate([lax.bitcast_convert_type(blocks[k][i], F32) for k in range(N_DEV)], axis=1)
        return out

    def first_weights(self):
        return self.weights(FIRST_GATHER, peer_exchange('gather_first_weights', self.weights_ride(FIRST_GATHER)[0], True))

    def grad_ride(self, group, g):
        def shard_of(nm, k):
            gv = g[nm]
            per = N_DEV
            if isinstance(gv, tuple):
                gv, k, per = gv[k // 4], k % 4, 4
            width = gv.shape[SHARD_AXIS[nm]] // per
            return lax.slice_in_dim(gv, k * width, (k + 1) * width, axis=SHARD_AXIS[nm])

        big, conv, small = _name_kinds(GRAD_GROUPS[group])
        srcs = [jnp.stack([shard_of(nm, k) for k in range(N_DEV)]) for nm in big]
        srcs.append(jnp.stack([_flat_rows([shard_of(nm, k) for nm in conv] + [[g[nm] for nm in small]],
                                          SUBLANES, ADAM_ROWS) for k in range(N_DEV)]))
        return srcs, False


def _step(a):
    comm = _Comm(a)
    loss, dx, _, parts = _local_step(a, comm.first_weights(), comm)
    loss = lax.psum(loss, ('x', 'y', 'c'))
    res = {}
    for group, names in GRAD_GROUPS.items():
        big, conv, small = _name_kinds(names)
        for nm, got in zip(big, parts[group]):
            res[nm] = adamw_sum(f'adamw_{nm}', got, a[nm], a['m_' + nm], a['v_' + nm])
        flat = lambda prefix: _flat_rows([a[prefix + nm] for nm in conv] + [[a[prefix + nm] for nm in small]],
                                         SUBLANES, ADAM_ROWS)
        outs = adamw_sum(f'adamw_{group}', parts[group][-1], flat(''), flat('m_'), flat('v_'))
        shapes = [a[nm].shape for nm in conv] + [[a[nm].shape for nm in small]]
        per_out = [_unflatten(o, shapes, SUBLANES) for o in outs]
        for i, nm in enumerate(conv):
            res[nm] = [per_out[j][i] for j in range(4)]
        for i, nm in enumerate(small):
            res[nm] = [per_out[j][-1][i] for j in range(4)]
    return (loss, dx, *[res[nm][j] for j in range(4) for nm in WEIGHTS])


def kernel(x, positions, l0_w_in, rwkv_mix, rwkv_w0, rwkv_w2, rwkv_a0, rwkv_a2, rwkv_g2, rwkv_k_k, rwkv_k_a, rwkv_r_k, rwkv_ln_g, rwkv_ln_b, ssm_conv_w, ssm_conv_b, ssm_dt_bias, ssm_a_log, ssm_d, ssm_norm_g, l0_w_out, l0_ln1_g, l0_ln1_b, ffn0_w_up, ffn0_conv_w, ffn0_conv_b, ffn0_w_down, l0_ln2_g, l0_ln2_b, l1_w_in, mla_q_norm_g, mla_w_uq, mla_kv_norm_g, mla_w_ukv, l1_w_out, l1_ln1_g, l1_ln1_b, ffn1_w_up, ffn1_conv_w, ffn1_conv_b, ffn1_w_down, l1_ln2_g, l1_ln2_b, loss_target, m_l0_w_in, m_rwkv_mix, m_rwkv_w0, m_rwkv_w2, m_rwkv_a0, m_rwkv_a2, m_rwkv_g2, m_rwkv_k_k, m_rwkv_k_a, m_rwkv_r_k, m_rwkv_ln_g, m_rwkv_ln_b, m_ssm_conv_w, m_ssm_conv_b, m_ssm_dt_bias, m_ssm_a_log, m_ssm_d, m_ssm_norm_g, m_l0_w_out, m_l0_ln1_g, m_l0_ln1_b, m_ffn0_w_up, m_ffn0_conv_w, m_ffn0_conv_b, m_ffn0_w_down, m_l0_ln2_g, m_l0_ln2_b, m_l1_w_in, m_mla_q_norm_g, m_mla_w_uq, m_mla_kv_norm_g, m_mla_w_ukv, m_l1_w_out, m_l1_ln1_g, m_l1_ln1_b, m_ffn1_w_up, m_ffn1_conv_w, m_ffn1_conv_b, m_ffn1_w_down, m_l1_ln2_g, m_l1_ln2_b, v_l0_w_in, v_rwkv_mix, v_rwkv_w0, v_rwkv_w2, v_rwkv_a0, v_rwkv_a2, v_rwkv_g2, v_rwkv_k_k, v_rwkv_k_a, v_rwkv_r_k, v_rwkv_ln_g, v_rwkv_ln_b, v_ssm_conv_w, v_ssm_conv_b, v_ssm_dt_bias, v_ssm_a_log, v_ssm_d, v_ssm_norm_g, v_l0_w_out, v_l0_ln1_g, v_l0_ln1_b, v_ffn0_w_up, v_ffn0_conv_w, v_ffn0_conv_b, v_ffn0_w_down, v_l0_ln2_g, v_l0_ln2_b, v_l1_w_in, v_mla_q_norm_g, v_mla_w_uq, v_mla_kv_norm_g, v_mla_w_ukv, v_l1_w_out, v_l1_ln1_g, v_l1_ln1_b, v_ffn1_w_up, v_ffn1_conv_w, v_ffn1_conv_b, v_ffn1_w_down, v_l1_ln2_g, v_l1_ln2_b):
    return _step(dict(locals()))
```

```python
import functools
import math

import jax
import jax.numpy as jnp
from jax import lax
from jax.experimental import pallas as pl
from jax.experimental.pallas import tpu as pltpu

F32 = jnp.float32
BF16 = jnp.bfloat16
HI = lax.Precision.HIGHEST

V7X_VMEM_BYTES = 64 * 1024 * 1024
VMEM_LIMIT = V7X_VMEM_BYTES - 8 * 1024 * 1024
LANES = 128
SUBLANES = 8
N_DEV = 8

D_MODEL = 1024
HEAD_DIM = 64
RWKV_DIM = 512
RWKV_HEADS = 8
RWKV_GN_EPS = 64e-5
RWKV_CHUNK = 64
SSM_DIM = 512
SSM_HEADS = 8
SSM_CHUNK = 128
SSM_STATE = 128
Q_BLOCK = 128
SB_HEADS_PER_STEP = 4
MLA_HEADS_PER_STEP = 4
ATTN_FWD_HEADS_PER_STEP = 8
MLA_NOPE = 64
MLA_ROPE = 32
ROPE_THETA = 10000.0
D_FF = 2816
DEPTH = 2
ALPHA = (2 * DEPTH) ** 0.25
RKV_COLS = 3 * RWKV_DIM
RWKV_COLS = RKV_COLS + 64 + 64 + 128
L0_COLS = RWKV_COLS + SSM_DIM + 2 * SSM_DIM + SSM_HEADS
P0_Z = RKV_COLS
P0_LORA = P0_Z + 3 * SSM_DIM
P0_DT = P0_LORA + (RWKV_COLS - RKV_COLS)
L0_PAD = P0_DT + LANES
L1_COLS = 3 * RWKV_DIM + 256 + 128 + MLA_ROPE
P1_CQ, P1_CKV, P1_KPE = 3 * RWKV_DIM, 3 * RWKV_DIM + 256, 3 * RWKV_DIM + 256 + 128
L1_PAD = P1_KPE + LANES

ADAM_LR = 0.001
ADAM_B1 = 0.9
ADAM_B2 = 0.999
ADAM_EPS = 1e-08
ADAM_WD = 0.01
ADAM_STEP = 10

NEG_BIG = -1e30


def _params(sem=None):
    return pltpu.CompilerParams(dimension_semantics=sem, vmem_limit_bytes=VMEM_LIMIT)


P_F32, P_BF16, P_BF16X3 = 0, 1, 2


def _dg_raw(a, b, ca, cb, fast):
    dims = (((ca,), (cb,)), ((), ()))
    if fast == P_BF16:
        return lax.dot_general(a.astype(BF16), b.astype(BF16), dims, preferred_element_type=F32)
    prec = HI if fast == P_F32 else lax.Precision.HIGH
    return lax.dot_general(a, b, dims, precision=prec, preferred_element_type=F32)


@functools.partial(jax.custom_vjp, nondiff_argnums=(2, 3, 4))
def dg(a, b, ca, cb, fast):
    return _dg_raw(a, b, ca, cb, fast)


def _dg_fwd(a, b, ca, cb, fast):
    return _dg_raw(a, b, ca, cb, fast), (a, b)


def _dg_bwd(ca, cb, fast, res, ct):
    a, b = res
    fa, fb = 1 - ca, 1 - cb
    da = _dg_raw(ct, b, 1, fb, fast) if ca == 1 else _dg_raw(b, ct, fb, 1, fast)
    db = _dg_raw(a, ct, fa, 0, fast) if cb == 0 else _dg_raw(ct, a, 0, fa, fast)
    return da.astype(a.dtype), db.astype(b.dtype)


dg.defvjp(_dg_fwd, _dg_bwd)


def mmb(a, b):
    return dg(a, b, 1, 0, P_BF16)


def mmf(a, b):
    return dg(a, b, 1, 0, P_F32)


def mm3(a, b):
    return dg(a, b, 1, 0, P_BF16X3)


def mm3_nt(a, b):
    return dg(a, b, 1, 1, P_BF16X3)


def mm3_tn(a, b):
    return dg(a, b, 0, 0, P_BF16X3)


def _split3_dot(x, m01, cb, terms=3):
    parts, rest = [], x
    for i in range(terms):
        parts.append(rest.astype(BF16))
        if i + 1 < terms:
            rest = rest - parts[-1].astype(F32)
    rows = x.shape[0]
    out = lax.dot_general(jnp.concatenate(parts, axis=0), m01.astype(BF16), (((1,), (cb,)), ((), ())),
                          preferred_element_type=F32)
    return functools.reduce(lambda a_, b_: a_ + b_, [out[i * rows:(i + 1) * rows] for i in range(terms)])


def _lower_ones(n):
    return jnp.where(_iota((n, n), 0) >= _iota((n, n), 1), 1.0, 0.0)


SUFFIX_TERMS = 2


@jax.custom_vjp
def suffix_sum(x):
    return _split3_dot(x, _lower_ones(x.shape[1]), 0, SUFFIX_TERMS)


def _suffix_sum_fwd(x):
    return suffix_sum(x), None


def _suffix_sum_bwd(_, ct):
    return (_split3_dot(ct, _lower_ones(ct.shape[1]), 1, SUFFIX_TERMS),)


suffix_sum.defvjp(_suffix_sum_fwd, _suffix_sum_bwd)


def _iota(shape, dim):
    return lax.broadcasted_iota(jnp.int32, shape, dim)


def _softplus(x):
    return jnp.maximum(x, 0.0) + jnp.log1p(jnp.exp(-jnp.abs(x)))


def _silu(x):
    return x * jax.nn.sigmoid(x)


def _largest_tile(n, cap, mult):
    best = None
    t = mult
    while t <= min(n, cap):
        if n % t == 0:
            best = t
        t += mult
    return n if best is None else best


MM_VMEM_BUDGET = 40 * 1024 * 1024
V7X_HBM_BYTES_PER_S = 3.2e12
V7X_VMEM_STORE_BYTES_PER_S = 7e12
GRID_STEP_S = 0.35e-6


def _mm_tiles(M, N, K, a_bytes, b_bytes, has_add):
    def divs(n):
        return [d for d in range(LANES, n + 1, LANES) if n % d == 0] or [n]

    best = None
    for tm in divs(M):
        for tn in divs(N):
            if tm * tn * 4 > 12 * 1024 * 1024:
                continue
            for tk in divs(K):
                vmem = (2 * (tm * tk * a_bytes + tk * tn * b_bytes) + 2 * tm * tn * 4 * (2 if has_add else 1)
                        + (tm * tk + tk * tn) * 2 + tm * tn * 4)
                if vmem > MM_VMEM_BUDGET:
                    continue
                ni, nj, nk = M // tm, N // tn, K // tk
                a_reads = M * K * a_bytes * (1 if nk == 1 else nj)
                b_reads = K * N * b_bytes * (1 if (nk == 1 and nj == 1) else ni)
                traffic = a_reads + b_reads + M * N * 4 * (2 if has_add else 1)
                cost = (traffic / V7X_HBM_BYTES_PER_S + ni * nj * nk * GRID_STEP_S
                        + M * N * 4 * nk / V7X_VMEM_STORE_BYTES_PER_S)
                if min(tm, tn, tk) < 256 and min(M, N, K) >= 256:
                    cost *= 1.5
                if best is None or cost < best[0]:
                    best = (cost, tm, tn, tk)
    return best[1:]


def mm(a, b, mode, name, add=None, b_rows=None, b_cols=None, ride=None, out_dtype=F32):
    r0, nr = b_rows or (0, b.shape[0])
    c0, nc = b_cols or (0, b.shape[1])
    if mode == "nn":
        (M, K), N = a.shape, nc
        assert nr == K
    elif mode == "nt":
        (M, K), N = a.shape, nr
        assert nc == K
    else:
        (K, M), N = a.shape, b.shape[1]
        assert b_rows is None and b_cols is None
    has_add = add is not None
    tm, tn, tk = _mm_tiles(M, N, K, a.dtype.itemsize, b.dtype.itemsize, has_add)
    nk = K // tk
    assert out_dtype == F32 or nk == 1
    keep_a = nk == 1 and N // tn > 1 and a.dtype != BF16
    if mode == "nn":
        assert r0 % tk == 0 and c0 % tn == 0
        a_spec = pl.BlockSpec((tm, tk), lambda i, j, k: (i, k))
        b_spec = pl.BlockSpec((tk, tn), lambda i, j, k: (k + r0 // tk, j + c0 // tn))
        dims = (((1,), (0,)), ((), ()))
    elif mode == "nt":
        assert r0 % tn == 0 and c0 % tk == 0
        a_spec = pl.BlockSpec((tm, tk), lambda i, j, k: (i, k))
        b_spec = pl.BlockSpec((tn, tk), lambda i, j, k: (j + r0 // tn, k + c0 // tk))
        dims = (((1,), (1,)), ((), ()))
    else:
        a_spec = pl.BlockSpec((tk, tm), lambda i, j, k: (k, i))
        b_spec = pl.BlockSpec((tk, tn), lambda i, j, k: (k, j))
        dims = (((0,), (0,)), ((), ()))
    o_spec = pl.BlockSpec((tm, tn), lambda i, j, k: (i, j))
    grid = (M // tm, N // tn, nk)
    r_in, r_specs, r_out, r_ospecs, r_scr = _ride_args(ride)
    n_add, n_ride = int(has_add), len(r_in)

    def body(a_ref, b_ref, *rest):
        o_ref = rest[n_add + n_ride]
        if ride is not None:
            first, last = _grid_first_last(grid)
            copies = _ride_start(rest[n_add:n_add + n_ride], rest[n_add + n_ride + 1:n_add + 2 * n_ride + 1],
                                 rest[-3:], ride[1], first)
        k = pl.program_id(2)
        if keep_a:
            a_bf = rest[n_add + 2 * n_ride + 1]

            @pl.when(pl.program_id(1) == 0)
            def _():
                a_bf[...] = a_ref[...].astype(BF16)

            av = a_bf[...]
        else:
            av = a_ref[...].astype(BF16)
        part = lax.dot_general(av, b_ref[...].astype(BF16), dims, preferred_element_type=F32)

        @pl.when(k == 0)
        def _():
            o_ref[...] = (part + rest[0][...] if has_add else part).astype(out_dtype)

        if nk > 1:
            @pl.when(k > 0)
            def _():
                o_ref[...] += part

        if ride is not None:
            _ride_wait(copies, last)

    ins = [a, b] + ([add] if has_add else []) + r_in
    specs = [a_spec, b_spec] + ([o_spec] if has_add else []) + r_specs
    outs = pl.pallas_call(
        body, name=name, grid=grid, in_specs=specs, out_specs=[o_spec] + r_ospecs,
        out_shape=[jax.ShapeDtypeStruct((M, N), out_dtype)] + r_out,
        scratch_shapes=([pltpu.VMEM(a_spec.block_shape, BF16)] if keep_a else []) + r_scr,
        compiler_params=_params(("arbitrary" if ride is not None else "parallel", "arbitrary", "arbitrary")),
    )(*ins)
    return outs[0] if ride is None else outs


def _is_heads(x):
    return not isinstance(x, tuple)


def _tok_arr(x):
    return x if _is_heads(x) else x[0]


def _tok_width(x):
    return x.shape[1] * x.shape[3] if _is_heads(x) else x[1]


def _heads_spec(h, dh, tm, tiles_per_seq):
    return pl.BlockSpec((None, h, tm, dh), lambda i: (i // tiles_per_seq, 0, i % tiles_per_seq, 0))


def _x_spec(x, tm, tiles_per_seq):
    if _is_heads(x):
        return _heads_spec(x.shape[1], x.shape[3], tm, tiles_per_seq)
    return pl.BlockSpec((tm, x[1]), functools.partial(lambda i, cb: (i, cb), cb=x[2]))


def _out_spec_shape(layout, n, seq, tm, dtype=F32):
    if isinstance(layout, tuple):
        h, dh = layout
        return _heads_spec(h, dh, tm, seq // tm), jax.ShapeDtypeStruct((n // seq, h, seq, dh), dtype)
    return pl.BlockSpec((tm, layout), lambda i: (i, 0)), jax.ShapeDtypeStruct((n, layout), dtype)


def _tok_load(ref):
    if len(ref.shape) == 3:
        return jnp.concatenate([ref[hh] for hh in range(ref.shape[0])], axis=1)
    return ref[...]


def _tok_store(ref, val):
    if len(ref.shape) == 3:
        dh = ref.shape[2]
        for hh in range(ref.shape[0]):
            ref[hh] = val[:, hh * dh:(hh + 1) * dh]
    else:
        ref[...] = val


def _p_specs(ps):
    return [pl.BlockSpec(p.shape, lambda i: (0, 0)) for p in ps]


def tok_fwd(name, f, xs, ps, out_layouts, tm, n, seq, out_dtype=F32):
    nx, npar = len(xs), len(ps)
    outs = [_out_spec_shape(lay, n, seq, tm, out_dtype) for lay in out_layouts]

    def body(*refs):
        xv = [_tok_load(r) for r in refs[:nx]]
        pv = [r[...].astype(F32) for r in refs[nx:nx + npar]]
        for o, r in zip(f(*xv, *pv), refs[nx + npar:]):
            _tok_store(r, o.astype(out_dtype))

    return pl.pallas_call(
        body, name=name, grid=(n // tm,),
        in_specs=[_x_spec(x, tm, seq // tm) for x in xs] + _p_specs(ps),
        out_specs=[o[0] for o in outs], out_shape=[o[1] for o in outs],
        compiler_params=_params(("parallel",)),
    )(*[_tok_arr(x) for x in xs], *ps)


def tok_bwd(name, f, xs, ps, cts, tm, n, seq, dx_layouts=None):
    nx, npar = len(xs), len(ps)
    ct_flat = [c for group in cts for c in group]
    nct = len(ct_flat)
    dx_layouts = dx_layouts or [None] * nx
    dxs = [_out_spec_shape(lay if lay else _tok_width(x), n, seq, tm) for x, lay in zip(xs, dx_layouts)]

    def body(*refs):
        xv = [_tok_load(r) for r in refs[:nx]]
        pv = [r[...].astype(F32) for r in refs[nx:nx + npar]]
        ct_refs = refs[nx + npar:nx + npar + nct]
        dx_refs = refs[nx + npar + nct:nx + npar + nct + nx]
        dp_refs = refs[nx + npar + nct + nx:]
        cv, pos = [], 0
        for group in cts:
            acc = _tok_load(ct_refs[pos])
            for r in ct_refs[pos + 1:pos + len(group)]:
                acc = acc + _tok_load(r)
            cv.append(acc)
            pos += len(group)
        _, vjp = jax.vjp(f, *xv, *pv)
        grads = vjp(tuple(cv))
        for g, r in zip(grads[:nx], dx_refs):
            _tok_store(r, g)

        @pl.when(pl.program_id(0) == 0)
        def _():
            for r in dp_refs:
                r[...] = jnp.zeros_like(r)

        for g, r in zip(grads[nx:], dp_refs):
            r[...] += g

    ct_specs = [_heads_spec(c.shape[1], c.shape[3], tm, seq // tm) if c.ndim == 4
                else pl.BlockSpec((tm, c.shape[1]), lambda i: (i, 0)) for c in ct_flat]
    outs = pl.pallas_call(
        body, name=name, grid=(n // tm,),
        in_specs=[_x_spec(x, tm, seq // tm) for x in xs] + _p_specs(ps) + ct_specs,
        out_specs=[d[0] for d in dxs] + _p_specs(ps),
        out_shape=[d[1] for d in dxs] + [jax.ShapeDtypeStruct(p.shape, F32) for p in ps],
        compiler_params=_params(("arbitrary",)),
    )(*[_tok_arr(x) for x in xs], *ps, *ct_flat)
    return outs[:nx], outs[nx:]


def f_ln(h, y, g, b):
    pre = ALPHA * h + y
    mu = jnp.mean(pre, axis=-1, keepdims=True)
    xc = pre - mu
    var = jnp.mean(xc * xc, axis=-1, keepdims=True)
    return (xc * lax.rsqrt(var + 1e-5) * g + b,)


def _head_sel(width, nheads_pad, per):
    return jnp.where(_iota((width, nheads_pad), 0) // per == _iota((width, nheads_pad), 1), 1.0, 0.0).astype(F32)


def _head_sel_t(nheads_pad, width, per):
    return jnp.where(_iota((nheads_pad, width), 1) // per == _iota((nheads_pad, width), 0), 1.0, 0.0).astype(F32)


@jax.custom_vjp
def head_sum(x):
    return _split3_dot(x, _head_sel(RWKV_DIM, LANES, HEAD_DIM), 0)


@jax.custom_vjp
def head_spread(y):
    return _split3_dot(y, _head_sel(RWKV_DIM, LANES, HEAD_DIM), 1)


head_sum.defvjp(lambda x: (head_sum(x), None), lambda _, ct: (head_spread(ct),))
head_spread.defvjp(lambda y: (head_spread(y), None), lambda _, ct: (head_sum(ct),))


def f_rwkv_pre(r, k, v, lora, glo, w0, w2p, a0, a2p, g2, k_k, k_a):
    lane = _iota(lora.shape, 1)
    tw = jnp.where(lane < 64, jnp.tanh(lora), 0.0)
    ta = jnp.where(lane >= 64, lora, 0.0)
    log_w = -_softplus(-(w0 + mmb(tw, w2p))) - 0.5
    lw = -jnp.exp(log_w)
    a = jax.nn.sigmoid(a0 + mmb(ta, a2p))
    g = mmb(jax.nn.sigmoid(glo), g2)
    kk = k * k_k
    nrm = jnp.sqrt(jnp.maximum(head_sum(kk * kk), 1e-24))
    kkn = kk * head_spread(1.0 / nrm)
    kmod = k * (1.0 + (a - 1.0) * k_a)
    return r, v, lw, kmod, -kkn, kkn * a, g


def f_rwkv_post(y, r, kmod, v, g, ln_g, ln_b, r_k):
    inv = 1.0 / HEAD_DIM
    mu = head_spread(head_sum(y) * inv)
    yc = y - mu
    var = head_sum(yc * yc) * inv
    rstd = head_spread(lax.rsqrt(var + RWKV_GN_EPS))
    yn = yc * rstd * ln_g + ln_b
    bonus = head_spread(head_sum(r * kmod * r_k)) * v
    return ((yn + bonus) * g,)


def f_ssd_post(y, xs, z, d_skip, norm_g):
    sel_t = _head_sel_t(LANES, SSM_DIM, HEAD_DIM)
    d_e = jnp.sum(mmf(jnp.broadcast_to(d_skip, (SUBLANES, LANES)), sel_t), axis=0, keepdims=True) * (1.0 / SUBLANES)
    u = (y + xs * d_e) * _silu(z)
    first = _iota(u.shape, 1) < (SSM_DIM // 2)
    uu = u * u
    inv = 2.0 / SSM_DIM
    ms0 = jnp.sum(jnp.where(first, uu, 0.0), axis=-1, keepdims=True) * inv
    ms1 = jnp.sum(jnp.where(first, 0.0, uu), axis=-1, keepdims=True) * inv
    ms = jnp.where(first, ms0, ms1)
    return (u * lax.rsqrt(ms + 1e-5) * norm_g,)


def f_mla_pre(cq, ckv, qg, wq_nope, wq_rope, kvg, wk_nope, wv):
    def rms(x, g):
        return x * lax.rsqrt(jnp.mean(x * x, axis=-1, keepdims=True) + 1e-6) * g
    q_in, kv_in = rms(cq, qg), rms(ckv, kvg)
    return mmb(q_in, wq_nope), mmb(q_in, wq_rope), mmb(kv_in, wk_nope), mmb(kv_in, wv)


def f_same(*xs):
    return xs


def f_concat(*xs):
    return (jnp.concatenate(xs, axis=1),)


CONV_TILE = 256


def _shift_down(x, s, row):
    return x if s == 0 else jnp.where(row >= s, pltpu.roll(x, s, 0), 0.0)


def _shift_up(x, s, row, t):
    return x if s == 0 else jnp.where(row < t - s, pltpu.roll(x, t - s, 0), 0.0)


def dwconv_fwd(name, u, colmap, w, b, taps, silu, upmap=None, out_dtype=F32):
    bsz, t, _ = u.shape
    c = w.shape[1]
    tc = CONV_TILE
    has_up = upmap is not None

    def body(*refs):
        u_ref, w_ref, b_ref = refs[:3]
        o_ref = refs[-1]
        uv = u_ref[...].astype(F32)
        wv = w_ref[...]
        row = _iota(uv.shape, 0)
        acc = jnp.broadcast_to(b_ref[...], uv.shape)
        for i in range(taps):
            acc = acc + wv[i:i + 1, :] * _shift_down(uv, taps - 1 - i, row)
        if silu:
            acc = _silu(acc)
        if has_up:
            acc = acc * refs[3][...].astype(F32)
        o_ref[...] = acc.astype(out_dtype)

    specs = [pl.BlockSpec((None, t, tc), lambda bb, j: (bb, 0, colmap(j))),
             pl.BlockSpec((taps, tc), lambda bb, j: (0, j)),
             pl.BlockSpec((1, tc), lambda bb, j: (0, j))]
    ins = [u, w, b]
    if has_up:
        specs.append(pl.BlockSpec((None, t, tc), lambda bb, j: (bb, 0, upmap(j))))
        ins.append(u)
    return pl.pallas_call(
        body, name=name, grid=(bsz, c // tc), in_specs=specs,
        out_specs=pl.BlockSpec((None, t, tc), lambda bb, j: (bb, 0, j)),
        out_shape=jax.ShapeDtypeStruct((bsz, t, c), out_dtype),
        compiler_params=_params(("parallel", "parallel")),
    )(*ins)


def dwconv_bwd(name, u, colmap, w, b, taps, silu, dout, upmap=None, grad_dtype=F32):
    bsz, t, _ = u.shape
    c = w.shape[1]
    tc = CONV_TILE
    has_up = upmap is not None

    def body(*refs):
        u_ref, w_ref, b_ref, d_ref = refs[:4]
        nin = 5 if has_up else 4
        du_ref, dw_ref, db_ref = refs[nin:nin + 3]
        uv = u_ref[...].astype(F32)
        wv = w_ref[...]
        dv = d_ref[...]
        row = _iota(uv.shape, 0)
        shifted = [_shift_down(uv, taps - 1 - i, row) for i in range(taps)]
        cg = jnp.broadcast_to(b_ref[...], uv.shape)
        for i in range(taps):
            cg = cg + wv[i:i + 1, :] * shifted[i]
        if silu:
            sg = jax.nn.sigmoid(cg)
            act = cg * sg
            dact_dcg = sg * (1.0 + cg * (1.0 - sg))
        else:
            act = cg
            dact_dcg = None
        if has_up:
            refs[nin + 3][...] = (dv * act).astype(grad_dtype)
            dv = dv * refs[4][...].astype(F32)
        dcg = dv * dact_dcg if silu else dv
        du = jnp.zeros_like(uv)
        for i in range(taps):
            du = du + wv[i:i + 1, :] * _shift_up(dcg, taps - 1 - i, row, t)
        du_ref[...] = du.astype(grad_dtype)

        @pl.when(pl.program_id(1) == 0)
        def _():
            dw_ref[...] = jnp.zeros_like(dw_ref)
            db_ref[...] = jnp.zeros_like(db_ref)

        for i in range(taps):
            dw_ref[i:i + 1, :] += jnp.sum(dcg * shifted[i], axis=0, keepdims=True)
        db_ref[...] += jnp.sum(dcg, axis=0, keepdims=True)

    specs = [pl.BlockSpec((None, t, tc), lambda j, bb: (bb, 0, colmap(j))),
             pl.BlockSpec((taps, tc), lambda j, bb: (0, j)),
             pl.BlockSpec((1, tc), lambda j, bb: (0, j)),
             pl.BlockSpec((None, t, tc), lambda j, bb: (bb, 0, j))]
    ins = [u, w, b, dout]
    if has_up:
        specs.append(pl.BlockSpec((None, t, tc), lambda j, bb: (bb, 0, upmap(j))))
        ins.append(u)
    big = pl.BlockSpec((None, t, tc), lambda j, bb: (bb, 0, j))
    out_specs = [big, pl.BlockSpec((taps, tc), lambda j, bb: (0, j)), pl.BlockSpec((1, tc), lambda j, bb: (0, j))]
    out_shape = [jax.ShapeDtypeStruct((bsz, t, c), grad_dtype), jax.ShapeDtypeStruct((taps, c), F32),
                 jax.ShapeDtypeStruct((1, c), F32)]
    if has_up:
        out_specs.append(big)
        out_shape.append(jax.ShapeDtypeStruct((bsz, t, c), grad_dtype))
    return pl.pallas_call(
        body, name=name, grid=(c // tc, bsz), in_specs=specs, out_specs=out_specs, out_shape=out_shape,
        compiler_params=_params(("parallel", "arbitrary")),
    )(*ins)


def _each(f, *lists):
    return [f(*xs) for xs in zip(*lists)]


def rwkv_chunk(s0, r, lw, k, v, al, be):
    c = r[0].shape[0]
    ii, jj = _iota((c, c), 0), _iota((c, c), 1)
    incl, strict = ii >= jj, ii > jj
    ones_incl = jnp.where(incl, 1.0, 0.0)
    eye = jnp.where(ii == jj, 1.0, 0.0)
    cum = _each(lambda x: mmf(ones_incl, x), lw)
    gam_inv = _each(lambda x: jnp.exp(-x), cum)
    at = _each(lambda a_, c_, l_: a_ * jnp.exp(c_ - l_), al, cum, lw)
    rt = _each(lambda r_, c_: r_ * jnp.exp(c_), r, cum)
    bt = _each(lambda b_, g_: b_ * g_, be, gam_inv)
    kt = _each(lambda k_, g_: k_ * g_, k, gam_inv)
    a_b = _each(lambda x, y_: jnp.where(strict, mm3_nt(x, y_), 0.0), at, bt)
    a_k = _each(lambda x, y_: jnp.where(strict, mm3_nt(x, y_), 0.0), at, kt)
    rhs0 = _each(mm3_nt, at, s0)
    rhs = _each(lambda x, a_, v_: x + mm3(a_, v_), rhs0, a_k, v)
    p = _each(lambda x: eye + x, a_b)
    m = a_b
    for _ in range(int(math.log2(c)) - 1):
        m = _each(mm3, m, m)
        p = _each(lambda p_, m_: p_ + mm3(p_, m_), p, m)
    u = _each(mm3, p, rhs)
    r_b = _each(lambda x, y_: jnp.where(incl, mm3_nt(x, y_), 0.0), rt, bt)
    r_k = _each(lambda x, y_: jnp.where(incl, mm3_nt(x, y_), 0.0), rt, kt)
    y0 = _each(mm3_nt, rt, s0)
    y1 = _each(lambda y_, b_, u_: y_ + mm3(b_, u_), y0, r_b, u)
    y = _each(lambda y_, k_, v_: y_ + mm3(k_, v_), y1, r_k, v)
    su = _each(mm3_tn, u, bt)
    sv = _each(mm3_tn, v, kt)
    s1 = _each(lambda s_, a_, b_, l_: (s_ + a_ + b_) * jnp.exp(jnp.sum(l_, axis=0, keepdims=True)), s0, su, sv, lw)
    return y, s1


def rwkv_scan_fwd(r, lw, k, v, al, be, ride=None):
    bsz, h, t, d = r.shape
    c = RWKV_CHUNK
    nc = t // c
    grid = (bsz, nc)
    r_in, r_specs, r_out, r_ospecs, r_scr = _ride_args(ride)

    def body(*refs):
        r_ref, lw_ref, k_ref, v_ref, al_ref, be_ref = refs[:6]
        y_ref, st_ref = refs[6 + len(r_in):8 + len(r_in)]
        s_scr = refs[8 + 2 * len(r_in)]
        if ride is not None:
            first, last = _grid_first_last(grid)
            copies = _ride_start(refs[6:6 + len(r_in)], refs[8 + len(r_in):8 + 2 * len(r_in)], refs[-3:], ride[1], first)

        @pl.when(pl.program_id(1) == 0)
        def _():
            s_scr[...] = jnp.zeros_like(s_scr)

        heads = lambda ref: [ref[hh] for hh in range(h)]
        s0 = heads(s_scr)
        y, s1 = rwkv_chunk(s0, heads(r_ref), heads(lw_ref), heads(k_ref), heads(v_ref), heads(al_ref),
                           heads(be_ref))
        for hh in range(h):
            st_ref[hh] = s0[hh]
            y_ref[hh] = y[hh]
            s_scr[hh] = s1[hh]
        if ride is not None:
            _ride_wait(copies, last)

    seq = pl.BlockSpec((None, h, c, d), lambda b, i: (b, 0, i, 0))
    return pl.pallas_call(
        body, name="rwkv_scan_fwd", grid=grid, in_specs=[seq] * 6 + r_specs,
        out_specs=[seq, pl.BlockSpec((None, h, None, d, d), lambda b, i: (b, 0, i, 0, 0))] + r_ospecs,
        out_shape=[jax.ShapeDtypeStruct((bsz, h, t, d), F32), jax.ShapeDtypeStruct((bsz, h, nc, d, d), F32)] + r_out,
        scratch_shapes=[pltpu.VMEM((h, d, d), F32)] + r_scr,
        compiler_params=_params(("arbitrary", "arbitrary")),
    )(r, lw, k, v, al, be, *r_in)


def rwkv_scan_bwd(r, lw, k, v, al, be, states, dy, ride=None):
    bsz, h, t, d = r.shape
    c = RWKV_CHUNK
    nc = t // c
    grid = (bsz, nc)
    r_in, r_specs, r_out, r_ospecs, r_scr = _ride_args(ride)

    def body(*refs):
        r_ref, lw_ref, k_ref, v_ref, al_ref, be_ref, st_ref, dy_ref = refs[:8]
        nin = 8 + len(r_in)
        dr_ref, dlw_ref, dk_ref, dv_ref, dal_ref, dbe_ref = refs[nin:nin + 6]
        ds_scr = refs[nin + 6 + len(r_in)]
        if ride is not None:
            first, last = _grid_first_last(grid)
            copies = _ride_start(refs[8:nin], refs[nin + 6:nin + 6 + len(r_in)], refs[-3:], ride[1], first)

        @pl.when(pl.program_id(1) == 0)
        def _():
            ds_scr[...] = jnp.zeros_like(ds_scr)

        heads = lambda ref: [ref[hh] for hh in range(h)]
        _, vjp = jax.vjp(rwkv_chunk, heads(st_ref), heads(r_ref), heads(lw_ref), heads(k_ref), heads(v_ref),
                         heads(al_ref), heads(be_ref))
        grads = vjp((heads(dy_ref), heads(ds_scr)))
        for ref, gl in zip((ds_scr, dr_ref, dlw_ref, dk_ref, dv_ref, dal_ref, dbe_ref), grads):
            for hh in range(h):
                ref[hh] = gl[hh]
        if ride is not None:
            _ride_wait(copies, last)

    seq = pl.BlockSpec((None, h, c, d), lambda b, i: (b, 0, nc - 1 - i, 0))
    st = pl.BlockSpec((None, h, None, d, d), lambda b, i: (b, 0, nc - 1 - i, 0, 0))
    return pl.pallas_call(
        body, name="rwkv_scan_bwd", grid=grid, in_specs=[seq] * 6 + [st, seq] + r_specs,
        out_specs=[seq] * 6 + r_ospecs, out_shape=[jax.ShapeDtypeStruct((bsz, h, t, d), F32)] * 6 + r_out,
        scratch_shapes=[pltpu.VMEM((h, d, d), F32)] + r_scr,
        compiler_params=_params(("arbitrary", "arbitrary")),
    )(r, lw, k, v, al, be, states, dy, *r_in)


def ssd_chunk(st, xs, bm, cm, dtr, dt_bias, a_log):
    n = SSM_CHUNK
    ii, jj = _iota((n, n), 0), _iota((n, n), 1)
    incl = ii >= jj
    lane = _iota((n, LANES), 1)
    dt = _softplus(dtr + dt_bias)
    a = dt * (-jnp.exp(a_log))
    acum = mmf(jnp.where(incl, 1.0, 0.0), a)
    last_row = jnp.where(jj == n - 1, 1.0, 0.0)
    cb = [mm3_nt(cm[g], bm[g]) for g in range(2)]
    pairs, heads = range(4), range(SSM_HEADS)
    e_m = [jnp.where(_iota((LANES, LANES), 0) == 2 * m + _iota((LANES, LANES), 1) // HEAD_DIM, 1.0, 0.0)
           for m in pairs]
    dt_m = [mmf(dt, e_m[m]) for m in pairs]
    ac_m = [mmf(acum, e_m[m]) for m in pairs]
    x = [xs[m] * dt_m[m] for m in pairs]
    last_m = [mmf(last_row, ac_m[m]) for m in pairs]
    colb = [mmf(acum, jnp.where(_iota((LANES, n), 0) == h, 1.0, 0.0)) for h in heads]
    decay = [jnp.exp(jnp.where(incl, colb[h] - colb[h].T, NEG_BIG)) for h in heads]
    yh = [mm3(cb[h // 4] * decay[h], x[h // 2]) for h in heads]
    y_off = [mm3(cm[m // 2], st[m]) for m in pairs]
    ys = [jnp.where(lane // HEAD_DIM == 0, yh[2 * m], yh[2 * m + 1]) + jnp.exp(ac_m[m]) * y_off[m] for m in pairs]
    st_in = [mm3_tn(bm[m // 2], x[m] * jnp.exp(last_m[m] - ac_m[m])) for m in pairs]
    st_new = [jnp.exp(last_m[m]) * st[m] + st_in[m] for m in pairs]
    return tuple(ys), tuple(st_new)


def _ssd_load(xbc_ref, dtr_ref):
    xs = tuple(xbc_ref[:, m * LANES:(m + 1) * LANES] for m in range(4))
    bm = tuple(xbc_ref[:, SSM_DIM + g * LANES:SSM_DIM + (g + 1) * LANES] for g in range(2))
    cm = tuple(xbc_ref[:, SSM_DIM + 2 * LANES + g * LANES:SSM_DIM + 2 * LANES + (g + 1) * LANES] for g in range(2))
    return xs, bm, cm, dtr_ref[...]


def ssd_fwd(xbc, proj, dt_col, dt_bias, a_log):
    bsz, t, _ = xbc.shape
    n = SSM_CHUNK
    nc = t // n

    def body(xbc_ref, dtr_ref, dtb_ref, al_ref, y_ref, st_ref, s_scr):
        @pl.when(pl.program_id(1) == 0)
        def _():
            s_scr[...] = jnp.zeros_like(s_scr)

        st = tuple(s_scr[m] for m in range(4))
        for m in range(4):
            st_ref[m] = st[m]
        xs, bm, cm, dtr = _ssd_load(xbc_ref, dtr_ref)
        ys, st_new = ssd_chunk(st, xs, bm, cm, dtr, dtb_ref[...], al_ref[...])
        for m in range(4):
            y_ref[:, m * LANES:(m + 1) * LANES] = ys[m]
            s_scr[m] = st_new[m]

    vec = pl.BlockSpec((1, LANES), lambda b, i: (0, 0))
    return pl.pallas_call(
        body, name="ssd_fwd", grid=(bsz, nc),
        in_specs=[pl.BlockSpec((None, n, 2 * SSM_DIM), lambda b, i: (b, i, 0)),
                  pl.BlockSpec((None, n, LANES), lambda b, i: (b, i, dt_col)), vec, vec],
        out_specs=[pl.BlockSpec((None, n, SSM_DIM), lambda b, i: (b, i, 0)),
                   pl.BlockSpec((None, None, 4, SSM_STATE, LANES), lambda b, i: (b, i, 0, 0, 0))],
        out_shape=[jax.ShapeDtypeStruct((bsz, t, SSM_DIM), F32),
                   jax.ShapeDtypeStruct((bsz, nc, 4, SSM_STATE, LANES), F32)],
        scratch_shapes=[pltpu.VMEM((4, SSM_STATE, LANES), F32)],
        compiler_params=_params(("parallel", "arbitrary")),
    )(xbc, proj, dt_bias, a_log)


def ssd_bwd(xbc, proj, dt_col, dt_bias, a_log, states, dy, dxs_extra):
    bsz, t, _ = xbc.shape
    n = SSM_CHUNK
    nc = t // n

    def body(xbc_ref, dtr_ref, dtb_ref, al_ref, st_ref, dy_ref, ex_ref,
             dxbc_ref, ddtr_ref, ddtb_ref, dal_ref, ds_scr):
        first = jnp.logical_and(pl.program_id(0) == 0, pl.program_id(1) == 0)

        @pl.when(pl.program_id(1) == 0)
        def _():
            ds_scr[...] = jnp.zeros_like(ds_scr)

        @pl.when(first)
        def _():
            ddtb_ref[...] = jnp.zeros_like(ddtb_ref)
            dal_ref[...] = jnp.zeros_like(dal_ref)

        st = tuple(st_ref[m] for m in range(4))
        xs, bm, cm, dtr = _ssd_load(xbc_ref, dtr_ref)
        _, vjp = jax.vjp(ssd_chunk, st, xs, bm, cm, dtr, dtb_ref[...], al_ref[...])
        dys = tuple(dy_ref[:, m * LANES:(m + 1) * LANES] for m in range(4))
        dst_in = tuple(ds_scr[m] for m in range(4))
        dst, dxs, dbm, dcm, ddtr, ddtb, dal = vjp((dys, dst_in))
        for m in range(4):
            ds_scr[m] = dst[m]
            sl = slice(m * LANES, (m + 1) * LANES)
            dxbc_ref[:, sl] = dxs[m] + ex_ref[:, sl]
        for g in range(2):
            dxbc_ref[:, SSM_DIM + g * LANES:SSM_DIM + (g + 1) * LANES] = dbm[g]
            dxbc_ref[:, SSM_DIM + 2 * LANES + g * LANES:SSM_DIM + 2 * LANES + (g + 1) * LANES] = dcm[g]
        ddtr_ref[...] = ddtr
        ddtb_ref[...] += ddtb
        dal_ref[...] += dal

    vec = pl.BlockSpec((1, LANES), lambda b, i: (0, 0))
    rev = lambda b, i: (b, nc - 1 - i, 0)
    return pl.pallas_call(
        body, name="ssd_bwd", grid=(bsz, nc),
        in_specs=[pl.BlockSpec((None, n, 2 * SSM_DIM), rev),
                  pl.BlockSpec((None, n, LANES), lambda b, i: (b, nc - 1 - i, dt_col)), vec, vec,
                  pl.BlockSpec((None, None, 4, SSM_STATE, LANES), lambda b, i: (b, nc - 1 - i, 0, 0, 0)),
                  pl.BlockSpec((None, n, SSM_DIM), rev), pl.BlockSpec((None, n, SSM_DIM), rev)],
        out_specs=[pl.BlockSpec((None, n, 2 * SSM_DIM), rev), pl.BlockSpec((None, n, LANES), rev), vec, vec],
        out_shape=[jax.ShapeDtypeStruct((bsz, t, 2 * SSM_DIM), F32), jax.ShapeDtypeStruct((bsz, t, LANES), F32),
                   jax.ShapeDtypeStruct((1, LANES), F32), jax.ShapeDtypeStruct((1, LANES), F32)],
        scratch_shapes=[pltpu.VMEM((4, SSM_STATE, LANES), F32)],
        compiler_params=_params(("arbitrary", "arbitrary")),
    )(xbc, proj, dt_bias, a_log, states, dy, dxs_extra)


def sb_block(q, kj, vj, carry, maskf):
    mask = maskf > 0.5
    z = _each(lambda q_, k_: dg(q_, k_, 1, 1, P_BF16) * (HEAD_DIM ** -0.5), q, kj)
    lk = _each(lambda z_: jnp.where(mask, -_softplus(z_), 0.0), z)
    sfx = _each(suffix_sum, lk)
    att = _each(lambda z_, c_, s_: jnp.exp(jnp.where(mask, z_ + c_ + s_, NEG_BIG)), z, carry, sfx)
    out = _each(mmb, att, vj)
    return out, _each(lambda c_, k_: c_ + jnp.sum(k_, axis=1, keepdims=True), carry, lk)


def _sb_mask(qi, j):
    n = Q_BLOCK
    return jnp.where(j * n + _iota((n, n), 1) < qi * n + _iota((n, n), 0), 1.0, 0.0)


def sb_fwd(q, k, v, ride=None):
    bsz, h, t, d = q.shape
    n = Q_BLOCK
    hp = ATTN_FWD_HEADS_PER_STEP
    grid = (bsz, h // hp, t // n)
    r_in, r_specs, r_out, r_ospecs, r_scr = _ride_args(ride)

    def body(*refs):
        q_ref, k_ref, v_ref = refs[:3]
        o_ref, c_ref = refs[3 + len(r_in):5 + len(r_in)]
        if ride is not None:
            first, last = _grid_first_last(grid)
            copies = _ride_start(refs[3:3 + len(r_in)], refs[5 + len(r_in):5 + 2 * len(r_in)], refs[-3:], ride[1], first)
        qi = pl.program_id(2)
        lane = _iota((n, LANES), 1)

        c_ref[...] = jnp.zeros_like(c_ref)
        o_ref[...] = jnp.zeros_like(o_ref)

        def step(i, carry):
            j = qi - i
            rows = pl.ds(pl.multiple_of(j * n, n), n)
            for hh in range(hp):
                c_ref[hh] = jnp.where(lane == j, carry[hh], c_ref[hh])
            o, carry = sb_block([q_ref[hh] for hh in range(hp)], [k_ref[hh, rows, :] for hh in range(hp)],
                                [v_ref[hh, rows, :] for hh in range(hp)], carry, _sb_mask(qi, j))
            for hh in range(hp):
                o_ref[hh] += o[hh]
            return carry

        lax.fori_loop(0, qi + 1, step, [jnp.zeros((n, 1), F32) for _ in range(hp)])
        if ride is not None:
            _ride_wait(copies, last)

    blk = pl.BlockSpec((None, hp, n, d), lambda b, hg, i: (b, hg, i, 0))
    cblk = pl.BlockSpec((None, hp, n, LANES), lambda b, hg, i: (b, hg, i, 0))
    full = pl.BlockSpec((None, hp, t, d), lambda b, hg, i: (b, hg, 0, 0))
    return pl.pallas_call(
        body, name="sb_fwd", grid=grid, in_specs=[blk, full, full] + r_specs, out_specs=[blk, cblk] + r_ospecs,
        out_shape=[jax.ShapeDtypeStruct((bsz, h, t, d), F32), jax.ShapeDtypeStruct((bsz, h, t, LANES), F32)] + r_out,
        scratch_shapes=r_scr, compiler_params=_params(("arbitrary", "arbitrary", "arbitrary")),
    )(q, k, v, *r_in)


def sb_bwd(q, k, v, kept, do, ride=None):
    bsz, h, t, d = q.shape
    n = Q_BLOCK
    hp = SB_HEADS_PER_STEP
    grid = (bsz, h // hp, t // n)
    r_in, r_specs, r_out, r_ospecs, r_scr = _ride_args(ride)

    def body(*refs):
        q_ref, k_ref, v_ref, c_ref, do_ref = refs[:5]
        nin = 5 + len(r_in)
        dq_ref, dk_ref, dv_ref = refs[nin:nin + 3]
        if ride is not None:
            first, last = _grid_first_last(grid)
            copies = _ride_start(refs[5:nin], refs[nin + 3:nin + 3 + len(r_in)], refs[-3:], ride[1], first)
        qi = pl.program_id(2)

        @pl.when(qi == 0)
        def _():
            dk_ref[...] = jnp.zeros_like(dk_ref)
            dv_ref[...] = jnp.zeros_like(dv_ref)

        heads = range(hp)
        qv = [q_ref[hh] for hh in heads]
        kept_v = [c_ref[hh] for hh in heads]
        lane = _iota((n, LANES), 1)

        dq_ref[...] = jnp.zeros_like(dq_ref)

        def bwd_step(j, dcarry):
            rows = pl.ds(pl.multiple_of(j * n, n), n)
            carry_in = [jnp.sum(jnp.where(lane == j, t_, 0.0), axis=1, keepdims=True) for t_ in kept_v]
            _, vjp = jax.vjp(sb_block, qv, [k_ref[hh, rows, :] for hh in heads],
                             [v_ref[hh, rows, :] for hh in heads], carry_in, _sb_mask(qi, j))
            dqj, dkj, dvj, dc, _ = vjp(([do_ref[hh] for hh in heads], dcarry))
            for hh in heads:
                dq_ref[hh] += dqj[hh]
                dk_ref[hh, rows, :] += dkj[hh]
                dv_ref[hh, rows, :] += dvj[hh]
            return dc

        lax.fori_loop(0, qi + 1, bwd_step, [jnp.zeros((n, 1), F32) for _ in heads])
        if ride is not None:
            _ride_wait(copies, last)

    blk = pl.BlockSpec((None, hp, n, d), lambda b, hg, i: (b, hg, i, 0))
    cblk = pl.BlockSpec((None, hp, n, LANES), lambda b, hg, i: (b, hg, i, 0))
    full = pl.BlockSpec((None, hp, t, d), lambda b, hg, i: (b, hg, 0, 0))
    shp = jax.ShapeDtypeStruct((bsz, h, t, d), F32)
    return pl.pallas_call(
        body, name="sb_bwd", grid=grid, in_specs=[blk, full, full, cblk, blk] + r_specs,
        out_specs=[blk, full, full] + r_ospecs, out_shape=[shp, shp, shp] + r_out,
        scratch_shapes=r_scr, compiler_params=_params(("arbitrary", "arbitrary", "arbitrary")),
    )(q, k, v, kept, do, *r_in)


def _bdot(a, b, ca, cb):
    return _dg_raw(a, b, ca, cb, P_BF16)


def _mla_scores(qn, qp, knj, kpj, qi, j):
    n = Q_BLOCK
    mask = j * n + _iota((n, n), 1) <= qi * n + _iota((n, n), 0)
    scale = (MLA_NOPE + MLA_ROPE) ** -0.5
    return _each(lambda a_, b_, k_: jnp.where(mask, (_bdot(a_, k_, 1, 1) + _bdot(b_, kpj, 1, 1)) * scale, NEG_BIG),
                 qn, qp, knj)


def _mla_specs(t, hp):
    n = Q_BLOCK
    return (pl.BlockSpec((None, hp, n, MLA_NOPE), lambda b, hg, i: (b, hg, i, 0)),
            pl.BlockSpec((None, hp, n, MLA_ROPE), lambda b, hg, i: (b, hg, i, 0)),
            pl.BlockSpec((None, hp, t, MLA_NOPE), lambda b, hg, i: (b, hg, 0, 0)),
            pl.BlockSpec((None, None, t, MLA_ROPE), lambda b, hg, i: (b, 0, 0, 0)),
            pl.BlockSpec((None, hp, n, 1), lambda b, hg, i: (b, hg, i, 0)))


def mla_fwd(qn, qp, kn, kp, v):
    bsz, h, t, _ = qn.shape
    n, hp = Q_BLOCK, ATTN_FWD_HEADS_PER_STEP
    heads = range(hp)

    def body(qn_ref, qp_ref, kn_ref, kp_ref, v_ref, o_ref, lse_ref):
        qi = pl.program_id(2)
        qn_v, qp_v = [qn_ref[hh] for hh in heads], [qp_ref[hh] for hh in heads]

        o_ref[...] = jnp.zeros_like(o_ref)

        def step(j, state):
            m, l = state
            rows = pl.ds(pl.multiple_of(j * n, n), n)
            s = _mla_scores(qn_v, qp_v, [kn_ref[hh, rows, :] for hh in heads], kp_ref[rows, :], qi, j)
            m_new = _each(lambda m_, s_: jnp.maximum(m_, jnp.max(s_, axis=1, keepdims=True)), m, s)
            p = _each(lambda s_, m_: jnp.exp(s_ - m_), s, m_new)
            corr = _each(lambda a_, b_: jnp.exp(a_ - b_), m, m_new)
            l = _each(lambda l_, c_, p_: l_ * c_ + jnp.sum(p_, axis=1, keepdims=True), l, corr, p)
            pv = _each(lambda p_, v_: _bdot(p_, v_, 1, 0), p, [v_ref[hh, rows, :] for hh in heads])
            for hh in heads:
                o_ref[hh] = o_ref[hh] * corr[hh] + pv[hh]
            return m_new, l

        init = ([jnp.full((n, 1), NEG_BIG, F32) for _ in heads], [jnp.zeros((n, 1), F32) for _ in heads])
        m, l = lax.fori_loop(0, qi + 1, step, init)
        for hh in heads:
            o_ref[hh] = o_ref[hh] / l[hh]
            lse_ref[hh] = m[hh] + jnp.log(l[hh])

    qn_s, qp_s, kn_s, kp_s, row_s = _mla_specs(t, hp)
    return pl.pallas_call(
        body, name="mla_fwd", grid=(bsz, h // hp, t // n), in_specs=[qn_s, qp_s, kn_s, kp_s, kn_s],
        out_specs=[qn_s, row_s],
        out_shape=[jax.ShapeDtypeStruct(qn.shape, F32), jax.ShapeDtypeStruct((bsz, h, t, 1), F32)],
        compiler_params=_params(("parallel", "parallel", "arbitrary")),
    )(qn, qp, kn, kp, v)


def mla_bwd(qn, qp, kn, kp, v, o, lse, do):
    bsz, h, t, _ = qn.shape
    n, hp = Q_BLOCK, MLA_HEADS_PER_STEP
    heads = range(hp)
    scale = (MLA_NOPE + MLA_ROPE) ** -0.5

    def body(qn_ref, qp_ref, kn_ref, kp_ref, v_ref, o_ref, lse_ref, do_ref,
             dqn_ref, dqp_ref, dkn_ref, dkp_ref, dv_ref):
        hg, qi = pl.program_id(1), pl.program_id(2)

        @pl.when(qi == 0)
        def _():
            dkn_ref[...] = jnp.zeros_like(dkn_ref)
            dv_ref[...] = jnp.zeros_like(dv_ref)

        @pl.when(jnp.logical_and(qi == 0, hg == 0))
        def _():
            dkp_ref[...] = jnp.zeros_like(dkp_ref)

        qn_v, qp_v = [qn_ref[hh] for hh in heads], [qp_ref[hh] for hh in heads]
        do_v, lse_v = [do_ref[hh] for hh in heads], [lse_ref[hh] for hh in heads]
        dsum = [jnp.sum(do_v[hh] * o_ref[hh], axis=1, keepdims=True) for hh in heads]

        dqn_ref[...] = jnp.zeros_like(dqn_ref)
        dqp_ref[...] = jnp.zeros_like(dqp_ref)

        def step(j, _):
            rows = pl.ds(pl.multiple_of(j * n, n), n)
            knj, vj, kpj = [kn_ref[hh, rows, :] for hh in heads], [v_ref[hh, rows, :] for hh in heads], kp_ref[rows, :]
            s = _mla_scores(qn_v, qp_v, knj, kpj, qi, j)
            p = _each(lambda s_, l_: jnp.exp(s_ - l_), s, lse_v)
            dp = _each(lambda d_, v_: _bdot(d_, v_, 1, 1), do_v, vj)
            ds = _each(lambda p_, dp_, d_: p_ * (dp_ - d_) * scale, p, dp, dsum)
            dqn = _each(lambda ds_, k_: _bdot(ds_, k_, 1, 0), ds, knj)
            dqp = _each(lambda ds_: _bdot(ds_, kpj, 1, 0), ds)
            dkn = _each(lambda ds_, q_: _bdot(ds_, q_, 0, 0), ds, qn_v)
            dv = _each(lambda p_, d_: _bdot(p_, d_, 0, 0), p, do_v)
            dkp = _each(lambda ds_, q_: _bdot(ds_, q_, 0, 0), ds, qp_v)
            for hh in heads:
                dqn_ref[hh] += dqn[hh]
                dqp_ref[hh] += dqp[hh]
                dkn_ref[hh, rows, :] += dkn[hh]
                dv_ref[hh, rows, :] += dv[hh]
            dkp_ref[rows, :] += functools.reduce(lambda a_, b_: a_ + b_, dkp)
            return 0

        lax.fori_loop(0, qi + 1, step, 0)

    qn_s, qp_s, kn_s, kp_s, row_s = _mla_specs(t, hp)
    return pl.pallas_call(
        body, name="mla_bwd", grid=(bsz, h // hp, t // n),
        in_specs=[qn_s, qp_s, kn_s, kp_s, kn_s, qn_s, row_s, qn_s],
        out_specs=[qn_s, qp_s, kn_s, kp_s, kn_s],
        out_shape=[jax.ShapeDtypeStruct(qn.shape, F32), jax.ShapeDtypeStruct(qp.shape, F32),
                   jax.ShapeDtypeStruct(kn.shape, F32), jax.ShapeDtypeStruct(kp.shape, F32),
                   jax.ShapeDtypeStruct(v.shape, F32)],
        compiler_params=_params(("parallel", "arbitrary", "arbitrary")),
    )(qn, qp, kn, kp, v, o, lse, do)


def rope(name, x, pos, inv_freq, sign):
    bsz, hx, t, d = x.shape
    half = d // 2

    tt = _largest_tile(t, 512, SUBLANES)

    def body(x_ref, pos_ref, f_ref, o_ref):
        ang = pos_ref[...].astype(F32) * f_ref[...]
        cos, sin = jnp.cos(ang), sign * jnp.sin(ang)
        ri, ci = _iota((d, d), 0), _iota((d, d), 1)
        rot = jnp.where(ri == ci + half, -1.0, 0.0) + jnp.where(ri + half == ci, 1.0, 0.0)
        for hh in range(hx):
            xv = x_ref[hh]
            o_ref[hh] = xv * cos + mmf(xv, rot) * sin

    blk = pl.BlockSpec((None, hx, tt, d), lambda b, i: (b, 0, i, 0))
    return pl.pallas_call(
        body, name=name, grid=(bsz, t // tt),
        in_specs=[blk, pl.BlockSpec((None, tt, 1), lambda b, i: (b, i, 0)), pl.BlockSpec((1, d), lambda b, i: (0, 0))],
        out_specs=blk, out_shape=jax.ShapeDtypeStruct(x.shape, F32),
        compiler_params=_params(("parallel", "parallel")),
    )(x, pos, inv_freq)


def loss_head(h, target, tm):
    n, d = h.shape

    def body(h_ref, t_ref, dh_ref, l_ref):
        @pl.when(pl.program_id(0) == 0)
        def _():
            l_ref[...] = jnp.zeros_like(l_ref)

        e = h_ref[...] - t_ref[...]
        dh_ref[...] = e * (1.0 / d)
        l_ref[...] += jnp.sum(e * e, axis=(0, 1), keepdims=True) * (0.5 / d)

    row = pl.BlockSpec((tm, d), lambda i: (i, 0))
    dh, l = pl.pallas_call(
        body, name="loss_head", grid=(n // tm,), in_specs=[row, row],
        out_specs=[row, pl.BlockSpec((SUBLANES, LANES), lambda i: (0, 0))],
        out_shape=[jax.ShapeDtypeStruct((n, d), F32), jax.ShapeDtypeStruct((SUBLANES, LANES), F32)],
        compiler_params=_params(("arbitrary",)),
    )(h, target)
    return dh, l[0, 0]


def _exchange_copies(src_refs, out_refs, send_sems, recv_sems, local_sems, gather):
    x, y, c = lax.axis_index("x"), lax.axis_index("y"), lax.axis_index("c")
    me = 4 * x + 2 * y + c
    local, remote = [], []
    for p, (src_ref, out_ref) in enumerate(zip(src_refs, out_refs)):
        local.append(pltpu.make_async_copy(src_ref if gather else src_ref.at[me], out_ref.at[me], local_sems.at[p]))
        for m in range(1, N_DEV):
            px, py, pc = x ^ (m >> 2), y ^ ((m >> 1) & 1), c ^ (m & 1)
            peer = 4 * px + 2 * py + pc
            remote.append(pltpu.make_async_remote_copy(
                src_ref=src_ref if gather else src_ref.at[peer], dst_ref=out_ref.at[me],
                send_sem=send_sems.at[p, m], recv_sem=recv_sems.at[p, m],
                device_id=(px, py, pc), device_id_type=pl.DeviceIdType.MESH))
    return local, remote


def _exchange_start(copies):
    local, remote = copies
    for cp in local + remote:
        cp.start()


def _exchange_wait(copies):
    local, remote = copies
    for cp in remote:
        cp.wait_recv()
    for cp in remote:
        cp.wait_send()
    for cp in local:
        cp.wait()


def _exchange_scratch(count):
    return [pltpu.SemaphoreType.DMA((count, N_DEV)), pltpu.SemaphoreType.DMA((count, N_DEV)),
            pltpu.SemaphoreType.DMA((count,))]


def _exchange_out(src, gather):
    return jax.ShapeDtypeStruct(((N_DEV,) + src.shape) if gather else src.shape, src.dtype)


def peer_exchange(name, srcs, gather):
    count = len(srcs)

    def body(*refs):
        copies = _exchange_copies(refs[:count], refs[count:2 * count], *refs[2 * count:], gather)
        _exchange_start(copies)
        _exchange_wait(copies)

    hbm = pl.BlockSpec(memory_space=pl.ANY)
    return pl.pallas_call(
        body, name=name, in_specs=[hbm] * count, out_specs=[hbm] * count,
        out_shape=[_exchange_out(s_, gather) for s_ in srcs], scratch_shapes=_exchange_scratch(count),
    )(*srcs)


def _grid_first_last(grid):
    ids = [pl.program_id(a) for a in range(len(grid))]
    first = functools.reduce(jnp.logical_and, [i == 0 for i in ids])
    last = functools.reduce(jnp.logical_and, [i == g - 1 for i, g in zip(ids, grid)])
    return first, last


def _ride_start(src_refs, out_refs, sem_refs, gather, first):
    copies = _exchange_copies(src_refs, out_refs, *sem_refs, gather)

    @pl.when(first)
    def _():
        _exchange_start(copies)

    return copies


def _ride_wait(copies, last):
    @pl.when(last)
    def _():
        _exchange_wait(copies)


def _ride_args(ride):
    if ride is None:
        return [], [], [], [], []
    srcs, gather = ride
    hbm = pl.BlockSpec(memory_space=pl.ANY)
    return (list(srcs), [hbm] * len(srcs), [_exchange_out(s_, gather) for s_ in srcs], [hbm] * len(srcs),
            _exchange_scratch(len(srcs)))


ADAM_BLOCK_BYTES = 4 * 1024 * 1024


def adamw_sum(name, parts, w, m, v):
    r, cols = w.shape
    tr = _largest_tile(r, min(ADAM_ROWS, max(SUBLANES, ADAM_BLOCK_BYTES // (N_DEV * cols * 4))), SUBLANES)

    def body(p_ref, w_ref, m_ref, v_ref, g_ref, d_ref, nm_ref, nv_ref):
        g = p_ref[0]
        for j in range(1, N_DEV):
            g = g + p_ref[j]
        mm_ = ADAM_B1 * m_ref[...] + (1.0 - ADAM_B1) * g
        vv = ADAM_B2 * v_ref[...] + (1.0 - ADAM_B2) * (g * g)
        m_hat = mm_ / (1.0 - ADAM_B1 ** ADAM_STEP)
        v_hat = vv / (1.0 - ADAM_B2 ** ADAM_STEP)
        g_ref[...] = g
        d_ref[...] = -ADAM_LR * (m_hat / (jnp.sqrt(v_hat) + ADAM_EPS) + ADAM_WD * w_ref[...])
        nm_ref[...] = mm_
        nv_ref[...] = vv

    row = pl.BlockSpec((tr, cols), lambda i: (i, 0))
    shp = jax.ShapeDtypeStruct((r, cols), F32)
    return pl.pallas_call(
        body, name=name, grid=(r // tr,),
        in_specs=[pl.BlockSpec((N_DEV, tr, cols), lambda i: (0, i, 0)), row, row, row],
        out_specs=[row] * 4, out_shape=[shp] * 4,
        compiler_params=_params(("parallel",)),
    )(parts, w, m, v)


WEIGHTS = ['l0_w_in', 'rwkv_mix', 'rwkv_w0', 'rwkv_w2', 'rwkv_a0', 'rwkv_a2', 'rwkv_g2', 'rwkv_k_k', 'rwkv_k_a',
           'rwkv_r_k', 'rwkv_ln_g', 'rwkv_ln_b', 'ssm_conv_w', 'ssm_conv_b', 'ssm_dt_bias', 'ssm_a_log', 'ssm_d',
           'ssm_norm_g', 'l0_w_out', 'l0_ln1_g', 'l0_ln1_b', 'ffn0_w_up', 'ffn0_conv_w', 'ffn0_conv_b',
           'ffn0_w_down', 'l0_ln2_g', 'l0_ln2_b', 'l1_w_in', 'mla_q_norm_g', 'mla_w_uq', 'mla_kv_norm_g',
           'mla_w_ukv', 'l1_w_out', 'l1_ln1_g', 'l1_ln1_b', 'ffn1_w_up', 'ffn1_conv_w', 'ffn1_conv_b',
           'ffn1_w_down', 'l1_ln2_g', 'l1_ln2_b']
SHARD_AXIS = {'l0_w_in': 1, 'rwkv_w2': 1, 'rwkv_a2': 1, 'rwkv_g2': 1, 'ssm_conv_w': 1, 'l0_w_out': 0,
              'ffn0_w_up': 1, 'ffn0_conv_w': 1, 'ffn0_w_down': 0, 'l1_w_in': 1, 'mla_w_uq': 1, 'mla_w_ukv': 1,
              'l1_w_out': 0, 'ffn1_w_up': 1, 'ffn1_conv_w': 1, 'ffn1_w_down': 0}
MATMUL_W = ['l0_w_in', 'rwkv_w2', 'rwkv_a2', 'rwkv_g2', 'l0_w_out', 'ffn0_w_up', 'ffn0_w_down', 'l1_w_in',
            'mla_w_uq', 'mla_w_ukv', 'l1_w_out', 'ffn1_w_up', 'ffn1_w_down']
CONV_W = ['ssm_conv_w', 'ffn0_conv_w', 'ffn1_conv_w']
TOK_TILE = 256
ADAM_ROWS = 512


def _ceil_to(size, unit):
    return -(-size // unit) * unit


def _flat_rows(pieces, seg_rows, total_rows):
    unit = seg_rows * LANES
    out, total = [], 0
    for p in pieces:
        p = jnp.concatenate([q.reshape(-1) for q in p]) if isinstance(p, list) else p.reshape(-1)
        pad = _ceil_to(p.size, unit) - p.size
        out.append(jnp.pad(p, (0, pad)) if pad else p)
        total += p.size + pad
    tail = _ceil_to(total, total_rows * LANES) - total
    if tail:
        out.append(jnp.zeros((tail,), out[0].dtype))
    return jnp.concatenate(out).reshape(-1, LANES)


def _unflatten(flat2d, shapes, seg_rows):
    flat = flat2d.reshape(-1)
    out, off = [], 0
    for shp in shapes:
        if isinstance(shp, list):
            seg, pos = [], off
            for s_ in shp:
                seg.append(flat[pos:pos + math.prod(s_)].reshape(s_))
                pos += math.prod(s_)
            out.append(seg)
            size = pos - off
        else:
            size = math.prod(shp)
            out.append(flat[off:off + size].reshape(shp))
        off += _ceil_to(size, seg_rows * LANES)
    return out


def _row(v):
    return v.reshape(1, -1)


def _pad_lanes(v):
    return jnp.pad(v.reshape(1, -1), ((0, 0), (0, LANES - v.size)))


def _local_step(a, w, comm=None):
    x = a['x']
    bsz, t, d = x.shape
    n = bsz * t
    tm = TOK_TILE
    pos = a['positions'].reshape(bsz, t, 1)
    inv_freq = 1.0 / (ROPE_THETA ** (jnp.arange(0, MLA_ROPE, 2, dtype=F32) / MLA_ROPE))
    inv_freq = jnp.concatenate([inv_freq, inv_freq]).reshape(1, MLA_ROPE)
    target = a['loss_target'].reshape(n, d)

    wi0 = w['l0_w_in']
    win0 = jnp.concatenate([wi0[:, :RKV_COLS], wi0[:, RWKV_COLS:L0_COLS - SSM_HEADS], wi0[:, RKV_COLS:RWKV_COLS],
                            wi0[:, L0_COLS - SSM_HEADS:], jnp.zeros((d, L0_PAD - L0_COLS), wi0.dtype)], axis=1)
    w2p = jnp.concatenate([w['rwkv_w2'], jnp.zeros_like(w['rwkv_w2'])], axis=0)
    a2p = jnp.concatenate([jnp.zeros_like(w['rwkv_a2']), w['rwkv_a2']], axis=0)
    mix = a['rwkv_mix']
    taps = jnp.stack([mix, 1.0 - mix])
    zero_b = jnp.zeros((1, mix.size), F32)
    rw_map = lambda j: j + jnp.where(j >= RKV_COLS // CONV_TILE, (P0_LORA - RKV_COLS) // CONV_TILE, 0)
    ssm_map = lambda j: j + (P0_Z + SSM_DIM) // CONV_TILE
    gate_map = lambda j: j
    up_map = lambda j: j + D_FF // CONV_TILE
    dt_col = P0_DT // LANES
    dtb, alog, dsk = _pad_lanes(a['ssm_dt_bias']), _pad_lanes(a['ssm_a_log']), _pad_lanes(a['ssm_d'])
    pre_p = [_row(a['rwkv_w0']), w2p, _row(a['rwkv_a0']), a2p, w['rwkv_g2'], _row(a['rwkv_k_k']), _row(a['rwkv_k_a'])]
    post_p = [_row(a['rwkv_ln_g']), _row(a['rwkv_ln_b']), _row(a['rwkv_r_k'])]
    sp_p = [dsk, _row(a['ssm_norm_g'])]

    def ln(name, h, y, layer, which):
        ps = [_row(a[f'l{layer}_ln{which}_g']), _row(a[f'l{layer}_ln{which}_b'])]
        return tok_fwd(name, f_ln, [(h, d, 0), (y, d, 0)], ps, [d], 2 * tm, n, t)[0]

    def ffn_fwd(layer, h):
        up = mm(h, w[f'ffn{layer}_w_up'], 'nn', f'ffn{layer}_up', out_dtype=BF16)
        act = dwconv_fwd(f'ffn{layer}_conv', up.reshape(bsz, t, 2 * D_FF), gate_map, w[f'ffn{layer}_conv_w'],
                         _row(a[f'ffn{layer}_conv_b']), 3, True, upmap=up_map, out_dtype=BF16)
        act = act.reshape(n, D_FF)
        return up, act, mm(act, w[f'ffn{layer}_w_down'], 'nn', f'ffn{layer}_down')

    x2 = x.reshape(n, d)
    proj0 = mm(x2, win0, 'nn', 'l0_in')
    p0 = proj0.reshape(bsz, t, L0_PAD)
    xs_r = dwconv_fwd('rwkv_shift', p0, rw_map, taps, zero_b, 2, False).reshape(n, RWKV_COLS)
    lora_blk = RKV_COLS // LANES
    pre_x = [(xs_r, RWKV_DIM, 0), (xs_r, RWKV_DIM, 1), (xs_r, RWKV_DIM, 2), (xs_r, LANES, lora_blk),
             (xs_r, LANES, lora_blk + 1)]
    heads64 = (RWKV_HEADS, HEAD_DIM)
    r_, v_, lw, kmod, al, be, gt = tok_fwd('rwkv_pre', f_rwkv_pre, pre_x, pre_p, [heads64] * 6 + [RWKV_DIM],
                                           tm, n, t)
    scan_in = [r_, lw, kmod, v_, al, be]
    if comm is None:
        y_h, rstates = rwkv_scan_fwd(*scan_in)
    else:
        y_h, rstates, *gots = rwkv_scan_fwd(*scan_in, ride=comm.weights_ride(MID_GATHER))
        w = {**w, **comm.weights(MID_GATHER, gots)}
    wi1 = w['l1_w_in']
    win1 = jnp.concatenate([wi1, jnp.zeros((d, L1_PAD - L1_COLS), wi1.dtype)], axis=1)
    wq3 = w['mla_w_uq'].reshape(-1, 8, MLA_NOPE + MLA_ROPE)
    wkv3 = w['mla_w_ukv'].reshape(-1, 8, 2 * MLA_NOPE)
    mla_p = [_row(a['mla_q_norm_g']), wq3[:, :, :MLA_NOPE].reshape(-1, 512), wq3[:, :, MLA_NOPE:].reshape(-1, 256),
             _row(a['mla_kv_norm_g']), wkv3[:, :, :MLA_NOPE].reshape(-1, 512), wkv3[:, :, MLA_NOPE:].reshape(-1, 512)]
    post_x = [y_h, r_, kmod, v_, (gt, 512, 0)]
    y_a = tok_fwd('rwkv_post', f_rwkv_post, post_x, post_p, [RWKV_DIM], tm, n, t)[0]
    xbc = dwconv_fwd('ssm_conv', p0, ssm_map, w['ssm_conv_w'], _row(a['ssm_conv_b']), 4, True)
    ys, sstates = ssd_fwd(xbc, p0, dt_col, dtb, alog)
    xbc2 = xbc.reshape(n, 2 * SSM_DIM)
    sp_x = [(ys.reshape(n, SSM_DIM), SSM_DIM, 0), (xbc2, SSM_DIM, 0), (proj0, SSM_DIM, P0_Z // SSM_DIM)]
    y_b = tok_fwd('ssd_post', f_ssd_post, sp_x, sp_p, [SSM_DIM], tm, n, t)[0]
    wo0 = w['l0_w_out']
    mixed0 = mm(y_b, wo0, 'nn', 'l0_out_b', b_rows=(512, 512), add=mm(y_a, wo0, 'nn', 'l0_out_a', b_rows=(0, 512)))
    h1 = ln('l0_ln1', x2, mixed0, 0, 1)
    up0, act0, f0 = ffn_fwd(0, h1)
    h2 = ln('l0_ln2', h1, f0, 0, 2)

    proj1 = mm(h2, win1, 'nn', 'l1_in')
    q_sb, k_sb, v_sb = tok_fwd('sb_split', f_same, [(proj1, 512, 0), (proj1, 512, 1), (proj1, 512, 2)], [],
                               [heads64] * 3, tm, n, t)
    if comm is None:
        o_c, sb_kept = sb_fwd(q_sb, k_sb, v_sb)
    else:
        o_c, sb_kept, *gots = sb_fwd(q_sb, k_sb, v_sb, ride=comm.weights_ride(FFN1_GATHER))
        w = {**w, **comm.weights(FFN1_GATHER, gots)}
    mla_x = [(proj1, P1_CKV - P1_CQ, P1_CQ // (P1_CKV - P1_CQ)), (proj1, LANES, P1_CKV // LANES)]
    qn, qp_raw, kn, vv = tok_fwd('mla_pre', f_mla_pre, mla_x, mla_p, [heads64, (8, MLA_ROPE), heads64, heads64],
                                 tm, n, t)
    kp_raw = proj1[:, P1_KPE:P1_KPE + MLA_ROPE].reshape(bsz, 1, t, MLA_ROPE)
    qp = rope('rope_q', qp_raw, pos, inv_freq, 1.0)
    kp = rope('rope_k', kp_raw, pos, inv_freq, 1.0)
    o_d, lse_d = mla_fwd(qn, qp, kn, kp, vv)
    y_cd = tok_fwd('attn_merge', f_concat, [o_c, o_d], [], [2 * RWKV_DIM], tm, n, t, out_dtype=BF16)[0]
    wo1 = w['l1_w_out']
    mixed1 = mm(y_cd, wo1, 'nn', 'l1_out')
    h3 = ln('l1_ln1', h2, mixed1, 1, 1)
    up1, act1, f1 = ffn_fwd(1, h3)
    h4 = ln('l1_ln2', h3, f1, 1, 2)
    dh4, loss = loss_head(h4, target, tm)

    g = {}

    def ln_bwd(name, h, y, layer, which, dout):
        ps = [_row(a[f'l{layer}_ln{which}_g']), _row(a[f'l{layer}_ln{which}_b'])]
        (dh, dy), (dg, db) = tok_bwd(name, f_ln, [(h, d, 0), (y, d, 0)], ps, [[dout]], 2 * tm, n, t)
        g[f'l{layer}_ln{which}_g'], g[f'l{layer}_ln{which}_b'] = dg.reshape(-1), db.reshape(-1)
        return dh, dy

    def ffn_bwd(layer, h, up, act, df, dh_res):
        wup, wdown = w[f'ffn{layer}_w_up'], w[f'ffn{layer}_w_down']
        g[f'ffn{layer}_w_down'] = mm(act, df, 'tn', f'ffn{layer}_dwdown')
        dact = mm(df, wdown, 'nt', f'ffn{layer}_dact').reshape(bsz, t, D_FF)
        dgate, dcw, dcb, dup = dwconv_bwd(f'ffn{layer}_conv_bwd', up.reshape(bsz, t, 2 * D_FF), gate_map,
                                          w[f'ffn{layer}_conv_w'], _row(a[f'ffn{layer}_conv_b']), 3, True, dact,
                                          upmap=up_map, grad_dtype=BF16)
        dgate, dup = dgate.reshape(n, D_FF), dup.reshape(n, D_FF)
        g[f'ffn{layer}_conv_w'], g[f'ffn{layer}_conv_b'] = dcw, dcb.reshape(-1)
        g[f'ffn{layer}_w_up'] = (mm(h, dgate, 'tn', f'ffn{layer}_dwgate'), mm(h, dup, 'tn', f'ffn{layer}_dwup'))
        dh = mm(dgate, wup, 'nt', f'ffn{layer}_dh_gate', add=dh_res, b_cols=(0, D_FF))
        return mm(dup, wup, 'nt', f'ffn{layer}_dh_up', add=dh, b_cols=(D_FF, D_FF))

    dh3_res, df1 = ln_bwd('l1_ln2_bwd', h3, f1, 1, 2, dh4)
    dh3 = ffn_bwd(1, h3, up1, act1, df1, dh3_res)
    dh2_res, dmixed1 = ln_bwd('l1_ln1_bwd', h2, mixed1, 1, 1, dh3)
    g['l1_w_out'] = mm(y_cd, dmixed1, 'tn', 'l1_dwout')
    dy_cd = mm(dmixed1, wo1, 'nt', 'l1_dy')
    dy_c, dy_d = tok_fwd('attn_split', f_same, [(dy_cd, 512, 0), (dy_cd, 512, 1)], [], [heads64] * 2, tm, n, t)
    parts = {}
    if comm is None:
        dq_sb, dk_sb, dv_sb = sb_bwd(q_sb, k_sb, v_sb, sb_kept, dy_c)
    else:
        dq_sb, dk_sb, dv_sb, *parts['a'] = sb_bwd(q_sb, k_sb, v_sb, sb_kept, dy_c, ride=comm.grad_ride('a', g))
    dqn, dqp, dkn, dkp, dvv = mla_bwd(qn, qp, kn, kp, vv, o_d, lse_d, dy_d)
    dqp_raw = rope('rope_q_bwd', dqp, pos, inv_freq, -1.0)
    dkp_raw = rope('rope_k_bwd', dkp, pos, inv_freq, -1.0).reshape(n, MLA_ROPE)
    (dcq, dckv), (dqg, dwq_n, dwq_p, dkvg, dwk, dwv) = tok_bwd('mla_pre_bwd', f_mla_pre, mla_x, mla_p,
                                                               [[dqn], [dqp_raw], [dkn], [dvv]], tm, n, t)
    g['mla_q_norm_g'], g['mla_kv_norm_g'] = dqg.reshape(-1), dkvg.reshape(-1)
    g['mla_w_uq'] = jnp.concatenate([dwq_n.reshape(-1, 8, MLA_NOPE), dwq_p.reshape(-1, 8, MLA_ROPE)],
                                    axis=2).reshape(-1, 8 * (MLA_NOPE + MLA_ROPE))
    g['mla_w_ukv'] = jnp.concatenate([dwk.reshape(-1, 8, MLA_NOPE), dwv.reshape(-1, 8, MLA_NOPE)],
                                     axis=2).reshape(-1, 16 * MLA_NOPE)
    dkp_pad = jnp.pad(dkp_raw, ((0, 0), (0, LANES - MLA_ROPE)))
    dproj1 = tok_fwd('l1_dproj', f_concat, [dq_sb, dk_sb, dv_sb, (dcq, 256, 0), (dckv, LANES, 0),
                                            (dkp_pad, LANES, 0)], [], [L1_PAD], tm, n, t, out_dtype=BF16)[0]
    g['l1_w_in'] = mm(h2, dproj1, 'tn', 'l1_dwin')[:, :L1_COLS]
    dh2 = mm(dproj1, win1, 'nt', 'l1_dh', add=dh2_res)

    dh1_res, df0 = ln_bwd('l0_ln2_bwd', h1, f0, 0, 2, dh2)
    dh1 = ffn_bwd(0, h1, up0, act0, df0, dh1_res)
    dx_res, dmixed0 = ln_bwd('l0_ln1_bwd', x2, mixed0, 0, 1, dh1)
    g['l0_w_out'] = (mm(y_a, dmixed0, 'tn', 'l0_dwout_a'), mm(y_b, dmixed0, 'tn', 'l0_dwout_b'))
    dy_a = mm(dmixed0, wo0, 'nt', 'l0_dy_a', b_rows=(0, 512))
    dy_b = mm(dmixed0, wo0, 'nt', 'l0_dy_b', b_rows=(512, 512))
    (dy_r, dr1, dkm1, dv1, dgt), (dlng, dlnb, drk) = tok_bwd('rwkv_post_bwd', f_rwkv_post, post_x, post_p, [[dy_a]],
                                                            tm, n, t, dx_layouts=[heads64] * 4 + [None])
    g['rwkv_ln_g'], g['rwkv_ln_b'] = dlng.reshape(-1), dlnb.reshape(-1)
    g['rwkv_r_k'] = drk.reshape(RWKV_HEADS, HEAD_DIM)
    (dys, dxs_skip, dz), (ddsk, dng) = tok_bwd('ssd_post_bwd', f_ssd_post, sp_x, sp_p, [[dy_b]], tm, n, t)
    g['ssm_d'], g['ssm_norm_g'] = ddsk[0, :SSM_HEADS], dng.reshape(-1)
    dxbc_act, ddtr, ddtb, dalog = ssd_bwd(xbc, p0, dt_col, dtb, alog, sstates, dys.reshape(bsz, t, SSM_DIM),
                                          dxs_skip.reshape(bsz, t, SSM_DIM))
    g['ssm_dt_bias'], g['ssm_a_log'] = ddtb[0, :SSM_HEADS], dalog[0, :SSM_HEADS]
    dxbc, dscw, dscb = dwconv_bwd('ssm_conv_bwd', p0, ssm_map, w['ssm_conv_w'], _row(a['ssm_conv_b']), 4, True,
                                  dxbc_act)
    g['ssm_conv_w'], g['ssm_conv_b'] = dscw, dscb.reshape(-1)
    if comm is None:
        dscan = rwkv_scan_bwd(*scan_in, rstates, dy_r)
    else:
        outs = rwkv_scan_bwd(*scan_in, rstates, dy_r, ride=comm.grad_ride('b', g))
        dscan, parts['b'] = outs[:6], outs[6:]
    dr2, dlw, dk2, dv2, dal, dbe = dscan
    pre_ct = [[dr1, dr2], [dv1, dv2], [dlw], [dkm1, dk2], [dal], [dbe], [dgt]]
    dpre_x, dpre_p = tok_bwd('rwkv_pre_bwd', f_rwkv_pre, pre_x, pre_p, pre_ct, tm, n, t)
    g['rwkv_w0'], g['rwkv_a0'] = dpre_p[0].reshape(-1), dpre_p[2].reshape(-1)
    g['rwkv_w2'], g['rwkv_a2'], g['rwkv_g2'] = dpre_p[1][:64], dpre_p[3][64:], dpre_p[4]
    g['rwkv_k_k'], g['rwkv_k_a'] = dpre_p[5].reshape(-1), dpre_p[6].reshape(-1)
    dxs_r = jnp.concatenate(dpre_x, axis=1).reshape(bsz, t, RWKV_COLS)
    d_rw, dtaps, _ = dwconv_bwd('rwkv_shift_bwd', p0, rw_map, taps, zero_b, 2, False, dxs_r)
    d_rw = d_rw.reshape(n, RWKV_COLS)
    g['rwkv_mix'] = dtaps[0] - dtaps[1]
    lora_cols = RWKV_COLS - RKV_COLS
    dproj0 = tok_fwd('l0_dproj', f_concat, [(d_rw, RKV_COLS, 0), (dz, SSM_DIM, 0),
                                            (dxbc.reshape(n, 2 * SSM_DIM), 2 * SSM_DIM, 0),
                                            (d_rw, lora_cols, RKV_COLS // lora_cols), (ddtr.reshape(n, LANES), LANES, 0)],
                     [], [L0_PAD], tm, n, t, out_dtype=BF16)[0]
    dwin0 = mm(x2, dproj0, 'tn', 'l0_dwin')
    g['l0_w_in'] = jnp.concatenate([dwin0[:, :RKV_COLS], dwin0[:, P0_LORA:P0_DT], dwin0[:, P0_Z:P0_LORA],
                                    dwin0[:, P0_DT:P0_DT + SSM_HEADS]], axis=1)
    if comm is None:
        dx = mm(dproj0, win0, 'nt', 'l0_dx', add=dx_res)
    else:
        dx, *parts['c'] = mm(dproj0, win0, 'nt', 'l0_dx', add=dx_res, ride=comm.grad_ride('c', g))
    return loss, dx.reshape(bsz, t, d), g, parts


GRAD_GROUPS = {
    'a': ['ffn1_w_up', 'ffn1_conv_w', 'ffn1_conv_b', 'ffn1_w_down', 'l1_ln2_g', 'l1_ln2_b'],
    'c': ['l0_w_in', 'rwkv_mix', 'rwkv_w0', 'rwkv_w2', 'rwkv_a0', 'rwkv_a2', 'rwkv_g2', 'rwkv_k_k', 'rwkv_k_a'],
}
GRAD_GROUPS['b'] = [nm for nm in WEIGHTS if nm not in GRAD_GROUPS['a'] + GRAD_GROUPS['c']]
FIRST_GATHER = ['l0_w_in', 'rwkv_w2', 'rwkv_a2', 'rwkv_g2'] + CONV_W
MID_GATHER = ['l0_w_out', 'ffn0_w_up', 'ffn0_w_down', 'l1_w_in', 'mla_w_uq', 'mla_w_ukv', 'l1_w_out']
FFN1_GATHER = ['ffn1_w_up', 'ffn1_w_down']
BF16_ROWS = 16


def _name_kinds(names):
    return ([nm for nm in names if nm in MATMUL_W], [nm for nm in names if nm in SHARD_AXIS and nm not in MATMUL_W],
            [nm for nm in names if nm not in SHARD_AXIS])


class _Comm:
    def __init__(self, a):
        self.a = a

    def weights_ride(self, names):
        big, conv, _ = _name_kinds(names)
        srcs = [self.a[nm].astype(BF16) for nm in big]
        if conv:
            srcs.append(_flat_rows([lax.bitcast_convert_type(self.a[nm], BF16) for nm in conv], BF16_ROWS, BF16_ROWS))
        return srcs, True

    def weights(self, names, gots):
        big, conv, _ = _name_kinds(names)
        out = {}
        for nm, got in zip(big, gots):
            out[nm] = (got.reshape(-1, got.shape[2]) if SHARD_AXIS[nm] == 0
                       else jnp.concatenate([got[k] for k in range(N_DEV)], axis=1))
        if conv:
            shapes = [self.a[nm].shape + (2,) for nm in conv]
            blocks = [_unflatten(gots[-1][k], shapes, BF16_ROWS) for k in range(N_DEV)]
            for i, nm in enumerate(conv):
                out[nm] = jnp.concatenate([lax.bitcast_convert_type(blocks[k][i], F32) for k in range(N_DEV)], axis=1)
        return out

    def first_weights(self):
        return self.weights(FIRST_GATHER, peer_exchange('gather_first_weights', self.weights_ride(FIRST_GATHER)[0], True))

    def grad_ride(self, group, g):
        def shard_of(nm, k):
            gv = g[nm]
            per = N_DEV
            if isinstance(gv, tuple):
                gv, k, per = gv[k // 4], k % 4, 4
            width = gv.shape[SHARD_AXIS[nm]] // per
            return lax.slice_in_dim(gv, k * width, (k + 1) * width, axis=SHARD_AXIS[nm])

        big, conv, small = _name_kinds(GRAD_GROUPS[group])
        srcs = [jnp.stack([shard_of(nm, k) for k in range(N_DEV)]) for nm in big]
        srcs.append(jnp.stack([_flat_rows([shard_of(nm, k) for nm in conv] + [[g[nm] for nm in small]],
                                          SUBLANES, ADAM_ROWS) for k in range(N_DEV)]))
        return srcs, False


def _step(a):
    comm = _Comm(a)
    loss, dx, _, parts = _local_step(a, comm.first_weights(), comm)
    loss = lax.psum(loss, ('x', 'y', 'c'))
    res = {}
    for group, names in GRAD_GROUPS.items():
        big, conv, small = _name_kinds(names)
        for nm, got in zip(big, parts[group]):
            res[nm] = adamw_sum(f'adamw_{nm}', got, a[nm], a['m_' + nm], a['v_' + nm])
        flat = lambda prefix: _flat_rows([a[prefix + nm] for nm in conv] + [[a[prefix + nm] for nm in small]],
                                         SUBLANES, ADAM_ROWS)
        outs = adamw_sum(f'adamw_{group}', parts[group][-1], flat(''), flat('m_'), flat('v_'))
        shapes = [a[nm].shape for nm in conv] + [[a[nm].shape for nm in small]]
        per_out = [_unflatten(o, shapes, SUBLANES) for o in outs]
        for i, nm in enumerate(conv):
            res[nm] = [per_out[j][i] for j in range(4)]
        for i, nm in enumerate(small):
            res[nm] = [per_out[j][-1][i] for j in range(4)]
    return (loss, dx, *[res[nm][j] for j in range(4) for nm in WEIGHTS])


def kernel(x, positions, l0_w_in, rwkv_mix, rwkv_w0, rwkv_w2, rwkv_a0, rwkv_a2, rwkv_g2, rwkv_k_k, rwkv_k_a, rwkv_r_k, rwkv_ln_g, rwkv_ln_b, ssm_conv_w, ssm_conv_b, ssm_dt_bias, ssm_a_log, ssm_d, ssm_norm_g, l0_w_out, l0_ln1_g, l0_ln1_b, ffn0_w_up, ffn0_conv_w, ffn0_conv_b, ffn0_w_down, l0_ln2_g, l0_ln2_b, l1_w_in, mla_q_norm_g, mla_w_uq, mla_kv_norm_g, mla_w_ukv, l1_w_out, l1_ln1_g, l1_ln1_b, ffn1_w_up, ffn1_conv_w, ffn1_conv_b, ffn1_w_down, l1_ln2_g, l1_ln2_b, loss_target, m_l0_w_in, m_rwkv_mix, m_rwkv_w0, m_rwkv_w2, m_rwkv_a0, m_rwkv_a2, m_rwkv_g2, m_rwkv_k_k, m_rwkv_k_a, m_rwkv_r_k, m_rwkv_ln_g, m_rwkv_ln_b, m_ssm_conv_w, m_ssm_conv_b, m_ssm_dt_bias, m_ssm_a_log, m_ssm_d, m_ssm_norm_g, m_l0_w_out, m_l0_ln1_g, m_l0_ln1_b, m_ffn0_w_up, m_ffn0_conv_w, m_ffn0_conv_b, m_ffn0_w_down, m_l0_ln2_g, m_l0_ln2_b, m_l1_w_in, m_mla_q_norm_g, m_mla_w_uq, m_mla_kv_norm_g, m_mla_w_ukv, m_l1_w_out, m_l1_ln1_g, m_l1_ln1_b, m_ffn1_w_up, m_ffn1_conv_w, m_ffn1_conv_b, m_ffn1_w_down, m_l1_ln2_g, m_l1_ln2_b, v_l0_w_in, v_rwkv_mix, v_rwkv_w0, v_rwkv_w2, v_rwkv_a0, v_rwkv_a2, v_rwkv_g2, v_rwkv_k_k, v_rwkv_k_a, v_rwkv_r_k, v_rwkv_ln_g, v_rwkv_ln_b, v_ssm_conv_w, v_ssm_conv_b, v_ssm_dt_bias, v_ssm_a_log, v_ssm_d, v_ssm_norm_g, v_l0_w_out, v_l0_ln1_g, v_l0_ln1_b, v_ffn0_w_up, v_ffn0_conv_w, v_ffn0_conv_b, v_ffn0_w_down, v_l0_ln2_g, v_l0_ln2_b, v_l1_w_in, v_mla_q_norm_g, v_mla_w_uq, v_mla_kv_norm_g, v_mla_w_ukv, v_l1_w_out, v_l1_ln1_g, v_l1_ln1_b, v_ffn1_w_up, v_ffn1_conv_w, v_ffn1_conv_b, v_ffn1_w_down, v_l1_ln2_g, v_l1_ln2_b):
    return _step(dict(locals()))
```

```python
import functools
import math

import jax
import jax.numpy as jnp
from jax import lax
from jax.experimental import pallas as pl
from jax.experimental.pallas import tpu as pltpu

F32 = jnp.float32
BF16 = jnp.bfloat16
HI = lax.Precision.HIGHEST

V7X_VMEM_BYTES = 64 * 1024 * 1024
VMEM_LIMIT = V7X_VMEM_BYTES - 8 * 1024 * 1024
LANES = 128
SUBLANES = 8
N_DEV = 8

D_MODEL = 1024
HEAD_DIM = 64
RWKV_DIM = 512
RWKV_HEADS = 8
RWKV_GN_EPS = 64e-5
RWKV_CHUNK = 64
SSM_DIM = 512
SSM_HEADS = 8
SSM_CHUNK = 128
SSM_STATE = 128
Q_BLOCK = 128
SB_HEADS_PER_STEP = 4
SB_BWD_ROW_SPLIT = 2
MLA_HEADS_PER_STEP = 4
ATTN_FWD_HEADS_PER_STEP = 8
MLA_NOPE = 64
MLA_ROPE = 32
ROPE_THETA = 10000.0
D_FF = 2816
DEPTH = 2
ALPHA = (2 * DEPTH) ** 0.25
RKV_COLS = 3 * RWKV_DIM
RWKV_COLS = RKV_COLS + 64 + 64 + 128
L0_COLS = RWKV_COLS + SSM_DIM + 2 * SSM_DIM + SSM_HEADS
P0_Z = RKV_COLS
P0_LORA = P0_Z + 3 * SSM_DIM
P0_DT = P0_LORA + (RWKV_COLS - RKV_COLS)
L0_PAD = P0_DT + LANES
L1_COLS = 3 * RWKV_DIM + 256 + 128 + MLA_ROPE
P1_CQ, P1_CKV, P1_KPE = 3 * RWKV_DIM, 3 * RWKV_DIM + 256, 3 * RWKV_DIM + 256 + 128
L1_PAD = P1_KPE + LANES

ADAM_LR = 0.001
ADAM_B1 = 0.9
ADAM_B2 = 0.999
ADAM_EPS = 1e-08
ADAM_WD = 0.01
ADAM_STEP = 10

NEG_BIG = -1e30


def _params(sem=None):
    return pltpu.CompilerParams(dimension_semantics=sem, vmem_limit_bytes=VMEM_LIMIT)


P_F32, P_BF16, P_BF16X3 = 0, 1, 2


def _dg_raw(a, b, ca, cb, fast):
    dims = (((ca,), (cb,)), ((), ()))
    if fast == P_BF16:
        return lax.dot_general(a.astype(BF16), b.astype(BF16), dims, preferred_element_type=F32)
    prec = HI if fast == P_F32 else lax.Precision.HIGH
    return lax.dot_general(a, b, dims, precision=prec, preferred_element_type=F32)


@functools.partial(jax.custom_vjp, nondiff_argnums=(2, 3, 4))
def dg(a, b, ca, cb, fast):
    return _dg_raw(a, b, ca, cb, fast)


def _dg_fwd(a, b, ca, cb, fast):
    return _dg_raw(a, b, ca, cb, fast), (a, b)


def _dg_bwd(ca, cb, fast, res, ct):
    a, b = res
    fa, fb = 1 - ca, 1 - cb
    da = _dg_raw(ct, b, 1, fb, fast) if ca == 1 else _dg_raw(b, ct, fb, 1, fast)
    db = _dg_raw(a, ct, fa, 0, fast) if cb == 0 else _dg_raw(ct, a, 0, fa, fast)
    return da.astype(a.dtype), db.astype(b.dtype)


dg.defvjp(_dg_fwd, _dg_bwd)


def mmb(a, b):
    return dg(a, b, 1, 0, P_BF16)


def mmf(a, b):
    return dg(a, b, 1, 0, P_F32)


def mm3(a, b):
    return dg(a, b, 1, 0, P_BF16X3)


def mm3_nt(a, b):
    return dg(a, b, 1, 1, P_BF16X3)


def mm3_tn(a, b):
    return dg(a, b, 0, 0, P_BF16X3)


def _split3_dot(x, m01, cb, terms=3):
    parts, rest = [], x
    for i in range(terms):
        parts.append(rest.astype(BF16))
        if i + 1 < terms:
            rest = rest - parts[-1].astype(F32)
    rows = x.shape[0]
    out = lax.dot_general(jnp.concatenate(parts, axis=0), m01.astype(BF16), (((1,), (cb,)), ((), ())),
                          preferred_element_type=F32)
    return functools.reduce(lambda a_, b_: a_ + b_, [out[i * rows:(i + 1) * rows] for i in range(terms)])


def _lower_ones(n):
    return jnp.where(_iota((n, n), 0) >= _iota((n, n), 1), 1.0, 0.0)


SUFFIX_TERMS = 2


@jax.custom_vjp
def suffix_sum(x):
    return _split3_dot(x, _lower_ones(x.shape[1]), 0, SUFFIX_TERMS)


def _suffix_sum_fwd(x):
    return suffix_sum(x), None


def _suffix_sum_bwd(_, ct):
    return (_split3_dot(ct, _lower_ones(ct.shape[1]), 1, SUFFIX_TERMS),)


suffix_sum.defvjp(_suffix_sum_fwd, _suffix_sum_bwd)


def _iota(shape, dim):
    return lax.broadcasted_iota(jnp.int32, shape, dim)


def _softplus(x):
    return jnp.maximum(x, 0.0) + jnp.log1p(jnp.exp(-jnp.abs(x)))


def _silu(x):
    return x * jax.nn.sigmoid(x)


def _largest_tile(n, cap, mult):
    best = None
    t = mult
    while t <= min(n, cap):
        if n % t == 0:
            best = t
        t += mult
    return n if best is None else best


MM_VMEM_BUDGET = 40 * 1024 * 1024
V7X_HBM_BYTES_PER_S = 3.2e12
V7X_VMEM_STORE_BYTES_PER_S = 7e12
GRID_STEP_S = 0.35e-6


def _mm_tiles(M, N, K, a_bytes, b_bytes, has_add):
    def divs(n):
        return [d for d in range(LANES, n + 1, LANES) if n % d == 0] or [n]

    best = None
    for tm in divs(M):
        for tn in divs(N):
            if tm * tn * 4 > 12 * 1024 * 1024:
                continue
            for tk in divs(K):
                vmem = (2 * (tm * tk * a_bytes + tk * tn * b_bytes) + 2 * tm * tn * 4 * (2 if has_add else 1)
                        + (tm * tk + tk * tn) * 2 + tm * tn * 4)
                if vmem > MM_VMEM_BUDGET:
                    continue
                ni, nj, nk = M // tm, N // tn, K // tk
                a_reads = M * K * a_bytes * (1 if nk == 1 else nj)
                b_reads = K * N * b_bytes * (1 if (nk == 1 and nj == 1) else ni)
                traffic = a_reads + b_reads + M * N * 4 * (2 if has_add else 1)
                cost = (traffic / V7X_HBM_BYTES_PER_S + ni * nj * nk * GRID_STEP_S
                        + M * N * 4 * nk / V7X_VMEM_STORE_BYTES_PER_S)
                if min(tm, tn, tk) < 256 and min(M, N, K) >= 256:
                    cost *= 1.5
                if best is None or cost < best[0]:
                    best = (cost, tm, tn, tk)
    return best[1:]


def mm(a, b, mode, name, add=None, b_rows=None, b_cols=None, ride=None):
    r0, nr = b_rows or (0, b.shape[0])
    c0, nc = b_cols or (0, b.shape[1])
    if mode == "nn":
        (M, K), N = a.shape, nc
        assert nr == K
    elif mode == "nt":
        (M, K), N = a.shape, nr
        assert nc == K
    else:
        (K, M), N = a.shape, b.shape[1]
        assert b_rows is None and b_cols is None
    has_add = add is not None
    tm, tn, tk = _mm_tiles(M, N, K, a.dtype.itemsize, b.dtype.itemsize, has_add)
    nk = K // tk
    keep_a = nk == 1 and N // tn > 1 and a.dtype != BF16
    if mode == "nn":
        assert r0 % tk == 0 and c0 % tn == 0
        a_spec = pl.BlockSpec((tm, tk), lambda i, j, k: (i, k))
        b_spec = pl.BlockSpec((tk, tn), lambda i, j, k: (k + r0 // tk, j + c0 // tn))
        dims = (((1,), (0,)), ((), ()))
    elif mode == "nt":
        assert r0 % tn == 0 and c0 % tk == 0
        a_spec = pl.BlockSpec((tm, tk), lambda i, j, k: (i, k))
        b_spec = pl.BlockSpec((tn, tk), lambda i, j, k: (j + r0 // tn, k + c0 // tk))
        dims = (((1,), (1,)), ((), ()))
    else:
        a_spec = pl.BlockSpec((tk, tm), lambda i, j, k: (k, i))
        b_spec = pl.BlockSpec((tk, tn), lambda i, j, k: (k, j))
        dims = (((0,), (0,)), ((), ()))
    o_spec = pl.BlockSpec((tm, tn), lambda i, j, k: (i, j))
    grid = (M // tm, N // tn, nk)
    r_in, r_specs, r_out, r_ospecs, r_scr = _ride_args(ride)
    n_add, n_ride = int(has_add), len(r_in)

    def body(a_ref, b_ref, *rest):
        o_ref = rest[n_add + n_ride]
        if ride is not None:
            first, last = _grid_first_last(grid)
            copies = _ride_start(rest[n_add:n_add + n_ride], rest[n_add + n_ride + 1:n_add + 2 * n_ride + 1],
                                 rest[-3:], ride[1], first)
        k = pl.program_id(2)
        if keep_a:
            a_bf = rest[n_add + 2 * n_ride + 1]

            @pl.when(pl.program_id(1) == 0)
            def _():
                a_bf[...] = a_ref[...].astype(BF16)

            av = a_bf[...]
        else:
            av = a_ref[...].astype(BF16)
        part = lax.dot_general(av, b_ref[...].astype(BF16), dims, preferred_element_type=F32)

        @pl.when(k == 0)
        def _():
            o_ref[...] = part + rest[0][...] if has_add else part

        @pl.when(k > 0)
        def _():
            o_ref[...] += part

        if ride is not None:
            _ride_wait(copies, last)

    ins = [a, b] + ([add] if has_add else []) + r_in
    specs = [a_spec, b_spec] + ([o_spec] if has_add else []) + r_specs
    outs = pl.pallas_call(
        body, name=name, grid=grid, in_specs=specs, out_specs=[o_spec] + r_ospecs,
        out_shape=[jax.ShapeDtypeStruct((M, N), F32)] + r_out,
        scratch_shapes=([pltpu.VMEM(a_spec.block_shape, BF16)] if keep_a else []) + r_scr,
        compiler_params=_params(("arbitrary" if ride is not None else "parallel", "arbitrary", "arbitrary")),
    )(*ins)
    return outs[0] if ride is None else outs


def _is_heads(x):
    return not isinstance(x, tuple)


def _tok_arr(x):
    return x if _is_heads(x) else x[0]


def _tok_width(x):
    return x.shape[1] * x.shape[3] if _is_heads(x) else x[1]


def _heads_spec(h, dh, tm, tiles_per_seq):
    return pl.BlockSpec((None, h, tm, dh), lambda i: (i // tiles_per_seq, 0, i % tiles_per_seq, 0))


def _x_spec(x, tm, tiles_per_seq):
    if _is_heads(x):
        return _heads_spec(x.shape[1], x.shape[3], tm, tiles_per_seq)
    return pl.BlockSpec((tm, x[1]), functools.partial(lambda i, cb: (i, cb), cb=x[2]))


def _out_spec_shape(layout, n, seq, tm, dtype=F32):
    if isinstance(layout, tuple):
        h, dh = layout
        return _heads_spec(h, dh, tm, seq // tm), jax.ShapeDtypeStruct((n // seq, h, seq, dh), dtype)
    return pl.BlockSpec((tm, layout), lambda i: (i, 0)), jax.ShapeDtypeStruct((n, layout), dtype)


def _tok_load(ref):
    if len(ref.shape) == 3:
        return jnp.concatenate([ref[hh] for hh in range(ref.shape[0])], axis=1)
    return ref[...]


def _tok_store(ref, val):
    if len(ref.shape) == 3:
        dh = ref.shape[2]
        for hh in range(ref.shape[0]):
            ref[hh] = val[:, hh * dh:(hh + 1) * dh]
    else:
        ref[...] = val


def _p_specs(ps):
    return [pl.BlockSpec(p.shape, lambda i: (0, 0)) for p in ps]


def tok_fwd(name, f, xs, ps, out_layouts, tm, n, seq, out_dtype=F32):
    nx, npar = len(xs), len(ps)
    outs = [_out_spec_shape(lay, n, seq, tm, out_dtype) for lay in out_layouts]

    def body(*refs):
        xv = [_tok_load(r) for r in refs[:nx]]
        pv = [r[...].astype(F32) for r in refs[nx:nx + npar]]
        for o, r in zip(f(*xv, *pv), refs[nx + npar:]):
            _tok_store(r, o.astype(out_dtype))

    return pl.pallas_call(
        body, name=name, grid=(n // tm,),
        in_specs=[_x_spec(x, tm, seq // tm) for x in xs] + _p_specs(ps),
        out_specs=[o[0] for o in outs], out_shape=[o[1] for o in outs],
        compiler_params=_params(("parallel",)),
    )(*[_tok_arr(x) for x in xs], *ps)


def tok_bwd(name, f, xs, ps, cts, tm, n, seq, dx_layouts=None):
    nx, npar = len(xs), len(ps)
    ct_flat = [c for group in cts for c in group]
    nct = len(ct_flat)
    dx_layouts = dx_layouts or [None] * nx
    dxs = [_out_spec_shape(lay if lay else _tok_width(x), n, seq, tm) for x, lay in zip(xs, dx_layouts)]

    def body(*refs):
        xv = [_tok_load(r) for r in refs[:nx]]
        pv = [r[...].astype(F32) for r in refs[nx:nx + npar]]
        ct_refs = refs[nx + npar:nx + npar + nct]
        dx_refs = refs[nx + npar + nct:nx + npar + nct + nx]
        dp_refs = refs[nx + npar + nct + nx:]
        cv, pos = [], 0
        for group in cts:
            acc = _tok_load(ct_refs[pos])
            for r in ct_refs[pos + 1:pos + len(group)]:
                acc = acc + _tok_load(r)
            cv.append(acc)
            pos += len(group)
        _, vjp = jax.vjp(f, *xv, *pv)
        grads = vjp(tuple(cv))
        for g, r in zip(grads[:nx], dx_refs):
            _tok_store(r, g)

        @pl.when(pl.program_id(0) == 0)
        def _():
            for r in dp_refs:
                r[...] = jnp.zeros_like(r)

        for g, r in zip(grads[nx:], dp_refs):
            r[...] += g

    ct_specs = [_heads_spec(c.shape[1], c.shape[3], tm, seq // tm) if c.ndim == 4
                else pl.BlockSpec((tm, c.shape[1]), lambda i: (i, 0)) for c in ct_flat]
    outs = pl.pallas_call(
        body, name=name, grid=(n // tm,),
        in_specs=[_x_spec(x, tm, seq // tm) for x in xs] + _p_specs(ps) + ct_specs,
        out_specs=[d[0] for d in dxs] + _p_specs(ps),
        out_shape=[d[1] for d in dxs] + [jax.ShapeDtypeStruct(p.shape, F32) for p in ps],
        compiler_params=_params(("arbitrary",)),
    )(*[_tok_arr(x) for x in xs], *ps, *ct_flat)
    return outs[:nx], outs[nx:]


def f_ln(h, y, g, b):
    pre = ALPHA * h + y
    mu = jnp.mean(pre, axis=-1, keepdims=True)
    xc = pre - mu
    var = jnp.mean(xc * xc, axis=-1, keepdims=True)
    return (xc * lax.rsqrt(var + 1e-5) * g + b,)


def _head_sel(width, nheads_pad, per):
    return jnp.where(_iota((width, nheads_pad), 0) // per == _iota((width, nheads_pad), 1), 1.0, 0.0).astype(F32)


def _head_sel_t(nheads_pad, width, per):
    return jnp.where(_iota((nheads_pad, width), 1) // per == _iota((nheads_pad, width), 0), 1.0, 0.0).astype(F32)


@jax.custom_vjp
def head_sum(x):
    return _split3_dot(x, _head_sel(RWKV_DIM, LANES, HEAD_DIM), 0)


@jax.custom_vjp
def head_spread(y):
    return _split3_dot(y, _head_sel(RWKV_DIM, LANES, HEAD_DIM), 1)


head_sum.defvjp(lambda x: (head_sum(x), None), lambda _, ct: (head_spread(ct),))
head_spread.defvjp(lambda y: (head_spread(y), None), lambda _, ct: (head_sum(ct),))


def f_rwkv_pre(r, k, v, lora, glo, w0, w2p, a0, a2p, g2, k_k, k_a):
    lane = _iota(lora.shape, 1)
    tw = jnp.where(lane < 64, jnp.tanh(lora), 0.0)
    ta = jnp.where(lane >= 64, lora, 0.0)
    log_w = -_softplus(-(w0 + mmb(tw, w2p))) - 0.5
    lw = -jnp.exp(log_w)
    a = jax.nn.sigmoid(a0 + mmb(ta, a2p))
    g = mmb(jax.nn.sigmoid(glo), g2)
    kk = k * k_k
    nrm = jnp.sqrt(jnp.maximum(head_sum(kk * kk), 1e-24))
    kkn = kk * head_spread(1.0 / nrm)
    kmod = k * (1.0 + (a - 1.0) * k_a)
    return r, v, lw, kmod, -kkn, kkn * a, g


def f_rwkv_post(y, r, kmod, v, g, ln_g, ln_b, r_k):
    inv = 1.0 / HEAD_DIM
    mu = head_spread(head_sum(y) * inv)
    yc = y - mu
    var = head_sum(yc * yc) * inv
    rstd = head_spread(lax.rsqrt(var + RWKV_GN_EPS))
    yn = yc * rstd * ln_g + ln_b
    bonus = head_spread(head_sum(r * kmod * r_k)) * v
    return ((yn + bonus) * g,)


def f_ssd_post(y, xs, z, d_skip, norm_g):
    sel_t = _head_sel_t(LANES, SSM_DIM, HEAD_DIM)
    d_e = jnp.sum(mmf(jnp.broadcast_to(d_skip, (SUBLANES, LANES)), sel_t), axis=0, keepdims=True) * (1.0 / SUBLANES)
    u = (y + xs * d_e) * _silu(z)
    first = _iota(u.shape, 1) < (SSM_DIM // 2)
    uu = u * u
    inv = 2.0 / SSM_DIM
    ms0 = jnp.sum(jnp.where(first, uu, 0.0), axis=-1, keepdims=True) * inv
    ms1 = jnp.sum(jnp.where(first, 0.0, uu), axis=-1, keepdims=True) * inv
    ms = jnp.where(first, ms0, ms1)
    return (u * lax.rsqrt(ms + 1e-5) * norm_g,)


def f_mla_pre(cq, ckv, qg, wq_nope, wq_rope, kvg, wk_nope, wv):
    def rms(x, g):
        return x * lax.rsqrt(jnp.mean(x * x, axis=-1, keepdims=True) + 1e-6) * g
    q_in, kv_in = rms(cq, qg), rms(ckv, kvg)
    return mmb(q_in, wq_nope), mmb(q_in, wq_rope), mmb(kv_in, wk_nope), mmb(kv_in, wv)


def f_same(*xs):
    return xs


def f_concat(*xs):
    return (jnp.concatenate(xs, axis=1),)


CONV_TILE = 256


def _shift_down(x, s, row):
    return x if s == 0 else jnp.where(row >= s, pltpu.roll(x, s, 0), 0.0)


def _shift_up(x, s, row, t):
    return x if s == 0 else jnp.where(row < t - s, pltpu.roll(x, t - s, 0), 0.0)


def dwconv_fwd(name, u, colmap, w, b, taps, silu, upmap=None, out_dtype=F32):
    bsz, t, _ = u.shape
    c = w.shape[1]
    tc = CONV_TILE
    has_up = upmap is not None

    def body(*refs):
        u_ref, w_ref, b_ref = refs[:3]
        o_ref = refs[-1]
        uv = u_ref[...]
        wv = w_ref[...]
        row = _iota(uv.shape, 0)
        acc = jnp.broadcast_to(b_ref[...], uv.shape)
        for i in range(taps):
            acc = acc + wv[i:i + 1, :] * _shift_down(uv, taps - 1 - i, row)
        if silu:
            acc = _silu(acc)
        if has_up:
            acc = acc * refs[3][...]
        o_ref[...] = acc.astype(out_dtype)

    specs = [pl.BlockSpec((None, t, tc), lambda bb, j: (bb, 0, colmap(j))),
             pl.BlockSpec((taps, tc), lambda bb, j: (0, j)),
             pl.BlockSpec((1, tc), lambda bb, j: (0, j))]
    ins = [u, w, b]
    if has_up:
        specs.append(pl.BlockSpec((None, t, tc), lambda bb, j: (bb, 0, upmap(j))))
        ins.append(u)
    return pl.pallas_call(
        body, name=name, grid=(bsz, c // tc), in_specs=specs,
        out_specs=pl.BlockSpec((None, t, tc), lambda bb, j: (bb, 0, j)),
        out_shape=jax.ShapeDtypeStruct((bsz, t, c), out_dtype),
        compiler_params=_params(("parallel", "parallel")),
    )(*ins)


def dwconv_bwd(name, u, colmap, w, b, taps, silu, dout, upmap=None, grad_dtype=F32):
    bsz, t, _ = u.shape
    c = w.shape[1]
    tc = CONV_TILE
    has_up = upmap is not None

    def body(*refs):
        u_ref, w_ref, b_ref, d_ref = refs[:4]
        nin = 5 if has_up else 4
        du_ref, dw_ref, db_ref = refs[nin:nin + 3]
        uv = u_ref[...]
        wv = w_ref[...]
        dv = d_ref[...]
        row = _iota(uv.shape, 0)
        shifted = [_shift_down(uv, taps - 1 - i, row) for i in range(taps)]
        cg = jnp.broadcast_to(b_ref[...], uv.shape)
        for i in range(taps):
            cg = cg + wv[i:i + 1, :] * shifted[i]
        if silu:
            sg = jax.nn.sigmoid(cg)
            act = cg * sg
            dact_dcg = sg * (1.0 + cg * (1.0 - sg))
        else:
            act = cg
            dact_dcg = None
        if has_up:
            refs[nin + 3][...] = (dv * act).astype(grad_dtype)
            dv = dv * refs[4][...]
        dcg = dv * dact_dcg if silu else dv
        du = jnp.zeros_like(uv)
        for i in range(taps):
            du = du + wv[i:i + 1, :] * _shift_up(dcg, taps - 1 - i, row, t)
        du_ref[...] = du.astype(grad_dtype)

        @pl.when(pl.program_id(1) == 0)
        def _():
            dw_ref[...] = jnp.zeros_like(dw_ref)
            db_ref[...] = jnp.zeros_like(db_ref)

        for i in range(taps):
            dw_ref[i:i + 1, :] += jnp.sum(dcg * shifted[i], axis=0, keepdims=True)
        db_ref[...] += jnp.sum(dcg, axis=0, keepdims=True)

    specs = [pl.BlockSpec((None, t, tc), lambda j, bb: (bb, 0, colmap(j))),
             pl.BlockSpec((taps, tc), lambda j, bb: (0, j)),
             pl.BlockSpec((1, tc), lambda j, bb: (0, j)),
             pl.BlockSpec((None, t, tc), lambda j, bb: (bb, 0, j))]
    ins = [u, w, b, dout]
    if has_up:
        specs.append(pl.BlockSpec((None, t, tc), lambda j, bb: (bb, 0, upmap(j))))
        ins.append(u)
    big = pl.BlockSpec((None, t, tc), lambda j, bb: (bb, 0, j))
    out_specs = [big, pl.BlockSpec((taps, tc), lambda j, bb: (0, j)), pl.BlockSpec((1, tc), lambda j, bb: (0, j))]
    out_shape = [jax.ShapeDtypeStruct((bsz, t, c), grad_dtype), jax.ShapeDtypeStruct((taps, c), F32),
                 jax.ShapeDtypeStruct((1, c), F32)]
    if has_up:
        out_specs.append(big)
        out_shape.append(jax.ShapeDtypeStruct((bsz, t, c), grad_dtype))
    return pl.pallas_call(
        body, name=name, grid=(c // tc, bsz), in_specs=specs, out_specs=out_specs, out_shape=out_shape,
        compiler_params=_params(("parallel", "arbitrary")),
    )(*ins)


def _each(f, *lists):
    return [f(*xs) for xs in zip(*lists)]


def rwkv_chunk(s0, r, lw, k, v, al, be):
    c = r[0].shape[0]
    ii, jj = _iota((c, c), 0), _iota((c, c), 1)
    incl, strict = ii >= jj, ii > jj
    ones_incl = jnp.where(incl, 1.0, 0.0)
    eye = jnp.where(ii == jj, 1.0, 0.0)
    cum = _each(lambda x: mmf(ones_incl, x), lw)
    gam_inv = _each(lambda x: jnp.exp(-x), cum)
    at = _each(lambda a_, c_, l_: a_ * jnp.exp(c_ - l_), al, cum, lw)
    rt = _each(lambda r_, c_: r_ * jnp.exp(c_), r, cum)
    bt = _each(lambda b_, g_: b_ * g_, be, gam_inv)
    kt = _each(lambda k_, g_: k_ * g_, k, gam_inv)
    a_b = _each(lambda x, y_: jnp.where(strict, mm3_nt(x, y_), 0.0), at, bt)
    a_k = _each(lambda x, y_: jnp.where(strict, mm3_nt(x, y_), 0.0), at, kt)
    rhs0 = _each(mm3_nt, at, s0)
    rhs = _each(lambda x, a_, v_: x + mm3(a_, v_), rhs0, a_k, v)
    p = _each(lambda x: eye + x, a_b)
    m = a_b
    for _ in range(int(math.log2(c)) - 1):
        m = _each(mm3, m, m)
        p = _each(lambda p_, m_: p_ + mm3(p_, m_), p, m)
    u = _each(mm3, p, rhs)
    r_b = _each(lambda x, y_: jnp.where(incl, mm3_nt(x, y_), 0.0), rt, bt)
    r_k = _each(lambda x, y_: jnp.where(incl, mm3_nt(x, y_), 0.0), rt, kt)
    y0 = _each(mm3_nt, rt, s0)
    y1 = _each(lambda y_, b_, u_: y_ + mm3(b_, u_), y0, r_b, u)
    y = _each(lambda y_, k_, v_: y_ + mm3(k_, v_), y1, r_k, v)
    su = _each(mm3_tn, u, bt)
    sv = _each(mm3_tn, v, kt)
    s1 = _each(lambda s_, a_, b_, l_: (s_ + a_ + b_) * jnp.exp(jnp.sum(l_, axis=0, keepdims=True)), s0, su, sv, lw)
    return y, s1


def rwkv_scan_fwd(r, lw, k, v, al, be, ride=None):
    bsz, h, t, d = r.shape
    c = RWKV_CHUNK
    nc = t // c
    grid = (bsz, nc)
    r_in, r_specs, r_out, r_ospecs, r_scr = _ride_args(ride)

    def body(*refs):
        r_ref, lw_ref, k_ref, v_ref, al_ref, be_ref = refs[:6]
        y_ref, st_ref = refs[6 + len(r_in):8 + len(r_in)]
        s_scr = refs[8 + 2 * len(r_in)]
        if ride is not None:
            first, last = _grid_first_last(grid)
            copies = _ride_start(refs[6:6 + len(r_in)], refs[8 + len(r_in):8 + 2 * len(r_in)], refs[-3:], ride[1], first)

        @pl.when(pl.program_id(1) == 0)
        def _():
            s_scr[...] = jnp.zeros_like(s_scr)

        heads = lambda ref: [ref[hh] for hh in range(h)]
        s0 = heads(s_scr)
        y, s1 = rwkv_chunk(s0, heads(r_ref), heads(lw_ref), heads(k_ref), heads(v_ref), heads(al_ref),
                           heads(be_ref))
        for hh in range(h):
            st_ref[hh] = s0[hh]
            y_ref[hh] = y[hh]
            s_scr[hh] = s1[hh]
        if ride is not None:
            _ride_wait(copies, last)

    seq = pl.BlockSpec((None, h, c, d), lambda b, i: (b, 0, i, 0))
    return pl.pallas_call(
        body, name="rwkv_scan_fwd", grid=grid, in_specs=[seq] * 6 + r_specs,
        out_specs=[seq, pl.BlockSpec((None, h, None, d, d), lambda b, i: (b, 0, i, 0, 0))] + r_ospecs,
        out_shape=[jax.ShapeDtypeStruct((bsz, h, t, d), F32), jax.ShapeDtypeStruct((bsz, h, nc, d, d), F32)] + r_out,
        scratch_shapes=[pltpu.VMEM((h, d, d), F32)] + r_scr,
        compiler_params=_params(("arbitrary", "arbitrary")),
    )(r, lw, k, v, al, be, *r_in)


def rwkv_scan_bwd(r, lw, k, v, al, be, states, dy, ride=None):
    bsz, h, t, d = r.shape
    c = RWKV_CHUNK
    nc = t // c
    grid = (bsz, nc)
    r_in, r_specs, r_out, r_ospecs, r_scr = _ride_args(ride)

    def body(*refs):
        r_ref, lw_ref, k_ref, v_ref, al_ref, be_ref, st_ref, dy_ref = refs[:8]
        nin = 8 + len(r_in)
        dr_ref, dlw_ref, dk_ref, dv_ref, dal_ref, dbe_ref = refs[nin:nin + 6]
        ds_scr = refs[nin + 6 + len(r_in)]
        if ride is not None:
            first, last = _grid_first_last(grid)
            copies = _ride_start(refs[8:nin], refs[nin + 6:nin + 6 + len(r_in)], refs[-3:], ride[1], first)

        @pl.when(pl.program_id(1) == 0)
        def _():
            ds_scr[...] = jnp.zeros_like(ds_scr)

        heads = lambda ref: [ref[hh] for hh in range(h)]
        _, vjp = jax.vjp(rwkv_chunk, heads(st_ref), heads(r_ref), heads(lw_ref), heads(k_ref), heads(v_ref),
                         heads(al_ref), heads(be_ref))
        grads = vjp((heads(dy_ref), heads(ds_scr)))
        for ref, gl in zip((ds_scr, dr_ref, dlw_ref, dk_ref, dv_ref, dal_ref, dbe_ref), grads):
            for hh in range(h):
                ref[hh] = gl[hh]
        if ride is not None:
            _ride_wait(copies, last)

    seq = pl.BlockSpec((None, h, c, d), lambda b, i: (b, 0, nc - 1 - i, 0))
    st = pl.BlockSpec((None, h, None, d, d), lambda b, i: (b, 0, nc - 1 - i, 0, 0))
    return pl.pallas_call(
        body, name="rwkv_scan_bwd", grid=grid, in_specs=[seq] * 6 + [st, seq] + r_specs,
        out_specs=[seq] * 6 + r_ospecs, out_shape=[jax.ShapeDtypeStruct((bsz, h, t, d), F32)] * 6 + r_out,
        scratch_shapes=[pltpu.VMEM((h, d, d), F32)] + r_scr,
        compiler_params=_params(("arbitrary", "arbitrary")),
    )(r, lw, k, v, al, be, states, dy, *r_in)


def ssd_chunk(st, xs, bm, cm, dtr, dt_bias, a_log):
    n = SSM_CHUNK
    ii, jj = _iota((n, n), 0), _iota((n, n), 1)
    incl = ii >= jj
    lane = _iota((n, LANES), 1)
    dt = _softplus(dtr + dt_bias)
    a = dt * (-jnp.exp(a_log))
    acum = mmf(jnp.where(incl, 1.0, 0.0), a)
    last_row = jnp.where(jj == n - 1, 1.0, 0.0)
    cb = [mm3_nt(cm[g], bm[g]) for g in range(2)]
    pairs, heads = range(4), range(SSM_HEADS)
    e_m = [jnp.where(_iota((LANES, LANES), 0) == 2 * m + _iota((LANES, LANES), 1) // HEAD_DIM, 1.0, 0.0)
           for m in pairs]
    dt_m = [mmf(dt, e_m[m]) for m in pairs]
    ac_m = [mmf(acum, e_m[m]) for m in pairs]
    x = [xs[m] * dt_m[m] for m in pairs]
    last_m = [mmf(last_row, ac_m[m]) for m in pairs]
    colb = [mmf(acum, jnp.where(_iota((LANES, n), 0) == h, 1.0, 0.0)) for h in heads]
    decay = [jnp.exp(jnp.where(incl, colb[h] - colb[h].T, NEG_BIG)) for h in heads]
    yh = [mm3(cb[h // 4] * decay[h], x[h // 2]) for h in heads]
    y_off = [mm3(cm[m // 2], st[m]) for m in pairs]
    ys = [jnp.where(lane // HEAD_DIM == 0, yh[2 * m], yh[2 * m + 1]) + jnp.exp(ac_m[m]) * y_off[m] for m in pairs]
    st_in = [mm3_tn(bm[m // 2], x[m] * jnp.exp(last_m[m] - ac_m[m])) for m in pairs]
    st_new = [jnp.exp(last_m[m]) * st[m] + st_in[m] for m in pairs]
    return tuple(ys), tuple(st_new)


def _ssd_load(xbc_ref, dtr_ref):
    xs = tuple(xbc_ref[:, m * LANES:(m + 1) * LANES] for m in range(4))
    bm = tuple(xbc_ref[:, SSM_DIM + g * LANES:SSM_DIM + (g + 1) * LANES] for g in range(2))
    cm = tuple(xbc_ref[:, SSM_DIM + 2 * LANES + g * LANES:SSM_DIM + 2 * LANES + (g + 1) * LANES] for g in range(2))
    return xs, bm, cm, dtr_ref[...]


def ssd_fwd(xbc, proj, dt_col, dt_bias, a_log):
    bsz, t, _ = xbc.shape
    n = SSM_CHUNK
    nc = t // n

    def body(xbc_ref, dtr_ref, dtb_ref, al_ref, y_ref, st_ref, s_scr):
        @pl.when(pl.program_id(1) == 0)
        def _():
            s_scr[...] = jnp.zeros_like(s_scr)

        st = tuple(s_scr[m] for m in range(4))
        for m in range(4):
            st_ref[m] = st[m]
        xs, bm, cm, dtr = _ssd_load(xbc_ref, dtr_ref)
        ys, st_new = ssd_chunk(st, xs, bm, cm, dtr, dtb_ref[...], al_ref[...])
        for m in range(4):
            y_ref[:, m * LANES:(m + 1) * LANES] = ys[m]
            s_scr[m] = st_new[m]

    vec = pl.BlockSpec((1, LANES), lambda b, i: (0, 0))
    return pl.pallas_call(
        body, name="ssd_fwd", grid=(bsz, nc),
        in_specs=[pl.BlockSpec((None, n, 2 * SSM_DIM), lambda b, i: (b, i, 0)),
                  pl.BlockSpec((None, n, LANES), lambda b, i: (b, i, dt_col)), vec, vec],
        out_specs=[pl.BlockSpec((None, n, SSM_DIM), lambda b, i: (b, i, 0)),
                   pl.BlockSpec((None, None, 4, SSM_STATE, LANES), lambda b, i: (b, i, 0, 0, 0))],
        out_shape=[jax.ShapeDtypeStruct((bsz, t, SSM_DIM), F32),
                   jax.ShapeDtypeStruct((bsz, nc, 4, SSM_STATE, LANES), F32)],
        scratch_shapes=[pltpu.VMEM((4, SSM_STATE, LANES), F32)],
        compiler_params=_params(("parallel", "arbitrary")),
    )(xbc, proj, dt_bias, a_log)


def ssd_bwd(xbc, proj, dt_col, dt_bias, a_log, states, dy, dxs_extra):
    bsz, t, _ = xbc.shape
    n = SSM_CHUNK
    nc = t // n

    def body(xbc_ref, dtr_ref, dtb_ref, al_ref, st_ref, dy_ref, ex_ref,
             dxbc_ref, ddtr_ref, ddtb_ref, dal_ref, ds_scr):
        first = jnp.logical_and(pl.program_id(0) == 0, pl.program_id(1) == 0)

        @pl.when(pl.program_id(1) == 0)
        def _():
            ds_scr[...] = jnp.zeros_like(ds_scr)

        @pl.when(first)
        def _():
            ddtb_ref[...] = jnp.zeros_like(ddtb_ref)
            dal_ref[...] = jnp.zeros_like(dal_ref)

        st = tuple(st_ref[m] for m in range(4))
        xs, bm, cm, dtr = _ssd_load(xbc_ref, dtr_ref)
        _, vjp = jax.vjp(ssd_chunk, st, xs, bm, cm, dtr, dtb_ref[...], al_ref[...])
        dys = tuple(dy_ref[:, m * LANES:(m + 1) * LANES] for m in range(4))
        dst_in = tuple(ds_scr[m] for m in range(4))
        dst, dxs, dbm, dcm, ddtr, ddtb, dal = vjp((dys, dst_in))
        for m in range(4):
            ds_scr[m] = dst[m]
            sl = slice(m * LANES, (m + 1) * LANES)
            dxbc_ref[:, sl] = dxs[m] + ex_ref[:, sl]
        for g in range(2):
            dxbc_ref[:, SSM_DIM + g * LANES:SSM_DIM + (g + 1) * LANES] = dbm[g]
            dxbc_ref[:, SSM_DIM + 2 * LANES + g * LANES:SSM_DIM + 2 * LANES + (g + 1) * LANES] = dcm[g]
        ddtr_ref[...] = ddtr
        ddtb_ref[...] += ddtb
        dal_ref[...] += dal

    vec = pl.BlockSpec((1, LANES), lambda b, i: (0, 0))
    rev = lambda b, i: (b, nc - 1 - i, 0)
    return pl.pallas_call(
        body, name="ssd_bwd", grid=(bsz, nc),
        in_specs=[pl.BlockSpec((None, n, 2 * SSM_DIM), rev),
                  pl.BlockSpec((None, n, LANES), lambda b, i: (b, nc - 1 - i, dt_col)), vec, vec,
                  pl.BlockSpec((None, None, 4, SSM_STATE, LANES), lambda b, i: (b, nc - 1 - i, 0, 0, 0)),
                  pl.BlockSpec((None, n, SSM_DIM), rev), pl.BlockSpec((None, n, SSM_DIM), rev)],
        out_specs=[pl.BlockSpec((None, n, 2 * SSM_DIM), rev), pl.BlockSpec((None, n, LANES), rev), vec, vec],
        out_shape=[jax.ShapeDtypeStruct((bsz, t, 2 * SSM_DIM), F32), jax.ShapeDtypeStruct((bsz, t, LANES), F32),
                   jax.ShapeDtypeStruct((1, LANES), F32), jax.ShapeDtypeStruct((1, LANES), F32)],
        scratch_shapes=[pltpu.VMEM((4, SSM_STATE, LANES), F32)],
        compiler_params=_params(("arbitrary", "arbitrary")),
    )(xbc, proj, dt_bias, a_log, states, dy, dxs_extra)


def sb_block(q, kj, vj, carry, maskf):
    masks = [m_ > 0.5 for m_ in maskf] if isinstance(maskf, list) else [maskf > 0.5] * len(q)
    z = _each(lambda q_, k_: dg(q_, k_, 1, 1, P_BF16) * (HEAD_DIM ** -0.5), q, kj)
    lk = _each(lambda z_, m_: jnp.where(m_, -_softplus(z_), 0.0), z, masks)
    sfx = _each(suffix_sum, lk)
    att = _each(lambda z_, c_, s_, m_: jnp.exp(jnp.where(m_, z_ + c_ + s_, NEG_BIG)), z, carry, sfx, masks)
    out = _each(mmb, att, vj)
    return out, _each(lambda c_, k_: c_ + jnp.sum(k_, axis=1, keepdims=True), carry, lk)


def _sb_mask(qi, j):
    n = Q_BLOCK
    return jnp.where(j * n + _iota((n, n), 1) < qi * n + _iota((n, n), 0), 1.0, 0.0)


def sb_fwd(q, k, v, ride=None):
    bsz, h, t, d = q.shape
    n = Q_BLOCK
    hp = ATTN_FWD_HEADS_PER_STEP
    grid = (bsz, h // hp, t // n)
    r_in, r_specs, r_out, r_ospecs, r_scr = _ride_args(ride)

    def body(*refs):
        q_ref, k_ref, v_ref = refs[:3]
        o_ref, c_ref = refs[3 + len(r_in):5 + len(r_in)]
        if ride is not None:
            first, last = _grid_first_last(grid)
            copies = _ride_start(refs[3:3 + len(r_in)], refs[5 + len(r_in):5 + 2 * len(r_in)], refs[-3:], ride[1], first)
        qi = pl.program_id(2)
        lane = _iota((n, LANES), 1)

        c_ref[...] = jnp.zeros_like(c_ref)
        o_ref[...] = jnp.zeros_like(o_ref)

        def step(i, carry):
            j = qi - i
            rows = pl.ds(pl.multiple_of(j * n, n), n)
            for hh in range(hp):
                c_ref[hh] = jnp.where(lane == j, carry[hh], c_ref[hh])
            o, carry = sb_block([q_ref[hh] for hh in range(hp)], [k_ref[hh, rows, :] for hh in range(hp)],
                                [v_ref[hh, rows, :] for hh in range(hp)], carry, _sb_mask(qi, j))
            for hh in range(hp):
                o_ref[hh] += o[hh]
            return carry

        lax.fori_loop(0, qi + 1, step, [jnp.zeros((n, 1), F32) for _ in range(hp)])
        if ride is not None:
            _ride_wait(copies, last)

    blk = pl.BlockSpec((None, hp, n, d), lambda b, hg, i: (b, hg, i, 0))
    cblk = pl.BlockSpec((None, hp, n, LANES), lambda b, hg, i: (b, hg, i, 0))
    full = pl.BlockSpec((None, hp, t, d), lambda b, hg, i: (b, hg, 0, 0))
    return pl.pallas_call(
        body, name="sb_fwd", grid=grid, in_specs=[blk, full, full] + r_specs, out_specs=[blk, cblk] + r_ospecs,
        out_shape=[jax.ShapeDtypeStruct((bsz, h, t, d), F32), jax.ShapeDtypeStruct((bsz, h, t, LANES), F32)] + r_out,
        scratch_shapes=r_scr, compiler_params=_params(("arbitrary", "arbitrary", "arbitrary")),
    )(q, k, v, *r_in)


def sb_bwd(q, k, v, kept, do, ride=None):
    bsz, h, t, d = q.shape
    n = Q_BLOCK
    hp = SB_HEADS_PER_STEP
    grid = (bsz, h // hp, t // n)
    r_in, r_specs, r_out, r_ospecs, r_scr = _ride_args(ride)

    def body(*refs):
        q_ref, k_ref, v_ref, c_ref, do_ref = refs[:5]
        nin = 5 + len(r_in)
        dq_ref, dk_ref, dv_ref = refs[nin:nin + 3]
        if ride is not None:
            first, last = _grid_first_last(grid)
            copies = _ride_start(refs[5:nin], refs[nin + 3:nin + 3 + len(r_in)], refs[-3:], ride[1], first)
        qi = pl.program_id(2)

        @pl.when(qi == 0)
        def _():
            dk_ref[...] = jnp.zeros_like(dk_ref)
            dv_ref[...] = jnp.zeros_like(dv_ref)

        heads = range(hp)
        qv = [q_ref[hh] for hh in heads]
        kept_v = [c_ref[hh] for hh in heads]
        lane = _iota((n, LANES), 1)

        dq_ref[...] = jnp.zeros_like(dq_ref)

        half = n // SB_BWD_ROW_SPLIT

        def bwd_step(j, dcarry):
            rows = pl.ds(pl.multiple_of(j * n, n), n)
            kj, vj = [k_ref[hh, rows, :] for hh in heads], [v_ref[hh, rows, :] for hh in heads]
            mask = _sb_mask(qi, j)
            carry_in = [jnp.sum(jnp.where(lane == j, t_, 0.0), axis=1, keepdims=True) for t_ in kept_v]
            parts = [slice(p * half, (p + 1) * half) for p in range(SB_BWD_ROW_SPLIT)]
            items = [(hh, rs) for hh in heads for rs in parts]
            _, vjp = jax.vjp(sb_block, [qv[hh][rs] for hh, rs in items], [kj[hh] for hh, _ in items],
                             [vj[hh] for hh, _ in items], [carry_in[hh][rs] for hh, rs in items],
                             [mask[rs] for _, rs in items])
            dqj, dkj, dvj, dc, _ = vjp(([do_ref[hh, rs, :] for hh, rs in items], [dcarry[hh][rs] for hh, rs in items]))
            for i, (hh, rs) in enumerate(items):
                dq_ref[hh, rs, :] += dqj[i]
                dk_ref[hh, rows, :] += dkj[i]
                dv_ref[hh, rows, :] += dvj[i]
            k_ = SB_BWD_ROW_SPLIT
            return [jnp.concatenate(dc[hh * k_:(hh + 1) * k_], axis=0) for hh in heads]

        lax.fori_loop(0, qi + 1, bwd_step, [jnp.zeros((n, 1), F32) for _ in heads])
        if ride is not None:
            _ride_wait(copies, last)

    blk = pl.BlockSpec((None, hp, n, d), lambda b, hg, i: (b, hg, i, 0))
    cblk = pl.BlockSpec((None, hp, n, LANES), lambda b, hg, i: (b, hg, i, 0))
    full = pl.BlockSpec((None, hp, t, d), lambda b, hg, i: (b, hg, 0, 0))
    shp = jax.ShapeDtypeStruct((bsz, h, t, d), F32)
    return pl.pallas_call(
        body, name="sb_bwd", grid=grid, in_specs=[blk, full, full, cblk, blk] + r_specs,
        out_specs=[blk, full, full] + r_ospecs, out_shape=[shp, shp, shp] + r_out,
        scratch_shapes=r_scr, compiler_params=_params(("arbitrary", "arbitrary", "arbitrary")),
    )(q, k, v, kept, do, *r_in)


def _bdot(a, b, ca, cb):
    return _dg_raw(a, b, ca, cb, P_BF16)


def _mla_scores(qn, qp, knj, kpj, qi, j):
    n = Q_BLOCK
    mask = j * n + _iota((n, n), 1) <= qi * n + _iota((n, n), 0)
    scale = (MLA_NOPE + MLA_ROPE) ** -0.5
    return _each(lambda a_, b_, k_: jnp.where(mask, (_bdot(a_, k_, 1, 1) + _bdot(b_, kpj, 1, 1)) * scale, NEG_BIG),
                 qn, qp, knj)


def _mla_specs(t, hp):
    n = Q_BLOCK
    return (pl.BlockSpec((None, hp, n, MLA_NOPE), lambda b, hg, i: (b, hg, i, 0)),
            pl.BlockSpec((None, hp, n, MLA_ROPE), lambda b, hg, i: (b, hg, i, 0)),
            pl.BlockSpec((None, hp, t, MLA_NOPE), lambda b, hg, i: (b, hg, 0, 0)),
            pl.BlockSpec((None, None, t, MLA_ROPE), lambda b, hg, i: (b, 0, 0, 0)),
            pl.BlockSpec((None, hp, n, 1), lambda b, hg, i: (b, hg, i, 0)))


def mla_fwd(qn, qp, kn, kp, v):
    bsz, h, t, _ = qn.shape
    n, hp = Q_BLOCK, ATTN_FWD_HEADS_PER_STEP
    heads = range(hp)

    def body(qn_ref, qp_ref, kn_ref, kp_ref, v_ref, o_ref, lse_ref):
        qi = pl.program_id(2)
        qn_v, qp_v = [qn_ref[hh] for hh in heads], [qp_ref[hh] for hh in heads]

        o_ref[...] = jnp.zeros_like(o_ref)

        def step(j, state):
            m, l = state
            rows = pl.ds(pl.multiple_of(j * n, n), n)
            s = _mla_scores(qn_v, qp_v, [kn_ref[hh, rows, :] for hh in heads], kp_ref[rows, :], qi, j)
            m_new = _each(lambda m_, s_: jnp.maximum(m_, jnp.max(s_, axis=1, keepdims=True)), m, s)
            p = _each(lambda s_, m_: jnp.exp(s_ - m_), s, m_new)
            corr = _each(lambda a_, b_: jnp.exp(a_ - b_), m, m_new)
            l = _each(lambda l_, c_, p_: l_ * c_ + jnp.sum(p_, axis=1, keepdims=True), l, corr, p)
            pv = _each(lambda p_, v_: _bdot(p_, v_, 1, 0), p, [v_ref[hh, rows, :] for hh in heads])
            for hh in heads:
                o_ref[hh] = o_ref[hh] * corr[hh] + pv[hh]
            return m_new, l

        init = ([jnp.full((n, 1), NEG_BIG, F32) for _ in heads], [jnp.zeros((n, 1), F32) for _ in heads])
        m, l = lax.fori_loop(0, qi + 1, step, init)
        for hh in heads:
            o_ref[hh] = o_ref[hh] / l[hh]
            lse_ref[hh] = m[hh] + jnp.log(l[hh])

    qn_s, qp_s, kn_s, kp_s, row_s = _mla_specs(t, hp)
    return pl.pallas_call(
        body, name="mla_fwd", grid=(bsz, h // hp, t // n), in_specs=[qn_s, qp_s, kn_s, kp_s, kn_s],
        out_specs=[qn_s, row_s],
        out_shape=[jax.ShapeDtypeStruct(qn.shape, F32), jax.ShapeDtypeStruct((bsz, h, t, 1), F32)],
        compiler_params=_params(("parallel", "parallel", "arbitrary")),
    )(qn, qp, kn, kp, v)


def mla_bwd(qn, qp, kn, kp, v, o, lse, do):
    bsz, h, t, _ = qn.shape
    n, hp = Q_BLOCK, MLA_HEADS_PER_STEP
    heads = range(hp)
    scale = (MLA_NOPE + MLA_ROPE) ** -0.5

    def body(qn_ref, qp_ref, kn_ref, kp_ref, v_ref, o_ref, lse_ref, do_ref,
             dqn_ref, dqp_ref, dkn_ref, dkp_ref, dv_ref):
        hg, qi = pl.program_id(1), pl.program_id(2)

        @pl.when(qi == 0)
        def _():
            dkn_ref[...] = jnp.zeros_like(dkn_ref)
            dv_ref[...] = jnp.zeros_like(dv_ref)

        @pl.when(jnp.logical_and(qi == 0, hg == 0))
        def _():
            dkp_ref[...] = jnp.zeros_like(dkp_ref)

        qn_v, qp_v = [qn_ref[hh] for hh in heads], [qp_ref[hh] for hh in heads]
        do_v, lse_v = [do_ref[hh] for hh in heads], [lse_ref[hh] for hh in heads]
        dsum = [jnp.sum(do_v[hh] * o_ref[hh], axis=1, keepdims=True) for hh in heads]

        dqn_ref[...] = jnp.zeros_like(dqn_ref)
        dqp_ref[...] = jnp.zeros_like(dqp_ref)

        def step(j, _):
            rows = pl.ds(pl.multiple_of(j * n, n), n)
            knj, vj, kpj = [kn_ref[hh, rows, :] for hh in heads], [v_ref[hh, rows, :] for hh in heads], kp_ref[rows, :]
            s = _mla_scores(qn_v, qp_v, knj, kpj, qi, j)
            p = _each(lambda s_, l_: jnp.exp(s_ - l_), s, lse_v)
            dp = _each(lambda d_, v_: _bdot(d_, v_, 1, 1), do_v, vj)
            ds = _each(lambda p_, dp_, d_: p_ * (dp_ - d_) * scale, p, dp, dsum)
            dqn = _each(lambda ds_, k_: _bdot(ds_, k_, 1, 0), ds, knj)
            dqp = _each(lambda ds_: _bdot(ds_, kpj, 1, 0), ds)
            dkn = _each(lambda ds_, q_: _bdot(ds_, q_, 0, 0), ds, qn_v)
            dv = _each(lambda p_, d_: _bdot(p_, d_, 0, 0), p, do_v)
            dkp = _each(lambda ds_, q_: _bdot(ds_, q_, 0, 0), ds, qp_v)
            for hh in heads:
                dqn_ref[hh] += dqn[hh]
                dqp_ref[hh] += dqp[hh]
                dkn_ref[hh, rows, :] += dkn[hh]
                dv_ref[hh, rows, :] += dv[hh]
            dkp_ref[rows, :] += functools.reduce(lambda a_, b_: a_ + b_, dkp)
            return 0

        lax.fori_loop(0, qi + 1, step, 0)

    qn_s, qp_s, kn_s, kp_s, row_s = _mla_specs(t, hp)
    return pl.pallas_call(
        body, name="mla_bwd", grid=(bsz, h // hp, t // n),
        in_specs=[qn_s, qp_s, kn_s, kp_s, kn_s, qn_s, row_s, qn_s],
        out_specs=[qn_s, qp_s, kn_s, kp_s, kn_s],
        out_shape=[jax.ShapeDtypeStruct(qn.shape, F32), jax.ShapeDtypeStruct(qp.shape, F32),
                   jax.ShapeDtypeStruct(kn.shape, F32), jax.ShapeDtypeStruct(kp.shape, F32),
                   jax.ShapeDtypeStruct(v.shape, F32)],
        compiler_params=_params(("parallel", "arbitrary", "arbitrary")),
    )(qn, qp, kn, kp, v, o, lse, do)


def rope(name, x, pos, inv_freq, sign):
    bsz, hx, t, d = x.shape
    half = d // 2

    tt = _largest_tile(t, 512, SUBLANES)

    def body(x_ref, pos_ref, f_ref, o_ref):
        ang = pos_ref[...].astype(F32) * f_ref[...]
        cos, sin = jnp.cos(ang), sign * jnp.sin(ang)
        ri, ci = _iota((d, d), 0), _iota((d, d), 1)
        rot = jnp.where(ri == ci + half, -1.0, 0.0) + jnp.where(ri + half == ci, 1.0, 0.0)
        for hh in range(hx):
            xv = x_ref[hh]
            o_ref[hh] = xv * cos + mmf(xv, rot) * sin

    blk = pl.BlockSpec((None, hx, tt, d), lambda b, i: (b, 0, i, 0))
    return pl.pallas_call(
        body, name=name, grid=(bsz, t // tt),
        in_specs=[blk, pl.BlockSpec((None, tt, 1), lambda b, i: (b, i, 0)), pl.BlockSpec((1, d), lambda b, i: (0, 0))],
        out_specs=blk, out_shape=jax.ShapeDtypeStruct(x.shape, F32),
        compiler_params=_params(("parallel", "parallel")),
    )(x, pos, inv_freq)


def loss_head(h, target, tm):
    n, d = h.shape

    def body(h_ref, t_ref, dh_ref, l_ref):
        @pl.when(pl.program_id(0) == 0)
        def _():
            l_ref[...] = jnp.zeros_like(l_ref)

        e = h_ref[...] - t_ref[...]
        dh_ref[...] = e * (1.0 / d)
        l_ref[...] += jnp.sum(e * e, axis=(0, 1), keepdims=True) * (0.5 / d)

    row = pl.BlockSpec((tm, d), lambda i: (i, 0))
    dh, l = pl.pallas_call(
        body, name="loss_head", grid=(n // tm,), in_specs=[row, row],
        out_specs=[row, pl.BlockSpec((SUBLANES, LANES), lambda i: (0, 0))],
        out_shape=[jax.ShapeDtypeStruct((n, d), F32), jax.ShapeDtypeStruct((SUBLANES, LANES), F32)],
        compiler_params=_params(("arbitrary",)),
    )(h, target)
    return dh, l[0, 0]


def _exchange_copies(src_refs, out_refs, send_sems, recv_sems, local_sems, gather):
    x, y, c = lax.axis_index("x"), lax.axis_index("y"), lax.axis_index("c")
    me = 4 * x + 2 * y + c
    local, remote = [], []
    for p, (src_ref, out_ref) in enumerate(zip(src_refs, out_refs)):
        local.append(pltpu.make_async_copy(src_ref if gather else src_ref.at[me], out_ref.at[me], local_sems.at[p]))
        for m in range(1, N_DEV):
            px, py, pc = x ^ (m >> 2), y ^ ((m >> 1) & 1), c ^ (m & 1)
            peer = 4 * px + 2 * py + pc
            remote.append(pltpu.make_async_remote_copy(
                src_ref=src_ref if gather else src_ref.at[peer], dst_ref=out_ref.at[me],
                send_sem=send_sems.at[p, m], recv_sem=recv_sems.at[p, m],
                device_id=(px, py, pc), device_id_type=pl.DeviceIdType.MESH))
    return local, remote


def _exchange_start(copies):
    local, remote = copies
    for cp in local + remote:
        cp.start()


def _exchange_wait(copies):
    local, remote = copies
    for cp in remote:
        cp.wait_recv()
    for cp in remote:
        cp.wait_send()
    for cp in local:
        cp.wait()


def _exchange_scratch(count):
    return [pltpu.SemaphoreType.DMA((count, N_DEV)), pltpu.SemaphoreType.DMA((count, N_DEV)),
            pltpu.SemaphoreType.DMA((count,))]


def _exchange_out(src, gather):
    return jax.ShapeDtypeStruct(((N_DEV,) + src.shape) if gather else src.shape, src.dtype)


def peer_exchange(name, srcs, gather):
    count = len(srcs)

    def body(*refs):
        copies = _exchange_copies(refs[:count], refs[count:2 * count], *refs[2 * count:], gather)
        _exchange_start(copies)
        _exchange_wait(copies)

    hbm = pl.BlockSpec(memory_space=pl.ANY)
    return pl.pallas_call(
        body, name=name, in_specs=[hbm] * count, out_specs=[hbm] * count,
        out_shape=[_exchange_out(s_, gather) for s_ in srcs], scratch_shapes=_exchange_scratch(count),
    )(*srcs)


def _grid_first_last(grid):
    ids = [pl.program_id(a) for a in range(len(grid))]
    first = functools.reduce(jnp.logical_and, [i == 0 for i in ids])
    last = functools.reduce(jnp.logical_and, [i == g - 1 for i, g in zip(ids, grid)])
    return first, last


def _ride_start(src_refs, out_refs, sem_refs, gather, first):
    copies = _exchange_copies(src_refs, out_refs, *sem_refs, gather)

    @pl.when(first)
    def _():
        _exchange_start(copies)

    return copies


def _ride_wait(copies, last):
    @pl.when(last)
    def _():
        _exchange_wait(copies)


def _ride_args(ride):
    if ride is None:
        return [], [], [], [], []
    srcs, gather = ride
    hbm = pl.BlockSpec(memory_space=pl.ANY)
    return (list(srcs), [hbm] * len(srcs), [_exchange_out(s_, gather) for s_ in srcs], [hbm] * len(srcs),
            _exchange_scratch(len(srcs)))


ADAM_BLOCK_BYTES = 4 * 1024 * 1024


def adamw_sum(name, parts, w, m, v):
    r, cols = w.shape
    tr = _largest_tile(r, min(ADAM_ROWS, max(SUBLANES, ADAM_BLOCK_BYTES // (N_DEV * cols * 4))), SUBLANES)

    def body(p_ref, w_ref, m_ref, v_ref, g_ref, d_ref, nm_ref, nv_ref):
        g = p_ref[0]
        for j in range(1, N_DEV):
            g = g + p_ref[j]
        mm_ = ADAM_B1 * m_ref[...] + (1.0 - ADAM_B1) * g
        vv = ADAM_B2 * v_ref[...] + (1.0 - ADAM_B2) * (g * g)
        m_hat = mm_ / (1.0 - ADAM_B1 ** ADAM_STEP)
        v_hat = vv / (1.0 - ADAM_B2 ** ADAM_STEP)
        g_ref[...] = g
        d_ref[...] = -ADAM_LR * (m_hat / (jnp.sqrt(v_hat) + ADAM_EPS) + ADAM_WD * w_ref[...])
        nm_ref[...] = mm_
        nv_ref[...] = vv

    row = pl.BlockSpec((tr, cols), lambda i: (i, 0))
    shp = jax.ShapeDtypeStruct((r, cols), F32)
    return pl.pallas_call(
        body, name=name, grid=(r // tr,),
        in_specs=[pl.BlockSpec((N_DEV, tr, cols), lambda i: (0, i, 0)), row, row, row],
        out_specs=[row] * 4, out_shape=[shp] * 4,
        compiler_params=_params(("parallel",)),
    )(parts, w, m, v)


WEIGHTS = ['l0_w_in', 'rwkv_mix', 'rwkv_w0', 'rwkv_w2', 'rwkv_a0', 'rwkv_a2', 'rwkv_g2', 'rwkv_k_k', 'rwkv_k_a',
           'rwkv_r_k', 'rwkv_ln_g', 'rwkv_ln_b', 'ssm_conv_w', 'ssm_conv_b', 'ssm_dt_bias', 'ssm_a_log', 'ssm_d',
           'ssm_norm_g', 'l0_w_out', 'l0_ln1_g', 'l0_ln1_b', 'ffn0_w_up', 'ffn0_conv_w', 'ffn0_conv_b',
           'ffn0_w_down', 'l0_ln2_g', 'l0_ln2_b', 'l1_w_in', 'mla_q_norm_g', 'mla_w_uq', 'mla_kv_norm_g',
           'mla_w_ukv', 'l1_w_out', 'l1_ln1_g', 'l1_ln1_b', 'ffn1_w_up', 'ffn1_conv_w', 'ffn1_conv_b',
           'ffn1_w_down', 'l1_ln2_g', 'l1_ln2_b']
SHARD_AXIS = {'l0_w_in': 1, 'rwkv_w2': 1, 'rwkv_a2': 1, 'rwkv_g2': 1, 'ssm_conv_w': 1, 'l0_w_out': 0,
              'ffn0_w_up': 1, 'ffn0_conv_w': 1, 'ffn0_w_down': 0, 'l1_w_in': 1, 'mla_w_uq': 1, 'mla_w_ukv': 1,
              'l1_w_out': 0, 'ffn1_w_up': 1, 'ffn1_conv_w': 1, 'ffn1_w_down': 0}
MATMUL_W = ['l0_w_in', 'rwkv_w2', 'rwkv_a2', 'rwkv_g2', 'l0_w_out', 'ffn0_w_up', 'ffn0_w_down', 'l1_w_in',
            'mla_w_uq', 'mla_w_ukv', 'l1_w_out', 'ffn1_w_up', 'ffn1_w_down']
CONV_W = ['ssm_conv_w', 'ffn0_conv_w', 'ffn1_conv_w']
TOK_TILE = 256
ADAM_ROWS = 512


def _ceil_to(size, unit):
    return -(-size // unit) * unit


def _flat_rows(pieces, seg_rows, total_rows):
    unit = seg_rows * LANES
    out, total = [], 0
    for p in pieces:
        p = jnp.concatenate([q.reshape(-1) for q in p]) if isinstance(p, list) else p.reshape(-1)
        pad = _ceil_to(p.size, unit) - p.size
        out.append(jnp.pad(p, (0, pad)) if pad else p)
        total += p.size + pad
    tail = _ceil_to(total, total_rows * LANES) - total
    if tail:
        out.append(jnp.zeros((tail,), out[0].dtype))
    return jnp.concatenate(out).reshape(-1, LANES)


def _unflatten(flat2d, shapes, seg_rows):
    flat = flat2d.reshape(-1)
    out, off = [], 0
    for shp in shapes:
        if isinstance(shp, list):
            seg, pos = [], off
            for s_ in shp:
                seg.append(flat[pos:pos + math.prod(s_)].reshape(s_))
                pos += math.prod(s_)
            out.append(seg)
            size = pos - off
        else:
            size = math.prod(shp)
            out.append(flat[off:off + size].reshape(shp))
        off += _ceil_to(size, seg_rows * LANES)
    return out


def _row(v):
    return v.reshape(1, -1)


def _pad_lanes(v):
    return jnp.pad(v.reshape(1, -1), ((0, 0), (0, LANES - v.size)))


def _local_step(a, w, comm=None):
    x = a['x']
    bsz, t, d = x.shape
    n = bsz * t
    tm = TOK_TILE
    pos = a['positions'].reshape(bsz, t, 1)
    inv_freq = 1.0 / (ROPE_THETA ** (jnp.arange(0, MLA_ROPE, 2, dtype=F32) / MLA_ROPE))
    inv_freq = jnp.concatenate([inv_freq, inv_freq]).reshape(1, MLA_ROPE)
    target = a['loss_target'].reshape(n, d)

    wi0 = w['l0_w_in']
    win0 = jnp.concatenate([wi0[:, :RKV_COLS], wi0[:, RWKV_COLS:L0_COLS - SSM_HEADS], wi0[:, RKV_COLS:RWKV_COLS],
                            wi0[:, L0_COLS - SSM_HEADS:], jnp.zeros((d, L0_PAD - L0_COLS), wi0.dtype)], axis=1)
    w2p = jnp.concatenate([w['rwkv_w2'], jnp.zeros_like(w['rwkv_w2'])], axis=0)
    a2p = jnp.concatenate([jnp.zeros_like(w['rwkv_a2']), w['rwkv_a2']], axis=0)
    mix = a['rwkv_mix']
    taps = jnp.stack([mix, 1.0 - mix])
    zero_b = jnp.zeros((1, mix.size), F32)
    rw_map = lambda j: j + jnp.where(j >= RKV_COLS // CONV_TILE, (P0_LORA - RKV_COLS) // CONV_TILE, 0)
    ssm_map = lambda j: j + (P0_Z + SSM_DIM) // CONV_TILE
    gate_map = lambda j: j
    up_map = lambda j: j + D_FF // CONV_TILE
    dt_col = P0_DT // LANES
    dtb, alog, dsk = _pad_lanes(a['ssm_dt_bias']), _pad_lanes(a['ssm_a_log']), _pad_lanes(a['ssm_d'])
    pre_p = [_row(a['rwkv_w0']), w2p, _row(a['rwkv_a0']), a2p, w['rwkv_g2'], _row(a['rwkv_k_k']), _row(a['rwkv_k_a'])]
    post_p = [_row(a['rwkv_ln_g']), _row(a['rwkv_ln_b']), _row(a['rwkv_r_k'])]
    sp_p = [dsk, _row(a['ssm_norm_g'])]

    def ln(name, h, y, layer, which):
        ps = [_row(a[f'l{layer}_ln{which}_g']), _row(a[f'l{layer}_ln{which}_b'])]
        return tok_fwd(name, f_ln, [(h, d, 0), (y, d, 0)], ps, [d], 2 * tm, n, t)[0]

    def ffn_fwd(layer, h):
        up = mm(h, w[f'ffn{layer}_w_up'], 'nn', f'ffn{layer}_up')
        act = dwconv_fwd(f'ffn{layer}_conv', up.reshape(bsz, t, 2 * D_FF), gate_map, w[f'ffn{layer}_conv_w'],
                         _row(a[f'ffn{layer}_conv_b']), 3, True, upmap=up_map, out_dtype=BF16)
        act = act.reshape(n, D_FF)
        return up, act, mm(act, w[f'ffn{layer}_w_down'], 'nn', f'ffn{layer}_down')

    x2 = x.reshape(n, d)
    proj0 = mm(x2, win0, 'nn', 'l0_in')
    p0 = proj0.reshape(bsz, t, L0_PAD)
    xs_r = dwconv_fwd('rwkv_shift', p0, rw_map, taps, zero_b, 2, False).reshape(n, RWKV_COLS)
    lora_blk = RKV_COLS // LANES
    pre_x = [(xs_r, RWKV_DIM, 0), (xs_r, RWKV_DIM, 1), (xs_r, RWKV_DIM, 2), (xs_r, LANES, lora_blk),
             (xs_r, LANES, lora_blk + 1)]
    heads64 = (RWKV_HEADS, HEAD_DIM)
    r_, v_, lw, kmod, al, be, gt = tok_fwd('rwkv_pre', f_rwkv_pre, pre_x, pre_p, [heads64] * 6 + [RWKV_DIM],
                                           tm, n, t)
    scan_in = [r_, lw, kmod, v_, al, be]
    if comm is None:
        y_h, rstates = rwkv_scan_fwd(*scan_in)
    else:
        y_h, rstates, *gots = rwkv_scan_fwd(*scan_in, ride=comm.weights_ride(MID_GATHER))
        w = {**w, **comm.weights(MID_GATHER, gots)}
    wi1 = w['l1_w_in']
    win1 = jnp.concatenate([wi1, jnp.zeros((d, L1_PAD - L1_COLS), wi1.dtype)], axis=1)
    wq3 = w['mla_w_uq'].reshape(-1, 8, MLA_NOPE + MLA_ROPE)
    wkv3 = w['mla_w_ukv'].reshape(-1, 8, 2 * MLA_NOPE)
    mla_p = [_row(a['mla_q_norm_g']), wq3[:, :, :MLA_NOPE].reshape(-1, 512), wq3[:, :, MLA_NOPE:].reshape(-1, 256),
             _row(a['mla_kv_norm_g']), wkv3[:, :, :MLA_NOPE].reshape(-1, 512), wkv3[:, :, MLA_NOPE:].reshape(-1, 512)]
    post_x = [y_h, r_, kmod, v_, (gt, 512, 0)]
    y_a = tok_fwd('rwkv_post', f_rwkv_post, post_x, post_p, [RWKV_DIM], tm, n, t)[0]
    xbc = dwconv_fwd('ssm_conv', p0, ssm_map, w['ssm_conv_w'], _row(a['ssm_conv_b']), 4, True)
    ys, sstates = ssd_fwd(xbc, p0, dt_col, dtb, alog)
    xbc2 = xbc.reshape(n, 2 * SSM_DIM)
    sp_x = [(ys.reshape(n, SSM_DIM), SSM_DIM, 0), (xbc2, SSM_DIM, 0), (proj0, SSM_DIM, P0_Z // SSM_DIM)]
    y_b = tok_fwd('ssd_post', f_ssd_post, sp_x, sp_p, [SSM_DIM], tm, n, t)[0]
    wo0 = w['l0_w_out']
    mixed0 = mm(y_b, wo0, 'nn', 'l0_out_b', b_rows=(512, 512), add=mm(y_a, wo0, 'nn', 'l0_out_a', b_rows=(0, 512)))
    h1 = ln('l0_ln1', x2, mixed0, 0, 1)
    up0, act0, f0 = ffn_fwd(0, h1)
    h2 = ln('l0_ln2', h1, f0, 0, 2)

    proj1 = mm(h2, win1, 'nn', 'l1_in')
    q_sb, k_sb, v_sb = tok_fwd('sb_split', f_same, [(proj1, 512, 0), (proj1, 512, 1), (proj1, 512, 2)], [],
                               [heads64] * 3, tm, n, t)
    if comm is None:
        o_c, sb_kept = sb_fwd(q_sb, k_sb, v_sb)
    else:
        o_c, sb_kept, *gots = sb_fwd(q_sb, k_sb, v_sb, ride=comm.weights_ride(FFN1_GATHER))
        w = {**w, **comm.weights(FFN1_GATHER, gots)}
    mla_x = [(proj1, P1_CKV - P1_CQ, P1_CQ // (P1_CKV - P1_CQ)), (proj1, LANES, P1_CKV // LANES)]
    qn, qp_raw, kn, vv = tok_fwd('mla_pre', f_mla_pre, mla_x, mla_p, [heads64, (8, MLA_ROPE), heads64, heads64],
                                 tm, n, t)
    kp_raw = proj1[:, P1_KPE:P1_KPE + MLA_ROPE].reshape(bsz, 1, t, MLA_ROPE)
    qp = rope('rope_q', qp_raw, pos, inv_freq, 1.0)
    kp = rope('rope_k', kp_raw, pos, inv_freq, 1.0)
    o_d, lse_d = mla_fwd(qn, qp, kn, kp, vv)
    y_cd = tok_fwd('attn_merge', f_concat, [o_c, o_d], [], [2 * RWKV_DIM], tm, n, t, out_dtype=BF16)[0]
    wo1 = w['l1_w_out']
    mixed1 = mm(y_cd, wo1, 'nn', 'l1_out')
    h3 = ln('l1_ln1', h2, mixed1, 1, 1)
    up1, act1, f1 = ffn_fwd(1, h3)
    h4 = ln('l1_ln2', h3, f1, 1, 2)
    dh4, loss = loss_head(h4, target, tm)

    g = {}

    def ln_bwd(name, h, y, layer, which, dout):
        ps = [_row(a[f'l{layer}_ln{which}_g']), _row(a[f'l{layer}_ln{which}_b'])]
        (dh, dy), (dg, db) = tok_bwd(name, f_ln, [(h, d, 0), (y, d, 0)], ps, [[dout]], 2 * tm, n, t)
        g[f'l{layer}_ln{which}_g'], g[f'l{layer}_ln{which}_b'] = dg.reshape(-1), db.reshape(-1)
        return dh, dy

    def ffn_bwd(layer, h, up, act, df, dh_res):
        wup, wdown = w[f'ffn{layer}_w_up'], w[f'ffn{layer}_w_down']
        g[f'ffn{layer}_w_down'] = mm(act, df, 'tn', f'ffn{layer}_dwdown')
        dact = mm(df, wdown, 'nt', f'ffn{layer}_dact').reshape(bsz, t, D_FF)
        dgate, dcw, dcb, dup = dwconv_bwd(f'ffn{layer}_conv_bwd', up.reshape(bsz, t, 2 * D_FF), gate_map,
                                          w[f'ffn{layer}_conv_w'], _row(a[f'ffn{layer}_conv_b']), 3, True, dact,
                                          upmap=up_map, grad_dtype=BF16)
        dgate, dup = dgate.reshape(n, D_FF), dup.reshape(n, D_FF)
        g[f'ffn{layer}_conv_w'], g[f'ffn{layer}_conv_b'] = dcw, dcb.reshape(-1)
        g[f'ffn{layer}_w_up'] = (mm(h, dgate, 'tn', f'ffn{layer}_dwgate'), mm(h, dup, 'tn', f'ffn{layer}_dwup'))
        dh = mm(dgate, wup, 'nt', f'ffn{layer}_dh_gate', add=dh_res, b_cols=(0, D_FF))
        return mm(dup, wup, 'nt', f'ffn{layer}_dh_up', add=dh, b_cols=(D_FF, D_FF))

    dh3_res, df1 = ln_bwd('l1_ln2_bwd', h3, f1, 1, 2, dh4)
    dh3 = ffn_bwd(1, h3, up1, act1, df1, dh3_res)
    dh2_res, dmixed1 = ln_bwd('l1_ln1_bwd', h2, mixed1, 1, 1, dh3)
    g['l1_w_out'] = mm(y_cd, dmixed1, 'tn', 'l1_dwout')
    dy_cd = mm(dmixed1, wo1, 'nt', 'l1_dy')
    dy_c, dy_d = tok_fwd('attn_split', f_same, [(dy_cd, 512, 0), (dy_cd, 512, 1)], [], [heads64] * 2, tm, n, t)
    parts = {}
    if comm is None:
        dq_sb, dk_sb, dv_sb = sb_bwd(q_sb, k_sb, v_sb, sb_kept, dy_c)
    else:
        dq_sb, dk_sb, dv_sb, *parts['a'] = sb_bwd(q_sb, k_sb, v_sb, sb_kept, dy_c, ride=comm.grad_ride('a', g))
    dqn, dqp, dkn, dkp, dvv = mla_bwd(qn, qp, kn, kp, vv, o_d, lse_d, dy_d)
    dqp_raw = rope('rope_q_bwd', dqp, pos, inv_freq, -1.0)
    dkp_raw = rope('rope_k_bwd', dkp, pos, inv_freq, -1.0).reshape(n, MLA_ROPE)
    (dcq, dckv), (dqg, dwq_n, dwq_p, dkvg, dwk, dwv) = tok_bwd('mla_pre_bwd', f_mla_pre, mla_x, mla_p,
                                                               [[dqn], [dqp_raw], [dkn], [dvv]], tm, n, t)
    g['mla_q_norm_g'], g['mla_kv_norm_g'] = dqg.reshape(-1), dkvg.reshape(-1)
    g['mla_w_uq'] = jnp.concatenate([dwq_n.reshape(-1, 8, MLA_NOPE), dwq_p.reshape(-1, 8, MLA_ROPE)],
                                    axis=2).reshape(-1, 8 * (MLA_NOPE + MLA_ROPE))
    g['mla_w_ukv'] = jnp.concatenate([dwk.reshape(-1, 8, MLA_NOPE), dwv.reshape(-1, 8, MLA_NOPE)],
                                     axis=2).reshape(-1, 16 * MLA_NOPE)
    dkp_pad = jnp.pad(dkp_raw, ((0, 0), (0, LANES - MLA_ROPE)))
    dproj1 = tok_fwd('l1_dproj', f_concat, [dq_sb, dk_sb, dv_sb, (dcq, 256, 0), (dckv, LANES, 0),
                                            (dkp_pad, LANES, 0)], [], [L1_PAD], tm, n, t, out_dtype=BF16)[0]
    g['l1_w_in'] = mm(h2, dproj1, 'tn', 'l1_dwin')[:, :L1_COLS]
    dh2 = mm(dproj1, win1, 'nt', 'l1_dh', add=dh2_res)

    dh1_res, df0 = ln_bwd('l0_ln2_bwd', h1, f0, 0, 2, dh2)
    dh1 = ffn_bwd(0, h1, up0, act0, df0, dh1_res)
    dx_res, dmixed0 = ln_bwd('l0_ln1_bwd', x2, mixed0, 0, 1, dh1)
    g['l0_w_out'] = (mm(y_a, dmixed0, 'tn', 'l0_dwout_a'), mm(y_b, dmixed0, 'tn', 'l0_dwout_b'))
    dy_a = mm(dmixed0, wo0, 'nt', 'l0_dy_a', b_rows=(0, 512))
    dy_b = mm(dmixed0, wo0, 'nt', 'l0_dy_b', b_rows=(512, 512))
    (dy_r, dr1, dkm1, dv1, dgt), (dlng, dlnb, drk) = tok_bwd('rwkv_post_bwd', f_rwkv_post, post_x, post_p, [[dy_a]],
                                                            tm, n, t, dx_layouts=[heads64] * 4 + [None])
    g['rwkv_ln_g'], g['rwkv_ln_b'] = dlng.reshape(-1), dlnb.reshape(-1)
    g['rwkv_r_k'] = drk.reshape(RWKV_HEADS, HEAD_DIM)
    (dys, dxs_skip, dz), (ddsk, dng) = tok_bwd('ssd_post_bwd', f_ssd_post, sp_x, sp_p, [[dy_b]], tm, n, t)
    g['ssm_d'], g['ssm_norm_g'] = ddsk[0, :SSM_HEADS], dng.reshape(-1)
    dxbc_act, ddtr, ddtb, dalog = ssd_bwd(xbc, p0, dt_col, dtb, alog, sstates, dys.reshape(bsz, t, SSM_DIM),
                                          dxs_skip.reshape(bsz, t, SSM_DIM))
    g['ssm_dt_bias'], g['ssm_a_log'] = ddtb[0, :SSM_HEADS], dalog[0, :SSM_HEADS]
    dxbc, dscw, dscb = dwconv_bwd('ssm_conv_bwd', p0, ssm_map, w['ssm_conv_w'], _row(a['ssm_conv_b']), 4, True,
                                  dxbc_act)
    g['ssm_conv_w'], g['ssm_conv_b'] = dscw, dscb.reshape(-1)
    if comm is None:
        dscan = rwkv_scan_bwd(*scan_in, rstates, dy_r)
    else:
        outs = rwkv_scan_bwd(*scan_in, rstates, dy_r, ride=comm.grad_ride('b', g))
        dscan, parts['b'] = outs[:6], outs[6:]
    dr2, dlw, dk2, dv2, dal, dbe = dscan
    pre_ct = [[dr1, dr2], [dv1, dv2], [dlw], [dkm1, dk2], [dal], [dbe], [dgt]]
    dpre_x, dpre_p = tok_bwd('rwkv_pre_bwd', f_rwkv_pre, pre_x, pre_p, pre_ct, tm, n, t)
    g['rwkv_w0'], g['rwkv_a0'] = dpre_p[0].reshape(-1), dpre_p[2].reshape(-1)
    g['rwkv_w2'], g['rwkv_a2'], g['rwkv_g2'] = dpre_p[1][:64], dpre_p[3][64:], dpre_p[4]
    g['rwkv_k_k'], g['rwkv_k_a'] = dpre_p[5].reshape(-1), dpre_p[6].reshape(-1)
    dxs_r = jnp.concatenate(dpre_x, axis=1).reshape(bsz, t, RWKV_COLS)
    d_rw, dtaps, _ = dwconv_bwd('rwkv_shift_bwd', p0, rw_map, taps, zero_b, 2, False, dxs_r)
    d_rw = d_rw.reshape(n, RWKV_COLS)
    g['rwkv_mix'] = dtaps[0] - dtaps[1]
    lora_cols = RWKV_COLS - RKV_COLS
    dproj0 = tok_fwd('l0_dproj', f_concat, [(d_rw, RKV_COLS, 0), (dz, SSM_DIM, 0),
                                            (dxbc.reshape(n, 2 * SSM_DIM), 2 * SSM_DIM, 0),
                                            (d_rw, lora_cols, RKV_COLS // lora_cols), (ddtr.reshape(n, LANES), LANES, 0)],
                     [], [L0_PAD], tm, n, t, out_dtype=BF16)[0]
    dwin0 = mm(x2, dproj0, 'tn', 'l0_dwin')
    g['l0_w_in'] = jnp.concatenate([dwin0[:, :RKV_COLS], dwin0[:, P0_LORA:P0_DT], dwin0[:, P0_Z:P0_LORA],
                                    dwin0[:, P0_DT:P0_DT + SSM_HEADS]], axis=1)
    if comm is None:
        dx = mm(dproj0, win0, 'nt', 'l0_dx', add=dx_res)
    else:
        dx, *parts['c'] = mm(dproj0, win0, 'nt', 'l0_dx', add=dx_res, ride=comm.grad_ride('c', g))
    return loss, dx.reshape(bsz, t, d), g, parts


GRAD_GROUPS = {
    'a': ['ffn1_w_up', 'ffn1_conv_w', 'ffn1_conv_b', 'ffn1_w_down', 'l1_ln2_g', 'l1_ln2_b'],
    'c': ['l0_w_in', 'rwkv_mix', 'rwkv_w0', 'rwkv_w2', 'rwkv_a0', 'rwkv_a2', 'rwkv_g2', 'rwkv_k_k', 'rwkv_k_a'],
}
GRAD_GROUPS['b'] = [nm for nm in WEIGHTS if nm not in GRAD_GROUPS['a'] + GRAD_GROUPS['c']]
FIRST_GATHER = ['l0_w_in', 'rwkv_w2', 'rwkv_a2', 'rwkv_g2'] + CONV_W
MID_GATHER = ['l0_w_out', 'ffn0_w_up', 'ffn0_w_down', 'l1_w_in', 'mla_w_uq', 'mla_w_ukv', 'l1_w_out']
FFN1_GATHER = ['ffn1_w_up', 'ffn1_w_down']
BF16_ROWS = 16


def _name_kinds(names):
    return ([nm for nm in names if nm in MATMUL_W], [nm for nm in names if nm in SHARD_AXIS and nm not in MATMUL_W],
            [nm for nm in names if nm not in SHARD_AXIS])


class _Comm:
    def __init__(self, a):
        self.a = a

    def weights_ride(self, names):
        big, conv, _ = _name_kinds(names)
        srcs = [self.a[nm].astype(BF16) for nm in big]
        if conv:
            srcs.append(_flat_rows([lax.bitcast_convert_type(self.a[nm], BF16) for nm in conv], BF16_ROWS, BF16_ROWS))
        return srcs, True

    def weights(self, names, gots):
        big, conv, _ = _name_kinds(names)
        out = {}
        for nm, got in zip(big, gots):
            out[nm] = (got.reshape(-1, got.shape[2]) if SHARD_AXIS[nm] == 0
                       else jnp.concatenate([got[k] for k in range(N_DEV)], axis=1))
        if conv:
            shapes = [self.a[nm].shape + (2,) for nm in conv]
            blocks = [_unflatten(gots[-1][k], shapes, BF16_ROWS) for k in range(N_DEV)]
            for i, nm in enumerate(conv):
                out[nm] = jnp.concatenate([lax.bitcast_convert_type(blocks[k][i], F32) for k in range(N_DEV)], axis=1)
        return out

    def first_weights(self):
        return self.weights(FIRST_GATHER, peer_exchange('gather_first_weights', self.weights_ride(FIRST_GATHER)[0], True))

    def grad_ride(self, group, g):
        def shard_of(nm, k):
            gv = g[nm]
            per = N_DEV
            if isinstance(gv, tuple):
                gv, k, per = gv[k // 4], k % 4, 4
            width = gv.shape[SHARD_AXIS[nm]] // per
            return lax.slice_in_dim(gv, k * width, (k + 1) * width, axis=SHARD_AXIS[nm])

        big, conv, small = _name_kinds(GRAD_GROUPS[group])
        srcs = [jnp.stack([shard_of(nm, k) for k in range(N_DEV)]) for nm in big]
        srcs.append(jnp.stack([_flat_rows([shard_of(nm, k) for nm in conv] + [[g[nm] for nm in small]],
                                          SUBLANES, ADAM_ROWS) for k in range(N_DEV)]))
        return srcs, False


def _step(a):
    comm = _Comm(a)
    loss, dx, _, parts = _local_step(a, comm.first_weights(), comm)
    loss = lax.psum(loss, ('x', 'y', 'c'))
    res = {}
    for group, names in GRAD_GROUPS.items():
        big, conv, small = _name_kinds(names)
        for nm, got in zip(big, parts[group]):
            res[nm] = adamw_sum(f'adamw_{nm}', got, a[nm], a['m_' + nm], a['v_' + nm])
        flat = lambda prefix: _flat_rows([a[prefix + nm] for nm in conv] + [[a[prefix + nm] for nm in small]],
                                         SUBLANES, ADAM_ROWS)
        outs = adamw_sum(f'adamw_{group}', parts[group][-1], flat(''), flat('m_'), flat('v_'))
        shapes = [a[nm].shape for nm in conv] + [[a[nm].shape for nm in small]]
        per_out = [_unflatten(o, shapes, SUBLANES) for o in outs]
        for i, nm in enumerate(conv):
            res[nm] = [per_out[j][i] for j in range(4)]
        for i, nm in enumerate(small):
            res[nm] = [per_out[j][-1][i] for j in range(4)]
    return (loss, dx, *[res[nm][j] for j in range(4) for nm in WEIGHTS])


def kernel(x, positions, l0_w_in, rwkv_mix, rwkv_w0, rwkv_w2, rwkv_a0, rwkv_a2, rwkv_g2, rwkv_k_k, rwkv_k_a, rwkv_r_k, rwkv_ln_g, rwkv_ln_b, ssm_conv_w, ssm_conv_b, ssm_dt_bias, ssm_a_log, ssm_d, ssm_norm_g, l0_w_out, l0_ln1_g, l0_ln1_b, ffn0_w_up, ffn0_conv_w, ffn0_conv_b, ffn0_w_down, l0_ln2_g, l0_ln2_b, l1_w_in, mla_q_norm_g, mla_w_uq, mla_kv_norm_g, mla_w_ukv, l1_w_out, l1_ln1_g, l1_ln1_b, ffn1_w_up, ffn1_conv_w, ffn1_conv_b, ffn1_w_down, l1_ln2_g, l1_ln2_b, loss_target, m_l0_w_in, m_rwkv_mix, m_rwkv_w0, m_rwkv_w2, m_rwkv_a0, m_rwkv_a2, m_rwkv_g2, m_rwkv_k_k, m_rwkv_k_a, m_rwkv_r_k, m_rwkv_ln_g, m_rwkv_ln_b, m_ssm_conv_w, m_ssm_conv_b, m_ssm_dt_bias, m_ssm_a_log, m_ssm_d, m_ssm_norm_g, m_l0_w_out, m_l0_ln1_g, m_l0_ln1_b, m_ffn0_w_up, m_ffn0_conv_w, m_ffn0_conv_b, m_ffn0_w_down, m_l0_ln2_g, m_l0_ln2_b, m_l1_w_in, m_mla_q_norm_g, m_mla_w_uq, m_mla_kv_norm_g, m_mla_w_ukv, m_l1_w_out, m_l1_ln1_g, m_l1_ln1_b, m_ffn1_w_up, m_ffn1_conv_w, m_ffn1_conv_b, m_ffn1_w_down, m_l1_ln2_g, m_l1_ln2_b, v_l0_w_in, v_rwkv_mix, v_rwkv_w0, v_rwkv_w2, v_rwkv_a0, v_rwkv_a2, v_rwkv_g2, v_rwkv_k_k, v_rwkv_k_a, v_rwkv_r_k, v_rwkv_ln_g, v_rwkv_ln_b, v_ssm_conv_w, v_ssm_conv_b, v_ssm_dt_bias, v_ssm_a_log, v_ssm_d, v_ssm_norm_g, v_l0_w_out, v_l0_ln1_g, v_l0_ln1_b, v_ffn0_w_up, v_ffn0_conv_w, v_ffn0_conv_b, v_ffn0_w_down, v_l0_ln2_g, v_l0_ln2_b, v_l1_w_in, v_mla_q_norm_g, v_mla_w_uq, v_mla_kv_norm_g, v_mla_w_ukv, v_l1_w_out, v_l1_ln1_g, v_l1_ln1_b, v_ffn1_w_up, v_ffn1_conv_w, v_ffn1_conv_b, v_ffn1_w_down, v_l1_ln2_g, v_l1_ln2_b):
    return _step(dict(locals()))
```
